```python
import jax, jax.numpy as jnp
from jax import lax
import numpy as np

D_MODEL = 1024
BATCH = 8
SEQ = 16384
DEPTH = 1

CHUNK = 64
Q_BLOCK = 128
N_MEM = 256
EPS = 1e-6

GLA_HEADS = 4
GLA_DK = D_MODEL // 16
GLA_DV = D_MODEL // 8
GLA_LOWRANK = 16
GLA_TAU = 16.0
FOX_HEADS = 8
FOX_DH = D_MODEL // 16
MEM_HEADS = 4
MEM_DH = D_MODEL // 8
D_FF = 4 * D_MODEL
N_BRANCH = 3

GLA_K = GLA_HEADS * GLA_DK
GLA_V = GLA_HEADS * GLA_DV
FOX_W = FOX_HEADS * FOX_DH
MEM_W = MEM_HEADS * MEM_DH
PROJ_SIZES = (GLA_K, GLA_K, GLA_V, GLA_V, GLA_LOWRANK, FOX_W, FOX_W, FOX_W, FOX_HEADS, MEM_W, N_BRANCH * D_MODEL)
D_IN = 2 * GLA_K + 2 * GLA_V + GLA_LOWRANK + 3 * FOX_W + FOX_HEADS + MEM_W + N_BRANCH * D_MODEL

kernel_name = "hybrid_gla_fox_memory_gated_block"


def rmsnorm(x, g):
    xf = x.astype(jnp.float32)
    r = lax.rsqrt(jnp.mean(xf * xf, axis=-1, keepdims=True) + EPS)
    return (xf * r).astype(x.dtype) * g


def split_cols(t, sizes):
    offs = np.cumsum(np.array(sizes))[:-1].tolist()
    return jnp.split(t, offs, axis=-1)


def gla_chunk_causal(q, k, v, log_a):
    B, S, H, DK = q.shape
    DV = v.shape[-1]
    N = S // CHUNK
    f32 = jnp.float32
    qc = q.reshape(B, N, CHUNK, H, DK).astype(f32)
    kc = k.reshape(B, N, CHUNK, H, DK).astype(f32)
    vc = v.reshape(B, N, CHUNK, H, DV).astype(f32)
    b = jnp.cumsum(log_a.reshape(B, N, CHUNK, H, DK).astype(f32), axis=2)
    b_last = b[:, :, -1:]
    e_pos = jnp.exp(b)
    e_neg = jnp.exp(-b)
    q_pos = qc * e_pos
    a_causal = jnp.einsum('bnthd,bnshd->bnhts', q_pos, kc * e_neg)
    a_anti = jnp.einsum('bnthd,bnshd->bnhts', qc * e_neg, kc * e_pos)
    t_idx = jnp.arange(CHUNK)
    lower = t_idx[:, None] >= t_idx[None, :]
    attn = jnp.where(lower, a_causal, a_anti)
    o_intra = jnp.einsum('bnhts,bnshv->bnthv', attn, vc)
    chunk_kv = jnp.einsum('bnshd,bnshv->bnhdv', kc * jnp.exp(b_last - b), vc)
    chunk_decay = jnp.exp(b_last[:, :, 0])

    def step(state, inp):
        kv, dec = inp
        return state * dec[..., None] + kv, state

    init = jnp.zeros((B, H, DK, DV), f32)
    _, prev = lax.scan(step, init, (jnp.moveaxis(chunk_kv, 1, 0), jnp.moveaxis(chunk_decay, 1, 0)))
    prev = jnp.moveaxis(prev, 0, 1)
    o_inter = jnp.einsum('bnthd,bnhdv->bnthv', q_pos, prev)
    return (o_intra + o_inter).reshape(B, S, H, DV).astype(v.dtype)


def forgetting_attention(q, k, v, log_f):
    B, S, H, Dh = q.shape
    nb = S // Q_BLOCK
    scale = Dh ** -0.5
    F = jnp.cumsum(log_f.astype(jnp.float32), axis=1).transpose(0, 2, 1)
    kh = k.transpose(0, 2, 1, 3)
    vh = v.transpose(0, 2, 1, 3)
    qb = q.reshape(B, nb, Q_BLOCK, H, Dh).transpose(1, 0, 3, 2, 4)
    Fq = F.reshape(B, H, nb, Q_BLOCK).transpose(2, 0, 1, 3)
    k_pos = jnp.arange(S)

    def block(args):
        qi, Fi, i = args
        s = jnp.einsum('bhqd,bhkd->bhqk', qi, kh).astype(jnp.float32) * scale
        s = s + Fi[..., None] - F[:, :, None, :]
        q_pos = i * Q_BLOCK + jnp.arange(Q_BLOCK)
        mask = k_pos[None, :] <= q_pos[:, None]
        p = jax.nn.softmax(jnp.where(mask, s, -jnp.inf), axis=-1)
        return jnp.einsum('bhqk,bhkd->bhqd', p.astype(vh.dtype), vh)

    out = lax.map(block, (qb, Fq, jnp.arange(nb)))
    return out.transpose(1, 0, 3, 2, 4).reshape(B, S, H, Dh)


def memory_attention(q, mk, mv):
    scale = q.shape[-1] ** -0.5
    s = jnp.einsum('bshd,bmhd->bhsm', q, mk).astype(jnp.float32) * scale
    p = jax.nn.softmax(s, axis=-1)
    return jnp.einsum('bhsm,bmhd->bshd', p.astype(mv.dtype), mv)


def _fwd_setup_inputs(seed: int = 0) -> dict:
    key = jax.random.key(seed)
    ks = jax.random.split(key, 20)
    f32 = jnp.float32

    def nrm(k, shape, fan_in):
        return jax.random.normal(k, shape, f32) * (fan_in ** -0.5)

    def gain(k, shape):
        return 1.0 + 0.02 * jax.random.normal(k, shape, f32)

    L = DEPTH
    return {
        "x": jax.random.normal(ks[0], (BATCH, SEQ, D_MODEL), f32),
        "mem": jax.random.normal(ks[1], (BATCH, N_MEM, D_MODEL), f32),
        "g_mix": gain(ks[2], (L, D_MODEL)),
        "w_in": nrm(ks[3], (L, D_MODEL, D_IN), D_MODEL),
        "w_alpha_up": nrm(ks[4], (L, GLA_LOWRANK, GLA_K), GLA_LOWRANK),
        "b_alpha": 0.02 * jax.random.normal(ks[5], (L, GLA_K), f32),
        "b_forget": jax.random.uniform(ks[6], (L, FOX_HEADS), f32, 1.0, 5.0),
        "g_gla_head": gain(ks[7], (L, GLA_HEADS, GLA_DV)),
        "g_mem": gain(ks[8], (L, D_MODEL)),
        "w_mem_kv": nrm(ks[9], (L, D_MODEL, 2 * MEM_W), D_MODEL),
        "w_gla_o": nrm(ks[10], (L, GLA_V, D_MODEL), GLA_V),
        "w_fox_o": nrm(ks[11], (L, FOX_W, D_MODEL), FOX_W),
        "w_mem_o": nrm(ks[12], (L, MEM_W, D_MODEL), MEM_W),
        "w_out": nrm(ks[13], (L, D_MODEL, D_MODEL), D_MODEL),
        "g_ffn": gain(ks[14], (L, D_MODEL)),
        "w_ff1": nrm(ks[15], (L, D_MODEL, D_FF), D_MODEL),
        "w_ff2": nrm(ks[16], (L, D_FF, D_MODEL), D_FF),
        "g_final": gain(ks[17], (D_MODEL,)),
    }


def _fwd_reference(x, mem, g_mix, w_in, w_alpha_up, b_alpha, b_forget, g_gla_head, g_mem, w_mem_kv,
              w_gla_o, w_fox_o, w_mem_o, w_out, g_ffn, w_ff1, w_ff2, g_final):
    B, S, D = x.shape
    M = mem.shape[1]
    h = x
    for l in range(DEPTH):
        u = rmsnorm(h, g_mix[l])
        proj = u @ w_in[l]
        (gq, gk, gv, gg, ga, fq, fk, fv, ff, mq, gates) = split_cols(proj, PROJ_SIZES)

        log_a = jax.nn.log_sigmoid(ga @ w_alpha_up[l] + b_alpha[l]) / GLA_TAU
        o_gla = gla_chunk_causal(
            gq.reshape(B, S, GLA_HEADS, GLA_DK) * (GLA_DK ** -0.5),
            gk.reshape(B, S, GLA_HEADS, GLA_DK),
            gv.reshape(B, S, GLA_HEADS, GLA_DV),
            log_a.reshape(B, S, GLA_HEADS, GLA_DK))
        o_gla = rmsnorm(o_gla, g_gla_head[l]) * jax.nn.silu(gg.reshape(B, S, GLA_HEADS, GLA_DV))
        y_gla = o_gla.reshape(B, S, GLA_V) @ w_gla_o[l]

        log_f = jax.nn.log_sigmoid(ff + b_forget[l])
        o_fox = forgetting_attention(
            fq.reshape(B, S, FOX_HEADS, FOX_DH),
            fk.reshape(B, S, FOX_HEADS, FOX_DH),
            fv.reshape(B, S, FOX_HEADS, FOX_DH),
            log_f)
        y_fox = o_fox.reshape(B, S, FOX_W) @ w_fox_o[l]

        mkv = rmsnorm(mem, g_mem[l]) @ w_mem_kv[l]
        mk, mv = jnp.split(mkv, 2, axis=-1)
        o_mem = memory_attention(
            mq.reshape(B, S, MEM_HEADS, MEM_DH),
            mk.reshape(B, M, MEM_HEADS, MEM_DH),
            mv.reshape(B, M, MEM_HEADS, MEM_DH))
        y_mem = o_mem.reshape(B, S, MEM_W) @ w_mem_o[l]

        gt = jax.nn.sigmoid(gates.reshape(B, S, N_BRANCH, D))
        merged = gt[:, :, 0] * y_gla + gt[:, :, 1] * y_fox + gt[:, :, 2] * y_mem
        h = h + merged @ w_out[l]

        u2 = rmsnorm(h, g_ffn[l])
        h = h + jnp.square(jax.nn.relu(u2 @ w_ff1[l])) @ w_ff2[l]
    return rmsnorm(h, g_final)


import jax as _jax
import jax.numpy as _jnp

TWIN_FORMAT = 'train_step'
FWD_PARAMS = ['x', 'mem', 'g_mix', 'w_in', 'w_alpha_up', 'b_alpha', 'b_forget', 'g_gla_head', 'g_mem', 'w_mem_kv', 'w_gla_o', 'w_fox_o', 'w_mem_o', 'w_out', 'g_ffn', 'w_ff1', 'w_ff2', 'g_final']
TWIN_WEIGHTS = ['g_mix', 'w_in', 'w_alpha_up', 'b_alpha', 'b_forget', 'g_gla_head', 'g_mem', 'w_mem_kv', 'w_gla_o', 'w_fox_o', 'w_mem_o', 'w_out', 'g_ffn', 'w_ff1', 'w_ff2', 'g_final']
TWIN_DIFF_INPUT = 'x'
TWIN_INPUTS = ['x', 'mem', 'g_mix', 'w_in', 'w_alpha_up', 'b_alpha', 'b_forget', 'g_gla_head', 'g_mem', 'w_mem_kv', 'w_gla_o', 'w_fox_o', 'w_mem_o', 'w_out', 'g_ffn', 'w_ff1', 'w_ff2', 'g_final', 'loss_target', 'm_g_mix', 'm_w_in', 'm_w_alpha_up', 'm_b_alpha', 'm_b_forget', 'm_g_gla_head', 'm_g_mem', 'm_w_mem_kv', 'm_w_gla_o', 'm_w_fox_o', 'm_w_mem_o', 'm_w_out', 'm_g_ffn', 'm_w_ff1', 'm_w_ff2', 'm_g_final', 'v_g_mix', 'v_w_in', 'v_w_alpha_up', 'v_b_alpha', 'v_b_forget', 'v_g_gla_head', 'v_g_mem', 'v_w_mem_kv', 'v_w_gla_o', 'v_w_fox_o', 'v_w_mem_o', 'v_w_out', 'v_g_ffn', 'v_w_ff1', 'v_w_ff2', 'v_g_final']
TWIN_OUTPUTS = ['loss', 'grad_x', 'grad_g_mix', 'grad_w_in', 'grad_w_alpha_up', 'grad_b_alpha', 'grad_b_forget', 'grad_g_gla_head', 'grad_g_mem', 'grad_w_mem_kv', 'grad_w_gla_o', 'grad_w_fox_o', 'grad_w_mem_o', 'grad_w_out', 'grad_g_ffn', 'grad_w_ff1', 'grad_w_ff2', 'grad_g_final', 'delta_g_mix', 'delta_w_in', 'delta_w_alpha_up', 'delta_b_alpha', 'delta_b_forget', 'delta_g_gla_head', 'delta_g_mem', 'delta_w_mem_kv', 'delta_w_gla_o', 'delta_w_fox_o', 'delta_w_mem_o', 'delta_w_out', 'delta_g_ffn', 'delta_w_ff1', 'delta_w_ff2', 'delta_g_final', 'new_m_g_mix', 'new_m_w_in', 'new_m_w_alpha_up', 'new_m_b_alpha', 'new_m_b_forget', 'new_m_g_gla_head', 'new_m_g_mem', 'new_m_w_mem_kv', 'new_m_w_gla_o', 'new_m_w_fox_o', 'new_m_w_mem_o', 'new_m_w_out', 'new_m_g_ffn', 'new_m_w_ff1', 'new_m_w_ff2', 'new_m_g_final', 'new_v_g_mix', 'new_v_w_in', 'new_v_w_alpha_up', 'new_v_b_alpha', 'new_v_b_forget', 'new_v_g_gla_head', 'new_v_g_mem', 'new_v_w_mem_kv', 'new_v_w_gla_o', 'new_v_w_fox_o', 'new_v_w_mem_o', 'new_v_w_out', 'new_v_g_ffn', 'new_v_w_ff1', 'new_v_w_ff2', 'new_v_g_final']
TWIN_LEAF_KINDS = {'loss': 'loss', 'grad_x': 'grad_x', 'grad_g_mix': 'grad_w', 'grad_w_in': 'grad_w', 'grad_w_alpha_up': 'grad_w', 'grad_b_alpha': 'grad_w', 'grad_b_forget': 'grad_w', 'grad_g_gla_head': 'grad_w', 'grad_g_mem': 'grad_w', 'grad_w_mem_kv': 'grad_w', 'grad_w_gla_o': 'grad_w', 'grad_w_fox_o': 'grad_w', 'grad_w_mem_o': 'grad_w', 'grad_w_out': 'grad_w', 'grad_g_ffn': 'grad_w', 'grad_w_ff1': 'grad_w', 'grad_w_ff2': 'grad_w', 'grad_g_final': 'grad_w', 'delta_g_mix': 'delta_w', 'delta_w_in': 'delta_w', 'delta_w_alpha_up': 'delta_w', 'delta_b_alpha': 'delta_w', 'delta_b_forget': 'delta_w', 'delta_g_gla_head': 'delta_w', 'delta_g_mem': 'delta_w', 'delta_w_mem_kv': 'delta_w', 'delta_w_gla_o': 'delta_w', 'delta_w_fox_o': 'delta_w', 'delta_w_mem_o': 'delta_w', 'delta_w_out': 'delta_w', 'delta_g_ffn': 'delta_w', 'delta_w_ff1': 'delta_w', 'delta_w_ff2': 'delta_w', 'delta_g_final': 'delta_w', 'new_m_g_mix': 'new_m', 'new_m_w_in': 'new_m', 'new_m_w_alpha_up': 'new_m', 'new_m_b_alpha': 'new_m', 'new_m_b_forget': 'new_m', 'new_m_g_gla_head': 'new_m', 'new_m_g_mem': 'new_m', 'new_m_w_mem_kv': 'new_m', 'new_m_w_gla_o': 'new_m', 'new_m_w_fox_o': 'new_m', 'new_m_w_mem_o': 'new_m', 'new_m_w_out': 'new_m', 'new_m_g_ffn': 'new_m', 'new_m_w_ff1': 'new_m', 'new_m_w_ff2': 'new_m', 'new_m_g_final': 'new_m', 'new_v_g_mix': 'new_v', 'new_v_w_in': 'new_v', 'new_v_w_alpha_up': 'new_v', 'new_v_b_alpha': 'new_v', 'new_v_b_forget': 'new_v', 'new_v_g_gla_head': 'new_v', 'new_v_g_mem': 'new_v', 'new_v_w_mem_kv': 'new_v', 'new_v_w_gla_o': 'new_v', 'new_v_w_fox_o': 'new_v', 'new_v_w_mem_o': 'new_v', 'new_v_w_out': 'new_v', 'new_v_g_ffn': 'new_v', 'new_v_w_ff1': 'new_v', 'new_v_w_ff2': 'new_v', 'new_v_g_final': 'new_v'}


def _forward(args):
    return _fwd_reference(*[args[k] for k in FWD_PARAMS])


def _output_shape():
    def fwd():
        inp = _fwd_setup_inputs(0)
        return _fwd_reference(*[inp[k] for k in FWD_PARAMS])
    out = _jax.eval_shape(fwd)
    return out.shape, out.dtype

N_MICROBATCH = 1
ADAM_LR = 0.001
ADAM_B1 = 0.9
ADAM_B2 = 0.999
ADAM_EPS = 1e-08
ADAM_WD = 0.01
ADAM_STEP = 10
PER_EXAMPLE_BATCH_AXIS = {'x': 0, 'mem': 0, 'loss_target': 0}
SHARED_INPUTS = []
_WEIGHT_DTYPES = {'g_mix': _jnp.float32, 'w_in': _jnp.float32, 'w_alpha_up': _jnp.float32, 'b_alpha': _jnp.float32, 'b_forget': _jnp.float32, 'g_gla_head': _jnp.float32, 'g_mem': _jnp.float32, 'w_mem_kv': _jnp.float32, 'w_gla_o': _jnp.float32, 'w_fox_o': _jnp.float32, 'w_mem_o': _jnp.float32, 'w_out': _jnp.float32, 'g_ffn': _jnp.float32, 'w_ff1': _jnp.float32, 'w_ff2': _jnp.float32, 'g_final': _jnp.float32}
MOMENT_SCALE = {'g_mix': 2.771444e-01, 'w_in': 1.033335e-01, 'w_alpha_up': 2.536296e-02, 'b_alpha': 9.021908e-02, 'b_forget': 4.135331e-01, 'g_gla_head': 1.698079e-01, 'g_mem': 3.151247e-02, 'w_mem_kv': 3.015700e-02, 'w_gla_o': 1.174762e-01, 'w_fox_o': 6.305828e-02, 'w_mem_o': 2.178274e-02, 'w_out': 1.347879e-01, 'g_ffn': 3.115485e-01, 'w_ff1': 1.554567e-01, 'w_ff2': 3.436044e-01, 'g_final': 1.290114e+02}


def _to_microbatches(a, axis):
    t = _jnp.moveaxis(a, axis, 0)
    t = t.reshape((N_MICROBATCH, t.shape[0] // N_MICROBATCH) + t.shape[1:])
    return _jnp.moveaxis(t, 1, axis + 1)


def setup_inputs(seed: int = 0) -> dict:
    inp = _fwd_setup_inputs(seed)
    key = _jax.random.fold_in(_jax.random.key(seed), 7919)
    shape, _ = _output_shape()
    out = dict(inp)
    out["loss_target"] = _jax.random.normal(_jax.random.fold_in(key, 0), shape, _jnp.float32)
    for i, name in enumerate(TWIN_WEIGHTS):
        w = inp[name].astype(_jnp.float32)
        if MOMENT_SCALE is None:
            s = _jnp.sqrt(_jnp.mean(_jnp.square(w)) + 1e-30)
        else:
            s = MOMENT_SCALE[name]
        km, kv = _jax.random.split(_jax.random.fold_in(key, i + 1))
        out[name] = w
        out["m_" + name] = s * _jax.random.normal(km, w.shape, _jnp.float32)
        out["v_" + name] = (s * s) * _jax.random.uniform(kv, w.shape, _jnp.float32, 0.5, 1.5)
    if N_MICROBATCH > 1:
        for name, axis in PER_EXAMPLE_BATCH_AXIS.items():
            out[name] = _to_microbatches(out[name], axis)
    return {'x': out['x'], 'mem': out['mem'], 'g_mix': out['g_mix'], 'w_in': out['w_in'], 'w_alpha_up': out['w_alpha_up'], 'b_alpha': out['b_alpha'], 'b_forget': out['b_forget'], 'g_gla_head': out['g_gla_head'], 'g_mem': out['g_mem'], 'w_mem_kv': out['w_mem_kv'], 'w_gla_o': out['w_gla_o'], 'w_fox_o': out['w_fox_o'], 'w_mem_o': out['w_mem_o'], 'w_out': out['w_out'], 'g_ffn': out['g_ffn'], 'w_ff1': out['w_ff1'], 'w_ff2': out['w_ff2'], 'g_final': out['g_final'], 'loss_target': out['loss_target'], 'm_g_mix': out['m_g_mix'], 'm_w_in': out['m_w_in'], 'm_w_alpha_up': out['m_w_alpha_up'], 'm_b_alpha': out['m_b_alpha'], 'm_b_forget': out['m_b_forget'], 'm_g_gla_head': out['m_g_gla_head'], 'm_g_mem': out['m_g_mem'], 'm_w_mem_kv': out['m_w_mem_kv'], 'm_w_gla_o': out['m_w_gla_o'], 'm_w_fox_o': out['m_w_fox_o'], 'm_w_mem_o': out['m_w_mem_o'], 'm_w_out': out['m_w_out'], 'm_g_ffn': out['m_g_ffn'], 'm_w_ff1': out['m_w_ff1'], 'm_w_ff2': out['m_w_ff2'], 'm_g_final': out['m_g_final'], 'v_g_mix': out['v_g_mix'], 'v_w_in': out['v_w_in'], 'v_w_alpha_up': out['v_w_alpha_up'], 'v_b_alpha': out['v_b_alpha'], 'v_b_forget': out['v_b_forget'], 'v_g_gla_head': out['v_g_gla_head'], 'v_g_mem': out['v_g_mem'], 'v_w_mem_kv': out['v_w_mem_kv'], 'v_w_gla_o': out['v_w_gla_o'], 'v_w_fox_o': out['v_w_fox_o'], 'v_w_mem_o': out['v_w_mem_o'], 'v_w_out': out['v_w_out'], 'v_g_ffn': out['v_g_ffn'], 'v_w_ff1': out['v_w_ff1'], 'v_w_ff2': out['v_w_ff2'], 'v_g_final': out['v_g_final']}


def _loss(weights, diff, rest, loss_target):
    with _jax.named_scope("forward"):
        args = {**rest, TWIN_DIFF_INPUT: diff, **{k: w.astype(_WEIGHT_DTYPES[k]) for k, w in weights.items()}}
        y = _forward(args)
    with _jax.named_scope("loss_head"):
        err = _jnp.square(y.astype(_jnp.float32) - loss_target)
        return 0.5 * _jnp.sum(_jnp.mean(err, axis=-1)) if err.ndim else 0.5 * err


def _adamw(w, g, m, v):
    m = ADAM_B1 * m + (1.0 - ADAM_B1) * g
    v = ADAM_B2 * v + (1.0 - ADAM_B2) * _jnp.square(g)
    m_hat = m / (1.0 - ADAM_B1 ** ADAM_STEP)
    v_hat = v / (1.0 - ADAM_B2 ** ADAM_STEP)
    delta = -ADAM_LR * (m_hat / (_jnp.sqrt(v_hat) + ADAM_EPS) + ADAM_WD * w)
    return delta, m, v


def reference(x, mem, g_mix, w_in, w_alpha_up, b_alpha, b_forget, g_gla_head, g_mem, w_mem_kv, w_gla_o, w_fox_o, w_mem_o, w_out, g_ffn, w_ff1, w_ff2, g_final, loss_target, m_g_mix, m_w_in, m_w_alpha_up, m_b_alpha, m_b_forget, m_g_gla_head, m_g_mem, m_w_mem_kv, m_w_gla_o, m_w_fox_o, m_w_mem_o, m_w_out, m_g_ffn, m_w_ff1, m_w_ff2, m_g_final, v_g_mix, v_w_in, v_w_alpha_up, v_b_alpha, v_b_forget, v_g_gla_head, v_g_mem, v_w_mem_kv, v_w_gla_o, v_w_fox_o, v_w_mem_o, v_w_out, v_g_ffn, v_w_ff1, v_w_ff2, v_g_final):
    given = dict(x=x, mem=mem, g_mix=g_mix, w_in=w_in, w_alpha_up=w_alpha_up, b_alpha=b_alpha, b_forget=b_forget, g_gla_head=g_gla_head, g_mem=g_mem, w_mem_kv=w_mem_kv, w_gla_o=w_gla_o, w_fox_o=w_fox_o, w_mem_o=w_mem_o, w_out=w_out, g_ffn=g_ffn, w_ff1=w_ff1, w_ff2=w_ff2, g_final=g_final, loss_target=loss_target, m_g_mix=m_g_mix, m_w_in=m_w_in, m_w_alpha_up=m_w_alpha_up, m_b_alpha=m_b_alpha, m_b_forget=m_b_forget, m_g_gla_head=m_g_gla_head, m_g_mem=m_g_mem, m_w_mem_kv=m_w_mem_kv, m_w_gla_o=m_w_gla_o, m_w_fox_o=m_w_fox_o, m_w_mem_o=m_w_mem_o, m_w_out=m_w_out, m_g_ffn=m_g_ffn, m_w_ff1=m_w_ff1, m_w_ff2=m_w_ff2, m_g_final=m_g_final, v_g_mix=v_g_mix, v_w_in=v_w_in, v_w_alpha_up=v_w_alpha_up, v_b_alpha=v_b_alpha, v_b_forget=v_b_forget, v_g_gla_head=v_g_gla_head, v_g_mem=v_g_mem, v_w_mem_kv=v_w_mem_kv, v_w_gla_o=v_w_gla_o, v_w_fox_o=v_w_fox_o, v_w_mem_o=v_w_mem_o, v_w_out=v_w_out, v_g_ffn=v_g_ffn, v_w_ff1=v_w_ff1, v_w_ff2=v_w_ff2, v_g_final=v_g_final)
    weights = {n: given[n] for n in TWIN_WEIGHTS}
    shared = {n: given[n] for n in SHARED_INPUTS}
    per_example = {n: given[n] for n in ['x', 'mem']}
    grad_fn = _jax.value_and_grad(_loss, argnums=(0, 1))

    def one_microbatch(ex, loss_target):
        ex = dict(ex)
        diff = ex.pop(TWIN_DIFF_INPUT)
        return grad_fn(weights, diff, {**shared, **ex}, loss_target)

    if N_MICROBATCH == 1:
        loss, (grad_w, grad_x) = one_microbatch(per_example, given["loss_target"])
    else:
        def body(carry, xs):
            loss_sum, grad_sum = carry
            l_k, (gw_k, gx_k) = one_microbatch(xs[0], xs[1])
            with _jax.named_scope("update"):
                return (loss_sum + l_k, _jax.tree.map(_jnp.add, grad_sum, gw_k)), gx_k

        init = (_jnp.zeros((), _jnp.float32), _jax.tree.map(_jnp.zeros_like, weights))
        (loss, grad_w), grad_x = _jax.lax.scan(body, init, (per_example, given["loss_target"]))
    with _jax.named_scope("update"):
        delta_w, new_m, new_v = {}, {}, {}
        for n in TWIN_WEIGHTS:
            delta_w[n], new_m[n], new_v[n] = _adamw(weights[n], grad_w[n], given["m_" + n], given["v_" + n])
    return (loss, grad_x, *[grad_w[n] for n in TWIN_WEIGHTS], *[delta_w[n] for n in TWIN_WEIGHTS],
            *[new_m[n] for n in TWIN_WEIGHTS], *[new_v[n] for n in TWIN_WEIGHTS])
```

```python
import jax
import jax.numpy as jnp
from jax import lax
from jax.experimental import pallas as pl
from jax.experimental.pallas import tpu as pltpu

F32, BF16 = jnp.float32, jnp.bfloat16
HIGHEST = lax.Precision.HIGHEST
MESH = pl.DeviceIdType.MESH

N_DEV = 8
D = 1024
EPS = 1e-6
CHUNK = 64
N_MEM = 256
GLA_H, GLA_DK, GLA_DV = 4, 64, 128
GLA_K, GLA_V, GLA_R = 256, 512, 16
FOX_H, FOX_DH, FOX_W = 8, 64, 512
MEM_H, MEM_DH, MEM_W = 4, 128, 512
D_FF = 4096
D_IN = 6680
FOX_SCALE = 0.125
GLA_SCALE = 0.125
MEM_SCALE = MEM_DH ** -0.5
GLA_TAU_INV = 1.0 / 16.0
NEG = -1e30

O_GQ, O_GK, O_GV, O_GG, O_GA, O_FQ, O_FK, O_FV, O_FF, O_MQ, O_GT = 0, 256, 512, 1024, 1536, 1552, 2064, 2576, 3088, 3096, 3608
A_FQ, A_FK, A_FV, A_MQ, A_W = 1024, 2048, 3072, 3584, 4096
S_W = 640
G_W = 3072
P_W = 8192
FF_LANE = 16
AUG = 64

ADAM_LR, ADAM_B1, ADAM_B2, ADAM_EPS, ADAM_WD, ADAM_STEP = 0.001, 0.9, 0.999, 1e-08, 0.01, 10
V7X_VMEM_LIMIT = 48 * 1024 * 1024


def _params(*sem):
    return pltpu.CompilerParams(dimension_semantics=sem, vmem_limit_bytes=V7X_VMEM_LIMIT)


def _nt(a, b):
    return lax.dot_general(a, b, (((1,), (1,)), ((), ())), preferred_element_type=F32)


def _tn(a, b):
    return lax.dot_general(a, b, (((0,), (0,)), ((), ())), preferred_element_type=F32)


def _nn(a, b):
    return jnp.dot(a, b, preferred_element_type=F32)


def _log_sigmoid(z):
    return jnp.minimum(z, 0.0) - jnp.log(1.0 + jnp.exp(-jnp.abs(z)))


def _sum8(x):
    return x.reshape(x.shape[0] // 8, 8, x.shape[1]).sum(axis=0)


def _rms(xv):
    r = lax.rsqrt(jnp.mean(xv * xv, axis=-1, keepdims=True) + EPS)
    return r, xv * r


def _rms_bwd(du, g, r, xh):
    w = du * g
    return r * (w - xh * jnp.mean(w * xh, axis=-1, keepdims=True))


def _tile(n, pref):
    t = min(n, pref)
    assert n % t == 0, (n, t)
    return t


def _proj(x, g, w, out_dtype, name, emit_u):
    S = x.shape[0]
    N = w.shape[1]
    tm, tn = _tile(S, 512), _tile(N, 1024) if N % 1024 == 0 else N

    def body(x_ref, g_ref, w_ref, o_ref, *rest):
        u_s = rest[-1]

        @pl.when(pl.program_id(1) == 0)
        def _():
            r, xh = _rms(x_ref[...])
            u_s[...] = (xh * g_ref[...]).astype(BF16)
            if emit_u:
                rest[0][...] = u_s[...]

        o_ref[...] = _nn(u_s[...], w_ref[...]).astype(out_dtype)

    out_shape = [jax.ShapeDtypeStruct((S, N), out_dtype)]
    out_specs = [pl.BlockSpec((tm, tn), lambda i, j: (i, j))]
    if emit_u:
        out_shape.append(jax.ShapeDtypeStruct((S, D), BF16))
        out_specs.append(pl.BlockSpec((tm, D), lambda i, j: (i, 0)))
    return pl.pallas_call(
        body, name=name, grid=(S // tm, N // tn),
        in_specs=[pl.BlockSpec((tm, D), lambda i, j: (i, 0)), pl.BlockSpec((1, D), lambda i, j: (0, 0)),
                  pl.BlockSpec((D, tn), lambda i, j: (0, j))],
        out_specs=out_specs, out_shape=out_shape,
        scratch_shapes=[pltpu.VMEM((tm, D), BF16)],
        compiler_params=_params("arbitrary", "arbitrary"),
    )(x, g, w)


def _wgrad(a, b, name):
    S, Ka = a.shape
    N = b.shape[1]
    tka, tn, ts = _tile(Ka, 1024), _tile(N, 1024), _tile(S, 1024)
    n_s = S // ts

    def body(a_ref, b_ref, o_ref, acc):
        s = pl.program_id(2)

        @pl.when(s == 0)
        def _():
            acc[...] = jnp.zeros_like(acc)

        acc[...] += _tn(a_ref[...].astype(BF16), b_ref[...].astype(BF16))

        @pl.when(s == n_s - 1)
        def _():
            o_ref[...] = acc[...]

    return pl.pallas_call(
        body, name=name, grid=(Ka // tka, N // tn, n_s),
        in_specs=[pl.BlockSpec((ts, tka), lambda i, j, s: (s, i)), pl.BlockSpec((ts, tn), lambda i, j, s: (s, j))],
        out_specs=pl.BlockSpec((tka, tn), lambda i, j, s: (i, j)),
        out_shape=jax.ShapeDtypeStruct((Ka, N), F32),
        scratch_shapes=[pltpu.VMEM((tka, tn), F32)],
        compiler_params=_params("arbitrary", "arbitrary", "arbitrary"),
    )(a, b)


def _nt_rmsbwd(a, w, xin, g, dres, name, emit_bf16):
    S, K = a.shape
    tm, tk = _tile(S, 512), _tile(K, 2048)
    n_k = K // tk

    def body(a_ref, w_ref, x_ref, g_ref, r_ref, o_ref, *rest):
        dg_ref, acc = rest[-2], rest[-1]
        i, k = pl.program_id(0), pl.program_id(1)

        @pl.when(k == 0)
        def _():
            acc[...] = jnp.zeros_like(acc)

        acc[...] += _nt(a_ref[...], w_ref[...])

        @pl.when(k == n_k - 1)
        def _():
            du = acc[...]
            r, xh = _rms(x_ref[...])
            out = r_ref[...] + _rms_bwd(du, g_ref[...], r, xh)
            o_ref[...] = out
            if emit_bf16:
                rest[0][...] = out.astype(BF16)
            part = _sum8(du * xh)

            @pl.when(i == 0)
            def _():
                dg_ref[...] = part

            @pl.when(i > 0)
            def _():
                dg_ref[...] += part

    row = pl.BlockSpec((tm, D), lambda i, k: (i, 0))
    out_shape = [jax.ShapeDtypeStruct((S, D), F32)]
    out_specs = [row]
    if emit_bf16:
        out_shape.append(jax.ShapeDtypeStruct((S, D), BF16))
        out_specs.append(row)
    out_shape.append(jax.ShapeDtypeStruct((8, D), F32))
    out_specs.append(pl.BlockSpec((8, D), lambda i, k: (0, 0)))
    return pl.pallas_call(
        body, name=name, grid=(S // tm, n_k),
        in_specs=[pl.BlockSpec((tm, tk), lambda i, k: (i, k)), pl.BlockSpec((D, tk), lambda i, k: (0, k)),
                  row, pl.BlockSpec((1, D), lambda i, k: (0, 0)), row],
        out_specs=out_specs, out_shape=out_shape,
        scratch_shapes=[pltpu.VMEM((tm, D), F32)],
        compiler_params=_params("arbitrary", "arbitrary"),
    )(a, w, xin, g, dres)


def _merge(og, ofox, omem, wg, wf, wm, pg):
    S = og.shape[0]
    tm = _tile(S, 256)

    def body(og_ref, of_ref, om_ref, wg_ref, wf_ref, wm_ref, pg_ref, y_ref, mg_ref):
        tot = None
        for i, (o_ref, w_ref) in enumerate(((og_ref, wg_ref), (of_ref, wf_ref), (om_ref, wm_ref))):
            y = _nn(o_ref[...].astype(BF16), w_ref[...])
            y_ref[i] = y
            t = jax.nn.sigmoid(pg_ref[:, D * i:D * (i + 1)]) * y
            tot = t if tot is None else tot + t
        mg_ref[...] = tot.astype(BF16)

    o_spec = pl.BlockSpec((tm, 512), lambda i: (i, 0))
    w_spec = pl.BlockSpec((512, D), lambda i: (0, 0))
    return pl.pallas_call(
        body, name="merge", grid=(S // tm,),
        in_specs=[o_spec, o_spec, o_spec, w_spec, w_spec, w_spec, pl.BlockSpec((tm, G_W), lambda i: (i, 0))],
        out_specs=[pl.BlockSpec((3, tm, D), lambda i: (0, i, 0)), pl.BlockSpec((tm, D), lambda i: (i, 0))],
        out_shape=[jax.ShapeDtypeStruct((3, S, D), F32), jax.ShapeDtypeStruct((S, D), BF16)],
        compiler_params=_params("arbitrary"),
    )(og, ofox, omem, wg, wf, wm, pg)


def _out_proj(mg, w_out, x, g_ffn):
    S = x.shape[0]
    tm = _tile(S, 512)

    def body(mg_ref, w_ref, x_ref, g_ref, h_ref, u_ref):
        h = x_ref[...] + _nn(mg_ref[...], w_ref[...])
        h_ref[...] = h
        r, xh = _rms(h)
        u_ref[...] = (xh * g_ref[...]).astype(BF16)

    row = pl.BlockSpec((tm, D), lambda i: (i, 0))
    return pl.pallas_call(
        body, name="out_proj", grid=(S // tm,),
        in_specs=[row, pl.BlockSpec((D, D), lambda i: (0, 0)), row, pl.BlockSpec((1, D), lambda i: (0, 0))],
        out_specs=[row, row],
        out_shape=[jax.ShapeDtypeStruct((S, D), F32), jax.ShapeDtypeStruct((S, D), BF16)],
        compiler_params=_params("arbitrary"),
    )(mg, w_out, x, g_ffn)


def _ff1(u2, w1):
    S = u2.shape[0]
    tm, tn = _tile(S, 1024), 1024

    def body(u_ref, w_ref, a_ref, act_ref):
        a = _nn(u_ref[...], w_ref[...])
        a_ref[...] = a.astype(BF16)
        act_ref[...] = jnp.square(jnp.maximum(a, 0.0)).astype(BF16)

    blk = pl.BlockSpec((tm, tn), lambda i, j: (i, j))
    return pl.pallas_call(
        body, name="ff1", grid=(S // tm, D_FF // tn),
        in_specs=[pl.BlockSpec((tm, D), lambda i, j: (i, 0)), pl.BlockSpec((D, tn), lambda i, j: (0, j))],
        out_specs=[blk, blk],
        out_shape=[jax.ShapeDtypeStruct((S, D_FF), BF16), jax.ShapeDtypeStruct((S, D_FF), BF16)],
        compiler_params=_params("arbitrary", "arbitrary"),
    )(u2, w1)


def _ff2_loss(act, w2, h1, g_final, target):
    S = act.shape[0]
    tm, tk = _tile(S, 512), 2048
    n_k = D_FF // tk

    def body(a_ref, w_ref, h_ref, g_ref, t_ref, d_ref, db_ref, ls_ref, dg_ref, acc):
        i, k = pl.program_id(0), pl.program_id(1)

        @pl.when(k == 0)
        def _():
            acc[...] = jnp.zeros_like(acc)

        acc[...] += _nn(a_ref[...], w_ref[...])

        @pl.when(k == n_k - 1)
        def _():
            h2 = h_ref[...] + acc[...]
            r, xh = _rms(h2)
            gf = g_ref[...]
            err = xh * gf - t_ref[...]
            dy = err * (1.0 / D)
            dh = _rms_bwd(dy, gf, r, xh)
            d_ref[...] = dh
            db_ref[...] = dh.astype(BF16)
            lp, gp = _sum8(err * err), _sum8(dy * xh)

            @pl.when(i == 0)
            def _():
                ls_ref[...] = lp
                dg_ref[...] = gp

            @pl.when(i > 0)
            def _():
                ls_ref[...] += lp
                dg_ref[...] += gp

    row = pl.BlockSpec((tm, D), lambda i, k: (i, 0))
    part = pl.BlockSpec((8, D), lambda i, k: (0, 0))
    return pl.pallas_call(
        body, name="ff2_loss", grid=(S // tm, n_k),
        in_specs=[pl.BlockSpec((tm, tk), lambda i, k: (i, k)), pl.BlockSpec((tk, D), lambda i, k: (k, 0)),
                  row, pl.BlockSpec((1, D), lambda i, k: (0, 0)), row],
        out_specs=[row, row, part, part],
        out_shape=[jax.ShapeDtypeStruct((S, D), F32), jax.ShapeDtypeStruct((S, D), BF16),
                   jax.ShapeDtypeStruct((8, D), F32), jax.ShapeDtypeStruct((8, D), F32)],
        scratch_shapes=[pltpu.VMEM((tm, D), F32)],
        compiler_params=_params("arbitrary", "arbitrary"),
    )(act, w2, h1, g_final, target)


def _dact(dh2b, w2, a):
    S = a.shape[0]
    tm, tn = _tile(S, 1024), 1024

    def body(d_ref, w_ref, a_ref, o_ref):
        da = _nt(d_ref[...], w_ref[...])
        o_ref[...] = (da * (2.0 * jnp.maximum(a_ref[...].astype(F32), 0.0))).astype(BF16)

    blk = pl.BlockSpec((tm, tn), lambda i, j: (i, j))
    return pl.pallas_call(
        body, name="dact", grid=(S // tm, D_FF // tn),
        in_specs=[pl.BlockSpec((tm, D), lambda i, j: (i, 0)), pl.BlockSpec((tn, D), lambda i, j: (j, 0)), blk],
        out_specs=blk, out_shape=jax.ShapeDtypeStruct((S, D_FF), BF16),
        compiler_params=_params("arbitrary", "arbitrary"),
    )(dh2b, w2, a)


def _dmerge(dh1b, w_out, pg, y3):
    S = dh1b.shape[0]
    tm = _tile(S, 256)

    def body(d_ref, w_ref, pg_ref, y_ref, dy_ref, dg_ref):
        dm = _nt(d_ref[...], w_ref[...])
        for i in range(3):
            gt = jax.nn.sigmoid(pg_ref[:, D * i:D * (i + 1)])
            dy_ref[i] = (dm * gt).astype(BF16)
            dg_ref[:, D * i:D * (i + 1)] = (dm * y_ref[i] * (gt * (1.0 - gt))).astype(BF16)

    return pl.pallas_call(
        body, name="dmerge", grid=(S // tm,),
        in_specs=[pl.BlockSpec((tm, D), lambda i: (i, 0)), pl.BlockSpec((D, D), lambda i: (0, 0)),
                  pl.BlockSpec((tm, G_W), lambda i: (i, 0)), pl.BlockSpec((3, tm, D), lambda i: (0, i, 0))],
        out_specs=[pl.BlockSpec((3, tm, D), lambda i: (0, i, 0)), pl.BlockSpec((tm, G_W), lambda i: (i, 0))],
        out_shape=[jax.ShapeDtypeStruct((3, S, D), BF16), jax.ShapeDtypeStruct((S, G_W), BF16)],
        compiler_params=_params("arbitrary"),
    )(dh1b, w_out, pg, y3)


def _dout3(dy3, wg, wf, wm):
    S = dy3.shape[1]
    tm = _tile(S, 512)

    def body(dy_ref, wg_ref, wf_ref, wm_ref, o_ref):
        for i, w_ref in enumerate((wg_ref, wf_ref, wm_ref)):
            o_ref[i] = _nt(dy_ref[i], w_ref[...])

    w_spec = pl.BlockSpec((512, D), lambda i: (0, 0))
    return pl.pallas_call(
        body, name="dout3", grid=(S // tm,),
        in_specs=[pl.BlockSpec((3, tm, D), lambda i: (0, i, 0)), w_spec, w_spec, w_spec],
        out_specs=pl.BlockSpec((3, tm, 512), lambda i: (0, i, 0)),
        out_shape=jax.ShapeDtypeStruct((3, S, 512), F32),
        compiler_params=_params("arbitrary"),
    )(dy3, wg, wf, wm)


def _gla_block_terms(gq_ref, gk_ref, ps_ref, wau_ref, ba_ref, tb):
    gaff = ps_ref[:, 512:640]
    z = _nn(gaff.astype(BF16), wau_ref[...]) + ba_ref[...]
    la = _log_sigmoid(z) * GLA_TAU_INV
    rr = lax.broadcasted_iota(jnp.int32, (tb, tb), 0)
    cc = lax.broadcasted_iota(jnp.int32, (tb, tb), 1)
    same = jnp.right_shift(rr, 6) == jnp.right_shift(cc, 6)
    tri = jnp.where(same & (cc <= rr), 1.0, 0.0).astype(F32)
    ones = jnp.where(same, 1.0, 0.0).astype(F32)
    b = jnp.dot(tri, la, preferred_element_type=F32, precision=HIGHEST)
    bl = jnp.dot(ones, la, preferred_element_type=F32, precision=HIGHEST)
    e_pos, e_neg, e_last, dec = jnp.exp(b), jnp.exp(-b), jnp.exp(bl - b), jnp.exp(bl)
    q = gq_ref[...].astype(F32) * GLA_SCALE
    k = gk_ref[...].astype(F32)
    return dict(gaff=gaff, z=z, same=same, rr=rr, cc=cc, ones=ones, e_pos=e_pos, e_neg=e_neg, e_last=e_last, dec=dec,
                qp=q * e_pos, qn=q * e_neg, kn=k * e_neg, kp=k * e_pos, kd=k * e_last)


def _head_masked(x, store):
    lane = lax.broadcasted_iota(jnp.int32, x.shape, 1)
    for h in range(GLA_H):
        store[h] = jnp.where(jnp.right_shift(lane, 6) == h, x, 0.0).astype(BF16)


def _gla_fwd(pa, ps, wau, ba, gh):
    S = pa.shape[0]
    tb = _tile(S, 512)
    n_c = tb // CHUNK

    def body(gq_ref, gk_ref, gv_ref, ps_ref, wau_ref, ba_ref, gh_ref, o_ref, og_ref, sp_ref,
             qpm, qnm, kdm, kn_s, kp_s, dec_s, state):
        @pl.when(pl.program_id(0) == 0)
        def _():
            state[...] = jnp.zeros_like(state)

        t = _gla_block_terms(gq_ref, gk_ref, ps_ref, wau_ref, ba_ref, tb)
        _head_masked(t["qp"], qpm)
        _head_masked(t["qn"], qnm)
        _head_masked(t["kd"], kdm)
        kn_s[...] = t["kn"].astype(BF16)
        kp_s[...] = t["kp"].astype(BF16)
        dec_s[...] = t["dec"]
        lower = lax.broadcasted_iota(jnp.int32, (CHUNK, CHUNK), 0) >= lax.broadcasted_iota(jnp.int32, (CHUNK, CHUNK), 1)

        def chunk(c, carry):
            r0 = pl.multiple_of(c * CHUNK, CHUNK)
            rows = pl.ds(r0, CHUNK)
            sp = state[...]
            sp_ref[c] = sp
            spb = sp.astype(BF16)
            knc, kpc = kn_s[rows, :], kp_s[rows, :]
            new = sp * dec_s[pl.ds(r0, 1), :]
            for h in range(GLA_H):
                cols = slice(GLA_DV * h, GLA_DV * (h + 1))
                qpc, qnc = qpm[h, rows, :], qnm[h, rows, :]
                attn = jnp.where(lower, _nt(qpc, knc), _nt(qnc, kpc)).astype(BF16)
                vh = gv_ref[rows, cols]
                o_ref[rows, cols] = _nn(attn, vh) + _nt(qpc, spb)
                new = new + _tn(vh, kdm[h, rows, :])
            state[...] = new
            return carry

        lax.fori_loop(0, n_c, chunk, 0)
        for h in range(GLA_H):
            cols = slice(GLA_DV * h, GLA_DV * (h + 1))
            r, xh = _rms(o_ref[:, cols])
            gg = ps_ref[:, cols]
            og_ref[:, cols] = ((xh * gh_ref[:, cols]) * (gg * jax.nn.sigmoid(gg))).astype(BF16)

    return pl.pallas_call(
        body, name="gla_fwd", grid=(S // tb,),
        in_specs=[pl.BlockSpec((tb, GLA_K), lambda i: (i, 0)), pl.BlockSpec((tb, GLA_K), lambda i: (i, 1)),
                  pl.BlockSpec((tb, GLA_V), lambda i: (i, 1)), pl.BlockSpec((tb, S_W), lambda i: (i, 0)),
                  pl.BlockSpec((128, GLA_K), lambda i: (0, 0)), pl.BlockSpec((1, GLA_K), lambda i: (0, 0)),
                  pl.BlockSpec((1, GLA_V), lambda i: (0, 0))],
        out_specs=[pl.BlockSpec((tb, GLA_V), lambda i: (i, 0)), pl.BlockSpec((tb, GLA_V), lambda i: (i, 0)),
                   pl.BlockSpec((n_c, GLA_DV, GLA_K), lambda i: (i, 0, 0))],
        out_shape=[jax.ShapeDtypeStruct((S, GLA_V), F32), jax.ShapeDtypeStruct((S, GLA_V), BF16),
                   jax.ShapeDtypeStruct((S // CHUNK, GLA_DV, GLA_K), F32)],
        scratch_shapes=[pltpu.VMEM((GLA_H, tb, GLA_K), BF16), pltpu.VMEM((GLA_H, tb, GLA_K), BF16),
                        pltpu.VMEM((GLA_H, tb, GLA_K), BF16), pltpu.VMEM((tb, GLA_K), BF16), pltpu.VMEM((tb, GLA_K), BF16),
                        pltpu.VMEM((tb, GLA_K), F32), pltpu.VMEM((GLA_DV, GLA_K), F32)],
        compiler_params=_params("arbitrary"),
    )(pa, pa, pa, ps, wau, ba, gh)


def _gla_bwd(pa, ps, wau, ba, gh, o_gla, d_og, sprev, dgaff_fox):
    S = pa.shape[0]
    tb = _tile(S, 512)
    n_c = tb // CHUNK
    n_b = S // tb

    def body(gq_ref, gk_ref, gv_ref, ps_ref, wau_ref, ba_ref, gh_ref, o_ref, dog_ref, sp_ref, dfx_ref,
             dgq_ref, dgk_ref, dgv_ref, dgg_ref, dgaff_ref, dwau_ref, dba_ref, dgh_ref,
             qpm, qnm, kdm, kn_s, kp_s, dec_s, do_s, dqp_s, dqn_s, dkn_s, dkp_s, dkd_s, ddec_s, dstate):
        first = pl.program_id(0) == 0

        @pl.when(first)
        def _():
            dstate[...] = jnp.zeros_like(dstate)

        t = _gla_block_terms(gq_ref, gk_ref, ps_ref, wau_ref, ba_ref, tb)
        _head_masked(t["qp"], qpm)
        _head_masked(t["qn"], qnm)
        _head_masked(t["kd"], kdm)
        kn_s[...] = t["kn"].astype(BF16)
        kp_s[...] = t["kp"].astype(BF16)
        dec_s[...] = t["dec"]

        dgh_parts = []
        for h in range(GLA_H):
            cols = slice(GLA_DV * h, GLA_DV * (h + 1))
            r, xh = _rms(o_ref[:, cols])
            g = gh_ref[:, cols]
            gg = ps_ref[:, cols]
            sg = jax.nn.sigmoid(gg)
            d_out = dog_ref[:, cols]
            dgg_ref[:, cols] = (d_out * (xh * g) * (sg * (1.0 + gg * (1.0 - sg)))).astype(BF16)
            d_on = d_out * (gg * sg)
            dgh_parts.append(_sum8(d_on * xh))
            do_s[:, cols] = _rms_bwd(d_on, g, r, xh).astype(BF16)
        dgh_part = jnp.concatenate(dgh_parts, axis=1)

        lower = lax.broadcasted_iota(jnp.int32, (CHUNK, CHUNK), 0) >= lax.broadcasted_iota(jnp.int32, (CHUNK, CHUNK), 1)
        lane = lax.broadcasted_iota(jnp.int32, (CHUNK, GLA_K), 1)

        def chunk(j, carry):
            c = n_c - 1 - j
            r0 = pl.multiple_of(c * CHUNK, CHUNK)
            rows = pl.ds(r0, CHUNK)
            ds_next = dstate[...]
            dsb = ds_next.astype(BF16)
            sp = sp_ref[c]
            spb = sp.astype(BF16)
            knc, kpc = kn_s[rows, :], kp_s[rows, :]
            dec_row = dec_s[pl.ds(r0, 1), :]
            ddec_s[rows, :] = jnp.broadcast_to(jnp.sum(ds_next * sp, axis=0, keepdims=True), (CHUNK, GLA_K))
            new = ds_next * dec_row
            dqp = jnp.zeros((CHUNK, GLA_K), F32)
            dqn, dkn, dkp, dkd = dqp, dqp, dqp, dqp
            for h in range(GLA_H):
                cols = slice(GLA_DV * h, GLA_DV * (h + 1))
                mine = jnp.right_shift(lane, 6) == h
                qpc, qnc, kdc = qpm[h, rows, :], qnm[h, rows, :], kdm[h, rows, :]
                vh = gv_ref[rows, cols]
                doh = do_s[rows, cols]
                attn = jnp.where(lower, _nt(qpc, knc), _nt(qnc, kpc)).astype(BF16)
                da = _nt(doh, vh)
                dac = jnp.where(lower, da, 0.0).astype(BF16)
                daa = jnp.where(lower, 0.0, da).astype(BF16)
                dqp = dqp + jnp.where(mine, _nn(dac, knc) + _nn(doh, spb), 0.0)
                dqn = dqn + jnp.where(mine, _nn(daa, kpc), 0.0)
                dkn = dkn + _tn(dac, qpc)
                dkp = dkp + _tn(daa, qnc)
                dkd = dkd + jnp.where(mine, _nn(vh, dsb), 0.0)
                dgv_ref[rows, cols] = (_tn(attn, doh) + _nt(kdc, dsb)).astype(BF16)
                new = new + _tn(doh, qpc)
            dqp_s[rows, :] = dqp
            dqn_s[rows, :] = dqn
            dkn_s[rows, :] = dkn
            dkp_s[rows, :] = dkp
            dkd_s[rows, :] = dkd
            dstate[...] = new
            return carry

        lax.fori_loop(0, n_c, chunk, 0)

        dqp, dqn, dkn, dkp, dkd = dqp_s[...], dqn_s[...], dkn_s[...], dkp_s[...], dkd_s[...]
        dgq_ref[...] = ((dqp * t["e_pos"] + dqn * t["e_neg"]) * GLA_SCALE).astype(BF16)
        dgk_ref[...] = (dkn * t["e_neg"] + dkp * t["e_pos"] + dkd * t["e_last"]).astype(BF16)
        kd_term = dkd * t["kd"]
        db = dqp * t["qp"] - dqn * t["qn"] - dkn * t["kn"] + dkp * t["kp"] - kd_term
        upper = jnp.where(t["same"] & (t["cc"] >= t["rr"]), 1.0, 0.0).astype(F32)
        dla = (jnp.dot(upper, db, preferred_element_type=F32, precision=HIGHEST)
               + jnp.dot(t["ones"], kd_term, preferred_element_type=F32, precision=HIGHEST)
               + ddec_s[...] * t["dec"])
        dz = dla * GLA_TAU_INV * jax.nn.sigmoid(-t["z"])
        dzb = dz.astype(BF16)
        dgaff_ref[...] = (_nt(dzb, wau_ref[...]) + dfx_ref[...]).astype(BF16)
        dwau_part = _tn(t["gaff"].astype(BF16), dzb)
        dba_part = _sum8(dz)

        @pl.when(first)
        def _():
            dwau_ref[...] = dwau_part
            dba_ref[...] = dba_part
            dgh_ref[...] = dgh_part

        @pl.when(jnp.logical_not(first))
        def _():
            dwau_ref[...] += dwau_part
            dba_ref[...] += dba_part
            dgh_ref[...] += dgh_part

    rev = lambda i: (n_b - 1 - i, 0)
    f32k = pltpu.VMEM((tb, GLA_K), F32)
    bf4 = pltpu.VMEM((GLA_H, tb, GLA_K), BF16)
    return pl.pallas_call(
        body, name="gla_bwd", grid=(n_b,),
        in_specs=[pl.BlockSpec((tb, GLA_K), rev), pl.BlockSpec((tb, GLA_K), lambda i: (n_b - 1 - i, 1)),
                  pl.BlockSpec((tb, GLA_V), lambda i: (n_b - 1 - i, 1)), pl.BlockSpec((tb, S_W), rev),
                  pl.BlockSpec((128, GLA_K), lambda i: (0, 0)), pl.BlockSpec((1, GLA_K), lambda i: (0, 0)),
                  pl.BlockSpec((1, GLA_V), lambda i: (0, 0)), pl.BlockSpec((tb, GLA_V), rev), pl.BlockSpec((tb, GLA_V), rev),
                  pl.BlockSpec((n_c, GLA_DV, GLA_K), lambda i: (n_b - 1 - i, 0, 0)), pl.BlockSpec((tb, 128), rev)],
        out_specs=[pl.BlockSpec((tb, GLA_K), rev), pl.BlockSpec((tb, GLA_K), rev), pl.BlockSpec((tb, GLA_V), rev),
                   pl.BlockSpec((tb, GLA_V), rev), pl.BlockSpec((tb, 128), rev),
                   pl.BlockSpec((128, GLA_K), lambda i: (0, 0)), pl.BlockSpec((8, GLA_K), lambda i: (0, 0)),
                   pl.BlockSpec((8, GLA_V), lambda i: (0, 0))],
        out_shape=[jax.ShapeDtypeStruct((S, GLA_K), BF16), jax.ShapeDtypeStruct((S, GLA_K), BF16),
                   jax.ShapeDtypeStruct((S, GLA_V), BF16), jax.ShapeDtypeStruct((S, GLA_V), BF16),
                   jax.ShapeDtypeStruct((S, 128), BF16), jax.ShapeDtypeStruct((128, GLA_K), F32),
                   jax.ShapeDtypeStruct((8, GLA_K), F32), jax.ShapeDtypeStruct((8, GLA_V), F32)],
        scratch_shapes=[bf4, bf4, bf4, pltpu.VMEM((tb, GLA_K), BF16), pltpu.VMEM((tb, GLA_K), BF16), f32k,
                        pltpu.VMEM((tb, GLA_V), BF16), f32k, f32k, f32k, f32k, f32k, f32k, pltpu.VMEM((GLA_DV, GLA_K), F32)],
        compiler_params=_params("arbitrary"),
    )(pa, pa, pa, ps, wau, ba, gh, o_gla, d_og, sprev, dgaff_fox)


def _split3(x):
    x1 = x.astype(BF16).astype(F32)
    x2 = (x - x1).astype(BF16).astype(F32)
    x3 = (x - x1 - x2).astype(BF16).astype(F32)
    return x1, x2, x3


def _fox_prep(pa, ps, bfg):
    S = pa.shape[0]
    tm = _tile(S, 512)

    def body(ps_ref, b_ref, fq_ref, fk_ref, q_ref, k_ref, carry):
        @pl.when(pl.program_id(0) == 0)
        def _():
            carry[...] = jnp.zeros_like(carry)

        lf = _log_sigmoid(ps_ref[...] + b_ref[...])
        rr = lax.broadcasted_iota(jnp.int32, (tm, tm), 0)
        cc = lax.broadcasted_iota(jnp.int32, (tm, tm), 1)
        tri = jnp.where(cc <= rr, 1.0, 0.0).astype(F32)
        f = jnp.dot(tri, lf, preferred_element_type=F32, precision=HIGHEST) + carry[0:1, :]
        carry[...] = jnp.broadcast_to(f[tm - 1:tm, :], carry.shape)
        f1, f2, f3 = _split3(f)
        lane = lax.broadcasted_iota(jnp.int32, (tm, 128), 1)
        for h in range(FOX_H):
            cols = slice(128 * h, 128 * (h + 1))
            c = FF_LANE + h
            a1, a2, a3 = f1[:, c:c + 1], f2[:, c:c + 1], f3[:, c:c + 1]
            q = fq_ref[:, cols].astype(F32) * FOX_SCALE
            k = fk_ref[:, cols].astype(F32)
            for n, a in enumerate((a1, a2, a3)):
                q = jnp.where(lane == AUG + n, a, q)
                k = jnp.where(lane == AUG + 3 + n, -a, k)
            q = jnp.where((lane >= AUG + 3) & (lane < AUG + 6), 1.0, q)
            k = jnp.where((lane >= AUG) & (lane < AUG + 3), 1.0, k)
            q_ref[:, cols] = q.astype(BF16)
            k_ref[:, cols] = k.astype(BF16)

    wide = lambda j: pl.BlockSpec((tm, 1024), lambda i: (i, j))
    return pl.pallas_call(
        body, name="fox_prep", grid=(S // tm,),
        in_specs=[pl.BlockSpec((tm, 128), lambda i: (i, 4)), pl.BlockSpec((1, 128), lambda i: (0, 0)), wide(1), wide(2)],
        out_specs=[wide(0), wide(0)],
        out_shape=[jax.ShapeDtypeStruct((S, 1024), BF16), jax.ShapeDtypeStruct((S, 1024), BF16)],
        scratch_shapes=[pltpu.VMEM((8, 128), F32)],
        compiler_params=_params("arbitrary"),
    )(ps, bfg, pa, pa)


def _row_select(rows):
    rr = lax.broadcasted_iota(jnp.int32, (8, 128), 0)
    cc = lax.broadcasted_iota(jnp.int32, (8, 128), 1)
    return jnp.where((rr == cc) & (rr < rows), 1.0, 0.0).astype(F32)


def _fox_fwd(qa, ka, pa):
    S = qa.shape[0]
    tq = _tile(S, 512)
    tk = tq

    def body(q_ref, k_ref, v_ref, o_ref, lse_ref):
        i = pl.program_id(1)
        rr = lax.broadcasted_iota(jnp.int32, (tq, tk), 0)
        cc = lax.broadcasted_iota(jnp.int32, (tq, tk), 1)
        lane = lax.broadcasted_iota(jnp.int32, (tq, 128), 1)
        lmat = jnp.zeros((tq, 128), F32)
        for hh in range(2):
            cols = slice(128 * hh, 128 * (hh + 1))
            qh = q_ref[:, cols]

            def blk(j, carry, masked):
                m, l, acc = carry
                ks = pl.ds(pl.multiple_of(j * tk, tk), tk)
                s = _nt(qh, k_ref[ks, cols])
                if masked:
                    s = jnp.where(cc <= rr, s, NEG)
                mn = jnp.maximum(m, jnp.max(s, axis=-1, keepdims=True))
                p = jnp.exp(s - mn)
                al = jnp.exp(m - mn)
                l = al * l + jnp.sum(p, axis=-1, keepdims=True)
                acc = al * acc + _nn(p.astype(BF16), v_ref[ks, :])
                return mn, l, acc

            init = (jnp.full((tq, 1), NEG, F32), jnp.zeros((tq, 1), F32), jnp.zeros((tq, 128), F32))
            carry = lax.fori_loop(0, i, lambda j, c: blk(j, c, False), init)
            m, l, acc = blk(i, carry, True)
            o = acc / l
            half = slice(FOX_DH * hh, FOX_DH * (hh + 1))
            o_ref[:, half] = o[:, half]
            lmat = jnp.where(lane == hh, m + jnp.log(l), lmat)
        lse_ref[0] = lax.dot_general(_row_select(2), lmat, (((1,), (1,)), ((), ())), preferred_element_type=F32,
                                     precision=HIGHEST)

    return pl.pallas_call(
        body, name="fox_fwd", grid=(FOX_H // 2, S // tq),
        in_specs=[pl.BlockSpec((tq, 256), lambda p, i: (i, p)), pl.BlockSpec((S, 256), lambda p, i: (0, p)),
                  pl.BlockSpec((S, 128), lambda p, i: (0, A_FV // 128 + p))],
        out_specs=[pl.BlockSpec((tq, 128), lambda p, i: (i, p)), pl.BlockSpec((1, 8, tq), lambda p, i: (p, 0, i))],
        out_shape=[jax.ShapeDtypeStruct((S, FOX_W), F32), jax.ShapeDtypeStruct((FOX_H // 2, 8, S), F32)],
        compiler_params=_params("arbitrary", "arbitrary"),
    )(qa, ka, pa)


def _fox_delta(d_o, o):
    S = o.shape[0]
    tm = _tile(S, 512)

    def body(d_ref, o_ref, db_ref, dl_ref):
        d = d_ref[...]
        db_ref[...] = d.astype(BF16)
        prod = d * o_ref[...]
        rr = lax.broadcasted_iota(jnp.int32, (8, 128), 0)
        cc = lax.broadcasted_iota(jnp.int32, (8, 128), 1)
        ind = jnp.where(jnp.right_shift(cc, 6) == rr, 1.0, 0.0).astype(F32)
        for p in range(FOX_H // 2):
            dl_ref[p] = lax.dot_general(ind, prod[:, 128 * p:128 * (p + 1)], (((1,), (1,)), ((), ())),
                                        preferred_element_type=F32, precision=HIGHEST)

    row = pl.BlockSpec((tm, FOX_W), lambda i: (i, 0))
    return pl.pallas_call(
        body, name="fox_delta", grid=(S // tm,),
        in_specs=[row, row],
        out_specs=[row, pl.BlockSpec((FOX_H // 2, 8, tm), lambda i: (0, 0, i))],
        out_shape=[jax.ShapeDtypeStruct((S, FOX_W), BF16), jax.ShapeDtypeStruct((FOX_H // 2, 8, S), F32)],
        compiler_params=_params("arbitrary"),
    )(d_o, o)


def _fox_bwd(qa, ka, pa, dob, lse, delta):
    S = qa.shape[0]
    tk = _tile(S, 512)
    tq = tk
    n_q = S // tq

    def body(q_ref, k_ref, v_ref, do_ref, lse_ref, dl_ref, dq_ref, dk_ref, dv_ref):
        h, jb = pl.program_id(0), pl.program_id(1)
        hh = h % 2

        @pl.when(jb == 0)
        def _():
            dq_ref[...] = jnp.zeros_like(dq_ref)

        lane = lax.broadcasted_iota(jnp.int32, (tk, 128), 1)
        vm = jnp.where(jnp.right_shift(lane, 6) == hh, v_ref[...], jnp.zeros((), BF16))
        kb = k_ref[...]
        rr = lax.broadcasted_iota(jnp.int32, (tk, tq), 0)
        cc = lax.broadcasted_iota(jnp.int32, (tk, tq), 1)

        def blk(ib, carry, masked):
            dk, dv = carry
            qs = pl.ds(pl.multiple_of(ib * tq, tq), tq)
            qb, dob_ = q_ref[qs, :], do_ref[qs, :]
            p = jnp.exp(_nt(kb, qb) - lse_ref[0, pl.ds(hh, 1), qs])
            if masked:
                p = jnp.where(cc >= rr, p, 0.0)
            ds = (p * (_nt(vm, dob_) - dl_ref[0, pl.ds(hh, 1), qs])).astype(BF16)
            dq_ref[qs, :] += _tn(ds, kb)
            return dk + _nn(ds, qb), dv + _nn(p.astype(BF16), dob_)

        zero = jnp.zeros((tk, 128), F32)
        carry = blk(jb, (zero, zero), True)
        dk, dv = lax.fori_loop(jb + 1, n_q, lambda ib, c: blk(ib, c, False), carry)
        dk_ref[...] = dk
        dv_ref[...] = dv

    head = pl.BlockSpec((tk, 128), lambda h, j: (j, h))
    rows = pl.BlockSpec((1, 8, S), lambda h, j: (h // 2, 0, 0))
    return pl.pallas_call(
        body, name="fox_bwd", grid=(FOX_H, S // tk),
        in_specs=[pl.BlockSpec((S, 128), lambda h, j: (0, h)), head,
                  pl.BlockSpec((tk, 128), lambda h, j: (j, A_FV // 128 + h // 2)),
                  pl.BlockSpec((S, 128), lambda h, j: (0, h // 2)), rows, rows],
        out_specs=[pl.BlockSpec((S, 128), lambda h, j: (0, h)), head, head],
        out_shape=[jax.ShapeDtypeStruct((S, 1024), F32)] * 3,
        compiler_params=_params("arbitrary", "arbitrary"),
    )(qa, ka, pa, dob, lse, delta)


def _fox_post(dq, dk, dv, ps, bfg):
    S = dq.shape[0]
    tm = _tile(S, 512)
    n_b = S // tm

    def body(dq_ref, dk_ref, dv_ref, ps_ref, b_ref, fq_ref, fk_ref, fv_ref, dff_ref, dbf_ref, carry):
        first = pl.program_id(0) == 0

        @pl.when(first)
        def _():
            carry[...] = jnp.zeros_like(carry)

        lane = lax.broadcasted_iota(jnp.int32, (tm, 128), 1)
        low = lane < FOX_DH
        g = jnp.zeros((tm, 128), F32)
        for h in range(FOX_H):
            cols = slice(128 * h, 128 * (h + 1))
            kblk, qblk = dk_ref[:, cols], dq_ref[:, cols]
            fq_ref[:, cols] = jnp.where(low, qblk * FOX_SCALE, 0.0).astype(BF16)
            fk_ref[:, cols] = jnp.where(low, kblk, 0.0).astype(BF16)
            g = jnp.where(lane == FF_LANE + h, kblk[:, AUG + 3:AUG + 4] - qblk[:, AUG:AUG + 1], g)
        for p in range(FOX_H // 2):
            fv_ref[:, 128 * p:128 * (p + 1)] = jnp.where(low, dv_ref[:, 256 * p:256 * p + 128],
                                                          dv_ref[:, 256 * p + 128:256 * p + 256]).astype(BF16)
        rr = lax.broadcasted_iota(jnp.int32, (tm, tm), 0)
        cc = lax.broadcasted_iota(jnp.int32, (tm, tm), 1)
        upper = jnp.where(cc >= rr, 1.0, 0.0).astype(F32)
        dlf = jnp.dot(upper, -g, preferred_element_type=F32, precision=HIGHEST) + carry[0:1, :]
        carry[...] = jnp.broadcast_to(dlf[0:1, :], carry.shape)
        valid = (lane >= FF_LANE) & (lane < FF_LANE + FOX_H)
        dff = jnp.where(valid, dlf * jax.nn.sigmoid(-(ps_ref[...] + b_ref[...])), 0.0)
        dff_ref[...] = dff
        part = _sum8(dff)

        @pl.when(first)
        def _():
            dbf_ref[...] = part

        @pl.when(jnp.logical_not(first))
        def _():
            dbf_ref[...] += part

    rev = lambda i: (n_b - 1 - i, 0)
    wide = pl.BlockSpec((tm, 1024), rev)
    return pl.pallas_call(
        body, name="fox_post", grid=(n_b,),
        in_specs=[wide, wide, wide, pl.BlockSpec((tm, 128), lambda i: (n_b - 1 - i, 4)), pl.BlockSpec((1, 128), lambda i: (0, 0))],
        out_specs=[wide, wide, pl.BlockSpec((tm, FOX_W), rev), pl.BlockSpec((tm, 128), rev), pl.BlockSpec((8, 128), lambda i: (0, 0))],
        out_shape=[jax.ShapeDtypeStruct((S, 1024), BF16), jax.ShapeDtypeStruct((S, 1024), BF16),
                   jax.ShapeDtypeStruct((S, FOX_W), BF16), jax.ShapeDtypeStruct((S, 128), F32),
                   jax.ShapeDtypeStruct((8, 128), F32)],
        scratch_shapes=[pltpu.VMEM((8, 128), F32)],
        compiler_params=_params("arbitrary"),
    )(dq, dk, dv, ps, bfg)


def _mem_prep(mem, g_mem, wkv):
    def body(m_ref, g_ref, w_ref, mn_ref, kv_ref):
        r, xh = _rms(m_ref[...])
        mn = (xh * g_ref[...]).astype(BF16)
        mn_ref[...] = mn
        kv_ref[...] = _nn(mn, w_ref[...]).astype(BF16)

    return pl.pallas_call(
        body, name="mem_prep",
        out_shape=[jax.ShapeDtypeStruct((N_MEM, D), BF16), jax.ShapeDtypeStruct((N_MEM, 2 * MEM_W), BF16)],
        compiler_params=pltpu.CompilerParams(vmem_limit_bytes=V7X_VMEM_LIMIT),
    )(mem, g_mem, wkv)


def _mem_softmax(qh, kh):
    s = _nt(qh, kh) * MEM_SCALE
    e = jnp.exp(s - jnp.max(s, axis=-1, keepdims=True))
    return e / jnp.sum(e, axis=-1, keepdims=True)


def _mem_fwd(pa, mkv):
    S = pa.shape[0]
    tm = _tile(S, 512)

    def body(q_ref, kv_ref, o_ref):
        for h in range(MEM_H):
            cols = slice(MEM_DH * h, MEM_DH * (h + 1))
            p = _mem_softmax(q_ref[:, cols], kv_ref[:, cols])
            o_ref[:, cols] = _nn(p.astype(BF16), kv_ref[:, MEM_W + MEM_DH * h:MEM_W + MEM_DH * (h + 1)])

    return pl.pallas_call(
        body, name="mem_fwd", grid=(S // tm,),
        in_specs=[pl.BlockSpec((tm, MEM_W), lambda i: (i, A_MQ // MEM_W)), pl.BlockSpec((N_MEM, 2 * MEM_W), lambda i: (0, 0))],
        out_specs=pl.BlockSpec((tm, MEM_W), lambda i: (i, 0)),
        out_shape=jax.ShapeDtypeStruct((S, MEM_W), F32),
        compiler_params=_params("arbitrary"),
    )(pa, mkv)


def _mem_bwd(pa, mkv, d_o):
    S = pa.shape[0]
    tm = _tile(S, 512)

    def body(q_ref, kv_ref, do_ref, dq_ref, dkv_ref):
        first = pl.program_id(0) == 0
        parts = []
        for h in range(MEM_H):
            cols = slice(MEM_DH * h, MEM_DH * (h + 1))
            vcols = slice(MEM_W + MEM_DH * h, MEM_W + MEM_DH * (h + 1))
            qh, kh = q_ref[:, cols], kv_ref[:, cols]
            p = _mem_softmax(qh, kh)
            dob = do_ref[:, cols].astype(BF16)
            dp = _nt(dob, kv_ref[:, vcols])
            ds = (p * (dp - jnp.sum(p * dp, axis=-1, keepdims=True)) * MEM_SCALE).astype(BF16)
            dq_ref[:, cols] = _nn(ds, kh).astype(BF16)
            parts.append((cols, _tn(ds, qh)))
            parts.append((vcols, _tn(p.astype(BF16), dob)))

        @pl.when(first)
        def _():
            for sl, v in parts:
                dkv_ref[:, sl] = v

        @pl.when(jnp.logical_not(first))
        def _():
            for sl, v in parts:
                dkv_ref[:, sl] += v

    return pl.pallas_call(
        body, name="mem_bwd", grid=(S // tm,),
        in_specs=[pl.BlockSpec((tm, MEM_W), lambda i: (i, A_MQ // MEM_W)), pl.BlockSpec((N_MEM, 2 * MEM_W), lambda i: (0, 0)),
                  pl.BlockSpec((tm, MEM_W), lambda i: (i, 0))],
        out_specs=[pl.BlockSpec((tm, MEM_W), lambda i: (i, 0)), pl.BlockSpec((N_MEM, 2 * MEM_W), lambda i: (0, 0))],
        out_shape=[jax.ShapeDtypeStruct((S, MEM_W), BF16), jax.ShapeDtypeStruct((N_MEM, 2 * MEM_W), F32)],
        compiler_params=_params("arbitrary"),
    )(pa, mkv, d_o)


def _mem_prep_bwd(mem, g_mem, mn, wkv, dkv):
    def body(m_ref, g_ref, mn_ref, w_ref, d_ref, dw_ref, dg_ref):
        db = d_ref[...].astype(BF16)
        dw_ref[...] = _tn(mn_ref[...], db)
        r, xh = _rms(m_ref[...])
        dg_ref[...] = _sum8(_nt(db, w_ref[...]) * xh)

    return pl.pallas_call(
        body, name="mem_prep_bwd",
        out_shape=[jax.ShapeDtypeStruct((D, 2 * MEM_W), F32), jax.ShapeDtypeStruct((8, D), F32)],
        compiler_params=pltpu.CompilerParams(vmem_limit_bytes=V7X_VMEM_LIMIT),
    )(mem, g_mem, mn, wkv, dkv)


def _rearrange_w_in(w):
    def heads128(cols):
        blk = w[:, cols:cols + FOX_W].reshape(D, FOX_H, FOX_DH)
        return jnp.pad(blk, ((0, 0), (0, 0), (0, 128 - FOX_DH))).reshape(D, FOX_H * 128)

    wa = jnp.concatenate([w[:, O_GQ:O_GG], heads128(O_FQ), heads128(O_FK), w[:, O_FV:O_FF], w[:, O_MQ:O_GT]], axis=1)
    wg = w[:, O_GT:]
    ws = jnp.concatenate([w[:, O_GG:O_GA], w[:, O_GA:O_FQ], w[:, O_FF:O_MQ], jnp.zeros((D, 128 - GLA_R - FOX_H), w.dtype)], axis=1)
    wp = jnp.concatenate([wa, wg, ws, jnp.zeros((D, P_W - A_W - G_W - S_W), w.dtype)], axis=1)
    return wa, wg, ws, wp


def _restore_w_in_grad(dwp):
    def unheads(off):
        return dwp[:, off:off + FOX_H * 128].reshape(D, FOX_H, 128)[:, :, :FOX_DH].reshape(D, FOX_W)

    s0 = A_W + G_W
    return jnp.concatenate([
        dwp[:, 0:1024], dwp[:, s0:s0 + 512], dwp[:, s0 + 512:s0 + 512 + GLA_R], unheads(A_FQ), unheads(A_FK),
        dwp[:, A_FV:A_MQ], dwp[:, s0 + 512 + GLA_R:s0 + 512 + GLA_R + FOX_H], dwp[:, A_MQ:A_W], dwp[:, A_W:A_W + G_W]], axis=1)


def _local_step(x, mem, target, p):
    S = x.shape[0]
    wa, wg, ws, wp = _rearrange_w_in(p["w_in"])
    wau = jnp.pad(p["w_alpha_up"], ((0, 128 - GLA_R), (0, 0)))
    bfg = jnp.pad(p["b_forget"], ((0, 0), (FF_LANE, 128 - FF_LANE - FOX_H)))
    gh = p["g_gla_head"].reshape(1, GLA_V)

    pa, u = _proj(x, p["g_mix"], wa, BF16, "proj_a", True)
    (pg,) = _proj(x, p["g_mix"], wg, F32, "proj_g", False)
    (ps,) = _proj(x, p["g_mix"], ws, F32, "proj_s", False)
    o_gla, og, sprev = _gla_fwd(pa, ps, wau, p["b_alpha"], gh)
    qa, ka = _fox_prep(pa, ps, bfg)
    o_fox, lse = _fox_fwd(qa, ka, pa)
    mn, mkv = _mem_prep(mem, p["g_mem"], p["w_mem_kv"])
    o_mem = _mem_fwd(pa, mkv)
    y3, mg = _merge(og, o_fox, o_mem, p["w_gla_o"], p["w_fox_o"], p["w_mem_o"], pg)
    h1, u2 = _out_proj(mg, p["w_out"], x, p["g_ffn"])
    a, act = _ff1(u2, p["w_ff1"])
    dh2, dh2b, loss8, dg_final = _ff2_loss(act, p["w_ff2"], h1, p["g_final"].reshape(1, D), target)

    d_a = _dact(dh2b, p["w_ff2"], a)
    dw_ff2 = _wgrad(act, dh2b, "wgrad_ff2")
    dh1, dh1b, dg_ffn = _nt_rmsbwd(d_a, p["w_ff1"], h1, p["g_ffn"], dh2, "dffn", True)
    dw_ff1 = _wgrad(u2, d_a, "wgrad_ff1")
    dy3, d_gates = _dmerge(dh1b, p["w_out"], pg, y3)
    dw_out = _wgrad(mg, dh1b, "wgrad_out")
    do3 = _dout3(dy3, p["w_gla_o"], p["w_fox_o"], p["w_mem_o"])
    dw_gla_o = _wgrad(og, dy3[0], "wgrad_gla_o")
    dw_fox_o = _wgrad(o_fox, dy3[1], "wgrad_fox_o")
    dw_mem_o = _wgrad(o_mem, dy3[2], "wgrad_mem_o")
    d_mq, d_mkv = _mem_bwd(pa, mkv, do3[2])
    dw_mem_kv, dg_mem = _mem_prep_bwd(mem, p["g_mem"], mn, p["w_mem_kv"], d_mkv)
    dob, delta = _fox_delta(do3[1], o_fox)
    dq, dk, dv = _fox_bwd(qa, ka, pa, dob, lse, delta)
    d_fq, d_fk, d_fv, dgaff_fox, db_forget = _fox_post(dq, dk, dv, ps, bfg)
    d_gq, d_gk, d_gv, d_gg, d_gaff, dw_au, db_alpha, dg_gla = _gla_bwd(pa, ps, wau, p["b_alpha"], gh, o_gla, do3[0], sprev, dgaff_fox)
    d_proj = jnp.concatenate([d_gq, d_gk, d_gv, d_fq, d_fk, d_fv, d_mq, d_gates, d_gg, d_gaff,
                              jnp.zeros((S, P_W - A_W - G_W - S_W), BF16)], axis=1)
    dx, dg_mix = _nt_rmsbwd(d_proj, wp, x, p["g_mix"], dh1, "dmix", False)
    dw_in = _restore_w_in_grad(_wgrad(u, d_proj, "wgrad_in"))

    big = dict(w_in=dw_in, w_mem_kv=dw_mem_kv, w_gla_o=dw_gla_o, w_fox_o=dw_fox_o, w_mem_o=dw_mem_o, w_out=dw_out,
               w_ff1=dw_ff1, w_ff2=dw_ff2)
    small = dict(g_mix=dg_mix, g_mem=dg_mem, g_ffn=dg_ffn, g_final=dg_final, b_alpha=db_alpha, g_gla_head=dg_gla,
                 b_forget=db_forget, w_alpha_up=dw_au, loss=loss8)
    return dx, big, small


BIG = (("w_in", 1), ("w_mem_kv", 0), ("w_gla_o", 1), ("w_fox_o", 1), ("w_mem_o", 1), ("w_out", 0), ("w_ff1", 1), ("w_ff2", 0))


def _peer(d):
    me = lax.axis_index("x") * 4 + lax.axis_index("y") * 2 + lax.axis_index("c")
    t = (me + d) % N_DEV
    return (t // 4, (t // 2) % 2, t % 2), me


def _exchange(blocks, name, mine):
    n = len(blocks)

    def body(*refs):
        ins, outs = refs[:n], refs[n:2 * n]
        send, recv, loc = refs[2 * n:]
        copies = []
        for k in range(n):
            _, me = _peer(0)
            own = pltpu.make_async_copy(ins[k] if mine else ins[k].at[me], outs[k].at[me], loc.at[k])
            own.start()
            copies.append(own)
        remote = []
        for d in range(1, N_DEV):
            to, me = _peer(d)
            for k in range(n):
                cp = pltpu.make_async_remote_copy(
                    src_ref=ins[k] if mine else ins[k].at[(me + d) % N_DEV], dst_ref=outs[k].at[me],
                    send_sem=send.at[k, d - 1], recv_sem=recv.at[k, d - 1], device_id=to, device_id_type=MESH)
                cp.start()
                remote.append(cp)
        for cp in remote:
            cp.wait_send()
        for cp in remote:
            cp.wait_recv()
        for cp in copies:
            cp.wait()

    any_spec = pl.BlockSpec(memory_space=pl.ANY)
    out_shape = [jax.ShapeDtypeStruct(((N_DEV,) + b.shape) if mine else b.shape, b.dtype) for b in blocks]
    return pl.pallas_call(
        body, name=name, in_specs=[any_spec] * n, out_specs=[any_spec] * n, out_shape=out_shape,
        scratch_shapes=[pltpu.SemaphoreType.DMA((n, N_DEV - 1)), pltpu.SemaphoreType.DMA((n, N_DEV - 1)),
                        pltpu.SemaphoreType.DMA((n,))],
    )(*blocks)


def _adamw_math(g, w, m, v):
    m2 = ADAM_B1 * m + (1.0 - ADAM_B1) * g
    v2 = ADAM_B2 * v + (1.0 - ADAM_B2) * jnp.square(g)
    m_hat = m2 / (1.0 - ADAM_B1 ** ADAM_STEP)
    v_hat = v2 / (1.0 - ADAM_B2 ** ADAM_STEP)
    delta = -ADAM_LR * (m_hat / (jnp.sqrt(v_hat) + ADAM_EPS) + ADAM_WD * w)
    return delta, m2, v2


def _adamw_sum(parts, w, m, v, name):
    R, C = w.shape
    tr = _tile(R, 128)

    def body(p_ref, w_ref, m_ref, v_ref, g_ref, d_ref, m2_ref, v2_ref):
        g = p_ref[0]
        for j in range(1, p_ref.shape[0]):
            g = g + p_ref[j]
        g_ref[...] = g
        d_ref[...], m2_ref[...], v2_ref[...] = _adamw_math(g, w_ref[...], m_ref[...], v_ref[...])

    blk = pl.BlockSpec((tr, C), lambda i: (i, 0))
    return pl.pallas_call(
        body, name=name, grid=(R // tr,),
        in_specs=[pl.BlockSpec((parts.shape[0], tr, C), lambda i: (0, i, 0)), blk, blk, blk],
        out_specs=[blk] * 4, out_shape=[jax.ShapeDtypeStruct((R, C), F32)] * 4,
        compiler_params=_params("arbitrary"),
    )(parts, w, m, v)


SMALL_ROWS = 24


def _pack_small(d):
    mixed = jnp.concatenate([d["b_alpha"].reshape(1, GLA_K), d["g_gla_head"].reshape(1, GLA_V),
                             jnp.pad(d["b_forget"].reshape(1, FOX_H), ((0, 0), (FF_LANE, 128 - FF_LANE - FOX_H))),
                             jnp.zeros((1, 128), F32)], axis=1)
    rows = [d["g_mix"].reshape(1, D), d["g_mem"].reshape(1, D), d["g_ffn"].reshape(1, D), d["g_final"].reshape(1, D), mixed,
            jnp.zeros((3, D), F32), jnp.pad(d["w_alpha_up"].reshape(GLA_R, GLA_K), ((0, 0), (0, D - GLA_K)))]
    return jnp.concatenate(rows, axis=0)


def _unpack_small(t):
    return dict(g_mix=t[0:1], g_mem=t[1:2], g_ffn=t[2:3], g_final=t[3], b_alpha=t[4:5, 0:GLA_K],
                g_gla_head=t[4:5, GLA_K:GLA_K + GLA_V].reshape(1, GLA_H, GLA_DV),
                b_forget=t[4:5, 768 + FF_LANE:768 + FF_LANE + FOX_H], w_alpha_up=t[8:24, 0:GLA_K].reshape(1, GLA_R, GLA_K))


def _small_allreduce(small, w, m, v):
    def body(gm, gme, gf, gfi, ba, gg, bf, wau, ls, w_ref, m_ref, v_ref, g_ref, d_ref, m2_ref, v2_ref, l_ref,
             buf, send, recv):
        _, me = _peer(0)
        buf[me] = jnp.zeros((SMALL_ROWS, D), F32)
        for r, ref in enumerate((gm, gme, gf, gfi)):
            buf[me, r:r + 1, :] = jnp.sum(ref[...], axis=0, keepdims=True)
        buf[me, 4:5, 0:GLA_K] = jnp.sum(ba[...], axis=0, keepdims=True)
        buf[me, 4:5, GLA_K:GLA_K + GLA_V] = jnp.sum(gg[...], axis=0, keepdims=True)
        buf[me, 4:5, 768:896] = jnp.sum(bf[...], axis=0, keepdims=True)
        lrow = jnp.sum(ls[...], axis=0, keepdims=True)
        lsum = lrow[:, 0:128]
        for c in range(1, D // 128):
            lsum = lsum + lrow[:, 128 * c:128 * (c + 1)]
        buf[me, 4:5, 896:1024] = lsum
        buf[me, 8:24, 0:GLA_K] = wau[0:GLA_R, :]
        remote = []
        for d in range(1, N_DEV):
            to, me = _peer(d)
            cp = pltpu.make_async_remote_copy(src_ref=buf.at[me], dst_ref=buf.at[me], send_sem=send.at[d - 1],
                                              recv_sem=recv.at[d - 1], device_id=to, device_id_type=MESH)
            cp.start()
            remote.append(cp)
        for cp in remote:
            cp.wait_send()
        for cp in remote:
            cp.wait_recv()
        g = buf[0]
        for j in range(1, N_DEV):
            g = g + buf[j]
        g_ref[...] = g
        d_ref[...], m2_ref[...], v2_ref[...] = _adamw_math(g, w_ref[...], m_ref[...], v_ref[...])
        l_ref[...] = g[4:5, 896:1024]

    packed = jax.ShapeDtypeStruct((SMALL_ROWS, D), F32)
    return pl.pallas_call(
        body, name="small_allreduce",
        out_shape=[packed, packed, packed, packed, jax.ShapeDtypeStruct((1, 128), F32)],
        scratch_shapes=[pltpu.VMEM((N_DEV, SMALL_ROWS, D), F32), pltpu.SemaphoreType.DMA((N_DEV - 1,)),
                        pltpu.SemaphoreType.DMA((N_DEV - 1,))],
    )(small["g_mix"], small["g_mem"], small["g_ffn"], small["g_final"], small["b_alpha"], small["g_gla_head"],
      small["b_forget"], small["w_alpha_up"], small["loss"], w, m, v)


def _slabs(g, axis):
    R, C = g.shape
    if axis == 0:
        return g.reshape(N_DEV, R // N_DEV, C)
    return g.reshape(R, N_DEV, C // N_DEV).transpose(1, 0, 2)


def _unslab(t, axis):
    n, r, c = t.shape
    if axis == 0:
        return t.reshape(n * r, c)
    return t.transpose(1, 0, 2).reshape(r, n * c)


def kernel(x, mem, g_mix, w_in, w_alpha_up, b_alpha, b_forget, g_gla_head, g_mem, w_mem_kv, w_gla_o, w_fox_o, w_mem_o, w_out, g_ffn, w_ff1, w_ff2, g_final, loss_target, m_g_mix, m_w_in, m_w_alpha_up, m_b_alpha, m_b_forget, m_g_gla_head, m_g_mem, m_w_mem_kv, m_w_gla_o, m_w_fox_o, m_w_mem_o, m_w_out, m_g_ffn, m_w_ff1, m_w_ff2, m_g_final, v_g_mix, v_w_in, v_w_alpha_up, v_b_alpha, v_b_forget, v_g_gla_head, v_g_mem, v_w_mem_kv, v_w_gla_o, v_w_fox_o, v_w_mem_o, v_w_out, v_g_ffn, v_w_ff1, v_w_ff2, v_g_final):
    names = ["g_mix", "w_in", "w_alpha_up", "b_alpha", "b_forget", "g_gla_head", "g_mem", "w_mem_kv", "w_gla_o", "w_fox_o",
             "w_mem_o", "w_out", "g_ffn", "w_ff1", "w_ff2", "g_final"]
    w = dict(g_mix=g_mix, w_in=w_in, w_alpha_up=w_alpha_up, b_alpha=b_alpha, b_forget=b_forget, g_gla_head=g_gla_head,
             g_mem=g_mem, w_mem_kv=w_mem_kv, w_gla_o=w_gla_o, w_fox_o=w_fox_o, w_mem_o=w_mem_o, w_out=w_out, g_ffn=g_ffn,
             w_ff1=w_ff1, w_ff2=w_ff2, g_final=g_final)
    m = dict(g_mix=m_g_mix, w_in=m_w_in, w_alpha_up=m_w_alpha_up, b_alpha=m_b_alpha, b_forget=m_b_forget,
             g_gla_head=m_g_gla_head, g_mem=m_g_mem, w_mem_kv=m_w_mem_kv, w_gla_o=m_w_gla_o, w_fox_o=m_w_fox_o,
             w_mem_o=m_w_mem_o, w_out=m_w_out, g_ffn=m_g_ffn, w_ff1=m_w_ff1, w_ff2=m_w_ff2, g_final=m_g_final)
    v = dict(g_mix=v_g_mix, w_in=v_w_in, w_alpha_up=v_w_alpha_up, b_alpha=v_b_alpha, b_forget=v_b_forget,
             g_gla_head=v_g_gla_head, g_mem=v_g_mem, w_mem_kv=v_w_mem_kv, w_gla_o=v_w_gla_o, w_fox_o=v_w_fox_o,
             w_mem_o=v_w_mem_o, w_out=v_w_out, g_ffn=v_g_ffn, w_ff1=v_w_ff1, w_ff2=v_w_ff2, g_final=v_g_final)
    me = lax.axis_index("x") * 4 + lax.axis_index("y") * 2 + lax.axis_index("c")

    shards = [w[n][0].astype(BF16) for n, _ in BIG] + [w["w_alpha_up"][0].astype(BF16)]
    gathered = _exchange(shards, "gather_weights", True)
    p = {n: _unslab(t, ax) for (n, ax), t in zip(BIG, gathered[:-1])}
    p["w_alpha_up"] = _unslab(gathered[-1], 1)
    p.update(g_mix=g_mix, b_alpha=b_alpha, b_forget=b_forget, g_gla_head=g_gla_head, g_mem=g_mem, g_ffn=g_ffn, g_final=g_final)

    dx, big, small = _local_step(x[0], mem[0], loss_target[0], p)

    recv = _exchange([_slabs(big[n], ax) for n, ax in BIG], "scatter_grads", False)
    out_g, out_d, out_m, out_v = {}, {}, {}, {}
    for (n, ax), parts in zip(BIG, recv):
        g_, d_, m_, v_ = _adamw_sum(parts, w[n][0], m[n][0], v[n][0], "adamw_" + n)
        out_g[n], out_d[n], out_m[n], out_v[n] = g_[None], d_[None], m_[None], v_[None]

    full = lambda d: dict(d, w_alpha_up=jnp.zeros((1, GLA_R, GLA_K), F32))
    gs, ds, ms, vs, lrow = _small_allreduce(small, _pack_small(full(w)), _pack_small(full(m)), _pack_small(full(v)))
    g_s, d_s, m_s, v_s = _unpack_small(gs), _unpack_small(ds), _unpack_small(ms), _unpack_small(vs)
    for n in names:
        if n not in out_g and n != "w_alpha_up":
            out_g[n], out_d[n], out_m[n], out_v[n] = g_s[n], d_s[n], m_s[n], v_s[n]
    g_au = lax.dynamic_slice_in_dim(g_s["w_alpha_up"][0], me * (GLA_K // N_DEV), GLA_K // N_DEV, axis=1)
    g_, d_, m_, v_ = _adamw_sum(g_au[None], w_alpha_up[0], m_w_alpha_up[0], v_w_alpha_up[0], "adamw_w_alpha_up")
    out_g["w_alpha_up"], out_d["w_alpha_up"], out_m["w_alpha_up"], out_v["w_alpha_up"] = g_[None], d_[None], m_[None], v_[None]

    loss = jnp.sum(lrow) * (0.5 / D)
    return (loss, dx[None], *[out_g[n] for n in names], *[out_d[n] for n in names], *[out_m[n] for n in names],
            *[out_v[n] for n in names])
```

```python
import jax
import jax.numpy as jnp
from jax import lax
from jax.experimental import pallas as pl
from jax.experimental.pallas import tpu as pltpu

F32, BF16 = jnp.float32, jnp.bfloat16
HIGHEST = lax.Precision.HIGHEST
MESH = pl.DeviceIdType.MESH

N_DEV = 8
D = 1024
EPS = 1e-6
CHUNK = 64
N_MEM = 256
GLA_H, GLA_DK, GLA_DV = 4, 64, 128
GLA_K, GLA_V, GLA_R = 256, 512, 16
FOX_H, FOX_DH, FOX_W = 8, 64, 512
MEM_H, MEM_DH, MEM_W = 4, 128, 512
D_FF = 4096
D_IN = 6680
FOX_SCALE = 0.125
GLA_SCALE = 0.125
MEM_SCALE = MEM_DH ** -0.5
GLA_TAU_INV = 1.0 / 16.0
NEG = -1e30

O_GQ, O_GK, O_GV, O_GG, O_GA, O_FQ, O_FK, O_FV, O_FF, O_MQ, O_GT = 0, 256, 512, 1024, 1536, 1552, 2064, 2576, 3088, 3096, 3608
A_FQ, A_FK, A_FV, A_MQ, A_W = 1024, 2048, 3072, 3584, 4096
S_W = 640
G_W = 3072
P_W = 8192
FF_LANE = 16
AUG = 64

ADAM_LR, ADAM_B1, ADAM_B2, ADAM_EPS, ADAM_WD, ADAM_STEP = 0.001, 0.9, 0.999, 1e-08, 0.01, 10
V7X_VMEM_LIMIT = 48 * 1024 * 1024
FOX_TQ, FOX_TK = 2048, 512


def _params(*sem):
    return pltpu.CompilerParams(dimension_semantics=sem, vmem_limit_bytes=V7X_VMEM_LIMIT)


def _nt(a, b):
    return lax.dot_general(a, b, (((1,), (1,)), ((), ())), preferred_element_type=F32)


def _tn(a, b):
    return lax.dot_general(a, b, (((0,), (0,)), ((), ())), preferred_element_type=F32)


def _nn(a, b):
    return jnp.dot(a, b, preferred_element_type=F32)


def _log_sigmoid(z):
    return jnp.minimum(z, 0.0) - jnp.log(1.0 + jnp.exp(-jnp.abs(z)))


def _sum8(x):
    return x.reshape(x.shape[0] // 8, 8, x.shape[1]).sum(axis=0)


def _rms(xv):
    r = lax.rsqrt(jnp.mean(xv * xv, axis=-1, keepdims=True) + EPS)
    return r, xv * r


def _rms_bwd(du, g, r, xh):
    w = du * g
    return r * (w - xh * jnp.mean(w * xh, axis=-1, keepdims=True))


def _tile(n, pref):
    t = min(n, pref)
    assert n % t == 0, (n, t)
    return t


def _proj(x, g, w, out_dtype, name, emit_u):
    S = x.shape[0]
    N = w.shape[1]
    tm, tn = _tile(S, 512), _tile(N, 1024) if N % 1024 == 0 else N

    def body(x_ref, g_ref, w_ref, o_ref, *rest):
        u_s = rest[-1]

        @pl.when(pl.program_id(1) == 0)
        def _():
            r, xh = _rms(x_ref[...])
            u_s[...] = (xh * g_ref[...]).astype(BF16)
            if emit_u:
                rest[0][...] = u_s[...]

        o_ref[...] = _nn(u_s[...], w_ref[...]).astype(out_dtype)

    out_shape = [jax.ShapeDtypeStruct((S, N), out_dtype)]
    out_specs = [pl.BlockSpec((tm, tn), lambda i, j: (i, j))]
    if emit_u:
        out_shape.append(jax.ShapeDtypeStruct((S, D), BF16))
        out_specs.append(pl.BlockSpec((tm, D), lambda i, j: (i, 0)))
    return pl.pallas_call(
        body, name=name, grid=(S // tm, N // tn),
        in_specs=[pl.BlockSpec((tm, D), lambda i, j: (i, 0)), pl.BlockSpec((1, D), lambda i, j: (0, 0)),
                  pl.BlockSpec((D, tn), lambda i, j: (0, j))],
        out_specs=out_specs, out_shape=out_shape,
        scratch_shapes=[pltpu.VMEM((tm, D), BF16)],
        compiler_params=_params("arbitrary", "arbitrary"),
    )(x, g, w)


def _wgrad(a, b, name):
    S, Ka = a.shape
    N = b.shape[1]
    tka, tn, ts = _tile(Ka, 1024), _tile(N, 1024), _tile(S, 1024)
    n_s = S // ts

    def body(a_ref, b_ref, o_ref, acc):
        s = pl.program_id(2)

        @pl.when(s == 0)
        def _():
            acc[...] = jnp.zeros_like(acc)

        acc[...] += _tn(a_ref[...].astype(BF16), b_ref[...].astype(BF16))

        @pl.when(s == n_s - 1)
        def _():
            o_ref[...] = acc[...]

    return pl.pallas_call(
        body, name=name, grid=(Ka // tka, N // tn, n_s),
        in_specs=[pl.BlockSpec((ts, tka), lambda i, j, s: (s, i)), pl.BlockSpec((ts, tn), lambda i, j, s: (s, j))],
        out_specs=pl.BlockSpec((tka, tn), lambda i, j, s: (i, j)),
        out_shape=jax.ShapeDtypeStruct((Ka, N), F32),
        scratch_shapes=[pltpu.VMEM((tka, tn), F32)],
        compiler_params=_params("arbitrary", "arbitrary", "arbitrary"),
    )(a, b)


def _nt_rmsbwd(a, w, xin, g, dres, name, emit_bf16):
    S, K = a.shape
    tm, tk = _tile(S, 512), _tile(K, 2048)
    n_k = K // tk

    def body(a_ref, w_ref, x_ref, g_ref, r_ref, o_ref, *rest):
        dg_ref, acc = rest[-2], rest[-1]
        i, k = pl.program_id(0), pl.program_id(1)

        @pl.when(k == 0)
        def _():
            acc[...] = jnp.zeros_like(acc)

        acc[...] += _nt(a_ref[...], w_ref[...])

        @pl.when(k == n_k - 1)
        def _():
            du = acc[...]
            r, xh = _rms(x_ref[...])
            out = r_ref[...] + _rms_bwd(du, g_ref[...], r, xh)
            o_ref[...] = out
            if emit_bf16:
                rest[0][...] = out.astype(BF16)
            part = _sum8(du * xh)

            @pl.when(i == 0)
            def _():
                dg_ref[...] = part

            @pl.when(i > 0)
            def _():
                dg_ref[...] += part

    row = pl.BlockSpec((tm, D), lambda i, k: (i, 0))
    out_shape = [jax.ShapeDtypeStruct((S, D), F32)]
    out_specs = [row]
    if emit_bf16:
        out_shape.append(jax.ShapeDtypeStruct((S, D), BF16))
        out_specs.append(row)
    out_shape.append(jax.ShapeDtypeStruct((8, D), F32))
    out_specs.append(pl.BlockSpec((8, D), lambda i, k: (0, 0)))
    return pl.pallas_call(
        body, name=name, grid=(S // tm, n_k),
        in_specs=[pl.BlockSpec((tm, tk), lambda i, k: (i, k)), pl.BlockSpec((D, tk), lambda i, k: (0, k)),
                  row, pl.BlockSpec((1, D), lambda i, k: (0, 0)), row],
        out_specs=out_specs, out_shape=out_shape,
        scratch_shapes=[pltpu.VMEM((tm, D), F32)],
        compiler_params=_params("arbitrary", "arbitrary"),
    )(a, w, xin, g, dres)


def _merge(og, ofox, omem, wg, wf, wm, pg):
    S = og.shape[0]
    tm = _tile(S, 256)

    def body(og_ref, of_ref, om_ref, wg_ref, wf_ref, wm_ref, pg_ref, y_ref, mg_ref):
        tot = None
        for i, (o_ref, w_ref) in enumerate(((og_ref, wg_ref), (of_ref, wf_ref), (om_ref, wm_ref))):
            y = _nn(o_ref[...].astype(BF16), w_ref[...])
            y_ref[i] = y
            t = jax.nn.sigmoid(pg_ref[:, D * i:D * (i + 1)]) * y
            tot = t if tot is None else tot + t
        mg_ref[...] = tot.astype(BF16)

    o_spec = pl.BlockSpec((tm, 512), lambda i: (i, 0))
    w_spec = pl.BlockSpec((512, D), lambda i: (0, 0))
    return pl.pallas_call(
        body, name="merge", grid=(S // tm,),
        in_specs=[o_spec, o_spec, o_spec, w_spec, w_spec, w_spec, pl.BlockSpec((tm, G_W), lambda i: (i, 0))],
        out_specs=[pl.BlockSpec((3, tm, D), lambda i: (0, i, 0)), pl.BlockSpec((tm, D), lambda i: (i, 0))],
        out_shape=[jax.ShapeDtypeStruct((3, S, D), F32), jax.ShapeDtypeStruct((S, D), BF16)],
        compiler_params=_params("arbitrary"),
    )(og, ofox, omem, wg, wf, wm, pg)


def _out_proj(mg, w_out, x, g_ffn):
    S = x.shape[0]
    tm = _tile(S, 512)

    def body(mg_ref, w_ref, x_ref, g_ref, h_ref, u_ref):
        h = x_ref[...] + _nn(mg_ref[...], w_ref[...])
        h_ref[...] = h
        r, xh = _rms(h)
        u_ref[...] = (xh * g_ref[...]).astype(BF16)

    row = pl.BlockSpec((tm, D), lambda i: (i, 0))
    return pl.pallas_call(
        body, name="out_proj", grid=(S // tm,),
        in_specs=[row, pl.BlockSpec((D, D), lambda i: (0, 0)), row, pl.BlockSpec((1, D), lambda i: (0, 0))],
        out_specs=[row, row],
        out_shape=[jax.ShapeDtypeStruct((S, D), F32), jax.ShapeDtypeStruct((S, D), BF16)],
        compiler_params=_params("arbitrary"),
    )(mg, w_out, x, g_ffn)


def _ff1(u2, w1):
    S = u2.shape[0]
    tm, tn = _tile(S, 1024), 1024

    def body(u_ref, w_ref, a_ref, act_ref):
        a = _nn(u_ref[...], w_ref[...])
        a_ref[...] = a.astype(BF16)
        act_ref[...] = jnp.square(jnp.maximum(a, 0.0)).astype(BF16)

    blk = pl.BlockSpec((tm, tn), lambda i, j: (i, j))
    return pl.pallas_call(
        body, name="ff1", grid=(S // tm, D_FF // tn),
        in_specs=[pl.BlockSpec((tm, D), lambda i, j: (i, 0)), pl.BlockSpec((D, tn), lambda i, j: (0, j))],
        out_specs=[blk, blk],
        out_shape=[jax.ShapeDtypeStruct((S, D_FF), BF16), jax.ShapeDtypeStruct((S, D_FF), BF16)],
        compiler_params=_params("arbitrary", "arbitrary"),
    )(u2, w1)


def _ff2_loss(act, w2, h1, g_final, target):
    S = act.shape[0]
    tm, tk = _tile(S, 512), 2048
    n_k = D_FF // tk

    def body(a_ref, w_ref, h_ref, g_ref, t_ref, d_ref, db_ref, ls_ref, dg_ref, acc):
        i, k = pl.program_id(0), pl.program_id(1)

        @pl.when(k == 0)
        def _():
            acc[...] = jnp.zeros_like(acc)

        acc[...] += _nn(a_ref[...], w_ref[...])

        @pl.when(k == n_k - 1)
        def _():
            h2 = h_ref[...] + acc[...]
            r, xh = _rms(h2)
            gf = g_ref[...]
            err = xh * gf - t_ref[...]
            dy = err * (1.0 / D)
            dh = _rms_bwd(dy, gf, r, xh)
            d_ref[...] = dh
            db_ref[...] = dh.astype(BF16)
            lp, gp = _sum8(err * err), _sum8(dy * xh)

            @pl.when(i == 0)
            def _():
                ls_ref[...] = lp
                dg_ref[...] = gp

            @pl.when(i > 0)
            def _():
                ls_ref[...] += lp
                dg_ref[...] += gp

    row = pl.BlockSpec((tm, D), lambda i, k: (i, 0))
    part = pl.BlockSpec((8, D), lambda i, k: (0, 0))
    return pl.pallas_call(
        body, name="ff2_loss", grid=(S // tm, n_k),
        in_specs=[pl.BlockSpec((tm, tk), lambda i, k: (i, k)), pl.BlockSpec((tk, D), lambda i, k: (k, 0)),
                  row, pl.BlockSpec((1, D), lambda i, k: (0, 0)), row],
        out_specs=[row, row, part, part],
        out_shape=[jax.ShapeDtypeStruct((S, D), F32), jax.ShapeDtypeStruct((S, D), BF16),
                   jax.ShapeDtypeStruct((8, D), F32), jax.ShapeDtypeStruct((8, D), F32)],
        scratch_shapes=[pltpu.VMEM((tm, D), F32)],
        compiler_params=_params("arbitrary", "arbitrary"),
    )(act, w2, h1, g_final, target)


def _dact(dh2b, w2, a):
    S = a.shape[0]
    tm, tn = _tile(S, 1024), 1024

    def body(d_ref, w_ref, a_ref, o_ref):
        da = _nt(d_ref[...], w_ref[...])
        o_ref[...] = (da * (2.0 * jnp.maximum(a_ref[...].astype(F32), 0.0))).astype(BF16)

    blk = pl.BlockSpec((tm, tn), lambda i, j: (i, j))
    return pl.pallas_call(
        body, name="dact", grid=(S // tm, D_FF // tn),
        in_specs=[pl.BlockSpec((tm, D), lambda i, j: (i, 0)), pl.BlockSpec((tn, D), lambda i, j: (j, 0)), blk],
        out_specs=blk, out_shape=jax.ShapeDtypeStruct((S, D_FF), BF16),
        compiler_params=_params("arbitrary", "arbitrary"),
    )(dh2b, w2, a)


def _dmerge(dh1b, w_out, pg, y3):
    S = dh1b.shape[0]
    tm = _tile(S, 256)

    def body(d_ref, w_ref, pg_ref, y_ref, dy_ref, dg_ref):
        dm = _nt(d_ref[...], w_ref[...])
        for i in range(3):
            gt = jax.nn.sigmoid(pg_ref[:, D * i:D * (i + 1)])
            dy_ref[i] = (dm * gt).astype(BF16)
            dg_ref[:, D * i:D * (i + 1)] = (dm * y_ref[i] * (gt * (1.0 - gt))).astype(BF16)

    return pl.pallas_call(
        body, name="dmerge", grid=(S // tm,),
        in_specs=[pl.BlockSpec((tm, D), lambda i: (i, 0)), pl.BlockSpec((D, D), lambda i: (0, 0)),
                  pl.BlockSpec((tm, G_W), lambda i: (i, 0)), pl.BlockSpec((3, tm, D), lambda i: (0, i, 0))],
        out_specs=[pl.BlockSpec((3, tm, D), lambda i: (0, i, 0)), pl.BlockSpec((tm, G_W), lambda i: (i, 0))],
        out_shape=[jax.ShapeDtypeStruct((3, S, D), BF16), jax.ShapeDtypeStruct((S, G_W), BF16)],
        compiler_params=_params("arbitrary"),
    )(dh1b, w_out, pg, y3)


def _dout3(dy3, wg, wf, wm):
    S = dy3.shape[1]
    tm = _tile(S, 512)

    def body(dy_ref, wg_ref, wf_ref, wm_ref, o_ref):
        for i, w_ref in enumerate((wg_ref, wf_ref, wm_ref)):
            o_ref[i] = _nt(dy_ref[i], w_ref[...])

    w_spec = pl.BlockSpec((512, D), lambda i: (0, 0))
    return pl.pallas_call(
        body, name="dout3", grid=(S // tm,),
        in_specs=[pl.BlockSpec((3, tm, D), lambda i: (0, i, 0)), w_spec, w_spec, w_spec],
        out_specs=pl.BlockSpec((3, tm, 512), lambda i: (0, i, 0)),
        out_shape=jax.ShapeDtypeStruct((3, S, 512), F32),
        compiler_params=_params("arbitrary"),
    )(dy3, wg, wf, wm)


def _gla_block_terms(gq_ref, gk_ref, ps_ref, wau_ref, ba_ref, tb):
    gaff = ps_ref[:, 512:640]
    z = _nn(gaff.astype(BF16), wau_ref[...]) + ba_ref[...]
    la = _log_sigmoid(z) * GLA_TAU_INV
    rr = lax.broadcasted_iota(jnp.int32, (tb, tb), 0)
    cc = lax.broadcasted_iota(jnp.int32, (tb, tb), 1)
    same = jnp.right_shift(rr, 6) == jnp.right_shift(cc, 6)
    tri = jnp.where(same & (cc <= rr), 1.0, 0.0).astype(F32)
    ones = jnp.where(same, 1.0, 0.0).astype(F32)
    b = jnp.dot(tri, la, preferred_element_type=F32, precision=HIGHEST)
    bl = jnp.dot(ones, la, preferred_element_type=F32, precision=HIGHEST)
    e_pos, e_neg, e_last, dec = jnp.exp(b), jnp.exp(-b), jnp.exp(bl - b), jnp.exp(bl)
    q = gq_ref[...].astype(F32) * GLA_SCALE
    k = gk_ref[...].astype(F32)
    return dict(gaff=gaff, z=z, same=same, rr=rr, cc=cc, ones=ones, e_pos=e_pos, e_neg=e_neg, e_last=e_last, dec=dec,
                qp=q * e_pos, qn=q * e_neg, kn=k * e_neg, kp=k * e_pos, kd=k * e_last)


def _head_masked(x, store):
    lane = lax.broadcasted_iota(jnp.int32, x.shape, 1)
    for h in range(GLA_H):
        store[h] = jnp.where(jnp.right_shift(lane, 6) == h, x, 0.0).astype(BF16)


def _gla_fwd(pa, ps, wau, ba, gh):
    S = pa.shape[0]
    tb = _tile(S, 512)
    n_c = tb // CHUNK

    def body(gq_ref, gk_ref, gv_ref, ps_ref, wau_ref, ba_ref, gh_ref, o_ref, og_ref, sp_ref,
             qpm, qnm, kdm, kn_s, kp_s, dec_s, state):
        @pl.when(pl.program_id(0) == 0)
        def _():
            state[...] = jnp.zeros_like(state)

        t = _gla_block_terms(gq_ref, gk_ref, ps_ref, wau_ref, ba_ref, tb)
        _head_masked(t["qp"], qpm)
        _head_masked(t["qn"], qnm)
        _head_masked(t["kd"], kdm)
        kn_s[...] = t["kn"].astype(BF16)
        kp_s[...] = t["kp"].astype(BF16)
        dec_s[...] = t["dec"]
        lower = lax.broadcasted_iota(jnp.int32, (CHUNK, CHUNK), 0) >= lax.broadcasted_iota(jnp.int32, (CHUNK, CHUNK), 1)

        def chunk(c, carry):
            r0 = pl.multiple_of(c * CHUNK, CHUNK)
            rows = pl.ds(r0, CHUNK)
            sp = state[...]
            sp_ref[c] = sp
            spb = sp.astype(BF16)
            knc, kpc = kn_s[rows, :], kp_s[rows, :]
            new = sp * dec_s[pl.ds(r0, 1), :]
            for h in range(GLA_H):
                cols = slice(GLA_DV * h, GLA_DV * (h + 1))
                qpc, qnc = qpm[h, rows, :], qnm[h, rows, :]
                attn = jnp.where(lower, _nt(qpc, knc), _nt(qnc, kpc)).astype(BF16)
                vh = gv_ref[rows, cols]
                o_ref[rows, cols] = _nn(attn, vh) + _nt(qpc, spb)
                new = new + _tn(vh, kdm[h, rows, :])
            state[...] = new
            return carry

        lax.fori_loop(0, n_c, chunk, 0)
        for h in range(GLA_H):
            cols = slice(GLA_DV * h, GLA_DV * (h + 1))
            r, xh = _rms(o_ref[:, cols])
            gg = ps_ref[:, cols]
            og_ref[:, cols] = ((xh * gh_ref[:, cols]) * (gg * jax.nn.sigmoid(gg))).astype(BF16)

    return pl.pallas_call(
        body, name="gla_fwd", grid=(S // tb,),
        in_specs=[pl.BlockSpec((tb, GLA_K), lambda i: (i, 0)), pl.BlockSpec((tb, GLA_K), lambda i: (i, 1)),
                  pl.BlockSpec((tb, GLA_V), lambda i: (i, 1)), pl.BlockSpec((tb, S_W), lambda i: (i, 0)),
                  pl.BlockSpec((128, GLA_K), lambda i: (0, 0)), pl.BlockSpec((1, GLA_K), lambda i: (0, 0)),
                  pl.BlockSpec((1, GLA_V), lambda i: (0, 0))],
        out_specs=[pl.BlockSpec((tb, GLA_V), lambda i: (i, 0)), pl.BlockSpec((tb, GLA_V), lambda i: (i, 0)),
                   pl.BlockSpec((n_c, GLA_DV, GLA_K), lambda i: (i, 0, 0))],
        out_shape=[jax.ShapeDtypeStruct((S, GLA_V), F32), jax.ShapeDtypeStruct((S, GLA_V), BF16),
                   jax.ShapeDtypeStruct((S // CHUNK, GLA_DV, GLA_K), F32)],
        scratch_shapes=[pltpu.VMEM((GLA_H, tb, GLA_K), BF16), pltpu.VMEM((GLA_H, tb, GLA_K), BF16),
                        pltpu.VMEM((GLA_H, tb, GLA_K), BF16), pltpu.VMEM((tb, GLA_K), BF16), pltpu.VMEM((tb, GLA_K), BF16),
                        pltpu.VMEM((tb, GLA_K), F32), pltpu.VMEM((GLA_DV, GLA_K), F32)],
        compiler_params=_params("arbitrary"),
    )(pa, pa, pa, ps, wau, ba, gh)


def _gla_bwd(pa, ps, wau, ba, gh, o_gla, d_og, sprev, dgaff_fox):
    S = pa.shape[0]
    tb = _tile(S, 512)
    n_c = tb // CHUNK
    n_b = S // tb

    def body(gq_ref, gk_ref, gv_ref, ps_ref, wau_ref, ba_ref, gh_ref, o_ref, dog_ref, sp_ref, dfx_ref,
             dgq_ref, dgk_ref, dgv_ref, dgg_ref, dgaff_ref, dwau_ref, dba_ref, dgh_ref,
             qpm, qnm, kdm, kn_s, kp_s, dec_s, do_s, dqp_s, dqn_s, dkn_s, dkp_s, dkd_s, ddec_s, dstate):
        first = pl.program_id(0) == 0

        @pl.when(first)
        def _():
            dstate[...] = jnp.zeros_like(dstate)

        t = _gla_block_terms(gq_ref, gk_ref, ps_ref, wau_ref, ba_ref, tb)
        _head_masked(t["qp"], qpm)
        _head_masked(t["qn"], qnm)
        _head_masked(t["kd"], kdm)
        kn_s[...] = t["kn"].astype(BF16)
        kp_s[...] = t["kp"].astype(BF16)
        dec_s[...] = t["dec"]

        dgh_parts = []
        for h in range(GLA_H):
            cols = slice(GLA_DV * h, GLA_DV * (h + 1))
            r, xh = _rms(o_ref[:, cols])
            g = gh_ref[:, cols]
            gg = ps_ref[:, cols]
            sg = jax.nn.sigmoid(gg)
            d_out = dog_ref[:, cols]
            dgg_ref[:, cols] = (d_out * (xh * g) * (sg * (1.0 + gg * (1.0 - sg)))).astype(BF16)
            d_on = d_out * (gg * sg)
            dgh_parts.append(_sum8(d_on * xh))
            do_s[:, cols] = _rms_bwd(d_on, g, r, xh).astype(BF16)
        dgh_part = jnp.concatenate(dgh_parts, axis=1)

        lower = lax.broadcasted_iota(jnp.int32, (CHUNK, CHUNK), 0) >= lax.broadcasted_iota(jnp.int32, (CHUNK, CHUNK), 1)
        lane = lax.broadcasted_iota(jnp.int32, (CHUNK, GLA_K), 1)

        def chunk(j, carry):
            c = n_c - 1 - j
            r0 = pl.multiple_of(c * CHUNK, CHUNK)
            rows = pl.ds(r0, CHUNK)
            ds_next = dstate[...]
            dsb = ds_next.astype(BF16)
            sp = sp_ref[c]
            spb = sp.astype(BF16)
            knc, kpc = kn_s[rows, :], kp_s[rows, :]
            dec_row = dec_s[pl.ds(r0, 1), :]
            ddec_s[rows, :] = jnp.broadcast_to(jnp.sum(ds_next * sp, axis=0, keepdims=True), (CHUNK, GLA_K))
            new = ds_next * dec_row
            dqp = jnp.zeros((CHUNK, GLA_K), F32)
            dqn, dkn, dkp, dkd = dqp, dqp, dqp, dqp
            for h in range(GLA_H):
                cols = slice(GLA_DV * h, GLA_DV * (h + 1))
                mine = jnp.right_shift(lane, 6) == h
                qpc, qnc, kdc = qpm[h, rows, :], qnm[h, rows, :], kdm[h, rows, :]
                vh = gv_ref[rows, cols]
                doh = do_s[rows, cols]
                attn = jnp.where(lower, _nt(qpc, knc), _nt(qnc, kpc)).astype(BF16)
                da = _nt(doh, vh)
                dac = jnp.where(lower, da, 0.0).astype(BF16)
                daa = jnp.where(lower, 0.0, da).astype(BF16)
                dqp = dqp + jnp.where(mine, _nn(dac, knc) + _nn(doh, spb), 0.0)
                dqn = dqn + jnp.where(mine, _nn(daa, kpc), 0.0)
                dkn = dkn + _tn(dac, qpc)
                dkp = dkp + _tn(daa, qnc)
                dkd = dkd + jnp.where(mine, _nn(vh, dsb), 0.0)
                dgv_ref[rows, cols] = (_tn(attn, doh) + _nt(kdc, dsb)).astype(BF16)
                new = new + _tn(doh, qpc)
            dqp_s[rows, :] = dqp
            dqn_s[rows, :] = dqn
            dkn_s[rows, :] = dkn
            dkp_s[rows, :] = dkp
            dkd_s[rows, :] = dkd
            dstate[...] = new
            return carry

        lax.fori_loop(0, n_c, chunk, 0)

        dqp, dqn, dkn, dkp, dkd = dqp_s[...], dqn_s[...], dkn_s[...], dkp_s[...], dkd_s[...]
        dgq_ref[...] = ((dqp * t["e_pos"] + dqn * t["e_neg"]) * GLA_SCALE).astype(BF16)
        dgk_ref[...] = (dkn * t["e_neg"] + dkp * t["e_pos"] + dkd * t["e_last"]).astype(BF16)
        kd_term = dkd * t["kd"]
        db = dqp * t["qp"] - dqn * t["qn"] - dkn * t["kn"] + dkp * t["kp"] - kd_term
        upper = jnp.where(t["same"] & (t["cc"] >= t["rr"]), 1.0, 0.0).astype(F32)
        dla = (jnp.dot(upper, db, preferred_element_type=F32, precision=HIGHEST)
               + jnp.dot(t["ones"], kd_term, preferred_element_type=F32, precision=HIGHEST)
               + ddec_s[...] * t["dec"])
        dz = dla * GLA_TAU_INV * jax.nn.sigmoid(-t["z"])
        dzb = dz.astype(BF16)
        dgaff_ref[...] = (_nt(dzb, wau_ref[...]) + dfx_ref[...]).astype(BF16)
        dwau_part = _tn(t["gaff"].astype(BF16), dzb)
        dba_part = _sum8(dz)

        @pl.when(first)
        def _():
            dwau_ref[...] = dwau_part
            dba_ref[...] = dba_part
            dgh_ref[...] = dgh_part

        @pl.when(jnp.logical_not(first))
        def _():
            dwau_ref[...] += dwau_part
            dba_ref[...] += dba_part
            dgh_ref[...] += dgh_part

    rev = lambda i: (n_b - 1 - i, 0)
    f32k = pltpu.VMEM((tb, GLA_K), F32)
    bf4 = pltpu.VMEM((GLA_H, tb, GLA_K), BF16)
    return pl.pallas_call(
        body, name="gla_bwd", grid=(n_b,),
        in_specs=[pl.BlockSpec((tb, GLA_K), rev), pl.BlockSpec((tb, GLA_K), lambda i: (n_b - 1 - i, 1)),
                  pl.BlockSpec((tb, GLA_V), lambda i: (n_b - 1 - i, 1)), pl.BlockSpec((tb, S_W), rev),
                  pl.BlockSpec((128, GLA_K), lambda i: (0, 0)), pl.BlockSpec((1, GLA_K), lambda i: (0, 0)),
                  pl.BlockSpec((1, GLA_V), lambda i: (0, 0)), pl.BlockSpec((tb, GLA_V), rev), pl.BlockSpec((tb, GLA_V), rev),
                  pl.BlockSpec((n_c, GLA_DV, GLA_K), lambda i: (n_b - 1 - i, 0, 0)), pl.BlockSpec((tb, 128), rev)],
        out_specs=[pl.BlockSpec((tb, GLA_K), rev), pl.BlockSpec((tb, GLA_K), rev), pl.BlockSpec((tb, GLA_V), rev),
                   pl.BlockSpec((tb, GLA_V), rev), pl.BlockSpec((tb, 128), rev),
                   pl.BlockSpec((128, GLA_K), lambda i: (0, 0)), pl.BlockSpec((8, GLA_K), lambda i: (0, 0)),
                   pl.BlockSpec((8, GLA_V), lambda i: (0, 0))],
        out_shape=[jax.ShapeDtypeStruct((S, GLA_K), BF16), jax.ShapeDtypeStruct((S, GLA_K), BF16),
                   jax.ShapeDtypeStruct((S, GLA_V), BF16), jax.ShapeDtypeStruct((S, GLA_V), BF16),
                   jax.ShapeDtypeStruct((S, 128), BF16), jax.ShapeDtypeStruct((128, GLA_K), F32),
                   jax.ShapeDtypeStruct((8, GLA_K), F32), jax.ShapeDtypeStruct((8, GLA_V), F32)],
        scratch_shapes=[bf4, bf4, bf4, pltpu.VMEM((tb, GLA_K), BF16), pltpu.VMEM((tb, GLA_K), BF16), f32k,
                        pltpu.VMEM((tb, GLA_V), BF16), f32k, f32k, f32k, f32k, f32k, f32k, pltpu.VMEM((GLA_DV, GLA_K), F32)],
        compiler_params=_params("arbitrary"),
    )(pa, pa, pa, ps, wau, ba, gh, o_gla, d_og, sprev, dgaff_fox)


def _split3(x):
    x1 = x.astype(BF16).astype(F32)
    x2 = (x - x1).astype(BF16).astype(F32)
    x3 = (x - x1 - x2).astype(BF16).astype(F32)
    return x1, x2, x3


def _fox_prep(pa, ps, bfg):
    S = pa.shape[0]
    tm = _tile(S, 512)

    def body(ps_ref, b_ref, fq_ref, fk_ref, fv_ref, q_ref, k_ref, qt_ref, kt_ref, vt_ref, carry):
        @pl.when(pl.program_id(0) == 0)
        def _():
            carry[...] = jnp.zeros_like(carry)

        vt_ref[...] = fv_ref[...].astype(F32).T.astype(BF16)
        lf = _log_sigmoid(ps_ref[...] + b_ref[...])
        rr = lax.broadcasted_iota(jnp.int32, (tm, tm), 0)
        cc = lax.broadcasted_iota(jnp.int32, (tm, tm), 1)
        tri = jnp.where(cc <= rr, 1.0, 0.0).astype(F32)
        f = jnp.dot(tri, lf, preferred_element_type=F32, precision=HIGHEST) + carry[0:1, :]
        carry[...] = jnp.broadcast_to(f[tm - 1:tm, :], carry.shape)
        f1, f2, f3 = _split3(f)
        lane = lax.broadcasted_iota(jnp.int32, (tm, 128), 1)
        for h in range(FOX_H):
            cols = slice(128 * h, 128 * (h + 1))
            c = FF_LANE + h
            a1, a2, a3 = f1[:, c:c + 1], f2[:, c:c + 1], f3[:, c:c + 1]
            q = fq_ref[:, cols].astype(F32) * FOX_SCALE
            k = fk_ref[:, cols].astype(F32)
            for n, a in enumerate((a1, a2, a3)):
                q = jnp.where(lane == AUG + n, a, q)
                k = jnp.where(lane == AUG + 3 + n, -a, k)
            q = jnp.where((lane >= AUG + 3) & (lane < AUG + 6), 1.0, q)
            k = jnp.where((lane >= AUG) & (lane < AUG + 3), 1.0, k)
            q_ref[:, cols] = q.astype(BF16)
            k_ref[:, cols] = k.astype(BF16)
            qt_ref[cols, :] = q.T.astype(BF16)
            kt_ref[cols, :] = k.T.astype(BF16)

    wide = lambda j: pl.BlockSpec((tm, 1024), lambda i: (i, j))
    tall = lambda n: pl.BlockSpec((n, tm), lambda i: (0, i))
    return pl.pallas_call(
        body, name="fox_prep", grid=(S // tm,),
        in_specs=[pl.BlockSpec((tm, 128), lambda i: (i, 4)), pl.BlockSpec((1, 128), lambda i: (0, 0)), wide(1), wide(2),
                  pl.BlockSpec((tm, FOX_W), lambda i: (i, A_FV // FOX_W))],
        out_specs=[wide(0), wide(0), tall(1024), tall(1024), tall(FOX_W)],
        out_shape=[jax.ShapeDtypeStruct((S, 1024), BF16), jax.ShapeDtypeStruct((S, 1024), BF16),
                   jax.ShapeDtypeStruct((1024, S), BF16), jax.ShapeDtypeStruct((1024, S), BF16),
                   jax.ShapeDtypeStruct((FOX_W, S), BF16)],
        scratch_shapes=[pltpu.VMEM((8, 128), F32)],
        compiler_params=_params("arbitrary"),
    )(ps, bfg, pa, pa, pa)


def _fox_fwd(qa, ka, vt):
    S = qa.shape[0]
    tq = _tile(S, FOX_TQ)
    tk = _tile(tq, FOX_TK)
    n_sub = tq // tk

    def body(q_ref, k_ref, vt_ref, o_ref, lse_ref):
        i = pl.program_id(1)
        both = lambda f: tuple(f(hh) for hh in range(2))

        def blk(j, carry, diag):
            ks = pl.ds(pl.multiple_of(j * tk, tk), tk)
            q0 = 0 if diag is None else diag * tk

            def head(hh):
                m, l, acc = carry[hh]
                mo, lo, ao = m[:, q0:], l[:, q0:], acc[:, q0:]
                s = _nt(k_ref[ks, 128 * hh:128 * (hh + 1)], q_ref[q0:, 128 * hh:128 * (hh + 1)])
                if diag is not None:
                    live = lax.broadcasted_iota(jnp.int32, s.shape, 1) >= lax.broadcasted_iota(jnp.int32, s.shape, 0)
                    s = jnp.where(live, s, NEG)
                mn = jnp.maximum(mo, jnp.max(s, axis=0, keepdims=True))
                p = jnp.exp(s - mn)
                al = jnp.exp(mo - mn)
                ln = al * lo + jnp.sum(p, axis=0, keepdims=True)
                an = al * ao + _nn(vt_ref[FOX_DH * hh:FOX_DH * (hh + 1), ks], p.astype(BF16))
                if q0:
                    mn, ln, an = (jnp.concatenate([old[:, :q0], new], axis=1) for old, new in ((m, mn), (l, ln), (acc, an)))
                return mn, ln, an

            return both(head)

        one = (jnp.full((1, tq), NEG, F32), jnp.zeros((1, tq), F32), jnp.zeros((FOX_DH, tq), F32))
        past = i * n_sub
        carry = lax.fori_loop(0, past // 2, lambda jj, c: blk(2 * jj + 1, blk(2 * jj, c, None), None), (one, one))
        carry = lax.cond(past % 2 == 1, lambda c: blk(past - 1, c, None), lambda c: c, carry)
        for d in range(n_sub):
            carry = blk(past + d, carry, d)
        (m0, l0, a0), (m1, l1, a1) = carry
        o_ref[...] = jnp.concatenate([a0 / l0, a1 / l1], axis=0).T
        lse_ref[0, 0:1, :] = m0 + jnp.log(l0)
        lse_ref[0, 1:2, :] = m1 + jnp.log(l1)
        lse_ref[0, 2:8, :] = jnp.zeros((6, tq), F32)

    once = pl.Buffered(1)
    return pl.pallas_call(
        body, name="fox_fwd", grid=(FOX_H // 2, S // tq),
        in_specs=[pl.BlockSpec((tq, 256), lambda p, i: (i, p)),
                  pl.BlockSpec((S, 256), lambda p, i: (0, p), pipeline_mode=once),
                  pl.BlockSpec((128, S), lambda p, i: (p, 0), pipeline_mode=once)],
        out_specs=[pl.BlockSpec((tq, 128), lambda p, i: (i, p)), pl.BlockSpec((1, 8, tq), lambda p, i: (p, 0, i))],
        out_shape=[jax.ShapeDtypeStruct((S, FOX_W), F32), jax.ShapeDtypeStruct((FOX_H // 2, 8, S), F32)],
        compiler_params=_params("arbitrary", "arbitrary"),
    )(qa, ka, vt)


def _fox_delta(d_o, o):
    S = o.shape[0]
    tm = _tile(S, 512)

    def body(d_ref, o_ref, db_ref, dbt_ref, dl_ref):
        d = d_ref[...]
        db_ref[...] = d.astype(BF16)
        dbt_ref[...] = d.T.astype(BF16)
        prod = d * o_ref[...]
        rr = lax.broadcasted_iota(jnp.int32, (8, 128), 0)
        cc = lax.broadcasted_iota(jnp.int32, (8, 128), 1)
        ind = jnp.where(jnp.right_shift(cc, 6) == rr, 1.0, 0.0).astype(F32)
        for p in range(FOX_H // 2):
            dl_ref[p] = lax.dot_general(ind, prod[:, 128 * p:128 * (p + 1)], (((1,), (1,)), ((), ())),
                                        preferred_element_type=F32, precision=HIGHEST)

    row = pl.BlockSpec((tm, FOX_W), lambda i: (i, 0))
    return pl.pallas_call(
        body, name="fox_delta", grid=(S // tm,),
        in_specs=[row, row],
        out_specs=[row, pl.BlockSpec((FOX_W, tm), lambda i: (0, i)), pl.BlockSpec((FOX_H // 2, 8, tm), lambda i: (0, 0, i))],
        out_shape=[jax.ShapeDtypeStruct((S, FOX_W), BF16), jax.ShapeDtypeStruct((FOX_W, S), BF16),
                   jax.ShapeDtypeStruct((FOX_H // 2, 8, S), F32)],
        compiler_params=_params("arbitrary"),
    )(d_o, o)


def _fox_bwd(qa, qat, ka, kat, pa, dob, dobt, lse, delta):
    S = qa.shape[0]
    tk = _tile(S, 512)
    wide = _tile(S, FOX_TQ)
    ratio = wide // tk
    n_wide = S // wide

    def body(q_ref, qt_ref, k_ref, kt_ref, v_ref, do_ref, dot_ref, lse_ref, dl_ref, dq_ref, dk_ref, dv_ref):
        h, jb = pl.program_id(0), pl.program_id(1)
        hh = h % 2

        @pl.when(jb == 0)
        def _():
            dq_ref[...] = jnp.zeros_like(dq_ref)

        lane = lax.broadcasted_iota(jnp.int32, (tk, 128), 1)
        vm = jnp.where(jnp.right_shift(lane, 6) == hh, v_ref[...], jnp.zeros((), BF16))
        kb, ktb = k_ref[...], kt_ref[...]
        mine = pl.ds(pl.multiple_of(hh * FOX_DH, FOX_DH), FOX_DH)

        def blk(ib, tq, carry, masked):
            dk, dv = carry
            qs = pl.ds(pl.multiple_of(ib * tq, tq), tq)
            p = jnp.exp(_nt(kb, q_ref[qs, :]) - lse_ref[0, pl.ds(hh, 1), qs])
            if masked:
                live = lax.broadcasted_iota(jnp.int32, p.shape, 1) >= lax.broadcasted_iota(jnp.int32, p.shape, 0)
                p = jnp.where(live, p, 0.0)
            ds = (p * (_nt(vm, do_ref[qs, :]) - dl_ref[0, pl.ds(hh, 1), qs])).astype(BF16)
            dq_ref[:, qs] += _nn(ktb, ds)
            return dk + _nt(qt_ref[:, qs], ds), dv + _nt(dot_ref[mine, qs], p.astype(BF16))

        carry = blk(jb, tk, (jnp.zeros((128, tk), F32), jnp.zeros((FOX_DH, tk), F32)), True)
        first_wide = jb // ratio + 1
        carry = lax.fori_loop(jb + 1, jnp.minimum(first_wide * ratio, S // tk), lambda ib, c: blk(ib, tk, c, False), carry)
        rest = jnp.maximum(n_wide - first_wide, 0)
        carry = lax.fori_loop(0, rest // 2, lambda t, c: blk(first_wide + 2 * t + 1, wide, blk(first_wide + 2 * t, wide, c, False),
                                                             False), carry)
        dk, dv = lax.cond(rest % 2 == 1, lambda c: blk(n_wide - 1, wide, c, False), lambda c: c, carry)
        dk_ref[...] = dk
        dv_ref[...] = dv

    once = pl.Buffered(1)
    rows = pl.BlockSpec((1, 8, S), lambda h, j: (h // 2, 0, 0))
    return pl.pallas_call(
        body, name="fox_bwd", grid=(FOX_H, S // tk),
        in_specs=[pl.BlockSpec((S, 128), lambda h, j: (0, h), pipeline_mode=once),
                  pl.BlockSpec((128, S), lambda h, j: (h, 0), pipeline_mode=once),
                  pl.BlockSpec((tk, 128), lambda h, j: (j, h)), pl.BlockSpec((128, tk), lambda h, j: (h, j)),
                  pl.BlockSpec((tk, 128), lambda h, j: (j, A_FV // 128 + h // 2)),
                  pl.BlockSpec((S, 128), lambda h, j: (0, h // 2), pipeline_mode=once),
                  pl.BlockSpec((128, S), lambda h, j: (h // 2, 0), pipeline_mode=once), rows, rows],
        out_specs=[pl.BlockSpec((128, S), lambda h, j: (h, 0), pipeline_mode=once),
                   pl.BlockSpec((128, tk), lambda h, j: (h, j)), pl.BlockSpec((FOX_DH, tk), lambda h, j: (h, j))],
        out_shape=[jax.ShapeDtypeStruct((1024, S), F32), jax.ShapeDtypeStruct((1024, S), F32),
                   jax.ShapeDtypeStruct((FOX_W, S), F32)],
        compiler_params=_params("arbitrary", "arbitrary"),
    )(qa, qat, ka, kat, pa, dob, dobt, lse, delta)


def _fox_post(dq, dk, dv, ps, bfg):
    S = dq.shape[1]
    tm = _tile(S, 512)
    n_b = S // tm

    def body(dq_ref, dk_ref, dv_ref, ps_ref, b_ref, fq_ref, fk_ref, fv_ref, dff_ref, dbf_ref, carry):
        first = pl.program_id(0) == 0

        @pl.when(first)
        def _():
            carry[...] = jnp.zeros_like(carry)

        low = lax.broadcasted_iota(jnp.int32, (tm, 128), 1) < FOX_DH
        for h in range(FOX_H):
            blk = slice(128 * h, 128 * (h + 1))
            fq_ref[:, blk] = jnp.where(low, dq_ref[blk, :].T * FOX_SCALE, 0.0).astype(BF16)
            fk_ref[:, blk] = jnp.where(low, dk_ref[blk, :].T, 0.0).astype(BF16)
        fv_ref[...] = dv_ref[...].T.astype(BF16)
        rr = lax.broadcasted_iota(jnp.int32, (FOX_H, 1024), 0)
        cc = lax.broadcasted_iota(jnp.int32, (FOX_H, 1024), 1)
        sel_k = jnp.where(cc == 128 * rr + AUG + 3, 1.0, 0.0).astype(F32)
        sel_q = jnp.where(cc == 128 * rr + AUG, 1.0, 0.0).astype(F32)
        g = (jnp.dot(sel_k, dk_ref[...], preferred_element_type=F32, precision=HIGHEST)
             - jnp.dot(sel_q, dq_ref[...], preferred_element_type=F32, precision=HIGHEST))
        t_from = lax.broadcasted_iota(jnp.int32, (tm, tm), 0)
        t_to = lax.broadcasted_iota(jnp.int32, (tm, tm), 1)
        later = jnp.where(t_from >= t_to, 1.0, 0.0).astype(F32)
        dlf = jnp.dot(-g, later, preferred_element_type=F32, precision=HIGHEST) + carry[:, 0:1]
        carry[...] = jnp.broadcast_to(dlf[:, 0:1], carry.shape)
        cols = jnp.concatenate([jnp.zeros((FF_LANE, tm), F32), dlf, jnp.zeros((128 - FF_LANE - FOX_H, tm), F32)], axis=0).T
        dff = cols * jax.nn.sigmoid(-(ps_ref[...] + b_ref[...]))
        dff_ref[...] = dff
        part = _sum8(dff)

        @pl.when(first)
        def _():
            dbf_ref[...] = part

        @pl.when(jnp.logical_not(first))
        def _():
            dbf_ref[...] += part

    rev = lambda i: (n_b - 1 - i, 0)
    wide = pl.BlockSpec((tm, 1024), rev)
    tall = lambda n: pl.BlockSpec((n, tm), lambda i: (0, n_b - 1 - i))
    return pl.pallas_call(
        body, name="fox_post", grid=(n_b,),
        in_specs=[tall(1024), tall(1024), tall(FOX_W), pl.BlockSpec((tm, 128), lambda i: (n_b - 1 - i, 4)),
                  pl.BlockSpec((1, 128), lambda i: (0, 0))],
        out_specs=[wide, wide, pl.BlockSpec((tm, FOX_W), rev), pl.BlockSpec((tm, 128), rev), pl.BlockSpec((8, 128), lambda i: (0, 0))],
        out_shape=[jax.ShapeDtypeStruct((S, 1024), BF16), jax.ShapeDtypeStruct((S, 1024), BF16),
                   jax.ShapeDtypeStruct((S, FOX_W), BF16), jax.ShapeDtypeStruct((S, 128), F32),
                   jax.ShapeDtypeStruct((8, 128), F32)],
        scratch_shapes=[pltpu.VMEM((8, 128), F32)],
        compiler_params=_params("arbitrary"),
    )(dq, dk, dv, ps, bfg)


def _mem_prep(mem, g_mem, wkv):
    def body(m_ref, g_ref, w_ref, mn_ref, kv_ref):
        r, xh = _rms(m_ref[...])
        mn = (xh * g_ref[...]).astype(BF16)
        mn_ref[...] = mn
        kv_ref[...] = _nn(mn, w_ref[...]).astype(BF16)

    return pl.pallas_call(
        body, name="mem_prep",
        out_shape=[jax.ShapeDtypeStruct((N_MEM, D), BF16), jax.ShapeDtypeStruct((N_MEM, 2 * MEM_W), BF16)],
        compiler_params=pltpu.CompilerParams(vmem_limit_bytes=V7X_VMEM_LIMIT),
    )(mem, g_mem, wkv)


def _mem_softmax(qh, kh):
    s = _nt(qh, kh) * MEM_SCALE
    e = jnp.exp(s - jnp.max(s, axis=-1, keepdims=True))
    return e / jnp.sum(e, axis=-1, keepdims=True)


def _mem_fwd(pa, mkv):
    S = pa.shape[0]
    tm = _tile(S, 512)

    def body(q_ref, kv_ref, o_ref):
        for h in range(MEM_H):
            cols = slice(MEM_DH * h, MEM_DH * (h + 1))
            p = _mem_softmax(q_ref[:, cols], kv_ref[:, cols])
            o_ref[:, cols] = _nn(p.astype(BF16), kv_ref[:, MEM_W + MEM_DH * h:MEM_W + MEM_DH * (h + 1)])

    return pl.pallas_call(
        body, name="mem_fwd", grid=(S // tm,),
        in_specs=[pl.BlockSpec((tm, MEM_W), lambda i: (i, A_MQ // MEM_W)), pl.BlockSpec((N_MEM, 2 * MEM_W), lambda i: (0, 0))],
        out_specs=pl.BlockSpec((tm, MEM_W), lambda i: (i, 0)),
        out_shape=jax.ShapeDtypeStruct((S, MEM_W), F32),
        compiler_params=_params("arbitrary"),
    )(pa, mkv)


def _mem_bwd(pa, mkv, d_o):
    S = pa.shape[0]
    tm = _tile(S, 512)

    def body(q_ref, kv_ref, do_ref, dq_ref, dkv_ref):
        first = pl.program_id(0) == 0
        parts = []
        for h in range(MEM_H):
            cols = slice(MEM_DH * h, MEM_DH * (h + 1))
            vcols = slice(MEM_W + MEM_DH * h, MEM_W + MEM_DH * (h + 1))
            qh, kh = q_ref[:, cols], kv_ref[:, cols]
            p = _mem_softmax(qh, kh)
            dob = do_ref[:, cols].astype(BF16)
            dp = _nt(dob, kv_ref[:, vcols])
            ds = (p * (dp - jnp.sum(p * dp, axis=-1, keepdims=True)) * MEM_SCALE).astype(BF16)
            dq_ref[:, cols] = _nn(ds, kh).astype(BF16)
            parts.append((cols, _tn(ds, qh)))
            parts.append((vcols, _tn(p.astype(BF16), dob)))

        @pl.when(first)
        def _():
            for sl, v in parts:
                dkv_ref[:, sl] = v

        @pl.when(jnp.logical_not(first))
        def _():
            for sl, v in parts:
                dkv_ref[:, sl] += v

    return pl.pallas_call(
        body, name="mem_bwd", grid=(S // tm,),
        in_specs=[pl.BlockSpec((tm, MEM_W), lambda i: (i, A_MQ // MEM_W)), pl.BlockSpec((N_MEM, 2 * MEM_W), lambda i: (0, 0)),
                  pl.BlockSpec((tm, MEM_W), lambda i: (i, 0))],
        out_specs=[pl.BlockSpec((tm, MEM_W), lambda i: (i, 0)), pl.BlockSpec((N_MEM, 2 * MEM_W), lambda i: (0, 0))],
        out_shape=[jax.ShapeDtypeStruct((S, MEM_W), BF16), jax.ShapeDtypeStruct((N_MEM, 2 * MEM_W), F32)],
        compiler_params=_params("arbitrary"),
    )(pa, mkv, d_o)


def _mem_prep_bwd(mem, g_mem, mn, wkv, dkv):
    def body(m_ref, g_ref, mn_ref, w_ref, d_ref, dw_ref, dg_ref):
        db = d_ref[...].astype(BF16)
        dw_ref[...] = _tn(mn_ref[...], db)
        r, xh = _rms(m_ref[...])
        dg_ref[...] = _sum8(_nt(db, w_ref[...]) * xh)

    return pl.pallas_call(
        body, name="mem_prep_bwd",
        out_shape=[jax.ShapeDtypeStruct((D, 2 * MEM_W), F32), jax.ShapeDtypeStruct((8, D), F32)],
        compiler_params=pltpu.CompilerParams(vmem_limit_bytes=V7X_VMEM_LIMIT),
    )(mem, g_mem, mn, wkv, dkv)


def _rearrange_w_in(w):
    def heads128(cols):
        blk = w[:, cols:cols + FOX_W].reshape(D, FOX_H, FOX_DH)
        return jnp.pad(blk, ((0, 0), (0, 0), (0, 128 - FOX_DH))).reshape(D, FOX_H * 128)

    wa = jnp.concatenate([w[:, O_GQ:O_GG], heads128(O_FQ), heads128(O_FK), w[:, O_FV:O_FF], w[:, O_MQ:O_GT]], axis=1)
    wg = w[:, O_GT:]
    ws = jnp.concatenate([w[:, O_GG:O_GA], w[:, O_GA:O_FQ], w[:, O_FF:O_MQ], jnp.zeros((D, 128 - GLA_R - FOX_H), w.dtype)], axis=1)
    wp = jnp.concatenate([wa, wg, ws, jnp.zeros((D, P_W - A_W - G_W - S_W), w.dtype)], axis=1)
    return wa, wg, ws, wp


def _restore_w_in_grad(dwp):
    def unheads(off):
        return dwp[:, off:off + FOX_H * 128].reshape(D, FOX_H, 128)[:, :, :FOX_DH].reshape(D, FOX_W)

    s0 = A_W + G_W
    return jnp.concatenate([
        dwp[:, 0:1024], dwp[:, s0:s0 + 512], dwp[:, s0 + 512:s0 + 512 + GLA_R], unheads(A_FQ), unheads(A_FK),
        dwp[:, A_FV:A_MQ], dwp[:, s0 + 512 + GLA_R:s0 + 512 + GLA_R + FOX_H], dwp[:, A_MQ:A_W], dwp[:, A_W:A_W + G_W]], axis=1)


def _local_step(x, mem, target, p):
    S = x.shape[0]
    wa, wg, ws, wp = _rearrange_w_in(p["w_in"])
    wau = jnp.pad(p["w_alpha_up"], ((0, 128 - GLA_R), (0, 0)))
    bfg = jnp.pad(p["b_forget"], ((0, 0), (FF_LANE, 128 - FF_LANE - FOX_H)))
    gh = p["g_gla_head"].reshape(1, GLA_V)

    pa, u = _proj(x, p["g_mix"], wa, BF16, "proj_a", True)
    (pg,) = _proj(x, p["g_mix"], wg, F32, "proj_g", False)
    (ps,) = _proj(x, p["g_mix"], ws, F32, "proj_s", False)
    o_gla, og, sprev = _gla_fwd(pa, ps, wau, p["b_alpha"], gh)
    qa, ka, qat, kat, vt = _fox_prep(pa, ps, bfg)
    o_fox, lse = _fox_fwd(qa, ka, vt)
    mn, mkv = _mem_prep(mem, p["g_mem"], p["w_mem_kv"])
    o_mem = _mem_fwd(pa, mkv)
    y3, mg = _merge(og, o_fox, o_mem, p["w_gla_o"], p["w_fox_o"], p["w_mem_o"], pg)
    h1, u2 = _out_proj(mg, p["w_out"], x, p["g_ffn"])
    a, act = _ff1(u2, p["w_ff1"])
    dh2, dh2b, loss8, dg_final = _ff2_loss(act, p["w_ff2"], h1, p["g_final"].reshape(1, D), target)

    d_a = _dact(dh2b, p["w_ff2"], a)
    dw_ff2 = _wgrad(act, dh2b, "wgrad_ff2")
    dh1, dh1b, dg_ffn = _nt_rmsbwd(d_a, p["w_ff1"], h1, p["g_ffn"], dh2, "dffn", True)
    dw_ff1 = _wgrad(u2, d_a, "wgrad_ff1")
    dy3, d_gates = _dmerge(dh1b, p["w_out"], pg, y3)
    dw_out = _wgrad(mg, dh1b, "wgrad_out")
    do3 = _dout3(dy3, p["w_gla_o"], p["w_fox_o"], p["w_mem_o"])
    dw_gla_o = _wgrad(og, dy3[0], "wgrad_gla_o")
    dw_fox_o = _wgrad(o_fox, dy3[1], "wgrad_fox_o")
    dw_mem_o = _wgrad(o_mem, dy3[2], "wgrad_mem_o")
    d_mq, d_mkv = _mem_bwd(pa, mkv, do3[2])
    dw_mem_kv, dg_mem = _mem_prep_bwd(mem, p["g_mem"], mn, p["w_mem_kv"], d_mkv)
    dob, dobt, delta = _fox_delta(do3[1], o_fox)
    dq, dk, dv = _fox_bwd(qa, qat, ka, kat, pa, dob, dobt, lse, delta)
    d_fq, d_fk, d_fv, dgaff_fox, db_forget = _fox_post(dq, dk, dv, ps, bfg)
    d_gq, d_gk, d_gv, d_gg, d_gaff, dw_au, db_alpha, dg_gla = _gla_bwd(pa, ps, wau, p["b_alpha"], gh, o_gla, do3[0], sprev, dgaff_fox)
    d_proj = jnp.concatenate([d_gq, d_gk, d_gv, d_fq, d_fk, d_fv, d_mq, d_gates, d_gg, d_gaff,
                              jnp.zeros((S, P_W - A_W - G_W - S_W), BF16)], axis=1)
    dx, dg_mix = _nt_rmsbwd(d_proj, wp, x, p["g_mix"], dh1, "dmix", False)
    dw_in = _restore_w_in_grad(_wgrad(u, d_proj, "wgrad_in"))

    big = dict(w_in=dw_in, w_mem_kv=dw_mem_kv, w_gla_o=dw_gla_o, w_fox_o=dw_fox_o, w_mem_o=dw_mem_o, w_out=dw_out,
               w_ff1=dw_ff1, w_ff2=dw_ff2)
    small = dict(g_mix=dg_mix, g_mem=dg_mem, g_ffn=dg_ffn, g_final=dg_final, b_alpha=db_alpha, g_gla_head=dg_gla,
                 b_forget=db_forget, w_alpha_up=dw_au, loss=loss8)
    return dx, big, small


BIG = (("w_in", 1), ("w_mem_kv", 0), ("w_gla_o", 1), ("w_fox_o", 1), ("w_mem_o", 1), ("w_out", 0), ("w_ff1", 1), ("w_ff2", 0))


def _peer(d):
    me = lax.axis_index("x") * 4 + lax.axis_index("y") * 2 + lax.axis_index("c")
    t = (me + d) % N_DEV
    return (t // 4, (t // 2) % 2, t % 2), me


def _exchange(blocks, name, mine):
    n = len(blocks)

    def body(*refs):
        ins, outs = refs[:n], refs[n:2 * n]
        send, recv, loc = refs[2 * n:]
        copies = []
        for k in range(n):
            _, me = _peer(0)
            own = pltpu.make_async_copy(ins[k] if mine else ins[k].at[me], outs[k].at[me], loc.at[k])
            own.start()
            copies.append(own)
        remote = []
        for d in range(1, N_DEV):
            to, me = _peer(d)
            for k in range(n):
                cp = pltpu.make_async_remote_copy(
                    src_ref=ins[k] if mine else ins[k].at[(me + d) % N_DEV], dst_ref=outs[k].at[me],
                    send_sem=send.at[k, d - 1], recv_sem=recv.at[k, d - 1], device_id=to, device_id_type=MESH)
                cp.start()
                remote.append(cp)
        for cp in remote:
            cp.wait_send()
        for cp in remote:
            cp.wait_recv()
        for cp in copies:
            cp.wait()

    any_spec = pl.BlockSpec(memory_space=pl.ANY)
    out_shape = [jax.ShapeDtypeStruct(((N_DEV,) + b.shape) if mine else b.shape, b.dtype) for b in blocks]
    return pl.pallas_call(
        body, name=name, in_specs=[any_spec] * n, out_specs=[any_spec] * n, out_shape=out_shape,
        scratch_shapes=[pltpu.SemaphoreType.DMA((n, N_DEV - 1)), pltpu.SemaphoreType.DMA((n, N_DEV - 1)),
                        pltpu.SemaphoreType.DMA((n,))],
    )(*blocks)


def _adamw_math(g, w, m, v):
    m2 = ADAM_B1 * m + (1.0 - ADAM_B1) * g
    v2 = ADAM_B2 * v + (1.0 - ADAM_B2) * jnp.square(g)
    m_hat = m2 / (1.0 - ADAM_B1 ** ADAM_STEP)
    v_hat = v2 / (1.0 - ADAM_B2 ** ADAM_STEP)
    delta = -ADAM_LR * (m_hat / (jnp.sqrt(v_hat) + ADAM_EPS) + ADAM_WD * w)
    return delta, m2, v2


def _adamw_sum(parts, w, m, v, name):
    R, C = w.shape
    tr = _tile(R, 128)

    def body(p_ref, w_ref, m_ref, v_ref, g_ref, d_ref, m2_ref, v2_ref):
        g = p_ref[0]
        for j in range(1, p_ref.shape[0]):
            g = g + p_ref[j]
        g_ref[...] = g
        d_ref[...], m2_ref[...], v2_ref[...] = _adamw_math(g, w_ref[...], m_ref[...], v_ref[...])

    blk = pl.BlockSpec((tr, C), lambda i: (i, 0))
    return pl.pallas_call(
        body, name=name, grid=(R // tr,),
        in_specs=[pl.BlockSpec((parts.shape[0], tr, C), lambda i: (0, i, 0)), blk, blk, blk],
        out_specs=[blk] * 4, out_shape=[jax.ShapeDtypeStruct((R, C), F32)] * 4,
        compiler_params=_params("arbitrary"),
    )(parts, w, m, v)


SMALL_ROWS = 24


def _pack_small(d):
    mixed = jnp.concatenate([d["b_alpha"].reshape(1, GLA_K), d["g_gla_head"].reshape(1, GLA_V),
                             jnp.pad(d["b_forget"].reshape(1, FOX_H), ((0, 0), (FF_LANE, 128 - FF_LANE - FOX_H))),
                             jnp.zeros((1, 128), F32)], axis=1)
    rows = [d["g_mix"].reshape(1, D), d["g_mem"].reshape(1, D), d["g_ffn"].reshape(1, D), d["g_final"].reshape(1, D), mixed,
            jnp.zeros((3, D), F32), jnp.pad(d["w_alpha_up"].reshape(GLA_R, GLA_K), ((0, 0), (0, D - GLA_K)))]
    return jnp.concatenate(rows, axis=0)


def _unpack_small(t):
    return dict(g_mix=t[0:1], g_mem=t[1:2], g_ffn=t[2:3], g_final=t[3], b_alpha=t[4:5, 0:GLA_K],
                g_gla_head=t[4:5, GLA_K:GLA_K + GLA_V].reshape(1, GLA_H, GLA_DV),
                b_forget=t[4:5, 768 + FF_LANE:768 + FF_LANE + FOX_H], w_alpha_up=t[8:24, 0:GLA_K].reshape(1, GLA_R, GLA_K))


def _small_allreduce(small, w, m, v):
    def body(gm, gme, gf, gfi, ba, gg, bf, wau, ls, w_ref, m_ref, v_ref, g_ref, d_ref, m2_ref, v2_ref, l_ref,
             buf, send, recv):
        _, me = _peer(0)
        buf[me] = jnp.zeros((SMALL_ROWS, D), F32)
        for r, ref in enumerate((gm, gme, gf, gfi)):
            buf[me, r:r + 1, :] = jnp.sum(ref[...], axis=0, keepdims=True)
        buf[me, 4:5, 0:GLA_K] = jnp.sum(ba[...], axis=0, keepdims=True)
        buf[me, 4:5, GLA_K:GLA_K + GLA_V] = jnp.sum(gg[...], axis=0, keepdims=True)
        buf[me, 4:5, 768:896] = jnp.sum(bf[...], axis=0, keepdims=True)
        lrow = jnp.sum(ls[...], axis=0, keepdims=True)
        lsum = lrow[:, 0:128]
        for c in range(1, D // 128):
            lsum = lsum + lrow[:, 128 * c:128 * (c + 1)]
        buf[me, 4:5, 896:1024] = lsum
        buf[me, 8:24, 0:GLA_K] = wau[0:GLA_R, :]
        remote = []
        for d in range(1, N_DEV):
            to, me = _peer(d)
            cp = pltpu.make_async_remote_copy(src_ref=buf.at[me], dst_ref=buf.at[me], send_sem=send.at[d - 1],
                                              recv_sem=recv.at[d - 1], device_id=to, device_id_type=MESH)
            cp.start()
            remote.append(cp)
        for cp in remote:
            cp.wait_send()
        for cp in remote:
            cp.wait_recv()
        g = buf[0]
        for j in range(1, N_DEV):
            g = g + buf[j]
        g_ref[...] = g
        d_ref[...], m2_ref[...], v2_ref[...] = _adamw_math(g, w_ref[...], m_ref[...], v_ref[...])
        l_ref[...] = g[4:5, 896:1024]

    packed = jax.ShapeDtypeStruct((SMALL_ROWS, D), F32)
    return pl.pallas_call(
        body, name="small_allreduce",
        out_shape=[packed, packed, packed, packed, jax.ShapeDtypeStruct((1, 128), F32)],
        scratch_shapes=[pltpu.VMEM((N_DEV, SMALL_ROWS, D), F32), pltpu.SemaphoreType.DMA((N_DEV - 1,)),
                        pltpu.SemaphoreType.DMA((N_DEV - 1,))],
    )(small["g_mix"], small["g_mem"], small["g_ffn"], small["g_final"], small["b_alpha"], small["g_gla_head"],
      small["b_forget"], small["w_alpha_up"], small["loss"], w, m, v)


def _slabs(g, axis):
    R, C = g.shape
    if axis == 0:
        return g.reshape(N_DEV, R // N_DEV, C)
    return g.reshape(R, N_DEV, C // N_DEV).transpose(1, 0, 2)


def _unslab(t, axis):
    n, r, c = t.shape
    if axis == 0:
        return t.reshape(n * r, c)
    return t.transpose(1, 0, 2).reshape(r, n * c)


def kernel(x, mem, g_mix, w_in, w_alpha_up, b_alpha, b_forget, g_gla_head, g_mem, w_mem_kv, w_gla_o, w_fox_o, w_mem_o, w_out, g_ffn, w_ff1, w_ff2, g_final, loss_target, m_g_mix, m_w_in, m_w_alpha_up, m_b_alpha, m_b_forget, m_g_gla_head, m_g_mem, m_w_mem_kv, m_w_gla_o, m_w_fox_o, m_w_mem_o, m_w_out, m_g_ffn, m_w_ff1, m_w_ff2, m_g_final, v_g_mix, v_w_in, v_w_alpha_up, v_b_alpha, v_b_forget, v_g_gla_head, v_g_mem, v_w_mem_kv, v_w_gla_o, v_w_fox_o, v_w_mem_o, v_w_out, v_g_ffn, v_w_ff1, v_w_ff2, v_g_final):
    names = ["g_mix", "w_in", "w_alpha_up", "b_alpha", "b_forget", "g_gla_head", "g_mem", "w_mem_kv", "w_gla_o", "w_fox_o",
             "w_mem_o", "w_out", "g_ffn", "w_ff1", "w_ff2", "g_final"]
    w = dict(g_mix=g_mix, w_in=w_in, w_alpha_up=w_alpha_up, b_alpha=b_alpha, b_forget=b_forget, g_gla_head=g_gla_head,
             g_mem=g_mem, w_mem_kv=w_mem_kv, w_gla_o=w_gla_o, w_fox_o=w_fox_o, w_mem_o=w_mem_o, w_out=w_out, g_ffn=g_ffn,
             w_ff1=w_ff1, w_ff2=w_ff2, g_final=g_final)
    m = dict(g_mix=m_g_mix, w_in=m_w_in, w_alpha_up=m_w_alpha_up, b_alpha=m_b_alpha, b_forget=m_b_forget,
             g_gla_head=m_g_gla_head, g_mem=m_g_mem, w_mem_kv=m_w_mem_kv, w_gla_o=m_w_gla_o, w_fox_o=m_w_fox_o,
             w_mem_o=m_w_mem_o, w_out=m_w_out, g_ffn=m_g_ffn, w_ff1=m_w_ff1, w_ff2=m_w_ff2, g_final=m_g_final)
    v = dict(g_mix=v_g_mix, w_in=v_w_in, w_alpha_up=v_w_alpha_up, b_alpha=v_b_alpha, b_forget=v_b_forget,
             g_gla_head=v_g_gla_head, g_mem=v_g_mem, w_mem_kv=v_w_mem_kv, w_gla_o=v_w_gla_o, w_fox_o=v_w_fox_o,
             w_mem_o=v_w_mem_o, w_out=v_w_out, g_ffn=v_g_ffn, w_ff1=v_w_ff1, w_ff2=v_w_ff2, g_final=v_g_final)
    me = lax.axis_index("x") * 4 + lax.axis_index("y") * 2 + lax.axis_index("c")

    shards = [w[n][0].astype(BF16) for n, _ in BIG] + [w["w_alpha_up"][0].astype(BF16)]
    gathered = _exchange(shards, "gather_weights", True)
    p = {n: _unslab(t, ax) for (n, ax), t in zip(BIG, gathered[:-1])}
    p["w_alpha_up"] = _unslab(gathered[-1], 1)
    p.update(g_mix=g_mix, b_alpha=b_alpha, b_forget=b_forget, g_gla_head=g_gla_head, g_mem=g_mem, g_ffn=g_ffn, g_final=g_final)

    dx, big, small = _local_step(x[0], mem[0], loss_target[0], p)

    recv = _exchange([_slabs(big[n], ax) for n, ax in BIG], "scatter_grads", False)
    out_g, out_d, out_m, out_v = {}, {}, {}, {}
    for (n, ax), parts in zip(BIG, recv):
        g_, d_, m_, v_ = _adamw_sum(parts, w[n][0], m[n][0], v[n][0], "adamw_" + n)
        out_g[n], out_d[n], out_m[n], out_v[n] = g_[None], d_[None], m_[None], v_[None]

    full = lambda d: dict(d, w_alpha_up=jnp.zeros((1, GLA_R, GLA_K), F32))
    gs, ds, ms, vs, lrow = _small_allreduce(small, _pack_small(full(w)), _pack_small(full(m)), _pack_small(full(v)))
    g_s, d_s, m_s, v_s = _unpack_small(gs), _unpack_small(ds), _unpack_small(ms), _unpack_small(vs)
    for n in names:
        if n not in out_g and n != "w_alpha_up":
            out_g[n], out_d[n], out_m[n], out_v[n] = g_s[n], d_s[n], m_s[n], v_s[n]
    g_au = lax.dynamic_slice_in_dim(g_s["w_alpha_up"][0], me * (GLA_K // N_DEV), GLA_K // N_DEV, axis=1)
    g_, d_, m_, v_ = _adamw_sum(g_au[None], w_alpha_up[0], m_w_alpha_up[0], v_w_alpha_up[0], "adamw_w_alpha_up")
    out_g["w_alpha_up"], out_d["w_alpha_up"], out_m["w_alpha_up"], out_v["w_alpha_up"] = g_[None], d_[None], m_[None], v_[None]

    loss = jnp.sum(lrow) * (0.5 / D)
    return (loss, dx[None], *[out_g[n] for n in names], *[out_d[n] for n in names], *[out_m[n] for n in names],
            *[out_v[n] for n in names])
```

```python
import jax
import jax.numpy as jnp
from jax import lax
from jax.experimental import pallas as pl
from jax.experimental.pallas import tpu as pltpu

F32, BF16 = jnp.float32, jnp.bfloat16
HIGHEST = lax.Precision.HIGHEST
MESH = pl.DeviceIdType.MESH

N_DEV = 8
D = 1024
EPS = 1e-6
CHUNK = 64
N_MEM = 256
GLA_H, GLA_DK, GLA_DV = 4, 64, 128
GLA_K, GLA_V, GLA_R = 256, 512, 16
FOX_H, FOX_DH, FOX_W = 8, 64, 512
MEM_H, MEM_DH, MEM_W = 4, 128, 512
D_FF = 4096
D_IN = 6680
FOX_SCALE = 0.125
GLA_SCALE = 0.125
MEM_SCALE = MEM_DH ** -0.5
GLA_TAU_INV = 1.0 / 16.0
NEG = -1e30

O_GQ, O_GK, O_GV, O_GG, O_GA, O_FQ, O_FK, O_FV, O_FF, O_MQ, O_GT = 0, 256, 512, 1024, 1536, 1552, 2064, 2576, 3088, 3096, 3608
A_FQ, A_FK, A_FV, A_MQ, A_W = 1024, 2048, 3072, 3584, 4096
S_W = 640
G_W = 3072
P_FOX, P_FOX_W, P_MQ, P_GT, P_GLA, P_GLA_W, P_W = 0, 2560, 2560, 3072, 6144, 2048, 8192
FF_LANE = 16
AUG = 64

ADAM_LR, ADAM_B1, ADAM_B2, ADAM_EPS, ADAM_WD, ADAM_STEP = 0.001, 0.9, 0.999, 1e-08, 0.01, 10
V7X_VMEM_LIMIT = 48 * 1024 * 1024
FOX_TQ, FOX_TK = 2048, 512


def _params(*sem):
    return pltpu.CompilerParams(dimension_semantics=sem, vmem_limit_bytes=V7X_VMEM_LIMIT)


def _nt(a, b):
    return lax.dot_general(a, b, (((1,), (1,)), ((), ())), preferred_element_type=F32)


def _tn(a, b):
    return lax.dot_general(a, b, (((0,), (0,)), ((), ())), preferred_element_type=F32)


def _nn(a, b):
    return jnp.dot(a, b, preferred_element_type=F32)


def _log_sigmoid(z):
    return jnp.minimum(z, 0.0) - jnp.log(1.0 + jnp.exp(-jnp.abs(z)))


def _sum8(x):
    return x.reshape(x.shape[0] // 8, 8, x.shape[1]).sum(axis=0)


def _rms(xv):
    r = lax.rsqrt(jnp.mean(xv * xv, axis=-1, keepdims=True) + EPS)
    return r, xv * r


def _rms_bwd(du, g, r, xh):
    w = du * g
    return r * (w - xh * jnp.mean(w * xh, axis=-1, keepdims=True))


def _row_chunks(n, size=256):
    return [slice(r, r + min(size, n)) for r in range(0, n, min(size, n))]


def _tile(n, pref):
    t = min(n, pref)
    assert n % t == 0, (n, t)
    return t


def _proj(x, g, wa, wg, ws):
    S = x.shape[0]
    tm, tn = _tile(S, 1024), 1024
    n_a, n_g = A_W // tn, G_W // tn

    def body(x_ref, g_ref, wa_ref, wg_ref, ws_ref, pa_ref, pg_ref, ps_ref, u_ref, u_s):
        j = pl.program_id(1)

        @pl.when(j == 0)
        def _():
            r, xh = _rms(x_ref[...])
            u_s[...] = (xh * g_ref[...]).astype(BF16)
            u_ref[...] = u_s[...]

        @pl.when(j < n_a)
        def _():
            pa_ref[...] = _nn(u_s[...], wa_ref[...]).astype(BF16)

        @pl.when((j >= n_a) & (j < n_a + n_g))
        def _():
            pg_ref[...] = _nn(u_s[...], wg_ref[...])

        @pl.when(j == n_a + n_g)
        def _():
            ps_ref[...] = _nn(u_s[...], ws_ref[...])

    in_a = lambda j: jnp.minimum(j, n_a - 1)
    in_g = lambda j: jnp.clip(j - n_a, 0, n_g - 1)
    row = pl.BlockSpec((tm, D), lambda i, j: (i, 0))
    return pl.pallas_call(
        body, name="proj", grid=(S // tm, n_a + n_g + 1),
        in_specs=[row, pl.BlockSpec((1, D), lambda i, j: (0, 0)), pl.BlockSpec((D, tn), lambda i, j: (0, in_a(j))),
                  pl.BlockSpec((D, tn), lambda i, j: (0, in_g(j))),
                  pl.BlockSpec((D, S_W), lambda i, j: (0, 0), pipeline_mode=pl.Buffered(1))],
        out_specs=[pl.BlockSpec((tm, tn), lambda i, j: (i, in_a(j))), pl.BlockSpec((tm, tn), lambda i, j: (i, in_g(j))),
                   pl.BlockSpec((tm, S_W), lambda i, j: (i, 0)), row],
        out_shape=[jax.ShapeDtypeStruct((S, A_W), BF16), jax.ShapeDtypeStruct((S, G_W), F32),
                   jax.ShapeDtypeStruct((S, S_W), F32), jax.ShapeDtypeStruct((S, D), BF16)],
        scratch_shapes=[pltpu.VMEM((tm, D), BF16)],
        compiler_params=_params("arbitrary", "arbitrary"),
    )(x, g, wa, wg, ws)


def _wgrad(a, b, name, slab_axis=None):
    S, Ka = a.shape
    N = b.shape[1]
    tka, ts = _tile(Ka, 1024), _tile(S, 1024)
    tn = N // N_DEV if slab_axis == 1 else _tile(N, 1024)
    n_s = S // ts

    def body(a_ref, b_ref, o_ref, acc):
        s = pl.program_id(2)

        @pl.when(s == 0)
        def _():
            acc[...] = jnp.zeros_like(acc)

        acc[...] += _tn(a_ref[...].astype(BF16), b_ref[...].astype(BF16))

        @pl.when(s == n_s - 1)
        def _():
            o_ref[...] = acc[...].astype(o_ref.dtype)

    if slab_axis == 1:
        out_spec = pl.BlockSpec((None, tka, tn), lambda i, j, s: (j, i, 0))
        out_shape = jax.ShapeDtypeStruct((N_DEV, Ka, tn), BF16)
    else:
        out_spec = pl.BlockSpec((tka, tn), lambda i, j, s: (i, j))
        out_shape = jax.ShapeDtypeStruct((Ka, N), F32 if slab_axis is None else BF16)
    out = pl.pallas_call(
        body, name=name, grid=(Ka // tka, N // tn, n_s),
        in_specs=[pl.BlockSpec((ts, tka), lambda i, j, s: (s, i)), pl.BlockSpec((ts, tn), lambda i, j, s: (s, j))],
        out_specs=out_spec, out_shape=out_shape,
        scratch_shapes=[pltpu.VMEM((tka, tn), F32)],
        compiler_params=_params("arbitrary", "arbitrary", "arbitrary"),
    )(a, b)
    return out.reshape(N_DEV, Ka // N_DEV, N) if slab_axis == 0 else out


def _nt_rmsbwd(a, w, xin, g, dres, name, emit_bf16):
    S, K = a.shape
    tm, tk = _tile(S, 1024), _tile(K, 1024)
    n_k = K // tk

    def body(a_ref, w_ref, x_ref, g_ref, r_ref, o_ref, *rest):
        dg_ref, acc = rest[-2], rest[-1]
        i, k = pl.program_id(0), pl.program_id(1)

        @pl.when(k == 0)
        def _():
            acc[...] = jnp.zeros_like(acc)

        acc[...] += _nt(a_ref[...], w_ref[...])

        @pl.when(k == n_k - 1)
        def _():
            @pl.when(i == 0)
            def _():
                dg_ref[...] = jnp.zeros_like(dg_ref)

            for rows in _row_chunks(tm):
                du = acc[rows, :]
                r, xh = _rms(x_ref[rows, :])
                out = r_ref[rows, :] + _rms_bwd(du, g_ref[...], r, xh)
                o_ref[rows, :] = out
                if emit_bf16:
                    rest[0][rows, :] = out.astype(BF16)
                dg_ref[...] += _sum8(du * xh)

    row = pl.BlockSpec((tm, D), lambda i, k: (i, 0))
    out_shape = [jax.ShapeDtypeStruct((S, D), F32)]
    out_specs = [row]
    if emit_bf16:
        out_shape.append(jax.ShapeDtypeStruct((S, D), BF16))
        out_specs.append(row)
    out_shape.append(jax.ShapeDtypeStruct((8, D), F32))
    out_specs.append(pl.BlockSpec((8, D), lambda i, k: (0, 0)))
    return pl.pallas_call(
        body, name=name, grid=(S // tm, n_k),
        in_specs=[pl.BlockSpec((tm, tk), lambda i, k: (i, k)), pl.BlockSpec((D, tk), lambda i, k: (0, k)),
                  row, pl.BlockSpec((1, D), lambda i, k: (0, 0)), row],
        out_specs=out_specs, out_shape=out_shape,
        scratch_shapes=[pltpu.VMEM((tm, D), F32)],
        compiler_params=_params("arbitrary", "arbitrary"),
    )(a, w, xin, g, dres)


def _merge(og, ofox, omem, wg, wf, wm, pg):
    S = og.shape[0]
    tm = _tile(S, 256)

    def body(og_ref, of_ref, om_ref, wg_ref, wf_ref, wm_ref, pg_ref, y_ref, mg_ref):
        tot = None
        for i, (o_ref, w_ref) in enumerate(((og_ref, wg_ref), (of_ref, wf_ref), (om_ref, wm_ref))):
            y = _nn(o_ref[...].astype(BF16), w_ref[...])
            y_ref[i] = y
            t = jax.nn.sigmoid(pg_ref[:, D * i:D * (i + 1)]) * y
            tot = t if tot is None else tot + t
        mg_ref[...] = tot.astype(BF16)

    o_spec = pl.BlockSpec((tm, 512), lambda i: (i, 0))
    w_spec = pl.BlockSpec((512, D), lambda i: (0, 0))
    return pl.pallas_call(
        body, name="merge", grid=(S // tm,),
        in_specs=[o_spec, o_spec, o_spec, w_spec, w_spec, w_spec, pl.BlockSpec((tm, G_W), lambda i: (i, 0))],
        out_specs=[pl.BlockSpec((3, tm, D), lambda i: (0, i, 0)), pl.BlockSpec((tm, D), lambda i: (i, 0))],
        out_shape=[jax.ShapeDtypeStruct((3, S, D), F32), jax.ShapeDtypeStruct((S, D), BF16)],
        compiler_params=_params("arbitrary"),
    )(og, ofox, omem, wg, wf, wm, pg)


def _out_proj(mg, w_out, x, g_ffn):
    S = x.shape[0]
    tm = _tile(S, 512)

    def body(mg_ref, w_ref, x_ref, g_ref, h_ref, u_ref):
        h = x_ref[...] + _nn(mg_ref[...], w_ref[...])
        h_ref[...] = h
        r, xh = _rms(h)
        u_ref[...] = (xh * g_ref[...]).astype(BF16)

    row = pl.BlockSpec((tm, D), lambda i: (i, 0))
    return pl.pallas_call(
        body, name="out_proj", grid=(S // tm,),
        in_specs=[row, pl.BlockSpec((D, D), lambda i: (0, 0)), row, pl.BlockSpec((1, D), lambda i: (0, 0))],
        out_specs=[row, row],
        out_shape=[jax.ShapeDtypeStruct((S, D), F32), jax.ShapeDtypeStruct((S, D), BF16)],
        compiler_params=_params("arbitrary"),
    )(mg, w_out, x, g_ffn)


def _ff1(u2, w1):
    S = u2.shape[0]
    tm, tn = _tile(S, 1024), 1024

    def body(u_ref, w_ref, a_ref, act_ref):
        a = _nn(u_ref[...], w_ref[...])
        a_ref[...] = a.astype(BF16)
        act_ref[...] = jnp.square(jnp.maximum(a, 0.0)).astype(BF16)

    blk = pl.BlockSpec((tm, tn), lambda i, j: (i, j))
    return pl.pallas_call(
        body, name="ff1", grid=(S // tm, D_FF // tn),
        in_specs=[pl.BlockSpec((tm, D), lambda i, j: (i, 0)), pl.BlockSpec((D, tn), lambda i, j: (0, j))],
        out_specs=[blk, blk],
        out_shape=[jax.ShapeDtypeStruct((S, D_FF), BF16), jax.ShapeDtypeStruct((S, D_FF), BF16)],
        compiler_params=_params("arbitrary", "arbitrary"),
    )(u2, w1)


def _ff2_loss(act, w2, h1, g_final, target):
    S = act.shape[0]
    tm, tk = _tile(S, 1024), 1024
    n_k = D_FF // tk

    def body(a_ref, w_ref, h_ref, g_ref, t_ref, d_ref, db_ref, ls_ref, dg_ref, acc):
        i, k = pl.program_id(0), pl.program_id(1)

        @pl.when(k == 0)
        def _():
            acc[...] = jnp.zeros_like(acc)

        acc[...] += _nn(a_ref[...], w_ref[...])

        @pl.when(k == n_k - 1)
        def _():
            @pl.when(i == 0)
            def _():
                ls_ref[...] = jnp.zeros_like(ls_ref)
                dg_ref[...] = jnp.zeros_like(dg_ref)

            gf = g_ref[...]
            for rows in _row_chunks(tm):
                r, xh = _rms(h_ref[rows, :] + acc[rows, :])
                err = xh * gf - t_ref[rows, :]
                dy = err * (1.0 / D)
                dh = _rms_bwd(dy, gf, r, xh)
                d_ref[rows, :] = dh
                db_ref[rows, :] = dh.astype(BF16)
                ls_ref[...] += _sum8(err * err)
                dg_ref[...] += _sum8(dy * xh)

    row = pl.BlockSpec((tm, D), lambda i, k: (i, 0))
    part = pl.BlockSpec((8, D), lambda i, k: (0, 0))
    return pl.pallas_call(
        body, name="ff2_loss", grid=(S // tm, n_k),
        in_specs=[pl.BlockSpec((tm, tk), lambda i, k: (i, k)), pl.BlockSpec((tk, D), lambda i, k: (k, 0)),
                  row, pl.BlockSpec((1, D), lambda i, k: (0, 0)), row],
        out_specs=[row, row, part, part],
        out_shape=[jax.ShapeDtypeStruct((S, D), F32), jax.ShapeDtypeStruct((S, D), BF16),
                   jax.ShapeDtypeStruct((8, D), F32), jax.ShapeDtypeStruct((8, D), F32)],
        scratch_shapes=[pltpu.VMEM((tm, D), F32)],
        compiler_params=_params("arbitrary", "arbitrary"),
    )(act, w2, h1, g_final, target)


def _dact(dh2b, w2, a):
    S = a.shape[0]
    tm, tn = _tile(S, 1024), 1024

    def body(d_ref, w_ref, a_ref, o_ref):
        da = _nt(d_ref[...], w_ref[...])
        o_ref[...] = (da * (2.0 * jnp.maximum(a_ref[...].astype(F32), 0.0))).astype(BF16)

    blk = pl.BlockSpec((tm, tn), lambda i, j: (i, j))
    return pl.pallas_call(
        body, name="dact", grid=(S // tm, D_FF // tn),
        in_specs=[pl.BlockSpec((tm, D), lambda i, j: (i, 0)), pl.BlockSpec((tn, D), lambda i, j: (j, 0)), blk],
        out_specs=blk, out_shape=jax.ShapeDtypeStruct((S, D_FF), BF16),
        compiler_params=_params("arbitrary", "arbitrary"),
    )(dh2b, w2, a)


def _dmerge(dh1b, w_out, pg, y3, wg, wf, wm):
    S = dh1b.shape[0]
    tm = _tile(S, 256)

    def body(d_ref, w_ref, pg_ref, y_ref, wg_ref, wf_ref, wm_ref, *outs):
        dy_refs, do_refs, dg_ref = outs[0:3], outs[3:6], outs[6]
        dm = _nt(d_ref[...], w_ref[...])
        for i, wo_ref in enumerate((wg_ref, wf_ref, wm_ref)):
            gt = jax.nn.sigmoid(pg_ref[:, D * i:D * (i + 1)])
            dy = (dm * gt).astype(BF16)
            dy_refs[i][...] = dy
            do_refs[i][...] = _nt(dy, wo_ref[...])
            dg_ref[:, D * i:D * (i + 1)] = (dm * y_ref[i] * (gt * (1.0 - gt))).astype(BF16)

    row = pl.BlockSpec((tm, D), lambda i: (i, 0))
    half = pl.BlockSpec((tm, 512), lambda i: (i, 0))
    w_spec = pl.BlockSpec((512, D), lambda i: (0, 0))
    return pl.pallas_call(
        body, name="dmerge", grid=(S // tm,),
        in_specs=[row, pl.BlockSpec((D, D), lambda i: (0, 0)), pl.BlockSpec((tm, G_W), lambda i: (i, 0)),
                  pl.BlockSpec((3, tm, D), lambda i: (0, i, 0)), w_spec, w_spec, w_spec],
        out_specs=[row, row, row, half, half, half, pl.BlockSpec((tm, G_W), lambda i: (i, P_GT // G_W))],
        out_shape=[jax.ShapeDtypeStruct((S, D), BF16)] * 3 + [jax.ShapeDtypeStruct((S, 512), F32)] * 3
        + [jax.ShapeDtypeStruct((S, P_W), BF16)],
        compiler_params=_params("arbitrary"),
    )(dh1b, w_out, pg, y3, wg, wf, wm)


def _gla_block_terms(gq_ref, gk_ref, ps_ref, wau_ref, ba_ref, tb):
    gaff = ps_ref[:, 512:640]
    z = _nn(gaff.astype(BF16), wau_ref[...]) + ba_ref[...]
    la = _log_sigmoid(z) * GLA_TAU_INV
    rr = lax.broadcasted_iota(jnp.int32, (tb, tb), 0)
    cc = lax.broadcasted_iota(jnp.int32, (tb, tb), 1)
    same = jnp.right_shift(rr, 6) == jnp.right_shift(cc, 6)
    tri = jnp.where(same & (cc <= rr), 1.0, 0.0).astype(F32)
    ones = jnp.where(same, 1.0, 0.0).astype(F32)
    b = jnp.dot(tri, la, preferred_element_type=F32, precision=HIGHEST)
    bl = jnp.dot(ones, la, preferred_element_type=F32, precision=HIGHEST)
    e_pos, e_neg, e_last, dec = jnp.exp(b), jnp.exp(-b), jnp.exp(bl - b), jnp.exp(bl)
    q = gq_ref[...].astype(F32) * GLA_SCALE
    k = gk_ref[...].astype(F32)
    return dict(gaff=gaff, z=z, same=same, rr=rr, cc=cc, ones=ones, e_pos=e_pos, e_neg=e_neg, e_last=e_last, dec=dec,
                qp=q * e_pos, qn=q * e_neg, kn=k * e_neg, kp=k * e_pos, kd=k * e_last)


def _head_masked(x, store):
    lane = lax.broadcasted_iota(jnp.int32, x.shape, 1)
    for h in range(GLA_H):
        store[h] = jnp.where(jnp.right_shift(lane, 6) == h, x, 0.0).astype(BF16)


def _gla_fwd(pa, ps, wau, ba, gh):
    S = pa.shape[0]
    tb = _tile(S, 512)
    n_c = tb // CHUNK

    def body(gq_ref, gk_ref, gv_ref, ps_ref, wau_ref, ba_ref, gh_ref, o_ref, og_ref, sp_ref,
             qpm, qnm, kdm, kn_s, kp_s, dec_s, state):
        @pl.when(pl.program_id(0) == 0)
        def _():
            state[...] = jnp.zeros_like(state)

        t = _gla_block_terms(gq_ref, gk_ref, ps_ref, wau_ref, ba_ref, tb)
        _head_masked(t["qp"], qpm)
        _head_masked(t["qn"], qnm)
        _head_masked(t["kd"], kdm)
        kn_s[...] = t["kn"].astype(BF16)
        kp_s[...] = t["kp"].astype(BF16)
        dec_s[...] = t["dec"]
        lower = lax.broadcasted_iota(jnp.int32, (CHUNK, CHUNK), 0) >= lax.broadcasted_iota(jnp.int32, (CHUNK, CHUNK), 1)

        def chunk(c, carry):
            r0 = pl.multiple_of(c * CHUNK, CHUNK)
            rows = pl.ds(r0, CHUNK)
            sp = state[...]
            sp_ref[c] = sp
            spb = sp.astype(BF16)
            knc, kpc = kn_s[rows, :], kp_s[rows, :]
            new = sp * dec_s[pl.ds(r0, 1), :]
            for h in range(GLA_H):
                cols = slice(GLA_DV * h, GLA_DV * (h + 1))
                qpc, qnc = qpm[h, rows, :], qnm[h, rows, :]
                attn = jnp.where(lower, _nt(qpc, knc), _nt(qnc, kpc)).astype(BF16)
                vh = gv_ref[rows, cols]
                o_ref[rows, cols] = _nn(attn, vh) + _nt(qpc, spb)
                new = new + _tn(vh, kdm[h, rows, :])
            state[...] = new
            return carry

        lax.fori_loop(0, n_c, chunk, 0)
        for h in range(GLA_H):
            cols = slice(GLA_DV * h, GLA_DV * (h + 1))
            r, xh = _rms(o_ref[:, cols])
            gg = ps_ref[:, cols]
            og_ref[:, cols] = ((xh * gh_ref[:, cols]) * (gg * jax.nn.sigmoid(gg))).astype(BF16)

    return pl.pallas_call(
        body, name="gla_fwd", grid=(S // tb,),
        in_specs=[pl.BlockSpec((tb, GLA_K), lambda i: (i, 0)), pl.BlockSpec((tb, GLA_K), lambda i: (i, 1)),
                  pl.BlockSpec((tb, GLA_V), lambda i: (i, 1)), pl.BlockSpec((tb, S_W), lambda i: (i, 0)),
                  pl.BlockSpec((128, GLA_K), lambda i: (0, 0)), pl.BlockSpec((1, GLA_K), lambda i: (0, 0)),
                  pl.BlockSpec((1, GLA_V), lambda i: (0, 0))],
        out_specs=[pl.BlockSpec((tb, GLA_V), lambda i: (i, 0)), pl.BlockSpec((tb, GLA_V), lambda i: (i, 0)),
                   pl.BlockSpec((n_c, GLA_DV, GLA_K), lambda i: (i, 0, 0))],
        out_shape=[jax.ShapeDtypeStruct((S, GLA_V), F32), jax.ShapeDtypeStruct((S, GLA_V), BF16),
                   jax.ShapeDtypeStruct((S // CHUNK, GLA_DV, GLA_K), F32)],
        scratch_shapes=[pltpu.VMEM((GLA_H, tb, GLA_K), BF16), pltpu.VMEM((GLA_H, tb, GLA_K), BF16),
                        pltpu.VMEM((GLA_H, tb, GLA_K), BF16), pltpu.VMEM((tb, GLA_K), BF16), pltpu.VMEM((tb, GLA_K), BF16),
                        pltpu.VMEM((tb, GLA_K), F32), pltpu.VMEM((GLA_DV, GLA_K), F32)],
        compiler_params=_params("arbitrary"),
    )(pa, pa, pa, ps, wau, ba, gh)


def _gla_bwd(pa, ps, wau, ba, gh, o_gla, d_og, sprev, dgaff_fox, d_proj):
    S = pa.shape[0]
    tb = _tile(S, 512)
    n_c = tb // CHUNK
    n_b = S // tb
    c_gk, c_gv, c_gg, c_ga, c_end = GLA_K, 2 * GLA_K, 2 * GLA_K + GLA_V, 2 * GLA_K + 2 * GLA_V, 2 * GLA_K + 2 * GLA_V + 128

    def body(gq_ref, gk_ref, gv_ref, ps_ref, wau_ref, ba_ref, gh_ref, o_ref, dog_ref, sp_ref, dfx_ref, _,
             dp_ref, dwau_ref, dba_ref, dgh_ref,
             qpm, qnm, kdm, kn_s, kp_s, dec_s, do_s, dqp_s, dqn_s, dkn_s, dkp_s, dkd_s, ddec_s, dstate):
        first = pl.program_id(0) == 0
        dp_ref[:, c_end:] = jnp.zeros((tb, P_GLA_W - c_end), BF16)

        @pl.when(first)
        def _():
            dstate[...] = jnp.zeros_like(dstate)

        t = _gla_block_terms(gq_ref, gk_ref, ps_ref, wau_ref, ba_ref, tb)
        _head_masked(t["qp"], qpm)
        _head_masked(t["qn"], qnm)
        _head_masked(t["kd"], kdm)
        kn_s[...] = t["kn"].astype(BF16)
        kp_s[...] = t["kp"].astype(BF16)
        dec_s[...] = t["dec"]

        dgh_parts = []
        for h in range(GLA_H):
            cols = slice(GLA_DV * h, GLA_DV * (h + 1))
            r, xh = _rms(o_ref[:, cols])
            g = gh_ref[:, cols]
            gg = ps_ref[:, cols]
            sg = jax.nn.sigmoid(gg)
            d_out = dog_ref[:, cols]
            dp_ref[:, c_gg + GLA_DV * h:c_gg + GLA_DV * (h + 1)] = (d_out * (xh * g) * (sg * (1.0 + gg * (1.0 - sg)))).astype(BF16)
            d_on = d_out * (gg * sg)
            dgh_parts.append(_sum8(d_on * xh))
            do_s[:, cols] = _rms_bwd(d_on, g, r, xh).astype(BF16)
        dgh_part = jnp.concatenate(dgh_parts, axis=1)

        lower = lax.broadcasted_iota(jnp.int32, (CHUNK, CHUNK), 0) >= lax.broadcasted_iota(jnp.int32, (CHUNK, CHUNK), 1)
        lane = lax.broadcasted_iota(jnp.int32, (CHUNK, GLA_K), 1)

        def chunk(j, carry):
            c = n_c - 1 - j
            r0 = pl.multiple_of(c * CHUNK, CHUNK)
            rows = pl.ds(r0, CHUNK)
            ds_next = dstate[...]
            dsb = ds_next.astype(BF16)
            sp = sp_ref[c]
            spb = sp.astype(BF16)
            knc, kpc = kn_s[rows, :], kp_s[rows, :]
            dec_row = dec_s[pl.ds(r0, 1), :]
            ddec_s[rows, :] = jnp.broadcast_to(jnp.sum(ds_next * sp, axis=0, keepdims=True), (CHUNK, GLA_K))
            new = ds_next * dec_row
            dqp = jnp.zeros((CHUNK, GLA_K), F32)
            dqn, dkn, dkp, dkd = dqp, dqp, dqp, dqp
            for h in range(GLA_H):
                cols = slice(GLA_DV * h, GLA_DV * (h + 1))
                mine = jnp.right_shift(lane, 6) == h
                qpc, qnc, kdc = qpm[h, rows, :], qnm[h, rows, :], kdm[h, rows, :]
                vh = gv_ref[rows, cols]
                doh = do_s[rows, cols]
                attn = jnp.where(lower, _nt(qpc, knc), _nt(qnc, kpc)).astype(BF16)
                da = _nt(doh, vh)
                dac = jnp.where(lower, da, 0.0).astype(BF16)
                daa = jnp.where(lower, 0.0, da).astype(BF16)
                dqp = dqp + jnp.where(mine, _nn(dac, knc) + _nn(doh, spb), 0.0)
                dqn = dqn + jnp.where(mine, _nn(daa, kpc), 0.0)
                dkn = dkn + _tn(dac, qpc)
                dkp = dkp + _tn(daa, qnc)
                dkd = dkd + jnp.where(mine, _nn(vh, dsb), 0.0)
                dp_ref[rows, c_gv + GLA_DV * h:c_gv + GLA_DV * (h + 1)] = (_tn(attn, doh) + _nt(kdc, dsb)).astype(BF16)
                new = new + _tn(doh, qpc)
            dqp_s[rows, :] = dqp
            dqn_s[rows, :] = dqn
            dkn_s[rows, :] = dkn
            dkp_s[rows, :] = dkp
            dkd_s[rows, :] = dkd
            dstate[...] = new
            return carry

        lax.fori_loop(0, n_c, chunk, 0)

        dqp, dqn, dkn, dkp, dkd = dqp_s[...], dqn_s[...], dkn_s[...], dkp_s[...], dkd_s[...]
        dp_ref[:, 0:c_gk] = ((dqp * t["e_pos"] + dqn * t["e_neg"]) * GLA_SCALE).astype(BF16)
        dp_ref[:, c_gk:c_gv] = (dkn * t["e_neg"] + dkp * t["e_pos"] + dkd * t["e_last"]).astype(BF16)
        kd_term = dkd * t["kd"]
        db = dqp * t["qp"] - dqn * t["qn"] - dkn * t["kn"] + dkp * t["kp"] - kd_term
        upper = jnp.where(t["same"] & (t["cc"] >= t["rr"]), 1.0, 0.0).astype(F32)
        dla = (jnp.dot(upper, db, preferred_element_type=F32, precision=HIGHEST)
               + jnp.dot(t["ones"], kd_term, preferred_element_type=F32, precision=HIGHEST)
               + ddec_s[...] * t["dec"])
        dz = dla * GLA_TAU_INV * jax.nn.sigmoid(-t["z"])
        dzb = dz.astype(BF16)
        dp_ref[:, c_ga:c_end] = (_nt(dzb, wau_ref[...]) + dfx_ref[...]).astype(BF16)
        dwau_part = _tn(t["gaff"].astype(BF16), dzb)
        dba_part = _sum8(dz)

        @pl.when(first)
        def _():
            dwau_ref[...] = dwau_part
            dba_ref[...] = dba_part
            dgh_ref[...] = dgh_part

        @pl.when(jnp.logical_not(first))
        def _():
            dwau_ref[...] += dwau_part
            dba_ref[...] += dba_part
            dgh_ref[...] += dgh_part

    rev = lambda i: (n_b - 1 - i, 0)
    f32k = pltpu.VMEM((tb, GLA_K), F32)
    bf4 = pltpu.VMEM((GLA_H, tb, GLA_K), BF16)
    return pl.pallas_call(
        body, name="gla_bwd", grid=(n_b,),
        in_specs=[pl.BlockSpec((tb, GLA_K), rev), pl.BlockSpec((tb, GLA_K), lambda i: (n_b - 1 - i, 1)),
                  pl.BlockSpec((tb, GLA_V), lambda i: (n_b - 1 - i, 1)), pl.BlockSpec((tb, S_W), rev),
                  pl.BlockSpec((128, GLA_K), lambda i: (0, 0)), pl.BlockSpec((1, GLA_K), lambda i: (0, 0)),
                  pl.BlockSpec((1, GLA_V), lambda i: (0, 0)), pl.BlockSpec((tb, GLA_V), rev), pl.BlockSpec((tb, GLA_V), rev),
                  pl.BlockSpec((n_c, GLA_DV, GLA_K), lambda i: (n_b - 1 - i, 0, 0)), pl.BlockSpec((tb, 128), rev),
                  pl.BlockSpec(memory_space=pl.ANY)],
        out_specs=[pl.BlockSpec((tb, P_GLA_W), lambda i: (n_b - 1 - i, P_GLA // P_GLA_W)),
                   pl.BlockSpec((128, GLA_K), lambda i: (0, 0)), pl.BlockSpec((8, GLA_K), lambda i: (0, 0)),
                   pl.BlockSpec((8, GLA_V), lambda i: (0, 0))],
        out_shape=[jax.ShapeDtypeStruct((S, P_W), BF16), jax.ShapeDtypeStruct((128, GLA_K), F32),
                   jax.ShapeDtypeStruct((8, GLA_K), F32), jax.ShapeDtypeStruct((8, GLA_V), F32)],
        input_output_aliases={11: 0},
        scratch_shapes=[bf4, bf4, bf4, pltpu.VMEM((tb, GLA_K), BF16), pltpu.VMEM((tb, GLA_K), BF16), f32k,
                        pltpu.VMEM((tb, GLA_V), BF16), f32k, f32k, f32k, f32k, f32k, f32k, pltpu.VMEM((GLA_DV, GLA_K), F32)],
        compiler_params=_params("arbitrary"),
    )(pa, pa, pa, ps, wau, ba, gh, o_gla, d_og, sprev, dgaff_fox, d_proj)


def _split3(x):
    x1 = x.astype(BF16).astype(F32)
    x2 = (x - x1).astype(BF16).astype(F32)
    x3 = (x - x1 - x2).astype(BF16).astype(F32)
    return x1, x2, x3


def _fox_prep(pa, ps, bfg):
    S = pa.shape[0]
    tm = _tile(S, 512)

    def body(ps_ref, b_ref, fq_ref, fk_ref, fv_ref, q_ref, k_ref, qt_ref, kt_ref, vt_ref, carry):
        @pl.when(pl.program_id(0) == 0)
        def _():
            carry[...] = jnp.zeros_like(carry)

        vt_ref[...] = fv_ref[...].astype(F32).T.astype(BF16)
        lf = _log_sigmoid(ps_ref[...] + b_ref[...])
        rr = lax.broadcasted_iota(jnp.int32, (tm, tm), 0)
        cc = lax.broadcasted_iota(jnp.int32, (tm, tm), 1)
        tri = jnp.where(cc <= rr, 1.0, 0.0).astype(F32)
        f = jnp.dot(tri, lf, preferred_element_type=F32, precision=HIGHEST) + carry[0:1, :]
        carry[...] = jnp.broadcast_to(f[tm - 1:tm, :], carry.shape)
        f1, f2, f3 = _split3(f)
        lane = lax.broadcasted_iota(jnp.int32, (tm, 128), 1)
        for h in range(FOX_H):
            cols = slice(128 * h, 128 * (h + 1))
            c = FF_LANE + h
            a1, a2, a3 = f1[:, c:c + 1], f2[:, c:c + 1], f3[:, c:c + 1]
            q = fq_ref[:, cols].astype(F32) * FOX_SCALE
            k = fk_ref[:, cols].astype(F32)
            for n, a in enumerate((a1, a2, a3)):
                q = jnp.where(lane == AUG + n, a, q)
                k = jnp.where(lane == AUG + 3 + n, -a, k)
            q = jnp.where((lane >= AUG + 3) & (lane < AUG + 6), 1.0, q)
            k = jnp.where((lane >= AUG) & (lane < AUG + 3), 1.0, k)
            q_ref[:, cols] = q.astype(BF16)
            k_ref[:, cols] = k.astype(BF16)
            qt_ref[cols, :] = q.T.astype(BF16)
            kt_ref[cols, :] = k.T.astype(BF16)

    wide = lambda j: pl.BlockSpec((tm, 1024), lambda i: (i, j))
    tall = lambda n: pl.BlockSpec((n, tm), lambda i: (0, i))
    return pl.pallas_call(
        body, name="fox_prep", grid=(S // tm,),
        in_specs=[pl.BlockSpec((tm, 128), lambda i: (i, 4)), pl.BlockSpec((1, 128), lambda i: (0, 0)), wide(1), wide(2),
                  pl.BlockSpec((tm, FOX_W), lambda i: (i, A_FV // FOX_W))],
        out_specs=[wide(0), wide(0), tall(1024), tall(1024), tall(FOX_W)],
        out_shape=[jax.ShapeDtypeStruct((S, 1024), BF16), jax.ShapeDtypeStruct((S, 1024), BF16),
                   jax.ShapeDtypeStruct((1024, S), BF16), jax.ShapeDtypeStruct((1024, S), BF16),
                   jax.ShapeDtypeStruct((FOX_W, S), BF16)],
        scratch_shapes=[pltpu.VMEM((8, 128), F32)],
        compiler_params=_params("arbitrary"),
    )(ps, bfg, pa, pa, pa)


def _fox_fwd(qa, ka, vt):
    S = qa.shape[0]
    tq = _tile(S, FOX_TQ)
    tk = _tile(tq, FOX_TK)
    n_sub = tq // tk

    def body(q_ref, k_ref, vt_ref, o_ref, lse_ref):
        i = pl.program_id(1)
        both = lambda f: tuple(f(hh) for hh in range(2))

        def blk(j, carry, diag):
            ks = pl.ds(pl.multiple_of(j * tk, tk), tk)
            q0 = 0 if diag is None else diag * tk

            def head(hh):
                m, l, acc = carry[hh]
                mo, lo, ao = m[:, q0:], l[:, q0:], acc[:, q0:]
                s = _nt(k_ref[ks, 128 * hh:128 * (hh + 1)], q_ref[q0:, 128 * hh:128 * (hh + 1)])
                if diag is not None:
                    live = lax.broadcasted_iota(jnp.int32, s.shape, 1) >= lax.broadcasted_iota(jnp.int32, s.shape, 0)
                    s = jnp.where(live, s, NEG)
                mn = jnp.maximum(mo, jnp.max(s, axis=0, keepdims=True))
                p = jnp.exp(s - mn)
                al = jnp.exp(mo - mn)
                ln = al * lo + jnp.sum(p, axis=0, keepdims=True)
                an = al * ao + _nn(vt_ref[FOX_DH * hh:FOX_DH * (hh + 1), ks], p.astype(BF16))
                if q0:
                    mn, ln, an = (jnp.concatenate([old[:, :q0], new], axis=1) for old, new in ((m, mn), (l, ln), (acc, an)))
                return mn, ln, an

            return both(head)

        one = (jnp.full((1, tq), NEG, F32), jnp.zeros((1, tq), F32), jnp.zeros((FOX_DH, tq), F32))
        past = i * n_sub
        carry = lax.fori_loop(0, past // 2, lambda jj, c: blk(2 * jj + 1, blk(2 * jj, c, None), None), (one, one))
        carry = lax.cond(past % 2 == 1, lambda c: blk(past - 1, c, None), lambda c: c, carry)
        for d in range(n_sub):
            carry = blk(past + d, carry, d)
        (m0, l0, a0), (m1, l1, a1) = carry
        o_ref[...] = jnp.concatenate([a0 / l0, a1 / l1], axis=0).T
        lse_ref[0, 0:1, :] = m0 + jnp.log(l0)
        lse_ref[0, 1:2, :] = m1 + jnp.log(l1)
        lse_ref[0, 2:8, :] = jnp.zeros((6, tq), F32)

    once = pl.Buffered(1)
    return pl.pallas_call(
        body, name="fox_fwd", grid=(FOX_H // 2, S // tq),
        in_specs=[pl.BlockSpec((tq, 256), lambda p, i: (i, p)),
                  pl.BlockSpec((S, 256), lambda p, i: (0, p), pipeline_mode=once),
                  pl.BlockSpec((128, S), lambda p, i: (p, 0), pipeline_mode=once)],
        out_specs=[pl.BlockSpec((tq, 128), lambda p, i: (i, p)), pl.BlockSpec((1, 8, tq), lambda p, i: (p, 0, i))],
        out_shape=[jax.ShapeDtypeStruct((S, FOX_W), F32), jax.ShapeDtypeStruct((FOX_H // 2, 8, S), F32)],
        compiler_params=_params("arbitrary", "arbitrary"),
    )(qa, ka, vt)


def _fox_delta(d_o, o):
    S = o.shape[0]
    tm = _tile(S, 512)

    def body(d_ref, o_ref, db_ref, dbt_ref, dl_ref):
        d = d_ref[...]
        db_ref[...] = d.astype(BF16)
        dbt_ref[...] = d.T.astype(BF16)
        prod = d * o_ref[...]
        rr = lax.broadcasted_iota(jnp.int32, (8, 128), 0)
        cc = lax.broadcasted_iota(jnp.int32, (8, 128), 1)
        ind = jnp.where(jnp.right_shift(cc, 6) == rr, 1.0, 0.0).astype(F32)
        for p in range(FOX_H // 2):
            dl_ref[p] = lax.dot_general(ind, prod[:, 128 * p:128 * (p + 1)], (((1,), (1,)), ((), ())),
                                        preferred_element_type=F32, precision=HIGHEST)

    row = pl.BlockSpec((tm, FOX_W), lambda i: (i, 0))
    return pl.pallas_call(
        body, name="fox_delta", grid=(S // tm,),
        in_specs=[row, row],
        out_specs=[row, pl.BlockSpec((FOX_W, tm), lambda i: (0, i)), pl.BlockSpec((FOX_H // 2, 8, tm), lambda i: (0, 0, i))],
        out_shape=[jax.ShapeDtypeStruct((S, FOX_W), BF16), jax.ShapeDtypeStruct((FOX_W, S), BF16),
                   jax.ShapeDtypeStruct((FOX_H // 2, 8, S), F32)],
        compiler_params=_params("arbitrary"),
    )(d_o, o)


def _fox_bwd(qa, qat, ka, kat, pa, dob, dobt, lse, delta):
    S = qa.shape[0]
    tk = _tile(S, 512)
    wide = _tile(S, FOX_TQ)
    ratio = wide // tk
    n_wide = S // wide

    def body(q_ref, qt_ref, k_ref, kt_ref, v_ref, do_ref, dot_ref, lse_ref, dl_ref, dq_ref, dk_ref, dv_ref):
        h, jb = pl.program_id(0), pl.program_id(1)
        hh = h % 2

        @pl.when(jb == 0)
        def _():
            dq_ref[...] = jnp.zeros_like(dq_ref)

        lane = lax.broadcasted_iota(jnp.int32, (tk, 128), 1)
        vm = jnp.where(jnp.right_shift(lane, 6) == hh, v_ref[...], jnp.zeros((), BF16))
        kb, ktb = k_ref[...], kt_ref[...]
        mine = pl.ds(pl.multiple_of(hh * FOX_DH, FOX_DH), FOX_DH)

        def blk(ib, tq, carry, masked):
            dk, dv = carry
            qs = pl.ds(pl.multiple_of(ib * tq, tq), tq)
            p = jnp.exp(_nt(kb, q_ref[qs, :]) - lse_ref[0, pl.ds(hh, 1), qs])
            if masked:
                live = lax.broadcasted_iota(jnp.int32, p.shape, 1) >= lax.broadcasted_iota(jnp.int32, p.shape, 0)
                p = jnp.where(live, p, 0.0)
            ds = (p * (_nt(vm, do_ref[qs, :]) - dl_ref[0, pl.ds(hh, 1), qs])).astype(BF16)
            dq_ref[:, qs] += _nn(ktb, ds)
            return dk + _nt(qt_ref[:, qs], ds), dv + _nt(dot_ref[mine, qs], p.astype(BF16))

        carry = blk(jb, tk, (jnp.zeros((128, tk), F32), jnp.zeros((FOX_DH, tk), F32)), True)
        first_wide = jb // ratio + 1
        carry = lax.fori_loop(jb + 1, jnp.minimum(first_wide * ratio, S // tk), lambda ib, c: blk(ib, tk, c, False), carry)
        rest = jnp.maximum(n_wide - first_wide, 0)
        carry = lax.fori_loop(0, rest // 2, lambda t, c: blk(first_wide + 2 * t + 1, wide, blk(first_wide + 2 * t, wide, c, False),
                                                             False), carry)
        dk, dv = lax.cond(rest % 2 == 1, lambda c: blk(n_wide - 1, wide, c, False), lambda c: c, carry)
        dk_ref[...] = dk
        dv_ref[...] = dv

    once = pl.Buffered(1)
    rows = pl.BlockSpec((1, 8, S), lambda h, j: (h // 2, 0, 0))
    return pl.pallas_call(
        body, name="fox_bwd", grid=(FOX_H, S // tk),
        in_specs=[pl.BlockSpec((S, 128), lambda h, j: (0, h), pipeline_mode=once),
                  pl.BlockSpec((128, S), lambda h, j: (h, 0), pipeline_mode=once),
                  pl.BlockSpec((tk, 128), lambda h, j: (j, h)), pl.BlockSpec((128, tk), lambda h, j: (h, j)),
                  pl.BlockSpec((tk, 128), lambda h, j: (j, A_FV // 128 + h // 2)),
                  pl.BlockSpec((S, 128), lambda h, j: (0, h // 2), pipeline_mode=once),
                  pl.BlockSpec((128, S), lambda h, j: (h // 2, 0), pipeline_mode=once), rows, rows],
        out_specs=[pl.BlockSpec((128, S), lambda h, j: (h, 0), pipeline_mode=once),
                   pl.BlockSpec((128, tk), lambda h, j: (h, j)), pl.BlockSpec((FOX_DH, tk), lambda h, j: (h, j))],
        out_shape=[jax.ShapeDtypeStruct((1024, S), F32), jax.ShapeDtypeStruct((1024, S), F32),
                   jax.ShapeDtypeStruct((FOX_W, S), F32)],
        compiler_params=_params("arbitrary", "arbitrary"),
    )(qa, qat, ka, kat, pa, dob, dobt, lse, delta)


def _fox_post(dq, dk, dv, ps, bfg, d_proj):
    S = dq.shape[1]
    tm = _tile(S, 512)
    n_b = S // tm

    def body(dq_ref, dk_ref, dv_ref, ps_ref, b_ref, _, dp_ref, dff_ref, dbf_ref, carry):
        first = pl.program_id(0) == 0

        @pl.when(first)
        def _():
            carry[...] = jnp.zeros_like(carry)

        low = lax.broadcasted_iota(jnp.int32, (tm, 128), 1) < FOX_DH
        for h in range(FOX_H):
            blk = slice(128 * h, 128 * (h + 1))
            dp_ref[:, blk] = jnp.where(low, dq_ref[blk, :].T * FOX_SCALE, 0.0).astype(BF16)
            dp_ref[:, 1024 + 128 * h:1024 + 128 * (h + 1)] = jnp.where(low, dk_ref[blk, :].T, 0.0).astype(BF16)
        dp_ref[:, 2048:P_FOX_W] = dv_ref[...].T.astype(BF16)
        rr = lax.broadcasted_iota(jnp.int32, (FOX_H, 1024), 0)
        cc = lax.broadcasted_iota(jnp.int32, (FOX_H, 1024), 1)
        sel_k = jnp.where(cc == 128 * rr + AUG + 3, 1.0, 0.0).astype(F32)
        sel_q = jnp.where(cc == 128 * rr + AUG, 1.0, 0.0).astype(F32)
        g = (jnp.dot(sel_k, dk_ref[...], preferred_element_type=F32, precision=HIGHEST)
             - jnp.dot(sel_q, dq_ref[...], preferred_element_type=F32, precision=HIGHEST))
        t_from = lax.broadcasted_iota(jnp.int32, (tm, tm), 0)
        t_to = lax.broadcasted_iota(jnp.int32, (tm, tm), 1)
        later = jnp.where(t_from >= t_to, 1.0, 0.0).astype(F32)
        dlf = jnp.dot(-g, later, preferred_element_type=F32, precision=HIGHEST) + carry[:, 0:1]
        carry[...] = jnp.broadcast_to(dlf[:, 0:1], carry.shape)
        cols = jnp.concatenate([jnp.zeros((FF_LANE, tm), F32), dlf, jnp.zeros((128 - FF_LANE - FOX_H, tm), F32)], axis=0).T
        dff = cols * jax.nn.sigmoid(-(ps_ref[...] + b_ref[...]))
        dff_ref[...] = dff
        part = _sum8(dff)

        @pl.when(first)
        def _():
            dbf_ref[...] = part

        @pl.when(jnp.logical_not(first))
        def _():
            dbf_ref[...] += part

    rev = lambda i: (n_b - 1 - i, 0)
    tall = lambda n: pl.BlockSpec((n, tm), lambda i: (0, n_b - 1 - i))
    return pl.pallas_call(
        body, name="fox_post", grid=(n_b,),
        in_specs=[tall(1024), tall(1024), tall(FOX_W), pl.BlockSpec((tm, 128), lambda i: (n_b - 1 - i, 4)),
                  pl.BlockSpec((1, 128), lambda i: (0, 0)), pl.BlockSpec(memory_space=pl.ANY)],
        out_specs=[pl.BlockSpec((tm, P_FOX_W), lambda i: (n_b - 1 - i, P_FOX // P_FOX_W)), pl.BlockSpec((tm, 128), rev),
                   pl.BlockSpec((8, 128), lambda i: (0, 0))],
        out_shape=[jax.ShapeDtypeStruct((S, P_W), BF16), jax.ShapeDtypeStruct((S, 128), F32),
                   jax.ShapeDtypeStruct((8, 128), F32)],
        input_output_aliases={5: 0},
        scratch_shapes=[pltpu.VMEM((8, 128), F32)],
        compiler_params=_params("arbitrary"),
    )(dq, dk, dv, ps, bfg, d_proj)


def _mem_prep(mem, g_mem, wkv):
    def body(m_ref, g_ref, w_ref, mn_ref, kv_ref):
        r, xh = _rms(m_ref[...])
        mn = (xh * g_ref[...]).astype(BF16)
        mn_ref[...] = mn
        kv_ref[...] = _nn(mn, w_ref[...]).astype(BF16)

    return pl.pallas_call(
        body, name="mem_prep",
        out_shape=[jax.ShapeDtypeStruct((N_MEM, D), BF16), jax.ShapeDtypeStruct((N_MEM, 2 * MEM_W), BF16)],
        compiler_params=pltpu.CompilerParams(vmem_limit_bytes=V7X_VMEM_LIMIT),
    )(mem, g_mem, wkv)


def _mem_softmax(qh, kh):
    s = _nt(qh, kh) * MEM_SCALE
    e = jnp.exp(s - jnp.max(s, axis=-1, keepdims=True))
    return e / jnp.sum(e, axis=-1, keepdims=True)


def _mem_fwd(pa, mkv):
    S = pa.shape[0]
    tm = _tile(S, 512)

    def body(q_ref, kv_ref, o_ref):
        for h in range(MEM_H):
            cols = slice(MEM_DH * h, MEM_DH * (h + 1))
            p = _mem_softmax(q_ref[:, cols], kv_ref[:, cols])
            o_ref[:, cols] = _nn(p.astype(BF16), kv_ref[:, MEM_W + MEM_DH * h:MEM_W + MEM_DH * (h + 1)])

    return pl.pallas_call(
        body, name="mem_fwd", grid=(S // tm,),
        in_specs=[pl.BlockSpec((tm, MEM_W), lambda i: (i, A_MQ // MEM_W)), pl.BlockSpec((N_MEM, 2 * MEM_W), lambda i: (0, 0))],
        out_specs=pl.BlockSpec((tm, MEM_W), lambda i: (i, 0)),
        out_shape=jax.ShapeDtypeStruct((S, MEM_W), F32),
        compiler_params=_params("arbitrary"),
    )(pa, mkv)


def _mem_bwd(pa, mkv, d_o, d_proj):
    S = pa.shape[0]
    tm = _tile(S, 512)

    def body(q_ref, kv_ref, do_ref, _, dq_ref, dkv_ref):
        first = pl.program_id(0) == 0
        parts = []
        for h in range(MEM_H):
            cols = slice(MEM_DH * h, MEM_DH * (h + 1))
            vcols = slice(MEM_W + MEM_DH * h, MEM_W + MEM_DH * (h + 1))
            qh, kh = q_ref[:, cols], kv_ref[:, cols]
            p = _mem_softmax(qh, kh)
            dob = do_ref[:, cols].astype(BF16)
            dp = _nt(dob, kv_ref[:, vcols])
            ds = (p * (dp - jnp.sum(p * dp, axis=-1, keepdims=True)) * MEM_SCALE).astype(BF16)
            dq_ref[:, cols] = _nn(ds, kh).astype(BF16)
            parts.append((cols, _tn(ds, qh)))
            parts.append((vcols, _tn(p.astype(BF16), dob)))

        @pl.when(first)
        def _():
            for sl, v in parts:
                dkv_ref[:, sl] = v

        @pl.when(jnp.logical_not(first))
        def _():
            for sl, v in parts:
                dkv_ref[:, sl] += v

    return pl.pallas_call(
        body, name="mem_bwd", grid=(S // tm,),
        in_specs=[pl.BlockSpec((tm, MEM_W), lambda i: (i, A_MQ // MEM_W)), pl.BlockSpec((N_MEM, 2 * MEM_W), lambda i: (0, 0)),
                  pl.BlockSpec((tm, MEM_W), lambda i: (i, 0)), pl.BlockSpec(memory_space=pl.ANY)],
        out_specs=[pl.BlockSpec((tm, MEM_W), lambda i: (i, P_MQ // MEM_W)), pl.BlockSpec((N_MEM, 2 * MEM_W), lambda i: (0, 0))],
        out_shape=[jax.ShapeDtypeStruct((S, P_W), BF16), jax.ShapeDtypeStruct((N_MEM, 2 * MEM_W), F32)],
        input_output_aliases={3: 0},
        compiler_params=_params("arbitrary"),
    )(pa, mkv, d_o, d_proj)


def _mem_prep_bwd(mem, g_mem, mn, wkv, dkv):
    def body(m_ref, g_ref, mn_ref, w_ref, d_ref, dw_ref, dg_ref):
        db = d_ref[...].astype(BF16)
        dw_ref[...] = _tn(mn_ref[...], db).astype(BF16)
        r, xh = _rms(m_ref[...])
        dg_ref[...] = _sum8(_nt(db, w_ref[...]) * xh)

    dw, dg = pl.pallas_call(
        body, name="mem_prep_bwd",
        out_shape=[jax.ShapeDtypeStruct((D, 2 * MEM_W), BF16), jax.ShapeDtypeStruct((8, D), F32)],
        compiler_params=pltpu.CompilerParams(vmem_limit_bytes=V7X_VMEM_LIMIT),
    )(mem, g_mem, mn, wkv, dkv)
    return dw.reshape(N_DEV, D // N_DEV, 2 * MEM_W), dg


def _rearrange_w_in(w):
    def heads128(cols):
        blk = w[:, cols:cols + FOX_W].reshape(D, FOX_H, FOX_DH)
        return jnp.pad(blk, ((0, 0), (0, 0), (0, 128 - FOX_DH))).reshape(D, FOX_H * 128)

    fq, fk, fv, mq, wg = heads128(O_FQ), heads128(O_FK), w[:, O_FV:O_FF], w[:, O_MQ:O_GT], w[:, O_GT:]
    gaff = jnp.concatenate([w[:, O_GA:O_FQ], w[:, O_FF:O_MQ], jnp.zeros((D, 128 - GLA_R - FOX_H), w.dtype)], axis=1)
    wa = jnp.concatenate([w[:, O_GQ:O_GG], fq, fk, fv, mq], axis=1)
    ws = jnp.concatenate([w[:, O_GG:O_GA], gaff], axis=1)
    wp = jnp.concatenate([fq, fk, fv, mq, wg, w[:, O_GQ:O_GG], ws, jnp.zeros((D, P_W - P_GLA - 1024 - S_W), w.dtype)], axis=1)
    return wa, wg, ws, wp


def _restore_w_in_grad(dwp):
    def unheads(off):
        return dwp[:, off:off + FOX_H * 128].reshape(D, FOX_H, 128)[:, :, :FOX_DH].reshape(D, FOX_W)

    g0 = P_GLA + 1024
    return jnp.concatenate([
        dwp[:, P_GLA:g0], dwp[:, g0:g0 + 512], dwp[:, g0 + 512:g0 + 512 + GLA_R], unheads(P_FOX), unheads(P_FOX + 1024),
        dwp[:, P_FOX + 2048:P_FOX + P_FOX_W], dwp[:, g0 + 512 + GLA_R:g0 + 512 + GLA_R + FOX_H], dwp[:, P_MQ:P_GT],
        dwp[:, P_GT:P_GLA]], axis=1)


def _local_step(x, mem, target, p):
    S = x.shape[0]
    wa, wg, ws, wp = _rearrange_w_in(p["w_in"])
    wau = jnp.pad(p["w_alpha_up"], ((0, 128 - GLA_R), (0, 0)))
    bfg = jnp.pad(p["b_forget"], ((0, 0), (FF_LANE, 128 - FF_LANE - FOX_H)))
    gh = p["g_gla_head"].reshape(1, GLA_V)

    pa, pg, ps, u = _proj(x, p["g_mix"], wa, wg, ws)
    o_gla, og, sprev = _gla_fwd(pa, ps, wau, p["b_alpha"], gh)
    qa, ka, qat, kat, vt = _fox_prep(pa, ps, bfg)
    o_fox, lse = _fox_fwd(qa, ka, vt)
    mn, mkv = _mem_prep(mem, p["g_mem"], p["w_mem_kv"])
    o_mem = _mem_fwd(pa, mkv)
    y3, mg = _merge(og, o_fox, o_mem, p["w_gla_o"], p["w_fox_o"], p["w_mem_o"], pg)
    h1, u2 = _out_proj(mg, p["w_out"], x, p["g_ffn"])
    a, act = _ff1(u2, p["w_ff1"])
    dh2, dh2b, loss8, dg_final = _ff2_loss(act, p["w_ff2"], h1, p["g_final"].reshape(1, D), target)

    d_a = _dact(dh2b, p["w_ff2"], a)
    dw_ff2 = _wgrad(act, dh2b, "wgrad_ff2", 0)
    dh1, dh1b, dg_ffn = _nt_rmsbwd(d_a, p["w_ff1"], h1, p["g_ffn"], dh2, "dffn", True)
    dw_ff1 = _wgrad(u2, d_a, "wgrad_ff1", 1)
    dy_g, dy_f, dy_m, do_g, do_f, do_m, d_proj = _dmerge(dh1b, p["w_out"], pg, y3, p["w_gla_o"], p["w_fox_o"], p["w_mem_o"])
    dw_out = _wgrad(mg, dh1b, "wgrad_out", 0)
    dw_gla_o = _wgrad(og, dy_g, "wgrad_gla_o", 1)
    dw_fox_o = _wgrad(o_fox, dy_f, "wgrad_fox_o", 1)
    dw_mem_o = _wgrad(o_mem, dy_m, "wgrad_mem_o", 1)
    d_proj, d_mkv = _mem_bwd(pa, mkv, do_m, d_proj)
    dw_mem_kv, dg_mem = _mem_prep_bwd(mem, p["g_mem"], mn, p["w_mem_kv"], d_mkv)
    dob, dobt, delta = _fox_delta(do_f, o_fox)
    dq, dk, dv = _fox_bwd(qa, qat, ka, kat, pa, dob, dobt, lse, delta)
    d_proj, dgaff_fox, db_forget = _fox_post(dq, dk, dv, ps, bfg, d_proj)
    d_proj, dw_au, db_alpha, dg_gla = _gla_bwd(pa, ps, wau, p["b_alpha"], gh, o_gla, do_g, sprev, dgaff_fox, d_proj)
    dx, dg_mix = _nt_rmsbwd(d_proj, wp, x, p["g_mix"], dh1, "dmix", False)
    dw_in = _slabs(_restore_w_in_grad(_wgrad(u, d_proj, "wgrad_in")), 1).astype(BF16)

    big = dict(w_in=dw_in, w_mem_kv=dw_mem_kv, w_gla_o=dw_gla_o, w_fox_o=dw_fox_o, w_mem_o=dw_mem_o, w_out=dw_out,
               w_ff1=dw_ff1, w_ff2=dw_ff2)
    small = dict(g_mix=dg_mix, g_mem=dg_mem, g_ffn=dg_ffn, g_final=dg_final, b_alpha=db_alpha, g_gla_head=dg_gla,
                 b_forget=db_forget, w_alpha_up=dw_au, loss=loss8)
    return dx, big, small


BIG = (("w_in", 1), ("w_mem_kv", 0), ("w_gla_o", 1), ("w_fox_o", 1), ("w_mem_o", 1), ("w_out", 0), ("w_ff1", 1), ("w_ff2", 0))


def _peer(d):
    me = lax.axis_index("x") * 4 + lax.axis_index("y") * 2 + lax.axis_index("c")
    t = (me + d) % N_DEV
    return (t // 4, (t // 2) % 2, t % 2), me


def _exchange_call(body, blocks, out_shape, name):
    n = len(blocks)
    any_spec = pl.BlockSpec(memory_space=pl.ANY)
    return pl.pallas_call(
        body, name=name, in_specs=[any_spec] * n, out_specs=[any_spec] * n, out_shape=out_shape,
        scratch_shapes=[pltpu.SemaphoreType.DMA((n, N_DEV - 1)), pltpu.SemaphoreType.DMA((n, N_DEV - 1)),
                        pltpu.SemaphoreType.DMA((n,))],
    )(*blocks)


def _scatter_grads(slabs):
    n = len(slabs)

    def body(*refs):
        ins, outs = refs[:n], refs[n:2 * n]
        send, recv, loc = refs[2 * n:]
        _, me = _peer(0)
        copies = [pltpu.make_async_copy(ins[k].at[me], outs[k].at[me], loc.at[k]) for k in range(n)]
        for d in range(1, N_DEV):
            to, _ = _peer(d)
            copies += [pltpu.make_async_remote_copy(
                src_ref=ins[k].at[(me + d) % N_DEV], dst_ref=outs[k].at[me], send_sem=send.at[k, d - 1],
                recv_sem=recv.at[k, d - 1], device_id=to, device_id_type=MESH) for k in range(n)]
        for cp in copies:
            cp.start()
        for cp in copies[n:]:
            cp.wait_send()
        for cp in copies[n:]:
            cp.wait_recv()
        for cp in copies[:n]:
            cp.wait()

    return _exchange_call(body, slabs, [jax.ShapeDtypeStruct(b.shape, b.dtype) for b in slabs], "scatter_grads")


def _gather_weights(shards):
    n = len(shards)

    def body(*refs):
        ins, outs = refs[:n], refs[n:2 * n]
        send, recv, loc = refs[2 * n:]
        x, y, c = lax.axis_index("x"), lax.axis_index("y"), lax.axis_index("c")
        sibling = (x, y, 1 - c)
        chips = [(1 - x, y), (x, 1 - y), (1 - x, 1 - y)]
        slot = lambda px, py, pc: px * 4 + py * 2 + pc

        def copy(k, s, block, to, src=None):
            rows = outs[k].at[slot(*block)]
            return pltpu.make_async_remote_copy(src_ref=rows if src is None else src, dst_ref=rows, send_sem=send.at[k, s],
                                                recv_sem=recv.at[k, s], device_id=to, device_id_type=MESH)

        me = (x, y, c)
        own = [pltpu.make_async_copy(ins[k], outs[k].at[slot(*me)], loc.at[k]) for k in range(n)]
        first = [copy(k, 0, me, sibling, src=ins[k]) for k in range(n)]
        first += [copy(k, 1 + j, me, (*chip, c), src=ins[k]) for j, chip in enumerate(chips) for k in range(n)]
        for cp in own + first:
            cp.start()
        passed = []
        for j, chip in enumerate(chips):
            for k in range(n):
                copy(k, 1 + j, (*chip, c), me).wait_recv()
                fwd = copy(k, 4 + j, (*chip, c), sibling)
                fwd.start()
                passed.append(fwd)
        for k in range(n):
            copy(k, 0, sibling, me).wait_recv()
        for j, chip in enumerate(chips):
            for k in range(n):
                copy(k, 4 + j, (*chip, 1 - c), me).wait_recv()
        for cp in first + passed:
            cp.wait_send()
        for cp in own:
            cp.wait()

    return _exchange_call(body, shards, [jax.ShapeDtypeStruct((N_DEV,) + b.shape, b.dtype) for b in shards], "gather_weights")


def _adamw_math(g, w, m, v):
    m2 = ADAM_B1 * m + (1.0 - ADAM_B1) * g
    v2 = ADAM_B2 * v + (1.0 - ADAM_B2) * jnp.square(g)
    m_hat = m2 / (1.0 - ADAM_B1 ** ADAM_STEP)
    v_hat = v2 / (1.0 - ADAM_B2 ** ADAM_STEP)
    delta = -ADAM_LR * (m_hat / (jnp.sqrt(v_hat) + ADAM_EPS) + ADAM_WD * w)
    return delta, m2, v2


def _adamw_sum(parts, w, m, v, name):
    R, C = w.shape
    tr = _tile(R, 128)

    def body(p_ref, w_ref, m_ref, v_ref, g_ref, d_ref, m2_ref, v2_ref):
        g = p_ref[0].astype(F32)
        for j in range(1, p_ref.shape[0]):
            g = g + p_ref[j].astype(F32)
        g_ref[...] = g
        d_ref[...], m2_ref[...], v2_ref[...] = _adamw_math(g, w_ref[...], m_ref[...], v_ref[...])

    blk = pl.BlockSpec((tr, C), lambda i: (i, 0))
    return pl.pallas_call(
        body, name=name, grid=(R // tr,),
        in_specs=[pl.BlockSpec((parts.shape[0], tr, C), lambda i: (0, i, 0)), blk, blk, blk],
        out_specs=[blk] * 4, out_shape=[jax.ShapeDtypeStruct((R, C), F32)] * 4,
        compiler_params=_params("arbitrary"),
    )(parts, w, m, v)


SMALL_ROWS = 24


def _pack_small(d):
    mixed = jnp.concatenate([d["b_alpha"].reshape(1, GLA_K), d["g_gla_head"].reshape(1, GLA_V),
                             jnp.pad(d["b_forget"].reshape(1, FOX_H), ((0, 0), (FF_LANE, 128 - FF_LANE - FOX_H))),
                             jnp.zeros((1, 128), F32)], axis=1)
    rows = [d["g_mix"].reshape(1, D), d["g_mem"].reshape(1, D), d["g_ffn"].reshape(1, D), d["g_final"].reshape(1, D), mixed,
            jnp.zeros((3, D), F32), jnp.pad(d["w_alpha_up"].reshape(GLA_R, GLA_K), ((0, 0), (0, D - GLA_K)))]
    return jnp.concatenate(rows, axis=0)


def _unpack_small(t):
    return dict(g_mix=t[0:1], g_mem=t[1:2], g_ffn=t[2:3], g_final=t[3], b_alpha=t[4:5, 0:GLA_K],
                g_gla_head=t[4:5, GLA_K:GLA_K + GLA_V].reshape(1, GLA_H, GLA_DV),
                b_forget=t[4:5, 768 + FF_LANE:768 + FF_LANE + FOX_H], w_alpha_up=t[8:24, 0:GLA_K].reshape(1, GLA_R, GLA_K))


def _small_allreduce(small, w, m, v):
    def body(gm, gme, gf, gfi, ba, gg, bf, wau, ls, w_ref, m_ref, v_ref, g_ref, d_ref, m2_ref, v2_ref, l_ref,
             buf, send, recv):
        _, me = _peer(0)
        buf[me] = jnp.zeros((SMALL_ROWS, D), F32)
        for r, ref in enumerate((gm, gme, gf, gfi)):
            buf[me, r:r + 1, :] = jnp.sum(ref[...], axis=0, keepdims=True)
        buf[me, 4:5, 0:GLA_K] = jnp.sum(ba[...], axis=0, keepdims=True)
        buf[me, 4:5, GLA_K:GLA_K + GLA_V] = jnp.sum(gg[...], axis=0, keepdims=True)
        buf[me, 4:5, 768:896] = jnp.sum(bf[...], axis=0, keepdims=True)
        lrow = jnp.sum(ls[...], axis=0, keepdims=True)
        lsum = lrow[:, 0:128]
        for c in range(1, D // 128):
            lsum = lsum + lrow[:, 128 * c:128 * (c + 1)]
        buf[me, 4:5, 896:1024] = lsum
        buf[me, 8:24, 0:GLA_K] = wau[0:GLA_R, :]
        remote = []
        for d in range(1, N_DEV):
            to, me = _peer(d)
            cp = pltpu.make_async_remote_copy(src_ref=buf.at[me], dst_ref=buf.at[me], send_sem=send.at[d - 1],
                                              recv_sem=recv.at[d - 1], device_id=to, device_id_type=MESH)
            cp.start()
            remote.append(cp)
        for cp in remote:
            cp.wait_send()
        for cp in remote:
            cp.wait_recv()
        g = buf[0]
        for j in range(1, N_DEV):
            g = g + buf[j]
        g_ref[...] = g
        d_ref[...], m2_ref[...], v2_ref[...] = _adamw_math(g, w_ref[...], m_ref[...], v_ref[...])
        l_ref[...] = g[4:5, 896:1024]

    packed = jax.ShapeDtypeStruct((SMALL_ROWS, D), F32)
    return pl.pallas_call(
        body, name="small_allreduce",
        out_shape=[packed, packed, packed, packed, jax.ShapeDtypeStruct((1, 128), F32)],
        scratch_shapes=[pltpu.VMEM((N_DEV, SMALL_ROWS, D), F32), pltpu.SemaphoreType.DMA((N_DEV - 1,)),
                        pltpu.SemaphoreType.DMA((N_DEV - 1,))],
    )(small["g_mix"], small["g_mem"], small["g_ffn"], small["g_final"], small["b_alpha"], small["g_gla_head"],
      small["b_forget"], small["w_alpha_up"], small["loss"], w, m, v)


def _slabs(g, axis):
    R, C = g.shape
    if axis == 0:
        return g.reshape(N_DEV, R // N_DEV, C)
    return g.reshape(R, N_DEV, C // N_DEV).transpose(1, 0, 2)


def _unslab(t, axis):
    n, r, c = t.shape
    if axis == 0:
        return t.reshape(n * r, c)
    return t.transpose(1, 0, 2).reshape(r, n * c)


def kernel(x, mem, g_mix, w_in, w_alpha_up, b_alpha, b_forget, g_gla_head, g_mem, w_mem_kv, w_gla_o, w_fox_o, w_mem_o, w_out, g_ffn, w_ff1, w_ff2, g_final, loss_target, m_g_mix, m_w_in, m_w_alpha_up, m_b_alpha, m_b_forget, m_g_gla_head, m_g_mem, m_w_mem_kv, m_w_gla_o, m_w_fox_o, m_w_mem_o, m_w_out, m_g_ffn, m_w_ff1, m_w_ff2, m_g_final, v_g_mix, v_w_in, v_w_alpha_up, v_b_alpha, v_b_forget, v_g_gla_head, v_g_mem, v_w_mem_kv, v_w_gla_o, v_w_fox_o, v_w_mem_o, v_w_out, v_g_ffn, v_w_ff1, v_w_ff2, v_g_final):
    names = ["g_mix", "w_in", "w_alpha_up", "b_alpha", "b_forget", "g_gla_head", "g_mem", "w_mem_kv", "w_gla_o", "w_fox_o",
             "w_mem_o", "w_out", "g_ffn", "w_ff1", "w_ff2", "g_final"]
    w = dict(g_mix=g_mix, w_in=w_in, w_alpha_up=w_alpha_up, b_alpha=b_alpha, b_forget=b_forget, g_gla_head=g_gla_head,
             g_mem=g_mem, w_mem_kv=w_mem_kv, w_gla_o=w_gla_o, w_fox_o=w_fox_o, w_mem_o=w_mem_o, w_out=w_out, g_ffn=g_ffn,
             w_ff1=w_ff1, w_ff2=w_ff2, g_final=g_final)
    m = dict(g_mix=m_g_mix, w_in=m_w_in, w_alpha_up=m_w_alpha_up, b_alpha=m_b_alpha, b_forget=m_b_forget,
             g_gla_head=m_g_gla_head, g_mem=m_g_mem, w_mem_kv=m_w_mem_kv, w_gla_o=m_w_gla_o, w_fox_o=m_w_fox_o,
             w_mem_o=m_w_mem_o, w_out=m_w_out, g_ffn=m_g_ffn, w_ff1=m_w_ff1, w_ff2=m_w_ff2, g_final=m_g_final)
    v = dict(g_mix=v_g_mix, w_in=v_w_in, w_alpha_up=v_w_alpha_up, b_alpha=v_b_alpha, b_forget=v_b_forget,
             g_gla_head=v_g_gla_head, g_mem=v_g_mem, w_mem_kv=v_w_mem_kv, w_gla_o=v_w_gla_o, w_fox_o=v_w_fox_o,
             w_mem_o=v_w_mem_o, w_out=v_w_out, g_ffn=v_g_ffn, w_ff1=v_w_ff1, w_ff2=v_w_ff2, g_final=v_g_final)
    me = lax.axis_index("x") * 4 + lax.axis_index("y") * 2 + lax.axis_index("c")

    shards = [w[n][0].astype(BF16) for n, _ in BIG] + [w["w_alpha_up"][0].astype(BF16)]
    gathered = _gather_weights(shards)
    p = {n: _unslab(t, ax) for (n, ax), t in zip(BIG, gathered[:-1])}
    p["w_alpha_up"] = _unslab(gathered[-1], 1)
    p.update(g_mix=g_mix, b_alpha=b_alpha, b_forget=b_forget, g_gla_head=g_gla_head, g_mem=g_mem, g_ffn=g_ffn, g_final=g_final)

    dx, big, small = _local_step(x[0], mem[0], loss_target[0], p)

    recv = _scatter_grads([big[n] for n, _ in BIG])
    out_g, out_d, out_m, out_v = {}, {}, {}, {}
    for (n, ax), parts in zip(BIG, recv):
        g_, d_, m_, v_ = _adamw_sum(parts, w[n][0], m[n][0], v[n][0], "adamw_" + n)
        out_g[n], out_d[n], out_m[n], out_v[n] = g_[None], d_[None], m_[None], v_[None]

    full = lambda d: dict(d, w_alpha_up=jnp.zeros((1, GLA_R, GLA_K), F32))
    gs, ds, ms, vs, lrow = _small_allreduce(small, _pack_small(full(w)), _pack_small(full(m)), _pack_small(full(v)))
    g_s, d_s, m_s, v_s = _unpack_small(gs), _unpack_small(ds), _unpack_small(ms), _unpack_small(vs)
    for n in names:
        if n not in out_g and n != "w_alpha_up":
            out_g[n], out_d[n], out_m[n], out_v[n] = g_s[n], d_s[n], m_s[n], v_s[n]
    g_au = lax.dynamic_slice_in_dim(g_s["w_alpha_up"][0], me * (GLA_K // N_DEV), GLA_K // N_DEV, axis=1)
    g_, d_, m_, v_ = _adamw_sum(g_au[None], w_alpha_up[0], m_w_alpha_up[0], v_w_alpha_up[0], "adamw_w_alpha_up")
    out_g["w_alpha_up"], out_d["w_alpha_up"], out_m["w_alpha_up"], out_v["w_alpha_up"] = g_[None], d_[None], m_[None], v_[None]

    loss = jnp.sum(lrow) * (0.5 / D)
    return (loss, dx[None], *[out_g[n] for n in names], *[out_d[n] for n in names], *[out_m[n] for n in names],
            *[out_v[n] for n in names])
```

```python
import jax
import jax.numpy as jnp
from jax import lax
from jax.experimental import pallas as pl
from jax.experimental.pallas import tpu as pltpu

F32, BF16 = jnp.float32, jnp.bfloat16
HIGHEST = lax.Precision.HIGHEST
MESH = pl.DeviceIdType.MESH

N_DEV = 8
D = 1024
EPS = 1e-6
CHUNK = 64
N_MEM = 256
GLA_H, GLA_DK, GLA_DV = 4, 64, 128
GLA_K, GLA_V, GLA_R = 256, 512, 16
FOX_H, FOX_DH, FOX_W = 8, 64, 512
MEM_H, MEM_DH, MEM_W = 4, 128, 512
D_FF = 4096
D_IN = 6680
FOX_SCALE = 0.125
GLA_SCALE = 0.125
MEM_SCALE = MEM_DH ** -0.5
GLA_TAU_INV = 1.0 / 16.0
NEG = -1e30

O_GQ, O_GK, O_GV, O_GG, O_GA, O_FQ, O_FK, O_FV, O_FF, O_MQ, O_GT = 0, 256, 512, 1024, 1536, 1552, 2064, 2576, 3088, 3096, 3608
A_FQ, A_FK, A_FV, A_MQ, A_W = 1024, 2048, 3072, 3584, 4096
S_W = 640
G_W = 3072
P_FOX, P_FOX_W, P_MQ, P_GT, P_GLA, P_GLA_W, P_W = 0, 2560, 2560, 3072, 6144, 2048, 8192
FF_LANE = 16
AUG = 64
FOX_LIVE = 80

ADAM_LR, ADAM_B1, ADAM_B2, ADAM_EPS, ADAM_WD, ADAM_STEP = 0.001, 0.9, 0.999, 1e-08, 0.01, 10
V7X_VMEM_LIMIT = 48 * 1024 * 1024
FOX_TQ, FOX_TK = 2048, 512


def _params(*sem):
    return pltpu.CompilerParams(dimension_semantics=sem, vmem_limit_bytes=V7X_VMEM_LIMIT)


def _nt(a, b):
    return lax.dot_general(a, b, (((1,), (1,)), ((), ())), preferred_element_type=F32)


def _tn(a, b):
    return lax.dot_general(a, b, (((0,), (0,)), ((), ())), preferred_element_type=F32)


def _nn(a, b):
    return jnp.dot(a, b, preferred_element_type=F32)


def _log_sigmoid(z):
    return jnp.minimum(z, 0.0) - jnp.log(1.0 + jnp.exp(-jnp.abs(z)))


def _sum8(x):
    return x.reshape(x.shape[0] // 8, 8, x.shape[1]).sum(axis=0)


def _rms(xv):
    r = lax.rsqrt(jnp.mean(xv * xv, axis=-1, keepdims=True) + EPS)
    return r, xv * r


def _rms_bwd(du, g, r, xh):
    w = du * g
    return r * (w - xh * jnp.mean(w * xh, axis=-1, keepdims=True))


def _row_chunks(n, size=256):
    return [slice(r, r + min(size, n)) for r in range(0, n, min(size, n))]


def _tile(n, pref):
    t = min(n, pref)
    assert n % t == 0, (n, t)
    return t


def _proj(x, g, wa, wg, ws):
    S = x.shape[0]
    tm, tn = _tile(S, 1024), 1024
    n_a, n_g = A_W // tn, G_W // tn

    def body(x_ref, g_ref, wa_ref, wg_ref, ws_ref, pa_ref, pg_ref, ps_ref, u_ref, u_s):
        j = pl.program_id(1)

        @pl.when(j == 0)
        def _():
            r, xh = _rms(x_ref[...])
            u_s[...] = (xh * g_ref[...]).astype(BF16)
            u_ref[...] = u_s[...]

        @pl.when(j < n_a)
        def _():
            pa_ref[...] = _nn(u_s[...], wa_ref[...]).astype(BF16)

        @pl.when((j >= n_a) & (j < n_a + n_g))
        def _():
            pg_ref[...] = _nn(u_s[...], wg_ref[...])

        @pl.when(j == n_a + n_g)
        def _():
            ps_ref[...] = _nn(u_s[...], ws_ref[...])

    in_a = lambda j: jnp.minimum(j, n_a - 1)
    in_g = lambda j: jnp.clip(j - n_a, 0, n_g - 1)
    row = pl.BlockSpec((tm, D), lambda i, j: (i, 0))
    return pl.pallas_call(
        body, name="proj", grid=(S // tm, n_a + n_g + 1),
        in_specs=[row, pl.BlockSpec((1, D), lambda i, j: (0, 0)), pl.BlockSpec((D, tn), lambda i, j: (0, in_a(j))),
                  pl.BlockSpec((D, tn), lambda i, j: (0, in_g(j))),
                  pl.BlockSpec((D, S_W), lambda i, j: (0, 0), pipeline_mode=pl.Buffered(1))],
        out_specs=[pl.BlockSpec((tm, tn), lambda i, j: (i, in_a(j))), pl.BlockSpec((tm, tn), lambda i, j: (i, in_g(j))),
                   pl.BlockSpec((tm, S_W), lambda i, j: (i, 0)), row],
        out_shape=[jax.ShapeDtypeStruct((S, A_W), BF16), jax.ShapeDtypeStruct((S, G_W), F32),
                   jax.ShapeDtypeStruct((S, S_W), F32), jax.ShapeDtypeStruct((S, D), BF16)],
        scratch_shapes=[pltpu.VMEM((tm, D), BF16)],
        compiler_params=_params("arbitrary", "arbitrary"),
    )(x, g, wa, wg, ws)


def _wgrad(a, b, name, slab_axis=None):
    S, Ka = a.shape
    N = b.shape[1]
    tka, tn, ts = _tile(Ka, 1024), _tile(N, 1024), _tile(S, 1024)
    n_s = S // ts
    per = N // N_DEV
    slabs_per_step = tn // per

    def body(a_ref, b_ref, o_ref, acc):
        s = pl.program_id(2)

        @pl.when(s == 0)
        def _():
            acc[...] = jnp.zeros_like(acc)

        acc[...] += _tn(a_ref[...].astype(BF16), b_ref[...].astype(BF16))

        @pl.when(s == n_s - 1)
        def _():
            if slab_axis == 1:
                for q in range(slabs_per_step):
                    o_ref[q] = acc[:, per * q:per * (q + 1)].astype(BF16)
            else:
                o_ref[...] = acc[...].astype(o_ref.dtype)

    if slab_axis == 1:
        out_spec = pl.BlockSpec((slabs_per_step, tka, per), lambda i, j, s: (j, i, 0))
        out_shape = jax.ShapeDtypeStruct((N_DEV, Ka, per), BF16)
    else:
        out_spec = pl.BlockSpec((tka, tn), lambda i, j, s: (i, j))
        out_shape = jax.ShapeDtypeStruct((Ka, N), F32 if slab_axis is None else BF16)
    out = pl.pallas_call(
        body, name=name, grid=(Ka // tka, N // tn, n_s),
        in_specs=[pl.BlockSpec((ts, tka), lambda i, j, s: (s, i)), pl.BlockSpec((ts, tn), lambda i, j, s: (s, j))],
        out_specs=out_spec, out_shape=out_shape,
        scratch_shapes=[pltpu.VMEM((tka, tn), F32)],
        compiler_params=_params("arbitrary", "arbitrary", "arbitrary"),
    )(a, b)
    return out.reshape(N_DEV, Ka // N_DEV, N) if slab_axis == 0 else out


def _nt_rmsbwd(a, w, xin, g, dres, name, emit_bf16):
    S, K = a.shape
    tm, tk = _tile(S, 1024), _tile(K, 1024)
    n_k = K // tk

    def body(a_ref, w_ref, x_ref, g_ref, r_ref, o_ref, *rest):
        dg_ref, acc = rest[-2], rest[-1]
        i, k = pl.program_id(0), pl.program_id(1)

        @pl.when(k == 0)
        def _():
            acc[...] = jnp.zeros_like(acc)

        acc[...] += _nt(a_ref[...], w_ref[...])

        @pl.when(k == n_k - 1)
        def _():
            @pl.when(i == 0)
            def _():
                dg_ref[...] = jnp.zeros_like(dg_ref)

            for rows in _row_chunks(tm):
                du = acc[rows, :]
                r, xh = _rms(x_ref[rows, :])
                out = r_ref[rows, :] + _rms_bwd(du, g_ref[...], r, xh)
                o_ref[rows, :] = out
                if emit_bf16:
                    rest[0][rows, :] = out.astype(BF16)
                dg_ref[...] += _sum8(du * xh)

    row = pl.BlockSpec((tm, D), lambda i, k: (i, 0))
    out_shape = [jax.ShapeDtypeStruct((S, D), F32)]
    out_specs = [row]
    if emit_bf16:
        out_shape.append(jax.ShapeDtypeStruct((S, D), BF16))
        out_specs.append(row)
    out_shape.append(jax.ShapeDtypeStruct((8, D), F32))
    out_specs.append(pl.BlockSpec((8, D), lambda i, k: (0, 0)))
    return pl.pallas_call(
        body, name=name, grid=(S // tm, n_k),
        in_specs=[pl.BlockSpec((tm, tk), lambda i, k: (i, k)), pl.BlockSpec((D, tk), lambda i, k: (0, k)),
                  row, pl.BlockSpec((1, D), lambda i, k: (0, 0)), row],
        out_specs=out_specs, out_shape=out_shape,
        scratch_shapes=[pltpu.VMEM((tm, D), F32)],
        compiler_params=_params("arbitrary", "arbitrary"),
    )(a, w, xin, g, dres)


def _merge(og, ofox, omem, wg, wf, wm, pg):
    S = og.shape[0]
    tm = _tile(S, 256)

    def body(og_ref, of_ref, om_ref, wg_ref, wf_ref, wm_ref, pg_ref, y_ref, mg_ref):
        tot = None
        for i, (o_ref, w_ref) in enumerate(((og_ref, wg_ref), (of_ref, wf_ref), (om_ref, wm_ref))):
            y = _nn(o_ref[...].astype(BF16), w_ref[...])
            y_ref[i] = y
            t = jax.nn.sigmoid(pg_ref[:, D * i:D * (i + 1)]) * y
            tot = t if tot is None else tot + t
        mg_ref[...] = tot.astype(BF16)

    o_spec = pl.BlockSpec((tm, 512), lambda i: (i, 0))
    w_spec = pl.BlockSpec((512, D), lambda i: (0, 0))
    return pl.pallas_call(
        body, name="merge", grid=(S // tm,),
        in_specs=[o_spec, o_spec, o_spec, w_spec, w_spec, w_spec, pl.BlockSpec((tm, G_W), lambda i: (i, 0))],
        out_specs=[pl.BlockSpec((3, tm, D), lambda i: (0, i, 0)), pl.BlockSpec((tm, D), lambda i: (i, 0))],
        out_shape=[jax.ShapeDtypeStruct((3, S, D), F32), jax.ShapeDtypeStruct((S, D), BF16)],
        compiler_params=_params("arbitrary"),
    )(og, ofox, omem, wg, wf, wm, pg)


def _out_proj(mg, w_out, x, g_ffn):
    S = x.shape[0]
    tm = _tile(S, 512)

    def body(mg_ref, w_ref, x_ref, g_ref, h_ref, u_ref):
        h = x_ref[...] + _nn(mg_ref[...], w_ref[...])
        h_ref[...] = h
        r, xh = _rms(h)
        u_ref[...] = (xh * g_ref[...]).astype(BF16)

    row = pl.BlockSpec((tm, D), lambda i: (i, 0))
    return pl.pallas_call(
        body, name="out_proj", grid=(S // tm,),
        in_specs=[row, pl.BlockSpec((D, D), lambda i: (0, 0)), row, pl.BlockSpec((1, D), lambda i: (0, 0))],
        out_specs=[row, row],
        out_shape=[jax.ShapeDtypeStruct((S, D), F32), jax.ShapeDtypeStruct((S, D), BF16)],
        compiler_params=_params("arbitrary"),
    )(mg, w_out, x, g_ffn)


def _ff1(u2, w1):
    S = u2.shape[0]
    tm, tn = _tile(S, 1024), 1024

    def body(u_ref, w_ref, a_ref, act_ref):
        a = _nn(u_ref[...], w_ref[...])
        a_ref[...] = a.astype(BF16)
        act_ref[...] = jnp.square(jnp.maximum(a, 0.0)).astype(BF16)

    blk = pl.BlockSpec((tm, tn), lambda i, j: (i, j))
    return pl.pallas_call(
        body, name="ff1", grid=(S // tm, D_FF // tn),
        in_specs=[pl.BlockSpec((tm, D), lambda i, j: (i, 0)), pl.BlockSpec((D, tn), lambda i, j: (0, j))],
        out_specs=[blk, blk],
        out_shape=[jax.ShapeDtypeStruct((S, D_FF), BF16), jax.ShapeDtypeStruct((S, D_FF), BF16)],
        compiler_params=_params("arbitrary", "arbitrary"),
    )(u2, w1)


def _ff2_loss(act, w2, h1, g_final, target):
    S = act.shape[0]
    tm, tk = _tile(S, 1024), 1024
    n_k = D_FF // tk

    def body(a_ref, w_ref, h_ref, g_ref, t_ref, d_ref, db_ref, ls_ref, dg_ref, acc):
        i, k = pl.program_id(0), pl.program_id(1)

        @pl.when(k == 0)
        def _():
            acc[...] = jnp.zeros_like(acc)

        acc[...] += _nn(a_ref[...], w_ref[...])

        @pl.when(k == n_k - 1)
        def _():
            @pl.when(i == 0)
            def _():
                ls_ref[...] = jnp.zeros_like(ls_ref)
                dg_ref[...] = jnp.zeros_like(dg_ref)

            gf = g_ref[...]
            for rows in _row_chunks(tm):
                r, xh = _rms(h_ref[rows, :] + acc[rows, :])
                err = xh * gf - t_ref[rows, :]
                dy = err * (1.0 / D)
                dh = _rms_bwd(dy, gf, r, xh)
                d_ref[rows, :] = dh
                db_ref[rows, :] = dh.astype(BF16)
                ls_ref[...] += _sum8(err * err)
                dg_ref[...] += _sum8(dy * xh)

    row = pl.BlockSpec((tm, D), lambda i, k: (i, 0))
    part = pl.BlockSpec((8, D), lambda i, k: (0, 0))
    return pl.pallas_call(
        body, name="ff2_loss", grid=(S // tm, n_k),
        in_specs=[pl.BlockSpec((tm, tk), lambda i, k: (i, k)), pl.BlockSpec((tk, D), lambda i, k: (k, 0)),
                  row, pl.BlockSpec((1, D), lambda i, k: (0, 0)), row],
        out_specs=[row, row, part, part],
        out_shape=[jax.ShapeDtypeStruct((S, D), F32), jax.ShapeDtypeStruct((S, D), BF16),
                   jax.ShapeDtypeStruct((8, D), F32), jax.ShapeDtypeStruct((8, D), F32)],
        scratch_shapes=[pltpu.VMEM((tm, D), F32)],
        compiler_params=_params("arbitrary", "arbitrary"),
    )(act, w2, h1, g_final, target)


def _dact(dh2b, w2, a):
    S = a.shape[0]
    tm, tn = _tile(S, 1024), 1024

    def body(d_ref, w_ref, a_ref, o_ref):
        da = _nt(d_ref[...], w_ref[...])
        o_ref[...] = (da * (2.0 * jnp.maximum(a_ref[...].astype(F32), 0.0))).astype(BF16)

    blk = pl.BlockSpec((tm, tn), lambda i, j: (i, j))
    return pl.pallas_call(
        body, name="dact", grid=(S // tm, D_FF // tn),
        in_specs=[pl.BlockSpec((tm, D), lambda i, j: (i, 0)), pl.BlockSpec((tn, D), lambda i, j: (j, 0)), blk],
        out_specs=blk, out_shape=jax.ShapeDtypeStruct((S, D_FF), BF16),
        compiler_params=_params("arbitrary", "arbitrary"),
    )(dh2b, w2, a)


def _dmerge(dh1b, w_out, pg, y3, wg, wf, wm):
    S = dh1b.shape[0]
    tm = _tile(S, 256)

    def body(d_ref, w_ref, pg_ref, y_ref, wg_ref, wf_ref, wm_ref, *outs):
        dy_refs, do_refs, dg_ref = outs[0:3], outs[3:6], outs[6]
        dm = _nt(d_ref[...], w_ref[...])
        for i, wo_ref in enumerate((wg_ref, wf_ref, wm_ref)):
            gt = jax.nn.sigmoid(pg_ref[:, D * i:D * (i + 1)])
            dy = (dm * gt).astype(BF16)
            dy_refs[i][...] = dy
            do_refs[i][...] = _nt(dy, wo_ref[...])
            dg_ref[:, D * i:D * (i + 1)] = (dm * y_ref[i] * (gt * (1.0 - gt))).astype(BF16)

    row = pl.BlockSpec((tm, D), lambda i: (i, 0))
    half = pl.BlockSpec((tm, 512), lambda i: (i, 0))
    w_spec = pl.BlockSpec((512, D), lambda i: (0, 0))
    return pl.pallas_call(
        body, name="dmerge", grid=(S // tm,),
        in_specs=[row, pl.BlockSpec((D, D), lambda i: (0, 0)), pl.BlockSpec((tm, G_W), lambda i: (i, 0)),
                  pl.BlockSpec((3, tm, D), lambda i: (0, i, 0)), w_spec, w_spec, w_spec],
        out_specs=[row, row, row, half, half, half, pl.BlockSpec((tm, G_W), lambda i: (i, P_GT // G_W))],
        out_shape=[jax.ShapeDtypeStruct((S, D), BF16)] * 3 + [jax.ShapeDtypeStruct((S, 512), F32)] * 3
        + [jax.ShapeDtypeStruct((S, P_W), BF16)],
        compiler_params=_params("arbitrary"),
    )(dh1b, w_out, pg, y3, wg, wf, wm)


def _gla_block_terms(gq_ref, gk_ref, ps_ref, wau_ref, ba_ref, tb):
    gaff = ps_ref[:, 512:640]
    z = _nn(gaff.astype(BF16), wau_ref[...]) + ba_ref[...]
    la = _log_sigmoid(z) * GLA_TAU_INV
    rr = lax.broadcasted_iota(jnp.int32, (tb, tb), 0)
    cc = lax.broadcasted_iota(jnp.int32, (tb, tb), 1)
    same = jnp.right_shift(rr, 6) == jnp.right_shift(cc, 6)
    tri = jnp.where(same & (cc <= rr), 1.0, 0.0).astype(F32)
    ones = jnp.where(same, 1.0, 0.0).astype(F32)
    b = jnp.dot(tri, la, preferred_element_type=F32, precision=HIGHEST)
    bl = jnp.dot(ones, la, preferred_element_type=F32, precision=HIGHEST)
    e_pos, e_neg, e_last, dec = jnp.exp(b), jnp.exp(-b), jnp.exp(bl - b), jnp.exp(bl)
    q = gq_ref[...].astype(F32) * GLA_SCALE
    k = gk_ref[...].astype(F32)
    return dict(gaff=gaff, z=z, same=same, rr=rr, cc=cc, ones=ones, e_pos=e_pos, e_neg=e_neg, e_last=e_last, dec=dec,
                qp=q * e_pos, qn=q * e_neg, kn=k * e_neg, kp=k * e_pos, kd=k * e_last)


def _head_masked(x, store):
    lane = lax.broadcasted_iota(jnp.int32, x.shape, 1)
    for h in range(GLA_H):
        store[h] = jnp.where(jnp.right_shift(lane, 6) == h, x, 0.0).astype(BF16)


def _gla_fwd(pa, ps, wau, ba, gh):
    S = pa.shape[0]
    tb = _tile(S, 512)
    n_c = tb // CHUNK

    def body(gq_ref, gk_ref, gv_ref, ps_ref, wau_ref, ba_ref, gh_ref, o_ref, og_ref, sp_ref,
             qpm, qnm, kdm, kn_s, kp_s, dec_s, state):
        @pl.when(pl.program_id(0) == 0)
        def _():
            state[...] = jnp.zeros_like(state)

        t = _gla_block_terms(gq_ref, gk_ref, ps_ref, wau_ref, ba_ref, tb)
        _head_masked(t["qp"], qpm)
        _head_masked(t["qn"], qnm)
        _head_masked(t["kd"], kdm)
        kn_s[...] = t["kn"].astype(BF16)
        kp_s[...] = t["kp"].astype(BF16)
        dec_s[...] = t["dec"]
        lower = lax.broadcasted_iota(jnp.int32, (CHUNK, CHUNK), 0) >= lax.broadcasted_iota(jnp.int32, (CHUNK, CHUNK), 1)

        def chunk(c, carry):
            r0 = pl.multiple_of(c * CHUNK, CHUNK)
            rows = pl.ds(r0, CHUNK)
            sp = state[...]
            sp_ref[c] = sp
            spb = sp.astype(BF16)
            knc, kpc = kn_s[rows, :], kp_s[rows, :]
            new = sp * dec_s[pl.ds(r0, 1), :]
            for h in range(GLA_H):
                cols = slice(GLA_DV * h, GLA_DV * (h + 1))
                qpc, qnc = qpm[h, rows, :], qnm[h, rows, :]
                attn = jnp.where(lower, _nt(qpc, knc), _nt(qnc, kpc)).astype(BF16)
                vh = gv_ref[rows, cols]
                o_ref[rows, cols] = _nn(attn, vh) + _nt(qpc, spb)
                new = new + _tn(vh, kdm[h, rows, :])
            state[...] = new
            return carry

        lax.fori_loop(0, n_c, chunk, 0)
        for h in range(GLA_H):
            cols = slice(GLA_DV * h, GLA_DV * (h + 1))
            r, xh = _rms(o_ref[:, cols])
            gg = ps_ref[:, cols]
            og_ref[:, cols] = ((xh * gh_ref[:, cols]) * (gg * jax.nn.sigmoid(gg))).astype(BF16)

    return pl.pallas_call(
        body, name="gla_fwd", grid=(S // tb,),
        in_specs=[pl.BlockSpec((tb, GLA_K), lambda i: (i, 0)), pl.BlockSpec((tb, GLA_K), lambda i: (i, 1)),
                  pl.BlockSpec((tb, GLA_V), lambda i: (i, 1)), pl.BlockSpec((tb, S_W), lambda i: (i, 0)),
                  pl.BlockSpec((128, GLA_K), lambda i: (0, 0)), pl.BlockSpec((1, GLA_K), lambda i: (0, 0)),
                  pl.BlockSpec((1, GLA_V), lambda i: (0, 0))],
        out_specs=[pl.BlockSpec((tb, GLA_V), lambda i: (i, 0)), pl.BlockSpec((tb, GLA_V), lambda i: (i, 0)),
                   pl.BlockSpec((n_c, GLA_DV, GLA_K), lambda i: (i, 0, 0))],
        out_shape=[jax.ShapeDtypeStruct((S, GLA_V), F32), jax.ShapeDtypeStruct((S, GLA_V), BF16),
                   jax.ShapeDtypeStruct((S // CHUNK, GLA_DV, GLA_K), F32)],
        scratch_shapes=[pltpu.VMEM((GLA_H, tb, GLA_K), BF16), pltpu.VMEM((GLA_H, tb, GLA_K), BF16),
                        pltpu.VMEM((GLA_H, tb, GLA_K), BF16), pltpu.VMEM((tb, GLA_K), BF16), pltpu.VMEM((tb, GLA_K), BF16),
                        pltpu.VMEM((tb, GLA_K), F32), pltpu.VMEM((GLA_DV, GLA_K), F32)],
        compiler_params=_params("arbitrary"),
    )(pa, pa, pa, ps, wau, ba, gh)


def _gla_bwd(pa, ps, wau, ba, gh, o_gla, d_og, sprev, dgaff_fox, d_proj):
    S = pa.shape[0]
    tb = _tile(S, 512)
    n_c = tb // CHUNK
    n_b = S // tb
    c_gk, c_gv, c_gg, c_ga, c_end = GLA_K, 2 * GLA_K, 2 * GLA_K + GLA_V, 2 * GLA_K + 2 * GLA_V, 2 * GLA_K + 2 * GLA_V + 128

    def body(gq_ref, gk_ref, gv_ref, ps_ref, wau_ref, ba_ref, gh_ref, o_ref, dog_ref, sp_ref, dfx_ref, _,
             dp_ref, dwau_ref, dba_ref, dgh_ref,
             qpm, qnm, kdm, kn_s, kp_s, dec_s, do_s, dqp_s, dqn_s, dkn_s, dkp_s, dkd_s, ddec_s, dstate):
        first = pl.program_id(0) == 0
        dp_ref[:, c_end:] = jnp.zeros((tb, P_GLA_W - c_end), BF16)

        @pl.when(first)
        def _():
            dstate[...] = jnp.zeros_like(dstate)

        t = _gla_block_terms(gq_ref, gk_ref, ps_ref, wau_ref, ba_ref, tb)
        _head_masked(t["qp"], qpm)
        _head_masked(t["qn"], qnm)
        _head_masked(t["kd"], kdm)
        kn_s[...] = t["kn"].astype(BF16)
        kp_s[...] = t["kp"].astype(BF16)
        dec_s[...] = t["dec"]

        dgh_parts = []
        for h in range(GLA_H):
            cols = slice(GLA_DV * h, GLA_DV * (h + 1))
            r, xh = _rms(o_ref[:, cols])
            g = gh_ref[:, cols]
            gg = ps_ref[:, cols]
            sg = jax.nn.sigmoid(gg)
            d_out = dog_ref[:, cols]
            dp_ref[:, c_gg + GLA_DV * h:c_gg + GLA_DV * (h + 1)] = (d_out * (xh * g) * (sg * (1.0 + gg * (1.0 - sg)))).astype(BF16)
            d_on = d_out * (gg * sg)
            dgh_parts.append(_sum8(d_on * xh))
            do_s[:, cols] = _rms_bwd(d_on, g, r, xh).astype(BF16)
        dgh_part = jnp.concatenate(dgh_parts, axis=1)

        lower = lax.broadcasted_iota(jnp.int32, (CHUNK, CHUNK), 0) >= lax.broadcasted_iota(jnp.int32, (CHUNK, CHUNK), 1)
        lane = lax.broadcasted_iota(jnp.int32, (CHUNK, GLA_K), 1)

        def chunk(j, carry):
            c = n_c - 1 - j
            r0 = pl.multiple_of(c * CHUNK, CHUNK)
            rows = pl.ds(r0, CHUNK)
            ds_next = dstate[...]
            dsb = ds_next.astype(BF16)
            sp = sp_ref[c]
            spb = sp.astype(BF16)
            knc, kpc = kn_s[rows, :], kp_s[rows, :]
            dec_row = dec_s[pl.ds(r0, 1), :]
            ddec_s[rows, :] = jnp.broadcast_to(jnp.sum(ds_next * sp, axis=0, keepdims=True), (CHUNK, GLA_K))
            new = ds_next * dec_row
            dqp = jnp.zeros((CHUNK, GLA_K), F32)
            dqn, dkn, dkp, dkd = dqp, dqp, dqp, dqp
            for h in range(GLA_H):
                cols = slice(GLA_DV * h, GLA_DV * (h + 1))
                mine = jnp.right_shift(lane, 6) == h
                qpc, qnc, kdc = qpm[h, rows, :], qnm[h, rows, :], kdm[h, rows, :]
                vh = gv_ref[rows, cols]
                doh = do_s[rows, cols]
                attn = jnp.where(lower, _nt(qpc, knc), _nt(qnc, kpc)).astype(BF16)
                da = _nt(doh, vh)
                dac = jnp.where(lower, da, 0.0).astype(BF16)
                daa = jnp.where(lower, 0.0, da).astype(BF16)
                dqp = dqp + jnp.where(mine, _nn(dac, knc) + _nn(doh, spb), 0.0)
                dqn = dqn + jnp.where(mine, _nn(daa, kpc), 0.0)
                dkn = dkn + _tn(dac, qpc)
                dkp = dkp + _tn(daa, qnc)
                dkd = dkd + jnp.where(mine, _nn(vh, dsb), 0.0)
                dp_ref[rows, c_gv + GLA_DV * h:c_gv + GLA_DV * (h + 1)] = (_tn(attn, doh) + _nt(kdc, dsb)).astype(BF16)
                new = new + _tn(doh, qpc)
            dqp_s[rows, :] = dqp
            dqn_s[rows, :] = dqn
            dkn_s[rows, :] = dkn
            dkp_s[rows, :] = dkp
            dkd_s[rows, :] = dkd
            dstate[...] = new
            return carry

        lax.fori_loop(0, n_c, chunk, 0)

        dqp, dqn, dkn, dkp, dkd = dqp_s[...], dqn_s[...], dkn_s[...], dkp_s[...], dkd_s[...]
        dp_ref[:, 0:c_gk] = ((dqp * t["e_pos"] + dqn * t["e_neg"]) * GLA_SCALE).astype(BF16)
        dp_ref[:, c_gk:c_gv] = (dkn * t["e_neg"] + dkp * t["e_pos"] + dkd * t["e_last"]).astype(BF16)
        kd_term = dkd * t["kd"]
        db = dqp * t["qp"] - dqn * t["qn"] - dkn * t["kn"] + dkp * t["kp"] - kd_term
        upper = jnp.where(t["same"] & (t["cc"] >= t["rr"]), 1.0, 0.0).astype(F32)
        dla = (jnp.dot(upper, db, preferred_element_type=F32, precision=HIGHEST)
               + jnp.dot(t["ones"], kd_term, preferred_element_type=F32, precision=HIGHEST)
               + ddec_s[...] * t["dec"])
        dz = dla * GLA_TAU_INV * jax.nn.sigmoid(-t["z"])
        dzb = dz.astype(BF16)
        dp_ref[:, c_ga:c_end] = (_nt(dzb, wau_ref[...]) + dfx_ref[...]).astype(BF16)
        dwau_part = _tn(t["gaff"].astype(BF16), dzb)
        dba_part = _sum8(dz)

        @pl.when(first)
        def _():
            dwau_ref[...] = dwau_part
            dba_ref[...] = dba_part
            dgh_ref[...] = dgh_part

        @pl.when(jnp.logical_not(first))
        def _():
            dwau_ref[...] += dwau_part
            dba_ref[...] += dba_part
            dgh_ref[...] += dgh_part

    rev = lambda i: (n_b - 1 - i, 0)
    f32k = pltpu.VMEM((tb, GLA_K), F32)
    bf4 = pltpu.VMEM((GLA_H, tb, GLA_K), BF16)
    return pl.pallas_call(
        body, name="gla_bwd", grid=(n_b,),
        in_specs=[pl.BlockSpec((tb, GLA_K), rev), pl.BlockSpec((tb, GLA_K), lambda i: (n_b - 1 - i, 1)),
                  pl.BlockSpec((tb, GLA_V), lambda i: (n_b - 1 - i, 1)), pl.BlockSpec((tb, S_W), rev),
                  pl.BlockSpec((128, GLA_K), lambda i: (0, 0)), pl.BlockSpec((1, GLA_K), lambda i: (0, 0)),
                  pl.BlockSpec((1, GLA_V), lambda i: (0, 0)), pl.BlockSpec((tb, GLA_V), rev), pl.BlockSpec((tb, GLA_V), rev),
                  pl.BlockSpec((n_c, GLA_DV, GLA_K), lambda i: (n_b - 1 - i, 0, 0)), pl.BlockSpec((tb, 128), rev),
                  pl.BlockSpec(memory_space=pl.ANY)],
        out_specs=[pl.BlockSpec((tb, P_GLA_W), lambda i: (n_b - 1 - i, P_GLA // P_GLA_W)),
                   pl.BlockSpec((128, GLA_K), lambda i: (0, 0)), pl.BlockSpec((8, GLA_K), lambda i: (0, 0)),
                   pl.BlockSpec((8, GLA_V), lambda i: (0, 0))],
        out_shape=[jax.ShapeDtypeStruct((S, P_W), BF16), jax.ShapeDtypeStruct((128, GLA_K), F32),
                   jax.ShapeDtypeStruct((8, GLA_K), F32), jax.ShapeDtypeStruct((8, GLA_V), F32)],
        input_output_aliases={11: 0},
        scratch_shapes=[bf4, bf4, bf4, pltpu.VMEM((tb, GLA_K), BF16), pltpu.VMEM((tb, GLA_K), BF16), f32k,
                        pltpu.VMEM((tb, GLA_V), BF16), f32k, f32k, f32k, f32k, f32k, f32k, pltpu.VMEM((GLA_DV, GLA_K), F32)],
        compiler_params=_params("arbitrary"),
    )(pa, pa, pa, ps, wau, ba, gh, o_gla, d_og, sprev, dgaff_fox, d_proj)


def _split3(x):
    x1 = x.astype(BF16).astype(F32)
    x2 = (x - x1).astype(BF16).astype(F32)
    x3 = (x - x1 - x2).astype(BF16).astype(F32)
    return x1, x2, x3


def _fox_prep(pa, ps, bfg):
    S = pa.shape[0]
    tm = _tile(S, 512)

    def body(ps_ref, b_ref, fq_ref, fk_ref, fv_ref, q_ref, k_ref, qt_ref, kt_ref, vt_ref, st_ref, carry):
        @pl.when(pl.program_id(0) == 0)
        def _():
            carry[...] = jnp.zeros_like(carry)

        vt_ref[...] = fv_ref[...].astype(F32).T.astype(BF16)
        lf = _log_sigmoid(ps_ref[...] + b_ref[...])
        rr = lax.broadcasted_iota(jnp.int32, (tm, tm), 0)
        cc = lax.broadcasted_iota(jnp.int32, (tm, tm), 1)
        tri = jnp.where(cc <= rr, 1.0, 0.0).astype(F32)
        f = jnp.dot(tri, lf, preferred_element_type=F32, precision=HIGHEST) + carry[0:1, :]
        carry[...] = jnp.broadcast_to(f[tm - 1:tm, :], carry.shape)
        f1, f2, f3 = _split3(f)
        lane = lax.broadcasted_iota(jnp.int32, (tm, 128), 1)
        st_row = lax.broadcasted_iota(jnp.int32, (8, 128), 0)
        st_lane = lax.broadcasted_iota(jnp.int32, (8, 128), 1)
        stats = jnp.zeros((8, 128), F32)
        for h in range(FOX_H):
            cols = slice(128 * h, 128 * (h + 1))
            c = FF_LANE + h
            a1, a2, a3 = f1[:, c:c + 1], f2[:, c:c + 1], f3[:, c:c + 1]
            q = fq_ref[:, cols].astype(F32) * FOX_SCALE
            k = fk_ref[:, cols].astype(F32)
            fh = f[:, c:c + 1]
            vals = (jnp.max(jnp.sum(q * q, axis=-1, keepdims=True)), jnp.max(jnp.sum(k * k, axis=-1, keepdims=True)),
                    jnp.max(fh), jnp.min(fh), jnp.min(jnp.sum(q * k, axis=-1, keepdims=True)))
            for n, val in enumerate(vals):
                stats = jnp.where((st_row == h) & (st_lane == n), val, stats)
            for n, a in enumerate((a1, a2, a3)):
                q = jnp.where(lane == AUG + n, a, q)
                k = jnp.where(lane == AUG + 3 + n, -a, k)
            q = jnp.where((lane >= AUG + 3) & (lane < AUG + 6), 1.0, q)
            k = jnp.where((lane >= AUG) & (lane < AUG + 3), 1.0, k)
            q_ref[:, cols] = q.astype(BF16)
            k_ref[:, cols] = k.astype(BF16)
            qt_ref[cols, :] = q.T.astype(BF16)
            kt_ref[cols, :] = k.T.astype(BF16)
        st_ref[0] = stats

    wide = lambda j: pl.BlockSpec((tm, 1024), lambda i: (i, j))
    tall = lambda n: pl.BlockSpec((n, tm), lambda i: (0, i))
    return pl.pallas_call(
        body, name="fox_prep", grid=(S // tm,),
        in_specs=[pl.BlockSpec((tm, 128), lambda i: (i, 4)), pl.BlockSpec((1, 128), lambda i: (0, 0)), wide(1), wide(2),
                  pl.BlockSpec((tm, FOX_W), lambda i: (i, A_FV // FOX_W))],
        out_specs=[wide(0), wide(0), tall(1024), tall(1024), tall(FOX_W), pl.BlockSpec((1, 8, 128), lambda i: (i, 0, 0))],
        out_shape=[jax.ShapeDtypeStruct((S, 1024), BF16), jax.ShapeDtypeStruct((S, 1024), BF16),
                   jax.ShapeDtypeStruct((1024, S), BF16), jax.ShapeDtypeStruct((1024, S), BF16),
                   jax.ShapeDtypeStruct((FOX_W, S), BF16), jax.ShapeDtypeStruct((S // tm, 8, 128), F32)],
        scratch_shapes=[pltpu.VMEM((8, 128), F32)],
        compiler_params=_params("arbitrary"),
    )(ps, bfg, pa, pa, pa)


FOX_PRUNE_AT = -90.0


def _fox_live_ranges(stats, n_sub, ratio):
    n_b = stats.shape[0]
    q2, k2, f_max, f_min, own = (stats[:, :, n].T for n in range(5))
    slack = 0.01 * jnp.sqrt(q2 * k2) + 1e-5 * jnp.abs(f_max) + 1.0
    bound = (1.01 * jnp.sqrt(q2[:, :, None] * k2[:, None, :]) + (f_max + slack - own)[:, :, None]
             - (f_min - 1e-5 * jnp.abs(f_min))[:, None, :])
    blocks = jnp.arange(n_b)
    dead = (bound <= FOX_PRUNE_AT) & (blocks[None, :] < blocks[:, None])[None]
    dead_fwd = dead.reshape(FOX_H, n_b // n_sub, n_sub, n_b).all(axis=2)
    first = jnp.sum(jnp.cumprod(dead_fwd.astype(jnp.int32), axis=2), axis=2)
    last_live = n_b - 1 - jnp.sum(jnp.cumprod(dead[:, ::-1, :].astype(jnp.int32), axis=1), axis=1)
    first_wide = blocks // ratio + 1
    narrow_end = jnp.minimum(jnp.minimum(first_wide * ratio, n_b)[None], last_live + 1)
    wide_end = jnp.where(last_live >= (first_wide * ratio)[None], last_live // ratio + 1, first_wide[None])
    return first.astype(jnp.int32), narrow_end.astype(jnp.int32), wide_end.astype(jnp.int32)


def _fox_fwd(qa, ka, vt, first):
    S = qa.shape[0]
    tq = _tile(S, FOX_TQ)
    tk = _tile(tq, FOX_TK)
    n_sub = tq // tk

    def body(first_ref, q_ref, k_ref, vt_ref, o_ref, lse_ref):
        pair, i = pl.program_id(0), pl.program_id(1)
        both = lambda f: tuple(f(hh) for hh in range(2))

        def blk(j, carry, diag, heads=(0, 1)):
            ks = pl.ds(pl.multiple_of(j * tk, tk), tk)
            q0 = 0 if diag is None else diag * tk

            def head(hh):
                if hh not in heads:
                    return carry[hh]
                m, l, acc = carry[hh]
                mo, lo, ao = m[:, q0:], l[:, q0:], acc[:, q0:]
                s = _nt(k_ref[ks, 128 * hh:128 * (hh + 1)], q_ref[q0:, 128 * hh:128 * (hh + 1)])
                if diag is not None:
                    live = lax.broadcasted_iota(jnp.int32, s.shape, 1) >= lax.broadcasted_iota(jnp.int32, s.shape, 0)
                    s = jnp.where(live, s, NEG)
                mn = jnp.maximum(mo, jnp.max(s, axis=0, keepdims=True))
                p = jnp.exp(s - mn)
                al = jnp.exp(mo - mn)
                ln = al * lo + jnp.sum(p, axis=0, keepdims=True)
                an = al * ao + _nn(vt_ref[FOX_DH * hh:FOX_DH * (hh + 1), ks], p.astype(BF16))
                if q0:
                    mn, ln, an = (jnp.concatenate([old[:, :q0], new], axis=1) for old, new in ((m, mn), (l, ln), (acc, an)))
                return mn, ln, an

            return both(head)

        one = (jnp.full((1, tq), NEG, F32), jnp.zeros((1, tq), F32), jnp.zeros((FOX_DH, tq), F32))
        past = i * n_sub
        f0, f1 = first_ref[2 * pair, i], first_ref[2 * pair + 1, i]
        join = jnp.maximum(f0, f1)
        solo = lambda hh: lambda c: lax.fori_loop(jnp.minimum(f0, f1), join, lambda j, cc: blk(j, cc, None, (hh,)), c)
        carry = lax.cond(f0 < f1, solo(0), solo(1), (one, one))
        n_both = past - join
        carry = lax.fori_loop(0, n_both // 2, lambda jj, c: blk(join + 2 * jj + 1, blk(join + 2 * jj, c, None), None), carry)
        carry = lax.cond(n_both % 2 == 1, lambda c: blk(past - 1, c, None), lambda c: c, carry)
        for d in range(n_sub):
            carry = blk(past + d, carry, d)
        (m0, l0, a0), (m1, l1, a1) = carry
        o_ref[...] = jnp.concatenate([a0 / l0, a1 / l1], axis=0).T
        lse_ref[0, 0:1, :] = m0 + jnp.log(l0)
        lse_ref[0, 1:2, :] = m1 + jnp.log(l1)
        lse_ref[0, 2:8, :] = jnp.zeros((6, tq), F32)

    once = pl.Buffered(1)
    return pl.pallas_call(
        body, name="fox_fwd", grid=(FOX_H // 2, S // tq),
        in_specs=[pl.BlockSpec(memory_space=pltpu.SMEM), pl.BlockSpec((tq, 256), lambda p, i: (i, p)),
                  pl.BlockSpec((S, 256), lambda p, i: (0, p), pipeline_mode=once),
                  pl.BlockSpec((128, S), lambda p, i: (p, 0), pipeline_mode=once)],
        out_specs=[pl.BlockSpec((tq, 128), lambda p, i: (i, p)), pl.BlockSpec((1, 8, tq), lambda p, i: (p, 0, i))],
        out_shape=[jax.ShapeDtypeStruct((S, FOX_W), F32), jax.ShapeDtypeStruct((FOX_H // 2, 8, S), F32)],
        compiler_params=_params("arbitrary", "arbitrary"),
    )(first, qa, ka, vt)


def _fox_delta(d_o, o):
    S = o.shape[0]
    tm = _tile(S, 512)

    def body(d_ref, o_ref, db_ref, dbt_ref, dl_ref):
        d = d_ref[...]
        db_ref[...] = d.astype(BF16)
        dbt_ref[...] = d.T.astype(BF16)
        prod = d * o_ref[...]
        rr = lax.broadcasted_iota(jnp.int32, (8, 128), 0)
        cc = lax.broadcasted_iota(jnp.int32, (8, 128), 1)
        ind = jnp.where(jnp.right_shift(cc, 6) == rr, 1.0, 0.0).astype(F32)
        for p in range(FOX_H // 2):
            dl_ref[p] = lax.dot_general(ind, prod[:, 128 * p:128 * (p + 1)], (((1,), (1,)), ((), ())),
                                        preferred_element_type=F32, precision=HIGHEST)

    row = pl.BlockSpec((tm, FOX_W), lambda i: (i, 0))
    return pl.pallas_call(
        body, name="fox_delta", grid=(S // tm,),
        in_specs=[row, row],
        out_specs=[row, pl.BlockSpec((FOX_W, tm), lambda i: (0, i)), pl.BlockSpec((FOX_H // 2, 8, tm), lambda i: (0, 0, i))],
        out_shape=[jax.ShapeDtypeStruct((S, FOX_W), BF16), jax.ShapeDtypeStruct((FOX_W, S), BF16),
                   jax.ShapeDtypeStruct((FOX_H // 2, 8, S), F32)],
        compiler_params=_params("arbitrary"),
    )(d_o, o)


def _fox_bwd(qa, qat, ka, kat, pa, dob, dobt, lse, delta, narrow_end, wide_end):
    S = qa.shape[0]
    tk = _tile(S, 512)
    wide = _tile(S, FOX_TQ)
    ratio = wide // tk
    n_wide = S // wide

    def body(ne_ref, we_ref, q_ref, qt_ref, k_ref, kt_ref, v_ref, do_ref, dot_ref, lse_ref, dl_ref, dq_ref, dk_ref, dv_ref):
        h, jb = pl.program_id(0), pl.program_id(1)
        hh = h % 2

        @pl.when(jb == 0)
        def _():
            dq_ref[...] = jnp.zeros_like(dq_ref)

        lane = lax.broadcasted_iota(jnp.int32, (tk, 128), 1)
        vm = jnp.where(jnp.right_shift(lane, 6) == hh, v_ref[...], jnp.zeros((), BF16))
        kb, ktb = k_ref[...], kt_ref[0:FOX_LIVE, :]
        mine = pl.ds(pl.multiple_of(hh * FOX_DH, FOX_DH), FOX_DH)

        def blk(ib, tq, carry, masked):
            dk, dv = carry
            qs = pl.ds(pl.multiple_of(ib * tq, tq), tq)
            p = jnp.exp(_nt(kb, q_ref[qs, :]) - lse_ref[0, pl.ds(hh, 1), qs])
            if masked:
                live = lax.broadcasted_iota(jnp.int32, p.shape, 1) >= lax.broadcasted_iota(jnp.int32, p.shape, 0)
                p = jnp.where(live, p, 0.0)
            ds = (p * (_nt(vm, do_ref[qs, :]) - dl_ref[0, pl.ds(hh, 1), qs])).astype(BF16)
            dq_ref[0:FOX_LIVE, qs] += _nn(ktb, ds)
            return dk + _nt(qt_ref[0:FOX_LIVE, qs], ds), dv + _nt(dot_ref[mine, qs], p.astype(BF16))

        carry = blk(jb, tk, (jnp.zeros((FOX_LIVE, tk), F32), jnp.zeros((FOX_DH, tk), F32)), True)
        first_wide = jb // ratio + 1
        carry = lax.fori_loop(jb + 1, ne_ref[h, jb], lambda ib, c: blk(ib, tk, c, False), carry)
        last_wide = we_ref[h, jb]
        rest = jnp.maximum(last_wide - first_wide, 0)
        carry = lax.fori_loop(0, rest // 2, lambda t, c: blk(first_wide + 2 * t + 1, wide, blk(first_wide + 2 * t, wide, c, False),
                                                             False), carry)
        dk, dv = lax.cond(rest % 2 == 1, lambda c: blk(last_wide - 1, wide, c, False), lambda c: c, carry)
        dk_ref[0:FOX_LIVE, :] = dk
        dk_ref[FOX_LIVE:, :] = jnp.zeros((128 - FOX_LIVE, tk), F32)
        dv_ref[...] = dv

    once = pl.Buffered(1)
    rows = pl.BlockSpec((1, 8, S), lambda h, j: (h // 2, 0, 0))
    return pl.pallas_call(
        body, name="fox_bwd", grid=(FOX_H, S // tk),
        in_specs=[pl.BlockSpec(memory_space=pltpu.SMEM), pl.BlockSpec(memory_space=pltpu.SMEM),
                  pl.BlockSpec((S, 128), lambda h, j: (0, h), pipeline_mode=once),
                  pl.BlockSpec((128, S), lambda h, j: (h, 0), pipeline_mode=once),
                  pl.BlockSpec((tk, 128), lambda h, j: (j, h)), pl.BlockSpec((128, tk), lambda h, j: (h, j)),
                  pl.BlockSpec((tk, 128), lambda h, j: (j, A_FV // 128 + h // 2)),
                  pl.BlockSpec((S, 128), lambda h, j: (0, h // 2), pipeline_mode=once),
                  pl.BlockSpec((128, S), lambda h, j: (h // 2, 0), pipeline_mode=once), rows, rows],
        out_specs=[pl.BlockSpec((128, S), lambda h, j: (h, 0), pipeline_mode=once),
                   pl.BlockSpec((128, tk), lambda h, j: (h, j)), pl.BlockSpec((FOX_DH, tk), lambda h, j: (h, j))],
        out_shape=[jax.ShapeDtypeStruct((1024, S), F32), jax.ShapeDtypeStruct((1024, S), F32),
                   jax.ShapeDtypeStruct((FOX_W, S), F32)],
        compiler_params=_params("arbitrary", "arbitrary"),
    )(narrow_end, wide_end, qa, qat, ka, kat, pa, dob, dobt, lse, delta)


def _fox_post(dq, dk, dv, ps, bfg, d_proj):
    S = dq.shape[1]
    tm = _tile(S, 512)
    n_b = S // tm

    def body(dq_ref, dk_ref, dv_ref, ps_ref, b_ref, _, dp_ref, dff_ref, dbf_ref, carry):
        first = pl.program_id(0) == 0

        @pl.when(first)
        def _():
            carry[...] = jnp.zeros_like(carry)

        low = lax.broadcasted_iota(jnp.int32, (tm, 128), 1) < FOX_DH
        for h in range(FOX_H):
            blk = slice(128 * h, 128 * (h + 1))
            dp_ref[:, blk] = jnp.where(low, dq_ref[blk, :].T * FOX_SCALE, 0.0).astype(BF16)
            dp_ref[:, 1024 + 128 * h:1024 + 128 * (h + 1)] = jnp.where(low, dk_ref[blk, :].T, 0.0).astype(BF16)
        dp_ref[:, 2048:P_FOX_W] = dv_ref[...].T.astype(BF16)
        rr = lax.broadcasted_iota(jnp.int32, (FOX_H, 1024), 0)
        cc = lax.broadcasted_iota(jnp.int32, (FOX_H, 1024), 1)
        sel_k = jnp.where(cc == 128 * rr + AUG + 3, 1.0, 0.0).astype(F32)
        sel_q = jnp.where(cc == 128 * rr + AUG, 1.0, 0.0).astype(F32)
        g = (jnp.dot(sel_k, dk_ref[...], preferred_element_type=F32, precision=HIGHEST)
             - jnp.dot(sel_q, dq_ref[...], preferred_element_type=F32, precision=HIGHEST))
        t_from = lax.broadcasted_iota(jnp.int32, (tm, tm), 0)
        t_to = lax.broadcasted_iota(jnp.int32, (tm, tm), 1)
        later = jnp.where(t_from >= t_to, 1.0, 0.0).astype(F32)
        dlf = jnp.dot(-g, later, preferred_element_type=F32, precision=HIGHEST) + carry[:, 0:1]
        carry[...] = jnp.broadcast_to(dlf[:, 0:1], carry.shape)
        cols = jnp.concatenate([jnp.zeros((FF_LANE, tm), F32), dlf, jnp.zeros((128 - FF_LANE - FOX_H, tm), F32)], axis=0).T
        dff = cols * jax.nn.sigmoid(-(ps_ref[...] + b_ref[...]))
        dff_ref[...] = dff
        part = _sum8(dff)

        @pl.when(first)
        def _():
            dbf_ref[...] = part

        @pl.when(jnp.logical_not(first))
        def _():
            dbf_ref[...] += part

    rev = lambda i: (n_b - 1 - i, 0)
    tall = lambda n: pl.BlockSpec((n, tm), lambda i: (0, n_b - 1 - i))
    return pl.pallas_call(
        body, name="fox_post", grid=(n_b,),
        in_specs=[tall(1024), tall(1024), tall(FOX_W), pl.BlockSpec((tm, 128), lambda i: (n_b - 1 - i, 4)),
                  pl.BlockSpec((1, 128), lambda i: (0, 0)), pl.BlockSpec(memory_space=pl.ANY)],
        out_specs=[pl.BlockSpec((tm, P_FOX_W), lambda i: (n_b - 1 - i, P_FOX // P_FOX_W)), pl.BlockSpec((tm, 128), rev),
                   pl.BlockSpec((8, 128), lambda i: (0, 0))],
        out_shape=[jax.ShapeDtypeStruct((S, P_W), BF16), jax.ShapeDtypeStruct((S, 128), F32),
                   jax.ShapeDtypeStruct((8, 128), F32)],
        input_output_aliases={5: 0},
        scratch_shapes=[pltpu.VMEM((8, 128), F32)],
        compiler_params=_params("arbitrary"),
    )(dq, dk, dv, ps, bfg, d_proj)


def _mem_prep(mem, g_mem, wkv):
    def body(m_ref, g_ref, w_ref, mn_ref, kv_ref):
        r, xh = _rms(m_ref[...])
        mn = (xh * g_ref[...]).astype(BF16)
        mn_ref[...] = mn
        kv_ref[...] = _nn(mn, w_ref[...]).astype(BF16)

    return pl.pallas_call(
        body, name="mem_prep",
        out_shape=[jax.ShapeDtypeStruct((N_MEM, D), BF16), jax.ShapeDtypeStruct((N_MEM, 2 * MEM_W), BF16)],
        compiler_params=pltpu.CompilerParams(vmem_limit_bytes=V7X_VMEM_LIMIT),
    )(mem, g_mem, wkv)


def _mem_softmax(qh, kh):
    s = _nt(qh, kh) * MEM_SCALE
    e = jnp.exp(s - jnp.max(s, axis=-1, keepdims=True))
    return e / jnp.sum(e, axis=-1, keepdims=True)


def _mem_fwd(pa, mkv):
    S = pa.shape[0]
    tm = _tile(S, 512)

    def body(q_ref, kv_ref, o_ref):
        for h in range(MEM_H):
            cols = slice(MEM_DH * h, MEM_DH * (h + 1))
            p = _mem_softmax(q_ref[:, cols], kv_ref[:, cols])
            o_ref[:, cols] = _nn(p.astype(BF16), kv_ref[:, MEM_W + MEM_DH * h:MEM_W + MEM_DH * (h + 1)])

    return pl.pallas_call(
        body, name="mem_fwd", grid=(S // tm,),
        in_specs=[pl.BlockSpec((tm, MEM_W), lambda i: (i, A_MQ // MEM_W)), pl.BlockSpec((N_MEM, 2 * MEM_W), lambda i: (0, 0))],
        out_specs=pl.BlockSpec((tm, MEM_W), lambda i: (i, 0)),
        out_shape=jax.ShapeDtypeStruct((S, MEM_W), F32),
        compiler_params=_params("arbitrary"),
    )(pa, mkv)


def _mem_bwd(pa, mkv, d_o, d_proj):
    S = pa.shape[0]
    tm = _tile(S, 512)

    def body(q_ref, kv_ref, do_ref, _, dq_ref, dkv_ref):
        first = pl.program_id(0) == 0
        parts = []
        for h in range(MEM_H):
            cols = slice(MEM_DH * h, MEM_DH * (h + 1))
            vcols = slice(MEM_W + MEM_DH * h, MEM_W + MEM_DH * (h + 1))
            qh, kh = q_ref[:, cols], kv_ref[:, cols]
            p = _mem_softmax(qh, kh)
            dob = do_ref[:, cols].astype(BF16)
            dp = _nt(dob, kv_ref[:, vcols])
            ds = (p * (dp - jnp.sum(p * dp, axis=-1, keepdims=True)) * MEM_SCALE).astype(BF16)
            dq_ref[:, cols] = _nn(ds, kh).astype(BF16)
            parts.append((cols, _tn(ds, qh)))
            parts.append((vcols, _tn(p.astype(BF16), dob)))

        @pl.when(first)
        def _():
            for sl, v in parts:
                dkv_ref[:, sl] = v

        @pl.when(jnp.logical_not(first))
        def _():
            for sl, v in parts:
                dkv_ref[:, sl] += v

    return pl.pallas_call(
        body, name="mem_bwd", grid=(S // tm,),
        in_specs=[pl.BlockSpec((tm, MEM_W), lambda i: (i, A_MQ // MEM_W)), pl.BlockSpec((N_MEM, 2 * MEM_W), lambda i: (0, 0)),
                  pl.BlockSpec((tm, MEM_W), lambda i: (i, 0)), pl.BlockSpec(memory_space=pl.ANY)],
        out_specs=[pl.BlockSpec((tm, MEM_W), lambda i: (i, P_MQ // MEM_W)), pl.BlockSpec((N_MEM, 2 * MEM_W), lambda i: (0, 0))],
        out_shape=[jax.ShapeDtypeStruct((S, P_W), BF16), jax.ShapeDtypeStruct((N_MEM, 2 * MEM_W), F32)],
        input_output_aliases={3: 0},
        compiler_params=_params("arbitrary"),
    )(pa, mkv, d_o, d_proj)


def _mem_prep_bwd(mem, g_mem, mn, wkv, dkv):
    def body(m_ref, g_ref, mn_ref, w_ref, d_ref, dw_ref, dg_ref):
        db = d_ref[...].astype(BF16)
        dw_ref[...] = _tn(mn_ref[...], db).astype(BF16)
        r, xh = _rms(m_ref[...])
        dg_ref[...] = _sum8(_nt(db, w_ref[...]) * xh)

    dw, dg = pl.pallas_call(
        body, name="mem_prep_bwd",
        out_shape=[jax.ShapeDtypeStruct((D, 2 * MEM_W), BF16), jax.ShapeDtypeStruct((8, D), F32)],
        compiler_params=pltpu.CompilerParams(vmem_limit_bytes=V7X_VMEM_LIMIT),
    )(mem, g_mem, mn, wkv, dkv)
    return dw.reshape(N_DEV, D // N_DEV, 2 * MEM_W), dg


def _rearrange_w_in(w):
    def heads128(cols):
        blk = w[:, cols:cols + FOX_W].reshape(D, FOX_H, FOX_DH)
        return jnp.pad(blk, ((0, 0), (0, 0), (0, 128 - FOX_DH))).reshape(D, FOX_H * 128)

    fq, fk, fv, mq, wg = heads128(O_FQ), heads128(O_FK), w[:, O_FV:O_FF], w[:, O_MQ:O_GT], w[:, O_GT:]
    gaff = jnp.concatenate([w[:, O_GA:O_FQ], w[:, O_FF:O_MQ], jnp.zeros((D, 128 - GLA_R - FOX_H), w.dtype)], axis=1)
    wa = jnp.concatenate([w[:, O_GQ:O_GG], fq, fk, fv, mq], axis=1)
    ws = jnp.concatenate([w[:, O_GG:O_GA], gaff], axis=1)
    wp = jnp.concatenate([fq, fk, fv, mq, wg, w[:, O_GQ:O_GG], ws, jnp.zeros((D, P_W - P_GLA - 1024 - S_W), w.dtype)], axis=1)
    return wa, wg, ws, wp


def _restore_w_in_grad(dwp):
    def unheads(off):
        return dwp[:, off:off + FOX_H * 128].reshape(D, FOX_H, 128)[:, :, :FOX_DH].reshape(D, FOX_W)

    g0 = P_GLA + 1024
    return jnp.concatenate([
        dwp[:, P_GLA:g0], dwp[:, g0:g0 + 512], dwp[:, g0 + 512:g0 + 512 + GLA_R], unheads(P_FOX), unheads(P_FOX + 1024),
        dwp[:, P_FOX + 2048:P_FOX + P_FOX_W], dwp[:, g0 + 512 + GLA_R:g0 + 512 + GLA_R + FOX_H], dwp[:, P_MQ:P_GT],
        dwp[:, P_GT:P_GLA]], axis=1)


def _local_step(x, mem, target, p):
    S = x.shape[0]
    wa, wg, ws, wp = _rearrange_w_in(p["w_in"])
    wau = jnp.pad(p["w_alpha_up"], ((0, 128 - GLA_R), (0, 0)))
    bfg = jnp.pad(p["b_forget"], ((0, 0), (FF_LANE, 128 - FF_LANE - FOX_H)))
    gh = p["g_gla_head"].reshape(1, GLA_V)

    pa, pg, ps, u = _proj(x, p["g_mix"], wa, wg, ws)
    o_gla, og, sprev = _gla_fwd(pa, ps, wau, p["b_alpha"], gh)
    qa, ka, qat, kat, vt, fox_stats = _fox_prep(pa, ps, bfg)
    fox_tq = _tile(S, FOX_TQ)
    fox_first, fox_narrow_end, fox_wide_end = _fox_live_ranges(fox_stats, fox_tq // _tile(fox_tq, FOX_TK),
                                                               fox_tq // _tile(S, FOX_TK))
    o_fox, lse = _fox_fwd(qa, ka, vt, fox_first)
    mn, mkv = _mem_prep(mem, p["g_mem"], p["w_mem_kv"])
    o_mem = _mem_fwd(pa, mkv)
    y3, mg = _merge(og, o_fox, o_mem, p["w_gla_o"], p["w_fox_o"], p["w_mem_o"], pg)
    h1, u2 = _out_proj(mg, p["w_out"], x, p["g_ffn"])
    a, act = _ff1(u2, p["w_ff1"])
    dh2, dh2b, loss8, dg_final = _ff2_loss(act, p["w_ff2"], h1, p["g_final"].reshape(1, D), target)

    d_a = _dact(dh2b, p["w_ff2"], a)
    dw_ff2 = _wgrad(act, dh2b, "wgrad_ff2", 0)
    dh1, dh1b, dg_ffn = _nt_rmsbwd(d_a, p["w_ff1"], h1, p["g_ffn"], dh2, "dffn", True)
    dw_ff1 = _wgrad(u2, d_a, "wgrad_ff1", 1)
    dy_g, dy_f, dy_m, do_g, do_f, do_m, d_proj = _dmerge(dh1b, p["w_out"], pg, y3, p["w_gla_o"], p["w_fox_o"], p["w_mem_o"])
    dw_out = _wgrad(mg, dh1b, "wgrad_out", 0)
    dw_gla_o = _wgrad(og, dy_g, "wgrad_gla_o", 1)
    dw_fox_o = _wgrad(o_fox, dy_f, "wgrad_fox_o", 1)
    dw_mem_o = _wgrad(o_mem, dy_m, "wgrad_mem_o", 1)
    d_proj, d_mkv = _mem_bwd(pa, mkv, do_m, d_proj)
    dw_mem_kv, dg_mem = _mem_prep_bwd(mem, p["g_mem"], mn, p["w_mem_kv"], d_mkv)
    dob, dobt, delta = _fox_delta(do_f, o_fox)
    dq, dk, dv = _fox_bwd(qa, qat, ka, kat, pa, dob, dobt, lse, delta, fox_narrow_end, fox_wide_end)
    d_proj, dgaff_fox, db_forget = _fox_post(dq, dk, dv, ps, bfg, d_proj)
    d_proj, dw_au, db_alpha, dg_gla = _gla_bwd(pa, ps, wau, p["b_alpha"], gh, o_gla, do_g, sprev, dgaff_fox, d_proj)
    dx, dg_mix = _nt_rmsbwd(d_proj, wp, x, p["g_mix"], dh1, "dmix", False)
    dw_in = _slabs(_restore_w_in_grad(_wgrad(u, d_proj, "wgrad_in")), 1).astype(BF16)

    big = dict(w_in=dw_in, w_mem_kv=dw_mem_kv, w_gla_o=dw_gla_o, w_fox_o=dw_fox_o, w_mem_o=dw_mem_o, w_out=dw_out,
               w_ff1=dw_ff1, w_ff2=dw_ff2)
    small = dict(g_mix=dg_mix, g_mem=dg_mem, g_ffn=dg_ffn, g_final=dg_final, b_alpha=db_alpha, g_gla_head=dg_gla,
                 b_forget=db_forget, w_alpha_up=dw_au, loss=loss8)
    return dx, big, small


BIG = (("w_in", 1), ("w_mem_kv", 0), ("w_gla_o", 1), ("w_fox_o", 1), ("w_mem_o", 1), ("w_out", 0), ("w_ff1", 1), ("w_ff2", 0))


def _peer(d):
    me = lax.axis_index("x") * 4 + lax.axis_index("y") * 2 + lax.axis_index("c")
    t = (me + d) % N_DEV
    return (t // 4, (t // 2) % 2, t % 2), me


def _exchange_call(body, blocks, out_shape, name):
    n = len(blocks)
    any_spec = pl.BlockSpec(memory_space=pl.ANY)
    return pl.pallas_call(
        body, name=name, in_specs=[any_spec] * n, out_specs=[any_spec] * n, out_shape=out_shape,
        scratch_shapes=[pltpu.SemaphoreType.DMA((n, N_DEV - 1)), pltpu.SemaphoreType.DMA((n, N_DEV - 1)),
                        pltpu.SemaphoreType.DMA((n,))],
    )(*blocks)


def _scatter_grads(slabs):
    n = len(slabs)

    def body(*refs):
        ins, outs = refs[:n], refs[n:2 * n]
        send, recv, loc = refs[2 * n:]
        _, me = _peer(0)
        copies = [pltpu.make_async_copy(ins[k].at[me], outs[k].at[me], loc.at[k]) for k in range(n)]
        for d in range(1, N_DEV):
            to, _ = _peer(d)
            copies += [pltpu.make_async_remote_copy(
                src_ref=ins[k].at[(me + d) % N_DEV], dst_ref=outs[k].at[me], send_sem=send.at[k, d - 1],
                recv_sem=recv.at[k, d - 1], device_id=to, device_id_type=MESH) for k in range(n)]
        for cp in copies:
            cp.start()
        for cp in copies[n:]:
            cp.wait_send()
        for cp in copies[n:]:
            cp.wait_recv()
        for cp in copies[:n]:
            cp.wait()

    return _exchange_call(body, slabs, [jax.ShapeDtypeStruct(b.shape, b.dtype) for b in slabs], "scatter_grads")


def _gather_weights(shards):
    n = len(shards)

    def body(*refs):
        ins, outs = refs[:n], refs[n:2 * n]
        send, recv, loc = refs[2 * n:]
        x, y, c = lax.axis_index("x"), lax.axis_index("y"), lax.axis_index("c")
        sibling = (x, y, 1 - c)
        chips = [(1 - x, y), (x, 1 - y), (1 - x, 1 - y)]
        slot = lambda px, py, pc: px * 4 + py * 2 + pc

        def copy(k, s, block, to, src=None):
            rows = outs[k].at[slot(*block)]
            return pltpu.make_async_remote_copy(src_ref=rows if src is None else src, dst_ref=rows, send_sem=send.at[k, s],
                                                recv_sem=recv.at[k, s], device_id=to, device_id_type=MESH)

        me = (x, y, c)
        own = [pltpu.make_async_copy(ins[k], outs[k].at[slot(*me)], loc.at[k]) for k in range(n)]
        first = [copy(k, 0, me, sibling, src=ins[k]) for k in range(n)]
        first += [copy(k, 1 + j, me, (*chip, c), src=ins[k]) for j, chip in enumerate(chips) for k in range(n)]
        for cp in own + first:
            cp.start()
        passed = []
        for j, chip in enumerate(chips):
            for k in range(n):
                copy(k, 1 + j, (*chip, c), me).wait_recv()
                fwd = copy(k, 4 + j, (*chip, c), sibling)
                fwd.start()
                passed.append(fwd)
        for k in range(n):
            copy(k, 0, sibling, me).wait_recv()
        for j, chip in enumerate(chips):
            for k in range(n):
                copy(k, 4 + j, (*chip, 1 - c), me).wait_recv()
        for cp in first + passed:
            cp.wait_send()
        for cp in own:
            cp.wait()

    return _exchange_call(body, shards, [jax.ShapeDtypeStruct((N_DEV,) + b.shape, b.dtype) for b in shards], "gather_weights")


def _adamw_math(g, w, m, v):
    m2 = ADAM_B1 * m + (1.0 - ADAM_B1) * g
    v2 = ADAM_B2 * v + (1.0 - ADAM_B2) * jnp.square(g)
    m_hat = m2 / (1.0 - ADAM_B1 ** ADAM_STEP)
    v_hat = v2 / (1.0 - ADAM_B2 ** ADAM_STEP)
    delta = -ADAM_LR * (m_hat / (jnp.sqrt(v_hat) + ADAM_EPS) + ADAM_WD * w)
    return delta, m2, v2


def _adamw_sum(parts, w, m, v, name):
    R, C = w.shape
    tr = _tile(R, 128)

    def body(p_ref, w_ref, m_ref, v_ref, g_ref, d_ref, m2_ref, v2_ref):
        g = p_ref[0].astype(F32)
        for j in range(1, p_ref.shape[0]):
            g = g + p_ref[j].astype(F32)
        g_ref[...] = g
        d_ref[...], m2_ref[...], v2_ref[...] = _adamw_math(g, w_ref[...], m_ref[...], v_ref[...])

    blk = pl.BlockSpec((tr, C), lambda i: (i, 0))
    return pl.pallas_call(
        body, name=name, grid=(R // tr,),
        in_specs=[pl.BlockSpec((parts.shape[0], tr, C), lambda i: (0, i, 0)), blk, blk, blk],
        out_specs=[blk] * 4, out_shape=[jax.ShapeDtypeStruct((R, C), F32)] * 4,
        compiler_params=_params("arbitrary"),
    )(parts, w, m, v)


SMALL_ROWS = 24


def _pack_small(d):
    mixed = jnp.concatenate([d["b_alpha"].reshape(1, GLA_K), d["g_gla_head"].reshape(1, GLA_V),
                             jnp.pad(d["b_forget"].reshape(1, FOX_H), ((0, 0), (FF_LANE, 128 - FF_LANE - FOX_H))),
                             jnp.zeros((1, 128), F32)], axis=1)
    rows = [d["g_mix"].reshape(1, D), d["g_mem"].reshape(1, D), d["g_ffn"].reshape(1, D), d["g_final"].reshape(1, D), mixed,
            jnp.zeros((3, D), F32), jnp.pad(d["w_alpha_up"].reshape(GLA_R, GLA_K), ((0, 0), (0, D - GLA_K)))]
    return jnp.concatenate(rows, axis=0)


def _unpack_small(t):
    return dict(g_mix=t[0:1], g_mem=t[1:2], g_ffn=t[2:3], g_final=t[3], b_alpha=t[4:5, 0:GLA_K],
                g_gla_head=t[4:5, GLA_K:GLA_K + GLA_V].reshape(1, GLA_H, GLA_DV),
                b_forget=t[4:5, 768 + FF_LANE:768 + FF_LANE + FOX_H], w_alpha_up=t[8:24, 0:GLA_K].reshape(1, GLA_R, GLA_K))


def _small_allreduce(small, w, m, v):
    def body(gm, gme, gf, gfi, ba, gg, bf, wau, ls, w_ref, m_ref, v_ref, g_ref, d_ref, m2_ref, v2_ref, l_ref,
             buf, send, recv):
        _, me = _peer(0)
        buf[me] = jnp.zeros((SMALL_ROWS, D), F32)
        for r, ref in enumerate((gm, gme, gf, gfi)):
            buf[me, r:r + 1, :] = jnp.sum(ref[...], axis=0, keepdims=True)
        buf[me, 4:5, 0:GLA_K] = jnp.sum(ba[...], axis=0, keepdims=True)
        buf[me, 4:5, GLA_K:GLA_K + GLA_V] = jnp.sum(gg[...], axis=0, keepdims=True)
        buf[me, 4:5, 768:896] = jnp.sum(bf[...], axis=0, keepdims=True)
        lrow = jnp.sum(ls[...], axis=0, keepdims=True)
        lsum = lrow[:, 0:128]
        for c in range(1, D // 128):
            lsum = lsum + lrow[:, 128 * c:128 * (c + 1)]
        buf[me, 4:5, 896:1024] = lsum
        buf[me, 8:24, 0:GLA_K] = wau[0:GLA_R, :]
        remote = []
        for d in range(1, N_DEV):
            to, me = _peer(d)
            cp = pltpu.make_async_remote_copy(src_ref=buf.at[me], dst_ref=buf.at[me], send_sem=send.at[d - 1],
                                              recv_sem=recv.at[d - 1], device_id=to, device_id_type=MESH)
            cp.start()
            remote.append(cp)
        for cp in remote:
            cp.wait_send()
        for cp in remote:
            cp.wait_recv()
        g = buf[0]
        for j in range(1, N_DEV):
            g = g + buf[j]
        g_ref[...] = g
        d_ref[...], m2_ref[...], v2_ref[...] = _adamw_math(g, w_ref[...], m_ref[...], v_ref[...])
        l_ref[...] = g[4:5, 896:1024]

    packed = jax.ShapeDtypeStruct((SMALL_ROWS, D), F32)
    return pl.pallas_call(
        body, name="small_allreduce",
        out_shape=[packed, packed, packed, packed, jax.ShapeDtypeStruct((1, 128), F32)],
        scratch_shapes=[pltpu.VMEM((N_DEV, SMALL_ROWS, D), F32), pltpu.SemaphoreType.DMA((N_DEV - 1,)),
                        pltpu.SemaphoreType.DMA((N_DEV - 1,))],
    )(small["g_mix"], small["g_mem"], small["g_ffn"], small["g_final"], small["b_alpha"], small["g_gla_head"],
      small["b_forget"], small["w_alpha_up"], small["loss"], w, m, v)


def _slabs(g, axis):
    R, C = g.shape
    if axis == 0:
        return g.reshape(N_DEV, R // N_DEV, C)
    return g.reshape(R, N_DEV, C // N_DEV).transpose(1, 0, 2)


def _unslab(t, axis):
    n, r, c = t.shape
    if axis == 0:
        return t.reshape(n * r, c)
    return t.transpose(1, 0, 2).reshape(r, n * c)


def kernel(x, mem, g_mix, w_in, w_alpha_up, b_alpha, b_forget, g_gla_head, g_mem, w_mem_kv, w_gla_o, w_fox_o, w_mem_o, w_out, g_ffn, w_ff1, w_ff2, g_final, loss_target, m_g_mix, m_w_in, m_w_alpha_up, m_b_alpha, m_b_forget, m_g_gla_head, m_g_mem, m_w_mem_kv, m_w_gla_o, m_w_fox_o, m_w_mem_o, m_w_out, m_g_ffn, m_w_ff1, m_w_ff2, m_g_final, v_g_mix, v_w_in, v_w_alpha_up, v_b_alpha, v_b_forget, v_g_gla_head, v_g_mem, v_w_mem_kv, v_w_gla_o, v_w_fox_o, v_w_mem_o, v_w_out, v_g_ffn, v_w_ff1, v_w_ff2, v_g_final):
    names = ["g_mix", "w_in", "w_alpha_up", "b_alpha", "b_forget", "g_gla_head", "g_mem", "w_mem_kv", "w_gla_o", "w_fox_o",
             "w_mem_o", "w_out", "g_ffn", "w_ff1", "w_ff2", "g_final"]
    w = dict(g_mix=g_mix, w_in=w_in, w_alpha_up=w_alpha_up, b_alpha=b_alpha, b_forget=b_forget, g_gla_head=g_gla_head,
             g_mem=g_mem, w_mem_kv=w_mem_kv, w_gla_o=w_gla_o, w_fox_o=w_fox_o, w_mem_o=w_mem_o, w_out=w_out, g_ffn=g_ffn,
             w_ff1=w_ff1, w_ff2=w_ff2, g_final=g_final)
    m = dict(g_mix=m_g_mix, w_in=m_w_in, w_alpha_up=m_w_alpha_up, b_alpha=m_b_alpha, b_forget=m_b_forget,
             g_gla_head=m_g_gla_head, g_mem=m_g_mem, w_mem_kv=m_w_mem_kv, w_gla_o=m_w_gla_o, w_fox_o=m_w_fox_o,
             w_mem_o=m_w_mem_o, w_out=m_w_out, g_ffn=m_g_ffn, w_ff1=m_w_ff1, w_ff2=m_w_ff2, g_final=m_g_final)
    v = dict(g_mix=v_g_mix, w_in=v_w_in, w_alpha_up=v_w_alpha_up, b_alpha=v_b_alpha, b_forget=v_b_forget,
             g_gla_head=v_g_gla_head, g_mem=v_g_mem, w_mem_kv=v_w_mem_kv, w_gla_o=v_w_gla_o, w_fox_o=v_w_fox_o,
             w_mem_o=v_w_mem_o, w_out=v_w_out, g_ffn=v_g_ffn, w_ff1=v_w_ff1, w_ff2=v_w_ff2, g_final=v_g_final)
    me = lax.axis_index("x") * 4 + lax.axis_index("y") * 2 + lax.axis_index("c")

    shards = [w[n][0].astype(BF16) for n, _ in BIG] + [w["w_alpha_up"][0].astype(BF16)]
    gathered = _gather_weights(shards)
    p = {n: _unslab(t, ax) for (n, ax), t in zip(BIG, gathered[:-1])}
    p["w_alpha_up"] = _unslab(gathered[-1], 1)
    p.update(g_mix=g_mix, b_alpha=b_alpha, b_forget=b_forget, g_gla_head=g_gla_head, g_mem=g_mem, g_ffn=g_ffn, g_final=g_final)

    dx, big, small = _local_step(x[0], mem[0], loss_target[0], p)

    recv = _scatter_grads([big[n] for n, _ in BIG])
    out_g, out_d, out_m, out_v = {}, {}, {}, {}
    for (n, ax), parts in zip(BIG, recv):
        g_, d_, m_, v_ = _adamw_sum(parts, w[n][0], m[n][0], v[n][0], "adamw_" + n)
        out_g[n], out_d[n], out_m[n], out_v[n] = g_[None], d_[None], m_[None], v_[None]

    full = lambda d: dict(d, w_alpha_up=jnp.zeros((1, GLA_R, GLA_K), F32))
    gs, ds, ms, vs, lrow = _small_allreduce(small, _pack_small(full(w)), _pack_small(full(m)), _pack_small(full(v)))
    g_s, d_s, m_s, v_s = _unpack_small(gs), _unpack_small(ds), _unpack_small(ms), _unpack_small(vs)
    for n in names:
        if n not in out_g and n != "w_alpha_up":
            out_g[n], out_d[n], out_m[n], out_v[n] = g_s[n], d_s[n], m_s[n], v_s[n]
    g_au = lax.dynamic_slice_in_dim(g_s["w_alpha_up"][0], me * (GLA_K // N_DEV), GLA_K // N_DEV, axis=1)
    g_, d_, m_, v_ = _adamw_sum(g_au[None], w_alpha_up[0], m_w_alpha_up[0], v_w_alpha_up[0], "adamw_w_alpha_up")
    out_g["w_alpha_up"], out_d["w_alpha_up"], out_m["w_alpha_up"], out_v["w_alpha_up"] = g_[None], d_[None], m_[None], v_[None]

    loss = jnp.sum(lrow) * (0.5 / D)
    return (loss, dx[None], *[out_g[n] for n in names], *[out_d[n] for n in names], *[out_m[n] for n in names],
            *[out_v[n] for n in names])
```

```python
import jax
import jax.numpy as jnp
from jax import lax
from jax.experimental import pallas as pl
from jax.experimental.pallas import tpu as pltpu

F32, BF16 = jnp.float32, jnp.bfloat16
HIGHEST = lax.Precision.HIGHEST
MESH = pl.DeviceIdType.MESH

N_DEV = 8
D = 1024
EPS = 1e-6
CHUNK = 64
N_MEM = 256
GLA_H, GLA_DK, GLA_DV = 4, 64, 128
GLA_K, GLA_V, GLA_R = 256, 512, 16
FOX_H, FOX_DH, FOX_W = 8, 64, 512
MEM_H, MEM_DH, MEM_W = 4, 128, 512
D_FF = 4096
D_IN = 6680
FOX_SCALE = 0.125
GLA_SCALE = 0.125
MEM_SCALE = MEM_DH ** -0.5
GLA_TAU_INV = 1.0 / 16.0
NEG = -1e30

O_GQ, O_GK, O_GV, O_GG, O_GA, O_FQ, O_FK, O_FV, O_FF, O_MQ, O_GT = 0, 256, 512, 1024, 1536, 1552, 2064, 2576, 3088, 3096, 3608
A_FQ, A_FK, A_FV, A_MQ, A_W = 1024, 2048, 3072, 3584, 4096
S_W = 640
G_W = 3072
P_FOX, P_FOX_W, P_MQ, P_GT, P_GLA, P_GLA_W, P_W = 0, 2560, 2560, 3072, 6144, 2048, 8192
FF_LANE = 16
AUG = 64
FOX_LIVE = 80

ADAM_LR, ADAM_B1, ADAM_B2, ADAM_EPS, ADAM_WD, ADAM_STEP = 0.001, 0.9, 0.999, 1e-08, 0.01, 10
V7X_VMEM_LIMIT = 48 * 1024 * 1024
FOX_TQ, FOX_TK = 2048, 512


def _params(*sem):
    return pltpu.CompilerParams(dimension_semantics=sem, vmem_limit_bytes=V7X_VMEM_LIMIT)


def _nt(a, b):
    return lax.dot_general(a, b, (((1,), (1,)), ((), ())), preferred_element_type=F32)


def _tn(a, b):
    return lax.dot_general(a, b, (((0,), (0,)), ((), ())), preferred_element_type=F32)


def _nn(a, b):
    return jnp.dot(a, b, preferred_element_type=F32)


def _log_sigmoid(z):
    return jnp.minimum(z, 0.0) - jnp.log(1.0 + jnp.exp(-jnp.abs(z)))


def _sum01(m01, x):
    x1 = x.astype(BF16)
    x2 = (x - x1.astype(F32)).astype(BF16)
    x3 = (x - x1.astype(F32) - x2.astype(F32)).astype(BF16)
    return _nn(m01, x1) + _nn(m01, x2) + _nn(m01, x3)


def _sum8(x):
    return x.reshape(x.shape[0] // 8, 8, x.shape[1]).sum(axis=0)


def _rms(xv):
    r = lax.rsqrt(jnp.mean(xv * xv, axis=-1, keepdims=True) + EPS)
    return r, xv * r


def _rms_bwd(du, g, r, xh):
    w = du * g
    return r * (w - xh * jnp.mean(w * xh, axis=-1, keepdims=True))


def _row_chunks(n, size=256):
    return [slice(r, r + min(size, n)) for r in range(0, n, min(size, n))]


def _tile(n, pref):
    t = min(n, pref)
    assert n % t == 0, (n, t)
    return t


def _proj(x, g, wa, wg, ws):
    S = x.shape[0]
    tm, tn = _tile(S, 1024), 1024
    n_a, n_g = A_W // tn, G_W // tn

    def body(x_ref, g_ref, wa_ref, wg_ref, ws_ref, pa_ref, pg_ref, ps_ref, u_ref, u_s):
        j = pl.program_id(1)

        @pl.when(j == 0)
        def _():
            r, xh = _rms(x_ref[...])
            u_s[...] = (xh * g_ref[...]).astype(BF16)
            u_ref[...] = u_s[...]

        @pl.when(j < n_a)
        def _():
            pa_ref[...] = _nn(u_s[...], wa_ref[...]).astype(BF16)

        @pl.when((j >= n_a) & (j < n_a + n_g))
        def _():
            pg_ref[...] = _nn(u_s[...], wg_ref[...])

        @pl.when(j == n_a + n_g)
        def _():
            ps_ref[...] = _nn(u_s[...], ws_ref[...])

    in_a = lambda j: jnp.minimum(j, n_a - 1)
    in_g = lambda j: jnp.clip(j - n_a, 0, n_g - 1)
    row = pl.BlockSpec((tm, D), lambda i, j: (i, 0))
    return pl.pallas_call(
        body, name="proj", grid=(S // tm, n_a + n_g + 1),
        in_specs=[row, pl.BlockSpec((1, D), lambda i, j: (0, 0)), pl.BlockSpec((D, tn), lambda i, j: (0, in_a(j))),
                  pl.BlockSpec((D, tn), lambda i, j: (0, in_g(j))),
                  pl.BlockSpec((D, S_W), lambda i, j: (0, 0), pipeline_mode=pl.Buffered(1))],
        out_specs=[pl.BlockSpec((tm, tn), lambda i, j: (i, in_a(j))), pl.BlockSpec((tm, tn), lambda i, j: (i, in_g(j))),
                   pl.BlockSpec((tm, S_W), lambda i, j: (i, 0)), row],
        out_shape=[jax.ShapeDtypeStruct((S, A_W), BF16), jax.ShapeDtypeStruct((S, G_W), F32),
                   jax.ShapeDtypeStruct((S, S_W), F32), jax.ShapeDtypeStruct((S, D), BF16)],
        scratch_shapes=[pltpu.VMEM((tm, D), BF16)],
        compiler_params=_params("arbitrary", "arbitrary"),
    )(x, g, wa, wg, ws)


def _wgrad(a, b, name, slab_axis=None):
    S, Ka = a.shape
    N = b.shape[1]
    tka, tn, ts = _tile(Ka, 1024), _tile(N, 1024), _tile(S, 1024)
    n_s = S // ts
    per = N // N_DEV
    slabs_per_step = tn // per

    def body(a_ref, b_ref, o_ref, acc):
        s = pl.program_id(2)

        @pl.when(s == 0)
        def _():
            acc[...] = jnp.zeros_like(acc)

        acc[...] += _tn(a_ref[...].astype(BF16), b_ref[...].astype(BF16))

        @pl.when(s == n_s - 1)
        def _():
            if slab_axis == 1:
                for q in range(slabs_per_step):
                    o_ref[q] = acc[:, per * q:per * (q + 1)].astype(BF16)
            else:
                o_ref[...] = acc[...].astype(o_ref.dtype)

    if slab_axis == 1:
        out_spec = pl.BlockSpec((slabs_per_step, tka, per), lambda i, j, s: (j, i, 0))
        out_shape = jax.ShapeDtypeStruct((N_DEV, Ka, per), BF16)
    else:
        out_spec = pl.BlockSpec((tka, tn), lambda i, j, s: (i, j))
        out_shape = jax.ShapeDtypeStruct((Ka, N), F32 if slab_axis is None else BF16)
    out = pl.pallas_call(
        body, name=name, grid=(Ka // tka, N // tn, n_s),
        in_specs=[pl.BlockSpec((ts, tka), lambda i, j, s: (s, i)), pl.BlockSpec((ts, tn), lambda i, j, s: (s, j))],
        out_specs=out_spec, out_shape=out_shape,
        scratch_shapes=[pltpu.VMEM((tka, tn), F32)],
        compiler_params=_params("arbitrary", "arbitrary", "arbitrary"),
    )(a, b)
    return out.reshape(N_DEV, Ka // N_DEV, N) if slab_axis == 0 else out


def _nt_rmsbwd(a, w, xin, g, dres, name, emit_bf16, slabs=()):
    S, K = a.shape
    tm, tk = _tile(S, 1024), _tile(K, 1024)
    n_i, n_k = S // tm, K // tk
    n_x, n_o = len(slabs), 3 if emit_bf16 else 2

    def body(*refs):
        a_ref, w_ref, x_ref, g_ref, r_ref = refs[:5]
        o_ref = refs[5 + n_x]
        rest = refs[6 + n_x:5 + n_x + n_o] + (refs[5 + 2 * n_x + n_o],)
        dg_ref, acc = rest[-2], rest[-1]
        scatter = lambda: _AllToAll(refs[5:5 + n_x], refs[5 + n_x + n_o:5 + 2 * n_x + n_o], refs[6 + 2 * n_x + n_o:], False)
        i, k = pl.program_id(0), pl.program_id(1)

        if n_x:
            @pl.when((i == 0) & (k == 0))
            def _():
                scatter().start()

        @pl.when(k == 0)
        def _():
            acc[...] = jnp.zeros_like(acc)

        acc[...] += _nt(a_ref[...], w_ref[...])

        @pl.when(k == n_k - 1)
        def _():
            @pl.when(i == 0)
            def _():
                dg_ref[...] = jnp.zeros_like(dg_ref)

            for rows in _row_chunks(tm):
                du = acc[rows, :]
                r, xh = _rms(x_ref[rows, :])
                out = r_ref[rows, :] + _rms_bwd(du, g_ref[...], r, xh)
                o_ref[rows, :] = out
                if emit_bf16:
                    rest[0][rows, :] = out.astype(BF16)
                dg_ref[...] += _sum8(du * xh)

        if n_x:
            @pl.when((i == n_i - 1) & (k == n_k - 1))
            def _():
                scatter().wait()

    row = pl.BlockSpec((tm, D), lambda i, k: (i, 0))
    any_spec = pl.BlockSpec(memory_space=pl.ANY)
    out_shape = [jax.ShapeDtypeStruct((S, D), F32)]
    out_specs = [row]
    if emit_bf16:
        out_shape.append(jax.ShapeDtypeStruct((S, D), BF16))
        out_specs.append(row)
    out_shape.append(jax.ShapeDtypeStruct((8, D), F32))
    out_specs.append(pl.BlockSpec((8, D), lambda i, k: (0, 0)))
    out = pl.pallas_call(
        body, name=name, grid=(n_i, n_k),
        in_specs=[pl.BlockSpec((tm, tk), lambda i, k: (i, k)), pl.BlockSpec((D, tk), lambda i, k: (0, k)),
                  row, pl.BlockSpec((1, D), lambda i, k: (0, 0)), row] + [any_spec] * n_x,
        out_specs=out_specs + [any_spec] * n_x,
        out_shape=out_shape + [jax.ShapeDtypeStruct(b.shape, b.dtype) for b in slabs],
        scratch_shapes=[pltpu.VMEM((tm, D), F32)] + (_exchange_sems(n_x) if n_x else []),
        compiler_params=_params("arbitrary", "arbitrary"),
    )(a, w, xin, g, dres, *slabs)
    return (*out[:n_o], out[n_o:]) if n_x else out


def _merge(og, ofox, omem, wg, wf, wm, pg):
    S = og.shape[0]
    tm = _tile(S, 256)

    def body(og_ref, of_ref, om_ref, wg_ref, wf_ref, wm_ref, pg_ref, y_ref, mg_ref):
        tot = None
        for i, (o_ref, w_ref) in enumerate(((og_ref, wg_ref), (of_ref, wf_ref), (om_ref, wm_ref))):
            y = _nn(o_ref[...].astype(BF16), w_ref[...])
            y_ref[i] = y.astype(BF16)
            t = jax.nn.sigmoid(pg_ref[:, D * i:D * (i + 1)]) * y
            tot = t if tot is None else tot + t
        mg_ref[...] = tot.astype(BF16)

    o_spec = pl.BlockSpec((tm, 512), lambda i: (i, 0))
    w_spec = pl.BlockSpec((512, D), lambda i: (0, 0))
    return pl.pallas_call(
        body, name="merge", grid=(S // tm,),
        in_specs=[o_spec, o_spec, o_spec, w_spec, w_spec, w_spec, pl.BlockSpec((tm, G_W), lambda i: (i, 0))],
        out_specs=[pl.BlockSpec((3, tm, D), lambda i: (0, i, 0)), pl.BlockSpec((tm, D), lambda i: (i, 0))],
        out_shape=[jax.ShapeDtypeStruct((3, S, D), BF16), jax.ShapeDtypeStruct((S, D), BF16)],
        compiler_params=_params("arbitrary"),
    )(og, ofox, omem, wg, wf, wm, pg)


def _out_proj(mg, w_out, x, g_ffn):
    S = x.shape[0]
    tm = _tile(S, 512)

    def body(mg_ref, w_ref, x_ref, g_ref, h_ref, u_ref):
        h = x_ref[...] + _nn(mg_ref[...], w_ref[...])
        h_ref[...] = h
        r, xh = _rms(h)
        u_ref[...] = (xh * g_ref[...]).astype(BF16)

    row = pl.BlockSpec((tm, D), lambda i: (i, 0))
    return pl.pallas_call(
        body, name="out_proj", grid=(S // tm,),
        in_specs=[row, pl.BlockSpec((D, D), lambda i: (0, 0)), row, pl.BlockSpec((1, D), lambda i: (0, 0))],
        out_specs=[row, row],
        out_shape=[jax.ShapeDtypeStruct((S, D), F32), jax.ShapeDtypeStruct((S, D), BF16)],
        compiler_params=_params("arbitrary"),
    )(mg, w_out, x, g_ffn)


def _ff1(u2, w1):
    S = u2.shape[0]
    tm, tn = _tile(S, 1024), 1024

    def body(u_ref, w_ref, a_ref, act_ref):
        a = _nn(u_ref[...], w_ref[...])
        a_ref[...] = a.astype(BF16)
        act_ref[...] = jnp.square(jnp.maximum(a, 0.0)).astype(BF16)

    blk = pl.BlockSpec((tm, tn), lambda i, j: (i, j))
    return pl.pallas_call(
        body, name="ff1", grid=(S // tm, D_FF // tn),
        in_specs=[pl.BlockSpec((tm, D), lambda i, j: (i, 0)), pl.BlockSpec((D, tn), lambda i, j: (0, j))],
        out_specs=[blk, blk],
        out_shape=[jax.ShapeDtypeStruct((S, D_FF), BF16), jax.ShapeDtypeStruct((S, D_FF), BF16)],
        compiler_params=_params("arbitrary", "arbitrary"),
    )(u2, w1)


def _ff2_loss(act, w2, h1, g_final, target):
    S = act.shape[0]
    tm, tk = _tile(S, 1024), 1024
    n_k = D_FF // tk

    def body(a_ref, w_ref, h_ref, g_ref, t_ref, d_ref, db_ref, ls_ref, dg_ref, acc):
        i, k = pl.program_id(0), pl.program_id(1)

        @pl.when(k == 0)
        def _():
            acc[...] = jnp.zeros_like(acc)

        acc[...] += _nn(a_ref[...], w_ref[...])

        @pl.when(k == n_k - 1)
        def _():
            @pl.when(i == 0)
            def _():
                ls_ref[...] = jnp.zeros_like(ls_ref)
                dg_ref[...] = jnp.zeros_like(dg_ref)

            gf = g_ref[...]
            for rows in _row_chunks(tm):
                r, xh = _rms(h_ref[rows, :] + acc[rows, :])
                err = xh * gf - t_ref[rows, :]
                dy = err * (1.0 / D)
                dh = _rms_bwd(dy, gf, r, xh)
                d_ref[rows, :] = dh
                db_ref[rows, :] = dh.astype(BF16)
                ls_ref[...] += _sum8(err * err)
                dg_ref[...] += _sum8(dy * xh)

    row = pl.BlockSpec((tm, D), lambda i, k: (i, 0))
    part = pl.BlockSpec((8, D), lambda i, k: (0, 0))
    return pl.pallas_call(
        body, name="ff2_loss", grid=(S // tm, n_k),
        in_specs=[pl.BlockSpec((tm, tk), lambda i, k: (i, k)), pl.BlockSpec((tk, D), lambda i, k: (k, 0)),
                  row, pl.BlockSpec((1, D), lambda i, k: (0, 0)), row],
        out_specs=[row, row, part, part],
        out_shape=[jax.ShapeDtypeStruct((S, D), F32), jax.ShapeDtypeStruct((S, D), BF16),
                   jax.ShapeDtypeStruct((8, D), F32), jax.ShapeDtypeStruct((8, D), F32)],
        scratch_shapes=[pltpu.VMEM((tm, D), F32)],
        compiler_params=_params("arbitrary", "arbitrary"),
    )(act, w2, h1, g_final, target)


def _dact(dh2b, w2, a):
    S = a.shape[0]
    tm, tn = _tile(S, 1024), 1024

    def body(d_ref, w_ref, a_ref, o_ref):
        da = _nt(d_ref[...], w_ref[...])
        o_ref[...] = (da * (2.0 * jnp.maximum(a_ref[...].astype(F32), 0.0))).astype(BF16)

    blk = pl.BlockSpec((tm, tn), lambda i, j: (i, j))
    return pl.pallas_call(
        body, name="dact", grid=(S // tm, D_FF // tn),
        in_specs=[pl.BlockSpec((tm, D), lambda i, j: (i, 0)), pl.BlockSpec((tn, D), lambda i, j: (j, 0)), blk],
        out_specs=blk, out_shape=jax.ShapeDtypeStruct((S, D_FF), BF16),
        compiler_params=_params("arbitrary", "arbitrary"),
    )(dh2b, w2, a)


def _dmerge(dh1b, w_out, pg, y3, wg, wf, wm):
    S = dh1b.shape[0]
    tm = _tile(S, 256)

    def body(d_ref, w_ref, pg_ref, y_ref, wg_ref, wf_ref, wm_ref, *outs):
        dy_refs, do_refs, dg_ref = outs[0:3], outs[3:6], outs[6]
        dm = _nt(d_ref[...], w_ref[...])
        for i, wo_ref in enumerate((wg_ref, wf_ref, wm_ref)):
            gt = jax.nn.sigmoid(pg_ref[:, D * i:D * (i + 1)])
            dy = (dm * gt).astype(BF16)
            dy_refs[i][...] = dy
            do_refs[i][...] = _nt(dy, wo_ref[...])
            dg_ref[:, D * i:D * (i + 1)] = (dm * y_ref[i].astype(F32) * (gt * (1.0 - gt))).astype(BF16)

    row = pl.BlockSpec((tm, D), lambda i: (i, 0))
    half = pl.BlockSpec((tm, 512), lambda i: (i, 0))
    w_spec = pl.BlockSpec((512, D), lambda i: (0, 0))
    return pl.pallas_call(
        body, name="dmerge", grid=(S // tm,),
        in_specs=[row, pl.BlockSpec((D, D), lambda i: (0, 0)), pl.BlockSpec((tm, G_W), lambda i: (i, 0)),
                  pl.BlockSpec((3, tm, D), lambda i: (0, i, 0)), w_spec, w_spec, w_spec],
        out_specs=[row, row, row, half, half, half, pl.BlockSpec((tm, G_W), lambda i: (i, P_GT // G_W))],
        out_shape=[jax.ShapeDtypeStruct((S, D), BF16)] * 3 + [jax.ShapeDtypeStruct((S, 512), F32)] * 3
        + [jax.ShapeDtypeStruct((S, P_W), BF16)],
        compiler_params=_params("arbitrary"),
    )(dh1b, w_out, pg, y3, wg, wf, wm)


def _gla_block_terms(gq_ref, gk_ref, ps_ref, wau_ref, ba_ref, tb):
    gaff = ps_ref[:, 512:640]
    z = _nn(gaff.astype(BF16), wau_ref[...]) + ba_ref[...]
    la = _log_sigmoid(z) * GLA_TAU_INV
    rr = lax.broadcasted_iota(jnp.int32, (tb, tb), 0)
    cc = lax.broadcasted_iota(jnp.int32, (tb, tb), 1)
    same = jnp.right_shift(rr, 6) == jnp.right_shift(cc, 6)
    tri = jnp.where(same & (cc <= rr), 1.0, 0.0).astype(BF16)
    ones = jnp.where(same, 1.0, 0.0).astype(BF16)
    b = _sum01(tri, la)
    bl = _sum01(ones, la)
    e_pos, e_neg, e_last, dec = jnp.exp(b), jnp.exp(-b), jnp.exp(bl - b), jnp.exp(bl)
    q = gq_ref[...].astype(F32) * GLA_SCALE
    k = gk_ref[...].astype(F32)
    return dict(gaff=gaff, z=z, same=same, rr=rr, cc=cc, ones=ones, e_pos=e_pos, e_neg=e_neg, e_last=e_last, dec=dec,
                qp=q * e_pos, qn=q * e_neg, kn=k * e_neg, kp=k * e_pos, kd=k * e_last)


def _head_masked(x, store):
    lane = lax.broadcasted_iota(jnp.int32, x.shape, 1)
    for h in range(GLA_H):
        store[h] = jnp.where(jnp.right_shift(lane, 6) == h, x, 0.0).astype(BF16)


def _gla_fwd(pa, ps, wau, ba, gh, shards):
    S = pa.shape[0]
    tb = _tile(S, 512)
    n_c = tb // CHUNK
    n_b = S // tb
    n_x = len(shards)

    def body(*refs):
        gq_ref, gk_ref, gv_ref, ps_ref, wau_ref, ba_ref, gh_ref = refs[:7]
        o_ref, og_ref, sp_ref = refs[7 + n_x:10 + n_x]
        qpm, qnm, kdm, kn_s, kp_s, dec_s, state = refs[10 + 2 * n_x:17 + 2 * n_x]
        gather = lambda: _AllToAll(refs[7:7 + n_x], refs[10 + n_x:10 + 2 * n_x], refs[17 + 2 * n_x:], True)

        @pl.when(pl.program_id(0) == 0)
        def _():
            state[...] = jnp.zeros_like(state)
            gather().start()

        t = _gla_block_terms(gq_ref, gk_ref, ps_ref, wau_ref, ba_ref, tb)
        _head_masked(t["qp"], qpm)
        _head_masked(t["qn"], qnm)
        _head_masked(t["kd"], kdm)
        kn_s[...] = t["kn"].astype(BF16)
        kp_s[...] = t["kp"].astype(BF16)
        dec_s[...] = t["dec"]
        lower = lax.broadcasted_iota(jnp.int32, (CHUNK, CHUNK), 0) >= lax.broadcasted_iota(jnp.int32, (CHUNK, CHUNK), 1)

        sp = state[...]
        for c in range(n_c):
            rows = slice(c * CHUNK, (c + 1) * CHUNK)
            sp_ref[c] = sp
            spb = sp.astype(BF16)
            knc, kpc = kn_s[rows, :], kp_s[rows, :]
            new = sp * dec_s[c * CHUNK:c * CHUNK + 1, :]
            for h in range(GLA_H):
                cols = slice(GLA_DV * h, GLA_DV * (h + 1))
                qpc, qnc = qpm[h, rows, :], qnm[h, rows, :]
                attn = jnp.where(lower, _nt(qpc, knc), _nt(qnc, kpc)).astype(BF16)
                vh = gv_ref[rows, cols]
                o_ref[rows, cols] = _nn(attn, vh) + _nt(qpc, spb)
                new = new + _tn(vh, kdm[h, rows, :])
            sp = new
        state[...] = sp
        for h in range(GLA_H):
            cols = slice(GLA_DV * h, GLA_DV * (h + 1))
            r, xh = _rms(o_ref[:, cols])
            gg = ps_ref[:, cols]
            og_ref[:, cols] = ((xh * gh_ref[:, cols]) * (gg * jax.nn.sigmoid(gg))).astype(BF16)

        @pl.when(pl.program_id(0) == n_b - 1)
        def _():
            gather().wait()

    any_spec = pl.BlockSpec(memory_space=pl.ANY)
    out = pl.pallas_call(
        body, name="gla_fwd", grid=(n_b,),
        in_specs=[pl.BlockSpec((tb, GLA_K), lambda i: (i, 0)), pl.BlockSpec((tb, GLA_K), lambda i: (i, 1)),
                  pl.BlockSpec((tb, GLA_V), lambda i: (i, 1)), pl.BlockSpec((tb, S_W), lambda i: (i, 0)),
                  pl.BlockSpec((128, GLA_K), lambda i: (0, 0)), pl.BlockSpec((1, GLA_K), lambda i: (0, 0)),
                  pl.BlockSpec((1, GLA_V), lambda i: (0, 0))] + [any_spec] * n_x,
        out_specs=[pl.BlockSpec((tb, GLA_V), lambda i: (i, 0)), pl.BlockSpec((tb, GLA_V), lambda i: (i, 0)),
                   pl.BlockSpec((n_c, GLA_DV, GLA_K), lambda i: (i, 0, 0))] + [any_spec] * n_x,
        out_shape=[jax.ShapeDtypeStruct((S, GLA_V), F32), jax.ShapeDtypeStruct((S, GLA_V), BF16),
                   jax.ShapeDtypeStruct((S // CHUNK, GLA_DV, GLA_K), F32)] + _gathered_shapes(shards),
        scratch_shapes=[pltpu.VMEM((GLA_H, tb, GLA_K), BF16), pltpu.VMEM((GLA_H, tb, GLA_K), BF16),
                        pltpu.VMEM((GLA_H, tb, GLA_K), BF16), pltpu.VMEM((tb, GLA_K), BF16), pltpu.VMEM((tb, GLA_K), BF16),
                        pltpu.VMEM((tb, GLA_K), F32), pltpu.VMEM((GLA_DV, GLA_K), F32)] + _exchange_sems(n_x),
        compiler_params=_params("arbitrary"),
    )(pa, pa, pa, ps, wau, ba, gh, *shards)
    return out[0], out[1], out[2], out[3:]


def _gla_bwd(pa, ps, wau, ba, gh, o_gla, d_og, sprev, dgaff_fox, d_proj, slabs):
    S = pa.shape[0]
    tb = _tile(S, 512)
    n_c = tb // CHUNK
    n_b = S // tb
    n_x = len(slabs)
    c_gk, c_gv, c_gg, c_ga, c_end = GLA_K, 2 * GLA_K, 2 * GLA_K + GLA_V, 2 * GLA_K + 2 * GLA_V, 2 * GLA_K + 2 * GLA_V + 128

    def body(*refs):
        gq_ref, gk_ref, gv_ref, ps_ref, wau_ref, ba_ref, gh_ref, o_ref, dog_ref, sp_ref, dfx_ref = refs[:11]
        dp_ref, dwau_ref, dba_ref, dgh_ref = refs[12 + n_x:16 + n_x]
        (qpm, qnm, kdm, kn_s, kp_s, dec_s, do_s, dqp_s, dqn_s, dkn_s, dkp_s, dkd_s, ddec_s,
         dstate) = refs[16 + 2 * n_x:30 + 2 * n_x]
        scatter = lambda: _AllToAll(refs[12:12 + n_x], refs[16 + n_x:16 + 2 * n_x], refs[30 + 2 * n_x:], False)
        first = pl.program_id(0) == 0
        dp_ref[:, c_end:] = jnp.zeros((tb, P_GLA_W - c_end), BF16)

        @pl.when(first)
        def _():
            dstate[...] = jnp.zeros_like(dstate)
            scatter().start()

        t = _gla_block_terms(gq_ref, gk_ref, ps_ref, wau_ref, ba_ref, tb)
        _head_masked(t["qp"], qpm)
        _head_masked(t["qn"], qnm)
        _head_masked(t["kd"], kdm)
        kn_s[...] = t["kn"].astype(BF16)
        kp_s[...] = t["kp"].astype(BF16)
        dec_s[...] = t["dec"]

        dgh_parts = []
        for h in range(GLA_H):
            cols = slice(GLA_DV * h, GLA_DV * (h + 1))
            r, xh = _rms(o_ref[:, cols])
            g = gh_ref[:, cols]
            gg = ps_ref[:, cols]
            sg = jax.nn.sigmoid(gg)
            d_out = dog_ref[:, cols]
            dp_ref[:, c_gg + GLA_DV * h:c_gg + GLA_DV * (h + 1)] = (d_out * (xh * g) * (sg * (1.0 + gg * (1.0 - sg)))).astype(BF16)
            d_on = d_out * (gg * sg)
            dgh_parts.append(_sum8(d_on * xh))
            do_s[:, cols] = _rms_bwd(d_on, g, r, xh).astype(BF16)
        dgh_part = jnp.concatenate(dgh_parts, axis=1)

        lower = lax.broadcasted_iota(jnp.int32, (CHUNK, CHUNK), 0) >= lax.broadcasted_iota(jnp.int32, (CHUNK, CHUNK), 1)
        lane = lax.broadcasted_iota(jnp.int32, (CHUNK, GLA_K), 1)

        ds_next = dstate[...]
        for c in reversed(range(n_c)):
            rows = slice(c * CHUNK, (c + 1) * CHUNK)
            dsb = ds_next.astype(BF16)
            sp = sp_ref[c]
            spb = sp.astype(BF16)
            knc, kpc = kn_s[rows, :], kp_s[rows, :]
            dec_row = dec_s[c * CHUNK:c * CHUNK + 1, :]
            ddec_s[rows, :] = jnp.broadcast_to(jnp.sum(ds_next * sp, axis=0, keepdims=True), (CHUNK, GLA_K))
            new = ds_next * dec_row
            dqp = jnp.zeros((CHUNK, GLA_K), F32)
            dqn, dkn, dkp, dkd = dqp, dqp, dqp, dqp
            for h in range(GLA_H):
                cols = slice(GLA_DV * h, GLA_DV * (h + 1))
                mine = jnp.right_shift(lane, 6) == h
                qpc, qnc, kdc = qpm[h, rows, :], qnm[h, rows, :], kdm[h, rows, :]
                vh = gv_ref[rows, cols]
                doh = do_s[rows, cols]
                attn = jnp.where(lower, _nt(qpc, knc), _nt(qnc, kpc)).astype(BF16)
                da = _nt(doh, vh)
                dac = jnp.where(lower, da, 0.0).astype(BF16)
                daa = jnp.where(lower, 0.0, da).astype(BF16)
                dqp = dqp + jnp.where(mine, _nn(dac, knc) + _nn(doh, spb), 0.0)
                dqn = dqn + jnp.where(mine, _nn(daa, kpc), 0.0)
                dkn = dkn + _tn(dac, qpc)
                dkp = dkp + _tn(daa, qnc)
                dkd = dkd + jnp.where(mine, _nn(vh, dsb), 0.0)
                dp_ref[rows, c_gv + GLA_DV * h:c_gv + GLA_DV * (h + 1)] = (_tn(attn, doh) + _nt(kdc, dsb)).astype(BF16)
                new = new + _tn(doh, qpc)
            dqp_s[rows, :] = dqp
            dqn_s[rows, :] = dqn
            dkn_s[rows, :] = dkn
            dkp_s[rows, :] = dkp
            dkd_s[rows, :] = dkd
            ds_next = new
        dstate[...] = ds_next

        dqp, dqn, dkn, dkp, dkd = dqp_s[...], dqn_s[...], dkn_s[...], dkp_s[...], dkd_s[...]
        dp_ref[:, 0:c_gk] = ((dqp * t["e_pos"] + dqn * t["e_neg"]) * GLA_SCALE).astype(BF16)
        dp_ref[:, c_gk:c_gv] = (dkn * t["e_neg"] + dkp * t["e_pos"] + dkd * t["e_last"]).astype(BF16)
        kd_term = dkd * t["kd"]
        db = dqp * t["qp"] - dqn * t["qn"] - dkn * t["kn"] + dkp * t["kp"] - kd_term
        upper = jnp.where(t["same"] & (t["cc"] >= t["rr"]), 1.0, 0.0).astype(BF16)
        dla = (_sum01(upper, db) + _sum01(t["ones"], kd_term)
               + ddec_s[...] * t["dec"])
        dz = dla * GLA_TAU_INV * jax.nn.sigmoid(-t["z"])
        dzb = dz.astype(BF16)
        dp_ref[:, c_ga:c_end] = (_nt(dzb, wau_ref[...]) + dfx_ref[...]).astype(BF16)
        dwau_part = _tn(t["gaff"].astype(BF16), dzb)
        dba_part = _sum8(dz)

        @pl.when(first)
        def _():
            dwau_ref[...] = dwau_part
            dba_ref[...] = dba_part
            dgh_ref[...] = dgh_part

        @pl.when(jnp.logical_not(first))
        def _():
            dwau_ref[...] += dwau_part
            dba_ref[...] += dba_part
            dgh_ref[...] += dgh_part

        @pl.when(pl.program_id(0) == n_b - 1)
        def _():
            scatter().wait()

    rev = lambda i: (n_b - 1 - i, 0)
    f32k = pltpu.VMEM((tb, GLA_K), F32)
    bf4 = pltpu.VMEM((GLA_H, tb, GLA_K), BF16)
    any_spec = pl.BlockSpec(memory_space=pl.ANY)
    out = pl.pallas_call(
        body, name="gla_bwd", grid=(n_b,),
        in_specs=[pl.BlockSpec((tb, GLA_K), rev), pl.BlockSpec((tb, GLA_K), lambda i: (n_b - 1 - i, 1)),
                  pl.BlockSpec((tb, GLA_V), lambda i: (n_b - 1 - i, 1)), pl.BlockSpec((tb, S_W), rev),
                  pl.BlockSpec((128, GLA_K), lambda i: (0, 0)), pl.BlockSpec((1, GLA_K), lambda i: (0, 0)),
                  pl.BlockSpec((1, GLA_V), lambda i: (0, 0)), pl.BlockSpec((tb, GLA_V), rev), pl.BlockSpec((tb, GLA_V), rev),
                  pl.BlockSpec((n_c, GLA_DV, GLA_K), lambda i: (n_b - 1 - i, 0, 0)), pl.BlockSpec((tb, 128), rev),
                  any_spec] + [any_spec] * n_x,
        out_specs=[pl.BlockSpec((tb, P_GLA_W), lambda i: (n_b - 1 - i, P_GLA // P_GLA_W)),
                   pl.BlockSpec((128, GLA_K), lambda i: (0, 0)), pl.BlockSpec((8, GLA_K), lambda i: (0, 0)),
                   pl.BlockSpec((8, GLA_V), lambda i: (0, 0))] + [any_spec] * n_x,
        out_shape=[jax.ShapeDtypeStruct((S, P_W), BF16), jax.ShapeDtypeStruct((128, GLA_K), F32),
                   jax.ShapeDtypeStruct((8, GLA_K), F32), jax.ShapeDtypeStruct((8, GLA_V), F32)]
        + [jax.ShapeDtypeStruct(b.shape, b.dtype) for b in slabs],
        input_output_aliases={11: 0},
        scratch_shapes=[bf4, bf4, bf4, pltpu.VMEM((tb, GLA_K), BF16), pltpu.VMEM((tb, GLA_K), BF16), f32k,
                        pltpu.VMEM((tb, GLA_V), BF16), f32k, f32k, f32k, f32k, f32k, f32k, pltpu.VMEM((GLA_DV, GLA_K), F32)]
        + _exchange_sems(n_x),
        compiler_params=_params("arbitrary"),
    )(pa, pa, pa, ps, wau, ba, gh, o_gla, d_og, sprev, dgaff_fox, d_proj, *slabs)
    return out[0], out[1], out[2], out[3], out[4:]


def _split3(x):
    x1 = x.astype(BF16).astype(F32)
    x2 = (x - x1).astype(BF16).astype(F32)
    x3 = (x - x1 - x2).astype(BF16).astype(F32)
    return x1, x2, x3


def _fox_prep(pa, ps, bfg):
    S = pa.shape[0]
    tm = _tile(S, 512)

    def body(ps_ref, b_ref, fq_ref, fk_ref, fv_ref, q_ref, k_ref, qt_ref, kt_ref, vt_ref, st_ref, carry):
        @pl.when(pl.program_id(0) == 0)
        def _():
            carry[...] = jnp.zeros_like(carry)

        vt_ref[...] = fv_ref[...].astype(F32).T.astype(BF16)
        lf = _log_sigmoid(ps_ref[...] + b_ref[...])
        rr = lax.broadcasted_iota(jnp.int32, (tm, tm), 0)
        cc = lax.broadcasted_iota(jnp.int32, (tm, tm), 1)
        tri = jnp.where(cc <= rr, 1.0, 0.0).astype(F32)
        f = jnp.dot(tri, lf, preferred_element_type=F32, precision=HIGHEST) + carry[0:1, :]
        carry[...] = jnp.broadcast_to(f[tm - 1:tm, :], carry.shape)
        f1, f2, f3 = _split3(f)
        lane = lax.broadcasted_iota(jnp.int32, (tm, 128), 1)
        st_row = lax.broadcasted_iota(jnp.int32, (8, 128), 0)
        st_lane = lax.broadcasted_iota(jnp.int32, (8, 128), 1)
        stats = jnp.zeros((8, 128), F32)
        for h in range(FOX_H):
            cols = slice(128 * h, 128 * (h + 1))
            c = FF_LANE + h
            a1, a2, a3 = f1[:, c:c + 1], f2[:, c:c + 1], f3[:, c:c + 1]
            q = fq_ref[:, cols].astype(F32) * FOX_SCALE
            k = fk_ref[:, cols].astype(F32)
            fh = f[:, c:c + 1]
            vals = (jnp.max(jnp.sum(q * q, axis=-1, keepdims=True)), jnp.max(jnp.sum(k * k, axis=-1, keepdims=True)),
                    jnp.max(fh), jnp.min(fh), jnp.min(jnp.sum(q * k, axis=-1, keepdims=True)))
            for n, val in enumerate(vals):
                stats = jnp.where((st_row == h) & (st_lane == n), val, stats)
            for n, a in enumerate((a1, a2, a3)):
                q = jnp.where(lane == AUG + n, a, q)
                k = jnp.where(lane == AUG + 3 + n, -a, k)
            q = jnp.where((lane >= AUG + 3) & (lane < AUG + 6), 1.0, q)
            k = jnp.where((lane >= AUG) & (lane < AUG + 3), 1.0, k)
            q_ref[:, cols] = q.astype(BF16)
            k_ref[:, cols] = k.astype(BF16)
            qt_ref[cols, :] = q.T.astype(BF16)
            kt_ref[cols, :] = k.T.astype(BF16)
        st_ref[0] = stats

    wide = lambda j: pl.BlockSpec((tm, 1024), lambda i: (i, j))
    tall = lambda n: pl.BlockSpec((n, tm), lambda i: (0, i))
    return pl.pallas_call(
        body, name="fox_prep", grid=(S // tm,),
        in_specs=[pl.BlockSpec((tm, 128), lambda i: (i, 4)), pl.BlockSpec((1, 128), lambda i: (0, 0)), wide(1), wide(2),
                  pl.BlockSpec((tm, FOX_W), lambda i: (i, A_FV // FOX_W))],
        out_specs=[wide(0), wide(0), tall(1024), tall(1024), tall(FOX_W), pl.BlockSpec((1, 8, 128), lambda i: (i, 0, 0))],
        out_shape=[jax.ShapeDtypeStruct((S, 1024), BF16), jax.ShapeDtypeStruct((S, 1024), BF16),
                   jax.ShapeDtypeStruct((1024, S), BF16), jax.ShapeDtypeStruct((1024, S), BF16),
                   jax.ShapeDtypeStruct((FOX_W, S), BF16), jax.ShapeDtypeStruct((S // tm, 8, 128), F32)],
        scratch_shapes=[pltpu.VMEM((8, 128), F32)],
        compiler_params=_params("arbitrary"),
    )(ps, bfg, pa, pa, pa)


FOX_PRUNE_AT = -90.0


def _fox_live_ranges(stats, n_sub, ratio):
    n_b = stats.shape[0]
    q2, k2, f_max, f_min, own = (stats[:, :, n].T for n in range(5))
    slack = 0.01 * jnp.sqrt(q2 * k2) + 1e-5 * jnp.abs(f_max) + 1.0
    bound = (1.01 * jnp.sqrt(q2[:, :, None] * k2[:, None, :]) + (f_max + slack - own)[:, :, None]
             - (f_min - 1e-5 * jnp.abs(f_min))[:, None, :])
    blocks = jnp.arange(n_b)
    dead = (bound <= FOX_PRUNE_AT) & (blocks[None, :] < blocks[:, None])[None]
    dead_fwd = dead.reshape(FOX_H, n_b // n_sub, n_sub, n_b).all(axis=2)
    first = jnp.sum(jnp.cumprod(dead_fwd.astype(jnp.int32), axis=2), axis=2)
    last_live = n_b - 1 - jnp.sum(jnp.cumprod(dead[:, ::-1, :].astype(jnp.int32), axis=1), axis=1)
    first_wide = blocks // ratio + 1
    narrow_end = jnp.minimum(jnp.minimum(first_wide * ratio, n_b)[None], last_live + 1)
    wide_end = jnp.where(last_live >= (first_wide * ratio)[None], last_live // ratio + 1, first_wide[None])
    return first.astype(jnp.int32), narrow_end.astype(jnp.int32), wide_end.astype(jnp.int32)


def _fox_fwd(qa, ka, vt, first):
    S = qa.shape[0]
    tq = _tile(S, FOX_TQ)
    tk = _tile(tq, FOX_TK)
    n_sub = tq // tk

    def body(first_ref, q_ref, k_ref, vt_ref, o_ref, lse_ref):
        pair, i = pl.program_id(0), pl.program_id(1)
        both = lambda f: tuple(f(hh) for hh in range(2))

        def blk(j, carry, diag, heads=(0, 1)):
            ks = pl.ds(pl.multiple_of(j * tk, tk), tk)
            q0 = 0 if diag is None else diag * tk

            def head(hh):
                if hh not in heads:
                    return carry[hh]
                m, l, acc = carry[hh]
                mo, lo, ao = m[:, q0:], l[:, q0:], acc[:, q0:]
                s = _nt(k_ref[ks, 128 * hh:128 * (hh + 1)], q_ref[q0:, 128 * hh:128 * (hh + 1)])
                if diag is not None:
                    live = lax.broadcasted_iota(jnp.int32, s.shape, 1) >= lax.broadcasted_iota(jnp.int32, s.shape, 0)
                    s = jnp.where(live, s, NEG)
                mn = jnp.maximum(mo, jnp.max(s, axis=0, keepdims=True))
                p = jnp.exp(s - mn)
                al = jnp.exp(mo - mn)
                ln = al * lo + jnp.sum(p, axis=0, keepdims=True)
                an = al * ao + _nn(vt_ref[FOX_DH * hh:FOX_DH * (hh + 1), ks], p.astype(BF16))
                if q0:
                    mn, ln, an = (jnp.concatenate([old[:, :q0], new], axis=1) for old, new in ((m, mn), (l, ln), (acc, an)))
                return mn, ln, an

            return both(head)

        one = (jnp.full((1, tq), NEG, F32), jnp.zeros((1, tq), F32), jnp.zeros((FOX_DH, tq), F32))
        past = i * n_sub
        f0, f1 = first_ref[2 * pair, i], first_ref[2 * pair + 1, i]
        join = jnp.maximum(f0, f1)
        solo = lambda hh: lambda c: lax.fori_loop(jnp.minimum(f0, f1), join, lambda j, cc: blk(j, cc, None, (hh,)), c)
        carry = lax.cond(f0 < f1, solo(0), solo(1), (one, one))
        n_both = past - join
        carry = lax.fori_loop(0, n_both // 2, lambda jj, c: blk(join + 2 * jj + 1, blk(join + 2 * jj, c, None), None), carry)
        carry = lax.cond(n_both % 2 == 1, lambda c: blk(past - 1, c, None), lambda c: c, carry)
        for d in range(n_sub):
            carry = blk(past + d, carry, d)
        (m0, l0, a0), (m1, l1, a1) = carry
        o_ref[...] = jnp.concatenate([a0 / l0, a1 / l1], axis=0).T
        lse_ref[0, 0:1, :] = m0 + jnp.log(l0)
        lse_ref[0, 1:2, :] = m1 + jnp.log(l1)
        lse_ref[0, 2:8, :] = jnp.zeros((6, tq), F32)

    once = pl.Buffered(1)
    return pl.pallas_call(
        body, name="fox_fwd", grid=(FOX_H // 2, S // tq),
        in_specs=[pl.BlockSpec(memory_space=pltpu.SMEM), pl.BlockSpec((tq, 256), lambda p, i: (i, p)),
                  pl.BlockSpec((S, 256), lambda p, i: (0, p), pipeline_mode=once),
                  pl.BlockSpec((128, S), lambda p, i: (p, 0), pipeline_mode=once)],
        out_specs=[pl.BlockSpec((tq, 128), lambda p, i: (i, p)), pl.BlockSpec((1, 8, tq), lambda p, i: (p, 0, i))],
        out_shape=[jax.ShapeDtypeStruct((S, FOX_W), F32), jax.ShapeDtypeStruct((FOX_H // 2, 8, S), F32)],
        compiler_params=_params("arbitrary", "arbitrary"),
    )(first, qa, ka, vt)


def _fox_delta(d_o, o):
    S = o.shape[0]
    tm = _tile(S, 512)

    def body(d_ref, o_ref, db_ref, dbt_ref, dl_ref):
        d = d_ref[...]
        db_ref[...] = d.astype(BF16)
        dbt_ref[...] = d.T.astype(BF16)
        prod = d * o_ref[...]
        rr = lax.broadcasted_iota(jnp.int32, (8, 128), 0)
        cc = lax.broadcasted_iota(jnp.int32, (8, 128), 1)
        ind = jnp.where(jnp.right_shift(cc, 6) == rr, 1.0, 0.0).astype(F32)
        for p in range(FOX_H // 2):
            dl_ref[p] = lax.dot_general(ind, prod[:, 128 * p:128 * (p + 1)], (((1,), (1,)), ((), ())),
                                        preferred_element_type=F32, precision=HIGHEST)

    row = pl.BlockSpec((tm, FOX_W), lambda i: (i, 0))
    return pl.pallas_call(
        body, name="fox_delta", grid=(S // tm,),
        in_specs=[row, row],
        out_specs=[row, pl.BlockSpec((FOX_W, tm), lambda i: (0, i)), pl.BlockSpec((FOX_H // 2, 8, tm), lambda i: (0, 0, i))],
        out_shape=[jax.ShapeDtypeStruct((S, FOX_W), BF16), jax.ShapeDtypeStruct((FOX_W, S), BF16),
                   jax.ShapeDtypeStruct((FOX_H // 2, 8, S), F32)],
        compiler_params=_params("arbitrary"),
    )(d_o, o)


def _fox_bwd(qa, qat, ka, kat, pa, dob, dobt, lse, delta, narrow_end, wide_end):
    S = qa.shape[0]
    tk = _tile(S, 512)
    wide = _tile(S, FOX_TQ)
    ratio = wide // tk
    n_wide = S // wide

    def body(ne_ref, we_ref, q_ref, qt_ref, k_ref, kt_ref, v_ref, do_ref, dot_ref, lse_ref, dl_ref, dq_ref, dk_ref, dv_ref):
        h, jb = pl.program_id(0), pl.program_id(1)
        hh = h % 2

        @pl.when(jb == 0)
        def _():
            dq_ref[...] = jnp.zeros_like(dq_ref)

        lane = lax.broadcasted_iota(jnp.int32, (tk, 128), 1)
        vm = jnp.where(jnp.right_shift(lane, 6) == hh, v_ref[...], jnp.zeros((), BF16))
        kb, ktb = k_ref[...], kt_ref[0:FOX_LIVE, :]
        mine = pl.ds(pl.multiple_of(hh * FOX_DH, FOX_DH), FOX_DH)

        def blk(ib, tq, carry, masked):
            dk, dv = carry
            qs = pl.ds(pl.multiple_of(ib * tq, tq), tq)
            p = jnp.exp(_nt(kb, q_ref[qs, :]) - lse_ref[0, pl.ds(hh, 1), qs])
            if masked:
                live = lax.broadcasted_iota(jnp.int32, p.shape, 1) >= lax.broadcasted_iota(jnp.int32, p.shape, 0)
                p = jnp.where(live, p, 0.0)
            ds = (p * (_nt(vm, do_ref[qs, :]) - dl_ref[0, pl.ds(hh, 1), qs])).astype(BF16)
            dq_ref[0:FOX_LIVE, qs] += _nn(ktb, ds)
            return dk + _nt(qt_ref[0:FOX_LIVE, qs], ds), dv + _nt(dot_ref[mine, qs], p.astype(BF16))

        carry = blk(jb, tk, (jnp.zeros((FOX_LIVE, tk), F32), jnp.zeros((FOX_DH, tk), F32)), True)
        first_wide = jb // ratio + 1
        carry = lax.fori_loop(jb + 1, ne_ref[h, jb], lambda ib, c: blk(ib, tk, c, False), carry)
        last_wide = we_ref[h, jb]
        rest = jnp.maximum(last_wide - first_wide, 0)
        carry = lax.fori_loop(0, rest // 2, lambda t, c: blk(first_wide + 2 * t + 1, wide, blk(first_wide + 2 * t, wide, c, False),
                                                             False), carry)
        dk, dv = lax.cond(rest % 2 == 1, lambda c: blk(last_wide - 1, wide, c, False), lambda c: c, carry)
        dk_ref[0:FOX_LIVE, :] = dk
        dk_ref[FOX_LIVE:, :] = jnp.zeros((128 - FOX_LIVE, tk), F32)
        dv_ref[...] = dv

    once = pl.Buffered(1)
    rows = pl.BlockSpec((1, 8, S), lambda h, j: (h // 2, 0, 0))
    return pl.pallas_call(
        body, name="fox_bwd", grid=(FOX_H, S // tk),
        in_specs=[pl.BlockSpec(memory_space=pltpu.SMEM), pl.BlockSpec(memory_space=pltpu.SMEM),
                  pl.BlockSpec((S, 128), lambda h, j: (0, h), pipeline_mode=once),
                  pl.BlockSpec((128, S), lambda h, j: (h, 0), pipeline_mode=once),
                  pl.BlockSpec((tk, 128), lambda h, j: (j, h)), pl.BlockSpec((128, tk), lambda h, j: (h, j)),
                  pl.BlockSpec((tk, 128), lambda h, j: (j, A_FV // 128 + h // 2)),
                  pl.BlockSpec((S, 128), lambda h, j: (0, h // 2), pipeline_mode=once),
                  pl.BlockSpec((128, S), lambda h, j: (h // 2, 0), pipeline_mode=once), rows, rows],
        out_specs=[pl.BlockSpec((128, S), lambda h, j: (h, 0), pipeline_mode=once),
                   pl.BlockSpec((128, tk), lambda h, j: (h, j)), pl.BlockSpec((FOX_DH, tk), lambda h, j: (h, j))],
        out_shape=[jax.ShapeDtypeStruct((1024, S), F32), jax.ShapeDtypeStruct((1024, S), F32),
                   jax.ShapeDtypeStruct((FOX_W, S), F32)],
        compiler_params=_params("arbitrary", "arbitrary"),
    )(narrow_end, wide_end, qa, qat, ka, kat, pa, dob, dobt, lse, delta)


def _fox_post(dq, dk, dv, ps, bfg, d_proj):
    S = dq.shape[1]
    tm = _tile(S, 512)
    n_b = S // tm

    def body(dq_ref, dk_ref, dv_ref, ps_ref, b_ref, _, dp_ref, dff_ref, dbf_ref, carry):
        first = pl.program_id(0) == 0

        @pl.when(first)
        def _():
            carry[...] = jnp.zeros_like(carry)

        low = lax.broadcasted_iota(jnp.int32, (tm, 128), 1) < FOX_DH
        for h in range(FOX_H):
            blk = slice(128 * h, 128 * (h + 1))
            dp_ref[:, blk] = jnp.where(low, dq_ref[blk, :].T * FOX_SCALE, 0.0).astype(BF16)
            dp_ref[:, 1024 + 128 * h:1024 + 128 * (h + 1)] = jnp.where(low, dk_ref[blk, :].T, 0.0).astype(BF16)
        dp_ref[:, 2048:P_FOX_W] = dv_ref[...].T.astype(BF16)
        rr = lax.broadcasted_iota(jnp.int32, (FOX_H, 1024), 0)
        cc = lax.broadcasted_iota(jnp.int32, (FOX_H, 1024), 1)
        sel_k = jnp.where(cc == 128 * rr + AUG + 3, 1.0, 0.0).astype(F32)
        sel_q = jnp.where(cc == 128 * rr + AUG, 1.0, 0.0).astype(F32)
        g = (jnp.dot(sel_k, dk_ref[...], preferred_element_type=F32, precision=HIGHEST)
             - jnp.dot(sel_q, dq_ref[...], preferred_element_type=F32, precision=HIGHEST))
        t_from = lax.broadcasted_iota(jnp.int32, (tm, tm), 0)
        t_to = lax.broadcasted_iota(jnp.int32, (tm, tm), 1)
        later = jnp.where(t_from >= t_to, 1.0, 0.0).astype(F32)
        dlf = jnp.dot(-g, later, preferred_element_type=F32, precision=HIGHEST) + carry[:, 0:1]
        carry[...] = jnp.broadcast_to(dlf[:, 0:1], carry.shape)
        cols = jnp.concatenate([jnp.zeros((FF_LANE, tm), F32), dlf, jnp.zeros((128 - FF_LANE - FOX_H, tm), F32)], axis=0).T
        dff = cols * jax.nn.sigmoid(-(ps_ref[...] + b_ref[...]))
        dff_ref[...] = dff
        part = _sum8(dff)

        @pl.when(first)
        def _():
            dbf_ref[...] = part

        @pl.when(jnp.logical_not(first))
        def _():
            dbf_ref[...] += part

    rev = lambda i: (n_b - 1 - i, 0)
    tall = lambda n: pl.BlockSpec((n, tm), lambda i: (0, n_b - 1 - i))
    return pl.pallas_call(
        body, name="fox_post", grid=(n_b,),
        in_specs=[tall(1024), tall(1024), tall(FOX_W), pl.BlockSpec((tm, 128), lambda i: (n_b - 1 - i, 4)),
                  pl.BlockSpec((1, 128), lambda i: (0, 0)), pl.BlockSpec(memory_space=pl.ANY)],
        out_specs=[pl.BlockSpec((tm, P_FOX_W), lambda i: (n_b - 1 - i, P_FOX // P_FOX_W)), pl.BlockSpec((tm, 128), rev),
                   pl.BlockSpec((8, 128), lambda i: (0, 0))],
        out_shape=[jax.ShapeDtypeStruct((S, P_W), BF16), jax.ShapeDtypeStruct((S, 128), F32),
                   jax.ShapeDtypeStruct((8, 128), F32)],
        input_output_aliases={5: 0},
        scratch_shapes=[pltpu.VMEM((8, 128), F32)],
        compiler_params=_params("arbitrary"),
    )(dq, dk, dv, ps, bfg, d_proj)


def _mem_prep(mem, g_mem, wkv):
    def body(m_ref, g_ref, w_ref, mn_ref, kv_ref):
        r, xh = _rms(m_ref[...])
        mn = (xh * g_ref[...]).astype(BF16)
        mn_ref[...] = mn
        kv_ref[...] = _nn(mn, w_ref[...]).astype(BF16)

    return pl.pallas_call(
        body, name="mem_prep",
        out_shape=[jax.ShapeDtypeStruct((N_MEM, D), BF16), jax.ShapeDtypeStruct((N_MEM, 2 * MEM_W), BF16)],
        compiler_params=pltpu.CompilerParams(vmem_limit_bytes=V7X_VMEM_LIMIT),
    )(mem, g_mem, wkv)


def _mem_softmax(qh, kh):
    s = _nt(qh, kh) * MEM_SCALE
    e = jnp.exp(s - jnp.max(s, axis=-1, keepdims=True))
    return e / jnp.sum(e, axis=-1, keepdims=True)


def _mem_fwd(pa, mkv):
    S = pa.shape[0]
    tm = _tile(S, 512)

    def body(q_ref, kv_ref, o_ref):
        for h in range(MEM_H):
            cols = slice(MEM_DH * h, MEM_DH * (h + 1))
            p = _mem_softmax(q_ref[:, cols], kv_ref[:, cols])
            o_ref[:, cols] = _nn(p.astype(BF16), kv_ref[:, MEM_W + MEM_DH * h:MEM_W + MEM_DH * (h + 1)])

    return pl.pallas_call(
        body, name="mem_fwd", grid=(S // tm,),
        in_specs=[pl.BlockSpec((tm, MEM_W), lambda i: (i, A_MQ // MEM_W)), pl.BlockSpec((N_MEM, 2 * MEM_W), lambda i: (0, 0))],
        out_specs=pl.BlockSpec((tm, MEM_W), lambda i: (i, 0)),
        out_shape=jax.ShapeDtypeStruct((S, MEM_W), F32),
        compiler_params=_params("arbitrary"),
    )(pa, mkv)


def _mem_bwd(pa, mkv, d_o, d_proj):
    S = pa.shape[0]
    tm = _tile(S, 512)

    def body(q_ref, kv_ref, do_ref, _, dq_ref, dkv_ref):
        first = pl.program_id(0) == 0
        parts = []
        for h in range(MEM_H):
            cols = slice(MEM_DH * h, MEM_DH * (h + 1))
            vcols = slice(MEM_W + MEM_DH * h, MEM_W + MEM_DH * (h + 1))
            qh, kh = q_ref[:, cols], kv_ref[:, cols]
            p = _mem_softmax(qh, kh)
            dob = do_ref[:, cols].astype(BF16)
            dp = _nt(dob, kv_ref[:, vcols])
            ds = (p * (dp - jnp.sum(p * dp, axis=-1, keepdims=True)) * MEM_SCALE).astype(BF16)
            dq_ref[:, cols] = _nn(ds, kh).astype(BF16)
            parts.append((cols, _tn(ds, qh)))
            parts.append((vcols, _tn(p.astype(BF16), dob)))

        @pl.when(first)
        def _():
            for sl, v in parts:
                dkv_ref[:, sl] = v

        @pl.when(jnp.logical_not(first))
        def _():
            for sl, v in parts:
                dkv_ref[:, sl] += v

    return pl.pallas_call(
        body, name="mem_bwd", grid=(S // tm,),
        in_specs=[pl.BlockSpec((tm, MEM_W), lambda i: (i, A_MQ // MEM_W)), pl.BlockSpec((N_MEM, 2 * MEM_W), lambda i: (0, 0)),
                  pl.BlockSpec((tm, MEM_W), lambda i: (i, 0)), pl.BlockSpec(memory_space=pl.ANY)],
        out_specs=[pl.BlockSpec((tm, MEM_W), lambda i: (i, P_MQ // MEM_W)), pl.BlockSpec((N_MEM, 2 * MEM_W), lambda i: (0, 0))],
        out_shape=[jax.ShapeDtypeStruct((S, P_W), BF16), jax.ShapeDtypeStruct((N_MEM, 2 * MEM_W), F32)],
        input_output_aliases={3: 0},
        compiler_params=_params("arbitrary"),
    )(pa, mkv, d_o, d_proj)


def _mem_prep_bwd(mem, g_mem, mn, wkv, dkv):
    def body(m_ref, g_ref, mn_ref, w_ref, d_ref, dw_ref, dg_ref):
        db = d_ref[...].astype(BF16)
        dw_ref[...] = _tn(mn_ref[...], db).astype(BF16)
        r, xh = _rms(m_ref[...])
        dg_ref[...] = _sum8(_nt(db, w_ref[...]) * xh)

    dw, dg = pl.pallas_call(
        body, name="mem_prep_bwd",
        out_shape=[jax.ShapeDtypeStruct((D, 2 * MEM_W), BF16), jax.ShapeDtypeStruct((8, D), F32)],
        compiler_params=pltpu.CompilerParams(vmem_limit_bytes=V7X_VMEM_LIMIT),
    )(mem, g_mem, mn, wkv, dkv)
    return dw.reshape(N_DEV, D // N_DEV, 2 * MEM_W), dg


def _rearrange_w_in(w):
    def heads128(cols):
        blk = w[:, cols:cols + FOX_W].reshape(D, FOX_H, FOX_DH)
        return jnp.pad(blk, ((0, 0), (0, 0), (0, 128 - FOX_DH))).reshape(D, FOX_H * 128)

    fq, fk, fv, mq, wg = heads128(O_FQ), heads128(O_FK), w[:, O_FV:O_FF], w[:, O_MQ:O_GT], w[:, O_GT:]
    gaff = jnp.concatenate([w[:, O_GA:O_FQ], w[:, O_FF:O_MQ], jnp.zeros((D, 128 - GLA_R - FOX_H), w.dtype)], axis=1)
    wa = jnp.concatenate([w[:, O_GQ:O_GG], fq, fk, fv, mq], axis=1)
    ws = jnp.concatenate([w[:, O_GG:O_GA], gaff], axis=1)
    wp = jnp.concatenate([fq, fk, fv, mq, wg, w[:, O_GQ:O_GG], ws, jnp.zeros((D, P_W - P_GLA - 1024 - S_W), w.dtype)], axis=1)
    return wa, wg, ws, wp


def _restore_w_in_grad(dwp):
    def unheads(off):
        return dwp[:, off:off + FOX_H * 128].reshape(D, FOX_H, 128)[:, :, :FOX_DH].reshape(D, FOX_W)

    g0 = P_GLA + 1024
    return jnp.concatenate([
        dwp[:, P_GLA:g0], dwp[:, g0:g0 + 512], dwp[:, g0 + 512:g0 + 512 + GLA_R], unheads(P_FOX), unheads(P_FOX + 1024),
        dwp[:, P_FOX + 2048:P_FOX + P_FOX_W], dwp[:, g0 + 512 + GLA_R:g0 + 512 + GLA_R + FOX_H], dwp[:, P_MQ:P_GT],
        dwp[:, P_GT:P_GLA]], axis=1)


def _local_step(x, mem, target, p, late_shards):
    S = x.shape[0]
    p = dict(p)
    wa, wg, ws, wp = _rearrange_w_in(p["w_in"])
    wau = jnp.pad(p["w_alpha_up"], ((0, 128 - GLA_R), (0, 0)))
    bfg = jnp.pad(p["b_forget"], ((0, 0), (FF_LANE, 128 - FF_LANE - FOX_H)))
    gh = p["g_gla_head"].reshape(1, GLA_V)

    pa, pg, ps, u = _proj(x, p["g_mix"], wa, wg, ws)
    o_gla, og, sprev, gathered = _gla_fwd(pa, ps, wau, p["b_alpha"], gh, late_shards)
    p.update({n: _unslab(t, ax) for (n, ax), t in zip(BIG[1:], gathered)})
    qa, ka, qat, kat, vt, fox_stats = _fox_prep(pa, ps, bfg)
    fox_tq = _tile(S, FOX_TQ)
    fox_first, fox_narrow_end, fox_wide_end = _fox_live_ranges(fox_stats, fox_tq // _tile(fox_tq, FOX_TK),
                                                               fox_tq // _tile(S, FOX_TK))
    o_fox, lse = _fox_fwd(qa, ka, vt, fox_first)
    mn, mkv = _mem_prep(mem, p["g_mem"], p["w_mem_kv"])
    o_mem = _mem_fwd(pa, mkv)
    y3, mg = _merge(og, o_fox, o_mem, p["w_gla_o"], p["w_fox_o"], p["w_mem_o"], pg)
    h1, u2 = _out_proj(mg, p["w_out"], x, p["g_ffn"])
    a, act = _ff1(u2, p["w_ff1"])
    dh2, dh2b, loss8, dg_final = _ff2_loss(act, p["w_ff2"], h1, p["g_final"].reshape(1, D), target)

    d_a = _dact(dh2b, p["w_ff2"], a)
    dw_ff2 = _wgrad(act, dh2b, "wgrad_ff2", 0)
    dh1, dh1b, dg_ffn = _nt_rmsbwd(d_a, p["w_ff1"], h1, p["g_ffn"], dh2, "dffn", True)
    dw_ff1 = _wgrad(u2, d_a, "wgrad_ff1", 1)
    dy_g, dy_f, dy_m, do_g, do_f, do_m, d_proj = _dmerge(dh1b, p["w_out"], pg, y3, p["w_gla_o"], p["w_fox_o"], p["w_mem_o"])
    dw_out = _wgrad(mg, dh1b, "wgrad_out", 0)
    dw_gla_o = _wgrad(og, dy_g, "wgrad_gla_o", 1)
    dw_fox_o = _wgrad(o_fox, dy_f, "wgrad_fox_o", 1)
    dw_mem_o = _wgrad(o_mem, dy_m, "wgrad_mem_o", 1)
    d_proj, d_mkv = _mem_bwd(pa, mkv, do_m, d_proj)
    dw_mem_kv, dg_mem = _mem_prep_bwd(mem, p["g_mem"], mn, p["w_mem_kv"], d_mkv)
    dob, dobt, delta = _fox_delta(do_f, o_fox)
    dq, dk, dv = _fox_bwd(qa, qat, ka, kat, pa, dob, dobt, lse, delta, fox_narrow_end, fox_wide_end)
    d_proj, dgaff_fox, db_forget = _fox_post(dq, dk, dv, ps, bfg, d_proj)
    ready = dict(w_mem_kv=dw_mem_kv, w_gla_o=dw_gla_o, w_fox_o=dw_fox_o, w_mem_o=dw_mem_o, w_out=dw_out, w_ff1=dw_ff1,
                 w_ff2=dw_ff2)
    d_proj, dw_au, db_alpha, dg_gla, arrived = _gla_bwd(pa, ps, wau, p["b_alpha"], gh, o_gla, do_g, sprev, dgaff_fox, d_proj,
                                                        [ready[n] for n, _ in BIG[1:]])
    dw_in = _slabs(_restore_w_in_grad(_wgrad(u, d_proj, "wgrad_in")), 1).astype(BF16)
    dx, dg_mix, arrived_in = _nt_rmsbwd(d_proj, wp, x, p["g_mix"], dh1, "dmix", False, [dw_in])

    big = dict(zip([n for n, _ in BIG], [arrived_in[0], *arrived]))
    small = dict(g_mix=dg_mix, g_mem=dg_mem, g_ffn=dg_ffn, g_final=dg_final, b_alpha=db_alpha, g_gla_head=dg_gla,
                 b_forget=db_forget, w_alpha_up=dw_au, loss=loss8)
    return dx, big, small


BIG = (("w_in", 1), ("w_mem_kv", 0), ("w_gla_o", 1), ("w_fox_o", 1), ("w_mem_o", 1), ("w_out", 0), ("w_ff1", 1), ("w_ff2", 0))


def _peer(d):
    me = lax.axis_index("x") * 4 + lax.axis_index("y") * 2 + lax.axis_index("c")
    t = (me + d) % N_DEV
    return (t // 4, (t // 2) % 2, t % 2), me


def _exchange_sems(n):
    return [pltpu.SemaphoreType.DMA((n, N_DEV - 1)), pltpu.SemaphoreType.DMA((n, N_DEV - 1)), pltpu.SemaphoreType.DMA((n,))]


def _exchange_call(body, blocks, out_shape, name):
    n = len(blocks)
    any_spec = pl.BlockSpec(memory_space=pl.ANY)
    return pl.pallas_call(body, name=name, in_specs=[any_spec] * n, out_specs=[any_spec] * n, out_shape=out_shape,
                          scratch_shapes=_exchange_sems(n))(*blocks)


class _AllToAll:
    def __init__(self, ins, outs, sems, gather):
        send, recv, loc = sems
        n = len(ins)
        _, me = _peer(0)
        src = (lambda k, j: ins[k]) if gather else (lambda k, j: ins[k].at[j])
        self.local = [pltpu.make_async_copy(src(k, me), outs[k].at[me], loc.at[k]) for k in range(n)]
        self.remote = []
        for d in range(1, N_DEV):
            to, _ = _peer(d)
            self.remote += [pltpu.make_async_remote_copy(
                src_ref=src(k, (me + d) % N_DEV), dst_ref=outs[k].at[me], send_sem=send.at[k, d - 1],
                recv_sem=recv.at[k, d - 1], device_id=to, device_id_type=MESH) for k in range(n)]

    def start(self):
        for cp in self.local + self.remote:
            cp.start()

    def wait(self):
        for cp in self.remote:
            cp.wait_send()
        for cp in self.remote:
            cp.wait_recv()
        for cp in self.local:
            cp.wait()


def _gathered_shapes(shards):
    return [jax.ShapeDtypeStruct((N_DEV,) + b.shape, b.dtype) for b in shards]


def _gather_weights(shards):
    n = len(shards)

    def body(*refs):
        ins, outs = refs[:n], refs[n:2 * n]
        send, recv, loc = refs[2 * n:]
        x, y, c = lax.axis_index("x"), lax.axis_index("y"), lax.axis_index("c")
        sibling = (x, y, 1 - c)
        chips = [(1 - x, y), (x, 1 - y), (1 - x, 1 - y)]
        slot = lambda px, py, pc: px * 4 + py * 2 + pc

        def copy(k, s, block, to, src=None):
            rows = outs[k].at[slot(*block)]
            return pltpu.make_async_remote_copy(src_ref=rows if src is None else src, dst_ref=rows, send_sem=send.at[k, s],
                                                recv_sem=recv.at[k, s], device_id=to, device_id_type=MESH)

        me = (x, y, c)
        own = [pltpu.make_async_copy(ins[k], outs[k].at[slot(*me)], loc.at[k]) for k in range(n)]
        first = [copy(k, 0, me, sibling, src=ins[k]) for k in range(n)]
        first += [copy(k, 1 + j, me, (*chip, c), src=ins[k]) for j, chip in enumerate(chips) for k in range(n)]
        for cp in own + first:
            cp.start()
        passed = []
        for j, chip in enumerate(chips):
            for k in range(n):
                copy(k, 1 + j, (*chip, c), me).wait_recv()
                fwd = copy(k, 4 + j, (*chip, c), sibling)
                fwd.start()
                passed.append(fwd)
        for k in range(n):
            copy(k, 0, sibling, me).wait_recv()
        for j, chip in enumerate(chips):
            for k in range(n):
                copy(k, 4 + j, (*chip, 1 - c), me).wait_recv()
        for cp in first + passed:
            cp.wait_send()
        for cp in own:
            cp.wait()

    return _exchange_call(body, shards, [jax.ShapeDtypeStruct((N_DEV,) + b.shape, b.dtype) for b in shards], "gather_weights")


def _adamw_math(g, w, m, v):
    m2 = ADAM_B1 * m + (1.0 - ADAM_B1) * g
    v2 = ADAM_B2 * v + (1.0 - ADAM_B2) * jnp.square(g)
    m_hat = m2 / (1.0 - ADAM_B1 ** ADAM_STEP)
    v_hat = v2 / (1.0 - ADAM_B2 ** ADAM_STEP)
    delta = -ADAM_LR * (m_hat / (jnp.sqrt(v_hat) + ADAM_EPS) + ADAM_WD * w)
    return delta, m2, v2


def _adamw_sum(parts, w, m, v, name):
    R, C = w.shape
    tr = _tile(R, 128)

    def body(p_ref, w_ref, m_ref, v_ref, g_ref, d_ref, m2_ref, v2_ref):
        g = p_ref[0].astype(F32)
        for j in range(1, p_ref.shape[0]):
            g = g + p_ref[j].astype(F32)
        g_ref[...] = g
        d_ref[...], m2_ref[...], v2_ref[...] = _adamw_math(g, w_ref[...], m_ref[...], v_ref[...])

    blk = pl.BlockSpec((tr, C), lambda i: (i, 0))
    return pl.pallas_call(
        body, name=name, grid=(R // tr,),
        in_specs=[pl.BlockSpec((parts.shape[0], tr, C), lambda i: (0, i, 0)), blk, blk, blk],
        out_specs=[blk] * 4, out_shape=[jax.ShapeDtypeStruct((R, C), F32)] * 4,
        compiler_params=_params("arbitrary"),
    )(parts, w, m, v)


SMALL_ROWS = 24


def _pack_small(d):
    mixed = jnp.concatenate([d["b_alpha"].reshape(1, GLA_K), d["g_gla_head"].reshape(1, GLA_V),
                             jnp.pad(d["b_forget"].reshape(1, FOX_H), ((0, 0), (FF_LANE, 128 - FF_LANE - FOX_H))),
                             jnp.zeros((1, 128), F32)], axis=1)
    rows = [d["g_mix"].reshape(1, D), d["g_mem"].reshape(1, D), d["g_ffn"].reshape(1, D), d["g_final"].reshape(1, D), mixed,
            jnp.zeros((3, D), F32), jnp.pad(d["w_alpha_up"].reshape(GLA_R, GLA_K), ((0, 0), (0, D - GLA_K)))]
    return jnp.concatenate(rows, axis=0)


def _unpack_small(t):
    return dict(g_mix=t[0:1], g_mem=t[1:2], g_ffn=t[2:3], g_final=t[3], b_alpha=t[4:5, 0:GLA_K],
                g_gla_head=t[4:5, GLA_K:GLA_K + GLA_V].reshape(1, GLA_H, GLA_DV),
                b_forget=t[4:5, 768 + FF_LANE:768 + FF_LANE + FOX_H], w_alpha_up=t[8:24, 0:GLA_K].reshape(1, GLA_R, GLA_K))


def _small_allreduce(small, w, m, v):
    def body(gm, gme, gf, gfi, ba, gg, bf, wau, ls, w_ref, m_ref, v_ref, g_ref, d_ref, m2_ref, v2_ref, l_ref,
             buf, send, recv):
        _, me = _peer(0)
        buf[me] = jnp.zeros((SMALL_ROWS, D), F32)
        for r, ref in enumerate((gm, gme, gf, gfi)):
            buf[me, r:r + 1, :] = jnp.sum(ref[...], axis=0, keepdims=True)
        buf[me, 4:5, 0:GLA_K] = jnp.sum(ba[...], axis=0, keepdims=True)
        buf[me, 4:5, GLA_K:GLA_K + GLA_V] = jnp.sum(gg[...], axis=0, keepdims=True)
        buf[me, 4:5, 768:896] = jnp.sum(bf[...], axis=0, keepdims=True)
        lrow = jnp.sum(ls[...], axis=0, keepdims=True)
        lsum = lrow[:, 0:128]
        for c in range(1, D // 128):
            lsum = lsum + lrow[:, 128 * c:128 * (c + 1)]
        buf[me, 4:5, 896:1024] = lsum
        buf[me, 8:24, 0:GLA_K] = wau[0:GLA_R, :]
        remote = []
        for d in range(1, N_DEV):
            to, me = _peer(d)
            cp = pltpu.make_async_remote_copy(src_ref=buf.at[me], dst_ref=buf.at[me], send_sem=send.at[d - 1],
                                              recv_sem=recv.at[d - 1], device_id=to, device_id_type=MESH)
            cp.start()
            remote.append(cp)
        for cp in remote:
            cp.wait_send()
        for cp in remote:
            cp.wait_recv()
        g = buf[0]
        for j in range(1, N_DEV):
            g = g + buf[j]
        g_ref[...] = g
        d_ref[...], m2_ref[...], v2_ref[...] = _adamw_math(g, w_ref[...], m_ref[...], v_ref[...])
        l_ref[...] = g[4:5, 896:1024]

    packed = jax.ShapeDtypeStruct((SMALL_ROWS, D), F32)
    return pl.pallas_call(
        body, name="small_allreduce",
        out_shape=[packed, packed, packed, packed, jax.ShapeDtypeStruct((1, 128), F32)],
        scratch_shapes=[pltpu.VMEM((N_DEV, SMALL_ROWS, D), F32), pltpu.SemaphoreType.DMA((N_DEV - 1,)),
                        pltpu.SemaphoreType.DMA((N_DEV - 1,))],
    )(small["g_mix"], small["g_mem"], small["g_ffn"], small["g_final"], small["b_alpha"], small["g_gla_head"],
      small["b_forget"], small["w_alpha_up"], small["loss"], w, m, v)


def _slabs(g, axis):
    R, C = g.shape
    if axis == 0:
        return g.reshape(N_DEV, R // N_DEV, C)
    return g.reshape(R, N_DEV, C // N_DEV).transpose(1, 0, 2)


def _unslab(t, axis):
    n, r, c = t.shape
    if axis == 0:
        return t.reshape(n * r, c)
    return t.transpose(1, 0, 2).reshape(r, n * c)


def kernel(x, mem, g_mix, w_in, w_alpha_up, b_alpha, b_forget, g_gla_head, g_mem, w_mem_kv, w_gla_o, w_fox_o, w_mem_o, w_out, g_ffn, w_ff1, w_ff2, g_final, loss_target, m_g_mix, m_w_in, m_w_alpha_up, m_b_alpha, m_b_forget, m_g_gla_head, m_g_mem, m_w_mem_kv, m_w_gla_o, m_w_fox_o, m_w_mem_o, m_w_out, m_g_ffn, m_w_ff1, m_w_ff2, m_g_final, v_g_mix, v_w_in, v_w_alpha_up, v_b_alpha, v_b_forget, v_g_gla_head, v_g_mem, v_w_mem_kv, v_w_gla_o, v_w_fox_o, v_w_mem_o, v_w_out, v_g_ffn, v_w_ff1, v_w_ff2, v_g_final):
    names = ["g_mix", "w_in", "w_alpha_up", "b_alpha", "b_forget", "g_gla_head", "g_mem", "w_mem_kv", "w_gla_o", "w_fox_o",
             "w_mem_o", "w_out", "g_ffn", "w_ff1", "w_ff2", "g_final"]
    w = dict(g_mix=g_mix, w_in=w_in, w_alpha_up=w_alpha_up, b_alpha=b_alpha, b_forget=b_forget, g_gla_head=g_gla_head,
             g_mem=g_mem, w_mem_kv=w_mem_kv, w_gla_o=w_gla_o, w_fox_o=w_fox_o, w_mem_o=w_mem_o, w_out=w_out, g_ffn=g_ffn,
             w_ff1=w_ff1, w_ff2=w_ff2, g_final=g_final)
    m = dict(g_mix=m_g_mix, w_in=m_w_in, w_alpha_up=m_w_alpha_up, b_alpha=m_b_alpha, b_forget=m_b_forget,
             g_gla_head=m_g_gla_head, g_mem=m_g_mem, w_mem_kv=m_w_mem_kv, w_gla_o=m_w_gla_o, w_fox_o=m_w_fox_o,
             w_mem_o=m_w_mem_o, w_out=m_w_out, g_ffn=m_g_ffn, w_ff1=m_w_ff1, w_ff2=m_w_ff2, g_final=m_g_final)
    v = dict(g_mix=v_g_mix, w_in=v_w_in, w_alpha_up=v_w_alpha_up, b_alpha=v_b_alpha, b_forget=v_b_forget,
             g_gla_head=v_g_gla_head, g_mem=v_g_mem, w_mem_kv=v_w_mem_kv, w_gla_o=v_w_gla_o, w_fox_o=v_w_fox_o,
             w_mem_o=v_w_mem_o, w_out=v_w_out, g_ffn=v_g_ffn, w_ff1=v_w_ff1, w_ff2=v_w_ff2, g_final=v_g_final)
    me = lax.axis_index("x") * 4 + lax.axis_index("y") * 2 + lax.axis_index("c")

    shard = lambda n: w[n][0].astype(BF16)
    w_in_all, w_au_all = _gather_weights([shard("w_in"), shard("w_alpha_up")])
    p = dict(w_in=_unslab(w_in_all, 1), w_alpha_up=_unslab(w_au_all, 1), g_mix=g_mix, b_alpha=b_alpha, b_forget=b_forget,
             g_gla_head=g_gla_head, g_mem=g_mem, g_ffn=g_ffn, g_final=g_final)

    dx, big, small = _local_step(x[0], mem[0], loss_target[0], p, [shard(n) for n, _ in BIG[1:]])

    out_g, out_d, out_m, out_v = {}, {}, {}, {}
    for n, _ in BIG:
        g_, d_, m_, v_ = _adamw_sum(big[n], w[n][0], m[n][0], v[n][0], "adamw_" + n)
        out_g[n], out_d[n], out_m[n], out_v[n] = g_[None], d_[None], m_[None], v_[None]

    full = lambda d: dict(d, w_alpha_up=jnp.zeros((1, GLA_R, GLA_K), F32))
    gs, ds, ms, vs, lrow = _small_allreduce(small, _pack_small(full(w)), _pack_small(full(m)), _pack_small(full(v)))
    g_s, d_s, m_s, v_s = _unpack_small(gs), _unpack_small(ds), _unpack_small(ms), _unpack_small(vs)
    for n in names:
        if n not in out_g and n != "w_alpha_up":
            out_g[n], out_d[n], out_m[n], out_v[n] = g_s[n], d_s[n], m_s[n], v_s[n]
    g_au = lax.dynamic_slice_in_dim(g_s["w_alpha_up"][0], me * (GLA_K // N_DEV), GLA_K // N_DEV, axis=1)
    g_, d_, m_, v_ = _adamw_sum(g_au[None], w_alpha_up[0], m_w_alpha_up[0], v_w_alpha_up[0], "adamw_w_alpha_up")
    out_g["w_alpha_up"], out_d["w_alpha_up"], out_m["w_alpha_up"], out_v["w_alpha_up"] = g_[None], d_[None], m_[None], v_[None]

    loss = jnp.sum(lrow) * (0.5 / D)
    return (loss, dx[None], *[out_g[n] for n in names], *[out_d[n] for n in names], *[out_m[n] for n in names],
            *[out_v[n] for n in names])
```

```python
import jax
import jax.numpy as jnp
from jax import lax
from jax.experimental import pallas as pl
from jax.experimental.pallas import tpu as pltpu

F32, BF16 = jnp.float32, jnp.bfloat16
HIGHEST = lax.Precision.HIGHEST
MESH = pl.DeviceIdType.MESH

N_DEV = 8
D = 1024
EPS = 1e-6
CHUNK = 64
N_MEM = 256
GLA_H, GLA_DK, GLA_DV = 4, 64, 128
GLA_K, GLA_V, GLA_R = 256, 512, 16
FOX_H, FOX_DH, FOX_W = 8, 64, 512
MEM_H, MEM_DH, MEM_W = 4, 128, 512
D_FF = 4096
D_IN = 6680
FOX_SCALE = 0.125
GLA_SCALE = 0.125
MEM_SCALE = MEM_DH ** -0.5
GLA_TAU_INV = 1.0 / 16.0
NEG = -1e30

O_GQ, O_GK, O_GV, O_GG, O_GA, O_FQ, O_FK, O_FV, O_FF, O_MQ, O_GT = 0, 256, 512, 1024, 1536, 1552, 2064, 2576, 3088, 3096, 3608
A_FQ, A_FK, A_FV, A_MQ, A_W = 1024, 2048, 3072, 3584, 4096
S_W = 640
G_W = 3072
P_FOX, P_FOX_W, P_MQ, P_GT, P_GLA, P_GLA_W, P_W = 0, 2560, 2560, 3072, 6144, 2048, 8192
FF_LANE = 16
AUG = 64
FOX_LIVE = 80

ADAM_LR, ADAM_B1, ADAM_B2, ADAM_EPS, ADAM_WD, ADAM_STEP = 0.001, 0.9, 0.999, 1e-08, 0.01, 10
V7X_VMEM_LIMIT = 48 * 1024 * 1024
FOX_TQ, FOX_TK = 2048, 512


def _params(*sem):
    return pltpu.CompilerParams(dimension_semantics=sem, vmem_limit_bytes=V7X_VMEM_LIMIT)


def _nt(a, b):
    return lax.dot_general(a, b, (((1,), (1,)), ((), ())), preferred_element_type=F32)


def _tn(a, b):
    return lax.dot_general(a, b, (((0,), (0,)), ((), ())), preferred_element_type=F32)


def _nn(a, b):
    return jnp.dot(a, b, preferred_element_type=F32)


def _log_sigmoid(z):
    return jnp.minimum(z, 0.0) - jnp.log(1.0 + jnp.exp(-jnp.abs(z)))


def _sum01(m01, x):
    x1 = x.astype(BF16)
    x2 = (x - x1.astype(F32)).astype(BF16)
    x3 = (x - x1.astype(F32) - x2.astype(F32)).astype(BF16)
    return _nn(m01, x1) + _nn(m01, x2) + _nn(m01, x3)


def _sum8(x):
    return x.reshape(x.shape[0] // 8, 8, x.shape[1]).sum(axis=0)


def _rms(xv):
    r = lax.rsqrt(jnp.mean(xv * xv, axis=-1, keepdims=True) + EPS)
    return r, xv * r


def _rms_bwd(du, g, r, xh):
    w = du * g
    return r * (w - xh * jnp.mean(w * xh, axis=-1, keepdims=True))


def _row_chunks(n, size=256):
    return [slice(r, r + min(size, n)) for r in range(0, n, min(size, n))]


def _tile(n, pref):
    t = min(n, pref)
    assert n % t == 0, (n, t)
    return t


def _proj(x, g, wa, wg, ws):
    S = x.shape[0]
    tm, tn = _tile(S, 1024), 1024
    n_a, n_g = A_W // tn, G_W // tn

    def body(x_ref, g_ref, wa_ref, wg_ref, ws_ref, pa_ref, pg_ref, ps_ref, u_ref, u_s):
        j = pl.program_id(1)

        @pl.when(j == 0)
        def _():
            r, xh = _rms(x_ref[...])
            u_s[...] = (xh * g_ref[...]).astype(BF16)
            u_ref[...] = u_s[...]

        @pl.when(j < n_a)
        def _():
            pa_ref[...] = _nn(u_s[...], wa_ref[...]).astype(BF16)

        @pl.when((j >= n_a) & (j < n_a + n_g))
        def _():
            pg_ref[...] = _nn(u_s[...], wg_ref[...]).astype(BF16)

        @pl.when(j == n_a + n_g)
        def _():
            ps_ref[...] = _nn(u_s[...], ws_ref[...])

    in_a = lambda j: jnp.minimum(j, n_a - 1)
    in_g = lambda j: jnp.clip(j - n_a, 0, n_g - 1)
    row = pl.BlockSpec((tm, D), lambda i, j: (i, 0))
    return pl.pallas_call(
        body, name="proj", grid=(S // tm, n_a + n_g + 1),
        in_specs=[row, pl.BlockSpec((1, D), lambda i, j: (0, 0)), pl.BlockSpec((D, tn), lambda i, j: (0, in_a(j))),
                  pl.BlockSpec((D, tn), lambda i, j: (0, in_g(j))),
                  pl.BlockSpec((D, S_W), lambda i, j: (0, 0), pipeline_mode=pl.Buffered(1))],
        out_specs=[pl.BlockSpec((tm, tn), lambda i, j: (i, in_a(j))), pl.BlockSpec((tm, tn), lambda i, j: (i, in_g(j))),
                   pl.BlockSpec((tm, S_W), lambda i, j: (i, 0)), row],
        out_shape=[jax.ShapeDtypeStruct((S, A_W), BF16), jax.ShapeDtypeStruct((S, G_W), BF16),
                   jax.ShapeDtypeStruct((S, S_W), F32), jax.ShapeDtypeStruct((S, D), BF16)],
        scratch_shapes=[pltpu.VMEM((tm, D), BF16)],
        compiler_params=_params("arbitrary", "arbitrary"),
    )(x, g, wa, wg, ws)


def _wgrad(a, b, name, slab_axis=None):
    S, Ka = a.shape
    N = b.shape[1]
    tka, tn, ts = _tile(Ka, 1024), _tile(N, 1024), _tile(S, 1024)
    n_s = S // ts
    per = N // N_DEV
    slabs_per_step = tn // per

    def body(a_ref, b_ref, o_ref, acc):
        s = pl.program_id(2)

        @pl.when(s == 0)
        def _():
            acc[...] = jnp.zeros_like(acc)

        acc[...] += _tn(a_ref[...].astype(BF16), b_ref[...].astype(BF16))

        @pl.when(s == n_s - 1)
        def _():
            if slab_axis == 1:
                for q in range(slabs_per_step):
                    o_ref[q] = acc[:, per * q:per * (q + 1)].astype(BF16)
            else:
                o_ref[...] = acc[...].astype(o_ref.dtype)

    if slab_axis == 1:
        out_spec = pl.BlockSpec((slabs_per_step, tka, per), lambda i, j, s: (j, i, 0))
        out_shape = jax.ShapeDtypeStruct((N_DEV, Ka, per), BF16)
    else:
        out_spec = pl.BlockSpec((tka, tn), lambda i, j, s: (i, j))
        out_shape = jax.ShapeDtypeStruct((Ka, N), F32 if slab_axis is None else BF16)
    out = pl.pallas_call(
        body, name=name, grid=(Ka // tka, N // tn, n_s),
        in_specs=[pl.BlockSpec((ts, tka), lambda i, j, s: (s, i)), pl.BlockSpec((ts, tn), lambda i, j, s: (s, j))],
        out_specs=out_spec, out_shape=out_shape,
        scratch_shapes=[pltpu.VMEM((tka, tn), F32)],
        compiler_params=_params("arbitrary", "arbitrary", "arbitrary"),
    )(a, b)
    return out.reshape(N_DEV, Ka // N_DEV, N) if slab_axis == 0 else out


def _nt_rmsbwd(a, w, xin, g, dres, name, emit_bf16, slabs=()):
    S, K = a.shape
    tm, tk = _tile(S, 1024), _tile(K, 1024)
    n_i, n_k = S // tm, K // tk
    n_x, n_o = len(slabs), 3 if emit_bf16 else 2

    def body(*refs):
        a_ref, w_ref, x_ref, g_ref, r_ref = refs[:5]
        o_ref = refs[5 + n_x]
        rest = refs[6 + n_x:5 + n_x + n_o] + (refs[5 + 2 * n_x + n_o],)
        dg_ref, acc = rest[-2], rest[-1]
        scatter = lambda: _AllToAll(refs[5:5 + n_x], refs[5 + n_x + n_o:5 + 2 * n_x + n_o], refs[6 + 2 * n_x + n_o:], False)
        i, k = pl.program_id(0), pl.program_id(1)

        if n_x:
            @pl.when((i == 0) & (k == 0))
            def _():
                scatter().start()

        @pl.when(k == 0)
        def _():
            acc[...] = jnp.zeros_like(acc)

        acc[...] += _nt(a_ref[...], w_ref[...])

        @pl.when(k == n_k - 1)
        def _():
            @pl.when(i == 0)
            def _():
                dg_ref[...] = jnp.zeros_like(dg_ref)

            for rows in _row_chunks(tm):
                du = acc[rows, :]
                r, xh = _rms(x_ref[rows, :])
                out = r_ref[rows, :] + _rms_bwd(du, g_ref[...], r, xh)
                o_ref[rows, :] = out
                if emit_bf16:
                    rest[0][rows, :] = out.astype(BF16)
                dg_ref[...] += _sum8(du * xh)

        if n_x:
            @pl.when((i == n_i - 1) & (k == n_k - 1))
            def _():
                scatter().wait()

    row = pl.BlockSpec((tm, D), lambda i, k: (i, 0))
    any_spec = pl.BlockSpec(memory_space=pl.ANY)
    out_shape = [jax.ShapeDtypeStruct((S, D), F32)]
    out_specs = [row]
    if emit_bf16:
        out_shape.append(jax.ShapeDtypeStruct((S, D), BF16))
        out_specs.append(row)
    out_shape.append(jax.ShapeDtypeStruct((8, D), F32))
    out_specs.append(pl.BlockSpec((8, D), lambda i, k: (0, 0)))
    out = pl.pallas_call(
        body, name=name, grid=(n_i, n_k),
        in_specs=[pl.BlockSpec((tm, tk), lambda i, k: (i, k)), pl.BlockSpec((D, tk), lambda i, k: (0, k)),
                  row, pl.BlockSpec((1, D), lambda i, k: (0, 0)), row] + [any_spec] * n_x,
        out_specs=out_specs + [any_spec] * n_x,
        out_shape=out_shape + [jax.ShapeDtypeStruct(b.shape, b.dtype) for b in slabs],
        scratch_shapes=[pltpu.VMEM((tm, D), F32)] + (_exchange_sems(n_x) if n_x else []),
        compiler_params=_params("arbitrary", "arbitrary"),
    )(a, w, xin, g, dres, *slabs)
    return (*out[:n_o], out[n_o:]) if n_x else out


def _merge(og, ofox, omem, wg, wf, wm, pg):
    S = og.shape[0]
    tm = _tile(S, 512)

    def body(og_ref, of_ref, om_ref, wg_ref, wf_ref, wm_ref, pg_ref, y_ref, mg_ref):
        tot = None
        for i, (o_ref, w_ref) in enumerate(((og_ref, wg_ref), (of_ref, wf_ref), (om_ref, wm_ref))):
            y = _nn(o_ref[...].astype(BF16), w_ref[...])
            y_ref[i] = y.astype(BF16)
            t = jax.nn.sigmoid(pg_ref[:, D * i:D * (i + 1)].astype(F32)) * y
            tot = t if tot is None else tot + t
        mg_ref[...] = tot.astype(BF16)

    o_spec = pl.BlockSpec((tm, 512), lambda i: (i, 0))
    w_spec = pl.BlockSpec((512, D), lambda i: (0, 0))
    return pl.pallas_call(
        body, name="merge", grid=(S // tm,),
        in_specs=[o_spec, o_spec, o_spec, w_spec, w_spec, w_spec, pl.BlockSpec((tm, G_W), lambda i: (i, 0))],
        out_specs=[pl.BlockSpec((3, tm, D), lambda i: (0, i, 0)), pl.BlockSpec((tm, D), lambda i: (i, 0))],
        out_shape=[jax.ShapeDtypeStruct((3, S, D), BF16), jax.ShapeDtypeStruct((S, D), BF16)],
        compiler_params=_params("arbitrary"),
    )(og, ofox, omem, wg, wf, wm, pg)


def _out_proj(mg, w_out, x, g_ffn):
    S = x.shape[0]
    tm = _tile(S, 512)

    def body(mg_ref, w_ref, x_ref, g_ref, h_ref, u_ref):
        h = x_ref[...] + _nn(mg_ref[...], w_ref[...])
        h_ref[...] = h
        r, xh = _rms(h)
        u_ref[...] = (xh * g_ref[...]).astype(BF16)

    row = pl.BlockSpec((tm, D), lambda i: (i, 0))
    return pl.pallas_call(
        body, name="out_proj", grid=(S // tm,),
        in_specs=[row, pl.BlockSpec((D, D), lambda i: (0, 0)), row, pl.BlockSpec((1, D), lambda i: (0, 0))],
        out_specs=[row, row],
        out_shape=[jax.ShapeDtypeStruct((S, D), F32), jax.ShapeDtypeStruct((S, D), BF16)],
        compiler_params=_params("arbitrary"),
    )(mg, w_out, x, g_ffn)


def _ff1(u2, w1):
    S = u2.shape[0]
    tm, tn = _tile(S, 1024), 1024

    def body(u_ref, w_ref, a_ref, act_ref):
        a = _nn(u_ref[...], w_ref[...])
        a_ref[...] = a.astype(BF16)
        act_ref[...] = jnp.square(jnp.maximum(a, 0.0)).astype(BF16)

    blk = pl.BlockSpec((tm, tn), lambda i, j: (i, j))
    return pl.pallas_call(
        body, name="ff1", grid=(S // tm, D_FF // tn),
        in_specs=[pl.BlockSpec((tm, D), lambda i, j: (i, 0)), pl.BlockSpec((D, tn), lambda i, j: (0, j))],
        out_specs=[blk, blk],
        out_shape=[jax.ShapeDtypeStruct((S, D_FF), BF16), jax.ShapeDtypeStruct((S, D_FF), BF16)],
        compiler_params=_params("arbitrary", "arbitrary"),
    )(u2, w1)


def _ff2_loss(act, w2, h1, g_final, target):
    S = act.shape[0]
    tm, tk = _tile(S, 1024), 1024
    n_k = D_FF // tk

    def body(a_ref, w_ref, h_ref, g_ref, t_ref, d_ref, db_ref, ls_ref, dg_ref, acc):
        i, k = pl.program_id(0), pl.program_id(1)

        @pl.when(k == 0)
        def _():
            acc[...] = jnp.zeros_like(acc)

        acc[...] += _nn(a_ref[...], w_ref[...])

        @pl.when(k == n_k - 1)
        def _():
            @pl.when(i == 0)
            def _():
                ls_ref[...] = jnp.zeros_like(ls_ref)
                dg_ref[...] = jnp.zeros_like(dg_ref)

            gf = g_ref[...]
            for rows in _row_chunks(tm):
                r, xh = _rms(h_ref[rows, :] + acc[rows, :])
                err = xh * gf - t_ref[rows, :]
                dy = err * (1.0 / D)
                dh = _rms_bwd(dy, gf, r, xh)
                d_ref[rows, :] = dh
                db_ref[rows, :] = dh.astype(BF16)
                ls_ref[...] += _sum8(err * err)
                dg_ref[...] += _sum8(dy * xh)

    row = pl.BlockSpec((tm, D), lambda i, k: (i, 0))
    part = pl.BlockSpec((8, D), lambda i, k: (0, 0))
    return pl.pallas_call(
        body, name="ff2_loss", grid=(S // tm, n_k),
        in_specs=[pl.BlockSpec((tm, tk), lambda i, k: (i, k)), pl.BlockSpec((tk, D), lambda i, k: (k, 0)),
                  row, pl.BlockSpec((1, D), lambda i, k: (0, 0)), row],
        out_specs=[row, row, part, part],
        out_shape=[jax.ShapeDtypeStruct((S, D), F32), jax.ShapeDtypeStruct((S, D), BF16),
                   jax.ShapeDtypeStruct((8, D), F32), jax.ShapeDtypeStruct((8, D), F32)],
        scratch_shapes=[pltpu.VMEM((tm, D), F32)],
        compiler_params=_params("arbitrary", "arbitrary"),
    )(act, w2, h1, g_final, target)


def _dact(dh2b, w2, a):
    S = a.shape[0]
    tm, tn = _tile(S, 1024), 1024

    def body(d_ref, w_ref, a_ref, o_ref):
        da = _nt(d_ref[...], w_ref[...])
        o_ref[...] = (da * (2.0 * jnp.maximum(a_ref[...].astype(F32), 0.0))).astype(BF16)

    blk = pl.BlockSpec((tm, tn), lambda i, j: (i, j))
    return pl.pallas_call(
        body, name="dact", grid=(S // tm, D_FF // tn),
        in_specs=[pl.BlockSpec((tm, D), lambda i, j: (i, 0)), pl.BlockSpec((tn, D), lambda i, j: (j, 0)), blk],
        out_specs=blk, out_shape=jax.ShapeDtypeStruct((S, D_FF), BF16),
        compiler_params=_params("arbitrary", "arbitrary"),
    )(dh2b, w2, a)


def _dmerge(dh1b, w_out, pg, y3, wg, wf, wm):
    S = dh1b.shape[0]
    tm = _tile(S, 512)

    def body(d_ref, w_ref, pg_ref, y_ref, wg_ref, wf_ref, wm_ref, *outs):
        dy_refs, do_refs, dg_ref = outs[0:3], outs[3:6], outs[6]
        dm = _nt(d_ref[...], w_ref[...])
        for i, wo_ref in enumerate((wg_ref, wf_ref, wm_ref)):
            gt = jax.nn.sigmoid(pg_ref[:, D * i:D * (i + 1)].astype(F32))
            dy = (dm * gt).astype(BF16)
            dy_refs[i][...] = dy
            do_refs[i][...] = _nt(dy, wo_ref[...])
            dg_ref[:, D * i:D * (i + 1)] = (dm * y_ref[i].astype(F32) * (gt * (1.0 - gt))).astype(BF16)

    row = pl.BlockSpec((tm, D), lambda i: (i, 0))
    half = pl.BlockSpec((tm, 512), lambda i: (i, 0))
    w_spec = pl.BlockSpec((512, D), lambda i: (0, 0))
    return pl.pallas_call(
        body, name="dmerge", grid=(S // tm,),
        in_specs=[row, pl.BlockSpec((D, D), lambda i: (0, 0)), pl.BlockSpec((tm, G_W), lambda i: (i, 0)),
                  pl.BlockSpec((3, tm, D), lambda i: (0, i, 0)), w_spec, w_spec, w_spec],
        out_specs=[row, row, row, half, half, half, pl.BlockSpec((tm, G_W), lambda i: (i, P_GT // G_W))],
        out_shape=[jax.ShapeDtypeStruct((S, D), BF16)] * 3 + [jax.ShapeDtypeStruct((S, 512), F32)] * 3
        + [jax.ShapeDtypeStruct((S, P_W), BF16)],
        compiler_params=_params("arbitrary"),
    )(dh1b, w_out, pg, y3, wg, wf, wm)


def _gla_block_terms(gq_ref, gk_ref, ps_ref, wau_ref, ba_ref, tb):
    gaff = ps_ref[:, 512:640]
    z = _nn(gaff.astype(BF16), wau_ref[...]) + ba_ref[...]
    la = _log_sigmoid(z) * GLA_TAU_INV
    rr = lax.broadcasted_iota(jnp.int32, (tb, tb), 0)
    cc = lax.broadcasted_iota(jnp.int32, (tb, tb), 1)
    same = jnp.right_shift(rr, 6) == jnp.right_shift(cc, 6)
    tri = jnp.where(same & (cc <= rr), 1.0, 0.0).astype(BF16)
    ones = jnp.where(same, 1.0, 0.0).astype(BF16)
    b = _sum01(tri, la)
    bl = _sum01(ones, la)
    e_pos, e_neg, e_last, dec = jnp.exp(b), jnp.exp(-b), jnp.exp(bl - b), jnp.exp(bl)
    q = gq_ref[...].astype(F32) * GLA_SCALE
    k = gk_ref[...].astype(F32)
    return dict(gaff=gaff, z=z, same=same, rr=rr, cc=cc, ones=ones, e_pos=e_pos, e_neg=e_neg, e_last=e_last, dec=dec,
                qp=q * e_pos, qn=q * e_neg, kn=k * e_neg, kp=k * e_pos, kd=k * e_last)


def _head_masked(x, store):
    lane = lax.broadcasted_iota(jnp.int32, x.shape, 1)
    for h in range(GLA_H):
        store[:, h] = jnp.where(jnp.right_shift(lane, 6) == h, x, 0.0).astype(BF16).reshape(-1, CHUNK, GLA_K)


def _lower4():
    t = jnp.bitwise_and(lax.broadcasted_iota(jnp.int32, (GLA_H * CHUNK, CHUNK), 0), CHUNK - 1)
    return t >= lax.broadcasted_iota(jnp.int32, (GLA_H * CHUNK, CHUNK), 1)


def _stack_heads(ref, rows):
    return jnp.concatenate([ref[rows, GLA_DV * h:GLA_DV * (h + 1)] for h in range(GLA_H)], axis=0)


def _gla_fwd(pa, ps, wau, ba, gh, shards):
    S = pa.shape[0]
    tb = _tile(S, 512)
    n_c = tb // CHUNK
    n_b = S // tb
    n_x = len(shards)

    def body(*refs):
        gq_ref, gk_ref, gv_ref, ps_ref, wau_ref, ba_ref, gh_ref = refs[:7]
        o_ref, og_ref, sp_ref = refs[7 + n_x:10 + n_x]
        qpm, qnm, kdm, kn_s, kp_s, dec_s, state = refs[10 + 2 * n_x:17 + 2 * n_x]
        gather = lambda: _AllToAll(refs[7:7 + n_x], refs[10 + n_x:10 + 2 * n_x], refs[17 + 2 * n_x:], True)

        @pl.when(pl.program_id(0) == 0)
        def _():
            state[...] = jnp.zeros_like(state)
            gather().start()

        t = _gla_block_terms(gq_ref, gk_ref, ps_ref, wau_ref, ba_ref, tb)
        _head_masked(t["qp"], qpm)
        _head_masked(t["qn"], qnm)
        _head_masked(t["kd"], kdm)
        kn_s[...] = t["kn"].astype(BF16)
        kp_s[...] = t["kp"].astype(BF16)
        dec_s[...] = t["dec"]
        lower = _lower4()

        sp = state[...]
        for c in range(n_c):
            rows = slice(c * CHUNK, (c + 1) * CHUNK)
            sp_ref[c] = sp
            qp, qn, kd = (s[c].reshape(GLA_H * CHUNK, GLA_K) for s in (qpm, qnm, kdm))
            attn = jnp.where(lower, _nt(qp, kn_s[rows, :]), _nt(qn, kp_s[rows, :])).astype(BF16)
            inter = _nt(qp, sp.astype(BF16))
            for h in range(GLA_H):
                mine = slice(CHUNK * h, CHUNK * (h + 1))
                cols = slice(GLA_DV * h, GLA_DV * (h + 1))
                o_ref[rows, cols] = _nn(attn[mine], gv_ref[rows, cols]) + inter[mine]
            sp = sp * dec_s[c * CHUNK:c * CHUNK + 1, :] + _tn(_stack_heads(gv_ref, rows), kd)
        state[...] = sp
        for h in range(GLA_H):
            cols = slice(GLA_DV * h, GLA_DV * (h + 1))
            r, xh = _rms(o_ref[:, cols])
            gg = ps_ref[:, cols]
            og_ref[:, cols] = ((xh * gh_ref[:, cols]) * (gg * jax.nn.sigmoid(gg))).astype(BF16)

        @pl.when(pl.program_id(0) == n_b - 1)
        def _():
            gather().wait()

    any_spec = pl.BlockSpec(memory_space=pl.ANY)
    out = pl.pallas_call(
        body, name="gla_fwd", grid=(n_b,),
        in_specs=[pl.BlockSpec((tb, GLA_K), lambda i: (i, 0)), pl.BlockSpec((tb, GLA_K), lambda i: (i, 1)),
                  pl.BlockSpec((tb, GLA_V), lambda i: (i, 1)), pl.BlockSpec((tb, S_W), lambda i: (i, 0)),
                  pl.BlockSpec((128, GLA_K), lambda i: (0, 0)), pl.BlockSpec((1, GLA_K), lambda i: (0, 0)),
                  pl.BlockSpec((1, GLA_V), lambda i: (0, 0))] + [any_spec] * n_x,
        out_specs=[pl.BlockSpec((tb, GLA_V), lambda i: (i, 0)), pl.BlockSpec((tb, GLA_V), lambda i: (i, 0)),
                   pl.BlockSpec((n_c, GLA_DV, GLA_K), lambda i: (i, 0, 0))] + [any_spec] * n_x,
        out_shape=[jax.ShapeDtypeStruct((S, GLA_V), F32), jax.ShapeDtypeStruct((S, GLA_V), BF16),
                   jax.ShapeDtypeStruct((S // CHUNK, GLA_DV, GLA_K), F32)] + _gathered_shapes(shards),
        scratch_shapes=[pltpu.VMEM((n_c, GLA_H, CHUNK, GLA_K), BF16)] * 3
        + [pltpu.VMEM((tb, GLA_K), BF16), pltpu.VMEM((tb, GLA_K), BF16), pltpu.VMEM((tb, GLA_K), F32),
           pltpu.VMEM((GLA_DV, GLA_K), F32)] + _exchange_sems(n_x),
        compiler_params=_params("arbitrary"),
    )(pa, pa, pa, ps, wau, ba, gh, *shards)
    return out[0], out[1], out[2], out[3:]


def _gla_bwd(pa, ps, wau, ba, gh, o_gla, d_og, sprev, dgaff_fox, d_proj, slabs):
    S = pa.shape[0]
    tb = _tile(S, 512)
    n_c = tb // CHUNK
    n_b = S // tb
    n_x = len(slabs)
    c_gk, c_gv, c_gg, c_ga, c_end = GLA_K, 2 * GLA_K, 2 * GLA_K + GLA_V, 2 * GLA_K + 2 * GLA_V, 2 * GLA_K + 2 * GLA_V + 128

    def body(*refs):
        gq_ref, gk_ref, gv_ref, ps_ref, wau_ref, ba_ref, gh_ref, o_ref, dog_ref, sp_ref, dfx_ref = refs[:11]
        dp_ref, dwau_ref, dba_ref, dgh_ref = refs[12 + n_x:16 + n_x]
        (qpm, qnm, kdm, kn_s, kp_s, dec_s, do_s, dqp_s, dqn_s, dkn_s, dkp_s, dkd_s, ddec_s,
         dstate) = refs[16 + 2 * n_x:30 + 2 * n_x]
        scatter = lambda: _AllToAll(refs[12:12 + n_x], refs[16 + n_x:16 + 2 * n_x], refs[30 + 2 * n_x:], False)
        first = pl.program_id(0) == 0
        dp_ref[:, c_end:] = jnp.zeros((tb, P_GLA_W - c_end), BF16)

        @pl.when(first)
        def _():
            dstate[...] = jnp.zeros_like(dstate)
            scatter().start()

        t = _gla_block_terms(gq_ref, gk_ref, ps_ref, wau_ref, ba_ref, tb)
        _head_masked(t["qp"], qpm)
        _head_masked(t["qn"], qnm)
        _head_masked(t["kd"], kdm)
        kn_s[...] = t["kn"].astype(BF16)
        kp_s[...] = t["kp"].astype(BF16)
        dec_s[...] = t["dec"]

        dgh_parts = []
        for h in range(GLA_H):
            cols = slice(GLA_DV * h, GLA_DV * (h + 1))
            r, xh = _rms(o_ref[:, cols])
            g = gh_ref[:, cols]
            gg = ps_ref[:, cols]
            sg = jax.nn.sigmoid(gg)
            d_out = dog_ref[:, cols]
            dp_ref[:, c_gg + GLA_DV * h:c_gg + GLA_DV * (h + 1)] = (d_out * (xh * g) * (sg * (1.0 + gg * (1.0 - sg)))).astype(BF16)
            d_on = d_out * (gg * sg)
            dgh_parts.append(_sum8(d_on * xh))
            do_s[:, cols] = _rms_bwd(d_on, g, r, xh).astype(BF16)
        dgh_part = jnp.concatenate(dgh_parts, axis=1)

        lower = _lower4()
        lane = lax.broadcasted_iota(jnp.int32, (CHUNK, GLA_K), 1)

        def own_columns(stacked):
            return sum(jnp.where(jnp.right_shift(lane, 6) == h, stacked[CHUNK * h:CHUNK * (h + 1)], 0.0) for h in range(GLA_H))

        ds_next = dstate[...]
        for c in reversed(range(n_c)):
            rows = slice(c * CHUNK, (c + 1) * CHUNK)
            dsb = ds_next.astype(BF16)
            sp = sp_ref[c]
            knc, kpc = kn_s[rows, :], kp_s[rows, :]
            qp, qn, kd = (s[c].reshape(GLA_H * CHUNK, GLA_K) for s in (qpm, qnm, kdm))
            v4, do4 = _stack_heads(gv_ref, rows), _stack_heads(do_s, rows)
            ddec_s[rows, :] = jnp.broadcast_to(jnp.sum(ds_next * sp, axis=0, keepdims=True), (CHUNK, GLA_K))
            attn = jnp.where(lower, _nt(qp, knc), _nt(qn, kpc)).astype(BF16)
            da = jnp.concatenate([_nt(do4[CHUNK * h:CHUNK * (h + 1)], v4[CHUNK * h:CHUNK * (h + 1)]) for h in range(GLA_H)],
                                 axis=0)
            dac = jnp.where(lower, da, 0.0).astype(BF16)
            daa = jnp.where(lower, 0.0, da).astype(BF16)
            dqp_s[rows, :] = own_columns(_nn(dac, knc) + _nn(do4, sp.astype(BF16)))
            dqn_s[rows, :] = own_columns(_nn(daa, kpc))
            dkd_s[rows, :] = own_columns(_nn(v4, dsb))
            dkn_s[rows, :] = _tn(dac, qp)
            dkp_s[rows, :] = _tn(daa, qn)
            dv_state = _nt(kd, dsb)
            for h in range(GLA_H):
                mine = slice(CHUNK * h, CHUNK * (h + 1))
                dp_ref[rows, c_gv + GLA_DV * h:c_gv + GLA_DV * (h + 1)] = (_tn(attn[mine], do4[mine]) + dv_state[mine]).astype(BF16)
            ds_next = ds_next * dec_s[c * CHUNK:c * CHUNK + 1, :] + _tn(do4, qp)
        dstate[...] = ds_next

        dqp, dqn, dkn, dkp, dkd = dqp_s[...], dqn_s[...], dkn_s[...], dkp_s[...], dkd_s[...]
        dp_ref[:, 0:c_gk] = ((dqp * t["e_pos"] + dqn * t["e_neg"]) * GLA_SCALE).astype(BF16)
        dp_ref[:, c_gk:c_gv] = (dkn * t["e_neg"] + dkp * t["e_pos"] + dkd * t["e_last"]).astype(BF16)
        kd_term = dkd * t["kd"]
        db = dqp * t["qp"] - dqn * t["qn"] - dkn * t["kn"] + dkp * t["kp"] - kd_term
        upper = jnp.where(t["same"] & (t["cc"] >= t["rr"]), 1.0, 0.0).astype(BF16)
        dla = (_sum01(upper, db) + _sum01(t["ones"], kd_term)
               + ddec_s[...] * t["dec"])
        dz = dla * GLA_TAU_INV * jax.nn.sigmoid(-t["z"])
        dzb = dz.astype(BF16)
        dp_ref[:, c_ga:c_end] = (_nt(dzb, wau_ref[...]) + dfx_ref[...]).astype(BF16)
        dwau_part = _tn(t["gaff"].astype(BF16), dzb)
        dba_part = _sum8(dz)

        @pl.when(first)
        def _():
            dwau_ref[...] = dwau_part
            dba_ref[...] = dba_part
            dgh_ref[...] = dgh_part

        @pl.when(jnp.logical_not(first))
        def _():
            dwau_ref[...] += dwau_part
            dba_ref[...] += dba_part
            dgh_ref[...] += dgh_part

        @pl.when(pl.program_id(0) == n_b - 1)
        def _():
            scatter().wait()

    rev = lambda i: (n_b - 1 - i, 0)
    f32k = pltpu.VMEM((tb, GLA_K), F32)
    bf4 = pltpu.VMEM((n_c, GLA_H, CHUNK, GLA_K), BF16)
    any_spec = pl.BlockSpec(memory_space=pl.ANY)
    out = pl.pallas_call(
        body, name="gla_bwd", grid=(n_b,),
        in_specs=[pl.BlockSpec((tb, GLA_K), rev), pl.BlockSpec((tb, GLA_K), lambda i: (n_b - 1 - i, 1)),
                  pl.BlockSpec((tb, GLA_V), lambda i: (n_b - 1 - i, 1)), pl.BlockSpec((tb, S_W), rev),
                  pl.BlockSpec((128, GLA_K), lambda i: (0, 0)), pl.BlockSpec((1, GLA_K), lambda i: (0, 0)),
                  pl.BlockSpec((1, GLA_V), lambda i: (0, 0)), pl.BlockSpec((tb, GLA_V), rev), pl.BlockSpec((tb, GLA_V), rev),
                  pl.BlockSpec((n_c, GLA_DV, GLA_K), lambda i: (n_b - 1 - i, 0, 0)), pl.BlockSpec((tb, 128), rev),
                  any_spec] + [any_spec] * n_x,
        out_specs=[pl.BlockSpec((tb, P_GLA_W), lambda i: (n_b - 1 - i, P_GLA // P_GLA_W)),
                   pl.BlockSpec((128, GLA_K), lambda i: (0, 0)), pl.BlockSpec((8, GLA_K), lambda i: (0, 0)),
                   pl.BlockSpec((8, GLA_V), lambda i: (0, 0))] + [any_spec] * n_x,
        out_shape=[jax.ShapeDtypeStruct((S, P_W), BF16), jax.ShapeDtypeStruct((128, GLA_K), F32),
                   jax.ShapeDtypeStruct((8, GLA_K), F32), jax.ShapeDtypeStruct((8, GLA_V), F32)]
        + [jax.ShapeDtypeStruct(b.shape, b.dtype) for b in slabs],
        input_output_aliases={11: 0},
        scratch_shapes=[bf4, bf4, bf4, pltpu.VMEM((tb, GLA_K), BF16), pltpu.VMEM((tb, GLA_K), BF16), f32k,
                        pltpu.VMEM((tb, GLA_V), BF16), f32k, f32k, f32k, f32k, f32k, f32k, pltpu.VMEM((GLA_DV, GLA_K), F32)]
        + _exchange_sems(n_x),
        compiler_params=_params("arbitrary"),
    )(pa, pa, pa, ps, wau, ba, gh, o_gla, d_og, sprev, dgaff_fox, d_proj, *slabs)
    return out[0], out[1], out[2], out[3], out[4:]


def _split3(x):
    x1 = x.astype(BF16).astype(F32)
    x2 = (x - x1).astype(BF16).astype(F32)
    x3 = (x - x1 - x2).astype(BF16).astype(F32)
    return x1, x2, x3


def _fox_prep(pa, ps, bfg):
    S = pa.shape[0]
    tm = _tile(S, 512)

    def body(ps_ref, b_ref, fq_ref, fk_ref, fv_ref, q_ref, k_ref, qt_ref, kt_ref, vt_ref, st_ref, carry):
        @pl.when(pl.program_id(0) == 0)
        def _():
            carry[...] = jnp.zeros_like(carry)

        vt_ref[...] = fv_ref[...].astype(F32).T.astype(BF16)
        lf = _log_sigmoid(ps_ref[...] + b_ref[...])
        rr = lax.broadcasted_iota(jnp.int32, (tm, tm), 0)
        cc = lax.broadcasted_iota(jnp.int32, (tm, tm), 1)
        tri = jnp.where(cc <= rr, 1.0, 0.0).astype(F32)
        f = jnp.dot(tri, lf, preferred_element_type=F32, precision=HIGHEST) + carry[0:1, :]
        carry[...] = jnp.broadcast_to(f[tm - 1:tm, :], carry.shape)
        f1, f2, f3 = _split3(f)
        lane = lax.broadcasted_iota(jnp.int32, (tm, 128), 1)
        st_row = lax.broadcasted_iota(jnp.int32, (8, 128), 0)
        st_lane = lax.broadcasted_iota(jnp.int32, (8, 128), 1)
        stats = jnp.zeros((8, 128), F32)
        for h in range(FOX_H):
            cols = slice(128 * h, 128 * (h + 1))
            c = FF_LANE + h
            a1, a2, a3 = f1[:, c:c + 1], f2[:, c:c + 1], f3[:, c:c + 1]
            q = fq_ref[:, cols].astype(F32) * FOX_SCALE
            k = fk_ref[:, cols].astype(F32)
            fh = f[:, c:c + 1]
            vals = (jnp.max(jnp.sum(q * q, axis=-1, keepdims=True)), jnp.max(jnp.sum(k * k, axis=-1, keepdims=True)),
                    jnp.max(fh), jnp.min(fh), jnp.min(jnp.sum(q * k, axis=-1, keepdims=True)))
            for n, val in enumerate(vals):
                stats = jnp.where((st_row == h) & (st_lane == n), val, stats)
            for n, a in enumerate((a1, a2, a3)):
                q = jnp.where(lane == AUG + n, a, q)
                k = jnp.where(lane == AUG + 3 + n, -a, k)
            q = jnp.where((lane >= AUG + 3) & (lane < AUG + 6), 1.0, q)
            k = jnp.where((lane >= AUG) & (lane < AUG + 3), 1.0, k)
            q_ref[:, cols] = q.astype(BF16)
            k_ref[:, cols] = k.astype(BF16)
            qt_ref[cols, :] = q.T.astype(BF16)
            kt_ref[cols, :] = k.T.astype(BF16)
        st_ref[0] = stats

    wide = lambda j: pl.BlockSpec((tm, 1024), lambda i: (i, j))
    tall = lambda n: pl.BlockSpec((n, tm), lambda i: (0, i))
    return pl.pallas_call(
        body, name="fox_prep", grid=(S // tm,),
        in_specs=[pl.BlockSpec((tm, 128), lambda i: (i, 4)), pl.BlockSpec((1, 128), lambda i: (0, 0)), wide(1), wide(2),
                  pl.BlockSpec((tm, FOX_W), lambda i: (i, A_FV // FOX_W))],
        out_specs=[wide(0), wide(0), tall(1024), tall(1024), tall(FOX_W), pl.BlockSpec((1, 8, 128), lambda i: (i, 0, 0))],
        out_shape=[jax.ShapeDtypeStruct((S, 1024), BF16), jax.ShapeDtypeStruct((S, 1024), BF16),
                   jax.ShapeDtypeStruct((1024, S), BF16), jax.ShapeDtypeStruct((1024, S), BF16),
                   jax.ShapeDtypeStruct((FOX_W, S), BF16), jax.ShapeDtypeStruct((S // tm, 8, 128), F32)],
        scratch_shapes=[pltpu.VMEM((8, 128), F32)],
        compiler_params=_params("arbitrary"),
    )(ps, bfg, pa, pa, pa)


FOX_PRUNE_AT = -90.0


def _fox_live_ranges(stats, n_sub, ratio):
    n_b = stats.shape[0]
    q2, k2, f_max, f_min, own = (stats[:, :, n].T for n in range(5))
    slack = 0.01 * jnp.sqrt(q2 * k2) + 1e-5 * jnp.abs(f_max) + 1.0
    bound = (1.01 * jnp.sqrt(q2[:, :, None] * k2[:, None, :]) + (f_max + slack - own)[:, :, None]
             - (f_min - 1e-5 * jnp.abs(f_min))[:, None, :])
    blocks = jnp.arange(n_b)
    dead = (bound <= FOX_PRUNE_AT) & (blocks[None, :] < blocks[:, None])[None]
    dead_fwd = dead.reshape(FOX_H, n_b // n_sub, n_sub, n_b).all(axis=2)
    first = jnp.sum(jnp.cumprod(dead_fwd.astype(jnp.int32), axis=2), axis=2)
    last_live = n_b - 1 - jnp.sum(jnp.cumprod(dead[:, ::-1, :].astype(jnp.int32), axis=1), axis=1)
    first_wide = blocks // ratio + 1
    narrow_end = jnp.minimum(jnp.minimum(first_wide * ratio, n_b)[None], last_live + 1)
    wide_end = jnp.where(last_live >= (first_wide * ratio)[None], last_live // ratio + 1, first_wide[None])
    return first.astype(jnp.int32), narrow_end.astype(jnp.int32), wide_end.astype(jnp.int32)


def _fox_fwd(qa, ka, vt, first):
    S = qa.shape[0]
    tq = _tile(S, FOX_TQ)
    tk = _tile(tq, FOX_TK)
    n_sub = tq // tk

    def body(first_ref, q_ref, k_ref, vt_ref, o_ref, lse_ref):
        pair, i = pl.program_id(0), pl.program_id(1)
        both = lambda f: tuple(f(hh) for hh in range(2))

        def blk(j, carry, diag, heads=(0, 1)):
            ks = pl.ds(pl.multiple_of(j * tk, tk), tk)
            q0 = 0 if diag is None else diag * tk

            def head(hh):
                if hh not in heads:
                    return carry[hh]
                m, l, acc = carry[hh]
                mo, lo, ao = m[:, q0:], l[:, q0:], acc[:, q0:]
                s = _nt(k_ref[ks, 128 * hh:128 * (hh + 1)], q_ref[q0:, 128 * hh:128 * (hh + 1)])
                if diag is not None:
                    live = lax.broadcasted_iota(jnp.int32, s.shape, 1) >= lax.broadcasted_iota(jnp.int32, s.shape, 0)
                    s = jnp.where(live, s, NEG)
                mn = jnp.maximum(mo, jnp.max(s, axis=0, keepdims=True))
                p = jnp.exp(s - mn)
                al = jnp.exp(mo - mn)
                ln = al * lo + jnp.sum(p, axis=0, keepdims=True)
                an = al * ao + _nn(vt_ref[FOX_DH * hh:FOX_DH * (hh + 1), ks], p.astype(BF16))
                if q0:
                    mn, ln, an = (jnp.concatenate([old[:, :q0], new], axis=1) for old, new in ((m, mn), (l, ln), (acc, an)))
                return mn, ln, an

            return both(head)

        one = (jnp.full((1, tq), NEG, F32), jnp.zeros((1, tq), F32), jnp.zeros((FOX_DH, tq), F32))
        past = i * n_sub
        f0, f1 = first_ref[2 * pair, i], first_ref[2 * pair + 1, i]
        join = jnp.maximum(f0, f1)
        solo = lambda hh: lambda c: lax.fori_loop(jnp.minimum(f0, f1), join, lambda j, cc: blk(j, cc, None, (hh,)), c)
        carry = lax.cond(f0 < f1, solo(0), solo(1), (one, one))
        n_both = past - join
        carry = lax.fori_loop(0, n_both // 2, lambda jj, c: blk(join + 2 * jj + 1, blk(join + 2 * jj, c, None), None), carry)
        carry = lax.cond(n_both % 2 == 1, lambda c: blk(past - 1, c, None), lambda c: c, carry)
        for d in range(n_sub):
            carry = blk(past + d, carry, d)
        (m0, l0, a0), (m1, l1, a1) = carry
        o_ref[...] = jnp.concatenate([a0 / l0, a1 / l1], axis=0).T
        lse_ref[0, 0:1, :] = m0 + jnp.log(l0)
        lse_ref[0, 1:2, :] = m1 + jnp.log(l1)
        lse_ref[0, 2:8, :] = jnp.zeros((6, tq), F32)

    once = pl.Buffered(1)
    return pl.pallas_call(
        body, name="fox_fwd", grid=(FOX_H // 2, S // tq),
        in_specs=[pl.BlockSpec(memory_space=pltpu.SMEM), pl.BlockSpec((tq, 256), lambda p, i: (i, p)),
                  pl.BlockSpec((S, 256), lambda p, i: (0, p), pipeline_mode=once),
                  pl.BlockSpec((128, S), lambda p, i: (p, 0), pipeline_mode=once)],
        out_specs=[pl.BlockSpec((tq, 128), lambda p, i: (i, p)), pl.BlockSpec((1, 8, tq), lambda p, i: (p, 0, i))],
        out_shape=[jax.ShapeDtypeStruct((S, FOX_W), F32), jax.ShapeDtypeStruct((FOX_H // 2, 8, S), F32)],
        compiler_params=_params("arbitrary", "arbitrary"),
    )(first, qa, ka, vt)


def _fox_delta(d_o, o):
    S = o.shape[0]
    tm = _tile(S, 512)

    def body(d_ref, o_ref, db_ref, dbt_ref, dl_ref):
        d = d_ref[...]
        db_ref[...] = d.astype(BF16)
        dbt_ref[...] = d.T.astype(BF16)
        prod = d * o_ref[...]
        rr = lax.broadcasted_iota(jnp.int32, (8, 128), 0)
        cc = lax.broadcasted_iota(jnp.int32, (8, 128), 1)
        ind = jnp.where(jnp.right_shift(cc, 6) == rr, 1.0, 0.0).astype(F32)
        for p in range(FOX_H // 2):
            dl_ref[p] = lax.dot_general(ind, prod[:, 128 * p:128 * (p + 1)], (((1,), (1,)), ((), ())),
                                        preferred_element_type=F32, precision=HIGHEST)

    row = pl.BlockSpec((tm, FOX_W), lambda i: (i, 0))
    return pl.pallas_call(
        body, name="fox_delta", grid=(S // tm,),
        in_specs=[row, row],
        out_specs=[row, pl.BlockSpec((FOX_W, tm), lambda i: (0, i)), pl.BlockSpec((FOX_H // 2, 8, tm), lambda i: (0, 0, i))],
        out_shape=[jax.ShapeDtypeStruct((S, FOX_W), BF16), jax.ShapeDtypeStruct((FOX_W, S), BF16),
                   jax.ShapeDtypeStruct((FOX_H // 2, 8, S), F32)],
        compiler_params=_params("arbitrary"),
    )(d_o, o)


def _fox_bwd(qa, qat, ka, kat, pa, dob, dobt, lse, delta, narrow_end, wide_end):
    S = qa.shape[0]
    tk = _tile(S, 512)
    wide = _tile(S, FOX_TQ)
    ratio = wide // tk
    n_wide = S // wide

    def body(ne_ref, we_ref, q_ref, qt_ref, k_ref, kt_ref, v_ref, do_ref, dot_ref, lse_ref, dl_ref, dq_ref, dk_ref, dv_ref):
        h, jb = pl.program_id(0), pl.program_id(1)
        hh = h % 2

        @pl.when(jb == 0)
        def _():
            dq_ref[...] = jnp.zeros_like(dq_ref)

        lane = lax.broadcasted_iota(jnp.int32, (tk, 128), 1)
        vm = jnp.where(jnp.right_shift(lane, 6) == hh, v_ref[...], jnp.zeros((), BF16))
        kb, ktb = k_ref[...], kt_ref[0:FOX_LIVE, :]
        mine = pl.ds(pl.multiple_of(hh * FOX_DH, FOX_DH), FOX_DH)

        def blk(ib, tq, carry, masked):
            dk, dv = carry
            qs = pl.ds(pl.multiple_of(ib * tq, tq), tq)
            p = jnp.exp(_nt(kb, q_ref[qs, :]) - lse_ref[0, pl.ds(hh, 1), qs])
            if masked:
                live = lax.broadcasted_iota(jnp.int32, p.shape, 1) >= lax.broadcasted_iota(jnp.int32, p.shape, 0)
                p = jnp.where(live, p, 0.0)
            ds = (p * (_nt(vm, do_ref[qs, :]) - dl_ref[0, pl.ds(hh, 1), qs])).astype(BF16)
            dq_ref[0:FOX_LIVE, qs] += _nn(ktb, ds)
            return dk + _nt(qt_ref[0:FOX_LIVE, qs], ds), dv + _nt(dot_ref[mine, qs], p.astype(BF16))

        carry = blk(jb, tk, (jnp.zeros((FOX_LIVE, tk), F32), jnp.zeros((FOX_DH, tk), F32)), True)
        first_wide = jb // ratio + 1
        carry = lax.fori_loop(jb + 1, ne_ref[h, jb], lambda ib, c: blk(ib, tk, c, False), carry)
        last_wide = we_ref[h, jb]
        rest = jnp.maximum(last_wide - first_wide, 0)
        carry = lax.fori_loop(0, rest // 2, lambda t, c: blk(first_wide + 2 * t + 1, wide, blk(first_wide + 2 * t, wide, c, False),
                                                             False), carry)
        dk, dv = lax.cond(rest % 2 == 1, lambda c: blk(last_wide - 1, wide, c, False), lambda c: c, carry)
        dk_ref[0:FOX_LIVE, :] = dk
        dk_ref[FOX_LIVE:, :] = jnp.zeros((128 - FOX_LIVE, tk), F32)
        dv_ref[...] = dv

    once = pl.Buffered(1)
    rows = pl.BlockSpec((1, 8, S), lambda h, j: (h // 2, 0, 0))
    return pl.pallas_call(
        body, name="fox_bwd", grid=(FOX_H, S // tk),
        in_specs=[pl.BlockSpec(memory_space=pltpu.SMEM), pl.BlockSpec(memory_space=pltpu.SMEM),
                  pl.BlockSpec((S, 128), lambda h, j: (0, h), pipeline_mode=once),
                  pl.BlockSpec((128, S), lambda h, j: (h, 0), pipeline_mode=once),
                  pl.BlockSpec((tk, 128), lambda h, j: (j, h)), pl.BlockSpec((128, tk), lambda h, j: (h, j)),
                  pl.BlockSpec((tk, 128), lambda h, j: (j, A_FV // 128 + h // 2)),
                  pl.BlockSpec((S, 128), lambda h, j: (0, h // 2), pipeline_mode=once),
                  pl.BlockSpec((128, S), lambda h, j: (h // 2, 0), pipeline_mode=once), rows, rows],
        out_specs=[pl.BlockSpec((128, S), lambda h, j: (h, 0), pipeline_mode=once),
                   pl.BlockSpec((128, tk), lambda h, j: (h, j)), pl.BlockSpec((FOX_DH, tk), lambda h, j: (h, j))],
        out_shape=[jax.ShapeDtypeStruct((1024, S), F32), jax.ShapeDtypeStruct((1024, S), F32),
                   jax.ShapeDtypeStruct((FOX_W, S), F32)],
        compiler_params=_params("arbitrary", "arbitrary"),
    )(narrow_end, wide_end, qa, qat, ka, kat, pa, dob, dobt, lse, delta)


def _fox_post(dq, dk, dv, ps, bfg, d_proj):
    S = dq.shape[1]
    tm = _tile(S, 512)
    n_b = S // tm

    def body(dq_ref, dk_ref, dv_ref, ps_ref, b_ref, _, dp_ref, dff_ref, dbf_ref, carry):
        first = pl.program_id(0) == 0

        @pl.when(first)
        def _():
            carry[...] = jnp.zeros_like(carry)

        low = lax.broadcasted_iota(jnp.int32, (tm, 128), 1) < FOX_DH
        for h in range(FOX_H):
            blk = slice(128 * h, 128 * (h + 1))
            dp_ref[:, blk] = jnp.where(low, dq_ref[blk, :].T * FOX_SCALE, 0.0).astype(BF16)
            dp_ref[:, 1024 + 128 * h:1024 + 128 * (h + 1)] = jnp.where(low, dk_ref[blk, :].T, 0.0).astype(BF16)
        dp_ref[:, 2048:P_FOX_W] = dv_ref[...].T.astype(BF16)
        rr = lax.broadcasted_iota(jnp.int32, (FOX_H, 1024), 0)
        cc = lax.broadcasted_iota(jnp.int32, (FOX_H, 1024), 1)
        sel_k = jnp.where(cc == 128 * rr + AUG + 3, 1.0, 0.0).astype(F32)
        sel_q = jnp.where(cc == 128 * rr + AUG, 1.0, 0.0).astype(F32)
        g = (jnp.dot(sel_k, dk_ref[...], preferred_element_type=F32, precision=HIGHEST)
             - jnp.dot(sel_q, dq_ref[...], preferred_element_type=F32, precision=HIGHEST))
        t_from = lax.broadcasted_iota(jnp.int32, (tm, tm), 0)
        t_to = lax.broadcasted_iota(jnp.int32, (tm, tm), 1)
        later = jnp.where(t_from >= t_to, 1.0, 0.0).astype(F32)
        dlf = jnp.dot(-g, later, preferred_element_type=F32, precision=HIGHEST) + carry[:, 0:1]
        carry[...] = jnp.broadcast_to(dlf[:, 0:1], carry.shape)
        cols = jnp.concatenate([jnp.zeros((FF_LANE, tm), F32), dlf, jnp.zeros((128 - FF_LANE - FOX_H, tm), F32)], axis=0).T
        dff = cols * jax.nn.sigmoid(-(ps_ref[...] + b_ref[...]))
        dff_ref[...] = dff
        part = _sum8(dff)

        @pl.when(first)
        def _():
            dbf_ref[...] = part

        @pl.when(jnp.logical_not(first))
        def _():
            dbf_ref[...] += part

    rev = lambda i: (n_b - 1 - i, 0)
    tall = lambda n: pl.BlockSpec((n, tm), lambda i: (0, n_b - 1 - i))
    return pl.pallas_call(
        body, name="fox_post", grid=(n_b,),
        in_specs=[tall(1024), tall(1024), tall(FOX_W), pl.BlockSpec((tm, 128), lambda i: (n_b - 1 - i, 4)),
                  pl.BlockSpec((1, 128), lambda i: (0, 0)), pl.BlockSpec(memory_space=pl.ANY)],
        out_specs=[pl.BlockSpec((tm, P_FOX_W), lambda i: (n_b - 1 - i, P_FOX // P_FOX_W)), pl.BlockSpec((tm, 128), rev),
                   pl.BlockSpec((8, 128), lambda i: (0, 0))],
        out_shape=[jax.ShapeDtypeStruct((S, P_W), BF16), jax.ShapeDtypeStruct((S, 128), F32),
                   jax.ShapeDtypeStruct((8, 128), F32)],
        input_output_aliases={5: 0},
        scratch_shapes=[pltpu.VMEM((8, 128), F32)],
        compiler_params=_params("arbitrary"),
    )(dq, dk, dv, ps, bfg, d_proj)


def _mem_prep(mem, g_mem, wkv):
    def body(m_ref, g_ref, w_ref, mn_ref, kv_ref):
        r, xh = _rms(m_ref[...])
        mn = (xh * g_ref[...]).astype(BF16)
        mn_ref[...] = mn
        kv_ref[...] = _nn(mn, w_ref[...]).astype(BF16)

    return pl.pallas_call(
        body, name="mem_prep",
        out_shape=[jax.ShapeDtypeStruct((N_MEM, D), BF16), jax.ShapeDtypeStruct((N_MEM, 2 * MEM_W), BF16)],
        compiler_params=pltpu.CompilerParams(vmem_limit_bytes=V7X_VMEM_LIMIT),
    )(mem, g_mem, wkv)


def _mem_softmax(qh, kh):
    s = _nt(qh, kh) * MEM_SCALE
    e = jnp.exp(s - jnp.max(s, axis=-1, keepdims=True))
    return e / jnp.sum(e, axis=-1, keepdims=True)


def _mem_fwd(pa, mkv):
    S = pa.shape[0]
    tm = _tile(S, 512)

    def body(q_ref, kv_ref, o_ref):
        for h in range(MEM_H):
            cols = slice(MEM_DH * h, MEM_DH * (h + 1))
            p = _mem_softmax(q_ref[:, cols], kv_ref[:, cols])
            o_ref[:, cols] = _nn(p.astype(BF16), kv_ref[:, MEM_W + MEM_DH * h:MEM_W + MEM_DH * (h + 1)])

    return pl.pallas_call(
        body, name="mem_fwd", grid=(S // tm,),
        in_specs=[pl.BlockSpec((tm, MEM_W), lambda i: (i, A_MQ // MEM_W)), pl.BlockSpec((N_MEM, 2 * MEM_W), lambda i: (0, 0))],
        out_specs=pl.BlockSpec((tm, MEM_W), lambda i: (i, 0)),
        out_shape=jax.ShapeDtypeStruct((S, MEM_W), F32),
        compiler_params=_params("arbitrary"),
    )(pa, mkv)


def _mem_bwd(pa, mkv, d_o, d_proj):
    S = pa.shape[0]
    tm = _tile(S, 512)

    def body(q_ref, kv_ref, do_ref, _, dq_ref, dkv_ref):
        first = pl.program_id(0) == 0
        parts = []
        for h in range(MEM_H):
            cols = slice(MEM_DH * h, MEM_DH * (h + 1))
            vcols = slice(MEM_W + MEM_DH * h, MEM_W + MEM_DH * (h + 1))
            qh, kh = q_ref[:, cols], kv_ref[:, cols]
            p = _mem_softmax(qh, kh)
            dob = do_ref[:, cols].astype(BF16)
            dp = _nt(dob, kv_ref[:, vcols])
            ds = (p * (dp - jnp.sum(p * dp, axis=-1, keepdims=True)) * MEM_SCALE).astype(BF16)
            dq_ref[:, cols] = _nn(ds, kh).astype(BF16)
            parts.append((cols, _tn(ds, qh)))
            parts.append((vcols, _tn(p.astype(BF16), dob)))

        @pl.when(first)
        def _():
            for sl, v in parts:
                dkv_ref[:, sl] = v

        @pl.when(jnp.logical_not(first))
        def _():
            for sl, v in parts:
                dkv_ref[:, sl] += v

    return pl.pallas_call(
        body, name="mem_bwd", grid=(S // tm,),
        in_specs=[pl.BlockSpec((tm, MEM_W), lambda i: (i, A_MQ // MEM_W)), pl.BlockSpec((N_MEM, 2 * MEM_W), lambda i: (0, 0)),
                  pl.BlockSpec((tm, MEM_W), lambda i: (i, 0)), pl.BlockSpec(memory_space=pl.ANY)],
        out_specs=[pl.BlockSpec((tm, MEM_W), lambda i: (i, P_MQ // MEM_W)), pl.BlockSpec((N_MEM, 2 * MEM_W), lambda i: (0, 0))],
        out_shape=[jax.ShapeDtypeStruct((S, P_W), BF16), jax.ShapeDtypeStruct((N_MEM, 2 * MEM_W), F32)],
        input_output_aliases={3: 0},
        compiler_params=_params("arbitrary"),
    )(pa, mkv, d_o, d_proj)


def _mem_prep_bwd(mem, g_mem, mn, wkv, dkv):
    def body(m_ref, g_ref, mn_ref, w_ref, d_ref, dw_ref, dg_ref):
        db = d_ref[...].astype(BF16)
        dw_ref[...] = _tn(mn_ref[...], db).astype(BF16)
        r, xh = _rms(m_ref[...])
        dg_ref[...] = _sum8(_nt(db, w_ref[...]) * xh)

    dw, dg = pl.pallas_call(
        body, name="mem_prep_bwd",
        out_shape=[jax.ShapeDtypeStruct((D, 2 * MEM_W), BF16), jax.ShapeDtypeStruct((8, D), F32)],
        compiler_params=pltpu.CompilerParams(vmem_limit_bytes=V7X_VMEM_LIMIT),
    )(mem, g_mem, mn, wkv, dkv)
    return dw.reshape(N_DEV, D // N_DEV, 2 * MEM_W), dg


def _rearrange_w_in(w):
    def heads128(cols):
        blk = w[:, cols:cols + FOX_W].reshape(D, FOX_H, FOX_DH)
        return jnp.pad(blk, ((0, 0), (0, 0), (0, 128 - FOX_DH))).reshape(D, FOX_H * 128)

    fq, fk, fv, mq, wg = heads128(O_FQ), heads128(O_FK), w[:, O_FV:O_FF], w[:, O_MQ:O_GT], w[:, O_GT:]
    gaff = jnp.concatenate([w[:, O_GA:O_FQ], w[:, O_FF:O_MQ], jnp.zeros((D, 128 - GLA_R - FOX_H), w.dtype)], axis=1)
    wa = jnp.concatenate([w[:, O_GQ:O_GG], fq, fk, fv, mq], axis=1)
    ws = jnp.concatenate([w[:, O_GG:O_GA], gaff], axis=1)
    wp = jnp.concatenate([fq, fk, fv, mq, wg, w[:, O_GQ:O_GG], ws, jnp.zeros((D, P_W - P_GLA - 1024 - S_W), w.dtype)], axis=1)
    return wa, wg, ws, wp


def _restore_w_in_grad(dwp):
    def unheads(off):
        return dwp[:, off:off + FOX_H * 128].reshape(D, FOX_H, 128)[:, :, :FOX_DH].reshape(D, FOX_W)

    g0 = P_GLA + 1024
    return jnp.concatenate([
        dwp[:, P_GLA:g0], dwp[:, g0:g0 + 512], dwp[:, g0 + 512:g0 + 512 + GLA_R], unheads(P_FOX), unheads(P_FOX + 1024),
        dwp[:, P_FOX + 2048:P_FOX + P_FOX_W], dwp[:, g0 + 512 + GLA_R:g0 + 512 + GLA_R + FOX_H], dwp[:, P_MQ:P_GT],
        dwp[:, P_GT:P_GLA]], axis=1)


def _local_step(x, mem, target, p, late_shards):
    S = x.shape[0]
    p = dict(p)
    wa, wg, ws, wp = _rearrange_w_in(p["w_in"])
    wau = jnp.pad(p["w_alpha_up"], ((0, 128 - GLA_R), (0, 0)))
    bfg = jnp.pad(p["b_forget"], ((0, 0), (FF_LANE, 128 - FF_LANE - FOX_H)))
    gh = p["g_gla_head"].reshape(1, GLA_V)

    pa, pg, ps, u = _proj(x, p["g_mix"], wa, wg, ws)
    o_gla, og, sprev, gathered = _gla_fwd(pa, ps, wau, p["b_alpha"], gh, late_shards)
    p.update({n: _unslab(t, ax) for (n, ax), t in zip(BIG[1:], gathered)})
    qa, ka, qat, kat, vt, fox_stats = _fox_prep(pa, ps, bfg)
    fox_tq = _tile(S, FOX_TQ)
    fox_first, fox_narrow_end, fox_wide_end = _fox_live_ranges(fox_stats, fox_tq // _tile(fox_tq, FOX_TK),
                                                               fox_tq // _tile(S, FOX_TK))
    o_fox, lse = _fox_fwd(qa, ka, vt, fox_first)
    mn, mkv = _mem_prep(mem, p["g_mem"], p["w_mem_kv"])
    o_mem = _mem_fwd(pa, mkv)
    y3, mg = _merge(og, o_fox, o_mem, p["w_gla_o"], p["w_fox_o"], p["w_mem_o"], pg)
    h1, u2 = _out_proj(mg, p["w_out"], x, p["g_ffn"])
    a, act = _ff1(u2, p["w_ff1"])
    dh2, dh2b, loss8, dg_final = _ff2_loss(act, p["w_ff2"], h1, p["g_final"].reshape(1, D), target)

    d_a = _dact(dh2b, p["w_ff2"], a)
    dw_ff2 = _wgrad(act, dh2b, "wgrad_ff2", 0)
    dh1, dh1b, dg_ffn = _nt_rmsbwd(d_a, p["w_ff1"], h1, p["g_ffn"], dh2, "dffn", True)
    dw_ff1 = _wgrad(u2, d_a, "wgrad_ff1", 1)
    dy_g, dy_f, dy_m, do_g, do_f, do_m, d_proj = _dmerge(dh1b, p["w_out"], pg, y3, p["w_gla_o"], p["w_fox_o"], p["w_mem_o"])
    dw_out = _wgrad(mg, dh1b, "wgrad_out", 0)
    dw_gla_o = _wgrad(og, dy_g, "wgrad_gla_o", 1)
    dw_fox_o = _wgrad(o_fox, dy_f, "wgrad_fox_o", 1)
    dw_mem_o = _wgrad(o_mem, dy_m, "wgrad_mem_o", 1)
    d_proj, d_mkv = _mem_bwd(pa, mkv, do_m, d_proj)
    dw_mem_kv, dg_mem = _mem_prep_bwd(mem, p["g_mem"], mn, p["w_mem_kv"], d_mkv)
    dob, dobt, delta = _fox_delta(do_f, o_fox)
    dq, dk, dv = _fox_bwd(qa, qat, ka, kat, pa, dob, dobt, lse, delta, fox_narrow_end, fox_wide_end)
    d_proj, dgaff_fox, db_forget = _fox_post(dq, dk, dv, ps, bfg, d_proj)
    ready = dict(w_mem_kv=dw_mem_kv, w_gla_o=dw_gla_o, w_fox_o=dw_fox_o, w_mem_o=dw_mem_o, w_out=dw_out, w_ff1=dw_ff1,
                 w_ff2=dw_ff2)
    d_proj, dw_au, db_alpha, dg_gla, arrived = _gla_bwd(pa, ps, wau, p["b_alpha"], gh, o_gla, do_g, sprev, dgaff_fox, d_proj,
                                                        [ready[n] for n, _ in BIG[1:]])
    dw_in = _slabs(_restore_w_in_grad(_wgrad(u, d_proj, "wgrad_in")), 1).astype(BF16)
    dx, dg_mix, arrived_in = _nt_rmsbwd(d_proj, wp, x, p["g_mix"], dh1, "dmix", False, [dw_in])

    big = dict(zip([n for n, _ in BIG], [arrived_in[0], *arrived]))
    small = dict(g_mix=dg_mix, g_mem=dg_mem, g_ffn=dg_ffn, g_final=dg_final, b_alpha=db_alpha, g_gla_head=dg_gla,
                 b_forget=db_forget, w_alpha_up=dw_au, loss=loss8)
    return dx, big, small


BIG = (("w_in", 1), ("w_mem_kv", 0), ("w_gla_o", 1), ("w_fox_o", 1), ("w_mem_o", 1), ("w_out", 0), ("w_ff1", 1), ("w_ff2", 0))


def _peer(d):
    me = lax.axis_index("x") * 4 + lax.axis_index("y") * 2 + lax.axis_index("c")
    t = (me + d) % N_DEV
    return (t // 4, (t // 2) % 2, t % 2), me


def _exchange_sems(n):
    return [pltpu.SemaphoreType.DMA((n, N_DEV - 1)), pltpu.SemaphoreType.DMA((n, N_DEV - 1)), pltpu.SemaphoreType.DMA((n,))]


def _exchange_call(body, blocks, out_shape, name):
    n = len(blocks)
    any_spec = pl.BlockSpec(memory_space=pl.ANY)
    return pl.pallas_call(body, name=name, in_specs=[any_spec] * n, out_specs=[any_spec] * n, out_shape=out_shape,
                          scratch_shapes=_exchange_sems(n))(*blocks)


class _AllToAll:
    def __init__(self, ins, outs, sems, gather):
        send, recv, loc = sems
        n = len(ins)
        _, me = _peer(0)
        src = (lambda k, j: ins[k]) if gather else (lambda k, j: ins[k].at[j])
        self.local = [pltpu.make_async_copy(src(k, me), outs[k].at[me], loc.at[k]) for k in range(n)]
        self.remote = []
        for d in range(1, N_DEV):
            to, _ = _peer(d)
            self.remote += [pltpu.make_async_remote_copy(
                src_ref=src(k, (me + d) % N_DEV), dst_ref=outs[k].at[me], send_sem=send.at[k, d - 1],
                recv_sem=recv.at[k, d - 1], device_id=to, device_id_type=MESH) for k in range(n)]

    def start(self):
        for cp in self.local + self.remote:
            cp.start()

    def wait(self):
        for cp in self.remote:
            cp.wait_send()
        for cp in self.remote:
            cp.wait_recv()
        for cp in self.local:
            cp.wait()


def _gathered_shapes(shards):
    return [jax.ShapeDtypeStruct((N_DEV,) + b.shape, b.dtype) for b in shards]


def _gather_weights(shards):
    n = len(shards)

    def body(*refs):
        ins, outs = refs[:n], refs[n:2 * n]
        send, recv, loc = refs[2 * n:]
        x, y, c = lax.axis_index("x"), lax.axis_index("y"), lax.axis_index("c")
        sibling = (x, y, 1 - c)
        chips = [(1 - x, y), (x, 1 - y), (1 - x, 1 - y)]
        slot = lambda px, py, pc: px * 4 + py * 2 + pc

        def copy(k, s, block, to, src=None):
            rows = outs[k].at[slot(*block)]
            return pltpu.make_async_remote_copy(src_ref=rows if src is None else src, dst_ref=rows, send_sem=send.at[k, s],
                                                recv_sem=recv.at[k, s], device_id=to, device_id_type=MESH)

        me = (x, y, c)
        own = [pltpu.make_async_copy(ins[k], outs[k].at[slot(*me)], loc.at[k]) for k in range(n)]
        first = [copy(k, 0, me, sibling, src=ins[k]) for k in range(n)]
        first += [copy(k, 1 + j, me, (*chip, c), src=ins[k]) for j, chip in enumerate(chips) for k in range(n)]
        for cp in own + first:
            cp.start()
        passed = []
        for j, chip in enumerate(chips):
            for k in range(n):
                copy(k, 1 + j, (*chip, c), me).wait_recv()
                fwd = copy(k, 4 + j, (*chip, c), sibling)
                fwd.start()
                passed.append(fwd)
        for k in range(n):
            copy(k, 0, sibling, me).wait_recv()
        for j, chip in enumerate(chips):
            for k in range(n):
                copy(k, 4 + j, (*chip, 1 - c), me).wait_recv()
        for cp in first + passed:
            cp.wait_send()
        for cp in own:
            cp.wait()

    return _exchange_call(body, shards, [jax.ShapeDtypeStruct((N_DEV,) + b.shape, b.dtype) for b in shards], "gather_weights")


def _adamw_math(g, w, m, v):
    m2 = ADAM_B1 * m + (1.0 - ADAM_B1) * g
    v2 = ADAM_B2 * v + (1.0 - ADAM_B2) * jnp.square(g)
    m_hat = m2 / (1.0 - ADAM_B1 ** ADAM_STEP)
    v_hat = v2 / (1.0 - ADAM_B2 ** ADAM_STEP)
    delta = -ADAM_LR * (m_hat / (jnp.sqrt(v_hat) + ADAM_EPS) + ADAM_WD * w)
    return delta, m2, v2


def _adamw_sum(parts, w, m, v, name):
    R, C = w.shape
    tr = _tile(R, 128)

    def body(p_ref, w_ref, m_ref, v_ref, g_ref, d_ref, m2_ref, v2_ref):
        g = p_ref[0].astype(F32)
        for j in range(1, p_ref.shape[0]):
            g = g + p_ref[j].astype(F32)
        g_ref[...] = g
        d_ref[...], m2_ref[...], v2_ref[...] = _adamw_math(g, w_ref[...], m_ref[...], v_ref[...])

    blk = pl.BlockSpec((tr, C), lambda i: (i, 0))
    return pl.pallas_call(
        body, name=name, grid=(R // tr,),
        in_specs=[pl.BlockSpec((parts.shape[0], tr, C), lambda i: (0, i, 0)), blk, blk, blk],
        out_specs=[blk] * 4, out_shape=[jax.ShapeDtypeStruct((R, C), F32)] * 4,
        compiler_params=_params("arbitrary"),
    )(parts, w, m, v)


SMALL_ROWS = 24


def _pack_small(d):
    mixed = jnp.concatenate([d["b_alpha"].reshape(1, GLA_K), d["g_gla_head"].reshape(1, GLA_V),
                             jnp.pad(d["b_forget"].reshape(1, FOX_H), ((0, 0), (FF_LANE, 128 - FF_LANE - FOX_H))),
                             jnp.zeros((1, 128), F32)], axis=1)
    rows = [d["g_mix"].reshape(1, D), d["g_mem"].reshape(1, D), d["g_ffn"].reshape(1, D), d["g_final"].reshape(1, D), mixed,
            jnp.zeros((3, D), F32), jnp.pad(d["w_alpha_up"].reshape(GLA_R, GLA_K), ((0, 0), (0, D - GLA_K)))]
    return jnp.concatenate(rows, axis=0)


def _unpack_small(t):
    return dict(g_mix=t[0:1], g_mem=t[1:2], g_ffn=t[2:3], g_final=t[3], b_alpha=t[4:5, 0:GLA_K],
                g_gla_head=t[4:5, GLA_K:GLA_K + GLA_V].reshape(1, GLA_H, GLA_DV),
                b_forget=t[4:5, 768 + FF_LANE:768 + FF_LANE + FOX_H], w_alpha_up=t[8:24, 0:GLA_K].reshape(1, GLA_R, GLA_K))


def _small_allreduce(small, w, m, v):
    def body(gm, gme, gf, gfi, ba, gg, bf, wau, ls, w_ref, m_ref, v_ref, g_ref, d_ref, m2_ref, v2_ref, l_ref,
             buf, send, recv):
        _, me = _peer(0)
        buf[me] = jnp.zeros((SMALL_ROWS, D), F32)
        for r, ref in enumerate((gm, gme, gf, gfi)):
            buf[me, r:r + 1, :] = jnp.sum(ref[...], axis=0, keepdims=True)
        buf[me, 4:5, 0:GLA_K] = jnp.sum(ba[...], axis=0, keepdims=True)
        buf[me, 4:5, GLA_K:GLA_K + GLA_V] = jnp.sum(gg[...], axis=0, keepdims=True)
        buf[me, 4:5, 768:896] = jnp.sum(bf[...], axis=0, keepdims=True)
        lrow = jnp.sum(ls[...], axis=0, keepdims=True)
        lsum = lrow[:, 0:128]
        for c in range(1, D // 128):
            lsum = lsum + lrow[:, 128 * c:128 * (c + 1)]
        buf[me, 4:5, 896:1024] = lsum
        buf[me, 8:24, 0:GLA_K] = wau[0:GLA_R, :]
        remote = []
        for d in range(1, N_DEV):
            to, me = _peer(d)
            cp = pltpu.make_async_remote_copy(src_ref=buf.at[me], dst_ref=buf.at[me], send_sem=send.at[d - 1],
                                              recv_sem=recv.at[d - 1], device_id=to, device_id_type=MESH)
            cp.start()
            remote.append(cp)
        for cp in remote:
            cp.wait_send()
        for cp in remote:
            cp.wait_recv()
        g = buf[0]
        for j in range(1, N_DEV):
            g = g + buf[j]
        g_ref[...] = g
        d_ref[...], m2_ref[...], v2_ref[...] = _adamw_math(g, w_ref[...], m_ref[...], v_ref[...])
        l_ref[...] = g[4:5, 896:1024]

    packed = jax.ShapeDtypeStruct((SMALL_ROWS, D), F32)
    return pl.pallas_call(
        body, name="small_allreduce",
        out_shape=[packed, packed, packed, packed, jax.ShapeDtypeStruct((1, 128), F32)],
        scratch_shapes=[pltpu.VMEM((N_DEV, SMALL_ROWS, D), F32), pltpu.SemaphoreType.DMA((N_DEV - 1,)),
                        pltpu.SemaphoreType.DMA((N_DEV - 1,))],
    )(small["g_mix"], small["g_mem"], small["g_ffn"], small["g_final"], small["b_alpha"], small["g_gla_head"],
      small["b_forget"], small["w_alpha_up"], small["loss"], w, m, v)


def _slabs(g, axis):
    R, C = g.shape
    if axis == 0:
        return g.reshape(N_DEV, R // N_DEV, C)
    return g.reshape(R, N_DEV, C // N_DEV).transpose(1, 0, 2)


def _unslab(t, axis):
    n, r, c = t.shape
    if axis == 0:
        return t.reshape(n * r, c)
    return t.transpose(1, 0, 2).reshape(r, n * c)


def kernel(x, mem, g_mix, w_in, w_alpha_up, b_alpha, b_forget, g_gla_head, g_mem, w_mem_kv, w_gla_o, w_fox_o, w_mem_o, w_out, g_ffn, w_ff1, w_ff2, g_final, loss_target, m_g_mix, m_w_in, m_w_alpha_up, m_b_alpha, m_b_forget, m_g_gla_head, m_g_mem, m_w_mem_kv, m_w_gla_o, m_w_fox_o, m_w_mem_o, m_w_out, m_g_ffn, m_w_ff1, m_w_ff2, m_g_final, v_g_mix, v_w_in, v_w_alpha_up, v_b_alpha, v_b_forget, v_g_gla_head, v_g_mem, v_w_mem_kv, v_w_gla_o, v_w_fox_o, v_w_mem_o, v_w_out, v_g_ffn, v_w_ff1, v_w_ff2, v_g_final):
    names = ["g_mix", "w_in", "w_alpha_up", "b_alpha", "b_forget", "g_gla_head", "g_mem", "w_mem_kv", "w_gla_o", "w_fox_o",
             "w_mem_o", "w_out", "g_ffn", "w_ff1", "w_ff2", "g_final"]
    w = dict(g_mix=g_mix, w_in=w_in, w_alpha_up=w_alpha_up, b_alpha=b_alpha, b_forget=b_forget, g_gla_head=g_gla_head,
             g_mem=g_mem, w_mem_kv=w_mem_kv, w_gla_o=w_gla_o, w_fox_o=w_fox_o, w_mem_o=w_mem_o, w_out=w_out, g_ffn=g_ffn,
             w_ff1=w_ff1, w_ff2=w_ff2, g_final=g_final)
    m = dict(g_mix=m_g_mix, w_in=m_w_in, w_alpha_up=m_w_alpha_up, b_alpha=m_b_alpha, b_forget=m_b_forget,
             g_gla_head=m_g_gla_head, g_mem=m_g_mem, w_mem_kv=m_w_mem_kv, w_gla_o=m_w_gla_o, w_fox_o=m_w_fox_o,
             w_mem_o=m_w_mem_o, w_out=m_w_out, g_ffn=m_g_ffn, w_ff1=m_w_ff1, w_ff2=m_w_ff2, g_final=m_g_final)
    v = dict(g_mix=v_g_mix, w_in=v_w_in, w_alpha_up=v_w_alpha_up, b_alpha=v_b_alpha, b_forget=v_b_forget,
             g_gla_head=v_g_gla_head, g_mem=v_g_mem, w_mem_kv=v_w_mem_kv, w_gla_o=v_w_gla_o, w_fox_o=v_w_fox_o,
             w_mem_o=v_w_mem_o, w_out=v_w_out, g_ffn=v_g_ffn, w_ff1=v_w_ff1, w_ff2=v_w_ff2, g_final=v_g_final)
    me = lax.axis_index("x") * 4 + lax.axis_index("y") * 2 + lax.axis_index("c")

    shard = lambda n: w[n][0].astype(BF16)
    w_in_all, w_au_all = _gather_weights([shard("w_in"), shard("w_alpha_up")])
    p = dict(w_in=_unslab(w_in_all, 1), w_alpha_up=_unslab(w_au_all, 1), g_mix=g_mix, b_alpha=b_alpha, b_forget=b_forget,
             g_gla_head=g_gla_head, g_mem=g_mem, g_ffn=g_ffn, g_final=g_final)

    dx, big, small = _local_step(x[0], mem[0], loss_target[0], p, [shard(n) for n, _ in BIG[1:]])

    out_g, out_d, out_m, out_v = {}, {}, {}, {}
    for n, _ in BIG:
        g_, d_, m_, v_ = _adamw_sum(big[n], w[n][0], m[n][0], v[n][0], "adamw_" + n)
        out_g[n], out_d[n], out_m[n], out_v[n] = g_[None], d_[None], m_[None], v_[None]

    full = lambda d: dict(d, w_alpha_up=jnp.zeros((1, GLA_R, GLA_K), F32))
    gs, ds, ms, vs, lrow = _small_allreduce(small, _pack_small(full(w)), _pack_small(full(m)), _pack_small(full(v)))
    g_s, d_s, m_s, v_s = _unpack_small(gs), _unpack_small(ds), _unpack_small(ms), _unpack_small(vs)
    for n in names:
        if n not in out_g and n != "w_alpha_up":
            out_g[n], out_d[n], out_m[n], out_v[n] = g_s[n], d_s[n], m_s[n], v_s[n]
    g_au = lax.dynamic_slice_in_dim(g_s["w_alpha_up"][0], me * (GLA_K // N_DEV), GLA_K // N_DEV, axis=1)
    g_, d_, m_, v_ = _adamw_sum(g_au[None], w_alpha_up[0], m_w_alpha_up[0], v_w_alpha_up[0], "adamw_w_alpha_up")
    out_g["w_alpha_up"], out_d["w_alpha_up"], out_m["w_alpha_up"], out_v["w_alpha_up"] = g_[None], d_[None], m_[None], v_[None]

    loss = jnp.sum(lrow) * (0.5 / D)
    return (loss, dx[None], *[out_g[n] for n in names], *[out_d[n] for n in names], *[out_m[n] for n in names],
            *[out_v[n] for n in names])
```

```python
import jax
import jax.numpy as jnp
from jax import lax
from jax.experimental import pallas as pl
from jax.experimental.pallas import tpu as pltpu

F32, BF16 = jnp.float32, jnp.bfloat16
HIGHEST = lax.Precision.HIGHEST
MESH = pl.DeviceIdType.MESH

N_DEV = 8
D = 1024
EPS = 1e-6
CHUNK = 64
N_MEM = 256
GLA_H, GLA_DK, GLA_DV = 4, 64, 128
GLA_K, GLA_V, GLA_R = 256, 512, 16
FOX_H, FOX_DH, FOX_W = 8, 64, 512
MEM_H, MEM_DH, MEM_W = 4, 128, 512
D_FF = 4096
D_IN = 6680
FOX_SCALE = 0.125
GLA_SCALE = 0.125
MEM_SCALE = MEM_DH ** -0.5
GLA_TAU_INV = 1.0 / 16.0
NEG = -1e30

O_GQ, O_GK, O_GV, O_GG, O_GA, O_FQ, O_FK, O_FV, O_FF, O_MQ, O_GT = 0, 256, 512, 1024, 1536, 1552, 2064, 2576, 3088, 3096, 3608
A_FQ, A_FK, A_FV, A_MQ, A_W = 1024, 2048, 3072, 3584, 4096
S_W = 640
G_W = 3072
P_FOX, P_FOX_W, P_MQ, P_GT, P_GLA, P_GLA_W, P_W = 0, 2560, 2560, 3072, 6144, 2048, 8192
FF_LANE = 16
AUG = 64
FOX_LIVE = 80

ADAM_LR, ADAM_B1, ADAM_B2, ADAM_EPS, ADAM_WD, ADAM_STEP = 0.001, 0.9, 0.999, 1e-08, 0.01, 10
V7X_VMEM_LIMIT = 48 * 1024 * 1024
FOX_TQ, FOX_TK = 2048, 512


def _params(*sem):
    return pltpu.CompilerParams(dimension_semantics=sem, vmem_limit_bytes=V7X_VMEM_LIMIT)


def _nt(a, b):
    return lax.dot_general(a, b, (((1,), (1,)), ((), ())), preferred_element_type=F32)


def _tn(a, b):
    return lax.dot_general(a, b, (((0,), (0,)), ((), ())), preferred_element_type=F32)


def _nn(a, b):
    return jnp.dot(a, b, preferred_element_type=F32)


def _log_sigmoid(z):
    return jnp.minimum(z, 0.0) - jnp.log(1.0 + jnp.exp(-jnp.abs(z)))


def _sum01(m01, x):
    x1 = x.astype(BF16)
    x2 = (x - x1.astype(F32)).astype(BF16)
    x3 = (x - x1.astype(F32) - x2.astype(F32)).astype(BF16)
    return _nn(m01, x1) + _nn(m01, x2) + _nn(m01, x3)


def _sum8(x):
    return x.reshape(x.shape[0] // 8, 8, x.shape[1]).sum(axis=0)


def _rms(xv):
    r = lax.rsqrt(jnp.mean(xv * xv, axis=-1, keepdims=True) + EPS)
    return r, xv * r


def _rms_bwd(du, g, r, xh):
    w = du * g
    return r * (w - xh * jnp.mean(w * xh, axis=-1, keepdims=True))


def _row_chunks(n, size=256):
    return [slice(r, r + min(size, n)) for r in range(0, n, min(size, n))]


def _tile(n, pref):
    t = min(n, pref)
    assert n % t == 0, (n, t)
    return t


def _proj(x, g, wa, wg, ws, shards):
    S = x.shape[0]
    tm, tn = _tile(S, 1024), 1024
    n_a, n_g = A_W // tn, G_W // tn
    n_i, n_j = S // tm, n_a + n_g + 1
    n_x = len(shards)

    def body(*refs):
        x_ref, g_ref, wa_ref, wg_ref, ws_ref = refs[:5]
        pa_ref, pg_ref, ps_ref, u_ref = refs[5 + n_x:9 + n_x]
        u_s = refs[9 + 2 * n_x]
        gather = lambda: _AllToAll(refs[5:5 + n_x], refs[9 + n_x:9 + 2 * n_x], refs[10 + 2 * n_x:], True)
        i, j = pl.program_id(0), pl.program_id(1)

        @pl.when((i == 0) & (j == 0))
        def _():
            gather().start()

        @pl.when(j == 0)
        def _():
            r, xh = _rms(x_ref[...])
            u_s[...] = (xh * g_ref[...]).astype(BF16)
            u_ref[...] = u_s[...]

        @pl.when(j < n_a)
        def _():
            pa_ref[...] = _nn(u_s[...], wa_ref[...]).astype(BF16)

        @pl.when((j >= n_a) & (j < n_a + n_g))
        def _():
            pg_ref[...] = _nn(u_s[...], wg_ref[...]).astype(BF16)

        @pl.when(j == n_a + n_g)
        def _():
            ps_ref[...] = _nn(u_s[...], ws_ref[...])

        @pl.when((i == n_i - 1) & (j == n_j - 1))
        def _():
            gather().wait()

    in_a = lambda j: jnp.minimum(j, n_a - 1)
    in_g = lambda j: jnp.clip(j - n_a, 0, n_g - 1)
    row = pl.BlockSpec((tm, D), lambda i, j: (i, 0))
    any_spec = pl.BlockSpec(memory_space=pl.ANY)
    out = pl.pallas_call(
        body, name="proj", grid=(n_i, n_j),
        in_specs=[row, pl.BlockSpec((1, D), lambda i, j: (0, 0)), pl.BlockSpec((D, tn), lambda i, j: (0, in_a(j))),
                  pl.BlockSpec((D, tn), lambda i, j: (0, in_g(j))),
                  pl.BlockSpec((D, S_W), lambda i, j: (0, 0), pipeline_mode=pl.Buffered(1))] + [any_spec] * n_x,
        out_specs=[pl.BlockSpec((tm, tn), lambda i, j: (i, in_a(j))), pl.BlockSpec((tm, tn), lambda i, j: (i, in_g(j))),
                   pl.BlockSpec((tm, S_W), lambda i, j: (i, 0)), row] + [any_spec] * n_x,
        out_shape=[jax.ShapeDtypeStruct((S, A_W), BF16), jax.ShapeDtypeStruct((S, G_W), BF16),
                   jax.ShapeDtypeStruct((S, S_W), F32), jax.ShapeDtypeStruct((S, D), BF16)] + _gathered_shapes(shards),
        scratch_shapes=[pltpu.VMEM((tm, D), BF16)] + _exchange_sems(n_x),
        compiler_params=_params("arbitrary", "arbitrary"),
    )(x, g, wa, wg, ws, *shards)
    return out[0], out[1], out[2], out[3], out[4:]


def _wgrad(a, b, name, slab_axis=None):
    S, Ka = a.shape
    N = b.shape[1]
    tka, tn, ts = _tile(Ka, 1024), _tile(N, 1024), _tile(S, 1024)
    n_s = S // ts
    per = N // N_DEV
    slabs_per_step = tn // per

    def body(a_ref, b_ref, o_ref, acc):
        s = pl.program_id(2)

        @pl.when(s == 0)
        def _():
            acc[...] = jnp.zeros_like(acc)

        acc[...] += _tn(a_ref[...].astype(BF16), b_ref[...].astype(BF16))

        @pl.when(s == n_s - 1)
        def _():
            if slab_axis == 1:
                for q in range(slabs_per_step):
                    o_ref[q] = acc[:, per * q:per * (q + 1)].astype(BF16)
            else:
                o_ref[...] = acc[...].astype(o_ref.dtype)

    if slab_axis == 1:
        out_spec = pl.BlockSpec((slabs_per_step, tka, per), lambda i, j, s: (j, i, 0))
        out_shape = jax.ShapeDtypeStruct((N_DEV, Ka, per), BF16)
    else:
        out_spec = pl.BlockSpec((tka, tn), lambda i, j, s: (i, j))
        out_shape = jax.ShapeDtypeStruct((Ka, N), F32 if slab_axis is None else BF16)
    out = pl.pallas_call(
        body, name=name, grid=(Ka // tka, N // tn, n_s),
        in_specs=[pl.BlockSpec((ts, tka), lambda i, j, s: (s, i)), pl.BlockSpec((ts, tn), lambda i, j, s: (s, j))],
        out_specs=out_spec, out_shape=out_shape,
        scratch_shapes=[pltpu.VMEM((tka, tn), F32)],
        compiler_params=_params("arbitrary", "arbitrary", "arbitrary"),
    )(a, b)
    return out.reshape(N_DEV, Ka // N_DEV, N) if slab_axis == 0 else out


def _nt_rmsbwd(a, w, xin, g, dres, name, emit_bf16, slabs=()):
    S, K = a.shape
    tm, tk = _tile(S, 1024), _tile(K, 1024)
    n_i, n_k = S // tm, K // tk
    n_x, n_o = len(slabs), 3 if emit_bf16 else 2

    def body(*refs):
        a_ref, w_ref, x_ref, g_ref, r_ref = refs[:5]
        o_ref = refs[5 + n_x]
        rest = refs[6 + n_x:5 + n_x + n_o] + (refs[5 + 2 * n_x + n_o],)
        dg_ref, acc = rest[-2], rest[-1]
        scatter = lambda: _AllToAll(refs[5:5 + n_x], refs[5 + n_x + n_o:5 + 2 * n_x + n_o], refs[6 + 2 * n_x + n_o:], False)
        i, k = pl.program_id(0), pl.program_id(1)

        if n_x:
            @pl.when((i == 0) & (k == 0))
            def _():
                scatter().start()

        @pl.when(k == 0)
        def _():
            acc[...] = jnp.zeros_like(acc)

        acc[...] += _nt(a_ref[...], w_ref[...])

        @pl.when(k == n_k - 1)
        def _():
            @pl.when(i == 0)
            def _():
                dg_ref[...] = jnp.zeros_like(dg_ref)

            for rows in _row_chunks(tm):
                du = acc[rows, :]
                r, xh = _rms(x_ref[rows, :])
                out = r_ref[rows, :] + _rms_bwd(du, g_ref[...], r, xh)
                o_ref[rows, :] = out
                if emit_bf16:
                    rest[0][rows, :] = out.astype(BF16)
                dg_ref[...] += _sum8(du * xh)

        if n_x:
            @pl.when((i == n_i - 1) & (k == n_k - 1))
            def _():
                scatter().wait()

    row = pl.BlockSpec((tm, D), lambda i, k: (i, 0))
    any_spec = pl.BlockSpec(memory_space=pl.ANY)
    out_shape = [jax.ShapeDtypeStruct((S, D), F32)]
    out_specs = [row]
    if emit_bf16:
        out_shape.append(jax.ShapeDtypeStruct((S, D), BF16))
        out_specs.append(row)
    out_shape.append(jax.ShapeDtypeStruct((8, D), F32))
    out_specs.append(pl.BlockSpec((8, D), lambda i, k: (0, 0)))
    out = pl.pallas_call(
        body, name=name, grid=(n_i, n_k),
        in_specs=[pl.BlockSpec((tm, tk), lambda i, k: (i, k)), pl.BlockSpec((D, tk), lambda i, k: (0, k)),
                  row, pl.BlockSpec((1, D), lambda i, k: (0, 0)), row] + [any_spec] * n_x,
        out_specs=out_specs + [any_spec] * n_x,
        out_shape=out_shape + [jax.ShapeDtypeStruct(b.shape, b.dtype) for b in slabs],
        scratch_shapes=[pltpu.VMEM((tm, D), F32)] + (_exchange_sems(n_x) if n_x else []),
        compiler_params=_params("arbitrary", "arbitrary"),
    )(a, w, xin, g, dres, *slabs)
    return (*out[:n_o], out[n_o:]) if n_x else out


def _merge(og, ofox, omem, wg, wf, wm, pg):
    S = og.shape[0]
    tm = _tile(S, 512)

    def body(og_ref, of_ref, om_ref, wg_ref, wf_ref, wm_ref, pg_ref, y_ref, mg_ref):
        tot = None
        for i, (o_ref, w_ref) in enumerate(((og_ref, wg_ref), (of_ref, wf_ref), (om_ref, wm_ref))):
            y = _nn(o_ref[...].astype(BF16), w_ref[...])
            y_ref[i] = y.astype(BF16)
            t = jax.nn.sigmoid(pg_ref[:, D * i:D * (i + 1)].astype(F32)) * y
            tot = t if tot is None else tot + t
        mg_ref[...] = tot.astype(BF16)

    o_spec = pl.BlockSpec((tm, 512), lambda i: (i, 0))
    w_spec = pl.BlockSpec((512, D), lambda i: (0, 0))
    return pl.pallas_call(
        body, name="merge", grid=(S // tm,),
        in_specs=[o_spec, o_spec, o_spec, w_spec, w_spec, w_spec, pl.BlockSpec((tm, G_W), lambda i: (i, 0))],
        out_specs=[pl.BlockSpec((3, tm, D), lambda i: (0, i, 0)), pl.BlockSpec((tm, D), lambda i: (i, 0))],
        out_shape=[jax.ShapeDtypeStruct((3, S, D), BF16), jax.ShapeDtypeStruct((S, D), BF16)],
        compiler_params=_params("arbitrary"),
    )(og, ofox, omem, wg, wf, wm, pg)


def _out_proj(mg, w_out, x, g_ffn):
    S = x.shape[0]
    tm = _tile(S, 512)

    def body(mg_ref, w_ref, x_ref, g_ref, h_ref, u_ref):
        h = x_ref[...] + _nn(mg_ref[...], w_ref[...])
        h_ref[...] = h
        r, xh = _rms(h)
        u_ref[...] = (xh * g_ref[...]).astype(BF16)

    row = pl.BlockSpec((tm, D), lambda i: (i, 0))
    return pl.pallas_call(
        body, name="out_proj", grid=(S // tm,),
        in_specs=[row, pl.BlockSpec((D, D), lambda i: (0, 0)), row, pl.BlockSpec((1, D), lambda i: (0, 0))],
        out_specs=[row, row],
        out_shape=[jax.ShapeDtypeStruct((S, D), F32), jax.ShapeDtypeStruct((S, D), BF16)],
        compiler_params=_params("arbitrary"),
    )(mg, w_out, x, g_ffn)


def _ff1(u2, w1):
    S = u2.shape[0]
    tm, tn = _tile(S, 1024), 1024

    def body(u_ref, w_ref, a_ref, act_ref):
        a = _nn(u_ref[...], w_ref[...])
        a_ref[...] = a.astype(BF16)
        act_ref[...] = jnp.square(jnp.maximum(a, 0.0)).astype(BF16)

    blk = pl.BlockSpec((tm, tn), lambda i, j: (i, j))
    return pl.pallas_call(
        body, name="ff1", grid=(S // tm, D_FF // tn),
        in_specs=[pl.BlockSpec((tm, D), lambda i, j: (i, 0)), pl.BlockSpec((D, tn), lambda i, j: (0, j))],
        out_specs=[blk, blk],
        out_shape=[jax.ShapeDtypeStruct((S, D_FF), BF16), jax.ShapeDtypeStruct((S, D_FF), BF16)],
        compiler_params=_params("arbitrary", "arbitrary"),
    )(u2, w1)


def _ff2_loss(act, w2, h1, g_final, target):
    S = act.shape[0]
    tm, tk = _tile(S, 1024), 1024
    n_k = D_FF // tk

    def body(a_ref, w_ref, h_ref, g_ref, t_ref, d_ref, db_ref, ls_ref, dg_ref, acc):
        i, k = pl.program_id(0), pl.program_id(1)

        @pl.when(k == 0)
        def _():
            acc[...] = jnp.zeros_like(acc)

        acc[...] += _nn(a_ref[...], w_ref[...])

        @pl.when(k == n_k - 1)
        def _():
            @pl.when(i == 0)
            def _():
                ls_ref[...] = jnp.zeros_like(ls_ref)
                dg_ref[...] = jnp.zeros_like(dg_ref)

            gf = g_ref[...]
            for rows in _row_chunks(tm):
                r, xh = _rms(h_ref[rows, :] + acc[rows, :])
                err = xh * gf - t_ref[rows, :]
                dy = err * (1.0 / D)
                dh = _rms_bwd(dy, gf, r, xh)
                d_ref[rows, :] = dh
                db_ref[rows, :] = dh.astype(BF16)
                ls_ref[...] += _sum8(err * err)
                dg_ref[...] += _sum8(dy * xh)

    row = pl.BlockSpec((tm, D), lambda i, k: (i, 0))
    part = pl.BlockSpec((8, D), lambda i, k: (0, 0))
    return pl.pallas_call(
        body, name="ff2_loss", grid=(S // tm, n_k),
        in_specs=[pl.BlockSpec((tm, tk), lambda i, k: (i, k)), pl.BlockSpec((tk, D), lambda i, k: (k, 0)),
                  row, pl.BlockSpec((1, D), lambda i, k: (0, 0)), row],
        out_specs=[row, row, part, part],
        out_shape=[jax.ShapeDtypeStruct((S, D), F32), jax.ShapeDtypeStruct((S, D), BF16),
                   jax.ShapeDtypeStruct((8, D), F32), jax.ShapeDtypeStruct((8, D), F32)],
        scratch_shapes=[pltpu.VMEM((tm, D), F32)],
        compiler_params=_params("arbitrary", "arbitrary"),
    )(act, w2, h1, g_final, target)


def _dact(dh2b, w2, a):
    S = a.shape[0]
    tm, tn = _tile(S, 1024), 1024

    def body(d_ref, w_ref, a_ref, o_ref):
        da = _nt(d_ref[...], w_ref[...])
        o_ref[...] = (da * (2.0 * jnp.maximum(a_ref[...].astype(F32), 0.0))).astype(BF16)

    blk = pl.BlockSpec((tm, tn), lambda i, j: (i, j))
    return pl.pallas_call(
        body, name="dact", grid=(S // tm, D_FF // tn),
        in_specs=[pl.BlockSpec((tm, D), lambda i, j: (i, 0)), pl.BlockSpec((tn, D), lambda i, j: (j, 0)), blk],
        out_specs=blk, out_shape=jax.ShapeDtypeStruct((S, D_FF), BF16),
        compiler_params=_params("arbitrary", "arbitrary"),
    )(dh2b, w2, a)


def _dmerge(dh1b, w_out, pg, y3, wg, wf, wm):
    S = dh1b.shape[0]
    tm = _tile(S, 512)

    def body(d_ref, w_ref, pg_ref, y_ref, wg_ref, wf_ref, wm_ref, *outs):
        dy_refs, do_refs, dg_ref = outs[0:3], outs[3:6], outs[6]
        dm = _nt(d_ref[...], w_ref[...])
        for i, wo_ref in enumerate((wg_ref, wf_ref, wm_ref)):
            gt = jax.nn.sigmoid(pg_ref[:, D * i:D * (i + 1)].astype(F32))
            dy = (dm * gt).astype(BF16)
            dy_refs[i][...] = dy
            do_refs[i][...] = _nt(dy, wo_ref[...])
            dg_ref[:, D * i:D * (i + 1)] = (dm * y_ref[i].astype(F32) * (gt * (1.0 - gt))).astype(BF16)

    row = pl.BlockSpec((tm, D), lambda i: (i, 0))
    half = pl.BlockSpec((tm, 512), lambda i: (i, 0))
    w_spec = pl.BlockSpec((512, D), lambda i: (0, 0))
    return pl.pallas_call(
        body, name="dmerge", grid=(S // tm,),
        in_specs=[row, pl.BlockSpec((D, D), lambda i: (0, 0)), pl.BlockSpec((tm, G_W), lambda i: (i, 0)),
                  pl.BlockSpec((3, tm, D), lambda i: (0, i, 0)), w_spec, w_spec, w_spec],
        out_specs=[row, row, row, half, half, half, pl.BlockSpec((tm, G_W), lambda i: (i, P_GT // G_W))],
        out_shape=[jax.ShapeDtypeStruct((S, D), BF16)] * 3 + [jax.ShapeDtypeStruct((S, 512), F32)] * 3
        + [jax.ShapeDtypeStruct((S, P_W), BF16)],
        compiler_params=_params("arbitrary"),
    )(dh1b, w_out, pg, y3, wg, wf, wm)


def _gla_block_terms(gq_ref, gk_ref, ps_ref, wau_ref, ba_ref, tb):
    gaff = ps_ref[:, 512:640]
    z = _nn(gaff.astype(BF16), wau_ref[...]) + ba_ref[...]
    la = _log_sigmoid(z) * GLA_TAU_INV
    rr = lax.broadcasted_iota(jnp.int32, (tb, tb), 0)
    cc = lax.broadcasted_iota(jnp.int32, (tb, tb), 1)
    same = jnp.right_shift(rr, 6) == jnp.right_shift(cc, 6)
    tri = jnp.where(same & (cc <= rr), 1.0, 0.0).astype(BF16)
    ones = jnp.where(same, 1.0, 0.0).astype(BF16)
    b = _sum01(tri, la)
    bl = _sum01(ones, la)
    e_pos, e_neg, e_last, dec = jnp.exp(b), jnp.exp(-b), jnp.exp(bl - b), jnp.exp(bl)
    q = gq_ref[...].astype(F32) * GLA_SCALE
    k = gk_ref[...].astype(F32)
    return dict(gaff=gaff, z=z, same=same, rr=rr, cc=cc, ones=ones, e_pos=e_pos, e_neg=e_neg, e_last=e_last, dec=dec,
                qp=q * e_pos, qn=q * e_neg, kn=k * e_neg, kp=k * e_pos, kd=k * e_last)


def _head_masked(x, store):
    lane = lax.broadcasted_iota(jnp.int32, x.shape, 1)
    for h in range(GLA_H):
        store[:, h] = jnp.where(jnp.right_shift(lane, 6) == h, x, 0.0).astype(BF16).reshape(-1, CHUNK, GLA_K)


def _lower4():
    t = jnp.bitwise_and(lax.broadcasted_iota(jnp.int32, (GLA_H * CHUNK, CHUNK), 0), CHUNK - 1)
    return t >= lax.broadcasted_iota(jnp.int32, (GLA_H * CHUNK, CHUNK), 1)


def _stack_heads(ref, rows):
    return jnp.concatenate([ref[rows, GLA_DV * h:GLA_DV * (h + 1)] for h in range(GLA_H)], axis=0)


def _gla_fwd(pa, ps, wau, ba, gh):
    S = pa.shape[0]
    tb = _tile(S, 512)
    n_c = tb // CHUNK
    n_b = S // tb

    def body(gq_ref, gk_ref, gv_ref, ps_ref, wau_ref, ba_ref, gh_ref, o_ref, og_ref, sp_ref,
             qpm, qnm, kdm, kn_s, kp_s, dec_s, state):
        @pl.when(pl.program_id(0) == 0)
        def _():
            state[...] = jnp.zeros_like(state)

        t = _gla_block_terms(gq_ref, gk_ref, ps_ref, wau_ref, ba_ref, tb)
        _head_masked(t["qp"], qpm)
        _head_masked(t["qn"], qnm)
        _head_masked(t["kd"], kdm)
        kn_s[...] = t["kn"].astype(BF16)
        kp_s[...] = t["kp"].astype(BF16)
        dec_s[...] = t["dec"]
        lower = _lower4()

        sp = state[...]
        for c in range(n_c):
            rows = slice(c * CHUNK, (c + 1) * CHUNK)
            sp_ref[c] = sp
            qp, qn, kd = (s[c].reshape(GLA_H * CHUNK, GLA_K) for s in (qpm, qnm, kdm))
            attn = jnp.where(lower, _nt(qp, kn_s[rows, :]), _nt(qn, kp_s[rows, :])).astype(BF16)
            inter = _nt(qp, sp.astype(BF16))
            for h in range(GLA_H):
                mine = slice(CHUNK * h, CHUNK * (h + 1))
                cols = slice(GLA_DV * h, GLA_DV * (h + 1))
                o_ref[rows, cols] = _nn(attn[mine], gv_ref[rows, cols]) + inter[mine]
            sp = sp * dec_s[c * CHUNK:c * CHUNK + 1, :] + _tn(_stack_heads(gv_ref, rows), kd)
        state[...] = sp
        for h in range(GLA_H):
            cols = slice(GLA_DV * h, GLA_DV * (h + 1))
            r, xh = _rms(o_ref[:, cols])
            gg = ps_ref[:, cols]
            og_ref[:, cols] = ((xh * gh_ref[:, cols]) * (gg * jax.nn.sigmoid(gg))).astype(BF16)

    return pl.pallas_call(
        body, name="gla_fwd", grid=(n_b,),
        in_specs=[pl.BlockSpec((tb, GLA_K), lambda i: (i, 0)), pl.BlockSpec((tb, GLA_K), lambda i: (i, 1)),
                  pl.BlockSpec((tb, GLA_V), lambda i: (i, 1)), pl.BlockSpec((tb, S_W), lambda i: (i, 0)),
                  pl.BlockSpec((128, GLA_K), lambda i: (0, 0)), pl.BlockSpec((1, GLA_K), lambda i: (0, 0)),
                  pl.BlockSpec((1, GLA_V), lambda i: (0, 0))],
        out_specs=[pl.BlockSpec((tb, GLA_V), lambda i: (i, 0)), pl.BlockSpec((tb, GLA_V), lambda i: (i, 0)),
                   pl.BlockSpec((n_c, GLA_DV, GLA_K), lambda i: (i, 0, 0))],
        out_shape=[jax.ShapeDtypeStruct((S, GLA_V), F32), jax.ShapeDtypeStruct((S, GLA_V), BF16),
                   jax.ShapeDtypeStruct((S // CHUNK, GLA_DV, GLA_K), F32)],
        scratch_shapes=[pltpu.VMEM((n_c, GLA_H, CHUNK, GLA_K), BF16)] * 3
        + [pltpu.VMEM((tb, GLA_K), BF16), pltpu.VMEM((tb, GLA_K), BF16), pltpu.VMEM((tb, GLA_K), F32),
           pltpu.VMEM((GLA_DV, GLA_K), F32)],
        compiler_params=_params("arbitrary"),
    )(pa, pa, pa, ps, wau, ba, gh)


def _gla_bwd(pa, ps, wau, ba, gh, o_gla, d_og, sprev, dgaff_fox, d_proj, slabs):
    S = pa.shape[0]
    tb = _tile(S, 512)
    n_c = tb // CHUNK
    n_b = S // tb
    n_x = len(slabs)
    c_gk, c_gv, c_gg, c_ga, c_end = GLA_K, 2 * GLA_K, 2 * GLA_K + GLA_V, 2 * GLA_K + 2 * GLA_V, 2 * GLA_K + 2 * GLA_V + 128

    def body(*refs):
        gq_ref, gk_ref, gv_ref, ps_ref, wau_ref, ba_ref, gh_ref, o_ref, dog_ref, sp_ref, dfx_ref = refs[:11]
        dp_ref, dwau_ref, dba_ref, dgh_ref = refs[12 + n_x:16 + n_x]
        (qpm, qnm, kdm, kn_s, kp_s, dec_s, do_s, dqp_s, dqn_s, dkn_s, dkp_s, dkd_s, ddec_s,
         dstate) = refs[16 + 2 * n_x:30 + 2 * n_x]
        scatter = lambda: _AllToAll(refs[12:12 + n_x], refs[16 + n_x:16 + 2 * n_x], refs[30 + 2 * n_x:], False)
        first = pl.program_id(0) == 0
        dp_ref[:, c_end:] = jnp.zeros((tb, P_GLA_W - c_end), BF16)

        @pl.when(first)
        def _():
            dstate[...] = jnp.zeros_like(dstate)
            scatter().start()

        t = _gla_block_terms(gq_ref, gk_ref, ps_ref, wau_ref, ba_ref, tb)
        _head_masked(t["qp"], qpm)
        _head_masked(t["qn"], qnm)
        _head_masked(t["kd"], kdm)
        kn_s[...] = t["kn"].astype(BF16)
        kp_s[...] = t["kp"].astype(BF16)
        dec_s[...] = t["dec"]

        dgh_parts = []
        for h in range(GLA_H):
            cols = slice(GLA_DV * h, GLA_DV * (h + 1))
            r, xh = _rms(o_ref[:, cols])
            g = gh_ref[:, cols]
            gg = ps_ref[:, cols]
            sg = jax.nn.sigmoid(gg)
            d_out = dog_ref[:, cols]
            dp_ref[:, c_gg + GLA_DV * h:c_gg + GLA_DV * (h + 1)] = (d_out * (xh * g) * (sg * (1.0 + gg * (1.0 - sg)))).astype(BF16)
            d_on = d_out * (gg * sg)
            dgh_parts.append(_sum8(d_on * xh))
            do_s[:, cols] = _rms_bwd(d_on, g, r, xh).astype(BF16)
        dgh_part = jnp.concatenate(dgh_parts, axis=1)

        lower = _lower4()
        lane = lax.broadcasted_iota(jnp.int32, (CHUNK, GLA_K), 1)

        def own_columns(stacked):
            return sum(jnp.where(jnp.right_shift(lane, 6) == h, stacked[CHUNK * h:CHUNK * (h + 1)], 0.0) for h in range(GLA_H))

        ds_next = dstate[...]
        for c in reversed(range(n_c)):
            rows = slice(c * CHUNK, (c + 1) * CHUNK)
            dsb = ds_next.astype(BF16)
            sp = sp_ref[c]
            knc, kpc = kn_s[rows, :], kp_s[rows, :]
            qp, qn, kd = (s[c].reshape(GLA_H * CHUNK, GLA_K) for s in (qpm, qnm, kdm))
            v4, do4 = _stack_heads(gv_ref, rows), _stack_heads(do_s, rows)
            ddec_s[rows, :] = jnp.broadcast_to(jnp.sum(ds_next * sp, axis=0, keepdims=True), (CHUNK, GLA_K))
            attn = jnp.where(lower, _nt(qp, knc), _nt(qn, kpc)).astype(BF16)
            da = jnp.concatenate([_nt(do4[CHUNK * h:CHUNK * (h + 1)], v4[CHUNK * h:CHUNK * (h + 1)]) for h in range(GLA_H)],
                                 axis=0)
            dac = jnp.where(lower, da, 0.0).astype(BF16)
            daa = jnp.where(lower, 0.0, da).astype(BF16)
            dqp_s[rows, :] = own_columns(_nn(dac, knc) + _nn(do4, sp.astype(BF16)))
            dqn_s[rows, :] = own_columns(_nn(daa, kpc))
            dkd_s[rows, :] = own_columns(_nn(v4, dsb))
            dkn_s[rows, :] = _tn(dac, qp)
            dkp_s[rows, :] = _tn(daa, qn)
            dv_state = _nt(kd, dsb)
            for h in range(GLA_H):
                mine = slice(CHUNK * h, CHUNK * (h + 1))
                dp_ref[rows, c_gv + GLA_DV * h:c_gv + GLA_DV * (h + 1)] = (_tn(attn[mine], do4[mine]) + dv_state[mine]).astype(BF16)
            ds_next = ds_next * dec_s[c * CHUNK:c * CHUNK + 1, :] + _tn(do4, qp)
        dstate[...] = ds_next

        dqp, dqn, dkn, dkp, dkd = dqp_s[...], dqn_s[...], dkn_s[...], dkp_s[...], dkd_s[...]
        dp_ref[:, 0:c_gk] = ((dqp * t["e_pos"] + dqn * t["e_neg"]) * GLA_SCALE).astype(BF16)
        dp_ref[:, c_gk:c_gv] = (dkn * t["e_neg"] + dkp * t["e_pos"] + dkd * t["e_last"]).astype(BF16)
        kd_term = dkd * t["kd"]
        db = dqp * t["qp"] - dqn * t["qn"] - dkn * t["kn"] + dkp * t["kp"] - kd_term
        upper = jnp.where(t["same"] & (t["cc"] >= t["rr"]), 1.0, 0.0).astype(BF16)
        dla = (_sum01(upper, db) + _sum01(t["ones"], kd_term)
               + ddec_s[...] * t["dec"])
        dz = dla * GLA_TAU_INV * jax.nn.sigmoid(-t["z"])
        dzb = dz.astype(BF16)
        dp_ref[:, c_ga:c_end] = (_nt(dzb, wau_ref[...]) + dfx_ref[...]).astype(BF16)
        dwau_part = _tn(t["gaff"].astype(BF16), dzb)
        dba_part = _sum8(dz)

        @pl.when(first)
        def _():
            dwau_ref[...] = dwau_part
            dba_ref[...] = dba_part
            dgh_ref[...] = dgh_part

        @pl.when(jnp.logical_not(first))
        def _():
            dwau_ref[...] += dwau_part
            dba_ref[...] += dba_part
            dgh_ref[...] += dgh_part

        @pl.when(pl.program_id(0) == n_b - 1)
        def _():
            scatter().wait()

    rev = lambda i: (n_b - 1 - i, 0)
    f32k = pltpu.VMEM((tb, GLA_K), F32)
    bf4 = pltpu.VMEM((n_c, GLA_H, CHUNK, GLA_K), BF16)
    any_spec = pl.BlockSpec(memory_space=pl.ANY)
    out = pl.pallas_call(
        body, name="gla_bwd", grid=(n_b,),
        in_specs=[pl.BlockSpec((tb, GLA_K), rev), pl.BlockSpec((tb, GLA_K), lambda i: (n_b - 1 - i, 1)),
                  pl.BlockSpec((tb, GLA_V), lambda i: (n_b - 1 - i, 1)), pl.BlockSpec((tb, S_W), rev),
                  pl.BlockSpec((128, GLA_K), lambda i: (0, 0)), pl.BlockSpec((1, GLA_K), lambda i: (0, 0)),
                  pl.BlockSpec((1, GLA_V), lambda i: (0, 0)), pl.BlockSpec((tb, GLA_V), rev), pl.BlockSpec((tb, GLA_V), rev),
                  pl.BlockSpec((n_c, GLA_DV, GLA_K), lambda i: (n_b - 1 - i, 0, 0)), pl.BlockSpec((tb, 128), rev),
                  any_spec] + [any_spec] * n_x,
        out_specs=[pl.BlockSpec((tb, P_GLA_W), lambda i: (n_b - 1 - i, P_GLA // P_GLA_W)),
                   pl.BlockSpec((128, GLA_K), lambda i: (0, 0)), pl.BlockSpec((8, GLA_K), lambda i: (0, 0)),
                   pl.BlockSpec((8, GLA_V), lambda i: (0, 0))] + [any_spec] * n_x,
        out_shape=[jax.ShapeDtypeStruct((S, P_W), BF16), jax.ShapeDtypeStruct((128, GLA_K), F32),
                   jax.ShapeDtypeStruct((8, GLA_K), F32), jax.ShapeDtypeStruct((8, GLA_V), F32)]
        + [jax.ShapeDtypeStruct(b.shape, b.dtype) for b in slabs],
        input_output_aliases={11: 0},
        scratch_shapes=[bf4, bf4, bf4, pltpu.VMEM((tb, GLA_K), BF16), pltpu.VMEM((tb, GLA_K), BF16), f32k,
                        pltpu.VMEM((tb, GLA_V), BF16), f32k, f32k, f32k, f32k, f32k, f32k, pltpu.VMEM((GLA_DV, GLA_K), F32)]
        + _exchange_sems(n_x),
        compiler_params=_params("arbitrary"),
    )(pa, pa, pa, ps, wau, ba, gh, o_gla, d_og, sprev, dgaff_fox, d_proj, *slabs)
    return out[0], out[1], out[2], out[3], out[4:]


def _split3(x):
    x1 = x.astype(BF16).astype(F32)
    x2 = (x - x1).astype(BF16).astype(F32)
    x3 = (x - x1 - x2).astype(BF16).astype(F32)
    return x1, x2, x3


def _fox_prep(pa, ps, bfg):
    S = pa.shape[0]
    tm = _tile(S, 512)

    def body(ps_ref, b_ref, fq_ref, fk_ref, fv_ref, q_ref, k_ref, qt_ref, kt_ref, vt_ref, st_ref, carry):
        @pl.when(pl.program_id(0) == 0)
        def _():
            carry[...] = jnp.zeros_like(carry)

        vt_ref[...] = fv_ref[...].astype(F32).T.astype(BF16)
        lf = _log_sigmoid(ps_ref[...] + b_ref[...])
        rr = lax.broadcasted_iota(jnp.int32, (tm, tm), 0)
        cc = lax.broadcasted_iota(jnp.int32, (tm, tm), 1)
        tri = jnp.where(cc <= rr, 1.0, 0.0).astype(F32)
        f = jnp.dot(tri, lf, preferred_element_type=F32, precision=HIGHEST) + carry[0:1, :]
        carry[...] = jnp.broadcast_to(f[tm - 1:tm, :], carry.shape)
        f1, f2, f3 = _split3(f)
        lane = lax.broadcasted_iota(jnp.int32, (tm, 128), 1)
        st_row = lax.broadcasted_iota(jnp.int32, (8, 128), 0)
        st_lane = lax.broadcasted_iota(jnp.int32, (8, 128), 1)
        stats = jnp.zeros((8, 128), F32)
        for h in range(FOX_H):
            cols = slice(128 * h, 128 * (h + 1))
            c = FF_LANE + h
            a1, a2, a3 = f1[:, c:c + 1], f2[:, c:c + 1], f3[:, c:c + 1]
            q = fq_ref[:, cols].astype(F32) * FOX_SCALE
            k = fk_ref[:, cols].astype(F32)
            fh = f[:, c:c + 1]
            vals = (jnp.max(jnp.sum(q * q, axis=-1, keepdims=True)), jnp.max(jnp.sum(k * k, axis=-1, keepdims=True)),
                    jnp.max(fh), jnp.min(fh), jnp.min(jnp.sum(q * k, axis=-1, keepdims=True)))
            for n, val in enumerate(vals):
                stats = jnp.where((st_row == h) & (st_lane == n), val, stats)
            for n, a in enumerate((a1, a2, a3)):
                q = jnp.where(lane == AUG + n, a, q)
                k = jnp.where(lane == AUG + 3 + n, -a, k)
            q = jnp.where((lane >= AUG + 3) & (lane < AUG + 6), 1.0, q)
            k = jnp.where((lane >= AUG) & (lane < AUG + 3), 1.0, k)
            q_ref[:, cols] = q.astype(BF16)
            k_ref[:, cols] = k.astype(BF16)
            qt_ref[cols, :] = q.T.astype(BF16)
            kt_ref[cols, :] = k.T.astype(BF16)
        st_ref[0] = stats

    wide = lambda j: pl.BlockSpec((tm, 1024), lambda i: (i, j))
    tall = lambda n: pl.BlockSpec((n, tm), lambda i: (0, i))
    return pl.pallas_call(
        body, name="fox_prep", grid=(S // tm,),
        in_specs=[pl.BlockSpec((tm, 128), lambda i: (i, 4)), pl.BlockSpec((1, 128), lambda i: (0, 0)), wide(1), wide(2),
                  pl.BlockSpec((tm, FOX_W), lambda i: (i, A_FV // FOX_W))],
        out_specs=[wide(0), wide(0), tall(1024), tall(1024), tall(FOX_W), pl.BlockSpec((1, 8, 128), lambda i: (i, 0, 0))],
        out_shape=[jax.ShapeDtypeStruct((S, 1024), BF16), jax.ShapeDtypeStruct((S, 1024), BF16),
                   jax.ShapeDtypeStruct((1024, S), BF16), jax.ShapeDtypeStruct((1024, S), BF16),
                   jax.ShapeDtypeStruct((FOX_W, S), BF16), jax.ShapeDtypeStruct((S // tm, 8, 128), F32)],
        scratch_shapes=[pltpu.VMEM((8, 128), F32)],
        compiler_params=_params("arbitrary"),
    )(ps, bfg, pa, pa, pa)


FOX_PRUNE_AT = -90.0


def _fox_live_ranges(stats, n_sub, ratio):
    n_b = stats.shape[0]
    q2, k2, f_max, f_min, own = (stats[:, :, n].T for n in range(5))
    slack = 0.01 * jnp.sqrt(q2 * k2) + 1e-5 * jnp.abs(f_max) + 1.0
    bound = (1.01 * jnp.sqrt(q2[:, :, None] * k2[:, None, :]) + (f_max + slack - own)[:, :, None]
             - (f_min - 1e-5 * jnp.abs(f_min))[:, None, :])
    blocks = jnp.arange(n_b)
    dead = (bound <= FOX_PRUNE_AT) & (blocks[None, :] < blocks[:, None])[None]
    dead_fwd = dead.reshape(FOX_H, n_b // n_sub, n_sub, n_b).all(axis=2)
    first = jnp.sum(jnp.cumprod(dead_fwd.astype(jnp.int32), axis=2), axis=2)
    last_live = n_b - 1 - jnp.sum(jnp.cumprod(dead[:, ::-1, :].astype(jnp.int32), axis=1), axis=1)
    first_wide = blocks // ratio + 1
    narrow_end = jnp.minimum(jnp.minimum(first_wide * ratio, n_b)[None], last_live + 1)
    wide_end = jnp.where(last_live >= (first_wide * ratio)[None], last_live // ratio + 1, first_wide[None])
    return first.astype(jnp.int32), narrow_end.astype(jnp.int32), wide_end.astype(jnp.int32)


def _fox_fwd(qa, ka, vt, first):
    S = qa.shape[0]
    tq = _tile(S, FOX_TQ)
    tk = _tile(tq, FOX_TK)
    n_sub = tq // tk

    def body(first_ref, q_ref, k_ref, vt_ref, o_ref, lse_ref):
        pair, i = pl.program_id(0), pl.program_id(1)
        both = lambda f: tuple(f(hh) for hh in range(2))

        def blk(j, carry, diag, heads=(0, 1)):
            ks = pl.ds(pl.multiple_of(j * tk, tk), tk)
            q0 = 0 if diag is None else diag * tk

            def head(hh):
                if hh not in heads:
                    return carry[hh]
                m, l, acc = carry[hh]
                mo, lo, ao = m[:, q0:], l[:, q0:], acc[:, q0:]
                s = _nt(k_ref[ks, 128 * hh:128 * (hh + 1)], q_ref[q0:, 128 * hh:128 * (hh + 1)])
                if diag is not None:
                    live = lax.broadcasted_iota(jnp.int32, s.shape, 1) >= lax.broadcasted_iota(jnp.int32, s.shape, 0)
                    s = jnp.where(live, s, NEG)
                mn = jnp.maximum(mo, jnp.max(s, axis=0, keepdims=True))
                p = jnp.exp(s - mn)
                al = jnp.exp(mo - mn)
                ln = al * lo + jnp.sum(p, axis=0, keepdims=True)
                an = al * ao + _nn(vt_ref[FOX_DH * hh:FOX_DH * (hh + 1), ks], p.astype(BF16))
                if q0:
                    mn, ln, an = (jnp.concatenate([old[:, :q0], new], axis=1) for old, new in ((m, mn), (l, ln), (acc, an)))
                return mn, ln, an

            return both(head)

        one = (jnp.full((1, tq), NEG, F32), jnp.zeros((1, tq), F32), jnp.zeros((FOX_DH, tq), F32))
        past = i * n_sub
        f0, f1 = first_ref[2 * pair, i], first_ref[2 * pair + 1, i]
        join = jnp.maximum(f0, f1)
        solo = lambda hh: lambda c: lax.fori_loop(jnp.minimum(f0, f1), join, lambda j, cc: blk(j, cc, None, (hh,)), c)
        carry = lax.cond(f0 < f1, solo(0), solo(1), (one, one))
        n_both = past - join
        carry = lax.fori_loop(0, n_both // 2, lambda jj, c: blk(join + 2 * jj + 1, blk(join + 2 * jj, c, None), None), carry)
        carry = lax.cond(n_both % 2 == 1, lambda c: blk(past - 1, c, None), lambda c: c, carry)
        for d in range(n_sub):
            carry = blk(past + d, carry, d)
        (m0, l0, a0), (m1, l1, a1) = carry
        o_ref[...] = jnp.concatenate([a0 / l0, a1 / l1], axis=0).T
        lse_ref[0, 0:1, :] = m0 + jnp.log(l0)
        lse_ref[0, 1:2, :] = m1 + jnp.log(l1)
        lse_ref[0, 2:8, :] = jnp.zeros((6, tq), F32)

    once = pl.Buffered(1)
    return pl.pallas_call(
        body, name="fox_fwd", grid=(FOX_H // 2, S // tq),
        in_specs=[pl.BlockSpec(memory_space=pltpu.SMEM), pl.BlockSpec((tq, 256), lambda p, i: (i, p)),
                  pl.BlockSpec((S, 256), lambda p, i: (0, p), pipeline_mode=once),
                  pl.BlockSpec((128, S), lambda p, i: (p, 0), pipeline_mode=once)],
        out_specs=[pl.BlockSpec((tq, 128), lambda p, i: (i, p)), pl.BlockSpec((1, 8, tq), lambda p, i: (p, 0, i))],
        out_shape=[jax.ShapeDtypeStruct((S, FOX_W), F32), jax.ShapeDtypeStruct((FOX_H // 2, 8, S), F32)],
        compiler_params=_params("arbitrary", "arbitrary"),
    )(first, qa, ka, vt)


def _fox_delta(d_o, o):
    S = o.shape[0]
    tm = _tile(S, 512)

    def body(d_ref, o_ref, db_ref, dbt_ref, dl_ref):
        d = d_ref[...]
        db_ref[...] = d.astype(BF16)
        dbt_ref[...] = d.T.astype(BF16)
        prod = d * o_ref[...]
        rr = lax.broadcasted_iota(jnp.int32, (8, 128), 0)
        cc = lax.broadcasted_iota(jnp.int32, (8, 128), 1)
        ind = jnp.where(jnp.right_shift(cc, 6) == rr, 1.0, 0.0).astype(F32)
        for p in range(FOX_H // 2):
            dl_ref[p] = lax.dot_general(ind, prod[:, 128 * p:128 * (p + 1)], (((1,), (1,)), ((), ())),
                                        preferred_element_type=F32, precision=HIGHEST)

    row = pl.BlockSpec((tm, FOX_W), lambda i: (i, 0))
    return pl.pallas_call(
        body, name="fox_delta", grid=(S // tm,),
        in_specs=[row, row],
        out_specs=[row, pl.BlockSpec((FOX_W, tm), lambda i: (0, i)), pl.BlockSpec((FOX_H // 2, 8, tm), lambda i: (0, 0, i))],
        out_shape=[jax.ShapeDtypeStruct((S, FOX_W), BF16), jax.ShapeDtypeStruct((FOX_W, S), BF16),
                   jax.ShapeDtypeStruct((FOX_H // 2, 8, S), F32)],
        compiler_params=_params("arbitrary"),
    )(d_o, o)


def _fox_bwd(qa, qat, ka, kat, pa, dob, dobt, lse, delta, narrow_end, wide_end):
    S = qa.shape[0]
    tk = _tile(S, 512)
    wide = _tile(S, FOX_TQ)
    ratio = wide // tk
    n_wide = S // wide

    def body(ne_ref, we_ref, q_ref, qt_ref, k_ref, kt_ref, v_ref, do_ref, dot_ref, lse_ref, dl_ref, dq_ref, dk_ref, dv_ref):
        h, jb = pl.program_id(0), pl.program_id(1)
        hh = h % 2

        @pl.when(jb == 0)
        def _():
            dq_ref[...] = jnp.zeros_like(dq_ref)

        lane = lax.broadcasted_iota(jnp.int32, (tk, 128), 1)
        vm = jnp.where(jnp.right_shift(lane, 6) == hh, v_ref[...], jnp.zeros((), BF16))
        kb, ktb = k_ref[...], kt_ref[0:FOX_LIVE, :]
        mine = pl.ds(pl.multiple_of(hh * FOX_DH, FOX_DH), FOX_DH)

        def blk(ib, tq, carry, masked):
            dk, dv = carry
            qs = pl.ds(pl.multiple_of(ib * tq, tq), tq)
            p = jnp.exp(_nt(kb, q_ref[qs, :]) - lse_ref[0, pl.ds(hh, 1), qs])
            if masked:
                live = lax.broadcasted_iota(jnp.int32, p.shape, 1) >= lax.broadcasted_iota(jnp.int32, p.shape, 0)
                p = jnp.where(live, p, 0.0)
            ds = (p * (_nt(vm, do_ref[qs, :]) - dl_ref[0, pl.ds(hh, 1), qs])).astype(BF16)
            dq_ref[0:FOX_LIVE, qs] += _nn(ktb, ds)
            return dk + _nt(qt_ref[0:FOX_LIVE, qs], ds), dv + _nt(dot_ref[mine, qs], p.astype(BF16))

        carry = blk(jb, tk, (jnp.zeros((FOX_LIVE, tk), F32), jnp.zeros((FOX_DH, tk), F32)), True)
        first_wide = jb // ratio + 1
        carry = lax.fori_loop(jb + 1, ne_ref[h, jb], lambda ib, c: blk(ib, tk, c, False), carry)
        last_wide = we_ref[h, jb]
        rest = jnp.maximum(last_wide - first_wide, 0)
        carry = lax.fori_loop(0, rest // 2, lambda t, c: blk(first_wide + 2 * t + 1, wide, blk(first_wide + 2 * t, wide, c, False),
                                                             False), carry)
        dk, dv = lax.cond(rest % 2 == 1, lambda c: blk(last_wide - 1, wide, c, False), lambda c: c, carry)
        dk_ref[0:FOX_LIVE, :] = dk
        dk_ref[FOX_LIVE:, :] = jnp.zeros((128 - FOX_LIVE, tk), F32)
        dv_ref[...] = dv

    once = pl.Buffered(1)
    rows = pl.BlockSpec((1, 8, S), lambda h, j: (h // 2, 0, 0))
    return pl.pallas_call(
        body, name="fox_bwd", grid=(FOX_H, S // tk),
        in_specs=[pl.BlockSpec(memory_space=pltpu.SMEM), pl.BlockSpec(memory_space=pltpu.SMEM),
                  pl.BlockSpec((S, 128), lambda h, j: (0, h), pipeline_mode=once),
                  pl.BlockSpec((128, S), lambda h, j: (h, 0), pipeline_mode=once),
                  pl.BlockSpec((tk, 128), lambda h, j: (j, h)), pl.BlockSpec((128, tk), lambda h, j: (h, j)),
                  pl.BlockSpec((tk, 128), lambda h, j: (j, A_FV // 128 + h // 2)),
                  pl.BlockSpec((S, 128), lambda h, j: (0, h // 2), pipeline_mode=once),
                  pl.BlockSpec((128, S), lambda h, j: (h // 2, 0), pipeline_mode=once), rows, rows],
        out_specs=[pl.BlockSpec((128, S), lambda h, j: (h, 0), pipeline_mode=once),
                   pl.BlockSpec((128, tk), lambda h, j: (h, j)), pl.BlockSpec((FOX_DH, tk), lambda h, j: (h, j))],
        out_shape=[jax.ShapeDtypeStruct((1024, S), F32), jax.ShapeDtypeStruct((1024, S), F32),
                   jax.ShapeDtypeStruct((FOX_W, S), F32)],
        compiler_params=_params("arbitrary", "arbitrary"),
    )(narrow_end, wide_end, qa, qat, ka, kat, pa, dob, dobt, lse, delta)


def _fox_post(dq, dk, dv, ps, bfg, d_proj):
    S = dq.shape[1]
    tm = _tile(S, 512)
    n_b = S // tm

    def body(dq_ref, dk_ref, dv_ref, ps_ref, b_ref, _, dp_ref, dff_ref, dbf_ref, carry):
        first = pl.program_id(0) == 0

        @pl.when(first)
        def _():
            carry[...] = jnp.zeros_like(carry)

        low = lax.broadcasted_iota(jnp.int32, (tm, 128), 1) < FOX_DH
        for h in range(FOX_H):
            blk = slice(128 * h, 128 * (h + 1))
            dp_ref[:, blk] = jnp.where(low, dq_ref[blk, :].T * FOX_SCALE, 0.0).astype(BF16)
            dp_ref[:, 1024 + 128 * h:1024 + 128 * (h + 1)] = jnp.where(low, dk_ref[blk, :].T, 0.0).astype(BF16)
        dp_ref[:, 2048:P_FOX_W] = dv_ref[...].T.astype(BF16)
        rr = lax.broadcasted_iota(jnp.int32, (FOX_H, 1024), 0)
        cc = lax.broadcasted_iota(jnp.int32, (FOX_H, 1024), 1)
        sel_k = jnp.where(cc == 128 * rr + AUG + 3, 1.0, 0.0).astype(F32)
        sel_q = jnp.where(cc == 128 * rr + AUG, 1.0, 0.0).astype(F32)
        g = (jnp.dot(sel_k, dk_ref[...], preferred_element_type=F32, precision=HIGHEST)
             - jnp.dot(sel_q, dq_ref[...], preferred_element_type=F32, precision=HIGHEST))
        t_from = lax.broadcasted_iota(jnp.int32, (tm, tm), 0)
        t_to = lax.broadcasted_iota(jnp.int32, (tm, tm), 1)
        later = jnp.where(t_from >= t_to, 1.0, 0.0).astype(F32)
        dlf = jnp.dot(-g, later, preferred_element_type=F32, precision=HIGHEST) + carry[:, 0:1]
        carry[...] = jnp.broadcast_to(dlf[:, 0:1], carry.shape)
        cols = jnp.concatenate([jnp.zeros((FF_LANE, tm), F32), dlf, jnp.zeros((128 - FF_LANE - FOX_H, tm), F32)], axis=0).T
        dff = cols * jax.nn.sigmoid(-(ps_ref[...] + b_ref[...]))
        dff_ref[...] = dff
        part = _sum8(dff)

        @pl.when(first)
        def _():
            dbf_ref[...] = part

        @pl.when(jnp.logical_not(first))
        def _():
            dbf_ref[...] += part

    rev = lambda i: (n_b - 1 - i, 0)
    tall = lambda n: pl.BlockSpec((n, tm), lambda i: (0, n_b - 1 - i))
    return pl.pallas_call(
        body, name="fox_post", grid=(n_b,),
        in_specs=[tall(1024), tall(1024), tall(FOX_W), pl.BlockSpec((tm, 128), lambda i: (n_b - 1 - i, 4)),
                  pl.BlockSpec((1, 128), lambda i: (0, 0)), pl.BlockSpec(memory_space=pl.ANY)],
        out_specs=[pl.BlockSpec((tm, P_FOX_W), lambda i: (n_b - 1 - i, P_FOX // P_FOX_W)), pl.BlockSpec((tm, 128), rev),
                   pl.BlockSpec((8, 128), lambda i: (0, 0))],
        out_shape=[jax.ShapeDtypeStruct((S, P_W), BF16), jax.ShapeDtypeStruct((S, 128), F32),
                   jax.ShapeDtypeStruct((8, 128), F32)],
        input_output_aliases={5: 0},
        scratch_shapes=[pltpu.VMEM((8, 128), F32)],
        compiler_params=_params("arbitrary"),
    )(dq, dk, dv, ps, bfg, d_proj)


def _mem_prep(mem, g_mem, wkv):
    def body(m_ref, g_ref, w_ref, mn_ref, kv_ref):
        r, xh = _rms(m_ref[...])
        mn = (xh * g_ref[...]).astype(BF16)
        mn_ref[...] = mn
        kv_ref[...] = _nn(mn, w_ref[...]).astype(BF16)

    return pl.pallas_call(
        body, name="mem_prep",
        out_shape=[jax.ShapeDtypeStruct((N_MEM, D), BF16), jax.ShapeDtypeStruct((N_MEM, 2 * MEM_W), BF16)],
        compiler_params=pltpu.CompilerParams(vmem_limit_bytes=V7X_VMEM_LIMIT),
    )(mem, g_mem, wkv)


def _mem_softmax(qh, kh):
    s = _nt(qh, kh) * MEM_SCALE
    e = jnp.exp(s - jnp.max(s, axis=-1, keepdims=True))
    return e / jnp.sum(e, axis=-1, keepdims=True)


def _mem_fwd(pa, mkv):
    S = pa.shape[0]
    tm = _tile(S, 512)

    def body(q_ref, kv_ref, o_ref):
        for h in range(MEM_H):
            cols = slice(MEM_DH * h, MEM_DH * (h + 1))
            p = _mem_softmax(q_ref[:, cols], kv_ref[:, cols])
            o_ref[:, cols] = _nn(p.astype(BF16), kv_ref[:, MEM_W + MEM_DH * h:MEM_W + MEM_DH * (h + 1)])

    return pl.pallas_call(
        body, name="mem_fwd", grid=(S // tm,),
        in_specs=[pl.BlockSpec((tm, MEM_W), lambda i: (i, A_MQ // MEM_W)), pl.BlockSpec((N_MEM, 2 * MEM_W), lambda i: (0, 0))],
        out_specs=pl.BlockSpec((tm, MEM_W), lambda i: (i, 0)),
        out_shape=jax.ShapeDtypeStruct((S, MEM_W), F32),
        compiler_params=_params("arbitrary"),
    )(pa, mkv)


def _mem_bwd(pa, mkv, d_o, d_proj):
    S = pa.shape[0]
    tm = _tile(S, 512)

    def body(q_ref, kv_ref, do_ref, _, dq_ref, dkv_ref):
        first = pl.program_id(0) == 0
        parts = []
        for h in range(MEM_H):
            cols = slice(MEM_DH * h, MEM_DH * (h + 1))
            vcols = slice(MEM_W + MEM_DH * h, MEM_W + MEM_DH * (h + 1))
            qh, kh = q_ref[:, cols], kv_ref[:, cols]
            p = _mem_softmax(qh, kh)
            dob = do_ref[:, cols].astype(BF16)
            dp = _nt(dob, kv_ref[:, vcols])
            ds = (p * (dp - jnp.sum(p * dp, axis=-1, keepdims=True)) * MEM_SCALE).astype(BF16)
            dq_ref[:, cols] = _nn(ds, kh).astype(BF16)
            parts.append((cols, _tn(ds, qh)))
            parts.append((vcols, _tn(p.astype(BF16), dob)))

        @pl.when(first)
        def _():
            for sl, v in parts:
                dkv_ref[:, sl] = v

        @pl.when(jnp.logical_not(first))
        def _():
            for sl, v in parts:
                dkv_ref[:, sl] += v

    return pl.pallas_call(
        body, name="mem_bwd", grid=(S // tm,),
        in_specs=[pl.BlockSpec((tm, MEM_W), lambda i: (i, A_MQ // MEM_W)), pl.BlockSpec((N_MEM, 2 * MEM_W), lambda i: (0, 0)),
                  pl.BlockSpec((tm, MEM_W), lambda i: (i, 0)), pl.BlockSpec(memory_space=pl.ANY)],
        out_specs=[pl.BlockSpec((tm, MEM_W), lambda i: (i, P_MQ // MEM_W)), pl.BlockSpec((N_MEM, 2 * MEM_W), lambda i: (0, 0))],
        out_shape=[jax.ShapeDtypeStruct((S, P_W), BF16), jax.ShapeDtypeStruct((N_MEM, 2 * MEM_W), F32)],
        input_output_aliases={3: 0},
        compiler_params=_params("arbitrary"),
    )(pa, mkv, d_o, d_proj)


def _mem_prep_bwd(mem, g_mem, mn, wkv, dkv):
    def body(m_ref, g_ref, mn_ref, w_ref, d_ref, dw_ref, dg_ref):
        db = d_ref[...].astype(BF16)
        dw_ref[...] = _tn(mn_ref[...], db).astype(BF16)
        r, xh = _rms(m_ref[...])
        dg_ref[...] = _sum8(_nt(db, w_ref[...]) * xh)

    dw, dg = pl.pallas_call(
        body, name="mem_prep_bwd",
        out_shape=[jax.ShapeDtypeStruct((D, 2 * MEM_W), BF16), jax.ShapeDtypeStruct((8, D), F32)],
        compiler_params=pltpu.CompilerParams(vmem_limit_bytes=V7X_VMEM_LIMIT),
    )(mem, g_mem, mn, wkv, dkv)
    return dw.reshape(N_DEV, D // N_DEV, 2 * MEM_W), dg


def _rearrange_w_in(w):
    def heads128(cols):
        blk = w[:, cols:cols + FOX_W].reshape(D, FOX_H, FOX_DH)
        return jnp.pad(blk, ((0, 0), (0, 0), (0, 128 - FOX_DH))).reshape(D, FOX_H * 128)

    fq, fk, fv, mq, wg = heads128(O_FQ), heads128(O_FK), w[:, O_FV:O_FF], w[:, O_MQ:O_GT], w[:, O_GT:]
    gaff = jnp.concatenate([w[:, O_GA:O_FQ], w[:, O_FF:O_MQ], jnp.zeros((D, 128 - GLA_R - FOX_H), w.dtype)], axis=1)
    wa = jnp.concatenate([w[:, O_GQ:O_GG], fq, fk, fv, mq], axis=1)
    ws = jnp.concatenate([w[:, O_GG:O_GA], gaff], axis=1)
    wp = jnp.concatenate([fq, fk, fv, mq, wg, w[:, O_GQ:O_GG], ws, jnp.zeros((D, P_W - P_GLA - 1024 - S_W), w.dtype)], axis=1)
    return wa, wg, ws, wp


def _restore_w_in_grad(dwp):
    def unheads(off):
        return dwp[:, off:off + FOX_H * 128].reshape(D, FOX_H, 128)[:, :, :FOX_DH].reshape(D, FOX_W)

    g0 = P_GLA + 1024
    return jnp.concatenate([
        dwp[:, P_GLA:g0], dwp[:, g0:g0 + 512], dwp[:, g0 + 512:g0 + 512 + GLA_R], unheads(P_FOX), unheads(P_FOX + 1024),
        dwp[:, P_FOX + 2048:P_FOX + P_FOX_W], dwp[:, g0 + 512 + GLA_R:g0 + 512 + GLA_R + FOX_H], dwp[:, P_MQ:P_GT],
        dwp[:, P_GT:P_GLA]], axis=1)


def _local_step(x, mem, target, p, late_shards):
    S = x.shape[0]
    p = dict(p)
    wa, wg, ws, wp = _rearrange_w_in(p["w_in"])
    wau = jnp.pad(p["w_alpha_up"], ((0, 128 - GLA_R), (0, 0)))
    bfg = jnp.pad(p["b_forget"], ((0, 0), (FF_LANE, 128 - FF_LANE - FOX_H)))
    gh = p["g_gla_head"].reshape(1, GLA_V)

    pa, pg, ps, u, gathered = _proj(x, p["g_mix"], wa, wg, ws, late_shards)
    p.update({n: _unslab(t, ax) for (n, ax), t in zip(BIG[1:], gathered)})
    o_gla, og, sprev = _gla_fwd(pa, ps, wau, p["b_alpha"], gh)
    qa, ka, qat, kat, vt, fox_stats = _fox_prep(pa, ps, bfg)
    fox_tq = _tile(S, FOX_TQ)
    fox_first, fox_narrow_end, fox_wide_end = _fox_live_ranges(fox_stats, fox_tq // _tile(fox_tq, FOX_TK),
                                                               fox_tq // _tile(S, FOX_TK))
    o_fox, lse = _fox_fwd(qa, ka, vt, fox_first)
    mn, mkv = _mem_prep(mem, p["g_mem"], p["w_mem_kv"])
    o_mem = _mem_fwd(pa, mkv)
    y3, mg = _merge(og, o_fox, o_mem, p["w_gla_o"], p["w_fox_o"], p["w_mem_o"], pg)
    h1, u2 = _out_proj(mg, p["w_out"], x, p["g_ffn"])
    a, act = _ff1(u2, p["w_ff1"])
    dh2, dh2b, loss8, dg_final = _ff2_loss(act, p["w_ff2"], h1, p["g_final"].reshape(1, D), target)

    d_a = _dact(dh2b, p["w_ff2"], a)
    dw_ff2 = _wgrad(act, dh2b, "wgrad_ff2", 0)
    dh1, dh1b, dg_ffn = _nt_rmsbwd(d_a, p["w_ff1"], h1, p["g_ffn"], dh2, "dffn", True)
    dw_ff1 = _wgrad(u2, d_a, "wgrad_ff1", 1)
    dy_g, dy_f, dy_m, do_g, do_f, do_m, d_proj = _dmerge(dh1b, p["w_out"], pg, y3, p["w_gla_o"], p["w_fox_o"], p["w_mem_o"])
    dw_out = _wgrad(mg, dh1b, "wgrad_out", 0)
    dw_gla_o = _wgrad(og, dy_g, "wgrad_gla_o", 1)
    dw_fox_o = _wgrad(o_fox, dy_f, "wgrad_fox_o", 1)
    dw_mem_o = _wgrad(o_mem, dy_m, "wgrad_mem_o", 1)
    d_proj, d_mkv = _mem_bwd(pa, mkv, do_m, d_proj)
    dw_mem_kv, dg_mem = _mem_prep_bwd(mem, p["g_mem"], mn, p["w_mem_kv"], d_mkv)
    dob, dobt, delta = _fox_delta(do_f, o_fox)
    dq, dk, dv = _fox_bwd(qa, qat, ka, kat, pa, dob, dobt, lse, delta, fox_narrow_end, fox_wide_end)
    d_proj, dgaff_fox, db_forget = _fox_post(dq, dk, dv, ps, bfg, d_proj)
    ready = dict(w_mem_kv=dw_mem_kv, w_gla_o=dw_gla_o, w_fox_o=dw_fox_o, w_mem_o=dw_mem_o, w_out=dw_out, w_ff1=dw_ff1,
                 w_ff2=dw_ff2)
    d_proj, dw_au, db_alpha, dg_gla, arrived = _gla_bwd(pa, ps, wau, p["b_alpha"], gh, o_gla, do_g, sprev, dgaff_fox, d_proj,
                                                        [ready[n] for n, _ in BIG[1:]])
    dw_in = _slabs(_restore_w_in_grad(_wgrad(u, d_proj, "wgrad_in")), 1).astype(BF16)
    dx, dg_mix, arrived_in = _nt_rmsbwd(d_proj, wp, x, p["g_mix"], dh1, "dmix", False, [dw_in])

    big = dict(zip([n for n, _ in BIG], [arrived_in[0], *arrived]))
    small = dict(g_mix=dg_mix, g_mem=dg_mem, g_ffn=dg_ffn, g_final=dg_final, b_alpha=db_alpha, g_gla_head=dg_gla,
                 b_forget=db_forget, w_alpha_up=dw_au, loss=loss8)
    return dx, big, small


BIG = (("w_in", 1), ("w_mem_kv", 0), ("w_gla_o", 1), ("w_fox_o", 1), ("w_mem_o", 1), ("w_out", 0), ("w_ff1", 1), ("w_ff2", 0))


def _peer(d):
    me = lax.axis_index("x") * 4 + lax.axis_index("y") * 2 + lax.axis_index("c")
    t = (me + d) % N_DEV
    return (t // 4, (t // 2) % 2, t % 2), me


def _exchange_sems(n):
    return [pltpu.SemaphoreType.DMA((n, N_DEV - 1)), pltpu.SemaphoreType.DMA((n, N_DEV - 1)), pltpu.SemaphoreType.DMA((n,))]


def _exchange_call(body, blocks, out_shape, name):
    n = len(blocks)
    any_spec = pl.BlockSpec(memory_space=pl.ANY)
    return pl.pallas_call(body, name=name, in_specs=[any_spec] * n, out_specs=[any_spec] * n, out_shape=out_shape,
                          scratch_shapes=_exchange_sems(n))(*blocks)


class _AllToAll:
    def __init__(self, ins, outs, sems, gather):
        send, recv, loc = sems
        n = len(ins)
        _, me = _peer(0)
        src = (lambda k, j: ins[k]) if gather else (lambda k, j: ins[k].at[j])
        self.local = [pltpu.make_async_copy(src(k, me), outs[k].at[me], loc.at[k]) for k in range(n)]
        self.remote = []
        for d in range(1, N_DEV):
            to, _ = _peer(d)
            self.remote += [pltpu.make_async_remote_copy(
                src_ref=src(k, (me + d) % N_DEV), dst_ref=outs[k].at[me], send_sem=send.at[k, d - 1],
                recv_sem=recv.at[k, d - 1], device_id=to, device_id_type=MESH) for k in range(n)]

    def start(self):
        for cp in self.local + self.remote:
            cp.start()

    def wait(self):
        for cp in self.remote:
            cp.wait_send()
        for cp in self.remote:
            cp.wait_recv()
        for cp in self.local:
            cp.wait()


def _gathered_shapes(shards):
    return [jax.ShapeDtypeStruct((N_DEV,) + b.shape, b.dtype) for b in shards]


def _gather_weights(shards):
    n = len(shards)

    def body(*refs):
        ins, outs = refs[:n], refs[n:2 * n]
        send, recv, loc = refs[2 * n:]
        x, y, c = lax.axis_index("x"), lax.axis_index("y"), lax.axis_index("c")
        sibling = (x, y, 1 - c)
        chips = [(1 - x, y), (x, 1 - y), (1 - x, 1 - y)]
        slot = lambda px, py, pc: px * 4 + py * 2 + pc

        def copy(k, s, block, to, src=None):
            rows = outs[k].at[slot(*block)]
            return pltpu.make_async_remote_copy(src_ref=rows if src is None else src, dst_ref=rows, send_sem=send.at[k, s],
                                                recv_sem=recv.at[k, s], device_id=to, device_id_type=MESH)

        me = (x, y, c)
        own = [pltpu.make_async_copy(ins[k], outs[k].at[slot(*me)], loc.at[k]) for k in range(n)]
        first = [copy(k, 0, me, sibling, src=ins[k]) for k in range(n)]
        first += [copy(k, 1 + j, me, (*chip, c), src=ins[k]) for j, chip in enumerate(chips) for k in range(n)]
        for cp in own + first:
            cp.start()
        passed = []
        for j, chip in enumerate(chips):
            for k in range(n):
                copy(k, 1 + j, (*chip, c), me).wait_recv()
                fwd = copy(k, 4 + j, (*chip, c), sibling)
                fwd.start()
                passed.append(fwd)
        for k in range(n):
            copy(k, 0, sibling, me).wait_recv()
        for j, chip in enumerate(chips):
            for k in range(n):
                copy(k, 4 + j, (*chip, 1 - c), me).wait_recv()
        for cp in first + passed:
            cp.wait_send()
        for cp in own:
            cp.wait()

    return _exchange_call(body, shards, [jax.ShapeDtypeStruct((N_DEV,) + b.shape, b.dtype) for b in shards], "gather_weights")


def _adamw_math(g, w, m, v):
    m2 = ADAM_B1 * m + (1.0 - ADAM_B1) * g
    v2 = ADAM_B2 * v + (1.0 - ADAM_B2) * jnp.square(g)
    m_hat = m2 / (1.0 - ADAM_B1 ** ADAM_STEP)
    v_hat = v2 / (1.0 - ADAM_B2 ** ADAM_STEP)
    delta = -ADAM_LR * (m_hat / (jnp.sqrt(v_hat) + ADAM_EPS) + ADAM_WD * w)
    return delta, m2, v2


def _adamw_sum(parts, w, m, v, name):
    R, C = w.shape
    tr = _tile(R, 128)

    def body(p_ref, w_ref, m_ref, v_ref, g_ref, d_ref, m2_ref, v2_ref):
        g = p_ref[0].astype(F32)
        for j in range(1, p_ref.shape[0]):
            g = g + p_ref[j].astype(F32)
        g_ref[...] = g
        d_ref[...], m2_ref[...], v2_ref[...] = _adamw_math(g, w_ref[...], m_ref[...], v_ref[...])

    blk = pl.BlockSpec((tr, C), lambda i: (i, 0))
    return pl.pallas_call(
        body, name=name, grid=(R // tr,),
        in_specs=[pl.BlockSpec((parts.shape[0], tr, C), lambda i: (0, i, 0)), blk, blk, blk],
        out_specs=[blk] * 4, out_shape=[jax.ShapeDtypeStruct((R, C), F32)] * 4,
        compiler_params=_params("arbitrary"),
    )(parts, w, m, v)


SMALL_ROWS = 24


def _pack_small(d):
    mixed = jnp.concatenate([d["b_alpha"].reshape(1, GLA_K), d["g_gla_head"].reshape(1, GLA_V),
                             jnp.pad(d["b_forget"].reshape(1, FOX_H), ((0, 0), (FF_LANE, 128 - FF_LANE - FOX_H))),
                             jnp.zeros((1, 128), F32)], axis=1)
    rows = [d["g_mix"].reshape(1, D), d["g_mem"].reshape(1, D), d["g_ffn"].reshape(1, D), d["g_final"].reshape(1, D), mixed,
            jnp.zeros((3, D), F32), jnp.pad(d["w_alpha_up"].reshape(GLA_R, GLA_K), ((0, 0), (0, D - GLA_K)))]
    return jnp.concatenate(rows, axis=0)


def _unpack_small(t):
    return dict(g_mix=t[0:1], g_mem=t[1:2], g_ffn=t[2:3], g_final=t[3], b_alpha=t[4:5, 0:GLA_K],
                g_gla_head=t[4:5, GLA_K:GLA_K + GLA_V].reshape(1, GLA_H, GLA_DV),
                b_forget=t[4:5, 768 + FF_LANE:768 + FF_LANE + FOX_H], w_alpha_up=t[8:24, 0:GLA_K].reshape(1, GLA_R, GLA_K))


def _small_allreduce(small, w, m, v):
    def body(gm, gme, gf, gfi, ba, gg, bf, wau, ls, w_ref, m_ref, v_ref, g_ref, d_ref, m2_ref, v2_ref, l_ref,
             buf, send, recv):
        _, me = _peer(0)
        buf[me] = jnp.zeros((SMALL_ROWS, D), F32)
        for r, ref in enumerate((gm, gme, gf, gfi)):
            buf[me, r:r + 1, :] = jnp.sum(ref[...], axis=0, keepdims=True)
        buf[me, 4:5, 0:GLA_K] = jnp.sum(ba[...], axis=0, keepdims=True)
        buf[me, 4:5, GLA_K:GLA_K + GLA_V] = jnp.sum(gg[...], axis=0, keepdims=True)
        buf[me, 4:5, 768:896] = jnp.sum(bf[...], axis=0, keepdims=True)
        lrow = jnp.sum(ls[...], axis=0, keepdims=True)
        lsum = lrow[:, 0:128]
        for c in range(1, D // 128):
            lsum = lsum + lrow[:, 128 * c:128 * (c + 1)]
        buf[me, 4:5, 896:1024] = lsum
        buf[me, 8:24, 0:GLA_K] = wau[0:GLA_R, :]
        remote = []
        for d in range(1, N_DEV):
            to, me = _peer(d)
            cp = pltpu.make_async_remote_copy(src_ref=buf.at[me], dst_ref=buf.at[me], send_sem=send.at[d - 1],
                                              recv_sem=recv.at[d - 1], device_id=to, device_id_type=MESH)
            cp.start()
            remote.append(cp)
        for cp in remote:
            cp.wait_send()
        for cp in remote:
            cp.wait_recv()
        g = buf[0]
        for j in range(1, N_DEV):
            g = g + buf[j]
        g_ref[...] = g
        d_ref[...], m2_ref[...], v2_ref[...] = _adamw_math(g, w_ref[...], m_ref[...], v_ref[...])
        l_ref[...] = g[4:5, 896:1024]

    packed = jax.ShapeDtypeStruct((SMALL_ROWS, D), F32)
    return pl.pallas_call(
        body, name="small_allreduce",
        out_shape=[packed, packed, packed, packed, jax.ShapeDtypeStruct((1, 128), F32)],
        scratch_shapes=[pltpu.VMEM((N_DEV, SMALL_ROWS, D), F32), pltpu.SemaphoreType.DMA((N_DEV - 1,)),
                        pltpu.SemaphoreType.DMA((N_DEV - 1,))],
    )(small["g_mix"], small["g_mem"], small["g_ffn"], small["g_final"], small["b_alpha"], small["g_gla_head"],
      small["b_forget"], small["w_alpha_up"], small["loss"], w, m, v)


def _slabs(g, axis):
    R, C = g.shape
    if axis == 0:
        return g.reshape(N_DEV, R // N_DEV, C)
    return g.reshape(R, N_DEV, C // N_DEV).transpose(1, 0, 2)


def _unslab(t, axis):
    n, r, c = t.shape
    if axis == 0:
        return t.reshape(n * r, c)
    return t.transpose(1, 0, 2).reshape(r, n * c)


def kernel(x, mem, g_mix, w_in, w_alpha_up, b_alpha, b_forget, g_gla_head, g_mem, w_mem_kv, w_gla_o, w_fox_o, w_mem_o, w_out, g_ffn, w_ff1, w_ff2, g_final, loss_target, m_g_mix, m_w_in, m_w_alpha_up, m_b_alpha, m_b_forget, m_g_gla_head, m_g_mem, m_w_mem_kv, m_w_gla_o, m_w_fox_o, m_w_mem_o, m_w_out, m_g_ffn, m_w_ff1, m_w_ff2, m_g_final, v_g_mix, v_w_in, v_w_alpha_up, v_b_alpha, v_b_forget, v_g_gla_head, v_g_mem, v_w_mem_kv, v_w_gla_o, v_w_fox_o, v_w_mem_o, v_w_out, v_g_ffn, v_w_ff1, v_w_ff2, v_g_final):
    names = ["g_mix", "w_in", "w_alpha_up", "b_alpha", "b_forget", "g_gla_head", "g_mem", "w_mem_kv", "w_gla_o", "w_fox_o",
             "w_mem_o", "w_out", "g_ffn", "w_ff1", "w_ff2", "g_final"]
    w = dict(g_mix=g_mix, w_in=w_in, w_alpha_up=w_alpha_up, b_alpha=b_alpha, b_forget=b_forget, g_gla_head=g_gla_head,
             g_mem=g_mem, w_mem_kv=w_mem_kv, w_gla_o=w_gla_o, w_fox_o=w_fox_o, w_mem_o=w_mem_o, w_out=w_out, g_ffn=g_ffn,
             w_ff1=w_ff1, w_ff2=w_ff2, g_final=g_final)
    m = dict(g_mix=m_g_mix, w_in=m_w_in, w_alpha_up=m_w_alpha_up, b_alpha=m_b_alpha, b_forget=m_b_forget,
             g_gla_head=m_g_gla_head, g_mem=m_g_mem, w_mem_kv=m_w_mem_kv, w_gla_o=m_w_gla_o, w_fox_o=m_w_fox_o,
             w_mem_o=m_w_mem_o, w_out=m_w_out, g_ffn=m_g_ffn, w_ff1=m_w_ff1, w_ff2=m_w_ff2, g_final=m_g_final)
    v = dict(g_mix=v_g_mix, w_in=v_w_in, w_alpha_up=v_w_alpha_up, b_alpha=v_b_alpha, b_forget=v_b_forget,
             g_gla_head=v_g_gla_head, g_mem=v_g_mem, w_mem_kv=v_w_mem_kv, w_gla_o=v_w_gla_o, w_fox_o=v_w_fox_o,
             w_mem_o=v_w_mem_o, w_out=v_w_out, g_ffn=v_g_ffn, w_ff1=v_w_ff1, w_ff2=v_w_ff2, g_final=v_g_final)
    me = lax.axis_index("x") * 4 + lax.axis_index("y") * 2 + lax.axis_index("c")

    shard = lambda n: w[n][0].astype(BF16)
    w_in_all, w_au_all = _gather_weights([shard("w_in"), shard("w_alpha_up")])
    p = dict(w_in=_unslab(w_in_all, 1), w_alpha_up=_unslab(w_au_all, 1), g_mix=g_mix, b_alpha=b_alpha, b_forget=b_forget,
             g_gla_head=g_gla_head, g_mem=g_mem, g_ffn=g_ffn, g_final=g_final)

    dx, big, small = _local_step(x[0], mem[0], loss_target[0], p, [shard(n) for n, _ in BIG[1:]])

    out_g, out_d, out_m, out_v = {}, {}, {}, {}
    for n, _ in BIG:
        g_, d_, m_, v_ = _adamw_sum(big[n], w[n][0], m[n][0], v[n][0], "adamw_" + n)
        out_g[n], out_d[n], out_m[n], out_v[n] = g_[None], d_[None], m_[None], v_[None]

    full = lambda d: dict(d, w_alpha_up=jnp.zeros((1, GLA_R, GLA_K), F32))
    gs, ds, ms, vs, lrow = _small_allreduce(small, _pack_small(full(w)), _pack_small(full(m)), _pack_small(full(v)))
    g_s, d_s, m_s, v_s = _unpack_small(gs), _unpack_small(ds), _unpack_small(ms), _unpack_small(vs)
    for n in names:
        if n not in out_g and n != "w_alpha_up":
            out_g[n], out_d[n], out_m[n], out_v[n] = g_s[n], d_s[n], m_s[n], v_s[n]
    g_au = lax.dynamic_slice_in_dim(g_s["w_alpha_up"][0], me * (GLA_K // N_DEV), GLA_K // N_DEV, axis=1)
    g_, d_, m_, v_ = _adamw_sum(g_au[None], w_alpha_up[0], m_w_alpha_up[0], v_w_alpha_up[0], "adamw_w_alpha_up")
    out_g["w_alpha_up"], out_d["w_alpha_up"], out_m["w_alpha_up"], out_v["w_alpha_up"] = g_[None], d_[None], m_[None], v_[None]

    loss = jnp.sum(lrow) * (0.5 / D)
    return (loss, dx[None], *[out_g[n] for n in names], *[out_d[n] for n in names], *[out_m[n] for n in names],
            *[out_v[n] for n in names])
```

```python
import jax
import jax.numpy as jnp
from jax import lax
from jax.experimental import pallas as pl
from jax.experimental.pallas import tpu as pltpu

F32, BF16 = jnp.float32, jnp.bfloat16
HIGHEST = lax.Precision.HIGHEST
MESH = pl.DeviceIdType.MESH

N_DEV = 8
D = 1024
EPS = 1e-6
CHUNK = 64
N_MEM = 256
GLA_H, GLA_DK, GLA_DV = 4, 64, 128
GLA_K, GLA_V, GLA_R = 256, 512, 16
FOX_H, FOX_DH, FOX_W = 8, 64, 512
MEM_H, MEM_DH, MEM_W = 4, 128, 512
D_FF = 4096
D_IN = 6680
FOX_SCALE = 0.125
GLA_SCALE = 0.125
MEM_SCALE = MEM_DH ** -0.5
GLA_TAU_INV = 1.0 / 16.0
NEG = -1e30

O_GQ, O_GK, O_GV, O_GG, O_GA, O_FQ, O_FK, O_FV, O_FF, O_MQ, O_GT = 0, 256, 512, 1024, 1536, 1552, 2064, 2576, 3088, 3096, 3608
A_FQ, A_FK, A_FV, A_MQ, A_W = 1024, 2048, 3072, 3584, 4096
S_W = 640
G_W = 3072
P_FOX, P_FOX_W, P_MQ, P_GT, P_GLA, P_GLA_W, P_W = 0, 2560, 2560, 3072, 6144, 2048, 8192
FF_LANE = 16
AUG = 64
FOX_LIVE = 80

ADAM_LR, ADAM_B1, ADAM_B2, ADAM_EPS, ADAM_WD, ADAM_STEP = 0.001, 0.9, 0.999, 1e-08, 0.01, 10
V7X_VMEM_LIMIT = 52 * 1024 * 1024
FOX_TK = 512
FOX_TQ = 1024
FOX_BWD_WIDE = 2048


def _params(*sem):
    return pltpu.CompilerParams(dimension_semantics=sem, vmem_limit_bytes=V7X_VMEM_LIMIT)


def _nt(a, b):
    return lax.dot_general(a, b, (((1,), (1,)), ((), ())), preferred_element_type=F32)


def _tn(a, b):
    return lax.dot_general(a, b, (((0,), (0,)), ((), ())), preferred_element_type=F32)


def _nn(a, b):
    return jnp.dot(a, b, preferred_element_type=F32)


def _log_sigmoid(z):
    return jnp.minimum(z, 0.0) - jnp.log(1.0 + jnp.exp(-jnp.abs(z)))


def _sum01(m01, x):
    x1 = x.astype(BF16)
    x2 = (x - x1.astype(F32)).astype(BF16)
    x3 = (x - x1.astype(F32) - x2.astype(F32)).astype(BF16)
    return _nn(m01, x1) + _nn(m01, x2) + _nn(m01, x3)


def _sum8(x):
    return x.reshape(x.shape[0] // 8, 8, x.shape[1]).sum(axis=0)


def _rms(xv):
    r = lax.rsqrt(jnp.mean(xv * xv, axis=-1, keepdims=True) + EPS)
    return r, xv * r


def _rms_bwd(du, g, r, xh):
    w = du * g
    return r * (w - xh * jnp.mean(w * xh, axis=-1, keepdims=True))


def _row_chunks(n, size=256):
    return [slice(r, r + min(size, n)) for r in range(0, n, min(size, n))]


def _tile(n, pref):
    t = min(n, pref)
    assert n % t == 0, (n, t)
    return t


def _proj(x, g, wa, wg, ws, shards):
    S = x.shape[0]
    tm, tn = _tile(S, 1024), 1024
    n_a, n_g = A_W // tn, G_W // tn
    n_i, n_j = S // tm, n_a + n_g + 1
    n_x = len(shards)

    def body(*refs):
        x_ref, g_ref, wa_ref, wg_ref, ws_ref = refs[:5]
        pa_ref, pg_ref, ps_ref, u_ref = refs[5 + n_x:9 + n_x]
        u_s = refs[9 + 2 * n_x]
        gather = lambda: _AllToAll(refs[5:5 + n_x], refs[9 + n_x:9 + 2 * n_x], refs[10 + 2 * n_x:], True)
        i, j = pl.program_id(0), pl.program_id(1)

        @pl.when((i == 0) & (j == 0))
        def _():
            gather().start()

        @pl.when(j == 0)
        def _():
            r, xh = _rms(x_ref[...])
            u_s[...] = (xh * g_ref[...]).astype(BF16)
            u_ref[...] = u_s[...]

        @pl.when(j < n_a)
        def _():
            pa_ref[...] = _nn(u_s[...], wa_ref[...]).astype(BF16)

        @pl.when((j >= n_a) & (j < n_a + n_g))
        def _():
            pg_ref[...] = _nn(u_s[...], wg_ref[...]).astype(BF16)

        @pl.when(j == n_a + n_g)
        def _():
            ps_ref[...] = _nn(u_s[...], ws_ref[...])

        @pl.when((i == n_i - 1) & (j == n_j - 1))
        def _():
            gather().wait()

    in_a = lambda j: jnp.minimum(j, n_a - 1)
    in_g = lambda j: jnp.clip(j - n_a, 0, n_g - 1)
    row = pl.BlockSpec((tm, D), lambda i, j: (i, 0))
    any_spec = pl.BlockSpec(memory_space=pl.ANY)
    out = pl.pallas_call(
        body, name="proj", grid=(n_i, n_j),
        in_specs=[row, pl.BlockSpec((1, D), lambda i, j: (0, 0)), pl.BlockSpec((D, tn), lambda i, j: (0, in_a(j))),
                  pl.BlockSpec((D, tn), lambda i, j: (0, in_g(j))),
                  pl.BlockSpec((D, S_W), lambda i, j: (0, 0), pipeline_mode=pl.Buffered(1))] + [any_spec] * n_x,
        out_specs=[pl.BlockSpec((tm, tn), lambda i, j: (i, in_a(j))), pl.BlockSpec((tm, tn), lambda i, j: (i, in_g(j))),
                   pl.BlockSpec((tm, S_W), lambda i, j: (i, 0)), row] + [any_spec] * n_x,
        out_shape=[jax.ShapeDtypeStruct((S, A_W), BF16), jax.ShapeDtypeStruct((S, G_W), BF16),
                   jax.ShapeDtypeStruct((S, S_W), F32), jax.ShapeDtypeStruct((S, D), BF16)] + _gathered_shapes(shards),
        scratch_shapes=[pltpu.VMEM((tm, D), BF16)] + _exchange_sems(n_x),
        compiler_params=_params("arbitrary", "arbitrary"),
    )(x, g, wa, wg, ws, *shards)
    return out[0], out[1], out[2], out[3], out[4:]


def _wgrad(a, b, name, slab_axis=None):
    S, Ka = a.shape
    N = b.shape[1]
    tka, tn, ts = _tile(Ka, 1024), _tile(N, 1024), _tile(S, 1024)
    n_s = S // ts
    per = N // N_DEV
    slabs_per_step = tn // per

    def body(a_ref, b_ref, o_ref, acc):
        s = pl.program_id(2)

        @pl.when(s == 0)
        def _():
            acc[...] = jnp.zeros_like(acc)

        acc[...] += _tn(a_ref[...].astype(BF16), b_ref[...].astype(BF16))

        @pl.when(s == n_s - 1)
        def _():
            if slab_axis == 1:
                for q in range(slabs_per_step):
                    o_ref[q] = acc[:, per * q:per * (q + 1)].astype(BF16)
            else:
                o_ref[...] = acc[...].astype(o_ref.dtype)

    if slab_axis == 1:
        out_spec = pl.BlockSpec((slabs_per_step, tka, per), lambda i, j, s: (j, i, 0))
        out_shape = jax.ShapeDtypeStruct((N_DEV, Ka, per), BF16)
    else:
        out_spec = pl.BlockSpec((tka, tn), lambda i, j, s: (i, j))
        out_shape = jax.ShapeDtypeStruct((Ka, N), F32 if slab_axis is None else BF16)
    out = pl.pallas_call(
        body, name=name, grid=(Ka // tka, N // tn, n_s),
        in_specs=[pl.BlockSpec((ts, tka), lambda i, j, s: (s, i)), pl.BlockSpec((ts, tn), lambda i, j, s: (s, j))],
        out_specs=out_spec, out_shape=out_shape,
        scratch_shapes=[pltpu.VMEM((tka, tn), F32)],
        compiler_params=_params("arbitrary", "arbitrary", "arbitrary"),
    )(a, b)
    return out.reshape(N_DEV, Ka // N_DEV, N) if slab_axis == 0 else out


def _nt_rmsbwd(a, w, xin, g, dres, name, emit_bf16, slabs=()):
    S, K = a.shape
    tm, tk = _tile(S, 1024), _tile(K, 1024)
    n_i, n_k = S // tm, K // tk
    n_x, n_o = len(slabs), 3 if emit_bf16 else 2

    def body(*refs):
        a_ref, w_ref, x_ref, g_ref, r_ref = refs[:5]
        o_ref = refs[5 + n_x]
        rest = refs[6 + n_x:5 + n_x + n_o] + (refs[5 + 2 * n_x + n_o],)
        dg_ref, acc = rest[-2], rest[-1]
        scatter = lambda: _AllToAll(refs[5:5 + n_x], refs[5 + n_x + n_o:5 + 2 * n_x + n_o], refs[6 + 2 * n_x + n_o:], False)
        i, k = pl.program_id(0), pl.program_id(1)

        if n_x:
            @pl.when((i == 0) & (k == 0))
            def _():
                scatter().start()

        @pl.when(k == 0)
        def _():
            acc[...] = jnp.zeros_like(acc)

        acc[...] += _nt(a_ref[...], w_ref[...])

        @pl.when(k == n_k - 1)
        def _():
            @pl.when(i == 0)
            def _():
                dg_ref[...] = jnp.zeros_like(dg_ref)

            for rows in _row_chunks(tm):
                du = acc[rows, :]
                r, xh = _rms(x_ref[rows, :])
                out = r_ref[rows, :] + _rms_bwd(du, g_ref[...], r, xh)
                o_ref[rows, :] = out
                if emit_bf16:
                    rest[0][rows, :] = out.astype(BF16)
                dg_ref[...] += _sum8(du * xh)

        if n_x:
            @pl.when((i == n_i - 1) & (k == n_k - 1))
            def _():
                scatter().wait()

    row = pl.BlockSpec((tm, D), lambda i, k: (i, 0))
    any_spec = pl.BlockSpec(memory_space=pl.ANY)
    out_shape = [jax.ShapeDtypeStruct((S, D), F32)]
    out_specs = [row]
    if emit_bf16:
        out_shape.append(jax.ShapeDtypeStruct((S, D), BF16))
        out_specs.append(row)
    out_shape.append(jax.ShapeDtypeStruct((8, D), F32))
    out_specs.append(pl.BlockSpec((8, D), lambda i, k: (0, 0)))
    out = pl.pallas_call(
        body, name=name, grid=(n_i, n_k),
        in_specs=[pl.BlockSpec((tm, tk), lambda i, k: (i, k)), pl.BlockSpec((D, tk), lambda i, k: (0, k)),
                  row, pl.BlockSpec((1, D), lambda i, k: (0, 0)), row] + [any_spec] * n_x,
        out_specs=out_specs + [any_spec] * n_x,
        out_shape=out_shape + [jax.ShapeDtypeStruct(b.shape, b.dtype) for b in slabs],
        scratch_shapes=[pltpu.VMEM((tm, D), F32)] + (_exchange_sems(n_x) if n_x else []),
        compiler_params=_params("arbitrary", "arbitrary"),
    )(a, w, xin, g, dres, *slabs)
    return (*out[:n_o], out[n_o:]) if n_x else out


def _merge(og, ofox, omem, wg, wf, wm, pg):
    S = og.shape[0]
    tm = _tile(S, 512)

    def body(og_ref, of_ref, om_ref, wg_ref, wf_ref, wm_ref, pg_ref, y_ref, mg_ref):
        tot = None
        for i, (o_ref, w_ref) in enumerate(((og_ref, wg_ref), (of_ref, wf_ref), (om_ref, wm_ref))):
            y = _nn(o_ref[...].astype(BF16), w_ref[...])
            y_ref[i] = y.astype(BF16)
            t = jax.nn.sigmoid(pg_ref[:, D * i:D * (i + 1)].astype(F32)) * y
            tot = t if tot is None else tot + t
        mg_ref[...] = tot.astype(BF16)

    o_spec = pl.BlockSpec((tm, 512), lambda i: (i, 0))
    w_spec = pl.BlockSpec((512, D), lambda i: (0, 0))
    return pl.pallas_call(
        body, name="merge", grid=(S // tm,),
        in_specs=[o_spec, o_spec, o_spec, w_spec, w_spec, w_spec, pl.BlockSpec((tm, G_W), lambda i: (i, 0))],
        out_specs=[pl.BlockSpec((3, tm, D), lambda i: (0, i, 0)), pl.BlockSpec((tm, D), lambda i: (i, 0))],
        out_shape=[jax.ShapeDtypeStruct((3, S, D), BF16), jax.ShapeDtypeStruct((S, D), BF16)],
        compiler_params=_params("arbitrary"),
    )(og, ofox, omem, wg, wf, wm, pg)


def _out_proj(mg, w_out, x, g_ffn):
    S = x.shape[0]
    tm = _tile(S, 512)

    def body(mg_ref, w_ref, x_ref, g_ref, h_ref, u_ref):
        h = x_ref[...] + _nn(mg_ref[...], w_ref[...])
        h_ref[...] = h
        r, xh = _rms(h)
        u_ref[...] = (xh * g_ref[...]).astype(BF16)

    row = pl.BlockSpec((tm, D), lambda i: (i, 0))
    return pl.pallas_call(
        body, name="out_proj", grid=(S // tm,),
        in_specs=[row, pl.BlockSpec((D, D), lambda i: (0, 0)), row, pl.BlockSpec((1, D), lambda i: (0, 0))],
        out_specs=[row, row],
        out_shape=[jax.ShapeDtypeStruct((S, D), F32), jax.ShapeDtypeStruct((S, D), BF16)],
        compiler_params=_params("arbitrary"),
    )(mg, w_out, x, g_ffn)


def _ff1(u2, w1):
    S = u2.shape[0]
    tm, tn = _tile(S, 1024), 1024

    def body(u_ref, w_ref, a_ref, act_ref):
        a = _nn(u_ref[...], w_ref[...])
        a_ref[...] = a.astype(BF16)
        act_ref[...] = jnp.square(jnp.maximum(a, 0.0)).astype(BF16)

    blk = pl.BlockSpec((tm, tn), lambda i, j: (i, j))
    return pl.pallas_call(
        body, name="ff1", grid=(S // tm, D_FF // tn),
        in_specs=[pl.BlockSpec((tm, D), lambda i, j: (i, 0)), pl.BlockSpec((D, tn), lambda i, j: (0, j))],
        out_specs=[blk, blk],
        out_shape=[jax.ShapeDtypeStruct((S, D_FF), BF16), jax.ShapeDtypeStruct((S, D_FF), BF16)],
        compiler_params=_params("arbitrary", "arbitrary"),
    )(u2, w1)


def _ff2_loss(act, w2, h1, g_final, target):
    S = act.shape[0]
    tm, tk = _tile(S, 1024), 1024
    n_k = D_FF // tk

    def body(a_ref, w_ref, h_ref, g_ref, t_ref, d_ref, db_ref, ls_ref, dg_ref, acc):
        i, k = pl.program_id(0), pl.program_id(1)

        @pl.when(k == 0)
        def _():
            acc[...] = jnp.zeros_like(acc)

        acc[...] += _nn(a_ref[...], w_ref[...])

        @pl.when(k == n_k - 1)
        def _():
            @pl.when(i == 0)
            def _():
                ls_ref[...] = jnp.zeros_like(ls_ref)
                dg_ref[...] = jnp.zeros_like(dg_ref)

            gf = g_ref[...]
            for rows in _row_chunks(tm):
                r, xh = _rms(h_ref[rows, :] + acc[rows, :])
                err = xh * gf - t_ref[rows, :]
                dy = err * (1.0 / D)
                dh = _rms_bwd(dy, gf, r, xh)
                d_ref[rows, :] = dh
                db_ref[rows, :] = dh.astype(BF16)
                ls_ref[...] += _sum8(err * err)
                dg_ref[...] += _sum8(dy * xh)

    row = pl.BlockSpec((tm, D), lambda i, k: (i, 0))
    part = pl.BlockSpec((8, D), lambda i, k: (0, 0))
    return pl.pallas_call(
        body, name="ff2_loss", grid=(S // tm, n_k),
        in_specs=[pl.BlockSpec((tm, tk), lambda i, k: (i, k)), pl.BlockSpec((tk, D), lambda i, k: (k, 0)),
                  row, pl.BlockSpec((1, D), lambda i, k: (0, 0)), row],
        out_specs=[row, row, part, part],
        out_shape=[jax.ShapeDtypeStruct((S, D), F32), jax.ShapeDtypeStruct((S, D), BF16),
                   jax.ShapeDtypeStruct((8, D), F32), jax.ShapeDtypeStruct((8, D), F32)],
        scratch_shapes=[pltpu.VMEM((tm, D), F32)],
        compiler_params=_params("arbitrary", "arbitrary"),
    )(act, w2, h1, g_final, target)


def _dact(dh2b, w2, a):
    S = a.shape[0]
    tm, tn = _tile(S, 1024), 1024

    def body(d_ref, w_ref, a_ref, o_ref):
        da = _nt(d_ref[...], w_ref[...])
        o_ref[...] = (da * (2.0 * jnp.maximum(a_ref[...].astype(F32), 0.0))).astype(BF16)

    blk = pl.BlockSpec((tm, tn), lambda i, j: (i, j))
    return pl.pallas_call(
        body, name="dact", grid=(S // tm, D_FF // tn),
        in_specs=[pl.BlockSpec((tm, D), lambda i, j: (i, 0)), pl.BlockSpec((tn, D), lambda i, j: (j, 0)), blk],
        out_specs=blk, out_shape=jax.ShapeDtypeStruct((S, D_FF), BF16),
        compiler_params=_params("arbitrary", "arbitrary"),
    )(dh2b, w2, a)


def _dmerge(dh1b, w_out, pg, y3, wg, wf, wm):
    S = dh1b.shape[0]
    tm = _tile(S, 512)

    def body(d_ref, w_ref, pg_ref, y_ref, wg_ref, wf_ref, wm_ref, *outs):
        dy_refs, do_refs, dg_ref = outs[0:3], outs[3:6], outs[6]
        dm = _nt(d_ref[...], w_ref[...])
        for i, wo_ref in enumerate((wg_ref, wf_ref, wm_ref)):
            gt = jax.nn.sigmoid(pg_ref[:, D * i:D * (i + 1)].astype(F32))
            dy = (dm * gt).astype(BF16)
            dy_refs[i][...] = dy
            do_refs[i][...] = _nt(dy, wo_ref[...])
            dg_ref[:, D * i:D * (i + 1)] = (dm * y_ref[i].astype(F32) * (gt * (1.0 - gt))).astype(BF16)

    row = pl.BlockSpec((tm, D), lambda i: (i, 0))
    half = pl.BlockSpec((tm, 512), lambda i: (i, 0))
    w_spec = pl.BlockSpec((512, D), lambda i: (0, 0))
    return pl.pallas_call(
        body, name="dmerge", grid=(S // tm,),
        in_specs=[row, pl.BlockSpec((D, D), lambda i: (0, 0)), pl.BlockSpec((tm, G_W), lambda i: (i, 0)),
                  pl.BlockSpec((3, tm, D), lambda i: (0, i, 0)), w_spec, w_spec, w_spec],
        out_specs=[row, row, row, half, half, half, pl.BlockSpec((tm, G_W), lambda i: (i, P_GT // G_W))],
        out_shape=[jax.ShapeDtypeStruct((S, D), BF16)] * 3 + [jax.ShapeDtypeStruct((S, 512), F32)] * 3
        + [jax.ShapeDtypeStruct((S, P_W), BF16)],
        compiler_params=_params("arbitrary"),
    )(dh1b, w_out, pg, y3, wg, wf, wm)


def _gla_block_terms(gq_ref, gk_ref, ps_ref, wau_ref, ba_ref, tb):
    gaff = ps_ref[:, 512:640]
    z = _nn(gaff.astype(BF16), wau_ref[...]) + ba_ref[...]
    la = _log_sigmoid(z) * GLA_TAU_INV
    rr = lax.broadcasted_iota(jnp.int32, (tb, tb), 0)
    cc = lax.broadcasted_iota(jnp.int32, (tb, tb), 1)
    same = jnp.right_shift(rr, 6) == jnp.right_shift(cc, 6)
    tri = jnp.where(same & (cc <= rr), 1.0, 0.0).astype(BF16)
    ones = jnp.where(same, 1.0, 0.0).astype(BF16)
    b = _sum01(tri, la)
    bl = _sum01(ones, la)
    e_pos, e_neg, e_last, dec = jnp.exp(b), jnp.exp(-b), jnp.exp(bl - b), jnp.exp(bl)
    q = gq_ref[...].astype(F32) * GLA_SCALE
    k = gk_ref[...].astype(F32)
    return dict(gaff=gaff, z=z, same=same, rr=rr, cc=cc, ones=ones, e_pos=e_pos, e_neg=e_neg, e_last=e_last, dec=dec,
                qp=q * e_pos, qn=q * e_neg, kn=k * e_neg, kp=k * e_pos, kd=k * e_last)


def _head_masked(x, store):
    lane = lax.broadcasted_iota(jnp.int32, x.shape, 1)
    for h in range(GLA_H):
        store[:, h] = jnp.where(jnp.right_shift(lane, 6) == h, x, 0.0).astype(BF16).reshape(-1, CHUNK, GLA_K)


def _lower4():
    t = jnp.bitwise_and(lax.broadcasted_iota(jnp.int32, (GLA_H * CHUNK, CHUNK), 0), CHUNK - 1)
    return t >= lax.broadcasted_iota(jnp.int32, (GLA_H * CHUNK, CHUNK), 1)


def _stack_heads(ref, rows):
    return jnp.concatenate([ref[rows, GLA_DV * h:GLA_DV * (h + 1)] for h in range(GLA_H)], axis=0)


def _gla_fwd(pa, ps, wau, ba, gh):
    S = pa.shape[0]
    tb = _tile(S, 512)
    n_c = tb // CHUNK
    n_b = S // tb

    def body(gq_ref, gk_ref, gv_ref, ps_ref, wau_ref, ba_ref, gh_ref, o_ref, og_ref, sp_ref,
             qpm, qnm, kdm, kn_s, kp_s, dec_s, state):
        @pl.when(pl.program_id(0) == 0)
        def _():
            state[...] = jnp.zeros_like(state)

        t = _gla_block_terms(gq_ref, gk_ref, ps_ref, wau_ref, ba_ref, tb)
        _head_masked(t["qp"], qpm)
        _head_masked(t["qn"], qnm)
        _head_masked(t["kd"], kdm)
        kn_s[...] = t["kn"].astype(BF16)
        kp_s[...] = t["kp"].astype(BF16)
        dec_s[...] = t["dec"]
        lower = _lower4()

        sp = state[...]
        for c in range(n_c):
            rows = slice(c * CHUNK, (c + 1) * CHUNK)
            sp_ref[c] = sp
            qp, qn, kd = (s[c].reshape(GLA_H * CHUNK, GLA_K) for s in (qpm, qnm, kdm))
            attn = jnp.where(lower, _nt(qp, kn_s[rows, :]), _nt(qn, kp_s[rows, :])).astype(BF16)
            inter = _nt(qp, sp.astype(BF16))
            for h in range(GLA_H):
                mine = slice(CHUNK * h, CHUNK * (h + 1))
                cols = slice(GLA_DV * h, GLA_DV * (h + 1))
                o_ref[rows, cols] = _nn(attn[mine], gv_ref[rows, cols]) + inter[mine]
            sp = sp * dec_s[c * CHUNK:c * CHUNK + 1, :] + _tn(_stack_heads(gv_ref, rows), kd)
        state[...] = sp
        for h in range(GLA_H):
            cols = slice(GLA_DV * h, GLA_DV * (h + 1))
            r, xh = _rms(o_ref[:, cols])
            gg = ps_ref[:, cols]
            og_ref[:, cols] = ((xh * gh_ref[:, cols]) * (gg * jax.nn.sigmoid(gg))).astype(BF16)

    return pl.pallas_call(
        body, name="gla_fwd", grid=(n_b,),
        in_specs=[pl.BlockSpec((tb, GLA_K), lambda i: (i, 0)), pl.BlockSpec((tb, GLA_K), lambda i: (i, 1)),
                  pl.BlockSpec((tb, GLA_V), lambda i: (i, 1)), pl.BlockSpec((tb, S_W), lambda i: (i, 0)),
                  pl.BlockSpec((128, GLA_K), lambda i: (0, 0)), pl.BlockSpec((1, GLA_K), lambda i: (0, 0)),
                  pl.BlockSpec((1, GLA_V), lambda i: (0, 0))],
        out_specs=[pl.BlockSpec((tb, GLA_V), lambda i: (i, 0)), pl.BlockSpec((tb, GLA_V), lambda i: (i, 0)),
                   pl.BlockSpec((n_c, GLA_DV, GLA_K), lambda i: (i, 0, 0))],
        out_shape=[jax.ShapeDtypeStruct((S, GLA_V), F32), jax.ShapeDtypeStruct((S, GLA_V), BF16),
                   jax.ShapeDtypeStruct((S // CHUNK, GLA_DV, GLA_K), F32)],
        scratch_shapes=[pltpu.VMEM((n_c, GLA_H, CHUNK, GLA_K), BF16)] * 3
        + [pltpu.VMEM((tb, GLA_K), BF16), pltpu.VMEM((tb, GLA_K), BF16), pltpu.VMEM((tb, GLA_K), F32),
           pltpu.VMEM((GLA_DV, GLA_K), F32)],
        compiler_params=_params("arbitrary"),
    )(pa, pa, pa, ps, wau, ba, gh)


def _gla_bwd(pa, ps, wau, ba, gh, o_gla, d_og, sprev, dgaff_fox, d_proj, slabs):
    S = pa.shape[0]
    tb = _tile(S, 512)
    n_c = tb // CHUNK
    n_b = S // tb
    n_x = len(slabs)
    c_gk, c_gv, c_gg, c_ga, c_end = GLA_K, 2 * GLA_K, 2 * GLA_K + GLA_V, 2 * GLA_K + 2 * GLA_V, 2 * GLA_K + 2 * GLA_V + 128

    def body(*refs):
        gq_ref, gk_ref, gv_ref, ps_ref, wau_ref, ba_ref, gh_ref, o_ref, dog_ref, sp_ref, dfx_ref = refs[:11]
        dp_ref, dwau_ref, dba_ref, dgh_ref = refs[12 + n_x:16 + n_x]
        (qpm, qnm, kdm, kn_s, kp_s, dec_s, do_s, dqp_s, dqn_s, dkn_s, dkp_s, dkd_s, ddec_s,
         dstate) = refs[16 + 2 * n_x:30 + 2 * n_x]
        scatter = lambda: _AllToAll(refs[12:12 + n_x], refs[16 + n_x:16 + 2 * n_x], refs[30 + 2 * n_x:], False)
        first = pl.program_id(0) == 0
        dp_ref[:, c_end:] = jnp.zeros((tb, P_GLA_W - c_end), BF16)

        @pl.when(first)
        def _():
            dstate[...] = jnp.zeros_like(dstate)
            scatter().start()

        t = _gla_block_terms(gq_ref, gk_ref, ps_ref, wau_ref, ba_ref, tb)
        _head_masked(t["qp"], qpm)
        _head_masked(t["qn"], qnm)
        _head_masked(t["kd"], kdm)
        kn_s[...] = t["kn"].astype(BF16)
        kp_s[...] = t["kp"].astype(BF16)
        dec_s[...] = t["dec"]

        dgh_parts = []
        for h in range(GLA_H):
            cols = slice(GLA_DV * h, GLA_DV * (h + 1))
            r, xh = _rms(o_ref[:, cols])
            g = gh_ref[:, cols]
            gg = ps_ref[:, cols]
            sg = jax.nn.sigmoid(gg)
            d_out = dog_ref[:, cols]
            dp_ref[:, c_gg + GLA_DV * h:c_gg + GLA_DV * (h + 1)] = (d_out * (xh * g) * (sg * (1.0 + gg * (1.0 - sg)))).astype(BF16)
            d_on = d_out * (gg * sg)
            dgh_parts.append(_sum8(d_on * xh))
            do_s[:, cols] = _rms_bwd(d_on, g, r, xh).astype(BF16)
        dgh_part = jnp.concatenate(dgh_parts, axis=1)

        lower = _lower4()
        lane = lax.broadcasted_iota(jnp.int32, (CHUNK, GLA_K), 1)

        def own_columns(stacked):
            return sum(jnp.where(jnp.right_shift(lane, 6) == h, stacked[CHUNK * h:CHUNK * (h + 1)], 0.0) for h in range(GLA_H))

        ds_next = dstate[...]
        for c in reversed(range(n_c)):
            rows = slice(c * CHUNK, (c + 1) * CHUNK)
            dsb = ds_next.astype(BF16)
            sp = sp_ref[c]
            knc, kpc = kn_s[rows, :], kp_s[rows, :]
            qp, qn, kd = (s[c].reshape(GLA_H * CHUNK, GLA_K) for s in (qpm, qnm, kdm))
            v4, do4 = _stack_heads(gv_ref, rows), _stack_heads(do_s, rows)
            ddec_s[rows, :] = jnp.broadcast_to(jnp.sum(ds_next * sp, axis=0, keepdims=True), (CHUNK, GLA_K))
            attn = jnp.where(lower, _nt(qp, knc), _nt(qn, kpc)).astype(BF16)
            da = jnp.concatenate([_nt(do4[CHUNK * h:CHUNK * (h + 1)], v4[CHUNK * h:CHUNK * (h + 1)]) for h in range(GLA_H)],
                                 axis=0)
            dac = jnp.where(lower, da, 0.0).astype(BF16)
            daa = jnp.where(lower, 0.0, da).astype(BF16)
            dqp_s[rows, :] = own_columns(_nn(dac, knc) + _nn(do4, sp.astype(BF16)))
            dqn_s[rows, :] = own_columns(_nn(daa, kpc))
            dkd_s[rows, :] = own_columns(_nn(v4, dsb))
            dkn_s[rows, :] = _tn(dac, qp)
            dkp_s[rows, :] = _tn(daa, qn)
            dv_state = _nt(kd, dsb)
            for h in range(GLA_H):
                mine = slice(CHUNK * h, CHUNK * (h + 1))
                dp_ref[rows, c_gv + GLA_DV * h:c_gv + GLA_DV * (h + 1)] = (_tn(attn[mine], do4[mine]) + dv_state[mine]).astype(BF16)
            ds_next = ds_next * dec_s[c * CHUNK:c * CHUNK + 1, :] + _tn(do4, qp)
        dstate[...] = ds_next

        dqp, dqn, dkn, dkp, dkd = dqp_s[...], dqn_s[...], dkn_s[...], dkp_s[...], dkd_s[...]
        dp_ref[:, 0:c_gk] = ((dqp * t["e_pos"] + dqn * t["e_neg"]) * GLA_SCALE).astype(BF16)
        dp_ref[:, c_gk:c_gv] = (dkn * t["e_neg"] + dkp * t["e_pos"] + dkd * t["e_last"]).astype(BF16)
        kd_term = dkd * t["kd"]
        db = dqp * t["qp"] - dqn * t["qn"] - dkn * t["kn"] + dkp * t["kp"] - kd_term
        upper = jnp.where(t["same"] & (t["cc"] >= t["rr"]), 1.0, 0.0).astype(BF16)
        dla = (_sum01(upper, db) + _sum01(t["ones"], kd_term)
               + ddec_s[...] * t["dec"])
        dz = dla * GLA_TAU_INV * jax.nn.sigmoid(-t["z"])
        dzb = dz.astype(BF16)
        dp_ref[:, c_ga:c_end] = (_nt(dzb, wau_ref[...]) + dfx_ref[...]).astype(BF16)
        dwau_part = _tn(t["gaff"].astype(BF16), dzb)
        dba_part = _sum8(dz)

        @pl.when(first)
        def _():
            dwau_ref[...] = dwau_part
            dba_ref[...] = dba_part
            dgh_ref[...] = dgh_part

        @pl.when(jnp.logical_not(first))
        def _():
            dwau_ref[...] += dwau_part
            dba_ref[...] += dba_part
            dgh_ref[...] += dgh_part

        @pl.when(pl.program_id(0) == n_b - 1)
        def _():
            scatter().wait()

    rev = lambda i: (n_b - 1 - i, 0)
    f32k = pltpu.VMEM((tb, GLA_K), F32)
    bf4 = pltpu.VMEM((n_c, GLA_H, CHUNK, GLA_K), BF16)
    any_spec = pl.BlockSpec(memory_space=pl.ANY)
    out = pl.pallas_call(
        body, name="gla_bwd", grid=(n_b,),
        in_specs=[pl.BlockSpec((tb, GLA_K), rev), pl.BlockSpec((tb, GLA_K), lambda i: (n_b - 1 - i, 1)),
                  pl.BlockSpec((tb, GLA_V), lambda i: (n_b - 1 - i, 1)), pl.BlockSpec((tb, S_W), rev),
                  pl.BlockSpec((128, GLA_K), lambda i: (0, 0)), pl.BlockSpec((1, GLA_K), lambda i: (0, 0)),
                  pl.BlockSpec((1, GLA_V), lambda i: (0, 0)), pl.BlockSpec((tb, GLA_V), rev), pl.BlockSpec((tb, GLA_V), rev),
                  pl.BlockSpec((n_c, GLA_DV, GLA_K), lambda i: (n_b - 1 - i, 0, 0)), pl.BlockSpec((tb, 128), rev),
                  any_spec] + [any_spec] * n_x,
        out_specs=[pl.BlockSpec((tb, P_GLA_W), lambda i: (n_b - 1 - i, P_GLA // P_GLA_W)),
                   pl.BlockSpec((128, GLA_K), lambda i: (0, 0)), pl.BlockSpec((8, GLA_K), lambda i: (0, 0)),
                   pl.BlockSpec((8, GLA_V), lambda i: (0, 0))] + [any_spec] * n_x,
        out_shape=[jax.ShapeDtypeStruct((S, P_W), BF16), jax.ShapeDtypeStruct((128, GLA_K), F32),
                   jax.ShapeDtypeStruct((8, GLA_K), F32), jax.ShapeDtypeStruct((8, GLA_V), F32)]
        + [jax.ShapeDtypeStruct(b.shape, b.dtype) for b in slabs],
        input_output_aliases={11: 0},
        scratch_shapes=[bf4, bf4, bf4, pltpu.VMEM((tb, GLA_K), BF16), pltpu.VMEM((tb, GLA_K), BF16), f32k,
                        pltpu.VMEM((tb, GLA_V), BF16), f32k, f32k, f32k, f32k, f32k, f32k, pltpu.VMEM((GLA_DV, GLA_K), F32)]
        + _exchange_sems(n_x),
        compiler_params=_params("arbitrary"),
    )(pa, pa, pa, ps, wau, ba, gh, o_gla, d_og, sprev, dgaff_fox, d_proj, *slabs)
    return out[0], out[1], out[2], out[3], out[4:]


def _split3(x):
    x1 = x.astype(BF16).astype(F32)
    x2 = (x - x1).astype(BF16).astype(F32)
    x3 = (x - x1 - x2).astype(BF16).astype(F32)
    return x1, x2, x3


def _fox_prep(pa, ps, bfg):
    S = pa.shape[0]
    tm = _tile(S, FOX_TK)

    def body(ps_ref, b_ref, fq_ref, fk_ref, fv_ref, q_ref, k_ref, qt_ref, kt_ref, vt_ref, st_ref, carry):
        @pl.when(pl.program_id(0) == 0)
        def _():
            carry[...] = jnp.zeros_like(carry)

        vt_ref[...] = fv_ref[...].astype(F32).T.astype(BF16)
        lf = _log_sigmoid(ps_ref[...] + b_ref[...])
        rr = lax.broadcasted_iota(jnp.int32, (tm, tm), 0)
        cc = lax.broadcasted_iota(jnp.int32, (tm, tm), 1)
        tri = jnp.where(cc <= rr, 1.0, 0.0).astype(F32)
        f = jnp.dot(tri, lf, preferred_element_type=F32, precision=HIGHEST) + carry[0:1, :]
        carry[...] = jnp.broadcast_to(f[tm - 1:tm, :], carry.shape)
        f1, f2, f3 = _split3(f)
        lane = lax.broadcasted_iota(jnp.int32, (tm, 128), 1)
        st_row = lax.broadcasted_iota(jnp.int32, (8, 128), 0)
        st_lane = lax.broadcasted_iota(jnp.int32, (8, 128), 1)
        stats = jnp.zeros((8, 128), F32)
        for h in range(FOX_H):
            cols = slice(128 * h, 128 * (h + 1))
            c = FF_LANE + h
            a1, a2, a3 = f1[:, c:c + 1], f2[:, c:c + 1], f3[:, c:c + 1]
            q = fq_ref[:, cols].astype(F32) * FOX_SCALE
            k = fk_ref[:, cols].astype(F32)
            fh = f[:, c:c + 1]
            vals = (jnp.max(jnp.sum(q * q, axis=-1, keepdims=True)), jnp.max(jnp.sum(k * k, axis=-1, keepdims=True)),
                    jnp.max(fh), jnp.min(fh), jnp.min(jnp.sum(q * k, axis=-1, keepdims=True)))
            for n, val in enumerate(vals):
                stats = jnp.where((st_row == h) & (st_lane == n), val, stats)
            for n, a in enumerate((a1, a2, a3)):
                q = jnp.where(lane == AUG + n, a, q)
                k = jnp.where(lane == AUG + 3 + n, -a, k)
            q = jnp.where((lane >= AUG + 3) & (lane < AUG + 6), 1.0, q)
            k = jnp.where((lane >= AUG) & (lane < AUG + 3), 1.0, k)
            q_ref[:, cols] = q.astype(BF16)
            k_ref[:, cols] = k.astype(BF16)
            qt_ref[cols, :] = q.T.astype(BF16)
            kt_ref[cols, :] = k.T.astype(BF16)
        st_ref[0] = stats

    wide = lambda j: pl.BlockSpec((tm, 1024), lambda i: (i, j))
    tall = lambda n: pl.BlockSpec((n, tm), lambda i: (0, i))
    return pl.pallas_call(
        body, name="fox_prep", grid=(S // tm,),
        in_specs=[pl.BlockSpec((tm, 128), lambda i: (i, 4)), pl.BlockSpec((1, 128), lambda i: (0, 0)), wide(1), wide(2),
                  pl.BlockSpec((tm, FOX_W), lambda i: (i, A_FV // FOX_W))],
        out_specs=[wide(0), wide(0), tall(1024), tall(1024), tall(FOX_W), pl.BlockSpec((1, 8, 128), lambda i: (i, 0, 0))],
        out_shape=[jax.ShapeDtypeStruct((S, 1024), BF16), jax.ShapeDtypeStruct((S, 1024), BF16),
                   jax.ShapeDtypeStruct((1024, S), BF16), jax.ShapeDtypeStruct((1024, S), BF16),
                   jax.ShapeDtypeStruct((FOX_W, S), BF16), jax.ShapeDtypeStruct((S // tm, 8, 128), F32)],
        scratch_shapes=[pltpu.VMEM((8, 128), F32)],
        compiler_params=_params("arbitrary"),
    )(ps, bfg, pa, pa, pa)


FOX_PRUNE_AT = -90.0


def _fox_live_ranges(stats, n_sub, ratio):
    n_b = stats.shape[0]
    q2, k2, f_max, f_min, own = (stats[:, :, n].T for n in range(5))
    slack = 0.01 * jnp.sqrt(q2 * k2) + 1e-5 * jnp.abs(f_max) + 1.0
    bound = (1.01 * jnp.sqrt(q2[:, :, None] * k2[:, None, :]) + (f_max + slack - own)[:, :, None]
             - (f_min - 1e-5 * jnp.abs(f_min))[:, None, :])
    blocks = jnp.arange(n_b)
    dead = (bound <= FOX_PRUNE_AT) & (blocks[None, :] < blocks[:, None])[None]
    dead_fwd = dead.reshape(FOX_H, n_b // n_sub, n_sub, n_b).all(axis=2)
    first = jnp.sum(jnp.cumprod(dead_fwd.astype(jnp.int32), axis=2), axis=2)
    last_live = n_b - 1 - jnp.sum(jnp.cumprod(dead[:, ::-1, :].astype(jnp.int32), axis=1), axis=1)
    first_wide = blocks // ratio + 1
    narrow_end = jnp.minimum(jnp.minimum(first_wide * ratio, n_b)[None], last_live + 1)
    wide_end = jnp.where(last_live >= (first_wide * ratio)[None], last_live // ratio + 1, first_wide[None])
    return first.astype(jnp.int32), narrow_end.astype(jnp.int32), wide_end.astype(jnp.int32)


def _fox_fwd(qa, ka, vt, first):
    S = qa.shape[0]
    tq = _tile(S, FOX_TQ)
    tk = _tile(tq, FOX_TK)
    n_sub = tq // tk

    def body(first_ref, q_ref, k_ref, vt_ref, o_ref, lse_ref):
        pair, i = pl.program_id(0), pl.program_id(1)
        both = lambda f: tuple(f(hh) for hh in range(2))

        def blk(j, carry, diag, heads=(0, 1)):
            ks = pl.ds(pl.multiple_of(j * tk, tk), tk)
            q0 = 0 if diag is None else diag * tk

            def head(hh):
                if hh not in heads:
                    return carry[hh]
                m, l, acc = carry[hh]
                mo, lo, ao = m[:, q0:], l[:, q0:], acc[:, q0:]
                s = _nt(k_ref[ks, 128 * hh:128 * (hh + 1)], q_ref[q0:, 128 * hh:128 * (hh + 1)])
                if diag is not None:
                    live = lax.broadcasted_iota(jnp.int32, s.shape, 1) >= lax.broadcasted_iota(jnp.int32, s.shape, 0)
                    s = jnp.where(live, s, NEG)
                mn = jnp.maximum(mo, jnp.max(s, axis=0, keepdims=True))
                p = jnp.exp(s - mn)
                al = jnp.exp(mo - mn)
                ln = al * lo + jnp.sum(p, axis=0, keepdims=True)
                an = al * ao + _nn(vt_ref[FOX_DH * hh:FOX_DH * (hh + 1), ks], p.astype(BF16))
                if q0:
                    mn, ln, an = (jnp.concatenate([old[:, :q0], new], axis=1) for old, new in ((m, mn), (l, ln), (acc, an)))
                return mn, ln, an

            return both(head)

        one = (jnp.full((1, tq), NEG, F32), jnp.zeros((1, tq), F32), jnp.zeros((FOX_DH, tq), F32))
        past = i * n_sub
        f0, f1 = first_ref[2 * pair, i], first_ref[2 * pair + 1, i]
        join = jnp.maximum(f0, f1)
        solo = lambda hh: lambda c: lax.fori_loop(jnp.minimum(f0, f1), join, lambda j, cc: blk(j, cc, None, (hh,)), c)
        carry = lax.cond(f0 < f1, solo(0), solo(1), (one, one))
        n_both = past - join
        carry = lax.fori_loop(0, n_both // 2, lambda jj, c: blk(join + 2 * jj + 1, blk(join + 2 * jj, c, None), None), carry)
        carry = lax.cond(n_both % 2 == 1, lambda c: blk(past - 1, c, None), lambda c: c, carry)
        for d in range(n_sub):
            carry = blk(past + d, carry, d)
        (m0, l0, a0), (m1, l1, a1) = carry
        o_ref[...] = jnp.concatenate([a0 / l0, a1 / l1], axis=0).T
        lse_ref[0, 0:1, :] = m0 + jnp.log(l0)
        lse_ref[0, 1:2, :] = m1 + jnp.log(l1)
        lse_ref[0, 2:8, :] = jnp.zeros((6, tq), F32)

    return pl.pallas_call(
        body, name="fox_fwd", grid=(FOX_H // 2, S // tq),
        in_specs=[pl.BlockSpec(memory_space=pltpu.SMEM), pl.BlockSpec((tq, 256), lambda p, i: (i, p)),
                  pl.BlockSpec((S, 256), lambda p, i: (0, p)), pl.BlockSpec((128, S), lambda p, i: (p, 0))],
        out_specs=[pl.BlockSpec((tq, 128), lambda p, i: (i, p)), pl.BlockSpec((1, 8, tq), lambda p, i: (p, 0, i))],
        out_shape=[jax.ShapeDtypeStruct((S, FOX_W), F32), jax.ShapeDtypeStruct((FOX_H // 2, 8, S), F32)],
        compiler_params=_params("arbitrary", "arbitrary"),
    )(first, qa, ka, vt)


def _fox_delta(d_o, o):
    S = o.shape[0]
    tm = _tile(S, 512)

    def body(d_ref, o_ref, db_ref, dbt_ref, dl_ref):
        d = d_ref[...]
        db_ref[...] = d.astype(BF16)
        dbt_ref[...] = d.T.astype(BF16)
        prod = d * o_ref[...]
        rr = lax.broadcasted_iota(jnp.int32, (8, 128), 0)
        cc = lax.broadcasted_iota(jnp.int32, (8, 128), 1)
        ind = jnp.where(jnp.right_shift(cc, 6) == rr, 1.0, 0.0).astype(F32)
        for p in range(FOX_H // 2):
            dl_ref[p] = lax.dot_general(ind, prod[:, 128 * p:128 * (p + 1)], (((1,), (1,)), ((), ())),
                                        preferred_element_type=F32, precision=HIGHEST)

    row = pl.BlockSpec((tm, FOX_W), lambda i: (i, 0))
    return pl.pallas_call(
        body, name="fox_delta", grid=(S // tm,),
        in_specs=[row, row],
        out_specs=[row, pl.BlockSpec((FOX_W, tm), lambda i: (0, i)), pl.BlockSpec((FOX_H // 2, 8, tm), lambda i: (0, 0, i))],
        out_shape=[jax.ShapeDtypeStruct((S, FOX_W), BF16), jax.ShapeDtypeStruct((FOX_W, S), BF16),
                   jax.ShapeDtypeStruct((FOX_H // 2, 8, S), F32)],
        compiler_params=_params("arbitrary"),
    )(d_o, o)


def _fox_bwd(qa, qat, ka, kat, pa, dob, dobt, lse, delta, narrow_end, wide_end):
    S = qa.shape[0]
    tk = _tile(S, FOX_TK)
    wide = _tile(S, FOX_BWD_WIDE)
    ratio = wide // tk
    n_wide = S // wide

    def body(ne_ref, we_ref, q_ref, qt_ref, k_ref, kt_ref, v_ref, do_ref, dot_ref, lse_ref, dl_ref, dq_ref, dk_ref, dv_ref):
        h, jb = pl.program_id(0), pl.program_id(1)
        hh = h % 2

        @pl.when(jb == 0)
        def _():
            dq_ref[...] = jnp.zeros_like(dq_ref)

        lane = lax.broadcasted_iota(jnp.int32, (tk, 128), 1)
        vm = jnp.where(jnp.right_shift(lane, 6) == hh, v_ref[...], jnp.zeros((), BF16))
        kb, ktb = k_ref[...], kt_ref[0:FOX_LIVE, :]
        mine = pl.ds(pl.multiple_of(hh * FOX_DH, FOX_DH), FOX_DH)

        def blk(ib, tq, carry, masked):
            dk, dv = carry
            qs = pl.ds(pl.multiple_of(ib * tq, tq), tq)
            p = jnp.exp(_nt(kb, q_ref[qs, :]) - lse_ref[0, pl.ds(hh, 1), qs])
            if masked:
                live = lax.broadcasted_iota(jnp.int32, p.shape, 1) >= lax.broadcasted_iota(jnp.int32, p.shape, 0)
                p = jnp.where(live, p, 0.0)
            ds = (p * (_nt(vm, do_ref[qs, :]) - dl_ref[0, pl.ds(hh, 1), qs])).astype(BF16)
            dq_ref[0:FOX_LIVE, qs] += _nn(ktb, ds)
            return dk + _nt(qt_ref[0:FOX_LIVE, qs], ds), dv + _nt(dot_ref[mine, qs], p.astype(BF16))

        carry = blk(jb, tk, (jnp.zeros((FOX_LIVE, tk), F32), jnp.zeros((FOX_DH, tk), F32)), True)
        first_wide = jb // ratio + 1
        carry = lax.fori_loop(jb + 1, ne_ref[h, jb], lambda ib, c: blk(ib, tk, c, False), carry)
        last_wide = we_ref[h, jb]
        rest = jnp.maximum(last_wide - first_wide, 0)
        carry = lax.fori_loop(0, rest // 2, lambda t, c: blk(first_wide + 2 * t + 1, wide, blk(first_wide + 2 * t, wide, c, False),
                                                             False), carry)
        dk, dv = lax.cond(rest % 2 == 1, lambda c: blk(last_wide - 1, wide, c, False), lambda c: c, carry)
        dk_ref[0:FOX_LIVE, :] = dk
        dk_ref[FOX_LIVE:, :] = jnp.zeros((128 - FOX_LIVE, tk), F32)
        dv_ref[...] = dv

    once = pl.Buffered(1)
    rows = pl.BlockSpec((1, 8, S), lambda h, j: (h // 2, 0, 0))
    return pl.pallas_call(
        body, name="fox_bwd", grid=(FOX_H, S // tk),
        in_specs=[pl.BlockSpec(memory_space=pltpu.SMEM), pl.BlockSpec(memory_space=pltpu.SMEM),
                  pl.BlockSpec((S, 128), lambda h, j: (0, h)), pl.BlockSpec((128, S), lambda h, j: (h, 0)),
                  pl.BlockSpec((tk, 128), lambda h, j: (j, h)), pl.BlockSpec((128, tk), lambda h, j: (h, j)),
                  pl.BlockSpec((tk, 128), lambda h, j: (j, A_FV // 128 + h // 2)),
                  pl.BlockSpec((S, 128), lambda h, j: (0, h // 2)), pl.BlockSpec((128, S), lambda h, j: (h // 2, 0)),
                  rows, rows],
        out_specs=[pl.BlockSpec((128, S), lambda h, j: (h, 0), pipeline_mode=once),
                   pl.BlockSpec((128, tk), lambda h, j: (h, j)), pl.BlockSpec((FOX_DH, tk), lambda h, j: (h, j))],
        out_shape=[jax.ShapeDtypeStruct((1024, S), F32), jax.ShapeDtypeStruct((1024, S), F32),
                   jax.ShapeDtypeStruct((FOX_W, S), F32)],
        compiler_params=_params("arbitrary", "arbitrary"),
    )(narrow_end, wide_end, qa, qat, ka, kat, pa, dob, dobt, lse, delta)


def _fox_post(dq, dk, dv, ps, bfg, d_proj):
    S = dq.shape[1]
    tm = _tile(S, 512)
    n_b = S // tm

    def body(dq_ref, dk_ref, dv_ref, ps_ref, b_ref, _, dp_ref, dff_ref, dbf_ref, carry):
        first = pl.program_id(0) == 0

        @pl.when(first)
        def _():
            carry[...] = jnp.zeros_like(carry)

        low = lax.broadcasted_iota(jnp.int32, (tm, 128), 1) < FOX_DH
        for h in range(FOX_H):
            blk = slice(128 * h, 128 * (h + 1))
            dp_ref[:, blk] = jnp.where(low, dq_ref[blk, :].T * FOX_SCALE, 0.0).astype(BF16)
            dp_ref[:, 1024 + 128 * h:1024 + 128 * (h + 1)] = jnp.where(low, dk_ref[blk, :].T, 0.0).astype(BF16)
        dp_ref[:, 2048:P_FOX_W] = dv_ref[...].T.astype(BF16)
        rr = lax.broadcasted_iota(jnp.int32, (FOX_H, 1024), 0)
        cc = lax.broadcasted_iota(jnp.int32, (FOX_H, 1024), 1)
        sel_k = jnp.where(cc == 128 * rr + AUG + 3, 1.0, 0.0).astype(F32)
        sel_q = jnp.where(cc == 128 * rr + AUG, 1.0, 0.0).astype(F32)
        g = (jnp.dot(sel_k, dk_ref[...], preferred_element_type=F32, precision=HIGHEST)
             - jnp.dot(sel_q, dq_ref[...], preferred_element_type=F32, precision=HIGHEST))
        t_from = lax.broadcasted_iota(jnp.int32, (tm, tm), 0)
        t_to = lax.broadcasted_iota(jnp.int32, (tm, tm), 1)
        later = jnp.where(t_from >= t_to, 1.0, 0.0).astype(F32)
        dlf = jnp.dot(-g, later, preferred_element_type=F32, precision=HIGHEST) + carry[:, 0:1]
        carry[...] = jnp.broadcast_to(dlf[:, 0:1], carry.shape)
        cols = jnp.concatenate([jnp.zeros((FF_LANE, tm), F32), dlf, jnp.zeros((128 - FF_LANE - FOX_H, tm), F32)], axis=0).T
        dff = cols * jax.nn.sigmoid(-(ps_ref[...] + b_ref[...]))
        dff_ref[...] = dff
        part = _sum8(dff)

        @pl.when(first)
        def _():
            dbf_ref[...] = part

        @pl.when(jnp.logical_not(first))
        def _():
            dbf_ref[...] += part

    rev = lambda i: (n_b - 1 - i, 0)
    tall = lambda n: pl.BlockSpec((n, tm), lambda i: (0, n_b - 1 - i))
    return pl.pallas_call(
        body, name="fox_post", grid=(n_b,),
        in_specs=[tall(1024), tall(1024), tall(FOX_W), pl.BlockSpec((tm, 128), lambda i: (n_b - 1 - i, 4)),
                  pl.BlockSpec((1, 128), lambda i: (0, 0)), pl.BlockSpec(memory_space=pl.ANY)],
        out_specs=[pl.BlockSpec((tm, P_FOX_W), lambda i: (n_b - 1 - i, P_FOX // P_FOX_W)), pl.BlockSpec((tm, 128), rev),
                   pl.BlockSpec((8, 128), lambda i: (0, 0))],
        out_shape=[jax.ShapeDtypeStruct((S, P_W), BF16), jax.ShapeDtypeStruct((S, 128), F32),
                   jax.ShapeDtypeStruct((8, 128), F32)],
        input_output_aliases={5: 0},
        scratch_shapes=[pltpu.VMEM((8, 128), F32)],
        compiler_params=_params("arbitrary"),
    )(dq, dk, dv, ps, bfg, d_proj)


def _mem_prep(mem, g_mem, wkv):
    def body(m_ref, g_ref, w_ref, mn_ref, kv_ref):
        r, xh = _rms(m_ref[...])
        mn = (xh * g_ref[...]).astype(BF16)
        mn_ref[...] = mn
        kv_ref[...] = _nn(mn, w_ref[...]).astype(BF16)

    return pl.pallas_call(
        body, name="mem_prep",
        out_shape=[jax.ShapeDtypeStruct((N_MEM, D), BF16), jax.ShapeDtypeStruct((N_MEM, 2 * MEM_W), BF16)],
        compiler_params=pltpu.CompilerParams(vmem_limit_bytes=V7X_VMEM_LIMIT),
    )(mem, g_mem, wkv)


def _mem_softmax(qh, kh):
    s = _nt(qh, kh) * MEM_SCALE
    e = jnp.exp(s - jnp.max(s, axis=-1, keepdims=True))
    return e / jnp.sum(e, axis=-1, keepdims=True)


def _mem_fwd(pa, mkv):
    S = pa.shape[0]
    tm = _tile(S, 512)

    def body(q_ref, kv_ref, o_ref):
        for h in range(MEM_H):
            cols = slice(MEM_DH * h, MEM_DH * (h + 1))
            p = _mem_softmax(q_ref[:, cols], kv_ref[:, cols])
            o_ref[:, cols] = _nn(p.astype(BF16), kv_ref[:, MEM_W + MEM_DH * h:MEM_W + MEM_DH * (h + 1)])

    return pl.pallas_call(
        body, name="mem_fwd", grid=(S // tm,),
        in_specs=[pl.BlockSpec((tm, MEM_W), lambda i: (i, A_MQ // MEM_W)), pl.BlockSpec((N_MEM, 2 * MEM_W), lambda i: (0, 0))],
        out_specs=pl.BlockSpec((tm, MEM_W), lambda i: (i, 0)),
        out_shape=jax.ShapeDtypeStruct((S, MEM_W), F32),
        compiler_params=_params("arbitrary"),
    )(pa, mkv)


def _mem_bwd(pa, mkv, d_o, d_proj):
    S = pa.shape[0]
    tm = _tile(S, 512)

    def body(q_ref, kv_ref, do_ref, _, dq_ref, dkv_ref):
        first = pl.program_id(0) == 0
        parts = []
        for h in range(MEM_H):
            cols = slice(MEM_DH * h, MEM_DH * (h + 1))
            vcols = slice(MEM_W + MEM_DH * h, MEM_W + MEM_DH * (h + 1))
            qh, kh = q_ref[:, cols], kv_ref[:, cols]
            p = _mem_softmax(qh, kh)
            dob = do_ref[:, cols].astype(BF16)
            dp = _nt(dob, kv_ref[:, vcols])
            ds = (p * (dp - jnp.sum(p * dp, axis=-1, keepdims=True)) * MEM_SCALE).astype(BF16)
            dq_ref[:, cols] = _nn(ds, kh).astype(BF16)
            parts.append((cols, _tn(ds, qh)))
            parts.append((vcols, _tn(p.astype(BF16), dob)))

        @pl.when(first)
        def _():
            for sl, v in parts:
                dkv_ref[:, sl] = v

        @pl.when(jnp.logical_not(first))
        def _():
            for sl, v in parts:
                dkv_ref[:, sl] += v

    return pl.pallas_call(
        body, name="mem_bwd", grid=(S // tm,),
        in_specs=[pl.BlockSpec((tm, MEM_W), lambda i: (i, A_MQ // MEM_W)), pl.BlockSpec((N_MEM, 2 * MEM_W), lambda i: (0, 0)),
                  pl.BlockSpec((tm, MEM_W), lambda i: (i, 0)), pl.BlockSpec(memory_space=pl.ANY)],
        out_specs=[pl.BlockSpec((tm, MEM_W), lambda i: (i, P_MQ // MEM_W)), pl.BlockSpec((N_MEM, 2 * MEM_W), lambda i: (0, 0))],
        out_shape=[jax.ShapeDtypeStruct((S, P_W), BF16), jax.ShapeDtypeStruct((N_MEM, 2 * MEM_W), F32)],
        input_output_aliases={3: 0},
        compiler_params=_params("arbitrary"),
    )(pa, mkv, d_o, d_proj)


def _mem_prep_bwd(mem, g_mem, mn, wkv, dkv):
    def body(m_ref, g_ref, mn_ref, w_ref, d_ref, dw_ref, dg_ref):
        db = d_ref[...].astype(BF16)
        dw_ref[...] = _tn(mn_ref[...], db).astype(BF16)
        r, xh = _rms(m_ref[...])
        dg_ref[...] = _sum8(_nt(db, w_ref[...]) * xh)

    dw, dg = pl.pallas_call(
        body, name="mem_prep_bwd",
        out_shape=[jax.ShapeDtypeStruct((D, 2 * MEM_W), BF16), jax.ShapeDtypeStruct((8, D), F32)],
        compiler_params=pltpu.CompilerParams(vmem_limit_bytes=V7X_VMEM_LIMIT),
    )(mem, g_mem, mn, wkv, dkv)
    return dw.reshape(N_DEV, D // N_DEV, 2 * MEM_W), dg


def _rearrange_w_in(w):
    def heads128(cols):
        blk = w[:, cols:cols + FOX_W].reshape(D, FOX_H, FOX_DH)
        return jnp.pad(blk, ((0, 0), (0, 0), (0, 128 - FOX_DH))).reshape(D, FOX_H * 128)

    fq, fk, fv, mq, wg = heads128(O_FQ), heads128(O_FK), w[:, O_FV:O_FF], w[:, O_MQ:O_GT], w[:, O_GT:]
    gaff = jnp.concatenate([w[:, O_GA:O_FQ], w[:, O_FF:O_MQ], jnp.zeros((D, 128 - GLA_R - FOX_H), w.dtype)], axis=1)
    wa = jnp.concatenate([w[:, O_GQ:O_GG], fq, fk, fv, mq], axis=1)
    ws = jnp.concatenate([w[:, O_GG:O_GA], gaff], axis=1)
    wp = jnp.concatenate([fq, fk, fv, mq, wg, w[:, O_GQ:O_GG], ws, jnp.zeros((D, P_W - P_GLA - 1024 - S_W), w.dtype)], axis=1)
    return wa, wg, ws, wp


def _restore_w_in_grad(dwp):
    def unheads(off):
        return dwp[:, off:off + FOX_H * 128].reshape(D, FOX_H, 128)[:, :, :FOX_DH].reshape(D, FOX_W)

    g0 = P_GLA + 1024
    return jnp.concatenate([
        dwp[:, P_GLA:g0], dwp[:, g0:g0 + 512], dwp[:, g0 + 512:g0 + 512 + GLA_R], unheads(P_FOX), unheads(P_FOX + 1024),
        dwp[:, P_FOX + 2048:P_FOX + P_FOX_W], dwp[:, g0 + 512 + GLA_R:g0 + 512 + GLA_R + FOX_H], dwp[:, P_MQ:P_GT],
        dwp[:, P_GT:P_GLA]], axis=1)


def _local_step(x, mem, target, p, late_shards):
    S = x.shape[0]
    p = dict(p)
    wa, wg, ws, wp = _rearrange_w_in(p["w_in"])
    wau = jnp.pad(p["w_alpha_up"], ((0, 128 - GLA_R), (0, 0)))
    bfg = jnp.pad(p["b_forget"], ((0, 0), (FF_LANE, 128 - FF_LANE - FOX_H)))
    gh = p["g_gla_head"].reshape(1, GLA_V)

    pa, pg, ps, u, gathered = _proj(x, p["g_mix"], wa, wg, ws, late_shards)
    p.update({n: _unslab(t, ax) for (n, ax), t in zip(BIG[1:], gathered)})
    o_gla, og, sprev = _gla_fwd(pa, ps, wau, p["b_alpha"], gh)
    qa, ka, qat, kat, vt, fox_stats = _fox_prep(pa, ps, bfg)
    fox_tk = _tile(S, FOX_TK)
    fox_first, fox_narrow_end, fox_wide_end = _fox_live_ranges(fox_stats, _tile(S, FOX_TQ) // fox_tk,
                                                               _tile(S, FOX_BWD_WIDE) // fox_tk)
    o_fox, lse = _fox_fwd(qa, ka, vt, fox_first)
    mn, mkv = _mem_prep(mem, p["g_mem"], p["w_mem_kv"])
    o_mem = _mem_fwd(pa, mkv)
    y3, mg = _merge(og, o_fox, o_mem, p["w_gla_o"], p["w_fox_o"], p["w_mem_o"], pg)
    h1, u2 = _out_proj(mg, p["w_out"], x, p["g_ffn"])
    a, act = _ff1(u2, p["w_ff1"])
    dh2, dh2b, loss8, dg_final = _ff2_loss(act, p["w_ff2"], h1, p["g_final"].reshape(1, D), target)

    d_a = _dact(dh2b, p["w_ff2"], a)
    dw_ff2 = _wgrad(act, dh2b, "wgrad_ff2", 0)
    dh1, dh1b, dg_ffn = _nt_rmsbwd(d_a, p["w_ff1"], h1, p["g_ffn"], dh2, "dffn", True)
    dw_ff1 = _wgrad(u2, d_a, "wgrad_ff1", 1)
    dy_g, dy_f, dy_m, do_g, do_f, do_m, d_proj = _dmerge(dh1b, p["w_out"], pg, y3, p["w_gla_o"], p["w_fox_o"], p["w_mem_o"])
    dw_out = _wgrad(mg, dh1b, "wgrad_out", 0)
    dw_gla_o = _wgrad(og, dy_g, "wgrad_gla_o", 1)
    dw_fox_o = _wgrad(o_fox, dy_f, "wgrad_fox_o", 1)
    dw_mem_o = _wgrad(o_mem, dy_m, "wgrad_mem_o", 1)
    d_proj, d_mkv = _mem_bwd(pa, mkv, do_m, d_proj)
    dw_mem_kv, dg_mem = _mem_prep_bwd(mem, p["g_mem"], mn, p["w_mem_kv"], d_mkv)
    dob, dobt, delta = _fox_delta(do_f, o_fox)
    dq, dk, dv = _fox_bwd(qa, qat, ka, kat, pa, dob, dobt, lse, delta, fox_narrow_end, fox_wide_end)
    d_proj, dgaff_fox, db_forget = _fox_post(dq, dk, dv, ps, bfg, d_proj)
    ready = dict(w_mem_kv=dw_mem_kv, w_gla_o=dw_gla_o, w_fox_o=dw_fox_o, w_mem_o=dw_mem_o, w_out=dw_out, w_ff1=dw_ff1,
                 w_ff2=dw_ff2)
    d_proj, dw_au, db_alpha, dg_gla, arrived = _gla_bwd(pa, ps, wau, p["b_alpha"], gh, o_gla, do_g, sprev, dgaff_fox, d_proj,
                                                        [ready[n] for n, _ in BIG[1:]])
    dw_in = _slabs(_restore_w_in_grad(_wgrad(u, d_proj, "wgrad_in")), 1).astype(BF16)
    dx, dg_mix, arrived_in = _nt_rmsbwd(d_proj, wp, x, p["g_mix"], dh1, "dmix", False, [dw_in])

    big = dict(zip([n for n, _ in BIG], [arrived_in[0], *arrived]))
    small = dict(g_mix=dg_mix, g_mem=dg_mem, g_ffn=dg_ffn, g_final=dg_final, b_alpha=db_alpha, g_gla_head=dg_gla,
                 b_forget=db_forget, w_alpha_up=dw_au, loss=loss8)
    return dx, big, small


BIG = (("w_in", 1), ("w_mem_kv", 0), ("w_gla_o", 1), ("w_fox_o", 1), ("w_mem_o", 1), ("w_out", 0), ("w_ff1", 1), ("w_ff2", 0))


def _peer(d):
    me = lax.axis_index("x") * 4 + lax.axis_index("y") * 2 + lax.axis_index("c")
    t = (me + d) % N_DEV
    return (t // 4, (t // 2) % 2, t % 2), me


def _exchange_sems(n):
    return [pltpu.SemaphoreType.DMA((n, N_DEV - 1)), pltpu.SemaphoreType.DMA((n, N_DEV - 1)), pltpu.SemaphoreType.DMA((n,))]


def _exchange_call(body, blocks, out_shape, name):
    n = len(blocks)
    any_spec = pl.BlockSpec(memory_space=pl.ANY)
    return pl.pallas_call(body, name=name, in_specs=[any_spec] * n, out_specs=[any_spec] * n, out_shape=out_shape,
                          scratch_shapes=_exchange_sems(n))(*blocks)


class _AllToAll:
    def __init__(self, ins, outs, sems, gather):
        send, recv, loc = sems
        n = len(ins)
        _, me = _peer(0)
        src = (lambda k, j: ins[k]) if gather else (lambda k, j: ins[k].at[j])
        self.local = [pltpu.make_async_copy(src(k, me), outs[k].at[me], loc.at[k]) for k in range(n)]
        self.remote = []
        for d in range(1, N_DEV):
            to, _ = _peer(d)
            self.remote += [pltpu.make_async_remote_copy(
                src_ref=src(k, (me + d) % N_DEV), dst_ref=outs[k].at[me], send_sem=send.at[k, d - 1],
                recv_sem=recv.at[k, d - 1], device_id=to, device_id_type=MESH) for k in range(n)]

    def start(self):
        for cp in self.local + self.remote:
            cp.start()

    def wait(self):
        for cp in self.remote:
            cp.wait_send()
        for cp in self.remote:
            cp.wait_recv()
        for cp in self.local:
            cp.wait()


def _gathered_shapes(shards):
    return [jax.ShapeDtypeStruct((N_DEV,) + b.shape, b.dtype) for b in shards]


def _gather_weights(shards):
    n = len(shards)

    def body(*refs):
        ins, outs = refs[:n], refs[n:2 * n]
        send, recv, loc = refs[2 * n:]
        x, y, c = lax.axis_index("x"), lax.axis_index("y"), lax.axis_index("c")
        sibling = (x, y, 1 - c)
        chips = [(1 - x, y), (x, 1 - y), (1 - x, 1 - y)]
        slot = lambda px, py, pc: px * 4 + py * 2 + pc

        def copy(k, s, block, to, src=None):
            rows = outs[k].at[slot(*block)]
            return pltpu.make_async_remote_copy(src_ref=rows if src is None else src, dst_ref=rows, send_sem=send.at[k, s],
                                                recv_sem=recv.at[k, s], device_id=to, device_id_type=MESH)

        me = (x, y, c)
        own = [pltpu.make_async_copy(ins[k], outs[k].at[slot(*me)], loc.at[k]) for k in range(n)]
        first = [copy(k, 0, me, sibling, src=ins[k]) for k in range(n)]
        first += [copy(k, 1 + j, me, (*chip, c), src=ins[k]) for j, chip in enumerate(chips) for k in range(n)]
        for cp in own + first:
            cp.start()
        passed = []
        for j, chip in enumerate(chips):
            for k in range(n):
                copy(k, 1 + j, (*chip, c), me).wait_recv()
                fwd = copy(k, 4 + j, (*chip, c), sibling)
                fwd.start()
                passed.append(fwd)
        for k in range(n):
            copy(k, 0, sibling, me).wait_recv()
        for j, chip in enumerate(chips):
            for k in range(n):
                copy(k, 4 + j, (*chip, 1 - c), me).wait_recv()
        for cp in first + passed:
            cp.wait_send()
        for cp in own:
            cp.wait()

    return _exchange_call(body, shards, [jax.ShapeDtypeStruct((N_DEV,) + b.shape, b.dtype) for b in shards], "gather_weights")


def _adamw_math(g, w, m, v):
    m2 = ADAM_B1 * m + (1.0 - ADAM_B1) * g
    v2 = ADAM_B2 * v + (1.0 - ADAM_B2) * jnp.square(g)
    m_hat = m2 / (1.0 - ADAM_B1 ** ADAM_STEP)
    v_hat = v2 / (1.0 - ADAM_B2 ** ADAM_STEP)
    delta = -ADAM_LR * (m_hat / (jnp.sqrt(v_hat) + ADAM_EPS) + ADAM_WD * w)
    return delta, m2, v2


def _adamw_sum(parts, w, m, v, name):
    R, C = w.shape
    tr = _tile(R, 128)

    def body(p_ref, w_ref, m_ref, v_ref, g_ref, d_ref, m2_ref, v2_ref):
        g = p_ref[0].astype(F32)
        for j in range(1, p_ref.shape[0]):
            g = g + p_ref[j].astype(F32)
        g_ref[...] = g
        d_ref[...], m2_ref[...], v2_ref[...] = _adamw_math(g, w_ref[...], m_ref[...], v_ref[...])

    blk = pl.BlockSpec((tr, C), lambda i: (i, 0))
    return pl.pallas_call(
        body, name=name, grid=(R // tr,),
        in_specs=[pl.BlockSpec((parts.shape[0], tr, C), lambda i: (0, i, 0)), blk, blk, blk],
        out_specs=[blk] * 4, out_shape=[jax.ShapeDtypeStruct((R, C), F32)] * 4,
        compiler_params=_params("arbitrary"),
    )(parts, w, m, v)


SMALL_ROWS = 24


def _pack_small(d):
    mixed = jnp.concatenate([d["b_alpha"].reshape(1, GLA_K), d["g_gla_head"].reshape(1, GLA_V),
                             jnp.pad(d["b_forget"].reshape(1, FOX_H), ((0, 0), (FF_LANE, 128 - FF_LANE - FOX_H))),
                             jnp.zeros((1, 128), F32)], axis=1)
    rows = [d["g_mix"].reshape(1, D), d["g_mem"].reshape(1, D), d["g_ffn"].reshape(1, D), d["g_final"].reshape(1, D), mixed,
            jnp.zeros((3, D), F32), jnp.pad(d["w_alpha_up"].reshape(GLA_R, GLA_K), ((0, 0), (0, D - GLA_K)))]
    return jnp.concatenate(rows, axis=0)


def _unpack_small(t):
    return dict(g_mix=t[0:1], g_mem=t[1:2], g_ffn=t[2:3], g_final=t[3], b_alpha=t[4:5, 0:GLA_K],
                g_gla_head=t[4:5, GLA_K:GLA_K + GLA_V].reshape(1, GLA_H, GLA_DV),
                b_forget=t[4:5, 768 + FF_LANE:768 + FF_LANE + FOX_H], w_alpha_up=t[8:24, 0:GLA_K].reshape(1, GLA_R, GLA_K))


def _small_allreduce(small, w, m, v):
    def body(gm, gme, gf, gfi, ba, gg, bf, wau, ls, w_ref, m_ref, v_ref, g_ref, d_ref, m2_ref, v2_ref, l_ref,
             buf, send, recv):
        _, me = _peer(0)
        buf[me] = jnp.zeros((SMALL_ROWS, D), F32)
        for r, ref in enumerate((gm, gme, gf, gfi)):
            buf[me, r:r + 1, :] = jnp.sum(ref[...], axis=0, keepdims=True)
        buf[me, 4:5, 0:GLA_K] = jnp.sum(ba[...], axis=0, keepdims=True)
        buf[me, 4:5, GLA_K:GLA_K + GLA_V] = jnp.sum(gg[...], axis=0, keepdims=True)
        buf[me, 4:5, 768:896] = jnp.sum(bf[...], axis=0, keepdims=True)
        lrow = jnp.sum(ls[...], axis=0, keepdims=True)
        lsum = lrow[:, 0:128]
        for c in range(1, D // 128):
            lsum = lsum + lrow[:, 128 * c:128 * (c + 1)]
        buf[me, 4:5, 896:1024] = lsum
        buf[me, 8:24, 0:GLA_K] = wau[0:GLA_R, :]
        remote = []
        for d in range(1, N_DEV):
            to, me = _peer(d)
            cp = pltpu.make_async_remote_copy(src_ref=buf.at[me], dst_ref=buf.at[me], send_sem=send.at[d - 1],
                                              recv_sem=recv.at[d - 1], device_id=to, device_id_type=MESH)
            cp.start()
            remote.append(cp)
        for cp in remote:
            cp.wait_send()
        for cp in remote:
            cp.wait_recv()
        g = buf[0]
        for j in range(1, N_DEV):
            g = g + buf[j]
        g_ref[...] = g
        d_ref[...], m2_ref[...], v2_ref[...] = _adamw_math(g, w_ref[...], m_ref[...], v_ref[...])
        l_ref[...] = g[4:5, 896:1024]

    packed = jax.ShapeDtypeStruct((SMALL_ROWS, D), F32)
    return pl.pallas_call(
        body, name="small_allreduce",
        out_shape=[packed, packed, packed, packed, jax.ShapeDtypeStruct((1, 128), F32)],
        scratch_shapes=[pltpu.VMEM((N_DEV, SMALL_ROWS, D), F32), pltpu.SemaphoreType.DMA((N_DEV - 1,)),
                        pltpu.SemaphoreType.DMA((N_DEV - 1,))],
    )(small["g_mix"], small["g_mem"], small["g_ffn"], small["g_final"], small["b_alpha"], small["g_gla_head"],
      small["b_forget"], small["w_alpha_up"], small["loss"], w, m, v)


def _slabs(g, axis):
    R, C = g.shape
    if axis == 0:
        return g.reshape(N_DEV, R // N_DEV, C)
    return g.reshape(R, N_DEV, C // N_DEV).transpose(1, 0, 2)


def _unslab(t, axis):
    n, r, c = t.shape
    if axis == 0:
        return t.reshape(n * r, c)
    return t.transpose(1, 0, 2).reshape(r, n * c)


def kernel(x, mem, g_mix, w_in, w_alpha_up, b_alpha, b_forget, g_gla_head, g_mem, w_mem_kv, w_gla_o, w_fox_o, w_mem_o, w_out, g_ffn, w_ff1, w_ff2, g_final, loss_target, m_g_mix, m_w_in, m_w_alpha_up, m_b_alpha, m_b_forget, m_g_gla_head, m_g_mem, m_w_mem_kv, m_w_gla_o, m_w_fox_o, m_w_mem_o, m_w_out, m_g_ffn, m_w_ff1, m_w_ff2, m_g_final, v_g_mix, v_w_in, v_w_alpha_up, v_b_alpha, v_b_forget, v_g_gla_head, v_g_mem, v_w_mem_kv, v_w_gla_o, v_w_fox_o, v_w_mem_o, v_w_out, v_g_ffn, v_w_ff1, v_w_ff2, v_g_final):
    names = ["g_mix", "w_in", "w_alpha_up", "b_alpha", "b_forget", "g_gla_head", "g_mem", "w_mem_kv", "w_gla_o", "w_fox_o",
             "w_mem_o", "w_out", "g_ffn", "w_ff1", "w_ff2", "g_final"]
    w = dict(g_mix=g_mix, w_in=w_in, w_alpha_up=w_alpha_up, b_alpha=b_alpha, b_forget=b_forget, g_gla_head=g_gla_head,
             g_mem=g_mem, w_mem_kv=w_mem_kv, w_gla_o=w_gla_o, w_fox_o=w_fox_o, w_mem_o=w_mem_o, w_out=w_out, g_ffn=g_ffn,
             w_ff1=w_ff1, w_ff2=w_ff2, g_final=g_final)
    m = dict(g_mix=m_g_mix, w_in=m_w_in, w_alpha_up=m_w_alpha_up, b_alpha=m_b_alpha, b_forget=m_b_forget,
             g_gla_head=m_g_gla_head, g_mem=m_g_mem, w_mem_kv=m_w_mem_kv, w_gla_o=m_w_gla_o, w_fox_o=m_w_fox_o,
             w_mem_o=m_w_mem_o, w_out=m_w_out, g_ffn=m_g_ffn, w_ff1=m_w_ff1, w_ff2=m_w_ff2, g_final=m_g_final)
    v = dict(g_mix=v_g_mix, w_in=v_w_in, w_alpha_up=v_w_alpha_up, b_alpha=v_b_alpha, b_forget=v_b_forget,
             g_gla_head=v_g_gla_head, g_mem=v_g_mem, w_mem_kv=v_w_mem_kv, w_gla_o=v_w_gla_o, w_fox_o=v_w_fox_o,
             w_mem_o=v_w_mem_o, w_out=v_w_out, g_ffn=v_g_ffn, w_ff1=v_w_ff1, w_ff2=v_w_ff2, g_final=v_g_final)
    me = lax.axis_index("x") * 4 + lax.axis_index("y") * 2 + lax.axis_index("c")

    shard = lambda n: w[n][0].astype(BF16)
    w_in_all, w_au_all = _gather_weights([shard("w_in"), shard("w_alpha_up")])
    p = dict(w_in=_unslab(w_in_all, 1), w_alpha_up=_unslab(w_au_all, 1), g_mix=g_mix, b_alpha=b_alpha, b_forget=b_forget,
             g_gla_head=g_gla_head, g_mem=g_mem, g_ffn=g_ffn, g_final=g_final)

    dx, big, small = _local_step(x[0], mem[0], loss_target[0], p, [shard(n) for n, _ in BIG[1:]])

    out_g, out_d, out_m, out_v = {}, {}, {}, {}
    for n, _ in BIG:
        g_, d_, m_, v_ = _adamw_sum(big[n], w[n][0], m[n][0], v[n][0], "adamw_" + n)
        out_g[n], out_d[n], out_m[n], out_v[n] = g_[None], d_[None], m_[None], v_[None]

    full = lambda d: dict(d, w_alpha_up=jnp.zeros((1, GLA_R, GLA_K), F32))
    gs, ds, ms, vs, lrow = _small_allreduce(small, _pack_small(full(w)), _pack_small(full(m)), _pack_small(full(v)))
    g_s, d_s, m_s, v_s = _unpack_small(gs), _unpack_small(ds), _unpack_small(ms), _unpack_small(vs)
    for n in names:
        if n not in out_g and n != "w_alpha_up":
            out_g[n], out_d[n], out_m[n], out_v[n] = g_s[n], d_s[n], m_s[n], v_s[n]
    g_au = lax.dynamic_slice_in_dim(g_s["w_alpha_up"][0], me * (GLA_K // N_DEV), GLA_K // N_DEV, axis=1)
    g_, d_, m_, v_ = _adamw_sum(g_au[None], w_alpha_up[0], m_w_alpha_up[0], v_w_alpha_up[0], "adamw_w_alpha_up")
    out_g["w_alpha_up"], out_d["w_alpha_up"], out_m["w_alpha_up"], out_v["w_alpha_up"] = g_[None], d_[None], m_[None], v_[None]

    loss = jnp.sum(lrow) * (0.5 / D)
    return (loss, dx[None], *[out_g[n] for n in names], *[out_d[n] for n in names], *[out_m[n] for n in names],
            *[out_v[n] for n in names])
```

```python
import jax
import jax.numpy as jnp
from jax import lax
from jax.experimental import pallas as pl
from jax.experimental.pallas import tpu as pltpu

F32, BF16 = jnp.float32, jnp.bfloat16
HIGHEST = lax.Precision.HIGHEST
MESH = pl.DeviceIdType.MESH

N_DEV = 8
D = 1024
EPS = 1e-6
CHUNK = 64
N_MEM = 256
GLA_H, GLA_DK, GLA_DV = 4, 64, 128
GLA_K, GLA_V, GLA_R = 256, 512, 16
FOX_H, FOX_DH, FOX_W = 8, 64, 512
MEM_H, MEM_DH, MEM_W = 4, 128, 512
D_FF = 4096
D_IN = 6680
FOX_SCALE = 0.125
GLA_SCALE = 0.125
MEM_SCALE = MEM_DH ** -0.5
GLA_TAU_INV = 1.0 / 16.0
NEG = -1e30

O_GQ, O_GK, O_GV, O_GG, O_GA, O_FQ, O_FK, O_FV, O_FF, O_MQ, O_GT = 0, 256, 512, 1024, 1536, 1552, 2064, 2576, 3088, 3096, 3608
A_FQ, A_FK, A_FV, A_MQ, A_W = 1024, 2048, 3072, 3584, 4096
S_W = 640
G_W = 3072
P_FOX, P_FOX_W, P_MQ, P_GT, P_GLA, P_GLA_W, P_W = 0, 2560, 2560, 3072, 6144, 2048, 8192
FF_LANE = 16
AUG = 64
FOX_LIVE = 80

ADAM_LR, ADAM_B1, ADAM_B2, ADAM_EPS, ADAM_WD, ADAM_STEP = 0.001, 0.9, 0.999, 1e-08, 0.01, 10
V7X_VMEM_LIMIT = 52 * 1024 * 1024
FOX_TK = 512
FOX_TQ = 2048
FOX_BWD_WIDE = 1024


def _params(*sem):
    return pltpu.CompilerParams(dimension_semantics=sem, vmem_limit_bytes=V7X_VMEM_LIMIT)


def _nt(a, b):
    return lax.dot_general(a, b, (((1,), (1,)), ((), ())), preferred_element_type=F32)


def _tn(a, b):
    return lax.dot_general(a, b, (((0,), (0,)), ((), ())), preferred_element_type=F32)


def _nn(a, b):
    return jnp.dot(a, b, preferred_element_type=F32)


def _log_sigmoid(z):
    return jnp.minimum(z, 0.0) - jnp.log(1.0 + jnp.exp(-jnp.abs(z)))


def _sum01(m01, x):
    x1 = x.astype(BF16)
    x2 = (x - x1.astype(F32)).astype(BF16)
    x3 = (x - x1.astype(F32) - x2.astype(F32)).astype(BF16)
    return _nn(m01, x1) + _nn(m01, x2) + _nn(m01, x3)


def _sum8(x):
    return x.reshape(x.shape[0] // 8, 8, x.shape[1]).sum(axis=0)


def _rms(xv):
    r = lax.rsqrt(jnp.mean(xv * xv, axis=-1, keepdims=True) + EPS)
    return r, xv * r


def _rms_bwd(du, g, r, xh):
    w = du * g
    return r * (w - xh * jnp.mean(w * xh, axis=-1, keepdims=True))


def _row_chunks(n, size=256):
    return [slice(r, r + min(size, n)) for r in range(0, n, min(size, n))]


def _tile(n, pref):
    t = min(n, pref)
    assert n % t == 0, (n, t)
    return t


def _proj(x, g, wa, wg, ws, shards):
    S = x.shape[0]
    tm, tn = _tile(S, 1024), 1024
    n_a, n_g = A_W // tn, G_W // tn
    n_i, n_j = S // tm, n_a + n_g + 1
    n_x = len(shards)

    def body(*refs):
        x_ref, g_ref, wa_ref, wg_ref, ws_ref = refs[:5]
        pa_ref, pg_ref, ps_ref, u_ref = refs[5 + n_x:9 + n_x]
        u_s = refs[9 + 2 * n_x]
        gather = lambda: _AllToAll(refs[5:5 + n_x], refs[9 + n_x:9 + 2 * n_x], refs[10 + 2 * n_x:], True)
        i, j = pl.program_id(0), pl.program_id(1)

        @pl.when((i == 0) & (j == 0))
        def _():
            gather().start()

        @pl.when(j == 0)
        def _():
            r, xh = _rms(x_ref[...])
            u_s[...] = (xh * g_ref[...]).astype(BF16)
            u_ref[...] = u_s[...]

        @pl.when(j < n_a)
        def _():
            pa_ref[...] = _nn(u_s[...], wa_ref[...]).astype(BF16)

        @pl.when((j >= n_a) & (j < n_a + n_g))
        def _():
            pg_ref[...] = _nn(u_s[...], wg_ref[...]).astype(BF16)

        @pl.when(j == n_a + n_g)
        def _():
            ps_ref[...] = _nn(u_s[...], ws_ref[...])

        @pl.when((i == n_i - 1) & (j == n_j - 1))
        def _():
            gather().wait()

    in_a = lambda j: jnp.minimum(j, n_a - 1)
    in_g = lambda j: jnp.clip(j - n_a, 0, n_g - 1)
    row = pl.BlockSpec((tm, D), lambda i, j: (i, 0))
    any_spec = pl.BlockSpec(memory_space=pl.ANY)
    out = pl.pallas_call(
        body, name="proj", grid=(n_i, n_j),
        in_specs=[row, pl.BlockSpec((1, D), lambda i, j: (0, 0)), pl.BlockSpec((D, tn), lambda i, j: (0, in_a(j))),
                  pl.BlockSpec((D, tn), lambda i, j: (0, in_g(j))),
                  pl.BlockSpec((D, S_W), lambda i, j: (0, 0), pipeline_mode=pl.Buffered(1))] + [any_spec] * n_x,
        out_specs=[pl.BlockSpec((tm, tn), lambda i, j: (i, in_a(j))), pl.BlockSpec((tm, tn), lambda i, j: (i, in_g(j))),
                   pl.BlockSpec((tm, S_W), lambda i, j: (i, 0)), row] + [any_spec] * n_x,
        out_shape=[jax.ShapeDtypeStruct((S, A_W), BF16), jax.ShapeDtypeStruct((S, G_W), BF16),
                   jax.ShapeDtypeStruct((S, S_W), F32), jax.ShapeDtypeStruct((S, D), BF16)] + _gathered_shapes(shards),
        scratch_shapes=[pltpu.VMEM((tm, D), BF16)] + _exchange_sems(n_x),
        compiler_params=_params("arbitrary", "arbitrary"),
    )(x, g, wa, wg, ws, *shards)
    return out[0], out[1], out[2], out[3], out[4:]


def _wgrad(a, b, name, slab_axis=None):
    S, Ka = a.shape
    N = b.shape[1]
    tka, tn, ts = _tile(Ka, 1024), _tile(N, 1024), _tile(S, 1024)
    n_s = S // ts
    per = N // N_DEV
    slabs_per_step = tn // per

    def body(a_ref, b_ref, o_ref, acc):
        s = pl.program_id(2)

        @pl.when(s == 0)
        def _():
            acc[...] = jnp.zeros_like(acc)

        acc[...] += _tn(a_ref[...].astype(BF16), b_ref[...].astype(BF16))

        @pl.when(s == n_s - 1)
        def _():
            if slab_axis == 1:
                for q in range(slabs_per_step):
                    o_ref[q] = acc[:, per * q:per * (q + 1)].astype(BF16)
            else:
                o_ref[...] = acc[...].astype(o_ref.dtype)

    if slab_axis == 1:
        out_spec = pl.BlockSpec((slabs_per_step, tka, per), lambda i, j, s: (j, i, 0))
        out_shape = jax.ShapeDtypeStruct((N_DEV, Ka, per), BF16)
    else:
        out_spec = pl.BlockSpec((tka, tn), lambda i, j, s: (i, j))
        out_shape = jax.ShapeDtypeStruct((Ka, N), F32 if slab_axis is None else BF16)
    out = pl.pallas_call(
        body, name=name, grid=(Ka // tka, N // tn, n_s),
        in_specs=[pl.BlockSpec((ts, tka), lambda i, j, s: (s, i)), pl.BlockSpec((ts, tn), lambda i, j, s: (s, j))],
        out_specs=out_spec, out_shape=out_shape,
        scratch_shapes=[pltpu.VMEM((tka, tn), F32)],
        compiler_params=_params("arbitrary", "arbitrary", "arbitrary"),
    )(a, b)
    return out.reshape(N_DEV, Ka // N_DEV, N) if slab_axis == 0 else out


def _nt_rmsbwd(a, w, xin, g, dres, name, emit_bf16, slabs=()):
    S, K = a.shape
    tm, tk = _tile(S, 1024), _tile(K, 1024)
    n_i, n_k = S // tm, K // tk
    n_x, n_o = len(slabs), 3 if emit_bf16 else 2

    def body(*refs):
        a_ref, w_ref, x_ref, g_ref, r_ref = refs[:5]
        o_ref = refs[5 + n_x]
        rest = refs[6 + n_x:5 + n_x + n_o] + (refs[5 + 2 * n_x + n_o],)
        dg_ref, acc = rest[-2], rest[-1]
        scatter = lambda: _AllToAll(refs[5:5 + n_x], refs[5 + n_x + n_o:5 + 2 * n_x + n_o], refs[6 + 2 * n_x + n_o:], False)
        i, k = pl.program_id(0), pl.program_id(1)

        if n_x:
            @pl.when((i == 0) & (k == 0))
            def _():
                scatter().start()

        @pl.when(k == 0)
        def _():
            acc[...] = jnp.zeros_like(acc)

        acc[...] += _nt(a_ref[...], w_ref[...])

        @pl.when(k == n_k - 1)
        def _():
            @pl.when(i == 0)
            def _():
                dg_ref[...] = jnp.zeros_like(dg_ref)

            for rows in _row_chunks(tm):
                du = acc[rows, :]
                r, xh = _rms(x_ref[rows, :])
                out = r_ref[rows, :] + _rms_bwd(du, g_ref[...], r, xh)
                o_ref[rows, :] = out
                if emit_bf16:
                    rest[0][rows, :] = out.astype(BF16)
                dg_ref[...] += _sum8(du * xh)

        if n_x:
            @pl.when((i == n_i - 1) & (k == n_k - 1))
            def _():
                scatter().wait()

    row = pl.BlockSpec((tm, D), lambda i, k: (i, 0))
    any_spec = pl.BlockSpec(memory_space=pl.ANY)
    out_shape = [jax.ShapeDtypeStruct((S, D), F32)]
    out_specs = [row]
    if emit_bf16:
        out_shape.append(jax.ShapeDtypeStruct((S, D), BF16))
        out_specs.append(row)
    out_shape.append(jax.ShapeDtypeStruct((8, D), F32))
    out_specs.append(pl.BlockSpec((8, D), lambda i, k: (0, 0)))
    out = pl.pallas_call(
        body, name=name, grid=(n_i, n_k),
        in_specs=[pl.BlockSpec((tm, tk), lambda i, k: (i, k)), pl.BlockSpec((D, tk), lambda i, k: (0, k)),
                  row, pl.BlockSpec((1, D), lambda i, k: (0, 0)), row] + [any_spec] * n_x,
        out_specs=out_specs + [any_spec] * n_x,
        out_shape=out_shape + [jax.ShapeDtypeStruct(b.shape, b.dtype) for b in slabs],
        scratch_shapes=[pltpu.VMEM((tm, D), F32)] + (_exchange_sems(n_x) if n_x else []),
        compiler_params=_params("arbitrary", "arbitrary"),
    )(a, w, xin, g, dres, *slabs)
    return (*out[:n_o], out[n_o:]) if n_x else out


def _merge(og, ofox, omem, wg, wf, wm, pg):
    S = og.shape[0]
    tm = _tile(S, 512)

    def body(og_ref, of_ref, om_ref, wg_ref, wf_ref, wm_ref, pg_ref, y_ref, mg_ref):
        tot = None
        for i, (o_ref, w_ref) in enumerate(((og_ref, wg_ref), (of_ref, wf_ref), (om_ref, wm_ref))):
            y = _nn(o_ref[...].astype(BF16), w_ref[...])
            y_ref[i] = y.astype(BF16)
            t = jax.nn.sigmoid(pg_ref[:, D * i:D * (i + 1)].astype(F32)) * y
            tot = t if tot is None else tot + t
        mg_ref[...] = tot.astype(BF16)

    o_spec = pl.BlockSpec((tm, 512), lambda i: (i, 0))
    w_spec = pl.BlockSpec((512, D), lambda i: (0, 0))
    return pl.pallas_call(
        body, name="merge", grid=(S // tm,),
        in_specs=[o_spec, o_spec, o_spec, w_spec, w_spec, w_spec, pl.BlockSpec((tm, G_W), lambda i: (i, 0))],
        out_specs=[pl.BlockSpec((3, tm, D), lambda i: (0, i, 0)), pl.BlockSpec((tm, D), lambda i: (i, 0))],
        out_shape=[jax.ShapeDtypeStruct((3, S, D), BF16), jax.ShapeDtypeStruct((S, D), BF16)],
        compiler_params=_params("arbitrary"),
    )(og, ofox, omem, wg, wf, wm, pg)


def _out_proj(mg, w_out, x, g_ffn):
    S = x.shape[0]
    tm = _tile(S, 512)

    def body(mg_ref, w_ref, x_ref, g_ref, h_ref, u_ref):
        h = x_ref[...] + _nn(mg_ref[...], w_ref[...])
        h_ref[...] = h
        r, xh = _rms(h)
        u_ref[...] = (xh * g_ref[...]).astype(BF16)

    row = pl.BlockSpec((tm, D), lambda i: (i, 0))
    return pl.pallas_call(
        body, name="out_proj", grid=(S // tm,),
        in_specs=[row, pl.BlockSpec((D, D), lambda i: (0, 0)), row, pl.BlockSpec((1, D), lambda i: (0, 0))],
        out_specs=[row, row],
        out_shape=[jax.ShapeDtypeStruct((S, D), F32), jax.ShapeDtypeStruct((S, D), BF16)],
        compiler_params=_params("arbitrary"),
    )(mg, w_out, x, g_ffn)


def _ff1(u2, w1):
    S = u2.shape[0]
    tm, tn = _tile(S, 1024), 1024

    def body(u_ref, w_ref, a_ref, act_ref):
        a = _nn(u_ref[...], w_ref[...])
        a_ref[...] = a.astype(BF16)
        act_ref[...] = jnp.square(jnp.maximum(a, 0.0)).astype(BF16)

    blk = pl.BlockSpec((tm, tn), lambda i, j: (i, j))
    return pl.pallas_call(
        body, name="ff1", grid=(S // tm, D_FF // tn),
        in_specs=[pl.BlockSpec((tm, D), lambda i, j: (i, 0)), pl.BlockSpec((D, tn), lambda i, j: (0, j))],
        out_specs=[blk, blk],
        out_shape=[jax.ShapeDtypeStruct((S, D_FF), BF16), jax.ShapeDtypeStruct((S, D_FF), BF16)],
        compiler_params=_params("arbitrary", "arbitrary"),
    )(u2, w1)


def _ff2_loss(act, w2, h1, g_final, target):
    S = act.shape[0]
    tm, tk = _tile(S, 1024), 1024
    n_k = D_FF // tk

    def body(a_ref, w_ref, h_ref, g_ref, t_ref, d_ref, db_ref, ls_ref, dg_ref, acc):
        i, k = pl.program_id(0), pl.program_id(1)

        @pl.when(k == 0)
        def _():
            acc[...] = jnp.zeros_like(acc)

        acc[...] += _nn(a_ref[...], w_ref[...])

        @pl.when(k == n_k - 1)
        def _():
            @pl.when(i == 0)
            def _():
                ls_ref[...] = jnp.zeros_like(ls_ref)
                dg_ref[...] = jnp.zeros_like(dg_ref)

            gf = g_ref[...]
            for rows in _row_chunks(tm):
                r, xh = _rms(h_ref[rows, :] + acc[rows, :])
                err = xh * gf - t_ref[rows, :]
                dy = err * (1.0 / D)
                dh = _rms_bwd(dy, gf, r, xh)
                d_ref[rows, :] = dh
                db_ref[rows, :] = dh.astype(BF16)
                ls_ref[...] += _sum8(err * err)
                dg_ref[...] += _sum8(dy * xh)

    row = pl.BlockSpec((tm, D), lambda i, k: (i, 0))
    part = pl.BlockSpec((8, D), lambda i, k: (0, 0))
    return pl.pallas_call(
        body, name="ff2_loss", grid=(S // tm, n_k),
        in_specs=[pl.BlockSpec((tm, tk), lambda i, k: (i, k)), pl.BlockSpec((tk, D), lambda i, k: (k, 0)),
                  row, pl.BlockSpec((1, D), lambda i, k: (0, 0)), row],
        out_specs=[row, row, part, part],
        out_shape=[jax.ShapeDtypeStruct((S, D), F32), jax.ShapeDtypeStruct((S, D), BF16),
                   jax.ShapeDtypeStruct((8, D), F32), jax.ShapeDtypeStruct((8, D), F32)],
        scratch_shapes=[pltpu.VMEM((tm, D), F32)],
        compiler_params=_params("arbitrary", "arbitrary"),
    )(act, w2, h1, g_final, target)


def _dact(dh2b, w2, a):
    S = a.shape[0]
    tm, tn = _tile(S, 1024), 1024

    def body(d_ref, w_ref, a_ref, o_ref):
        da = _nt(d_ref[...], w_ref[...])
        o_ref[...] = (da * (2.0 * jnp.maximum(a_ref[...].astype(F32), 0.0))).astype(BF16)

    blk = pl.BlockSpec((tm, tn), lambda i, j: (i, j))
    return pl.pallas_call(
        body, name="dact", grid=(S // tm, D_FF // tn),
        in_specs=[pl.BlockSpec((tm, D), lambda i, j: (i, 0)), pl.BlockSpec((tn, D), lambda i, j: (j, 0)), blk],
        out_specs=blk, out_shape=jax.ShapeDtypeStruct((S, D_FF), BF16),
        compiler_params=_params("arbitrary", "arbitrary"),
    )(dh2b, w2, a)


def _dmerge(dh1b, w_out, pg, y3, wg, wf, wm):
    S = dh1b.shape[0]
    tm = _tile(S, 512)

    def body(d_ref, w_ref, pg_ref, y_ref, wg_ref, wf_ref, wm_ref, *outs):
        dy_refs, do_refs, dg_ref = outs[0:3], outs[3:6], outs[6]
        dm = _nt(d_ref[...], w_ref[...])
        for i, wo_ref in enumerate((wg_ref, wf_ref, wm_ref)):
            gt = jax.nn.sigmoid(pg_ref[:, D * i:D * (i + 1)].astype(F32))
            dy = (dm * gt).astype(BF16)
            dy_refs[i][...] = dy
            do_refs[i][...] = _nt(dy, wo_ref[...])
            dg_ref[:, D * i:D * (i + 1)] = (dm * y_ref[i].astype(F32) * (gt * (1.0 - gt))).astype(BF16)

    row = pl.BlockSpec((tm, D), lambda i: (i, 0))
    half = pl.BlockSpec((tm, 512), lambda i: (i, 0))
    w_spec = pl.BlockSpec((512, D), lambda i: (0, 0))
    return pl.pallas_call(
        body, name="dmerge", grid=(S // tm,),
        in_specs=[row, pl.BlockSpec((D, D), lambda i: (0, 0)), pl.BlockSpec((tm, G_W), lambda i: (i, 0)),
                  pl.BlockSpec((3, tm, D), lambda i: (0, i, 0)), w_spec, w_spec, w_spec],
        out_specs=[row, row, row, half, half, half, pl.BlockSpec((tm, G_W), lambda i: (i, P_GT // G_W))],
        out_shape=[jax.ShapeDtypeStruct((S, D), BF16)] * 3 + [jax.ShapeDtypeStruct((S, 512), F32)] * 3
        + [jax.ShapeDtypeStruct((S, P_W), BF16)],
        compiler_params=_params("arbitrary"),
    )(dh1b, w_out, pg, y3, wg, wf, wm)


def _gla_block_terms(gq_ref, gk_ref, ps_ref, wau_ref, ba_ref, tb):
    gaff = ps_ref[:, 512:640]
    z = _nn(gaff.astype(BF16), wau_ref[...]) + ba_ref[...]
    la = _log_sigmoid(z) * GLA_TAU_INV
    rr = lax.broadcasted_iota(jnp.int32, (tb, tb), 0)
    cc = lax.broadcasted_iota(jnp.int32, (tb, tb), 1)
    same = jnp.right_shift(rr, 6) == jnp.right_shift(cc, 6)
    tri = jnp.where(same & (cc <= rr), 1.0, 0.0).astype(BF16)
    ones = jnp.where(same, 1.0, 0.0).astype(BF16)
    b = _sum01(tri, la)
    bl = _sum01(ones, la)
    e_pos, e_neg, e_last, dec = jnp.exp(b), jnp.exp(-b), jnp.exp(bl - b), jnp.exp(bl)
    q = gq_ref[...].astype(F32) * GLA_SCALE
    k = gk_ref[...].astype(F32)
    return dict(gaff=gaff, z=z, same=same, rr=rr, cc=cc, ones=ones, e_pos=e_pos, e_neg=e_neg, e_last=e_last, dec=dec,
                qp=q * e_pos, qn=q * e_neg, kn=k * e_neg, kp=k * e_pos, kd=k * e_last)


def _head_masked(x, store):
    lane = lax.broadcasted_iota(jnp.int32, x.shape, 1)
    for h in range(GLA_H):
        store[:, h] = jnp.where(jnp.right_shift(lane, 6) == h, x, 0.0).astype(BF16).reshape(-1, CHUNK, GLA_K)


def _lower4():
    t = jnp.bitwise_and(lax.broadcasted_iota(jnp.int32, (GLA_H * CHUNK, CHUNK), 0), CHUNK - 1)
    return t >= lax.broadcasted_iota(jnp.int32, (GLA_H * CHUNK, CHUNK), 1)


def _stack_heads(ref, rows):
    return jnp.concatenate([ref[rows, GLA_DV * h:GLA_DV * (h + 1)] for h in range(GLA_H)], axis=0)


def _gla_fwd(pa, ps, wau, ba, gh):
    S = pa.shape[0]
    tb = _tile(S, 512)
    n_c = tb // CHUNK
    n_b = S // tb

    def body(gq_ref, gk_ref, gv_ref, ps_ref, wau_ref, ba_ref, gh_ref, o_ref, og_ref, sp_ref,
             qpm, qnm, kdm, kn_s, kp_s, dec_s, state):
        @pl.when(pl.program_id(0) == 0)
        def _():
            state[...] = jnp.zeros_like(state)

        t = _gla_block_terms(gq_ref, gk_ref, ps_ref, wau_ref, ba_ref, tb)
        _head_masked(t["qp"], qpm)
        _head_masked(t["qn"], qnm)
        _head_masked(t["kd"], kdm)
        kn_s[...] = t["kn"].astype(BF16)
        kp_s[...] = t["kp"].astype(BF16)
        dec_s[...] = t["dec"]
        lower = _lower4()

        sp = state[...]
        for c in range(n_c):
            rows = slice(c * CHUNK, (c + 1) * CHUNK)
            sp_ref[c] = sp
            qp, qn, kd = (s[c].reshape(GLA_H * CHUNK, GLA_K) for s in (qpm, qnm, kdm))
            attn = jnp.where(lower, _nt(qp, kn_s[rows, :]), _nt(qn, kp_s[rows, :])).astype(BF16)
            inter = _nt(qp, sp.astype(BF16))
            for h in range(GLA_H):
                mine = slice(CHUNK * h, CHUNK * (h + 1))
                cols = slice(GLA_DV * h, GLA_DV * (h + 1))
                o_ref[rows, cols] = _nn(attn[mine], gv_ref[rows, cols]) + inter[mine]
            sp = sp * dec_s[c * CHUNK:c * CHUNK + 1, :] + _tn(_stack_heads(gv_ref, rows), kd)
        state[...] = sp
        for h in range(GLA_H):
            cols = slice(GLA_DV * h, GLA_DV * (h + 1))
            r, xh = _rms(o_ref[:, cols])
            gg = ps_ref[:, cols]
            og_ref[:, cols] = ((xh * gh_ref[:, cols]) * (gg * jax.nn.sigmoid(gg))).astype(BF16)

    return pl.pallas_call(
        body, name="gla_fwd", grid=(n_b,),
        in_specs=[pl.BlockSpec((tb, GLA_K), lambda i: (i, 0)), pl.BlockSpec((tb, GLA_K), lambda i: (i, 1)),
                  pl.BlockSpec((tb, GLA_V), lambda i: (i, 1)), pl.BlockSpec((tb, S_W), lambda i: (i, 0)),
                  pl.BlockSpec((128, GLA_K), lambda i: (0, 0)), pl.BlockSpec((1, GLA_K), lambda i: (0, 0)),
                  pl.BlockSpec((1, GLA_V), lambda i: (0, 0))],
        out_specs=[pl.BlockSpec((tb, GLA_V), lambda i: (i, 0)), pl.BlockSpec((tb, GLA_V), lambda i: (i, 0)),
                   pl.BlockSpec((n_c, GLA_DV, GLA_K), lambda i: (i, 0, 0))],
        out_shape=[jax.ShapeDtypeStruct((S, GLA_V), F32), jax.ShapeDtypeStruct((S, GLA_V), BF16),
                   jax.ShapeDtypeStruct((S // CHUNK, GLA_DV, GLA_K), F32)],
        scratch_shapes=[pltpu.VMEM((n_c, GLA_H, CHUNK, GLA_K), BF16)] * 3
        + [pltpu.VMEM((tb, GLA_K), BF16), pltpu.VMEM((tb, GLA_K), BF16), pltpu.VMEM((tb, GLA_K), F32),
           pltpu.VMEM((GLA_DV, GLA_K), F32)],
        compiler_params=_params("arbitrary"),
    )(pa, pa, pa, ps, wau, ba, gh)


def _gla_bwd(pa, ps, wau, ba, gh, o_gla, d_og, sprev, dgaff_fox, d_proj, slabs):
    S = pa.shape[0]
    tb = _tile(S, 512)
    n_c = tb // CHUNK
    n_b = S // tb
    n_x = len(slabs)
    c_gk, c_gv, c_gg, c_ga, c_end = GLA_K, 2 * GLA_K, 2 * GLA_K + GLA_V, 2 * GLA_K + 2 * GLA_V, 2 * GLA_K + 2 * GLA_V + 128

    def body(*refs):
        gq_ref, gk_ref, gv_ref, ps_ref, wau_ref, ba_ref, gh_ref, o_ref, dog_ref, sp_ref, dfx_ref = refs[:11]
        dp_ref, dwau_ref, dba_ref, dgh_ref = refs[12 + n_x:16 + n_x]
        (qpm, qnm, kdm, kn_s, kp_s, dec_s, do_s, dqp_s, dqn_s, dkn_s, dkp_s, dkd_s, ddec_s,
         dstate) = refs[16 + 2 * n_x:30 + 2 * n_x]
        scatter = lambda: _AllToAll(refs[12:12 + n_x], refs[16 + n_x:16 + 2 * n_x], refs[30 + 2 * n_x:], False)
        first = pl.program_id(0) == 0
        dp_ref[:, c_end:] = jnp.zeros((tb, P_GLA_W - c_end), BF16)

        @pl.when(first)
        def _():
            dstate[...] = jnp.zeros_like(dstate)
            scatter().start()

        t = _gla_block_terms(gq_ref, gk_ref, ps_ref, wau_ref, ba_ref, tb)
        _head_masked(t["qp"], qpm)
        _head_masked(t["qn"], qnm)
        _head_masked(t["kd"], kdm)
        kn_s[...] = t["kn"].astype(BF16)
        kp_s[...] = t["kp"].astype(BF16)
        dec_s[...] = t["dec"]

        dgh_parts = []
        for h in range(GLA_H):
            cols = slice(GLA_DV * h, GLA_DV * (h + 1))
            r, xh = _rms(o_ref[:, cols])
            g = gh_ref[:, cols]
            gg = ps_ref[:, cols]
            sg = jax.nn.sigmoid(gg)
            d_out = dog_ref[:, cols]
            dp_ref[:, c_gg + GLA_DV * h:c_gg + GLA_DV * (h + 1)] = (d_out * (xh * g) * (sg * (1.0 + gg * (1.0 - sg)))).astype(BF16)
            d_on = d_out * (gg * sg)
            dgh_parts.append(_sum8(d_on * xh))
            do_s[:, cols] = _rms_bwd(d_on, g, r, xh).astype(BF16)
        dgh_part = jnp.concatenate(dgh_parts, axis=1)

        lower = _lower4()
        lane = lax.broadcasted_iota(jnp.int32, (CHUNK, GLA_K), 1)

        def own_columns(stacked):
            return sum(jnp.where(jnp.right_shift(lane, 6) == h, stacked[CHUNK * h:CHUNK * (h + 1)], 0.0) for h in range(GLA_H))

        ds_next = dstate[...]
        for c in reversed(range(n_c)):
            rows = slice(c * CHUNK, (c + 1) * CHUNK)
            dsb = ds_next.astype(BF16)
            sp = sp_ref[c]
            knc, kpc = kn_s[rows, :], kp_s[rows, :]
            qp, qn, kd = (s[c].reshape(GLA_H * CHUNK, GLA_K) for s in (qpm, qnm, kdm))
            v4, do4 = _stack_heads(gv_ref, rows), _stack_heads(do_s, rows)
            ddec_s[rows, :] = jnp.broadcast_to(jnp.sum(ds_next * sp, axis=0, keepdims=True), (CHUNK, GLA_K))
            attn = jnp.where(lower, _nt(qp, knc), _nt(qn, kpc)).astype(BF16)
            da = jnp.concatenate([_nt(do4[CHUNK * h:CHUNK * (h + 1)], v4[CHUNK * h:CHUNK * (h + 1)]) for h in range(GLA_H)],
                                 axis=0)
            dac = jnp.where(lower, da, 0.0).astype(BF16)
            daa = jnp.where(lower, 0.0, da).astype(BF16)
            dqp_s[rows, :] = own_columns(_nn(dac, knc) + _nn(do4, sp.astype(BF16)))
            dqn_s[rows, :] = own_columns(_nn(daa, kpc))
            dkd_s[rows, :] = own_columns(_nn(v4, dsb))
            dkn_s[rows, :] = _tn(dac, qp)
            dkp_s[rows, :] = _tn(daa, qn)
            dv_state = _nt(kd, dsb)
            for h in range(GLA_H):
                mine = slice(CHUNK * h, CHUNK * (h + 1))
                dp_ref[rows, c_gv + GLA_DV * h:c_gv + GLA_DV * (h + 1)] = (_tn(attn[mine], do4[mine]) + dv_state[mine]).astype(BF16)
            ds_next = ds_next * dec_s[c * CHUNK:c * CHUNK + 1, :] + _tn(do4, qp)
        dstate[...] = ds_next

        dqp, dqn, dkn, dkp, dkd = dqp_s[...], dqn_s[...], dkn_s[...], dkp_s[...], dkd_s[...]
        dp_ref[:, 0:c_gk] = ((dqp * t["e_pos"] + dqn * t["e_neg"]) * GLA_SCALE).astype(BF16)
        dp_ref[:, c_gk:c_gv] = (dkn * t["e_neg"] + dkp * t["e_pos"] + dkd * t["e_last"]).astype(BF16)
        kd_term = dkd * t["kd"]
        db = dqp * t["qp"] - dqn * t["qn"] - dkn * t["kn"] + dkp * t["kp"] - kd_term
        upper = jnp.where(t["same"] & (t["cc"] >= t["rr"]), 1.0, 0.0).astype(BF16)
        dla = (_sum01(upper, db) + _sum01(t["ones"], kd_term)
               + ddec_s[...] * t["dec"])
        dz = dla * GLA_TAU_INV * jax.nn.sigmoid(-t["z"])
        dzb = dz.astype(BF16)
        dp_ref[:, c_ga:c_end] = (_nt(dzb, wau_ref[...]) + dfx_ref[...]).astype(BF16)
        dwau_part = _tn(t["gaff"].astype(BF16), dzb)
        dba_part = _sum8(dz)

        @pl.when(first)
        def _():
            dwau_ref[...] = dwau_part
            dba_ref[...] = dba_part
            dgh_ref[...] = dgh_part

        @pl.when(jnp.logical_not(first))
        def _():
            dwau_ref[...] += dwau_part
            dba_ref[...] += dba_part
            dgh_ref[...] += dgh_part

        @pl.when(pl.program_id(0) == n_b - 1)
        def _():
            scatter().wait()

    rev = lambda i: (n_b - 1 - i, 0)
    f32k = pltpu.VMEM((tb, GLA_K), F32)
    bf4 = pltpu.VMEM((n_c, GLA_H, CHUNK, GLA_K), BF16)
    any_spec = pl.BlockSpec(memory_space=pl.ANY)
    out = pl.pallas_call(
        body, name="gla_bwd", grid=(n_b,),
        in_specs=[pl.BlockSpec((tb, GLA_K), rev), pl.BlockSpec((tb, GLA_K), lambda i: (n_b - 1 - i, 1)),
                  pl.BlockSpec((tb, GLA_V), lambda i: (n_b - 1 - i, 1)), pl.BlockSpec((tb, S_W), rev),
                  pl.BlockSpec((128, GLA_K), lambda i: (0, 0)), pl.BlockSpec((1, GLA_K), lambda i: (0, 0)),
                  pl.BlockSpec((1, GLA_V), lambda i: (0, 0)), pl.BlockSpec((tb, GLA_V), rev), pl.BlockSpec((tb, GLA_V), rev),
                  pl.BlockSpec((n_c, GLA_DV, GLA_K), lambda i: (n_b - 1 - i, 0, 0)), pl.BlockSpec((tb, 128), rev),
                  any_spec] + [any_spec] * n_x,
        out_specs=[pl.BlockSpec((tb, P_GLA_W), lambda i: (n_b - 1 - i, P_GLA // P_GLA_W)),
                   pl.BlockSpec((128, GLA_K), lambda i: (0, 0)), pl.BlockSpec((8, GLA_K), lambda i: (0, 0)),
                   pl.BlockSpec((8, GLA_V), lambda i: (0, 0))] + [any_spec] * n_x,
        out_shape=[jax.ShapeDtypeStruct((S, P_W), BF16), jax.ShapeDtypeStruct((128, GLA_K), F32),
                   jax.ShapeDtypeStruct((8, GLA_K), F32), jax.ShapeDtypeStruct((8, GLA_V), F32)]
        + [jax.ShapeDtypeStruct(b.shape, b.dtype) for b in slabs],
        input_output_aliases={11: 0},
        scratch_shapes=[bf4, bf4, bf4, pltpu.VMEM((tb, GLA_K), BF16), pltpu.VMEM((tb, GLA_K), BF16), f32k,
                        pltpu.VMEM((tb, GLA_V), BF16), f32k, f32k, f32k, f32k, f32k, f32k, pltpu.VMEM((GLA_DV, GLA_K), F32)]
        + _exchange_sems(n_x),
        compiler_params=_params("arbitrary"),
    )(pa, pa, pa, ps, wau, ba, gh, o_gla, d_og, sprev, dgaff_fox, d_proj, *slabs)
    return out[0], out[1], out[2], out[3], out[4:]


def _split3(x):
    x1 = x.astype(BF16).astype(F32)
    x2 = (x - x1).astype(BF16).astype(F32)
    x3 = (x - x1 - x2).astype(BF16).astype(F32)
    return x1, x2, x3


def _fox_prep(pa, ps, bfg):
    S = pa.shape[0]
    tm = _tile(S, FOX_TK)

    def body(ps_ref, b_ref, fq_ref, fk_ref, fv_ref, q_ref, k_ref, qt_ref, kt_ref, vt_ref, st_ref, carry):
        @pl.when(pl.program_id(0) == 0)
        def _():
            carry[...] = jnp.zeros_like(carry)

        vt_ref[...] = fv_ref[...].astype(F32).T.astype(BF16)
        lf = _log_sigmoid(ps_ref[...] + b_ref[...])
        rr = lax.broadcasted_iota(jnp.int32, (tm, tm), 0)
        cc = lax.broadcasted_iota(jnp.int32, (tm, tm), 1)
        tri = jnp.where(cc <= rr, 1.0, 0.0).astype(F32)
        f = jnp.dot(tri, lf, preferred_element_type=F32, precision=HIGHEST) + carry[0:1, :]
        carry[...] = jnp.broadcast_to(f[tm - 1:tm, :], carry.shape)
        f1, f2, f3 = _split3(f)
        lane = lax.broadcasted_iota(jnp.int32, (tm, 128), 1)
        st_row = lax.broadcasted_iota(jnp.int32, (8, 128), 0)
        st_lane = lax.broadcasted_iota(jnp.int32, (8, 128), 1)
        stats = jnp.zeros((8, 128), F32)
        for h in range(FOX_H):
            cols = slice(128 * h, 128 * (h + 1))
            c = FF_LANE + h
            a1, a2, a3 = f1[:, c:c + 1], f2[:, c:c + 1], f3[:, c:c + 1]
            q = fq_ref[:, cols].astype(F32) * FOX_SCALE
            k = fk_ref[:, cols].astype(F32)
            fh = f[:, c:c + 1]
            vals = (jnp.max(jnp.sum(q * q, axis=-1, keepdims=True)), jnp.max(jnp.sum(k * k, axis=-1, keepdims=True)),
                    jnp.max(fh), jnp.min(fh), jnp.min(jnp.sum(q * k, axis=-1, keepdims=True)))
            for n, val in enumerate(vals):
                stats = jnp.where((st_row == h) & (st_lane == n), val, stats)
            for n, a in enumerate((a1, a2, a3)):
                q = jnp.where(lane == AUG + n, a, q)
                k = jnp.where(lane == AUG + 3 + n, -a, k)
            q = jnp.where((lane >= AUG + 3) & (lane < AUG + 6), 1.0, q)
            k = jnp.where((lane >= AUG) & (lane < AUG + 3), 1.0, k)
            q_ref[:, cols] = q.astype(BF16)
            k_ref[:, cols] = k.astype(BF16)
            qt_ref[cols, :] = q.T.astype(BF16)
            kt_ref[cols, :] = k.T.astype(BF16)
        st_ref[0] = stats

    wide = lambda j: pl.BlockSpec((tm, 1024), lambda i: (i, j))
    tall = lambda n: pl.BlockSpec((n, tm), lambda i: (0, i))
    return pl.pallas_call(
        body, name="fox_prep", grid=(S // tm,),
        in_specs=[pl.BlockSpec((tm, 128), lambda i: (i, 4)), pl.BlockSpec((1, 128), lambda i: (0, 0)), wide(1), wide(2),
                  pl.BlockSpec((tm, FOX_W), lambda i: (i, A_FV // FOX_W))],
        out_specs=[wide(0), wide(0), tall(1024), tall(1024), tall(FOX_W), pl.BlockSpec((1, 8, 128), lambda i: (i, 0, 0))],
        out_shape=[jax.ShapeDtypeStruct((S, 1024), BF16), jax.ShapeDtypeStruct((S, 1024), BF16),
                   jax.ShapeDtypeStruct((1024, S), BF16), jax.ShapeDtypeStruct((1024, S), BF16),
                   jax.ShapeDtypeStruct((FOX_W, S), BF16), jax.ShapeDtypeStruct((S // tm, 8, 128), F32)],
        scratch_shapes=[pltpu.VMEM((8, 128), F32)],
        compiler_params=_params("arbitrary"),
    )(ps, bfg, pa, pa, pa)


FOX_PRUNE_AT = -90.0


def _fox_live_ranges(stats, n_sub, ratio):
    n_b = stats.shape[0]
    q2, k2, f_max, f_min, own = (stats[:, :, n].T for n in range(5))
    slack = 0.01 * jnp.sqrt(q2 * k2) + 1e-5 * jnp.abs(f_max) + 1.0
    bound = (1.01 * jnp.sqrt(q2[:, :, None] * k2[:, None, :]) + (f_max + slack - own)[:, :, None]
             - (f_min - 1e-5 * jnp.abs(f_min))[:, None, :])
    blocks = jnp.arange(n_b)
    dead = (bound <= FOX_PRUNE_AT) & (blocks[None, :] < blocks[:, None])[None]
    dead_fwd = dead.reshape(FOX_H, n_b // n_sub, n_sub, n_b).all(axis=2)
    first = jnp.sum(jnp.cumprod(dead_fwd.astype(jnp.int32), axis=2), axis=2)
    last_live = n_b - 1 - jnp.sum(jnp.cumprod(dead[:, ::-1, :].astype(jnp.int32), axis=1), axis=1)
    first_wide = blocks // ratio + 1
    narrow_end = jnp.minimum(jnp.minimum(first_wide * ratio, n_b)[None], last_live + 1)
    wide_end = jnp.where(last_live >= (first_wide * ratio)[None], last_live // ratio + 1, first_wide[None])
    return first.astype(jnp.int32), narrow_end.astype(jnp.int32), wide_end.astype(jnp.int32)


def _fox_fwd(qa, ka, vt, first):
    S = qa.shape[0]
    tq = _tile(S, FOX_TQ)
    tk = _tile(tq, FOX_TK)
    n_sub = tq // tk

    def body(first_ref, q_ref, k_ref, vt_ref, o_ref, lse_ref):
        pair, i = pl.program_id(0), pl.program_id(1)
        both = lambda f: tuple(f(hh) for hh in range(2))

        def blk(j, carry, diag, heads=(0, 1)):
            ks = pl.ds(pl.multiple_of(j * tk, tk), tk)
            q0 = 0 if diag is None else diag * tk

            def head(hh):
                if hh not in heads:
                    return carry[hh]
                m, l, acc = carry[hh]
                mo, lo, ao = m[:, q0:], l[:, q0:], acc[:, q0:]
                s = _nt(k_ref[ks, 128 * hh:128 * (hh + 1)], q_ref[q0:, 128 * hh:128 * (hh + 1)])
                if diag is not None:
                    live = lax.broadcasted_iota(jnp.int32, s.shape, 1) >= lax.broadcasted_iota(jnp.int32, s.shape, 0)
                    s = jnp.where(live, s, NEG)
                mn = jnp.maximum(mo, jnp.max(s, axis=0, keepdims=True))
                p = jnp.exp(s - mn)
                al = jnp.exp(mo - mn)
                ln = al * lo + jnp.sum(p, axis=0, keepdims=True)
                an = al * ao + _nn(vt_ref[FOX_DH * hh:FOX_DH * (hh + 1), ks], p.astype(BF16))
                if q0:
                    mn, ln, an = (jnp.concatenate([old[:, :q0], new], axis=1) for old, new in ((m, mn), (l, ln), (acc, an)))
                return mn, ln, an

            return both(head)

        one = (jnp.full((1, tq), NEG, F32), jnp.zeros((1, tq), F32), jnp.zeros((FOX_DH, tq), F32))
        past = i * n_sub
        f0, f1 = first_ref[2 * pair, i], first_ref[2 * pair + 1, i]
        join = jnp.maximum(f0, f1)
        solo = lambda hh: lambda c: lax.fori_loop(jnp.minimum(f0, f1), join, lambda j, cc: blk(j, cc, None, (hh,)), c)
        carry = lax.cond(f0 < f1, solo(0), solo(1), (one, one))
        n_both = past - join
        carry = lax.fori_loop(0, n_both // 2, lambda jj, c: blk(join + 2 * jj + 1, blk(join + 2 * jj, c, None), None), carry)
        carry = lax.cond(n_both % 2 == 1, lambda c: blk(past - 1, c, None), lambda c: c, carry)
        for d in range(n_sub):
            carry = blk(past + d, carry, d)
        (m0, l0, a0), (m1, l1, a1) = carry
        o_ref[...] = jnp.concatenate([a0 / l0, a1 / l1], axis=0).T
        lse_ref[0, 0:1, :] = m0 + jnp.log(l0)
        lse_ref[0, 1:2, :] = m1 + jnp.log(l1)
        lse_ref[0, 2:8, :] = jnp.zeros((6, tq), F32)

    return pl.pallas_call(
        body, name="fox_fwd", grid=(FOX_H // 2, S // tq),
        in_specs=[pl.BlockSpec(memory_space=pltpu.SMEM), pl.BlockSpec((tq, 256), lambda p, i: (i, p)),
                  pl.BlockSpec((S, 256), lambda p, i: (0, p)), pl.BlockSpec((128, S), lambda p, i: (p, 0))],
        out_specs=[pl.BlockSpec((tq, 128), lambda p, i: (i, p)), pl.BlockSpec((1, 8, tq), lambda p, i: (p, 0, i))],
        out_shape=[jax.ShapeDtypeStruct((S, FOX_W), F32), jax.ShapeDtypeStruct((FOX_H // 2, 8, S), F32)],
        compiler_params=_params("arbitrary", "arbitrary"),
    )(first, qa, ka, vt)


def _fox_delta(d_o, o):
    S = o.shape[0]
    tm = _tile(S, 512)

    def body(d_ref, o_ref, db_ref, dbt_ref, dl_ref):
        d = d_ref[...]
        db_ref[...] = d.astype(BF16)
        dbt_ref[...] = d.T.astype(BF16)
        prod = d * o_ref[...]
        rr = lax.broadcasted_iota(jnp.int32, (8, 128), 0)
        cc = lax.broadcasted_iota(jnp.int32, (8, 128), 1)
        ind = jnp.where(jnp.right_shift(cc, 6) == rr, 1.0, 0.0).astype(F32)
        for p in range(FOX_H // 2):
            dl_ref[p] = lax.dot_general(ind, prod[:, 128 * p:128 * (p + 1)], (((1,), (1,)), ((), ())),
                                        preferred_element_type=F32, precision=HIGHEST)

    row = pl.BlockSpec((tm, FOX_W), lambda i: (i, 0))
    return pl.pallas_call(
        body, name="fox_delta", grid=(S // tm,),
        in_specs=[row, row],
        out_specs=[row, pl.BlockSpec((FOX_W, tm), lambda i: (0, i)), pl.BlockSpec((FOX_H // 2, 8, tm), lambda i: (0, 0, i))],
        out_shape=[jax.ShapeDtypeStruct((S, FOX_W), BF16), jax.ShapeDtypeStruct((FOX_W, S), BF16),
                   jax.ShapeDtypeStruct((FOX_H // 2, 8, S), F32)],
        compiler_params=_params("arbitrary"),
    )(d_o, o)


def _fox_bwd(qa, qat, ka, kat, pa, dob, dobt, lse, delta, narrow_end, wide_end):
    S = qa.shape[0]
    tk = _tile(S, FOX_TK)
    wide = _tile(S, FOX_BWD_WIDE)
    ratio = wide // tk
    n_wide = S // wide

    def body(ne_ref, we_ref, q_ref, qt_ref, k_ref, kt_ref, v_ref, do_ref, dot_ref, lse_ref, dl_ref, dq_ref, dk_ref, dv_ref):
        h, jb = pl.program_id(0), pl.program_id(1)
        hh = h % 2

        @pl.when(jb == 0)
        def _():
            dq_ref[...] = jnp.zeros_like(dq_ref)

        lane = lax.broadcasted_iota(jnp.int32, (tk, 128), 1)
        vm = jnp.where(jnp.right_shift(lane, 6) == hh, v_ref[...], jnp.zeros((), BF16))
        kb, ktb = k_ref[...], kt_ref[0:FOX_LIVE, :]
        mine = pl.ds(pl.multiple_of(hh * FOX_DH, FOX_DH), FOX_DH)

        def blk(ib, tq, carry, masked):
            dk, dv = carry
            qs = pl.ds(pl.multiple_of(ib * tq, tq), tq)
            p = jnp.exp(_nt(kb, q_ref[qs, :]) - lse_ref[0, pl.ds(hh, 1), qs])
            if masked:
                live = lax.broadcasted_iota(jnp.int32, p.shape, 1) >= lax.broadcasted_iota(jnp.int32, p.shape, 0)
                p = jnp.where(live, p, 0.0)
            ds = (p * (_nt(vm, do_ref[qs, :]) - dl_ref[0, pl.ds(hh, 1), qs])).astype(BF16)
            dq_ref[0:FOX_LIVE, qs] += _nn(ktb, ds)
            return dk + _nt(qt_ref[0:FOX_LIVE, qs], ds), dv + _nt(dot_ref[mine, qs], p.astype(BF16))

        carry = blk(jb, tk, (jnp.zeros((FOX_LIVE, tk), F32), jnp.zeros((FOX_DH, tk), F32)), True)
        first_wide = jb // ratio + 1
        carry = lax.fori_loop(jb + 1, ne_ref[h, jb], lambda ib, c: blk(ib, tk, c, False), carry)
        last_wide = we_ref[h, jb]
        rest = jnp.maximum(last_wide - first_wide, 0)
        carry = lax.fori_loop(0, rest // 2, lambda t, c: blk(first_wide + 2 * t + 1, wide, blk(first_wide + 2 * t, wide, c, False),
                                                             False), carry)
        dk, dv = lax.cond(rest % 2 == 1, lambda c: blk(last_wide - 1, wide, c, False), lambda c: c, carry)
        dk_ref[0:FOX_LIVE, :] = dk
        dk_ref[FOX_LIVE:, :] = jnp.zeros((128 - FOX_LIVE, tk), F32)
        dv_ref[...] = dv

    once = pl.Buffered(1)
    rows = pl.BlockSpec((1, 8, S), lambda h, j: (h // 2, 0, 0))
    return pl.pallas_call(
        body, name="fox_bwd", grid=(FOX_H, S // tk),
        in_specs=[pl.BlockSpec(memory_space=pltpu.SMEM), pl.BlockSpec(memory_space=pltpu.SMEM),
                  pl.BlockSpec((S, 128), lambda h, j: (0, h)), pl.BlockSpec((128, S), lambda h, j: (h, 0)),
                  pl.BlockSpec((tk, 128), lambda h, j: (j, h)), pl.BlockSpec((128, tk), lambda h, j: (h, j)),
                  pl.BlockSpec((tk, 128), lambda h, j: (j, A_FV // 128 + h // 2)),
                  pl.BlockSpec((S, 128), lambda h, j: (0, h // 2)), pl.BlockSpec((128, S), lambda h, j: (h // 2, 0)),
                  rows, rows],
        out_specs=[pl.BlockSpec((128, S), lambda h, j: (h, 0), pipeline_mode=once),
                   pl.BlockSpec((128, tk), lambda h, j: (h, j)), pl.BlockSpec((FOX_DH, tk), lambda h, j: (h, j))],
        out_shape=[jax.ShapeDtypeStruct((1024, S), F32), jax.ShapeDtypeStruct((1024, S), F32),
                   jax.ShapeDtypeStruct((FOX_W, S), F32)],
        compiler_params=_params("arbitrary", "arbitrary"),
    )(narrow_end, wide_end, qa, qat, ka, kat, pa, dob, dobt, lse, delta)


def _fox_post(dq, dk, dv, ps, bfg, d_proj):
    S = dq.shape[1]
    tm = _tile(S, 512)
    n_b = S // tm

    def body(dq_ref, dk_ref, dv_ref, ps_ref, b_ref, _, dp_ref, dff_ref, dbf_ref, carry):
        first = pl.program_id(0) == 0

        @pl.when(first)
        def _():
            carry[...] = jnp.zeros_like(carry)

        low = lax.broadcasted_iota(jnp.int32, (tm, 128), 1) < FOX_DH
        for h in range(FOX_H):
            blk = slice(128 * h, 128 * (h + 1))
            dp_ref[:, blk] = jnp.where(low, dq_ref[blk, :].T * FOX_SCALE, 0.0).astype(BF16)
            dp_ref[:, 1024 + 128 * h:1024 + 128 * (h + 1)] = jnp.where(low, dk_ref[blk, :].T, 0.0).astype(BF16)
        dp_ref[:, 2048:P_FOX_W] = dv_ref[...].T.astype(BF16)
        rr = lax.broadcasted_iota(jnp.int32, (FOX_H, 1024), 0)
        cc = lax.broadcasted_iota(jnp.int32, (FOX_H, 1024), 1)
        sel_k = jnp.where(cc == 128 * rr + AUG + 3, 1.0, 0.0).astype(F32)
        sel_q = jnp.where(cc == 128 * rr + AUG, 1.0, 0.0).astype(F32)
        g = (jnp.dot(sel_k, dk_ref[...], preferred_element_type=F32, precision=HIGHEST)
             - jnp.dot(sel_q, dq_ref[...], preferred_element_type=F32, precision=HIGHEST))
        t_from = lax.broadcasted_iota(jnp.int32, (tm, tm), 0)
        t_to = lax.broadcasted_iota(jnp.int32, (tm, tm), 1)
        later = jnp.where(t_from >= t_to, 1.0, 0.0).astype(F32)
        dlf = jnp.dot(-g, later, preferred_element_type=F32, precision=HIGHEST) + carry[:, 0:1]
        carry[...] = jnp.broadcast_to(dlf[:, 0:1], carry.shape)
        cols = jnp.concatenate([jnp.zeros((FF_LANE, tm), F32), dlf, jnp.zeros((128 - FF_LANE - FOX_H, tm), F32)], axis=0).T
        dff = cols * jax.nn.sigmoid(-(ps_ref[...] + b_ref[...]))
        dff_ref[...] = dff
        part = _sum8(dff)

        @pl.when(first)
        def _():
            dbf_ref[...] = part

        @pl.when(jnp.logical_not(first))
        def _():
            dbf_ref[...] += part

    rev = lambda i: (n_b - 1 - i, 0)
    tall = lambda n: pl.BlockSpec((n, tm), lambda i: (0, n_b - 1 - i))
    return pl.pallas_call(
        body, name="fox_post", grid=(n_b,),
        in_specs=[tall(1024), tall(1024), tall(FOX_W), pl.BlockSpec((tm, 128), lambda i: (n_b - 1 - i, 4)),
                  pl.BlockSpec((1, 128), lambda i: (0, 0)), pl.BlockSpec(memory_space=pl.ANY)],
        out_specs=[pl.BlockSpec((tm, P_FOX_W), lambda i: (n_b - 1 - i, P_FOX // P_FOX_W)), pl.BlockSpec((tm, 128), rev),
                   pl.BlockSpec((8, 128), lambda i: (0, 0))],
        out_shape=[jax.ShapeDtypeStruct((S, P_W), BF16), jax.ShapeDtypeStruct((S, 128), F32),
                   jax.ShapeDtypeStruct((8, 128), F32)],
        input_output_aliases={5: 0},
        scratch_shapes=[pltpu.VMEM((8, 128), F32)],
        compiler_params=_params("arbitrary"),
    )(dq, dk, dv, ps, bfg, d_proj)


def _mem_prep(mem, g_mem, wkv):
    def body(m_ref, g_ref, w_ref, mn_ref, kv_ref):
        r, xh = _rms(m_ref[...])
        mn = (xh * g_ref[...]).astype(BF16)
        mn_ref[...] = mn
        kv_ref[...] = _nn(mn, w_ref[...]).astype(BF16)

    return pl.pallas_call(
        body, name="mem_prep",
        out_shape=[jax.ShapeDtypeStruct((N_MEM, D), BF16), jax.ShapeDtypeStruct((N_MEM, 2 * MEM_W), BF16)],
        compiler_params=pltpu.CompilerParams(vmem_limit_bytes=V7X_VMEM_LIMIT),
    )(mem, g_mem, wkv)


def _mem_softmax(qh, kh):
    s = _nt(qh, kh) * MEM_SCALE
    e = jnp.exp(s - jnp.max(s, axis=-1, keepdims=True))
    return e / jnp.sum(e, axis=-1, keepdims=True)


def _mem_fwd(pa, mkv):
    S = pa.shape[0]
    tm = _tile(S, 512)

    def body(q_ref, kv_ref, o_ref):
        for h in range(MEM_H):
            cols = slice(MEM_DH * h, MEM_DH * (h + 1))
            p = _mem_softmax(q_ref[:, cols], kv_ref[:, cols])
            o_ref[:, cols] = _nn(p.astype(BF16), kv_ref[:, MEM_W + MEM_DH * h:MEM_W + MEM_DH * (h + 1)])

    return pl.pallas_call(
        body, name="mem_fwd", grid=(S // tm,),
        in_specs=[pl.BlockSpec((tm, MEM_W), lambda i: (i, A_MQ // MEM_W)), pl.BlockSpec((N_MEM, 2 * MEM_W), lambda i: (0, 0))],
        out_specs=pl.BlockSpec((tm, MEM_W), lambda i: (i, 0)),
        out_shape=jax.ShapeDtypeStruct((S, MEM_W), F32),
        compiler_params=_params("arbitrary"),
    )(pa, mkv)


def _mem_bwd(pa, mkv, d_o, d_proj):
    S = pa.shape[0]
    tm = _tile(S, 512)

    def body(q_ref, kv_ref, do_ref, _, dq_ref, dkv_ref):
        first = pl.program_id(0) == 0
        parts = []
        for h in range(MEM_H):
            cols = slice(MEM_DH * h, MEM_DH * (h + 1))
            vcols = slice(MEM_W + MEM_DH * h, MEM_W + MEM_DH * (h + 1))
            qh, kh = q_ref[:, cols], kv_ref[:, cols]
            p = _mem_softmax(qh, kh)
            dob = do_ref[:, cols].astype(BF16)
            dp = _nt(dob, kv_ref[:, vcols])
            ds = (p * (dp - jnp.sum(p * dp, axis=-1, keepdims=True)) * MEM_SCALE).astype(BF16)
            dq_ref[:, cols] = _nn(ds, kh).astype(BF16)
            parts.append((cols, _tn(ds, qh)))
            parts.append((vcols, _tn(p.astype(BF16), dob)))

        @pl.when(first)
        def _():
            for sl, v in parts:
                dkv_ref[:, sl] = v

        @pl.when(jnp.logical_not(first))
        def _():
            for sl, v in parts:
                dkv_ref[:, sl] += v

    return pl.pallas_call(
        body, name="mem_bwd", grid=(S // tm,),
        in_specs=[pl.BlockSpec((tm, MEM_W), lambda i: (i, A_MQ // MEM_W)), pl.BlockSpec((N_MEM, 2 * MEM_W), lambda i: (0, 0)),
                  pl.BlockSpec((tm, MEM_W), lambda i: (i, 0)), pl.BlockSpec(memory_space=pl.ANY)],
        out_specs=[pl.BlockSpec((tm, MEM_W), lambda i: (i, P_MQ // MEM_W)), pl.BlockSpec((N_MEM, 2 * MEM_W), lambda i: (0, 0))],
        out_shape=[jax.ShapeDtypeStruct((S, P_W), BF16), jax.ShapeDtypeStruct((N_MEM, 2 * MEM_W), F32)],
        input_output_aliases={3: 0},
        compiler_params=_params("arbitrary"),
    )(pa, mkv, d_o, d_proj)


def _mem_prep_bwd(mem, g_mem, mn, wkv, dkv):
    def body(m_ref, g_ref, mn_ref, w_ref, d_ref, dw_ref, dg_ref):
        db = d_ref[...].astype(BF16)
        dw_ref[...] = _tn(mn_ref[...], db).astype(BF16)
        r, xh = _rms(m_ref[...])
        dg_ref[...] = _sum8(_nt(db, w_ref[...]) * xh)

    dw, dg = pl.pallas_call(
        body, name="mem_prep_bwd",
        out_shape=[jax.ShapeDtypeStruct((D, 2 * MEM_W), BF16), jax.ShapeDtypeStruct((8, D), F32)],
        compiler_params=pltpu.CompilerParams(vmem_limit_bytes=V7X_VMEM_LIMIT),
    )(mem, g_mem, mn, wkv, dkv)
    return dw.reshape(N_DEV, D // N_DEV, 2 * MEM_W), dg


def _rearrange_w_in(w):
    def heads128(cols):
        blk = w[:, cols:cols + FOX_W].reshape(D, FOX_H, FOX_DH)
        return jnp.pad(blk, ((0, 0), (0, 0), (0, 128 - FOX_DH))).reshape(D, FOX_H * 128)

    fq, fk, fv, mq, wg = heads128(O_FQ), heads128(O_FK), w[:, O_FV:O_FF], w[:, O_MQ:O_GT], w[:, O_GT:]
    gaff = jnp.concatenate([w[:, O_GA:O_FQ], w[:, O_FF:O_MQ], jnp.zeros((D, 128 - GLA_R - FOX_H), w.dtype)], axis=1)
    wa = jnp.concatenate([w[:, O_GQ:O_GG], fq, fk, fv, mq], axis=1)
    ws = jnp.concatenate([w[:, O_GG:O_GA], gaff], axis=1)
    wp = jnp.concatenate([fq, fk, fv, mq, wg, w[:, O_GQ:O_GG], ws, jnp.zeros((D, P_W - P_GLA - 1024 - S_W), w.dtype)], axis=1)
    return wa, wg, ws, wp


def _restore_w_in_grad(dwp):
    def unheads(off):
        return dwp[:, off:off + FOX_H * 128].reshape(D, FOX_H, 128)[:, :, :FOX_DH].reshape(D, FOX_W)

    g0 = P_GLA + 1024
    return jnp.concatenate([
        dwp[:, P_GLA:g0], dwp[:, g0:g0 + 512], dwp[:, g0 + 512:g0 + 512 + GLA_R], unheads(P_FOX), unheads(P_FOX + 1024),
        dwp[:, P_FOX + 2048:P_FOX + P_FOX_W], dwp[:, g0 + 512 + GLA_R:g0 + 512 + GLA_R + FOX_H], dwp[:, P_MQ:P_GT],
        dwp[:, P_GT:P_GLA]], axis=1)


def _local_step(x, mem, target, p, late_shards):
    S = x.shape[0]
    p = dict(p)
    wa, wg, ws, wp = _rearrange_w_in(p["w_in"])
    wau = jnp.pad(p["w_alpha_up"], ((0, 128 - GLA_R), (0, 0)))
    bfg = jnp.pad(p["b_forget"], ((0, 0), (FF_LANE, 128 - FF_LANE - FOX_H)))
    gh = p["g_gla_head"].reshape(1, GLA_V)

    pa, pg, ps, u, gathered = _proj(x, p["g_mix"], wa, wg, ws, late_shards)
    p.update({n: _unslab(t, ax) for (n, ax), t in zip(BIG[1:], gathered)})
    o_gla, og, sprev = _gla_fwd(pa, ps, wau, p["b_alpha"], gh)
    qa, ka, qat, kat, vt, fox_stats = _fox_prep(pa, ps, bfg)
    fox_tk = _tile(S, FOX_TK)
    fox_first, fox_narrow_end, fox_wide_end = _fox_live_ranges(fox_stats, _tile(S, FOX_TQ) // fox_tk,
                                                               _tile(S, FOX_BWD_WIDE) // fox_tk)
    o_fox, lse = _fox_fwd(qa, ka, vt, fox_first)
    mn, mkv = _mem_prep(mem, p["g_mem"], p["w_mem_kv"])
    o_mem = _mem_fwd(pa, mkv)
    y3, mg = _merge(og, o_fox, o_mem, p["w_gla_o"], p["w_fox_o"], p["w_mem_o"], pg)
    h1, u2 = _out_proj(mg, p["w_out"], x, p["g_ffn"])
    a, act = _ff1(u2, p["w_ff1"])
    dh2, dh2b, loss8, dg_final = _ff2_loss(act, p["w_ff2"], h1, p["g_final"].reshape(1, D), target)

    d_a = _dact(dh2b, p["w_ff2"], a)
    dw_ff2 = _wgrad(act, dh2b, "wgrad_ff2", 0)
    dh1, dh1b, dg_ffn = _nt_rmsbwd(d_a, p["w_ff1"], h1, p["g_ffn"], dh2, "dffn", True)
    dw_ff1 = _wgrad(u2, d_a, "wgrad_ff1", 1)
    dy_g, dy_f, dy_m, do_g, do_f, do_m, d_proj = _dmerge(dh1b, p["w_out"], pg, y3, p["w_gla_o"], p["w_fox_o"], p["w_mem_o"])
    dw_out = _wgrad(mg, dh1b, "wgrad_out", 0)
    dw_gla_o = _wgrad(og, dy_g, "wgrad_gla_o", 1)
    dw_fox_o = _wgrad(o_fox, dy_f, "wgrad_fox_o", 1)
    dw_mem_o = _wgrad(o_mem, dy_m, "wgrad_mem_o", 1)
    d_proj, d_mkv = _mem_bwd(pa, mkv, do_m, d_proj)
    dw_mem_kv, dg_mem = _mem_prep_bwd(mem, p["g_mem"], mn, p["w_mem_kv"], d_mkv)
    dob, dobt, delta = _fox_delta(do_f, o_fox)
    dq, dk, dv = _fox_bwd(qa, qat, ka, kat, pa, dob, dobt, lse, delta, fox_narrow_end, fox_wide_end)
    d_proj, dgaff_fox, db_forget = _fox_post(dq, dk, dv, ps, bfg, d_proj)
    ready = dict(w_mem_kv=dw_mem_kv, w_gla_o=dw_gla_o, w_fox_o=dw_fox_o, w_mem_o=dw_mem_o, w_out=dw_out, w_ff1=dw_ff1,
                 w_ff2=dw_ff2)
    d_proj, dw_au, db_alpha, dg_gla, arrived = _gla_bwd(pa, ps, wau, p["b_alpha"], gh, o_gla, do_g, sprev, dgaff_fox, d_proj,
                                                        [ready[n] for n, _ in BIG[1:]])
    dw_in = _slabs(_restore_w_in_grad(_wgrad(u, d_proj, "wgrad_in")), 1).astype(BF16)
    dx, dg_mix, arrived_in = _nt_rmsbwd(d_proj, wp, x, p["g_mix"], dh1, "dmix", False, [dw_in])

    big = dict(zip([n for n, _ in BIG], [arrived_in[0], *arrived]))
    small = dict(g_mix=dg_mix, g_mem=dg_mem, g_ffn=dg_ffn, g_final=dg_final, b_alpha=db_alpha, g_gla_head=dg_gla,
                 b_forget=db_forget, w_alpha_up=dw_au, loss=loss8)
    return dx, big, small


BIG = (("w_in", 1), ("w_mem_kv", 0), ("w_gla_o", 1), ("w_fox_o", 1), ("w_mem_o", 1), ("w_out", 0), ("w_ff1", 1), ("w_ff2", 0))


def _peer(d):
    me = lax.axis_index("x") * 4 + lax.axis_index("y") * 2 + lax.axis_index("c")
    t = (me + d) % N_DEV
    return (t // 4, (t // 2) % 2, t % 2), me


def _exchange_sems(n):
    return [pltpu.SemaphoreType.DMA((n, N_DEV - 1)), pltpu.SemaphoreType.DMA((n, N_DEV - 1)), pltpu.SemaphoreType.DMA((n,))]


def _exchange_call(body, blocks, out_shape, name):
    n = len(blocks)
    any_spec = pl.BlockSpec(memory_space=pl.ANY)
    return pl.pallas_call(body, name=name, in_specs=[any_spec] * n, out_specs=[any_spec] * n, out_shape=out_shape,
                          scratch_shapes=_exchange_sems(n))(*blocks)


class _AllToAll:
    def __init__(self, ins, outs, sems, gather):
        send, recv, loc = sems
        n = len(ins)
        _, me = _peer(0)
        src = (lambda k, j: ins[k]) if gather else (lambda k, j: ins[k].at[j])
        self.local = [pltpu.make_async_copy(src(k, me), outs[k].at[me], loc.at[k]) for k in range(n)]
        self.remote = []
        for d in range(1, N_DEV):
            to, _ = _peer(d)
            self.remote += [pltpu.make_async_remote_copy(
                src_ref=src(k, (me + d) % N_DEV), dst_ref=outs[k].at[me], send_sem=send.at[k, d - 1],
                recv_sem=recv.at[k, d - 1], device_id=to, device_id_type=MESH) for k in range(n)]

    def start(self):
        for cp in self.local + self.remote:
            cp.start()

    def wait(self):
        for cp in self.remote:
            cp.wait_send()
        for cp in self.remote:
            cp.wait_recv()
        for cp in self.local:
            cp.wait()


def _gathered_shapes(shards):
    return [jax.ShapeDtypeStruct((N_DEV,) + b.shape, b.dtype) for b in shards]


def _gather_weights(shards):
    n = len(shards)

    def body(*refs):
        ins, outs = refs[:n], refs[n:2 * n]
        send, recv, loc = refs[2 * n:]
        x, y, c = lax.axis_index("x"), lax.axis_index("y"), lax.axis_index("c")
        sibling = (x, y, 1 - c)
        chips = [(1 - x, y), (x, 1 - y), (1 - x, 1 - y)]
        slot = lambda px, py, pc: px * 4 + py * 2 + pc

        def copy(k, s, block, to, src=None):
            rows = outs[k].at[slot(*block)]
            return pltpu.make_async_remote_copy(src_ref=rows if src is None else src, dst_ref=rows, send_sem=send.at[k, s],
                                                recv_sem=recv.at[k, s], device_id=to, device_id_type=MESH)

        me = (x, y, c)
        own = [pltpu.make_async_copy(ins[k], outs[k].at[slot(*me)], loc.at[k]) for k in range(n)]
        first = [copy(k, 0, me, sibling, src=ins[k]) for k in range(n)]
        first += [copy(k, 1 + j, me, (*chip, c), src=ins[k]) for j, chip in enumerate(chips) for k in range(n)]
        for cp in own + first:
            cp.start()
        passed = []
        for j, chip in enumerate(chips):
            for k in range(n):
                copy(k, 1 + j, (*chip, c), me).wait_recv()
                fwd = copy(k, 4 + j, (*chip, c), sibling)
                fwd.start()
                passed.append(fwd)
        for k in range(n):
            copy(k, 0, sibling, me).wait_recv()
        for j, chip in enumerate(chips):
            for k in range(n):
                copy(k, 4 + j, (*chip, 1 - c), me).wait_recv()
        for cp in first + passed:
            cp.wait_send()
        for cp in own:
            cp.wait()

    return _exchange_call(body, shards, [jax.ShapeDtypeStruct((N_DEV,) + b.shape, b.dtype) for b in shards], "gather_weights")


def _adamw_math(g, w, m, v):
    m2 = ADAM_B1 * m + (1.0 - ADAM_B1) * g
    v2 = ADAM_B2 * v + (1.0 - ADAM_B2) * jnp.square(g)
    m_hat = m2 / (1.0 - ADAM_B1 ** ADAM_STEP)
    v_hat = v2 / (1.0 - ADAM_B2 ** ADAM_STEP)
    delta = -ADAM_LR * (m_hat / (jnp.sqrt(v_hat) + ADAM_EPS) + ADAM_WD * w)
    return delta, m2, v2


def _adamw_sum(parts, w, m, v, name):
    R, C = w.shape
    tr = _tile(R, 128)

    def body(p_ref, w_ref, m_ref, v_ref, g_ref, d_ref, m2_ref, v2_ref):
        g = p_ref[0].astype(F32)
        for j in range(1, p_ref.shape[0]):
            g = g + p_ref[j].astype(F32)
        g_ref[...] = g
        d_ref[...], m2_ref[...], v2_ref[...] = _adamw_math(g, w_ref[...], m_ref[...], v_ref[...])

    blk = pl.BlockSpec((tr, C), lambda i: (i, 0))
    return pl.pallas_call(
        body, name=name, grid=(R // tr,),
        in_specs=[pl.BlockSpec((parts.shape[0], tr, C), lambda i: (0, i, 0)), blk, blk, blk],
        out_specs=[blk] * 4, out_shape=[jax.ShapeDtypeStruct((R, C), F32)] * 4,
        compiler_params=_params("arbitrary"),
    )(parts, w, m, v)


SMALL_ROWS = 24


def _pack_small(d):
    mixed = jnp.concatenate([d["b_alpha"].reshape(1, GLA_K), d["g_gla_head"].reshape(1, GLA_V),
                             jnp.pad(d["b_forget"].reshape(1, FOX_H), ((0, 0), (FF_LANE, 128 - FF_LANE - FOX_H))),
                             jnp.zeros((1, 128), F32)], axis=1)
    rows = [d["g_mix"].reshape(1, D), d["g_mem"].reshape(1, D), d["g_ffn"].reshape(1, D), d["g_final"].reshape(1, D), mixed,
            jnp.zeros((3, D), F32), jnp.pad(d["w_alpha_up"].reshape(GLA_R, GLA_K), ((0, 0), (0, D - GLA_K)))]
    return jnp.concatenate(rows, axis=0)


def _unpack_small(t):
    return dict(g_mix=t[0:1], g_mem=t[1:2], g_ffn=t[2:3], g_final=t[3], b_alpha=t[4:5, 0:GLA_K],
                g_gla_head=t[4:5, GLA_K:GLA_K + GLA_V].reshape(1, GLA_H, GLA_DV),
                b_forget=t[4:5, 768 + FF_LANE:768 + FF_LANE + FOX_H], w_alpha_up=t[8:24, 0:GLA_K].reshape(1, GLA_R, GLA_K))


def _small_allreduce(small, w, m, v):
    def body(gm, gme, gf, gfi, ba, gg, bf, wau, ls, w_ref, m_ref, v_ref, g_ref, d_ref, m2_ref, v2_ref, l_ref,
             buf, send, recv):
        _, me = _peer(0)
        buf[me] = jnp.zeros((SMALL_ROWS, D), F32)
        for r, ref in enumerate((gm, gme, gf, gfi)):
            buf[me, r:r + 1, :] = jnp.sum(ref[...], axis=0, keepdims=True)
        buf[me, 4:5, 0:GLA_K] = jnp.sum(ba[...], axis=0, keepdims=True)
        buf[me, 4:5, GLA_K:GLA_K + GLA_V] = jnp.sum(gg[...], axis=0, keepdims=True)
        buf[me, 4:5, 768:896] = jnp.sum(bf[...], axis=0, keepdims=True)
        lrow = jnp.sum(ls[...], axis=0, keepdims=True)
        lsum = lrow[:, 0:128]
        for c in range(1, D // 128):
            lsum = lsum + lrow[:, 128 * c:128 * (c + 1)]
        buf[me, 4:5, 896:1024] = lsum
        buf[me, 8:24, 0:GLA_K] = wau[0:GLA_R, :]
        remote = []
        for d in range(1, N_DEV):
            to, me = _peer(d)
            cp = pltpu.make_async_remote_copy(src_ref=buf.at[me], dst_ref=buf.at[me], send_sem=send.at[d - 1],
                                              recv_sem=recv.at[d - 1], device_id=to, device_id_type=MESH)
            cp.start()
            remote.append(cp)
        for cp in remote:
            cp.wait_send()
        for cp in remote:
            cp.wait_recv()
        g = buf[0]
        for j in range(1, N_DEV):
            g = g + buf[j]
        g_ref[...] = g
        d_ref[...], m2_ref[...], v2_ref[...] = _adamw_math(g, w_ref[...], m_ref[...], v_ref[...])
        l_ref[...] = g[4:5, 896:1024]

    packed = jax.ShapeDtypeStruct((SMALL_ROWS, D), F32)
    return pl.pallas_call(
        body, name="small_allreduce",
        out_shape=[packed, packed, packed, packed, jax.ShapeDtypeStruct((1, 128), F32)],
        scratch_shapes=[pltpu.VMEM((N_DEV, SMALL_ROWS, D), F32), pltpu.SemaphoreType.DMA((N_DEV - 1,)),
                        pltpu.SemaphoreType.DMA((N_DEV - 1,))],
    )(small["g_mix"], small["g_mem"], small["g_ffn"], small["g_final"], small["b_alpha"], small["g_gla_head"],
      small["b_forget"], small["w_alpha_up"], small["loss"], w, m, v)


def _slabs(g, axis):
    R, C = g.shape
    if axis == 0:
        return g.reshape(N_DEV, R // N_DEV, C)
    return g.reshape(R, N_DEV, C // N_DEV).transpose(1, 0, 2)


def _unslab(t, axis):
    n, r, c = t.shape
    if axis == 0:
        return t.reshape(n * r, c)
    return t.transpose(1, 0, 2).reshape(r, n * c)


def kernel(x, mem, g_mix, w_in, w_alpha_up, b_alpha, b_forget, g_gla_head, g_mem, w_mem_kv, w_gla_o, w_fox_o, w_mem_o, w_out, g_ffn, w_ff1, w_ff2, g_final, loss_target, m_g_mix, m_w_in, m_w_alpha_up, m_b_alpha, m_b_forget, m_g_gla_head, m_g_mem, m_w_mem_kv, m_w_gla_o, m_w_fox_o, m_w_mem_o, m_w_out, m_g_ffn, m_w_ff1, m_w_ff2, m_g_final, v_g_mix, v_w_in, v_w_alpha_up, v_b_alpha, v_b_forget, v_g_gla_head, v_g_mem, v_w_mem_kv, v_w_gla_o, v_w_fox_o, v_w_mem_o, v_w_out, v_g_ffn, v_w_ff1, v_w_ff2, v_g_final):
    names = ["g_mix", "w_in", "w_alpha_up", "b_alpha", "b_forget", "g_gla_head", "g_mem", "w_mem_kv", "w_gla_o", "w_fox_o",
             "w_mem_o", "w_out", "g_ffn", "w_ff1", "w_ff2", "g_final"]
    w = dict(g_mix=g_mix, w_in=w_in, w_alpha_up=w_alpha_up, b_alpha=b_alpha, b_forget=b_forget, g_gla_head=g_gla_head,
             g_mem=g_mem, w_mem_kv=w_mem_kv, w_gla_o=w_gla_o, w_fox_o=w_fox_o, w_mem_o=w_mem_o, w_out=w_out, g_ffn=g_ffn,
             w_ff1=w_ff1, w_ff2=w_ff2, g_final=g_final)
    m = dict(g_mix=m_g_mix, w_in=m_w_in, w_alpha_up=m_w_alpha_up, b_alpha=m_b_alpha, b_forget=m_b_forget,
             g_gla_head=m_g_gla_head, g_mem=m_g_mem, w_mem_kv=m_w_mem_kv, w_gla_o=m_w_gla_o, w_fox_o=m_w_fox_o,
             w_mem_o=m_w_mem_o, w_out=m_w_out, g_ffn=m_g_ffn, w_ff1=m_w_ff1, w_ff2=m_w_ff2, g_final=m_g_final)
    v = dict(g_mix=v_g_mix, w_in=v_w_in, w_alpha_up=v_w_alpha_up, b_alpha=v_b_alpha, b_forget=v_b_forget,
             g_gla_head=v_g_gla_head, g_mem=v_g_mem, w_mem_kv=v_w_mem_kv, w_gla_o=v_w_gla_o, w_fox_o=v_w_fox_o,
             w_mem_o=v_w_mem_o, w_out=v_w_out, g_ffn=v_g_ffn, w_ff1=v_w_ff1, w_ff2=v_w_ff2, g_final=v_g_final)
    me = lax.axis_index("x") * 4 + lax.axis_index("y") * 2 + lax.axis_index("c")

    shard = lambda n: w[n][0].astype(BF16)
    w_in_all, w_au_all = _gather_weights([shard("w_in"), shard("w_alpha_up")])
    p = dict(w_in=_unslab(w_in_all, 1), w_alpha_up=_unslab(w_au_all, 1), g_mix=g_mix, b_alpha=b_alpha, b_forget=b_forget,
             g_gla_head=g_gla_head, g_mem=g_mem, g_ffn=g_ffn, g_final=g_final)

    dx, big, small = _local_step(x[0], mem[0], loss_target[0], p, [shard(n) for n, _ in BIG[1:]])

    out_g, out_d, out_m, out_v = {}, {}, {}, {}
    for n, _ in BIG:
        g_, d_, m_, v_ = _adamw_sum(big[n], w[n][0], m[n][0], v[n][0], "adamw_" + n)
        out_g[n], out_d[n], out_m[n], out_v[n] = g_[None], d_[None], m_[None], v_[None]

    full = lambda d: dict(d, w_alpha_up=jnp.zeros((1, GLA_R, GLA_K), F32))
    gs, ds, ms, vs, lrow = _small_allreduce(small, _pack_small(full(w)), _pack_small(full(m)), _pack_small(full(v)))
    g_s, d_s, m_s, v_s = _unpack_small(gs), _unpack_small(ds), _unpack_small(ms), _unpack_small(vs)
    for n in names:
        if n not in out_g and n != "w_alpha_up":
            out_g[n], out_d[n], out_m[n], out_v[n] = g_s[n], d_s[n], m_s[n], v_s[n]
    g_au = lax.dynamic_slice_in_dim(g_s["w_alpha_up"][0], me * (GLA_K // N_DEV), GLA_K // N_DEV, axis=1)
    g_, d_, m_, v_ = _adamw_sum(g_au[None], w_alpha_up[0], m_w_alpha_up[0], v_w_alpha_up[0], "adamw_w_alpha_up")
    out_g["w_alpha_up"], out_d["w_alpha_up"], out_m["w_alpha_up"], out_v["w_alpha_up"] = g_[None], d_[None], m_[None], v_[None]

    loss = jnp.sum(lrow) * (0.5 / D)
    return (loss, dx[None], *[out_g[n] for n in names], *[out_d[n] for n in names], *[out_m[n] for n in names],
            *[out_v[n] for n in names])
```

```python
import jax
import jax.numpy as jnp
from jax import lax
from jax.experimental import pallas as pl
from jax.experimental.pallas import tpu as pltpu

F32, BF16 = jnp.float32, jnp.bfloat16
HIGHEST = lax.Precision.HIGHEST
MESH = pl.DeviceIdType.MESH

N_DEV = 8
D = 1024
EPS = 1e-6
CHUNK = 64
N_MEM = 256
GLA_H, GLA_DK, GLA_DV = 4, 64, 128
GLA_K, GLA_V, GLA_R = 256, 512, 16
FOX_H, FOX_DH, FOX_W = 8, 64, 512
MEM_H, MEM_DH, MEM_W = 4, 128, 512
D_FF = 4096
D_IN = 6680
FOX_SCALE = 0.125
GLA_SCALE = 0.125
MEM_SCALE = MEM_DH ** -0.5
GLA_TAU_INV = 1.0 / 16.0
NEG = -1e30

O_GQ, O_GK, O_GV, O_GG, O_GA, O_FQ, O_FK, O_FV, O_FF, O_MQ, O_GT = 0, 256, 512, 1024, 1536, 1552, 2064, 2576, 3088, 3096, 3608
A_FQ, A_FK, A_FV, A_MQ, A_W = 1024, 2048, 3072, 3584, 4096
S_W = 640
G_W = 3072
P_FOX, P_FOX_W, P_MQ, P_GT, P_GLA, P_GLA_W, P_W = 0, 2560, 2560, 3072, 6144, 2048, 8192
FF_LANE = 16
AUG = 64
FOX_LIVE = 80
FOX_VT = 80

ADAM_LR, ADAM_B1, ADAM_B2, ADAM_EPS, ADAM_WD, ADAM_STEP = 0.001, 0.9, 0.999, 1e-08, 0.01, 10
V7X_VMEM_LIMIT = 52 * 1024 * 1024
FOX_TK = 512
FOX_TQ = 2048
FOX_BWD_WIDE = 1024


def _params(*sem):
    return pltpu.CompilerParams(dimension_semantics=sem, vmem_limit_bytes=V7X_VMEM_LIMIT)


def _nt(a, b):
    return lax.dot_general(a, b, (((1,), (1,)), ((), ())), preferred_element_type=F32)


def _tn(a, b):
    return lax.dot_general(a, b, (((0,), (0,)), ((), ())), preferred_element_type=F32)


def _nn(a, b):
    return jnp.dot(a, b, preferred_element_type=F32)


def _log_sigmoid(z):
    return jnp.minimum(z, 0.0) - jnp.log(1.0 + jnp.exp(-jnp.abs(z)))


def _sum01(m01, x):
    x1 = x.astype(BF16)
    x2 = (x - x1.astype(F32)).astype(BF16)
    x3 = (x - x1.astype(F32) - x2.astype(F32)).astype(BF16)
    return _nn(m01, x1) + _nn(m01, x2) + _nn(m01, x3)


def _sum8(x):
    return x.reshape(x.shape[0] // 8, 8, x.shape[1]).sum(axis=0)


def _rms(xv):
    r = lax.rsqrt(jnp.mean(xv * xv, axis=-1, keepdims=True) + EPS)
    return r, xv * r


def _rms_bwd(du, g, r, xh):
    w = du * g
    return r * (w - xh * jnp.mean(w * xh, axis=-1, keepdims=True))


def _row_chunks(n, size=256):
    return [slice(r, r + min(size, n)) for r in range(0, n, min(size, n))]


def _tile(n, pref):
    t = min(n, pref)
    assert n % t == 0, (n, t)
    return t


def _proj(x, g, wa, wg, ws, shards):
    S = x.shape[0]
    tm, tn = _tile(S, 1024), 1024
    n_a, n_g = A_W // tn, G_W // tn
    n_i, n_j = S // tm, n_a + n_g + 1
    n_x = len(shards)

    def body(*refs):
        x_ref, g_ref, wa_ref, wg_ref, ws_ref = refs[:5]
        pa_ref, pg_ref, ps_ref, u_ref = refs[5 + n_x:9 + n_x]
        u_s = refs[9 + 2 * n_x]
        gather = lambda: _AllToAll(refs[5:5 + n_x], refs[9 + n_x:9 + 2 * n_x], refs[10 + 2 * n_x:], True)
        i, j = pl.program_id(0), pl.program_id(1)

        @pl.when((i == 0) & (j == 0))
        def _():
            gather().start()

        @pl.when(j == 0)
        def _():
            r, xh = _rms(x_ref[...])
            u_s[...] = (xh * g_ref[...]).astype(BF16)
            u_ref[...] = u_s[...]

        @pl.when(j < n_a)
        def _():
            pa_ref[...] = _nn(u_s[...], wa_ref[...]).astype(BF16)

        @pl.when((j >= n_a) & (j < n_a + n_g))
        def _():
            pg_ref[...] = _nn(u_s[...], wg_ref[...]).astype(BF16)

        @pl.when(j == n_a + n_g)
        def _():
            ps_ref[...] = _nn(u_s[...], ws_ref[...])

        @pl.when((i == n_i - 1) & (j == n_j - 1))
        def _():
            gather().wait()

    in_a = lambda j: jnp.minimum(j, n_a - 1)
    in_g = lambda j: jnp.clip(j - n_a, 0, n_g - 1)
    row = pl.BlockSpec((tm, D), lambda i, j: (i, 0))
    any_spec = pl.BlockSpec(memory_space=pl.ANY)
    out = pl.pallas_call(
        body, name="proj", grid=(n_i, n_j),
        in_specs=[row, pl.BlockSpec((1, D), lambda i, j: (0, 0)), pl.BlockSpec((D, tn), lambda i, j: (0, in_a(j))),
                  pl.BlockSpec((D, tn), lambda i, j: (0, in_g(j))),
                  pl.BlockSpec((D, S_W), lambda i, j: (0, 0), pipeline_mode=pl.Buffered(1))] + [any_spec] * n_x,
        out_specs=[pl.BlockSpec((tm, tn), lambda i, j: (i, in_a(j))), pl.BlockSpec((tm, tn), lambda i, j: (i, in_g(j))),
                   pl.BlockSpec((tm, S_W), lambda i, j: (i, 0)), row] + [any_spec] * n_x,
        out_shape=[jax.ShapeDtypeStruct((S, A_W), BF16), jax.ShapeDtypeStruct((S, G_W), BF16),
                   jax.ShapeDtypeStruct((S, S_W), F32), jax.ShapeDtypeStruct((S, D), BF16)] + _gathered_shapes(shards),
        scratch_shapes=[pltpu.VMEM((tm, D), BF16)] + _exchange_sems(n_x),
        compiler_params=_params("arbitrary", "arbitrary"),
    )(x, g, wa, wg, ws, *shards)
    return out[0], out[1], out[2], out[3], out[4:]


def _wgrad(a, b, name, slab_axis=None):
    S, Ka = a.shape
    N = b.shape[1]
    tka, tn, ts = _tile(Ka, 1024), _tile(N, 1024), _tile(S, 1024)
    n_s = S // ts
    per = N // N_DEV
    slabs_per_step = tn // per

    def body(a_ref, b_ref, o_ref, acc):
        s = pl.program_id(2)

        @pl.when(s == 0)
        def _():
            acc[...] = jnp.zeros_like(acc)

        acc[...] += _tn(a_ref[...].astype(BF16), b_ref[...].astype(BF16))

        @pl.when(s == n_s - 1)
        def _():
            if slab_axis == 1:
                for q in range(slabs_per_step):
                    o_ref[q] = acc[:, per * q:per * (q + 1)].astype(BF16)
            else:
                o_ref[...] = acc[...].astype(o_ref.dtype)

    if slab_axis == 1:
        out_spec = pl.BlockSpec((slabs_per_step, tka, per), lambda i, j, s: (j, i, 0))
        out_shape = jax.ShapeDtypeStruct((N_DEV, Ka, per), BF16)
    else:
        out_spec = pl.BlockSpec((tka, tn), lambda i, j, s: (i, j))
        out_shape = jax.ShapeDtypeStruct((Ka, N), F32 if slab_axis is None else BF16)
    out = pl.pallas_call(
        body, name=name, grid=(Ka // tka, N // tn, n_s),
        in_specs=[pl.BlockSpec((ts, tka), lambda i, j, s: (s, i)), pl.BlockSpec((ts, tn), lambda i, j, s: (s, j))],
        out_specs=out_spec, out_shape=out_shape,
        scratch_shapes=[pltpu.VMEM((tka, tn), F32)],
        compiler_params=_params("arbitrary", "arbitrary", "arbitrary"),
    )(a, b)
    return out.reshape(N_DEV, Ka // N_DEV, N) if slab_axis == 0 else out


def _nt_rmsbwd(a, w, xin, g, dres, name, emit_bf16, slabs=()):
    S, K = a.shape
    tm, tk = _tile(S, 1024), _tile(K, 1024)
    n_i, n_k = S // tm, K // tk
    n_x, n_o = len(slabs), 3 if emit_bf16 else 2

    def body(*refs):
        a_ref, w_ref, x_ref, g_ref, r_ref = refs[:5]
        o_ref = refs[5 + n_x]
        rest = refs[6 + n_x:5 + n_x + n_o] + (refs[5 + 2 * n_x + n_o],)
        dg_ref, acc = rest[-2], rest[-1]
        scatter = lambda: _AllToAll(refs[5:5 + n_x], refs[5 + n_x + n_o:5 + 2 * n_x + n_o], refs[6 + 2 * n_x + n_o:], False)
        i, k = pl.program_id(0), pl.program_id(1)

        if n_x:
            @pl.when((i == 0) & (k == 0))
            def _():
                scatter().start()

        @pl.when(k == 0)
        def _():
            acc[...] = jnp.zeros_like(acc)

        acc[...] += _nt(a_ref[...], w_ref[...])

        @pl.when(k == n_k - 1)
        def _():
            @pl.when(i == 0)
            def _():
                dg_ref[...] = jnp.zeros_like(dg_ref)

            for rows in _row_chunks(tm):
                du = acc[rows, :]
                r, xh = _rms(x_ref[rows, :])
                out = r_ref[rows, :] + _rms_bwd(du, g_ref[...], r, xh)
                o_ref[rows, :] = out
                if emit_bf16:
                    rest[0][rows, :] = out.astype(BF16)
                dg_ref[...] += _sum8(du * xh)

        if n_x:
            @pl.when((i == n_i - 1) & (k == n_k - 1))
            def _():
                scatter().wait()

    row = pl.BlockSpec((tm, D), lambda i, k: (i, 0))
    any_spec = pl.BlockSpec(memory_space=pl.ANY)
    out_shape = [jax.ShapeDtypeStruct((S, D), F32)]
    out_specs = [row]
    if emit_bf16:
        out_shape.append(jax.ShapeDtypeStruct((S, D), BF16))
        out_specs.append(row)
    out_shape.append(jax.ShapeDtypeStruct((8, D), F32))
    out_specs.append(pl.BlockSpec((8, D), lambda i, k: (0, 0)))
    out = pl.pallas_call(
        body, name=name, grid=(n_i, n_k),
        in_specs=[pl.BlockSpec((tm, tk), lambda i, k: (i, k)), pl.BlockSpec((D, tk), lambda i, k: (0, k)),
                  row, pl.BlockSpec((1, D), lambda i, k: (0, 0)), row] + [any_spec] * n_x,
        out_specs=out_specs + [any_spec] * n_x,
        out_shape=out_shape + [jax.ShapeDtypeStruct(b.shape, b.dtype) for b in slabs],
        scratch_shapes=[pltpu.VMEM((tm, D), F32)] + (_exchange_sems(n_x) if n_x else []),
        compiler_params=_params("arbitrary", "arbitrary"),
    )(a, w, xin, g, dres, *slabs)
    return (*out[:n_o], out[n_o:]) if n_x else out


def _merge(og, ofox, omem, wg, wf, wm, pg):
    S = og.shape[0]
    tm = _tile(S, 512)

    def body(og_ref, of_ref, om_ref, wg_ref, wf_ref, wm_ref, pg_ref, y_ref, mg_ref):
        tot = None
        for i, (o_ref, w_ref) in enumerate(((og_ref, wg_ref), (of_ref, wf_ref), (om_ref, wm_ref))):
            y = _nn(o_ref[...].astype(BF16), w_ref[...])
            y_ref[i] = y.astype(BF16)
            t = jax.nn.sigmoid(pg_ref[:, D * i:D * (i + 1)].astype(F32)) * y
            tot = t if tot is None else tot + t
        mg_ref[...] = tot.astype(BF16)

    o_spec = pl.BlockSpec((tm, 512), lambda i: (i, 0))
    w_spec = pl.BlockSpec((512, D), lambda i: (0, 0))
    return pl.pallas_call(
        body, name="merge", grid=(S // tm,),
        in_specs=[o_spec, o_spec, o_spec, w_spec, w_spec, w_spec, pl.BlockSpec((tm, G_W), lambda i: (i, 0))],
        out_specs=[pl.BlockSpec((3, tm, D), lambda i: (0, i, 0)), pl.BlockSpec((tm, D), lambda i: (i, 0))],
        out_shape=[jax.ShapeDtypeStruct((3, S, D), BF16), jax.ShapeDtypeStruct((S, D), BF16)],
        compiler_params=_params("arbitrary"),
    )(og, ofox, omem, wg, wf, wm, pg)


def _out_proj(mg, w_out, x, g_ffn):
    S = x.shape[0]
    tm = _tile(S, 512)

    def body(mg_ref, w_ref, x_ref, g_ref, h_ref, u_ref):
        h = x_ref[...] + _nn(mg_ref[...], w_ref[...])
        h_ref[...] = h
        r, xh = _rms(h)
        u_ref[...] = (xh * g_ref[...]).astype(BF16)

    row = pl.BlockSpec((tm, D), lambda i: (i, 0))
    return pl.pallas_call(
        body, name="out_proj", grid=(S // tm,),
        in_specs=[row, pl.BlockSpec((D, D), lambda i: (0, 0)), row, pl.BlockSpec((1, D), lambda i: (0, 0))],
        out_specs=[row, row],
        out_shape=[jax.ShapeDtypeStruct((S, D), F32), jax.ShapeDtypeStruct((S, D), BF16)],
        compiler_params=_params("arbitrary"),
    )(mg, w_out, x, g_ffn)


def _ff1(u2, w1):
    S = u2.shape[0]
    tm, tn = _tile(S, 1024), 1024

    def body(u_ref, w_ref, a_ref, act_ref):
        a = _nn(u_ref[...], w_ref[...])
        a_ref[...] = a.astype(BF16)
        act_ref[...] = jnp.square(jnp.maximum(a, 0.0)).astype(BF16)

    blk = pl.BlockSpec((tm, tn), lambda i, j: (i, j))
    return pl.pallas_call(
        body, name="ff1", grid=(S // tm, D_FF // tn),
        in_specs=[pl.BlockSpec((tm, D), lambda i, j: (i, 0)), pl.BlockSpec((D, tn), lambda i, j: (0, j))],
        out_specs=[blk, blk],
        out_shape=[jax.ShapeDtypeStruct((S, D_FF), BF16), jax.ShapeDtypeStruct((S, D_FF), BF16)],
        compiler_params=_params("arbitrary", "arbitrary"),
    )(u2, w1)


def _ff2_loss(act, w2, h1, g_final, target):
    S = act.shape[0]
    tm, tk = _tile(S, 1024), 1024
    n_k = D_FF // tk

    def body(a_ref, w_ref, h_ref, g_ref, t_ref, d_ref, db_ref, ls_ref, dg_ref, acc):
        i, k = pl.program_id(0), pl.program_id(1)

        @pl.when(k == 0)
        def _():
            acc[...] = jnp.zeros_like(acc)

        acc[...] += _nn(a_ref[...], w_ref[...])

        @pl.when(k == n_k - 1)
        def _():
            @pl.when(i == 0)
            def _():
                ls_ref[...] = jnp.zeros_like(ls_ref)
                dg_ref[...] = jnp.zeros_like(dg_ref)

            gf = g_ref[...]
            for rows in _row_chunks(tm):
                r, xh = _rms(h_ref[rows, :] + acc[rows, :])
                err = xh * gf - t_ref[rows, :]
                dy = err * (1.0 / D)
                dh = _rms_bwd(dy, gf, r, xh)
                d_ref[rows, :] = dh
                db_ref[rows, :] = dh.astype(BF16)
                ls_ref[...] += _sum8(err * err)
                dg_ref[...] += _sum8(dy * xh)

    row = pl.BlockSpec((tm, D), lambda i, k: (i, 0))
    part = pl.BlockSpec((8, D), lambda i, k: (0, 0))
    return pl.pallas_call(
        body, name="ff2_loss", grid=(S // tm, n_k),
        in_specs=[pl.BlockSpec((tm, tk), lambda i, k: (i, k)), pl.BlockSpec((tk, D), lambda i, k: (k, 0)),
                  row, pl.BlockSpec((1, D), lambda i, k: (0, 0)), row],
        out_specs=[row, row, part, part],
        out_shape=[jax.ShapeDtypeStruct((S, D), F32), jax.ShapeDtypeStruct((S, D), BF16),
                   jax.ShapeDtypeStruct((8, D), F32), jax.ShapeDtypeStruct((8, D), F32)],
        scratch_shapes=[pltpu.VMEM((tm, D), F32)],
        compiler_params=_params("arbitrary", "arbitrary"),
    )(act, w2, h1, g_final, target)


def _dact(dh2b, w2, a):
    S = a.shape[0]
    tm, tn = _tile(S, 1024), 1024

    def body(d_ref, w_ref, a_ref, o_ref):
        da = _nt(d_ref[...], w_ref[...])
        o_ref[...] = (da * (2.0 * jnp.maximum(a_ref[...].astype(F32), 0.0))).astype(BF16)

    blk = pl.BlockSpec((tm, tn), lambda i, j: (i, j))
    return pl.pallas_call(
        body, name="dact", grid=(S // tm, D_FF // tn),
        in_specs=[pl.BlockSpec((tm, D), lambda i, j: (i, 0)), pl.BlockSpec((tn, D), lambda i, j: (j, 0)), blk],
        out_specs=blk, out_shape=jax.ShapeDtypeStruct((S, D_FF), BF16),
        compiler_params=_params("arbitrary", "arbitrary"),
    )(dh2b, w2, a)


def _dmerge(dh1b, w_out, pg, y3, wg, wf, wm):
    S = dh1b.shape[0]
    tm = _tile(S, 512)

    def body(d_ref, w_ref, pg_ref, y_ref, wg_ref, wf_ref, wm_ref, *outs):
        dy_refs, do_refs, dg_ref = outs[0:3], outs[3:6], outs[6]
        dm = _nt(d_ref[...], w_ref[...])
        for i, wo_ref in enumerate((wg_ref, wf_ref, wm_ref)):
            gt = jax.nn.sigmoid(pg_ref[:, D * i:D * (i + 1)].astype(F32))
            dy = (dm * gt).astype(BF16)
            dy_refs[i][...] = dy
            do_refs[i][...] = _nt(dy, wo_ref[...])
            dg_ref[:, D * i:D * (i + 1)] = (dm * y_ref[i].astype(F32) * (gt * (1.0 - gt))).astype(BF16)

    row = pl.BlockSpec((tm, D), lambda i: (i, 0))
    half = pl.BlockSpec((tm, 512), lambda i: (i, 0))
    w_spec = pl.BlockSpec((512, D), lambda i: (0, 0))
    return pl.pallas_call(
        body, name="dmerge", grid=(S // tm,),
        in_specs=[row, pl.BlockSpec((D, D), lambda i: (0, 0)), pl.BlockSpec((tm, G_W), lambda i: (i, 0)),
                  pl.BlockSpec((3, tm, D), lambda i: (0, i, 0)), w_spec, w_spec, w_spec],
        out_specs=[row, row, row, half, half, half, pl.BlockSpec((tm, G_W), lambda i: (i, P_GT // G_W))],
        out_shape=[jax.ShapeDtypeStruct((S, D), BF16)] * 3 + [jax.ShapeDtypeStruct((S, 512), F32)] * 3
        + [jax.ShapeDtypeStruct((S, P_W), BF16)],
        compiler_params=_params("arbitrary"),
    )(dh1b, w_out, pg, y3, wg, wf, wm)


def _gla_block_terms(gq_ref, gk_ref, ps_ref, wau_ref, ba_ref, tb):
    gaff = ps_ref[:, 512:640]
    z = _nn(gaff.astype(BF16), wau_ref[...]) + ba_ref[...]
    la = _log_sigmoid(z) * GLA_TAU_INV
    rr = lax.broadcasted_iota(jnp.int32, (tb, tb), 0)
    cc = lax.broadcasted_iota(jnp.int32, (tb, tb), 1)
    same = jnp.right_shift(rr, 6) == jnp.right_shift(cc, 6)
    tri = jnp.where(same & (cc <= rr), 1.0, 0.0).astype(BF16)
    ones = jnp.where(same, 1.0, 0.0).astype(BF16)
    b = _sum01(tri, la)
    bl = _sum01(ones, la)
    e_pos, e_neg, e_last, dec = jnp.exp(b), jnp.exp(-b), jnp.exp(bl - b), jnp.exp(bl)
    q = gq_ref[...].astype(F32) * GLA_SCALE
    k = gk_ref[...].astype(F32)
    return dict(gaff=gaff, z=z, same=same, rr=rr, cc=cc, ones=ones, e_pos=e_pos, e_neg=e_neg, e_last=e_last, dec=dec,
                qp=q * e_pos, qn=q * e_neg, kn=k * e_neg, kp=k * e_pos, kd=k * e_last)


def _head_masked(x, store):
    lane = lax.broadcasted_iota(jnp.int32, x.shape, 1)
    for h in range(GLA_H):
        store[:, h] = jnp.where(jnp.right_shift(lane, 6) == h, x, 0.0).astype(BF16).reshape(-1, CHUNK, GLA_K)


def _lower4():
    t = jnp.bitwise_and(lax.broadcasted_iota(jnp.int32, (GLA_H * CHUNK, CHUNK), 0), CHUNK - 1)
    return t >= lax.broadcasted_iota(jnp.int32, (GLA_H * CHUNK, CHUNK), 1)


def _stack_heads(ref, rows):
    return jnp.concatenate([ref[rows, GLA_DV * h:GLA_DV * (h + 1)] for h in range(GLA_H)], axis=0)


def _gla_fwd(pa, ps, wau, ba, gh):
    S = pa.shape[0]
    tb = _tile(S, 512)
    n_c = tb // CHUNK
    n_b = S // tb

    def body(gq_ref, gk_ref, gv_ref, ps_ref, wau_ref, ba_ref, gh_ref, o_ref, og_ref, sp_ref,
             qpm, qnm, kdm, kn_s, kp_s, dec_s, state):
        @pl.when(pl.program_id(0) == 0)
        def _():
            state[...] = jnp.zeros_like(state)

        t = _gla_block_terms(gq_ref, gk_ref, ps_ref, wau_ref, ba_ref, tb)
        _head_masked(t["qp"], qpm)
        _head_masked(t["qn"], qnm)
        _head_masked(t["kd"], kdm)
        kn_s[...] = t["kn"].astype(BF16)
        kp_s[...] = t["kp"].astype(BF16)
        dec_s[...] = t["dec"]
        lower = _lower4()

        sp = state[...]
        for c in range(n_c):
            rows = slice(c * CHUNK, (c + 1) * CHUNK)
            sp_ref[c] = sp
            qp, qn, kd = (s[c].reshape(GLA_H * CHUNK, GLA_K) for s in (qpm, qnm, kdm))
            attn = jnp.where(lower, _nt(qp, kn_s[rows, :]), _nt(qn, kp_s[rows, :])).astype(BF16)
            inter = _nt(qp, sp.astype(BF16))
            for h in range(GLA_H):
                mine = slice(CHUNK * h, CHUNK * (h + 1))
                cols = slice(GLA_DV * h, GLA_DV * (h + 1))
                o_ref[rows, cols] = _nn(attn[mine], gv_ref[rows, cols]) + inter[mine]
            sp = sp * dec_s[c * CHUNK:c * CHUNK + 1, :] + _tn(_stack_heads(gv_ref, rows), kd)
        state[...] = sp
        for h in range(GLA_H):
            cols = slice(GLA_DV * h, GLA_DV * (h + 1))
            r, xh = _rms(o_ref[:, cols])
            gg = ps_ref[:, cols]
            og_ref[:, cols] = ((xh * gh_ref[:, cols]) * (gg * jax.nn.sigmoid(gg))).astype(BF16)

    return pl.pallas_call(
        body, name="gla_fwd", grid=(n_b,),
        in_specs=[pl.BlockSpec((tb, GLA_K), lambda i: (i, 0)), pl.BlockSpec((tb, GLA_K), lambda i: (i, 1)),
                  pl.BlockSpec((tb, GLA_V), lambda i: (i, 1)), pl.BlockSpec((tb, S_W), lambda i: (i, 0)),
                  pl.BlockSpec((128, GLA_K), lambda i: (0, 0)), pl.BlockSpec((1, GLA_K), lambda i: (0, 0)),
                  pl.BlockSpec((1, GLA_V), lambda i: (0, 0))],
        out_specs=[pl.BlockSpec((tb, GLA_V), lambda i: (i, 0)), pl.BlockSpec((tb, GLA_V), lambda i: (i, 0)),
                   pl.BlockSpec((n_c, GLA_DV, GLA_K), lambda i: (i, 0, 0))],
        out_shape=[jax.ShapeDtypeStruct((S, GLA_V), F32), jax.ShapeDtypeStruct((S, GLA_V), BF16),
                   jax.ShapeDtypeStruct((S // CHUNK, GLA_DV, GLA_K), F32)],
        scratch_shapes=[pltpu.VMEM((n_c, GLA_H, CHUNK, GLA_K), BF16)] * 3
        + [pltpu.VMEM((tb, GLA_K), BF16), pltpu.VMEM((tb, GLA_K), BF16), pltpu.VMEM((tb, GLA_K), F32),
           pltpu.VMEM((GLA_DV, GLA_K), F32)],
        compiler_params=_params("arbitrary"),
    )(pa, pa, pa, ps, wau, ba, gh)


def _gla_bwd(pa, ps, wau, ba, gh, o_gla, d_og, sprev, dgaff_fox, d_proj, slabs):
    S = pa.shape[0]
    tb = _tile(S, 512)
    n_c = tb // CHUNK
    n_b = S // tb
    n_x = len(slabs)
    c_gk, c_gv, c_gg, c_ga, c_end = GLA_K, 2 * GLA_K, 2 * GLA_K + GLA_V, 2 * GLA_K + 2 * GLA_V, 2 * GLA_K + 2 * GLA_V + 128

    def body(*refs):
        gq_ref, gk_ref, gv_ref, ps_ref, wau_ref, ba_ref, gh_ref, o_ref, dog_ref, sp_ref, dfx_ref = refs[:11]
        dp_ref, dwau_ref, dba_ref, dgh_ref = refs[12 + n_x:16 + n_x]
        (qpm, qnm, kdm, kn_s, kp_s, dec_s, do_s, dqp_s, dqn_s, dkn_s, dkp_s, dkd_s, ddec_s,
         dstate) = refs[16 + 2 * n_x:30 + 2 * n_x]
        scatter = lambda: _AllToAll(refs[12:12 + n_x], refs[16 + n_x:16 + 2 * n_x], refs[30 + 2 * n_x:], False)
        first = pl.program_id(0) == 0
        dp_ref[:, c_end:] = jnp.zeros((tb, P_GLA_W - c_end), BF16)

        @pl.when(first)
        def _():
            dstate[...] = jnp.zeros_like(dstate)
            scatter().start()

        t = _gla_block_terms(gq_ref, gk_ref, ps_ref, wau_ref, ba_ref, tb)
        _head_masked(t["qp"], qpm)
        _head_masked(t["qn"], qnm)
        _head_masked(t["kd"], kdm)
        kn_s[...] = t["kn"].astype(BF16)
        kp_s[...] = t["kp"].astype(BF16)
        dec_s[...] = t["dec"]

        dgh_parts = []
        for h in range(GLA_H):
            cols = slice(GLA_DV * h, GLA_DV * (h + 1))
            r, xh = _rms(o_ref[:, cols])
            g = gh_ref[:, cols]
            gg = ps_ref[:, cols]
            sg = jax.nn.sigmoid(gg)
            d_out = dog_ref[:, cols]
            dp_ref[:, c_gg + GLA_DV * h:c_gg + GLA_DV * (h + 1)] = (d_out * (xh * g) * (sg * (1.0 + gg * (1.0 - sg)))).astype(BF16)
            d_on = d_out * (gg * sg)
            dgh_parts.append(_sum8(d_on * xh))
            do_s[:, cols] = _rms_bwd(d_on, g, r, xh).astype(BF16)
        dgh_part = jnp.concatenate(dgh_parts, axis=1)

        lower = _lower4()
        lane = lax.broadcasted_iota(jnp.int32, (CHUNK, GLA_K), 1)

        def own_columns(stacked):
            return sum(jnp.where(jnp.right_shift(lane, 6) == h, stacked[CHUNK * h:CHUNK * (h + 1)], 0.0) for h in range(GLA_H))

        ds_next = dstate[...]
        for c in reversed(range(n_c)):
            rows = slice(c * CHUNK, (c + 1) * CHUNK)
            dsb = ds_next.astype(BF16)
            sp = sp_ref[c]
            knc, kpc = kn_s[rows, :], kp_s[rows, :]
            qp, qn, kd = (s[c].reshape(GLA_H * CHUNK, GLA_K) for s in (qpm, qnm, kdm))
            v4, do4 = _stack_heads(gv_ref, rows), _stack_heads(do_s, rows)
            ddec_s[rows, :] = jnp.broadcast_to(jnp.sum(ds_next * sp, axis=0, keepdims=True), (CHUNK, GLA_K))
            attn = jnp.where(lower, _nt(qp, knc), _nt(qn, kpc)).astype(BF16)
            da = jnp.concatenate([_nt(do4[CHUNK * h:CHUNK * (h + 1)], v4[CHUNK * h:CHUNK * (h + 1)]) for h in range(GLA_H)],
                                 axis=0)
            dac = jnp.where(lower, da, 0.0).astype(BF16)
            daa = jnp.where(lower, 0.0, da).astype(BF16)
            dqp_s[rows, :] = own_columns(_nn(dac, knc) + _nn(do4, sp.astype(BF16)))
            dqn_s[rows, :] = own_columns(_nn(daa, kpc))
            dkd_s[rows, :] = own_columns(_nn(v4, dsb))
            dkn_s[rows, :] = _tn(dac, qp)
            dkp_s[rows, :] = _tn(daa, qn)
            dv_state = _nt(kd, dsb)
            for h in range(GLA_H):
                mine = slice(CHUNK * h, CHUNK * (h + 1))
                dp_ref[rows, c_gv + GLA_DV * h:c_gv + GLA_DV * (h + 1)] = (_tn(attn[mine], do4[mine]) + dv_state[mine]).astype(BF16)
            ds_next = ds_next * dec_s[c * CHUNK:c * CHUNK + 1, :] + _tn(do4, qp)
        dstate[...] = ds_next

        dqp, dqn, dkn, dkp, dkd = dqp_s[...], dqn_s[...], dkn_s[...], dkp_s[...], dkd_s[...]
        dp_ref[:, 0:c_gk] = ((dqp * t["e_pos"] + dqn * t["e_neg"]) * GLA_SCALE).astype(BF16)
        dp_ref[:, c_gk:c_gv] = (dkn * t["e_neg"] + dkp * t["e_pos"] + dkd * t["e_last"]).astype(BF16)
        kd_term = dkd * t["kd"]
        db = dqp * t["qp"] - dqn * t["qn"] - dkn * t["kn"] + dkp * t["kp"] - kd_term
        upper = jnp.where(t["same"] & (t["cc"] >= t["rr"]), 1.0, 0.0).astype(BF16)
        dla = (_sum01(upper, db) + _sum01(t["ones"], kd_term)
               + ddec_s[...] * t["dec"])
        dz = dla * GLA_TAU_INV * jax.nn.sigmoid(-t["z"])
        dzb = dz.astype(BF16)
        dp_ref[:, c_ga:c_end] = (_nt(dzb, wau_ref[...]) + dfx_ref[...]).astype(BF16)
        dwau_part = _tn(t["gaff"].astype(BF16), dzb)
        dba_part = _sum8(dz)

        @pl.when(first)
        def _():
            dwau_ref[...] = dwau_part
            dba_ref[...] = dba_part
            dgh_ref[...] = dgh_part

        @pl.when(jnp.logical_not(first))
        def _():
            dwau_ref[...] += dwau_part
            dba_ref[...] += dba_part
            dgh_ref[...] += dgh_part

        @pl.when(pl.program_id(0) == n_b - 1)
        def _():
            scatter().wait()

    rev = lambda i: (n_b - 1 - i, 0)
    f32k = pltpu.VMEM((tb, GLA_K), F32)
    bf4 = pltpu.VMEM((n_c, GLA_H, CHUNK, GLA_K), BF16)
    any_spec = pl.BlockSpec(memory_space=pl.ANY)
    out = pl.pallas_call(
        body, name="gla_bwd", grid=(n_b,),
        in_specs=[pl.BlockSpec((tb, GLA_K), rev), pl.BlockSpec((tb, GLA_K), lambda i: (n_b - 1 - i, 1)),
                  pl.BlockSpec((tb, GLA_V), lambda i: (n_b - 1 - i, 1)), pl.BlockSpec((tb, S_W), rev),
                  pl.BlockSpec((128, GLA_K), lambda i: (0, 0)), pl.BlockSpec((1, GLA_K), lambda i: (0, 0)),
                  pl.BlockSpec((1, GLA_V), lambda i: (0, 0)), pl.BlockSpec((tb, GLA_V), rev), pl.BlockSpec((tb, GLA_V), rev),
                  pl.BlockSpec((n_c, GLA_DV, GLA_K), lambda i: (n_b - 1 - i, 0, 0)), pl.BlockSpec((tb, 128), rev),
                  any_spec] + [any_spec] * n_x,
        out_specs=[pl.BlockSpec((tb, P_GLA_W), lambda i: (n_b - 1 - i, P_GLA // P_GLA_W)),
                   pl.BlockSpec((128, GLA_K), lambda i: (0, 0)), pl.BlockSpec((8, GLA_K), lambda i: (0, 0)),
                   pl.BlockSpec((8, GLA_V), lambda i: (0, 0))] + [any_spec] * n_x,
        out_shape=[jax.ShapeDtypeStruct((S, P_W), BF16), jax.ShapeDtypeStruct((128, GLA_K), F32),
                   jax.ShapeDtypeStruct((8, GLA_K), F32), jax.ShapeDtypeStruct((8, GLA_V), F32)]
        + [jax.ShapeDtypeStruct(b.shape, b.dtype) for b in slabs],
        input_output_aliases={11: 0},
        scratch_shapes=[bf4, bf4, bf4, pltpu.VMEM((tb, GLA_K), BF16), pltpu.VMEM((tb, GLA_K), BF16), f32k,
                        pltpu.VMEM((tb, GLA_V), BF16), f32k, f32k, f32k, f32k, f32k, f32k, pltpu.VMEM((GLA_DV, GLA_K), F32)]
        + _exchange_sems(n_x),
        compiler_params=_params("arbitrary"),
    )(pa, pa, pa, ps, wau, ba, gh, o_gla, d_og, sprev, dgaff_fox, d_proj, *slabs)
    return out[0], out[1], out[2], out[3], out[4:]


def _split3(x):
    x1 = x.astype(BF16).astype(F32)
    x2 = (x - x1).astype(BF16).astype(F32)
    x3 = (x - x1 - x2).astype(BF16).astype(F32)
    return x1, x2, x3


def _fox_prep(pa, ps, bfg):
    S = pa.shape[0]
    tm = _tile(S, FOX_TK)

    def body(ps_ref, b_ref, fq_ref, fk_ref, fv_ref, q_ref, k_ref, qt_ref, kt_ref, vt_ref, st_ref, carry):
        @pl.when(pl.program_id(0) == 0)
        def _():
            carry[...] = jnp.zeros_like(carry)

        vt = fv_ref[...].astype(F32).T.astype(BF16)
        ones_row = jnp.where(lax.broadcasted_iota(jnp.int32, (FOX_VT - FOX_DH, tm), 0) == 0, 1.0, 0.0).astype(BF16)
        for h in range(FOX_H):
            vt_ref[FOX_VT * h:FOX_VT * h + FOX_DH, :] = vt[FOX_DH * h:FOX_DH * (h + 1), :]
            vt_ref[FOX_VT * h + FOX_DH:FOX_VT * (h + 1), :] = ones_row
        lf = _log_sigmoid(ps_ref[...] + b_ref[...])
        rr = lax.broadcasted_iota(jnp.int32, (tm, tm), 0)
        cc = lax.broadcasted_iota(jnp.int32, (tm, tm), 1)
        tri = jnp.where(cc <= rr, 1.0, 0.0).astype(F32)
        f = jnp.dot(tri, lf, preferred_element_type=F32, precision=HIGHEST) + carry[0:1, :]
        carry[...] = jnp.broadcast_to(f[tm - 1:tm, :], carry.shape)
        f1, f2, f3 = _split3(f)
        lane = lax.broadcasted_iota(jnp.int32, (tm, 128), 1)
        st_row = lax.broadcasted_iota(jnp.int32, (8, 128), 0)
        st_lane = lax.broadcasted_iota(jnp.int32, (8, 128), 1)
        stats = jnp.zeros((8, 128), F32)
        for h in range(FOX_H):
            cols = slice(128 * h, 128 * (h + 1))
            c = FF_LANE + h
            a1, a2, a3 = f1[:, c:c + 1], f2[:, c:c + 1], f3[:, c:c + 1]
            q = fq_ref[:, cols].astype(F32) * FOX_SCALE
            k = fk_ref[:, cols].astype(F32)
            fh = f[:, c:c + 1]
            vals = (jnp.max(jnp.sum(q * q, axis=-1, keepdims=True)), jnp.max(jnp.sum(k * k, axis=-1, keepdims=True)),
                    jnp.max(fh), jnp.min(fh), jnp.min(jnp.sum(q * k, axis=-1, keepdims=True)))
            for n, val in enumerate(vals):
                stats = jnp.where((st_row == h) & (st_lane == n), val, stats)
            for n, a in enumerate((a1, a2, a3)):
                q = jnp.where(lane == AUG + n, a, q)
                k = jnp.where(lane == AUG + 3 + n, -a, k)
            q = jnp.where((lane >= AUG + 3) & (lane < AUG + 6), 1.0, q)
            k = jnp.where((lane >= AUG) & (lane < AUG + 3), 1.0, k)
            q_ref[:, cols] = q.astype(BF16)
            k_ref[:, cols] = k.astype(BF16)
            qt_ref[cols, :] = q.T.astype(BF16)
            kt_ref[cols, :] = k.T.astype(BF16)
        st_ref[0] = stats

    wide = lambda j: pl.BlockSpec((tm, 1024), lambda i: (i, j))
    tall = lambda n: pl.BlockSpec((n, tm), lambda i: (0, i))
    return pl.pallas_call(
        body, name="fox_prep", grid=(S // tm,),
        in_specs=[pl.BlockSpec((tm, 128), lambda i: (i, 4)), pl.BlockSpec((1, 128), lambda i: (0, 0)), wide(1), wide(2),
                  pl.BlockSpec((tm, FOX_W), lambda i: (i, A_FV // FOX_W))],
        out_specs=[wide(0), wide(0), tall(1024), tall(1024), tall(FOX_H * FOX_VT), pl.BlockSpec((1, 8, 128), lambda i: (i, 0, 0))],
        out_shape=[jax.ShapeDtypeStruct((S, 1024), BF16), jax.ShapeDtypeStruct((S, 1024), BF16),
                   jax.ShapeDtypeStruct((1024, S), BF16), jax.ShapeDtypeStruct((1024, S), BF16),
                   jax.ShapeDtypeStruct((FOX_H * FOX_VT, S), BF16), jax.ShapeDtypeStruct((S // tm, 8, 128), F32)],
        scratch_shapes=[pltpu.VMEM((8, 128), F32)],
        compiler_params=_params("arbitrary"),
    )(ps, bfg, pa, pa, pa)


FOX_PRUNE_AT = -90.0


def _fox_live_ranges(stats, n_sub, ratio):
    n_b = stats.shape[0]
    q2, k2, f_max, f_min, own = (stats[:, :, n].T for n in range(5))
    slack = 0.01 * jnp.sqrt(q2 * k2) + 1e-5 * jnp.abs(f_max) + 1.0
    bound = (1.01 * jnp.sqrt(q2[:, :, None] * k2[:, None, :]) + (f_max + slack - own)[:, :, None]
             - (f_min - 1e-5 * jnp.abs(f_min))[:, None, :])
    blocks = jnp.arange(n_b)
    dead = (bound <= FOX_PRUNE_AT) & (blocks[None, :] < blocks[:, None])[None]
    dead_fwd = dead.reshape(FOX_H, n_b // n_sub, n_sub, n_b).all(axis=2)
    first = jnp.sum(jnp.cumprod(dead_fwd.astype(jnp.int32), axis=2), axis=2)
    last_live = n_b - 1 - jnp.sum(jnp.cumprod(dead[:, ::-1, :].astype(jnp.int32), axis=1), axis=1)
    first_wide = blocks // ratio + 1
    narrow_end = jnp.minimum(jnp.minimum(first_wide * ratio, n_b)[None], last_live + 1)
    wide_end = jnp.where(last_live >= (first_wide * ratio)[None], last_live // ratio + 1, first_wide[None])
    return first.astype(jnp.int32), narrow_end.astype(jnp.int32), wide_end.astype(jnp.int32)


def _fox_fwd(qa, ka, vt, first):
    S = qa.shape[0]
    tq = _tile(S, FOX_TQ)
    tk = _tile(tq, FOX_TK)
    n_sub = tq // tk

    def body(first_ref, q_ref, k_ref, vt_ref, o_ref, lse_ref):
        pair, i = pl.program_id(0), pl.program_id(1)
        both = lambda f: tuple(f(hh) for hh in range(2))

        def blk(j, carry, diag, heads=(0, 1)):
            ks = pl.ds(pl.multiple_of(j * tk, tk), tk)
            q0 = 0 if diag is None else diag * tk

            def head(hh):
                if hh not in heads:
                    return carry[hh]
                m, acc = carry[hh]
                mo, ao = m[:, q0:], acc[:, q0:]
                s = _nt(k_ref[ks, 128 * hh:128 * (hh + 1)], q_ref[q0:, 128 * hh:128 * (hh + 1)])
                if diag is not None:
                    live = lax.broadcasted_iota(jnp.int32, s.shape, 1) >= lax.broadcasted_iota(jnp.int32, s.shape, 0)
                    s = jnp.where(live, s, NEG)
                mn = jnp.maximum(mo, jnp.max(s, axis=0, keepdims=True))
                p = jnp.exp((s - mn).astype(BF16))
                an = jnp.exp(mo - mn) * ao + _nn(vt_ref[FOX_VT * hh:FOX_VT * (hh + 1), ks], p)
                if q0:
                    mn, an = (jnp.concatenate([old[:, :q0], new], axis=1) for old, new in ((m, mn), (acc, an)))
                return mn, an

            return both(head)

        one = (jnp.full((1, tq), NEG, F32), jnp.zeros((FOX_VT, tq), F32))
        past = i * n_sub
        f0, f1 = first_ref[2 * pair, i], first_ref[2 * pair + 1, i]
        join = jnp.maximum(f0, f1)
        solo = lambda hh: lambda c: lax.fori_loop(jnp.minimum(f0, f1), join, lambda j, cc: blk(j, cc, None, (hh,)), c)
        carry = lax.cond(f0 < f1, solo(0), solo(1), (one, one))
        n_both = past - join
        carry = lax.fori_loop(0, n_both // 2, lambda jj, c: blk(join + 2 * jj + 1, blk(join + 2 * jj, c, None), None), carry)
        carry = lax.cond(n_both % 2 == 1, lambda c: blk(past - 1, c, None), lambda c: c, carry)
        for d in range(n_sub):
            carry = blk(past + d, carry, d)
        (m0, a0), (m1, a1) = carry
        l0, l1 = a0[FOX_DH:FOX_DH + 1], a1[FOX_DH:FOX_DH + 1]
        o_ref[...] = jnp.concatenate([a0[:FOX_DH] / l0, a1[:FOX_DH] / l1], axis=0).T
        lse_ref[0, 0:1, :] = m0 + jnp.log(l0)
        lse_ref[0, 1:2, :] = m1 + jnp.log(l1)
        lse_ref[0, 2:8, :] = jnp.zeros((6, tq), F32)

    return pl.pallas_call(
        body, name="fox_fwd", grid=(FOX_H // 2, S // tq),
        in_specs=[pl.BlockSpec(memory_space=pltpu.SMEM), pl.BlockSpec((tq, 256), lambda p, i: (i, p)),
                  pl.BlockSpec((S, 256), lambda p, i: (0, p)), pl.BlockSpec((2 * FOX_VT, S), lambda p, i: (p, 0))],
        out_specs=[pl.BlockSpec((tq, 128), lambda p, i: (i, p)), pl.BlockSpec((1, 8, tq), lambda p, i: (p, 0, i))],
        out_shape=[jax.ShapeDtypeStruct((S, FOX_W), F32), jax.ShapeDtypeStruct((FOX_H // 2, 8, S), F32)],
        compiler_params=_params("arbitrary", "arbitrary"),
    )(first, qa, ka, vt)


def _fox_delta(d_o, o):
    S = o.shape[0]
    tm = _tile(S, 512)

    def body(d_ref, o_ref, db_ref, dbt_ref, dl_ref):
        d = d_ref[...]
        db_ref[...] = d.astype(BF16)
        dbt_ref[...] = d.T.astype(BF16)
        prod = d * o_ref[...]
        rr = lax.broadcasted_iota(jnp.int32, (8, 128), 0)
        cc = lax.broadcasted_iota(jnp.int32, (8, 128), 1)
        ind = jnp.where(jnp.right_shift(cc, 6) == rr, 1.0, 0.0).astype(F32)
        for p in range(FOX_H // 2):
            dl_ref[p] = lax.dot_general(ind, prod[:, 128 * p:128 * (p + 1)], (((1,), (1,)), ((), ())),
                                        preferred_element_type=F32, precision=HIGHEST)

    row = pl.BlockSpec((tm, FOX_W), lambda i: (i, 0))
    return pl.pallas_call(
        body, name="fox_delta", grid=(S // tm,),
        in_specs=[row, row],
        out_specs=[row, pl.BlockSpec((FOX_W, tm), lambda i: (0, i)), pl.BlockSpec((FOX_H // 2, 8, tm), lambda i: (0, 0, i))],
        out_shape=[jax.ShapeDtypeStruct((S, FOX_W), BF16), jax.ShapeDtypeStruct((FOX_W, S), BF16),
                   jax.ShapeDtypeStruct((FOX_H // 2, 8, S), F32)],
        compiler_params=_params("arbitrary"),
    )(d_o, o)


def _fox_bwd(qa, qat, ka, kat, pa, dob, dobt, lse, delta, narrow_end, wide_end):
    S = qa.shape[0]
    tk = _tile(S, FOX_TK)
    wide = _tile(S, FOX_BWD_WIDE)
    ratio = wide // tk
    n_wide = S // wide

    def body(ne_ref, we_ref, q_ref, qt_ref, k_ref, kt_ref, v_ref, do_ref, dot_ref, lse_ref, dl_ref, dq_ref, dk_ref, dv_ref):
        h, jb = pl.program_id(0), pl.program_id(1)
        hh = h % 2

        @pl.when(jb == 0)
        def _():
            dq_ref[...] = jnp.zeros_like(dq_ref)

        lane = lax.broadcasted_iota(jnp.int32, (tk, 128), 1)
        vm = jnp.where(jnp.right_shift(lane, 6) == hh, v_ref[...], jnp.zeros((), BF16))
        kb, ktb = k_ref[...], kt_ref[0:FOX_LIVE, :]
        mine = pl.ds(pl.multiple_of(hh * FOX_DH, FOX_DH), FOX_DH)

        def blk(ib, tq, carry, masked):
            dk, dv = carry
            qs = pl.ds(pl.multiple_of(ib * tq, tq), tq)
            p = jnp.exp(_nt(kb, q_ref[qs, :]) - lse_ref[0, pl.ds(hh, 1), qs])
            if masked:
                live = lax.broadcasted_iota(jnp.int32, p.shape, 1) >= lax.broadcasted_iota(jnp.int32, p.shape, 0)
                p = jnp.where(live, p, 0.0)
            ds = (p * (_nt(vm, do_ref[qs, :]) - dl_ref[0, pl.ds(hh, 1), qs])).astype(BF16)
            dq_ref[0:FOX_LIVE, qs] += _nn(ktb, ds)
            return dk + _nt(qt_ref[0:FOX_LIVE, qs], ds), dv + _nt(dot_ref[mine, qs], p.astype(BF16))

        carry = blk(jb, tk, (jnp.zeros((FOX_LIVE, tk), F32), jnp.zeros((FOX_DH, tk), F32)), True)
        first_wide = jb // ratio + 1
        carry = lax.fori_loop(jb + 1, ne_ref[h, jb], lambda ib, c: blk(ib, tk, c, False), carry)
        last_wide = we_ref[h, jb]
        rest = jnp.maximum(last_wide - first_wide, 0)
        carry = lax.fori_loop(0, rest // 2, lambda t, c: blk(first_wide + 2 * t + 1, wide, blk(first_wide + 2 * t, wide, c, False),
                                                             False), carry)
        dk, dv = lax.cond(rest % 2 == 1, lambda c: blk(last_wide - 1, wide, c, False), lambda c: c, carry)
        dk_ref[0:FOX_LIVE, :] = dk
        dk_ref[FOX_LIVE:, :] = jnp.zeros((128 - FOX_LIVE, tk), F32)
        dv_ref[...] = dv

    once = pl.Buffered(1)
    rows = pl.BlockSpec((1, 8, S), lambda h, j: (h // 2, 0, 0))
    return pl.pallas_call(
        body, name="fox_bwd", grid=(FOX_H, S // tk),
        in_specs=[pl.BlockSpec(memory_space=pltpu.SMEM), pl.BlockSpec(memory_space=pltpu.SMEM),
                  pl.BlockSpec((S, 128), lambda h, j: (0, h)), pl.BlockSpec((128, S), lambda h, j: (h, 0)),
                  pl.BlockSpec((tk, 128), lambda h, j: (j, h)), pl.BlockSpec((128, tk), lambda h, j: (h, j)),
                  pl.BlockSpec((tk, 128), lambda h, j: (j, A_FV // 128 + h // 2)),
                  pl.BlockSpec((S, 128), lambda h, j: (0, h // 2)), pl.BlockSpec((128, S), lambda h, j: (h // 2, 0)),
                  rows, rows],
        out_specs=[pl.BlockSpec((128, S), lambda h, j: (h, 0), pipeline_mode=once),
                   pl.BlockSpec((128, tk), lambda h, j: (h, j)), pl.BlockSpec((FOX_DH, tk), lambda h, j: (h, j))],
        out_shape=[jax.ShapeDtypeStruct((1024, S), F32), jax.ShapeDtypeStruct((1024, S), F32),
                   jax.ShapeDtypeStruct((FOX_W, S), F32)],
        compiler_params=_params("arbitrary", "arbitrary"),
    )(narrow_end, wide_end, qa, qat, ka, kat, pa, dob, dobt, lse, delta)


def _fox_post(dq, dk, dv, ps, bfg, d_proj):
    S = dq.shape[1]
    tm = _tile(S, 512)
    n_b = S // tm

    def body(dq_ref, dk_ref, dv_ref, ps_ref, b_ref, _, dp_ref, dff_ref, dbf_ref, carry):
        first = pl.program_id(0) == 0

        @pl.when(first)
        def _():
            carry[...] = jnp.zeros_like(carry)

        low = lax.broadcasted_iota(jnp.int32, (tm, 128), 1) < FOX_DH
        for h in range(FOX_H):
            blk = slice(128 * h, 128 * (h + 1))
            dp_ref[:, blk] = jnp.where(low, dq_ref[blk, :].T * FOX_SCALE, 0.0).astype(BF16)
            dp_ref[:, 1024 + 128 * h:1024 + 128 * (h + 1)] = jnp.where(low, dk_ref[blk, :].T, 0.0).astype(BF16)
        dp_ref[:, 2048:P_FOX_W] = dv_ref[...].T.astype(BF16)
        rr = lax.broadcasted_iota(jnp.int32, (FOX_H, 1024), 0)
        cc = lax.broadcasted_iota(jnp.int32, (FOX_H, 1024), 1)
        sel_k = jnp.where(cc == 128 * rr + AUG + 3, 1.0, 0.0).astype(F32)
        sel_q = jnp.where(cc == 128 * rr + AUG, 1.0, 0.0).astype(F32)
        g = (jnp.dot(sel_k, dk_ref[...], preferred_element_type=F32, precision=HIGHEST)
             - jnp.dot(sel_q, dq_ref[...], preferred_element_type=F32, precision=HIGHEST))
        t_from = lax.broadcasted_iota(jnp.int32, (tm, tm), 0)
        t_to = lax.broadcasted_iota(jnp.int32, (tm, tm), 1)
        later = jnp.where(t_from >= t_to, 1.0, 0.0).astype(F32)
        dlf = jnp.dot(-g, later, preferred_element_type=F32, precision=HIGHEST) + carry[:, 0:1]
        carry[...] = jnp.broadcast_to(dlf[:, 0:1], carry.shape)
        cols = jnp.concatenate([jnp.zeros((FF_LANE, tm), F32), dlf, jnp.zeros((128 - FF_LANE - FOX_H, tm), F32)], axis=0).T
        dff = cols * jax.nn.sigmoid(-(ps_ref[...] + b_ref[...]))
        dff_ref[...] = dff
        part = _sum8(dff)

        @pl.when(first)
        def _():
            dbf_ref[...] = part

        @pl.when(jnp.logical_not(first))
        def _():
            dbf_ref[...] += part

    rev = lambda i: (n_b - 1 - i, 0)
    tall = lambda n: pl.BlockSpec((n, tm), lambda i: (0, n_b - 1 - i))
    return pl.pallas_call(
        body, name="fox_post", grid=(n_b,),
        in_specs=[tall(1024), tall(1024), tall(FOX_W), pl.BlockSpec((tm, 128), lambda i: (n_b - 1 - i, 4)),
                  pl.BlockSpec((1, 128), lambda i: (0, 0)), pl.BlockSpec(memory_space=pl.ANY)],
        out_specs=[pl.BlockSpec((tm, P_FOX_W), lambda i: (n_b - 1 - i, P_FOX // P_FOX_W)), pl.BlockSpec((tm, 128), rev),
                   pl.BlockSpec((8, 128), lambda i: (0, 0))],
        out_shape=[jax.ShapeDtypeStruct((S, P_W), BF16), jax.ShapeDtypeStruct((S, 128), F32),
                   jax.ShapeDtypeStruct((8, 128), F32)],
        input_output_aliases={5: 0},
        scratch_shapes=[pltpu.VMEM((8, 128), F32)],
        compiler_params=_params("arbitrary"),
    )(dq, dk, dv, ps, bfg, d_proj)


def _mem_prep(mem, g_mem, wkv):
    def body(m_ref, g_ref, w_ref, mn_ref, kv_ref):
        r, xh = _rms(m_ref[...])
        mn = (xh * g_ref[...]).astype(BF16)
        mn_ref[...] = mn
        kv_ref[...] = _nn(mn, w_ref[...]).astype(BF16)

    return pl.pallas_call(
        body, name="mem_prep",
        out_shape=[jax.ShapeDtypeStruct((N_MEM, D), BF16), jax.ShapeDtypeStruct((N_MEM, 2 * MEM_W), BF16)],
        compiler_params=pltpu.CompilerParams(vmem_limit_bytes=V7X_VMEM_LIMIT),
    )(mem, g_mem, wkv)


def _mem_softmax(qh, kh):
    s = _nt(qh, kh) * MEM_SCALE
    e = jnp.exp(s - jnp.max(s, axis=-1, keepdims=True))
    return e / jnp.sum(e, axis=-1, keepdims=True)


def _mem_fwd(pa, mkv):
    S = pa.shape[0]
    tm = _tile(S, 512)

    def body(q_ref, kv_ref, o_ref):
        for h in range(MEM_H):
            cols = slice(MEM_DH * h, MEM_DH * (h + 1))
            p = _mem_softmax(q_ref[:, cols], kv_ref[:, cols])
            o_ref[:, cols] = _nn(p.astype(BF16), kv_ref[:, MEM_W + MEM_DH * h:MEM_W + MEM_DH * (h + 1)])

    return pl.pallas_call(
        body, name="mem_fwd", grid=(S // tm,),
        in_specs=[pl.BlockSpec((tm, MEM_W), lambda i: (i, A_MQ // MEM_W)), pl.BlockSpec((N_MEM, 2 * MEM_W), lambda i: (0, 0))],
        out_specs=pl.BlockSpec((tm, MEM_W), lambda i: (i, 0)),
        out_shape=jax.ShapeDtypeStruct((S, MEM_W), F32),
        compiler_params=_params("arbitrary"),
    )(pa, mkv)


def _mem_bwd(pa, mkv, d_o, d_proj):
    S = pa.shape[0]
    tm = _tile(S, 512)

    def body(q_ref, kv_ref, do_ref, _, dq_ref, dkv_ref):
        first = pl.program_id(0) == 0
        parts = []
        for h in range(MEM_H):
            cols = slice(MEM_DH * h, MEM_DH * (h + 1))
            vcols = slice(MEM_W + MEM_DH * h, MEM_W + MEM_DH * (h + 1))
            qh, kh = q_ref[:, cols], kv_ref[:, cols]
            p = _mem_softmax(qh, kh)
            dob = do_ref[:, cols].astype(BF16)
            dp = _nt(dob, kv_ref[:, vcols])
            ds = (p * (dp - jnp.sum(p * dp, axis=-1, keepdims=True)) * MEM_SCALE).astype(BF16)
            dq_ref[:, cols] = _nn(ds, kh).astype(BF16)
            parts.append((cols, _tn(ds, qh)))
            parts.append((vcols, _tn(p.astype(BF16), dob)))

        @pl.when(first)
        def _():
            for sl, v in parts:
                dkv_ref[:, sl] = v

        @pl.when(jnp.logical_not(first))
        def _():
            for sl, v in parts:
                dkv_ref[:, sl] += v

    return pl.pallas_call(
        body, name="mem_bwd", grid=(S // tm,),
        in_specs=[pl.BlockSpec((tm, MEM_W), lambda i: (i, A_MQ // MEM_W)), pl.BlockSpec((N_MEM, 2 * MEM_W), lambda i: (0, 0)),
                  pl.BlockSpec((tm, MEM_W), lambda i: (i, 0)), pl.BlockSpec(memory_space=pl.ANY)],
        out_specs=[pl.BlockSpec((tm, MEM_W), lambda i: (i, P_MQ // MEM_W)), pl.BlockSpec((N_MEM, 2 * MEM_W), lambda i: (0, 0))],
        out_shape=[jax.ShapeDtypeStruct((S, P_W), BF16), jax.ShapeDtypeStruct((N_MEM, 2 * MEM_W), F32)],
        input_output_aliases={3: 0},
        compiler_params=_params("arbitrary"),
    )(pa, mkv, d_o, d_proj)


def _mem_prep_bwd(mem, g_mem, mn, wkv, dkv):
    def body(m_ref, g_ref, mn_ref, w_ref, d_ref, dw_ref, dg_ref):
        db = d_ref[...].astype(BF16)
        dw_ref[...] = _tn(mn_ref[...], db).astype(BF16)
        r, xh = _rms(m_ref[...])
        dg_ref[...] = _sum8(_nt(db, w_ref[...]) * xh)

    dw, dg = pl.pallas_call(
        body, name="mem_prep_bwd",
        out_shape=[jax.ShapeDtypeStruct((D, 2 * MEM_W), BF16), jax.ShapeDtypeStruct((8, D), F32)],
        compiler_params=pltpu.CompilerParams(vmem_limit_bytes=V7X_VMEM_LIMIT),
    )(mem, g_mem, mn, wkv, dkv)
    return dw.reshape(N_DEV, D // N_DEV, 2 * MEM_W), dg


def _rearrange_w_in(w):
    def heads128(cols):
        blk = w[:, cols:cols + FOX_W].reshape(D, FOX_H, FOX_DH)
        return jnp.pad(blk, ((0, 0), (0, 0), (0, 128 - FOX_DH))).reshape(D, FOX_H * 128)

    fq, fk, fv, mq, wg = heads128(O_FQ), heads128(O_FK), w[:, O_FV:O_FF], w[:, O_MQ:O_GT], w[:, O_GT:]
    gaff = jnp.concatenate([w[:, O_GA:O_FQ], w[:, O_FF:O_MQ], jnp.zeros((D, 128 - GLA_R - FOX_H), w.dtype)], axis=1)
    wa = jnp.concatenate([w[:, O_GQ:O_GG], fq, fk, fv, mq], axis=1)
    ws = jnp.concatenate([w[:, O_GG:O_GA], gaff], axis=1)
    wp = jnp.concatenate([fq, fk, fv, mq, wg, w[:, O_GQ:O_GG], ws, jnp.zeros((D, P_W - P_GLA - 1024 - S_W), w.dtype)], axis=1)
    return wa, wg, ws, wp


def _restore_w_in_grad(dwp):
    def unheads(off):
        return dwp[:, off:off + FOX_H * 128].reshape(D, FOX_H, 128)[:, :, :FOX_DH].reshape(D, FOX_W)

    g0 = P_GLA + 1024
    return jnp.concatenate([
        dwp[:, P_GLA:g0], dwp[:, g0:g0 + 512], dwp[:, g0 + 512:g0 + 512 + GLA_R], unheads(P_FOX), unheads(P_FOX + 1024),
        dwp[:, P_FOX + 2048:P_FOX + P_FOX_W], dwp[:, g0 + 512 + GLA_R:g0 + 512 + GLA_R + FOX_H], dwp[:, P_MQ:P_GT],
        dwp[:, P_GT:P_GLA]], axis=1)


def _local_step(x, mem, target, p, late_shards):
    S = x.shape[0]
    p = dict(p)
    wa, wg, ws, wp = _rearrange_w_in(p["w_in"])
    wau = jnp.pad(p["w_alpha_up"], ((0, 128 - GLA_R), (0, 0)))
    bfg = jnp.pad(p["b_forget"], ((0, 0), (FF_LANE, 128 - FF_LANE - FOX_H)))
    gh = p["g_gla_head"].reshape(1, GLA_V)

    pa, pg, ps, u, gathered = _proj(x, p["g_mix"], wa, wg, ws, late_shards)
    p.update({n: _unslab(t, ax) for (n, ax), t in zip(BIG[1:], gathered)})
    o_gla, og, sprev = _gla_fwd(pa, ps, wau, p["b_alpha"], gh)
    qa, ka, qat, kat, vt, fox_stats = _fox_prep(pa, ps, bfg)
    fox_tk = _tile(S, FOX_TK)
    fox_first, fox_narrow_end, fox_wide_end = _fox_live_ranges(fox_stats, _tile(S, FOX_TQ) // fox_tk,
                                                               _tile(S, FOX_BWD_WIDE) // fox_tk)
    o_fox, lse = _fox_fwd(qa, ka, vt, fox_first)
    mn, mkv = _mem_prep(mem, p["g_mem"], p["w_mem_kv"])
    o_mem = _mem_fwd(pa, mkv)
    y3, mg = _merge(og, o_fox, o_mem, p["w_gla_o"], p["w_fox_o"], p["w_mem_o"], pg)
    h1, u2 = _out_proj(mg, p["w_out"], x, p["g_ffn"])
    a, act = _ff1(u2, p["w_ff1"])
    dh2, dh2b, loss8, dg_final = _ff2_loss(act, p["w_ff2"], h1, p["g_final"].reshape(1, D), target)

    d_a = _dact(dh2b, p["w_ff2"], a)
    dw_ff2 = _wgrad(act, dh2b, "wgrad_ff2", 0)
    dh1, dh1b, dg_ffn = _nt_rmsbwd(d_a, p["w_ff1"], h1, p["g_ffn"], dh2, "dffn", True)
    dw_ff1 = _wgrad(u2, d_a, "wgrad_ff1", 1)
    dy_g, dy_f, dy_m, do_g, do_f, do_m, d_proj = _dmerge(dh1b, p["w_out"], pg, y3, p["w_gla_o"], p["w_fox_o"], p["w_mem_o"])
    dw_out = _wgrad(mg, dh1b, "wgrad_out", 0)
    dw_gla_o = _wgrad(og, dy_g, "wgrad_gla_o", 1)
    dw_fox_o = _wgrad(o_fox, dy_f, "wgrad_fox_o", 1)
    dw_mem_o = _wgrad(o_mem, dy_m, "wgrad_mem_o", 1)
    d_proj, d_mkv = _mem_bwd(pa, mkv, do_m, d_proj)
    dw_mem_kv, dg_mem = _mem_prep_bwd(mem, p["g_mem"], mn, p["w_mem_kv"], d_mkv)
    dob, dobt, delta = _fox_delta(do_f, o_fox)
    dq, dk, dv = _fox_bwd(qa, qat, ka, kat, pa, dob, dobt, lse, delta, fox_narrow_end, fox_wide_end)
    d_proj, dgaff_fox, db_forget = _fox_post(dq, dk, dv, ps, bfg, d_proj)
    ready = dict(w_mem_kv=dw_mem_kv, w_gla_o=dw_gla_o, w_fox_o=dw_fox_o, w_mem_o=dw_mem_o, w_out=dw_out, w_ff1=dw_ff1,
                 w_ff2=dw_ff2)
    d_proj, dw_au, db_alpha, dg_gla, arrived = _gla_bwd(pa, ps, wau, p["b_alpha"], gh, o_gla, do_g, sprev, dgaff_fox, d_proj,
                                                        [ready[n] for n, _ in BIG[1:]])
    dw_in = _slabs(_restore_w_in_grad(_wgrad(u, d_proj, "wgrad_in")), 1).astype(BF16)
    dx, dg_mix, arrived_in = _nt_rmsbwd(d_proj, wp, x, p["g_mix"], dh1, "dmix", False, [dw_in])

    big = dict(zip([n for n, _ in BIG], [arrived_in[0], *arrived]))
    small = dict(g_mix=dg_mix, g_mem=dg_mem, g_ffn=dg_ffn, g_final=dg_final, b_alpha=db_alpha, g_gla_head=dg_gla,
                 b_forget=db_forget, w_alpha_up=dw_au, loss=loss8)
    return dx, big, small


BIG = (("w_in", 1), ("w_mem_kv", 0), ("w_gla_o", 1), ("w_fox_o", 1), ("w_mem_o", 1), ("w_out", 0), ("w_ff1", 1), ("w_ff2", 0))


def _peer(d):
    me = lax.axis_index("x") * 4 + lax.axis_index("y") * 2 + lax.axis_index("c")
    t = (me + d) % N_DEV
    return (t // 4, (t // 2) % 2, t % 2), me


def _exchange_sems(n):
    return [pltpu.SemaphoreType.DMA((n, N_DEV - 1)), pltpu.SemaphoreType.DMA((n, N_DEV - 1)), pltpu.SemaphoreType.DMA((n,))]


def _exchange_call(body, blocks, out_shape, name):
    n = len(blocks)
    any_spec = pl.BlockSpec(memory_space=pl.ANY)
    return pl.pallas_call(body, name=name, in_specs=[any_spec] * n, out_specs=[any_spec] * n, out_shape=out_shape,
                          scratch_shapes=_exchange_sems(n))(*blocks)


class _AllToAll:
    def __init__(self, ins, outs, sems, gather):
        send, recv, loc = sems
        n = len(ins)
        _, me = _peer(0)
        src = (lambda k, j: ins[k]) if gather else (lambda k, j: ins[k].at[j])
        self.local = [pltpu.make_async_copy(src(k, me), outs[k].at[me], loc.at[k]) for k in range(n)]
        self.remote = []
        for d in range(1, N_DEV):
            to, _ = _peer(d)
            self.remote += [pltpu.make_async_remote_copy(
                src_ref=src(k, (me + d) % N_DEV), dst_ref=outs[k].at[me], send_sem=send.at[k, d - 1],
                recv_sem=recv.at[k, d - 1], device_id=to, device_id_type=MESH) for k in range(n)]

    def start(self):
        for cp in self.local + self.remote:
            cp.start()

    def wait(self):
        for cp in self.remote:
            cp.wait_send()
        for cp in self.remote:
            cp.wait_recv()
        for cp in self.local:
            cp.wait()


def _gathered_shapes(shards):
    return [jax.ShapeDtypeStruct((N_DEV,) + b.shape, b.dtype) for b in shards]


def _gather_weights(shards):
    n = len(shards)

    def body(*refs):
        ins, outs = refs[:n], refs[n:2 * n]
        send, recv, loc = refs[2 * n:]
        x, y, c = lax.axis_index("x"), lax.axis_index("y"), lax.axis_index("c")
        sibling = (x, y, 1 - c)
        chips = [(1 - x, y), (x, 1 - y), (1 - x, 1 - y)]
        slot = lambda px, py, pc: px * 4 + py * 2 + pc

        def copy(k, s, block, to, src=None):
            rows = outs[k].at[slot(*block)]
            return pltpu.make_async_remote_copy(src_ref=rows if src is None else src, dst_ref=rows, send_sem=send.at[k, s],
                                                recv_sem=recv.at[k, s], device_id=to, device_id_type=MESH)

        me = (x, y, c)
        own = [pltpu.make_async_copy(ins[k], outs[k].at[slot(*me)], loc.at[k]) for k in range(n)]
        first = [copy(k, 0, me, sibling, src=ins[k]) for k in range(n)]
        first += [copy(k, 1 + j, me, (*chip, c), src=ins[k]) for j, chip in enumerate(chips) for k in range(n)]
        for cp in own + first:
            cp.start()
        passed = []
        for j, chip in enumerate(chips):
            for k in range(n):
                copy(k, 1 + j, (*chip, c), me).wait_recv()
                fwd = copy(k, 4 + j, (*chip, c), sibling)
                fwd.start()
                passed.append(fwd)
        for k in range(n):
            copy(k, 0, sibling, me).wait_recv()
        for j, chip in enumerate(chips):
            for k in range(n):
                copy(k, 4 + j, (*chip, 1 - c), me).wait_recv()
        for cp in first + passed:
            cp.wait_send()
        for cp in own:
            cp.wait()

    return _exchange_call(body, shards, [jax.ShapeDtypeStruct((N_DEV,) + b.shape, b.dtype) for b in shards], "gather_weights")


def _adamw_math(g, w, m, v):
    m2 = ADAM_B1 * m + (1.0 - ADAM_B1) * g
    v2 = ADAM_B2 * v + (1.0 - ADAM_B2) * jnp.square(g)
    m_hat = m2 / (1.0 - ADAM_B1 ** ADAM_STEP)
    v_hat = v2 / (1.0 - ADAM_B2 ** ADAM_STEP)
    delta = -ADAM_LR * (m_hat / (jnp.sqrt(v_hat) + ADAM_EPS) + ADAM_WD * w)
    return delta, m2, v2


def _adamw_sum(parts, w, m, v, name):
    R, C = w.shape
    tr = _tile(R, 128)

    def body(p_ref, w_ref, m_ref, v_ref, g_ref, d_ref, m2_ref, v2_ref):
        g = p_ref[0].astype(F32)
        for j in range(1, p_ref.shape[0]):
            g = g + p_ref[j].astype(F32)
        g_ref[...] = g
        d_ref[...], m2_ref[...], v2_ref[...] = _adamw_math(g, w_ref[...], m_ref[...], v_ref[...])

    blk = pl.BlockSpec((tr, C), lambda i: (i, 0))
    return pl.pallas_call(
        body, name=name, grid=(R // tr,),
        in_specs=[pl.BlockSpec((parts.shape[0], tr, C), lambda i: (0, i, 0)), blk, blk, blk],
        out_specs=[blk] * 4, out_shape=[jax.ShapeDtypeStruct((R, C), F32)] * 4,
        compiler_params=_params("arbitrary"),
    )(parts, w, m, v)


SMALL_ROWS = 24


def _pack_small(d):
    mixed = jnp.concatenate([d["b_alpha"].reshape(1, GLA_K), d["g_gla_head"].reshape(1, GLA_V),
                             jnp.pad(d["b_forget"].reshape(1, FOX_H), ((0, 0), (FF_LANE, 128 - FF_LANE - FOX_H))),
                             jnp.zeros((1, 128), F32)], axis=1)
    rows = [d["g_mix"].reshape(1, D), d["g_mem"].reshape(1, D), d["g_ffn"].reshape(1, D), d["g_final"].reshape(1, D), mixed,
            jnp.zeros((3, D), F32), jnp.pad(d["w_alpha_up"].reshape(GLA_R, GLA_K), ((0, 0), (0, D - GLA_K)))]
    return jnp.concatenate(rows, axis=0)


def _unpack_small(t):
    return dict(g_mix=t[0:1], g_mem=t[1:2], g_ffn=t[2:3], g_final=t[3], b_alpha=t[4:5, 0:GLA_K],
                g_gla_head=t[4:5, GLA_K:GLA_K + GLA_V].reshape(1, GLA_H, GLA_DV),
                b_forget=t[4:5, 768 + FF_LANE:768 + FF_LANE + FOX_H], w_alpha_up=t[8:24, 0:GLA_K].reshape(1, GLA_R, GLA_K))


def _small_allreduce(small, w, m, v):
    def body(gm, gme, gf, gfi, ba, gg, bf, wau, ls, w_ref, m_ref, v_ref, g_ref, d_ref, m2_ref, v2_ref, l_ref,
             buf, send, recv):
        _, me = _peer(0)
        buf[me] = jnp.zeros((SMALL_ROWS, D), F32)
        for r, ref in enumerate((gm, gme, gf, gfi)):
            buf[me, r:r + 1, :] = jnp.sum(ref[...], axis=0, keepdims=True)
        buf[me, 4:5, 0:GLA_K] = jnp.sum(ba[...], axis=0, keepdims=True)
        buf[me, 4:5, GLA_K:GLA_K + GLA_V] = jnp.sum(gg[...], axis=0, keepdims=True)
        buf[me, 4:5, 768:896] = jnp.sum(bf[...], axis=0, keepdims=True)
        lrow = jnp.sum(ls[...], axis=0, keepdims=True)
        lsum = lrow[:, 0:128]
        for c in range(1, D // 128):
            lsum = lsum + lrow[:, 128 * c:128 * (c + 1)]
        buf[me, 4:5, 896:1024] = lsum
        buf[me, 8:24, 0:GLA_K] = wau[0:GLA_R, :]
        remote = []
        for d in range(1, N_DEV):
            to, me = _peer(d)
            cp = pltpu.make_async_remote_copy(src_ref=buf.at[me], dst_ref=buf.at[me], send_sem=send.at[d - 1],
                                              recv_sem=recv.at[d - 1], device_id=to, device_id_type=MESH)
            cp.start()
            remote.append(cp)
        for cp in remote:
            cp.wait_send()
        for cp in remote:
            cp.wait_recv()
        g = buf[0]
        for j in range(1, N_DEV):
            g = g + buf[j]
        g_ref[...] = g
        d_ref[...], m2_ref[...], v2_ref[...] = _adamw_math(g, w_ref[...], m_ref[...], v_ref[...])
        l_ref[...] = g[4:5, 896:1024]

    packed = jax.ShapeDtypeStruct((SMALL_ROWS, D), F32)
    return pl.pallas_call(
        body, name="small_allreduce",
        out_shape=[packed, packed, packed, packed, jax.ShapeDtypeStruct((1, 128), F32)],
        scratch_shapes=[pltpu.VMEM((N_DEV, SMALL_ROWS, D), F32), pltpu.SemaphoreType.DMA((N_DEV - 1,)),
                        pltpu.SemaphoreType.DMA((N_DEV - 1,))],
    )(small["g_mix"], small["g_mem"], small["g_ffn"], small["g_final"], small["b_alpha"], small["g_gla_head"],
      small["b_forget"], small["w_alpha_up"], small["loss"], w, m, v)


def _slabs(g, axis):
    R, C = g.shape
    if axis == 0:
        return g.reshape(N_DEV, R // N_DEV, C)
    return g.reshape(R, N_DEV, C // N_DEV).transpose(1, 0, 2)


def _unslab(t, axis):
    n, r, c = t.shape
    if axis == 0:
        return t.reshape(n * r, c)
    return t.transpose(1, 0, 2).reshape(r, n * c)


def kernel(x, mem, g_mix, w_in, w_alpha_up, b_alpha, b_forget, g_gla_head, g_mem, w_mem_kv, w_gla_o, w_fox_o, w_mem_o, w_out, g_ffn, w_ff1, w_ff2, g_final, loss_target, m_g_mix, m_w_in, m_w_alpha_up, m_b_alpha, m_b_forget, m_g_gla_head, m_g_mem, m_w_mem_kv, m_w_gla_o, m_w_fox_o, m_w_mem_o, m_w_out, m_g_ffn, m_w_ff1, m_w_ff2, m_g_final, v_g_mix, v_w_in, v_w_alpha_up, v_b_alpha, v_b_forget, v_g_gla_head, v_g_mem, v_w_mem_kv, v_w_gla_o, v_w_fox_o, v_w_mem_o, v_w_out, v_g_ffn, v_w_ff1, v_w_ff2, v_g_final):
    names = ["g_mix", "w_in", "w_alpha_up", "b_alpha", "b_forget", "g_gla_head", "g_mem", "w_mem_kv", "w_gla_o", "w_fox_o",
             "w_mem_o", "w_out", "g_ffn", "w_ff1", "w_ff2", "g_final"]
    w = dict(g_mix=g_mix, w_in=w_in, w_alpha_up=w_alpha_up, b_alpha=b_alpha, b_forget=b_forget, g_gla_head=g_gla_head,
             g_mem=g_mem, w_mem_kv=w_mem_kv, w_gla_o=w_gla_o, w_fox_o=w_fox_o, w_mem_o=w_mem_o, w_out=w_out, g_ffn=g_ffn,
             w_ff1=w_ff1, w_ff2=w_ff2, g_final=g_final)
    m = dict(g_mix=m_g_mix, w_in=m_w_in, w_alpha_up=m_w_alpha_up, b_alpha=m_b_alpha, b_forget=m_b_forget,
             g_gla_head=m_g_gla_head, g_mem=m_g_mem, w_mem_kv=m_w_mem_kv, w_gla_o=m_w_gla_o, w_fox_o=m_w_fox_o,
             w_mem_o=m_w_mem_o, w_out=m_w_out, g_ffn=m_g_ffn, w_ff1=m_w_ff1, w_ff2=m_w_ff2, g_final=m_g_final)
    v = dict(g_mix=v_g_mix, w_in=v_w_in, w_alpha_up=v_w_alpha_up, b_alpha=v_b_alpha, b_forget=v_b_forget,
             g_gla_head=v_g_gla_head, g_mem=v_g_mem, w_mem_kv=v_w_mem_kv, w_gla_o=v_w_gla_o, w_fox_o=v_w_fox_o,
             w_mem_o=v_w_mem_o, w_out=v_w_out, g_ffn=v_g_ffn, w_ff1=v_w_ff1, w_ff2=v_w_ff2, g_final=v_g_final)
    me = lax.axis_index("x") * 4 + lax.axis_index("y") * 2 + lax.axis_index("c")

    shard = lambda n: w[n][0].astype(BF16)
    w_in_all, w_au_all = _gather_weights([shard("w_in"), shard("w_alpha_up")])
    p = dict(w_in=_unslab(w_in_all, 1), w_alpha_up=_unslab(w_au_all, 1), g_mix=g_mix, b_alpha=b_alpha, b_forget=b_forget,
             g_gla_head=g_gla_head, g_mem=g_mem, g_ffn=g_ffn, g_final=g_final)

    dx, big, small = _local_step(x[0], mem[0], loss_target[0], p, [shard(n) for n, _ in BIG[1:]])

    out_g, out_d, out_m, out_v = {}, {}, {}, {}
    for n, _ in BIG:
        g_, d_, m_, v_ = _adamw_sum(big[n], w[n][0], m[n][0], v[n][0], "adamw_" + n)
        out_g[n], out_d[n], out_m[n], out_v[n] = g_[None], d_[None], m_[None], v_[None]

    full = lambda d: dict(d, w_alpha_up=jnp.zeros((1, GLA_R, GLA_K), F32))
    gs, ds, ms, vs, lrow = _small_allreduce(small, _pack_small(full(w)), _pack_small(full(m)), _pack_small(full(v)))
    g_s, d_s, m_s, v_s = _unpack_small(gs), _unpack_small(ds), _unpack_small(ms), _unpack_small(vs)
    for n in names:
        if n not in out_g and n != "w_alpha_up":
            out_g[n], out_d[n], out_m[n], out_v[n] = g_s[n], d_s[n], m_s[n], v_s[n]
    g_au = lax.dynamic_slice_in_dim(g_s["w_alpha_up"][0], me * (GLA_K // N_DEV), GLA_K // N_DEV, axis=1)
    g_, d_, m_, v_ = _adamw_sum(g_au[None], w_alpha_up[0], m_w_alpha_up[0], v_w_alpha_up[0], "adamw_w_alpha_up")
    out_g["w_alpha_up"], out_d["w_alpha_up"], out_m["w_alpha_up"], out_v["w_alpha_up"] = g_[None], d_[None], m_[None], v_[None]

    loss = jnp.sum(lrow) * (0.5 / D)
    return (loss, dx[None], *[out_g[n] for n in names], *[out_d[n] for n in names], *[out_m[n] for n in names],
            *[out_v[n] for n in names])
```

```python
import jax
import jax.numpy as jnp
from jax import lax
from jax.experimental import pallas as pl
from jax.experimental.pallas import tpu as pltpu

F32, BF16 = jnp.float32, jnp.bfloat16
HIGHEST = lax.Precision.HIGHEST
MESH = pl.DeviceIdType.MESH

N_DEV = 8
D = 1024
EPS = 1e-6
CHUNK = 64
N_MEM = 256
GLA_H, GLA_DK, GLA_DV = 4, 64, 128
GLA_K, GLA_V, GLA_R = 256, 512, 16
FOX_H, FOX_DH, FOX_W = 8, 64, 512
MEM_H, MEM_DH, MEM_W = 4, 128, 512
D_FF = 4096
D_IN = 6680
FOX_SCALE = 0.125
GLA_SCALE = 0.125
MEM_SCALE = MEM_DH ** -0.5
GLA_TAU_INV = 1.0 / 16.0
NEG = -1e30

O_GQ, O_GK, O_GV, O_GG, O_GA, O_FQ, O_FK, O_FV, O_FF, O_MQ, O_GT = 0, 256, 512, 1024, 1536, 1552, 2064, 2576, 3088, 3096, 3608
A_FQ, A_FK, A_FV, A_MQ, A_W = 1024, 2048, 3072, 3584, 4096
S_W = 640
G_W = 3072
P_FOX, P_FOX_W, P_MQ, P_GT, P_GLA, P_GLA_W, P_W = 0, 2560, 2560, 3072, 6144, 2048, 8192
FF_LANE = 16
AUG = 64
FOX_LIVE = 80
FOX_VT = 80

ADAM_LR, ADAM_B1, ADAM_B2, ADAM_EPS, ADAM_WD, ADAM_STEP = 0.001, 0.9, 0.999, 1e-08, 0.01, 10
V7X_VMEM_LIMIT = 54 * 1024 * 1024
FOX_TK = 512
FOX_TQ = 2048
FOX_BWD_WIDE = 1024


def _params(*sem):
    return pltpu.CompilerParams(dimension_semantics=sem, vmem_limit_bytes=V7X_VMEM_LIMIT)


def _nt(a, b):
    return lax.dot_general(a, b, (((1,), (1,)), ((), ())), preferred_element_type=F32)


def _tn(a, b):
    return lax.dot_general(a, b, (((0,), (0,)), ((), ())), preferred_element_type=F32)


def _nn(a, b):
    return jnp.dot(a, b, preferred_element_type=F32)


def _log_sigmoid(z):
    return jnp.minimum(z, 0.0) - jnp.log(1.0 + jnp.exp(-jnp.abs(z)))


def _sum01(m01, x):
    x1 = x.astype(BF16)
    x2 = (x - x1.astype(F32)).astype(BF16)
    x3 = (x - x1.astype(F32) - x2.astype(F32)).astype(BF16)
    return _nn(m01, x1) + _nn(m01, x2) + _nn(m01, x3)


def _sum8(x):
    return x.reshape(x.shape[0] // 8, 8, x.shape[1]).sum(axis=0)


def _rms(xv):
    r = lax.rsqrt(jnp.mean(xv * xv, axis=-1, keepdims=True) + EPS)
    return r, xv * r


def _rms_bwd(du, g, r, xh):
    w = du * g
    return r * (w - xh * jnp.mean(w * xh, axis=-1, keepdims=True))


def _row_chunks(n, size=256):
    return [slice(r, r + min(size, n)) for r in range(0, n, min(size, n))]


def _tile(n, pref):
    t = min(n, pref)
    assert n % t == 0, (n, t)
    return t


def _proj(x, g, wa, wg, ws, shards):
    S = x.shape[0]
    tm, tn = _tile(S, 1024), 1024
    n_a, n_g = A_W // tn, G_W // tn
    n_i, n_j = S // tm, n_a + n_g + 1
    n_x = len(shards)

    def body(*refs):
        x_ref, g_ref, wa_ref, wg_ref, ws_ref = refs[:5]
        pa_ref, pg_ref, ps_ref, u_ref = refs[5 + n_x:9 + n_x]
        u_s = refs[9 + 2 * n_x]
        gather = lambda: _AllToAll(refs[5:5 + n_x], refs[9 + n_x:9 + 2 * n_x], refs[10 + 2 * n_x:], True)
        i, j = pl.program_id(0), pl.program_id(1)

        @pl.when((i == 0) & (j == 0))
        def _():
            gather().start()

        @pl.when(j == 0)
        def _():
            r, xh = _rms(x_ref[...])
            u_s[...] = (xh * g_ref[...]).astype(BF16)
            u_ref[...] = u_s[...]

        @pl.when(j < n_a)
        def _():
            pa_ref[...] = _nn(u_s[...], wa_ref[...]).astype(BF16)

        @pl.when((j >= n_a) & (j < n_a + n_g))
        def _():
            pg_ref[...] = _nn(u_s[...], wg_ref[...]).astype(BF16)

        @pl.when(j == n_a + n_g)
        def _():
            ps_ref[...] = _nn(u_s[...], ws_ref[...])

        @pl.when((i == n_i - 1) & (j == n_j - 1))
        def _():
            gather().wait()

    in_a = lambda j: jnp.minimum(j, n_a - 1)
    in_g = lambda j: jnp.clip(j - n_a, 0, n_g - 1)
    row = pl.BlockSpec((tm, D), lambda i, j: (i, 0))
    any_spec = pl.BlockSpec(memory_space=pl.ANY)
    out = pl.pallas_call(
        body, name="proj", grid=(n_i, n_j),
        in_specs=[row, pl.BlockSpec((1, D), lambda i, j: (0, 0)), pl.BlockSpec((D, tn), lambda i, j: (0, in_a(j))),
                  pl.BlockSpec((D, tn), lambda i, j: (0, in_g(j))),
                  pl.BlockSpec((D, S_W), lambda i, j: (0, 0), pipeline_mode=pl.Buffered(1))] + [any_spec] * n_x,
        out_specs=[pl.BlockSpec((tm, tn), lambda i, j: (i, in_a(j))), pl.BlockSpec((tm, tn), lambda i, j: (i, in_g(j))),
                   pl.BlockSpec((tm, S_W), lambda i, j: (i, 0)), row] + [any_spec] * n_x,
        out_shape=[jax.ShapeDtypeStruct((S, A_W), BF16), jax.ShapeDtypeStruct((S, G_W), BF16),
                   jax.ShapeDtypeStruct((S, S_W), F32), jax.ShapeDtypeStruct((S, D), BF16)] + _gathered_shapes(shards),
        scratch_shapes=[pltpu.VMEM((tm, D), BF16)] + _exchange_sems(n_x),
        compiler_params=_params("arbitrary", "arbitrary"),
    )(x, g, wa, wg, ws, *shards)
    return out[0], out[1], out[2], out[3], out[4:]


def _wgrad(a, b, name, slab_axis=None):
    S, Ka = a.shape
    N = b.shape[1]
    tka, tn, ts = _tile(Ka, 1024), _tile(N, 1024), _tile(S, 1024)
    n_s = S // ts
    per = N // N_DEV
    slabs_per_step = tn // per

    def body(a_ref, b_ref, o_ref, acc):
        s = pl.program_id(2)

        @pl.when(s == 0)
        def _():
            acc[...] = jnp.zeros_like(acc)

        acc[...] += _tn(a_ref[...].astype(BF16), b_ref[...].astype(BF16))

        @pl.when(s == n_s - 1)
        def _():
            if slab_axis == 1:
                for q in range(slabs_per_step):
                    o_ref[q] = acc[:, per * q:per * (q + 1)].astype(BF16)
            else:
                o_ref[...] = acc[...].astype(o_ref.dtype)

    if slab_axis == 1:
        out_spec = pl.BlockSpec((slabs_per_step, tka, per), lambda i, j, s: (j, i, 0))
        out_shape = jax.ShapeDtypeStruct((N_DEV, Ka, per), BF16)
    else:
        out_spec = pl.BlockSpec((tka, tn), lambda i, j, s: (i, j))
        out_shape = jax.ShapeDtypeStruct((Ka, N), F32 if slab_axis is None else BF16)
    out = pl.pallas_call(
        body, name=name, grid=(Ka // tka, N // tn, n_s),
        in_specs=[pl.BlockSpec((ts, tka), lambda i, j, s: (s, i)), pl.BlockSpec((ts, tn), lambda i, j, s: (s, j))],
        out_specs=out_spec, out_shape=out_shape,
        scratch_shapes=[pltpu.VMEM((tka, tn), F32)],
        compiler_params=_params("arbitrary", "arbitrary", "arbitrary"),
    )(a, b)
    return out.reshape(N_DEV, Ka // N_DEV, N) if slab_axis == 0 else out


def _nt_rmsbwd(a, w, xin, g, dres, name, emit_bf16, slabs=()):
    S, K = a.shape
    tm, tk = _tile(S, 1024), _tile(K, 1024 if emit_bf16 else 2048)
    n_i, n_k = S // tm, K // tk
    n_x, n_o = len(slabs), 3 if emit_bf16 else 2

    def body(*refs):
        a_ref, w_ref, x_ref, g_ref, r_ref = refs[:5]
        o_ref = refs[5 + n_x]
        rest = refs[6 + n_x:5 + n_x + n_o] + (refs[5 + 2 * n_x + n_o],)
        dg_ref, acc = rest[-2], rest[-1]
        scatter = lambda: _AllToAll(refs[5:5 + n_x], refs[5 + n_x + n_o:5 + 2 * n_x + n_o], refs[6 + 2 * n_x + n_o:], False)
        i, k = pl.program_id(0), pl.program_id(1)

        if n_x:
            @pl.when((i == 0) & (k == 0))
            def _():
                scatter().start()

        @pl.when(k == 0)
        def _():
            acc[...] = jnp.zeros_like(acc)

        acc[...] += _nt(a_ref[...], w_ref[...])

        @pl.when(k == n_k - 1)
        def _():
            @pl.when(i == 0)
            def _():
                dg_ref[...] = jnp.zeros_like(dg_ref)

            for rows in _row_chunks(tm):
                du = acc[rows, :]
                r, xh = _rms(x_ref[rows, :])
                out = r_ref[rows, :] + _rms_bwd(du, g_ref[...], r, xh)
                o_ref[rows, :] = out
                if emit_bf16:
                    rest[0][rows, :] = out.astype(BF16)
                dg_ref[...] += _sum8(du * xh)

        if n_x:
            @pl.when((i == n_i - 1) & (k == n_k - 1))
            def _():
                scatter().wait()

    row = pl.BlockSpec((tm, D), lambda i, k: (i, 0))
    any_spec = pl.BlockSpec(memory_space=pl.ANY)
    out_shape = [jax.ShapeDtypeStruct((S, D), F32)]
    out_specs = [row]
    if emit_bf16:
        out_shape.append(jax.ShapeDtypeStruct((S, D), BF16))
        out_specs.append(row)
    out_shape.append(jax.ShapeDtypeStruct((8, D), F32))
    out_specs.append(pl.BlockSpec((8, D), lambda i, k: (0, 0)))
    out = pl.pallas_call(
        body, name=name, grid=(n_i, n_k),
        in_specs=[pl.BlockSpec((tm, tk), lambda i, k: (i, k)), pl.BlockSpec((D, tk), lambda i, k: (0, k)),
                  row, pl.BlockSpec((1, D), lambda i, k: (0, 0)), row] + [any_spec] * n_x,
        out_specs=out_specs + [any_spec] * n_x,
        out_shape=out_shape + [jax.ShapeDtypeStruct(b.shape, b.dtype) for b in slabs],
        scratch_shapes=[pltpu.VMEM((tm, D), F32)] + (_exchange_sems(n_x) if n_x else []),
        compiler_params=_params("arbitrary", "arbitrary"),
    )(a, w, xin, g, dres, *slabs)
    return (*out[:n_o], out[n_o:]) if n_x else out


def _merge(og, ofox, omem, wg, wf, wm, pg):
    S = og.shape[0]
    tm = _tile(S, 512)

    def body(og_ref, of_ref, om_ref, wg_ref, wf_ref, wm_ref, pg_ref, y_ref, mg_ref):
        tot = None
        for i, (o_ref, w_ref) in enumerate(((og_ref, wg_ref), (of_ref, wf_ref), (om_ref, wm_ref))):
            y = _nn(o_ref[...].astype(BF16), w_ref[...])
            y_ref[i] = y.astype(BF16)
            t = jax.nn.sigmoid(pg_ref[:, D * i:D * (i + 1)].astype(F32)) * y
            tot = t if tot is None else tot + t
        mg_ref[...] = tot.astype(BF16)

    o_spec = pl.BlockSpec((tm, 512), lambda i: (i, 0))
    w_spec = pl.BlockSpec((512, D), lambda i: (0, 0))
    return pl.pallas_call(
        body, name="merge", grid=(S // tm,),
        in_specs=[o_spec, o_spec, o_spec, w_spec, w_spec, w_spec, pl.BlockSpec((tm, G_W), lambda i: (i, 0))],
        out_specs=[pl.BlockSpec((3, tm, D), lambda i: (0, i, 0)), pl.BlockSpec((tm, D), lambda i: (i, 0))],
        out_shape=[jax.ShapeDtypeStruct((3, S, D), BF16), jax.ShapeDtypeStruct((S, D), BF16)],
        compiler_params=_params("arbitrary"),
    )(og, ofox, omem, wg, wf, wm, pg)


def _out_proj(mg, w_out, x, g_ffn):
    S = x.shape[0]
    tm = _tile(S, 512)

    def body(mg_ref, w_ref, x_ref, g_ref, h_ref, u_ref):
        h = x_ref[...] + _nn(mg_ref[...], w_ref[...])
        h_ref[...] = h
        r, xh = _rms(h)
        u_ref[...] = (xh * g_ref[...]).astype(BF16)

    row = pl.BlockSpec((tm, D), lambda i: (i, 0))
    return pl.pallas_call(
        body, name="out_proj", grid=(S // tm,),
        in_specs=[row, pl.BlockSpec((D, D), lambda i: (0, 0)), row, pl.BlockSpec((1, D), lambda i: (0, 0))],
        out_specs=[row, row],
        out_shape=[jax.ShapeDtypeStruct((S, D), F32), jax.ShapeDtypeStruct((S, D), BF16)],
        compiler_params=_params("arbitrary"),
    )(mg, w_out, x, g_ffn)


def _ff1(u2, w1):
    S = u2.shape[0]
    tm, tn = _tile(S, 1024), 1024

    def body(u_ref, w_ref, a_ref, act_ref):
        a = _nn(u_ref[...], w_ref[...])
        a_ref[...] = a.astype(BF16)
        act_ref[...] = jnp.square(jnp.maximum(a, 0.0)).astype(BF16)

    blk = pl.BlockSpec((tm, tn), lambda i, j: (i, j))
    return pl.pallas_call(
        body, name="ff1", grid=(S // tm, D_FF // tn),
        in_specs=[pl.BlockSpec((tm, D), lambda i, j: (i, 0)), pl.BlockSpec((D, tn), lambda i, j: (0, j))],
        out_specs=[blk, blk],
        out_shape=[jax.ShapeDtypeStruct((S, D_FF), BF16), jax.ShapeDtypeStruct((S, D_FF), BF16)],
        compiler_params=_params("arbitrary", "arbitrary"),
    )(u2, w1)


def _ff2_loss(act, w2, h1, g_final, target):
    S = act.shape[0]
    tm, tk = _tile(S, 1024), 1024
    n_k = D_FF // tk

    def body(a_ref, w_ref, h_ref, g_ref, t_ref, d_ref, db_ref, ls_ref, dg_ref, acc):
        i, k = pl.program_id(0), pl.program_id(1)

        @pl.when(k == 0)
        def _():
            acc[...] = jnp.zeros_like(acc)

        acc[...] += _nn(a_ref[...], w_ref[...])

        @pl.when(k == n_k - 1)
        def _():
            @pl.when(i == 0)
            def _():
                ls_ref[...] = jnp.zeros_like(ls_ref)
                dg_ref[...] = jnp.zeros_like(dg_ref)

            gf = g_ref[...]
            for rows in _row_chunks(tm):
                r, xh = _rms(h_ref[rows, :] + acc[rows, :])
                err = xh * gf - t_ref[rows, :]
                dy = err * (1.0 / D)
                dh = _rms_bwd(dy, gf, r, xh)
                d_ref[rows, :] = dh
                db_ref[rows, :] = dh.astype(BF16)
                ls_ref[...] += _sum8(err * err)
                dg_ref[...] += _sum8(dy * xh)

    row = pl.BlockSpec((tm, D), lambda i, k: (i, 0))
    part = pl.BlockSpec((8, D), lambda i, k: (0, 0))
    return pl.pallas_call(
        body, name="ff2_loss", grid=(S // tm, n_k),
        in_specs=[pl.BlockSpec((tm, tk), lambda i, k: (i, k)), pl.BlockSpec((tk, D), lambda i, k: (k, 0)),
                  row, pl.BlockSpec((1, D), lambda i, k: (0, 0)), row],
        out_specs=[row, row, part, part],
        out_shape=[jax.ShapeDtypeStruct((S, D), F32), jax.ShapeDtypeStruct((S, D), BF16),
                   jax.ShapeDtypeStruct((8, D), F32), jax.ShapeDtypeStruct((8, D), F32)],
        scratch_shapes=[pltpu.VMEM((tm, D), F32)],
        compiler_params=_params("arbitrary", "arbitrary"),
    )(act, w2, h1, g_final, target)


def _dact(dh2b, w2, a):
    S = a.shape[0]
    tm, tn = _tile(S, 1024), 1024

    def body(d_ref, w_ref, a_ref, o_ref):
        da = _nt(d_ref[...], w_ref[...])
        o_ref[...] = (da * (2.0 * jnp.maximum(a_ref[...].astype(F32), 0.0))).astype(BF16)

    blk = pl.BlockSpec((tm, tn), lambda i, j: (i, j))
    return pl.pallas_call(
        body, name="dact", grid=(S // tm, D_FF // tn),
        in_specs=[pl.BlockSpec((tm, D), lambda i, j: (i, 0)), pl.BlockSpec((tn, D), lambda i, j: (j, 0)), blk],
        out_specs=blk, out_shape=jax.ShapeDtypeStruct((S, D_FF), BF16),
        compiler_params=_params("arbitrary", "arbitrary"),
    )(dh2b, w2, a)


def _dmerge(dh1b, w_out, pg, y3, wg, wf, wm, slabs):
    S = dh1b.shape[0]
    tm = _tile(S, 512)
    n_i, n_x = S // tm, len(slabs)

    def body(*refs):
        d_ref, w_ref, pg_ref, y_ref, wg_ref, wf_ref, wm_ref = refs[:7]
        outs = refs[7 + n_x:14 + n_x]
        scatter = lambda: _AllToAll(refs[7:7 + n_x], refs[14 + n_x:14 + 2 * n_x], refs[14 + 2 * n_x:], False)
        dy_refs, do_refs, dg_ref = outs[0:3], outs[3:6], outs[6]

        @pl.when(pl.program_id(0) == 0)
        def _():
            scatter().start()

        dm = _nt(d_ref[...], w_ref[...])
        for i, wo_ref in enumerate((wg_ref, wf_ref, wm_ref)):
            gt = jax.nn.sigmoid(pg_ref[:, D * i:D * (i + 1)].astype(F32))
            dy = (dm * gt).astype(BF16)
            dy_refs[i][...] = dy
            do_refs[i][...] = _nt(dy, wo_ref[...])
            dg_ref[:, D * i:D * (i + 1)] = (dm * y_ref[i].astype(F32) * (gt * (1.0 - gt))).astype(BF16)

        @pl.when(pl.program_id(0) == n_i - 1)
        def _():
            scatter().wait()

    row = pl.BlockSpec((tm, D), lambda i: (i, 0))
    half = pl.BlockSpec((tm, 512), lambda i: (i, 0))
    w_spec = pl.BlockSpec((512, D), lambda i: (0, 0))
    any_spec = pl.BlockSpec(memory_space=pl.ANY)
    out = pl.pallas_call(
        body, name="dmerge", grid=(n_i,),
        in_specs=[row, pl.BlockSpec((D, D), lambda i: (0, 0)), pl.BlockSpec((tm, G_W), lambda i: (i, 0)),
                  pl.BlockSpec((3, tm, D), lambda i: (0, i, 0)), w_spec, w_spec, w_spec] + [any_spec] * n_x,
        out_specs=[row, row, row, half, half, half, pl.BlockSpec((tm, G_W), lambda i: (i, P_GT // G_W))] + [any_spec] * n_x,
        out_shape=[jax.ShapeDtypeStruct((S, D), BF16)] * 3 + [jax.ShapeDtypeStruct((S, 512), F32)] * 3
        + [jax.ShapeDtypeStruct((S, P_W), BF16)] + [jax.ShapeDtypeStruct(b.shape, b.dtype) for b in slabs],
        scratch_shapes=_exchange_sems(n_x),
        compiler_params=_params("arbitrary"),
    )(dh1b, w_out, pg, y3, wg, wf, wm, *slabs)
    return (*out[:7], out[7:])


def _gla_block_terms(gq_ref, gk_ref, ps_ref, wau_ref, ba_ref, tb):
    gaff = ps_ref[:, 512:640]
    z = _nn(gaff.astype(BF16), wau_ref[...]) + ba_ref[...]
    la = _log_sigmoid(z) * GLA_TAU_INV
    rr = lax.broadcasted_iota(jnp.int32, (tb, tb), 0)
    cc = lax.broadcasted_iota(jnp.int32, (tb, tb), 1)
    same = jnp.right_shift(rr, 6) == jnp.right_shift(cc, 6)
    tri = jnp.where(same & (cc <= rr), 1.0, 0.0).astype(BF16)
    ones = jnp.where(same, 1.0, 0.0).astype(BF16)
    b = _sum01(tri, la)
    bl = _sum01(ones, la)
    e_pos, e_neg, e_last, dec = jnp.exp(b), jnp.exp(-b), jnp.exp(bl - b), jnp.exp(bl)
    q = gq_ref[...].astype(F32) * GLA_SCALE
    k = gk_ref[...].astype(F32)
    return dict(gaff=gaff, z=z, same=same, rr=rr, cc=cc, ones=ones, e_pos=e_pos, e_neg=e_neg, e_last=e_last, dec=dec,
                qp=q * e_pos, qn=q * e_neg, kn=k * e_neg, kp=k * e_pos, kd=k * e_last)


def _head_masked(x, store):
    lane = lax.broadcasted_iota(jnp.int32, x.shape, 1)
    for h in range(GLA_H):
        store[:, h] = jnp.where(jnp.right_shift(lane, 6) == h, x, 0.0).astype(BF16).reshape(-1, CHUNK, GLA_K)


def _lower4():
    t = jnp.bitwise_and(lax.broadcasted_iota(jnp.int32, (GLA_H * CHUNK, CHUNK), 0), CHUNK - 1)
    return t >= lax.broadcasted_iota(jnp.int32, (GLA_H * CHUNK, CHUNK), 1)


def _stack_heads(ref, rows):
    return jnp.concatenate([ref[rows, GLA_DV * h:GLA_DV * (h + 1)] for h in range(GLA_H)], axis=0)


def _gla_fwd(pa, ps, wau, ba, gh):
    S = pa.shape[0]
    tb = _tile(S, 512)
    n_c = tb // CHUNK
    n_b = S // tb

    def body(gq_ref, gk_ref, gv_ref, ps_ref, wau_ref, ba_ref, gh_ref, o_ref, og_ref, sp_ref,
             qpm, qnm, kdm, kn_s, kp_s, dec_s, state):
        @pl.when(pl.program_id(0) == 0)
        def _():
            state[...] = jnp.zeros_like(state)

        t = _gla_block_terms(gq_ref, gk_ref, ps_ref, wau_ref, ba_ref, tb)
        _head_masked(t["qp"], qpm)
        _head_masked(t["qn"], qnm)
        _head_masked(t["kd"], kdm)
        kn_s[...] = t["kn"].astype(BF16)
        kp_s[...] = t["kp"].astype(BF16)
        dec_s[...] = t["dec"]
        lower = _lower4()

        sp = state[...]
        for c in range(n_c):
            rows = slice(c * CHUNK, (c + 1) * CHUNK)
            sp_ref[c] = sp
            qp, qn, kd = (s[c].reshape(GLA_H * CHUNK, GLA_K) for s in (qpm, qnm, kdm))
            attn = jnp.where(lower, _nt(qp, kn_s[rows, :]), _nt(qn, kp_s[rows, :])).astype(BF16)
            inter = _nt(qp, sp.astype(BF16))
            for h in range(GLA_H):
                mine = slice(CHUNK * h, CHUNK * (h + 1))
                cols = slice(GLA_DV * h, GLA_DV * (h + 1))
                o_ref[rows, cols] = _nn(attn[mine], gv_ref[rows, cols]) + inter[mine]
            sp = sp * dec_s[c * CHUNK:c * CHUNK + 1, :] + _tn(_stack_heads(gv_ref, rows), kd)
        state[...] = sp
        for h in range(GLA_H):
            cols = slice(GLA_DV * h, GLA_DV * (h + 1))
            r, xh = _rms(o_ref[:, cols])
            gg = ps_ref[:, cols]
            og_ref[:, cols] = ((xh * gh_ref[:, cols]) * (gg * jax.nn.sigmoid(gg))).astype(BF16)

    return pl.pallas_call(
        body, name="gla_fwd", grid=(n_b,),
        in_specs=[pl.BlockSpec((tb, GLA_K), lambda i: (i, 0)), pl.BlockSpec((tb, GLA_K), lambda i: (i, 1)),
                  pl.BlockSpec((tb, GLA_V), lambda i: (i, 1)), pl.BlockSpec((tb, S_W), lambda i: (i, 0)),
                  pl.BlockSpec((128, GLA_K), lambda i: (0, 0)), pl.BlockSpec((1, GLA_K), lambda i: (0, 0)),
                  pl.BlockSpec((1, GLA_V), lambda i: (0, 0))],
        out_specs=[pl.BlockSpec((tb, GLA_V), lambda i: (i, 0)), pl.BlockSpec((tb, GLA_V), lambda i: (i, 0)),
                   pl.BlockSpec((n_c, GLA_DV, GLA_K), lambda i: (i, 0, 0))],
        out_shape=[jax.ShapeDtypeStruct((S, GLA_V), F32), jax.ShapeDtypeStruct((S, GLA_V), BF16),
                   jax.ShapeDtypeStruct((S // CHUNK, GLA_DV, GLA_K), F32)],
        scratch_shapes=[pltpu.VMEM((n_c, GLA_H, CHUNK, GLA_K), BF16)] * 3
        + [pltpu.VMEM((tb, GLA_K), BF16), pltpu.VMEM((tb, GLA_K), BF16), pltpu.VMEM((tb, GLA_K), F32),
           pltpu.VMEM((GLA_DV, GLA_K), F32)],
        compiler_params=_params("arbitrary"),
    )(pa, pa, pa, ps, wau, ba, gh)


def _gla_bwd(pa, ps, wau, ba, gh, o_gla, d_og, sprev, dgaff_fox, d_proj, slabs):
    S = pa.shape[0]
    tb = _tile(S, 512)
    n_c = tb // CHUNK
    n_b = S // tb
    n_x = len(slabs)
    c_gk, c_gv, c_gg, c_ga, c_end = GLA_K, 2 * GLA_K, 2 * GLA_K + GLA_V, 2 * GLA_K + 2 * GLA_V, 2 * GLA_K + 2 * GLA_V + 128

    def body(*refs):
        gq_ref, gk_ref, gv_ref, ps_ref, wau_ref, ba_ref, gh_ref, o_ref, dog_ref, sp_ref, dfx_ref = refs[:11]
        dp_ref, dwau_ref, dba_ref, dgh_ref = refs[12 + n_x:16 + n_x]
        (qpm, qnm, kdm, kn_s, kp_s, dec_s, do_s, dqp_s, dqn_s, dkn_s, dkp_s, dkd_s, ddec_s,
         dstate) = refs[16 + 2 * n_x:30 + 2 * n_x]
        scatter = lambda: _AllToAll(refs[12:12 + n_x], refs[16 + n_x:16 + 2 * n_x], refs[30 + 2 * n_x:], False)
        first = pl.program_id(0) == 0
        dp_ref[:, c_end:] = jnp.zeros((tb, P_GLA_W - c_end), BF16)

        @pl.when(first)
        def _():
            dstate[...] = jnp.zeros_like(dstate)
            scatter().start()

        t = _gla_block_terms(gq_ref, gk_ref, ps_ref, wau_ref, ba_ref, tb)
        _head_masked(t["qp"], qpm)
        _head_masked(t["qn"], qnm)
        _head_masked(t["kd"], kdm)
        kn_s[...] = t["kn"].astype(BF16)
        kp_s[...] = t["kp"].astype(BF16)
        dec_s[...] = t["dec"]

        dgh_parts = []
        for h in range(GLA_H):
            cols = slice(GLA_DV * h, GLA_DV * (h + 1))
            r, xh = _rms(o_ref[:, cols])
            g = gh_ref[:, cols]
            gg = ps_ref[:, cols]
            sg = jax.nn.sigmoid(gg)
            d_out = dog_ref[:, cols]
            dp_ref[:, c_gg + GLA_DV * h:c_gg + GLA_DV * (h + 1)] = (d_out * (xh * g) * (sg * (1.0 + gg * (1.0 - sg)))).astype(BF16)
            d_on = d_out * (gg * sg)
            dgh_parts.append(_sum8(d_on * xh))
            do_s[:, cols] = _rms_bwd(d_on, g, r, xh).astype(BF16)
        dgh_part = jnp.concatenate(dgh_parts, axis=1)

        lower = _lower4()
        lane = lax.broadcasted_iota(jnp.int32, (CHUNK, GLA_K), 1)

        def own_columns(stacked):
            return sum(jnp.where(jnp.right_shift(lane, 6) == h, stacked[CHUNK * h:CHUNK * (h + 1)], 0.0) for h in range(GLA_H))

        ds_next = dstate[...]
        for c in reversed(range(n_c)):
            rows = slice(c * CHUNK, (c + 1) * CHUNK)
            dsb = ds_next.astype(BF16)
            sp = sp_ref[c]
            knc, kpc = kn_s[rows, :], kp_s[rows, :]
            qp, qn, kd = (s[c].reshape(GLA_H * CHUNK, GLA_K) for s in (qpm, qnm, kdm))
            v4, do4 = _stack_heads(gv_ref, rows), _stack_heads(do_s, rows)
            ddec_s[rows, :] = jnp.broadcast_to(jnp.sum(ds_next * sp, axis=0, keepdims=True), (CHUNK, GLA_K))
            attn = jnp.where(lower, _nt(qp, knc), _nt(qn, kpc)).astype(BF16)
            da = jnp.concatenate([_nt(do4[CHUNK * h:CHUNK * (h + 1)], v4[CHUNK * h:CHUNK * (h + 1)]) for h in range(GLA_H)],
                                 axis=0)
            dac = jnp.where(lower, da, 0.0).astype(BF16)
            daa = jnp.where(lower, 0.0, da).astype(BF16)
            dqp_s[rows, :] = own_columns(_nn(dac, knc) + _nn(do4, sp.astype(BF16)))
            dqn_s[rows, :] = own_columns(_nn(daa, kpc))
            dkd_s[rows, :] = own_columns(_nn(v4, dsb))
            dkn_s[rows, :] = _tn(dac, qp)
            dkp_s[rows, :] = _tn(daa, qn)
            dv_state = _nt(kd, dsb)
            for h in range(GLA_H):
                mine = slice(CHUNK * h, CHUNK * (h + 1))
                dp_ref[rows, c_gv + GLA_DV * h:c_gv + GLA_DV * (h + 1)] = (_tn(attn[mine], do4[mine]) + dv_state[mine]).astype(BF16)
            ds_next = ds_next * dec_s[c * CHUNK:c * CHUNK + 1, :] + _tn(do4, qp)
        dstate[...] = ds_next

        dqp, dqn, dkn, dkp, dkd = dqp_s[...], dqn_s[...], dkn_s[...], dkp_s[...], dkd_s[...]
        dp_ref[:, 0:c_gk] = ((dqp * t["e_pos"] + dqn * t["e_neg"]) * GLA_SCALE).astype(BF16)
        dp_ref[:, c_gk:c_gv] = (dkn * t["e_neg"] + dkp * t["e_pos"] + dkd * t["e_last"]).astype(BF16)
        kd_term = dkd * t["kd"]
        db = dqp * t["qp"] - dqn * t["qn"] - dkn * t["kn"] + dkp * t["kp"] - kd_term
        upper = jnp.where(t["same"] & (t["cc"] >= t["rr"]), 1.0, 0.0).astype(BF16)
        dla = (_sum01(upper, db) + _sum01(t["ones"], kd_term)
               + ddec_s[...] * t["dec"])
        dz = dla * GLA_TAU_INV * jax.nn.sigmoid(-t["z"])
        dzb = dz.astype(BF16)
        dp_ref[:, c_ga:c_end] = (_nt(dzb, wau_ref[...]) + dfx_ref[...]).astype(BF16)
        dwau_part = _tn(t["gaff"].astype(BF16), dzb)
        dba_part = _sum8(dz)

        @pl.when(first)
        def _():
            dwau_ref[...] = dwau_part
            dba_ref[...] = dba_part
            dgh_ref[...] = dgh_part

        @pl.when(jnp.logical_not(first))
        def _():
            dwau_ref[...] += dwau_part
            dba_ref[...] += dba_part
            dgh_ref[...] += dgh_part

        @pl.when(pl.program_id(0) == n_b - 1)
        def _():
            scatter().wait()

    rev = lambda i: (n_b - 1 - i, 0)
    f32k = pltpu.VMEM((tb, GLA_K), F32)
    bf4 = pltpu.VMEM((n_c, GLA_H, CHUNK, GLA_K), BF16)
    any_spec = pl.BlockSpec(memory_space=pl.ANY)
    out = pl.pallas_call(
        body, name="gla_bwd", grid=(n_b,),
        in_specs=[pl.BlockSpec((tb, GLA_K), rev), pl.BlockSpec((tb, GLA_K), lambda i: (n_b - 1 - i, 1)),
                  pl.BlockSpec((tb, GLA_V), lambda i: (n_b - 1 - i, 1)), pl.BlockSpec((tb, S_W), rev),
                  pl.BlockSpec((128, GLA_K), lambda i: (0, 0)), pl.BlockSpec((1, GLA_K), lambda i: (0, 0)),
                  pl.BlockSpec((1, GLA_V), lambda i: (0, 0)), pl.BlockSpec((tb, GLA_V), rev), pl.BlockSpec((tb, GLA_V), rev),
                  pl.BlockSpec((n_c, GLA_DV, GLA_K), lambda i: (n_b - 1 - i, 0, 0)), pl.BlockSpec((tb, 128), rev),
                  any_spec] + [any_spec] * n_x,
        out_specs=[pl.BlockSpec((tb, P_GLA_W), lambda i: (n_b - 1 - i, P_GLA // P_GLA_W)),
                   pl.BlockSpec((128, GLA_K), lambda i: (0, 0)), pl.BlockSpec((8, GLA_K), lambda i: (0, 0)),
                   pl.BlockSpec((8, GLA_V), lambda i: (0, 0))] + [any_spec] * n_x,
        out_shape=[jax.ShapeDtypeStruct((S, P_W), BF16), jax.ShapeDtypeStruct((128, GLA_K), F32),
                   jax.ShapeDtypeStruct((8, GLA_K), F32), jax.ShapeDtypeStruct((8, GLA_V), F32)]
        + [jax.ShapeDtypeStruct(b.shape, b.dtype) for b in slabs],
        input_output_aliases={11: 0},
        scratch_shapes=[bf4, bf4, bf4, pltpu.VMEM((tb, GLA_K), BF16), pltpu.VMEM((tb, GLA_K), BF16), f32k,
                        pltpu.VMEM((tb, GLA_V), BF16), f32k, f32k, f32k, f32k, f32k, f32k, pltpu.VMEM((GLA_DV, GLA_K), F32)]
        + _exchange_sems(n_x),
        compiler_params=_params("arbitrary"),
    )(pa, pa, pa, ps, wau, ba, gh, o_gla, d_og, sprev, dgaff_fox, d_proj, *slabs)
    return out[0], out[1], out[2], out[3], out[4:]


def _split3(x):
    x1 = x.astype(BF16).astype(F32)
    x2 = (x - x1).astype(BF16).astype(F32)
    x3 = (x - x1 - x2).astype(BF16).astype(F32)
    return x1, x2, x3


def _fox_prep(pa, ps, bfg):
    S = pa.shape[0]
    tm = _tile(S, FOX_TK)

    def body(ps_ref, b_ref, fq_ref, fk_ref, fv_ref, q_ref, k_ref, qt_ref, kt_ref, vt_ref, st_ref, carry):
        @pl.when(pl.program_id(0) == 0)
        def _():
            carry[...] = jnp.zeros_like(carry)

        vt = fv_ref[...].astype(F32).T.astype(BF16)
        ones_row = jnp.where(lax.broadcasted_iota(jnp.int32, (FOX_VT - FOX_DH, tm), 0) == 0, 1.0, 0.0).astype(BF16)
        for h in range(FOX_H):
            vt_ref[FOX_VT * h:FOX_VT * h + FOX_DH, :] = vt[FOX_DH * h:FOX_DH * (h + 1), :]
            vt_ref[FOX_VT * h + FOX_DH:FOX_VT * (h + 1), :] = ones_row
        lf = _log_sigmoid(ps_ref[...] + b_ref[...])
        rr = lax.broadcasted_iota(jnp.int32, (tm, tm), 0)
        cc = lax.broadcasted_iota(jnp.int32, (tm, tm), 1)
        tri = jnp.where(cc <= rr, 1.0, 0.0).astype(F32)
        f = jnp.dot(tri, lf, preferred_element_type=F32, precision=HIGHEST) + carry[0:1, :]
        carry[...] = jnp.broadcast_to(f[tm - 1:tm, :], carry.shape)
        f1, f2, f3 = _split3(f)
        lane = lax.broadcasted_iota(jnp.int32, (tm, 128), 1)
        st_row = lax.broadcasted_iota(jnp.int32, (8, 128), 0)
        st_lane = lax.broadcasted_iota(jnp.int32, (8, 128), 1)
        stats = jnp.zeros((8, 128), F32)
        for h in range(FOX_H):
            cols = slice(128 * h, 128 * (h + 1))
            c = FF_LANE + h
            a1, a2, a3 = f1[:, c:c + 1], f2[:, c:c + 1], f3[:, c:c + 1]
            q = fq_ref[:, cols].astype(F32) * FOX_SCALE
            k = fk_ref[:, cols].astype(F32)
            fh = f[:, c:c + 1]
            vals = (jnp.max(jnp.sum(q * q, axis=-1, keepdims=True)), jnp.max(jnp.sum(k * k, axis=-1, keepdims=True)),
                    jnp.max(fh), jnp.min(fh), jnp.min(jnp.sum(q * k, axis=-1, keepdims=True)))
            for n, val in enumerate(vals):
                stats = jnp.where((st_row == h) & (st_lane == n), val, stats)
            for n, a in enumerate((a1, a2, a3)):
                q = jnp.where(lane == AUG + n, a, q)
                k = jnp.where(lane == AUG + 3 + n, -a, k)
            q = jnp.where((lane >= AUG + 3) & (lane < AUG + 6), 1.0, q)
            k = jnp.where((lane >= AUG) & (lane < AUG + 3), 1.0, k)
            q_ref[:, cols] = q.astype(BF16)
            k_ref[:, cols] = k.astype(BF16)
            qt_ref[cols, :] = q.T.astype(BF16)
            kt_ref[cols, :] = k.T.astype(BF16)
        st_ref[0] = stats

    wide = lambda j: pl.BlockSpec((tm, 1024), lambda i: (i, j))
    tall = lambda n: pl.BlockSpec((n, tm), lambda i: (0, i))
    return pl.pallas_call(
        body, name="fox_prep", grid=(S // tm,),
        in_specs=[pl.BlockSpec((tm, 128), lambda i: (i, 4)), pl.BlockSpec((1, 128), lambda i: (0, 0)), wide(1), wide(2),
                  pl.BlockSpec((tm, FOX_W), lambda i: (i, A_FV // FOX_W))],
        out_specs=[wide(0), wide(0), tall(1024), tall(1024), tall(FOX_H * FOX_VT), pl.BlockSpec((1, 8, 128), lambda i: (i, 0, 0))],
        out_shape=[jax.ShapeDtypeStruct((S, 1024), BF16), jax.ShapeDtypeStruct((S, 1024), BF16),
                   jax.ShapeDtypeStruct((1024, S), BF16), jax.ShapeDtypeStruct((1024, S), BF16),
                   jax.ShapeDtypeStruct((FOX_H * FOX_VT, S), BF16), jax.ShapeDtypeStruct((S // tm, 8, 128), F32)],
        scratch_shapes=[pltpu.VMEM((8, 128), F32)],
        compiler_params=_params("arbitrary"),
    )(ps, bfg, pa, pa, pa)


FOX_PRUNE_AT = -90.0


def _fox_live_ranges(stats, n_sub, ratio):
    n_b = stats.shape[0]
    q2, k2, f_max, f_min, own = (stats[:, :, n].T for n in range(5))
    slack = 0.01 * jnp.sqrt(q2 * k2) + 1e-5 * jnp.abs(f_max) + 1.0
    bound = (1.01 * jnp.sqrt(q2[:, :, None] * k2[:, None, :]) + (f_max + slack - own)[:, :, None]
             - (f_min - 1e-5 * jnp.abs(f_min))[:, None, :])
    blocks = jnp.arange(n_b)
    dead = (bound <= FOX_PRUNE_AT) & (blocks[None, :] < blocks[:, None])[None]
    dead_fwd = dead.reshape(FOX_H, n_b // n_sub, n_sub, n_b).all(axis=2)
    first = jnp.sum(jnp.cumprod(dead_fwd.astype(jnp.int32), axis=2), axis=2)
    last_live = n_b - 1 - jnp.sum(jnp.cumprod(dead[:, ::-1, :].astype(jnp.int32), axis=1), axis=1)
    first_wide = blocks // ratio + 1
    narrow_end = jnp.minimum(jnp.minimum(first_wide * ratio, n_b)[None], last_live + 1)
    wide_end = jnp.where(last_live >= (first_wide * ratio)[None], last_live // ratio + 1, first_wide[None])
    return first.astype(jnp.int32), narrow_end.astype(jnp.int32), wide_end.astype(jnp.int32)


def _fox_fwd(qa, ka, vt, first):
    S = qa.shape[0]
    tq = _tile(S, FOX_TQ)
    tk = _tile(tq, FOX_TK)
    n_sub = tq // tk

    def body(first_ref, q_ref, k_ref, vt_ref, o_ref, lse_ref):
        pair, i = pl.program_id(0), pl.program_id(1)
        both = lambda f: tuple(f(hh) for hh in range(2))

        def blk(j, carry, diag, heads=(0, 1)):
            ks = pl.ds(pl.multiple_of(j * tk, tk), tk)
            q0 = 0 if diag is None else diag * tk

            def head(hh):
                if hh not in heads:
                    return carry[hh]
                m, acc = carry[hh]
                mo, ao = m[:, q0:], acc[:, q0:]
                s = _nt(k_ref[ks, 128 * hh:128 * (hh + 1)], q_ref[q0:, 128 * hh:128 * (hh + 1)])
                if diag is not None:
                    live = lax.broadcasted_iota(jnp.int32, s.shape, 1) >= lax.broadcasted_iota(jnp.int32, s.shape, 0)
                    s = jnp.where(live, s, NEG)
                mn = jnp.maximum(mo, jnp.max(s, axis=0, keepdims=True))
                p = jnp.exp((s - mn).astype(BF16))
                an = jnp.exp(mo - mn) * ao + _nn(vt_ref[FOX_VT * hh:FOX_VT * (hh + 1), ks], p)
                if q0:
                    mn, an = (jnp.concatenate([old[:, :q0], new], axis=1) for old, new in ((m, mn), (acc, an)))
                return mn, an

            return both(head)

        one = (jnp.full((1, tq), NEG, F32), jnp.zeros((FOX_VT, tq), F32))
        past = i * n_sub
        f0, f1 = first_ref[2 * pair, i], first_ref[2 * pair + 1, i]
        join = jnp.maximum(f0, f1)
        solo = lambda hh: lambda c: lax.fori_loop(jnp.minimum(f0, f1), join, lambda j, cc: blk(j, cc, None, (hh,)), c)
        carry = lax.cond(f0 < f1, solo(0), solo(1), (one, one))
        n_both = past - join
        carry = lax.fori_loop(0, n_both // 2, lambda jj, c: blk(join + 2 * jj + 1, blk(join + 2 * jj, c, None), None), carry)
        carry = lax.cond(n_both % 2 == 1, lambda c: blk(past - 1, c, None), lambda c: c, carry)
        for d in range(n_sub):
            carry = blk(past + d, carry, d)
        (m0, a0), (m1, a1) = carry
        l0, l1 = a0[FOX_DH:FOX_DH + 1], a1[FOX_DH:FOX_DH + 1]
        o_ref[...] = jnp.concatenate([a0[:FOX_DH] / l0, a1[:FOX_DH] / l1], axis=0).T
        lse_ref[0, 0:1, :] = m0 + jnp.log(l0)
        lse_ref[0, 1:2, :] = m1 + jnp.log(l1)
        lse_ref[0, 2:8, :] = jnp.zeros((6, tq), F32)

    return pl.pallas_call(
        body, name="fox_fwd", grid=(FOX_H // 2, S // tq),
        in_specs=[pl.BlockSpec(memory_space=pltpu.SMEM), pl.BlockSpec((tq, 256), lambda p, i: (i, p)),
                  pl.BlockSpec((S, 256), lambda p, i: (0, p)), pl.BlockSpec((2 * FOX_VT, S), lambda p, i: (p, 0))],
        out_specs=[pl.BlockSpec((tq, 128), lambda p, i: (i, p)), pl.BlockSpec((1, 8, tq), lambda p, i: (p, 0, i))],
        out_shape=[jax.ShapeDtypeStruct((S, FOX_W), F32), jax.ShapeDtypeStruct((FOX_H // 2, 8, S), F32)],
        compiler_params=_params("arbitrary", "arbitrary"),
    )(first, qa, ka, vt)


def _fox_delta(d_o, o):
    S = o.shape[0]
    tm = _tile(S, 512)

    def body(d_ref, o_ref, db_ref, dbt_ref, dl_ref):
        d = d_ref[...]
        db_ref[...] = d.astype(BF16)
        dbt_ref[...] = d.T.astype(BF16)
        prod = d * o_ref[...]
        rr = lax.broadcasted_iota(jnp.int32, (8, 128), 0)
        cc = lax.broadcasted_iota(jnp.int32, (8, 128), 1)
        ind = jnp.where(jnp.right_shift(cc, 6) == rr, 1.0, 0.0).astype(F32)
        for p in range(FOX_H // 2):
            dl_ref[p] = lax.dot_general(ind, prod[:, 128 * p:128 * (p + 1)], (((1,), (1,)), ((), ())),
                                        preferred_element_type=F32, precision=HIGHEST)

    row = pl.BlockSpec((tm, FOX_W), lambda i: (i, 0))
    return pl.pallas_call(
        body, name="fox_delta", grid=(S // tm,),
        in_specs=[row, row],
        out_specs=[row, pl.BlockSpec((FOX_W, tm), lambda i: (0, i)), pl.BlockSpec((FOX_H // 2, 8, tm), lambda i: (0, 0, i))],
        out_shape=[jax.ShapeDtypeStruct((S, FOX_W), BF16), jax.ShapeDtypeStruct((FOX_W, S), BF16),
                   jax.ShapeDtypeStruct((FOX_H // 2, 8, S), F32)],
        compiler_params=_params("arbitrary"),
    )(d_o, o)


def _fox_bwd(qa, qat, ka, kat, pa, dob, dobt, lse, delta, narrow_end, wide_end):
    S = qa.shape[0]
    tk = _tile(S, FOX_TK)
    wide = _tile(S, FOX_BWD_WIDE)
    ratio = wide // tk
    n_wide = S // wide

    def body(ne_ref, we_ref, q_ref, qt_ref, k_ref, kt_ref, v_ref, do_ref, dot_ref, lse_ref, dl_ref, dq_ref, dk_ref, dv_ref):
        h, jb = pl.program_id(0), pl.program_id(1)
        hh = h % 2

        @pl.when(jb == 0)
        def _():
            dq_ref[...] = jnp.zeros_like(dq_ref)

        lane = lax.broadcasted_iota(jnp.int32, (tk, 128), 1)
        vm = jnp.where(jnp.right_shift(lane, 6) == hh, v_ref[...], jnp.zeros((), BF16))
        kb, ktb = k_ref[...], kt_ref[0:FOX_LIVE, :]
        mine = pl.ds(pl.multiple_of(hh * FOX_DH, FOX_DH), FOX_DH)

        def blk(ib, tq, carry, masked):
            dk, dv = carry
            qs = pl.ds(pl.multiple_of(ib * tq, tq), tq)
            p = jnp.exp(_nt(kb, q_ref[qs, :]) - lse_ref[0, pl.ds(hh, 1), qs])
            if masked:
                live = lax.broadcasted_iota(jnp.int32, p.shape, 1) >= lax.broadcasted_iota(jnp.int32, p.shape, 0)
                p = jnp.where(live, p, 0.0)
            ds = (p * (_nt(vm, do_ref[qs, :]) - dl_ref[0, pl.ds(hh, 1), qs])).astype(BF16)
            dq_ref[0:FOX_LIVE, qs] += _nn(ktb, ds)
            return dk + _nt(qt_ref[0:FOX_LIVE, qs], ds), dv + _nt(dot_ref[mine, qs], p.astype(BF16))

        carry = blk(jb, tk, (jnp.zeros((FOX_LIVE, tk), F32), jnp.zeros((FOX_DH, tk), F32)), True)
        first_wide = jb // ratio + 1
        carry = lax.fori_loop(jb + 1, ne_ref[h, jb], lambda ib, c: blk(ib, tk, c, False), carry)
        last_wide = we_ref[h, jb]
        rest = jnp.maximum(last_wide - first_wide, 0)
        carry = lax.fori_loop(0, rest // 2, lambda t, c: blk(first_wide + 2 * t + 1, wide, blk(first_wide + 2 * t, wide, c, False),
                                                             False), carry)
        dk, dv = lax.cond(rest % 2 == 1, lambda c: blk(last_wide - 1, wide, c, False), lambda c: c, carry)
        dk_ref[0:FOX_LIVE, :] = dk
        dk_ref[FOX_LIVE:, :] = jnp.zeros((128 - FOX_LIVE, tk), F32)
        dv_ref[...] = dv

    once = pl.Buffered(1)
    rows = pl.BlockSpec((1, 8, S), lambda h, j: (h // 2, 0, 0))
    return pl.pallas_call(
        body, name="fox_bwd", grid=(FOX_H, S // tk),
        in_specs=[pl.BlockSpec(memory_space=pltpu.SMEM), pl.BlockSpec(memory_space=pltpu.SMEM),
                  pl.BlockSpec((S, 128), lambda h, j: (0, h)), pl.BlockSpec((128, S), lambda h, j: (h, 0)),
                  pl.BlockSpec((tk, 128), lambda h, j: (j, h)), pl.BlockSpec((128, tk), lambda h, j: (h, j)),
                  pl.BlockSpec((tk, 128), lambda h, j: (j, A_FV // 128 + h // 2)),
                  pl.BlockSpec((S, 128), lambda h, j: (0, h // 2)), pl.BlockSpec((128, S), lambda h, j: (h // 2, 0)),
                  rows, rows],
        out_specs=[pl.BlockSpec((128, S), lambda h, j: (h, 0), pipeline_mode=once),
                   pl.BlockSpec((128, tk), lambda h, j: (h, j)), pl.BlockSpec((FOX_DH, tk), lambda h, j: (h, j))],
        out_shape=[jax.ShapeDtypeStruct((1024, S), F32), jax.ShapeDtypeStruct((1024, S), F32),
                   jax.ShapeDtypeStruct((FOX_W, S), F32)],
        compiler_params=_params("arbitrary", "arbitrary"),
    )(narrow_end, wide_end, qa, qat, ka, kat, pa, dob, dobt, lse, delta)


def _fox_post(dq, dk, dv, ps, bfg, d_proj):
    S = dq.shape[1]
    tm = _tile(S, 512)
    n_b = S // tm

    def body(dq_ref, dk_ref, dv_ref, ps_ref, b_ref, _, dp_ref, dff_ref, dbf_ref, carry):
        first = pl.program_id(0) == 0

        @pl.when(first)
        def _():
            carry[...] = jnp.zeros_like(carry)

        low = lax.broadcasted_iota(jnp.int32, (tm, 128), 1) < FOX_DH
        for h in range(FOX_H):
            blk = slice(128 * h, 128 * (h + 1))
            dp_ref[:, blk] = jnp.where(low, dq_ref[blk, :].T * FOX_SCALE, 0.0).astype(BF16)
            dp_ref[:, 1024 + 128 * h:1024 + 128 * (h + 1)] = jnp.where(low, dk_ref[blk, :].T, 0.0).astype(BF16)
        dp_ref[:, 2048:P_FOX_W] = dv_ref[...].T.astype(BF16)
        rr = lax.broadcasted_iota(jnp.int32, (FOX_H, 1024), 0)
        cc = lax.broadcasted_iota(jnp.int32, (FOX_H, 1024), 1)
        sel_k = jnp.where(cc == 128 * rr + AUG + 3, 1.0, 0.0).astype(F32)
        sel_q = jnp.where(cc == 128 * rr + AUG, 1.0, 0.0).astype(F32)
        g = (jnp.dot(sel_k, dk_ref[...], preferred_element_type=F32, precision=HIGHEST)
             - jnp.dot(sel_q, dq_ref[...], preferred_element_type=F32, precision=HIGHEST))
        t_from = lax.broadcasted_iota(jnp.int32, (tm, tm), 0)
        t_to = lax.broadcasted_iota(jnp.int32, (tm, tm), 1)
        later = jnp.where(t_from >= t_to, 1.0, 0.0).astype(F32)
        dlf = jnp.dot(-g, later, preferred_element_type=F32, precision=HIGHEST) + carry[:, 0:1]
        carry[...] = jnp.broadcast_to(dlf[:, 0:1], carry.shape)
        cols = jnp.concatenate([jnp.zeros((FF_LANE, tm), F32), dlf, jnp.zeros((128 - FF_LANE - FOX_H, tm), F32)], axis=0).T
        dff = cols * jax.nn.sigmoid(-(ps_ref[...] + b_ref[...]))
        dff_ref[...] = dff
        part = _sum8(dff)

        @pl.when(first)
        def _():
            dbf_ref[...] = part

        @pl.when(jnp.logical_not(first))
        def _():
            dbf_ref[...] += part

    rev = lambda i: (n_b - 1 - i, 0)
    tall = lambda n: pl.BlockSpec((n, tm), lambda i: (0, n_b - 1 - i))
    return pl.pallas_call(
        body, name="fox_post", grid=(n_b,),
        in_specs=[tall(1024), tall(1024), tall(FOX_W), pl.BlockSpec((tm, 128), lambda i: (n_b - 1 - i, 4)),
                  pl.BlockSpec((1, 128), lambda i: (0, 0)), pl.BlockSpec(memory_space=pl.ANY)],
        out_specs=[pl.BlockSpec((tm, P_FOX_W), lambda i: (n_b - 1 - i, P_FOX // P_FOX_W)), pl.BlockSpec((tm, 128), rev),
                   pl.BlockSpec((8, 128), lambda i: (0, 0))],
        out_shape=[jax.ShapeDtypeStruct((S, P_W), BF16), jax.ShapeDtypeStruct((S, 128), F32),
                   jax.ShapeDtypeStruct((8, 128), F32)],
        input_output_aliases={5: 0},
        scratch_shapes=[pltpu.VMEM((8, 128), F32)],
        compiler_params=_params("arbitrary"),
    )(dq, dk, dv, ps, bfg, d_proj)


def _mem_prep(mem, g_mem, wkv):
    def body(m_ref, g_ref, w_ref, mn_ref, kv_ref):
        r, xh = _rms(m_ref[...])
        mn = (xh * g_ref[...]).astype(BF16)
        mn_ref[...] = mn
        kv_ref[...] = _nn(mn, w_ref[...]).astype(BF16)

    return pl.pallas_call(
        body, name="mem_prep",
        out_shape=[jax.ShapeDtypeStruct((N_MEM, D), BF16), jax.ShapeDtypeStruct((N_MEM, 2 * MEM_W), BF16)],
        compiler_params=pltpu.CompilerParams(vmem_limit_bytes=V7X_VMEM_LIMIT),
    )(mem, g_mem, wkv)


def _mem_softmax(qh, kh):
    s = _nt(qh, kh) * MEM_SCALE
    e = jnp.exp(s - jnp.max(s, axis=-1, keepdims=True))
    return e / jnp.sum(e, axis=-1, keepdims=True)


def _mem_fwd(pa, mkv):
    S = pa.shape[0]
    tm = _tile(S, 512)

    def body(q_ref, kv_ref, o_ref):
        for h in range(MEM_H):
            cols = slice(MEM_DH * h, MEM_DH * (h + 1))
            p = _mem_softmax(q_ref[:, cols], kv_ref[:, cols])
            o_ref[:, cols] = _nn(p.astype(BF16), kv_ref[:, MEM_W + MEM_DH * h:MEM_W + MEM_DH * (h + 1)])

    return pl.pallas_call(
        body, name="mem_fwd", grid=(S // tm,),
        in_specs=[pl.BlockSpec((tm, MEM_W), lambda i: (i, A_MQ // MEM_W)), pl.BlockSpec((N_MEM, 2 * MEM_W), lambda i: (0, 0))],
        out_specs=pl.BlockSpec((tm, MEM_W), lambda i: (i, 0)),
        out_shape=jax.ShapeDtypeStruct((S, MEM_W), F32),
        compiler_params=_params("arbitrary"),
    )(pa, mkv)


def _mem_bwd(pa, mkv, d_o, d_proj):
    S = pa.shape[0]
    tm = _tile(S, 512)

    def body(q_ref, kv_ref, do_ref, _, dq_ref, dkv_ref):
        first = pl.program_id(0) == 0
        parts = []
        for h in range(MEM_H):
            cols = slice(MEM_DH * h, MEM_DH * (h + 1))
            vcols = slice(MEM_W + MEM_DH * h, MEM_W + MEM_DH * (h + 1))
            qh, kh = q_ref[:, cols], kv_ref[:, cols]
            p = _mem_softmax(qh, kh)
            dob = do_ref[:, cols].astype(BF16)
            dp = _nt(dob, kv_ref[:, vcols])
            ds = (p * (dp - jnp.sum(p * dp, axis=-1, keepdims=True)) * MEM_SCALE).astype(BF16)
            dq_ref[:, cols] = _nn(ds, kh).astype(BF16)
            parts.append((cols, _tn(ds, qh)))
            parts.append((vcols, _tn(p.astype(BF16), dob)))

        @pl.when(first)
        def _():
            for sl, v in parts:
                dkv_ref[:, sl] = v

        @pl.when(jnp.logical_not(first))
        def _():
            for sl, v in parts:
                dkv_ref[:, sl] += v

    return pl.pallas_call(
        body, name="mem_bwd", grid=(S // tm,),
        in_specs=[pl.BlockSpec((tm, MEM_W), lambda i: (i, A_MQ // MEM_W)), pl.BlockSpec((N_MEM, 2 * MEM_W), lambda i: (0, 0)),
                  pl.BlockSpec((tm, MEM_W), lambda i: (i, 0)), pl.BlockSpec(memory_space=pl.ANY)],
        out_specs=[pl.BlockSpec((tm, MEM_W), lambda i: (i, P_MQ // MEM_W)), pl.BlockSpec((N_MEM, 2 * MEM_W), lambda i: (0, 0))],
        out_shape=[jax.ShapeDtypeStruct((S, P_W), BF16), jax.ShapeDtypeStruct((N_MEM, 2 * MEM_W), F32)],
        input_output_aliases={3: 0},
        compiler_params=_params("arbitrary"),
    )(pa, mkv, d_o, d_proj)


def _mem_prep_bwd(mem, g_mem, mn, wkv, dkv):
    def body(m_ref, g_ref, mn_ref, w_ref, d_ref, dw_ref, dg_ref):
        db = d_ref[...].astype(BF16)
        dw_ref[...] = _tn(mn_ref[...], db).astype(BF16)
        r, xh = _rms(m_ref[...])
        dg_ref[...] = _sum8(_nt(db, w_ref[...]) * xh)

    dw, dg = pl.pallas_call(
        body, name="mem_prep_bwd",
        out_shape=[jax.ShapeDtypeStruct((D, 2 * MEM_W), BF16), jax.ShapeDtypeStruct((8, D), F32)],
        compiler_params=pltpu.CompilerParams(vmem_limit_bytes=V7X_VMEM_LIMIT),
    )(mem, g_mem, mn, wkv, dkv)
    return dw.reshape(N_DEV, D // N_DEV, 2 * MEM_W), dg


def _rearrange_w_in(w):
    def heads128(cols):
        blk = w[:, cols:cols + FOX_W].reshape(D, FOX_H, FOX_DH)
        return jnp.pad(blk, ((0, 0), (0, 0), (0, 128 - FOX_DH))).reshape(D, FOX_H * 128)

    fq, fk, fv, mq, wg = heads128(O_FQ), heads128(O_FK), w[:, O_FV:O_FF], w[:, O_MQ:O_GT], w[:, O_GT:]
    gaff = jnp.concatenate([w[:, O_GA:O_FQ], w[:, O_FF:O_MQ], jnp.zeros((D, 128 - GLA_R - FOX_H), w.dtype)], axis=1)
    wa = jnp.concatenate([w[:, O_GQ:O_GG], fq, fk, fv, mq], axis=1)
    ws = jnp.concatenate([w[:, O_GG:O_GA], gaff], axis=1)
    wp = jnp.concatenate([fq, fk, fv, mq, wg, w[:, O_GQ:O_GG], ws, jnp.zeros((D, P_W - P_GLA - 1024 - S_W), w.dtype)], axis=1)
    return wa, wg, ws, wp


def _restore_w_in_grad(dwp):
    def unheads(off):
        return dwp[:, off:off + FOX_H * 128].reshape(D, FOX_H, 128)[:, :, :FOX_DH].reshape(D, FOX_W)

    g0 = P_GLA + 1024
    return jnp.concatenate([
        dwp[:, P_GLA:g0], dwp[:, g0:g0 + 512], dwp[:, g0 + 512:g0 + 512 + GLA_R], unheads(P_FOX), unheads(P_FOX + 1024),
        dwp[:, P_FOX + 2048:P_FOX + P_FOX_W], dwp[:, g0 + 512 + GLA_R:g0 + 512 + GLA_R + FOX_H], dwp[:, P_MQ:P_GT],
        dwp[:, P_GT:P_GLA]], axis=1)


def _local_step(x, mem, target, p, late_shards):
    S = x.shape[0]
    p = dict(p)
    wa, wg, ws, wp = _rearrange_w_in(p["w_in"])
    wau = jnp.pad(p["w_alpha_up"], ((0, 128 - GLA_R), (0, 0)))
    bfg = jnp.pad(p["b_forget"], ((0, 0), (FF_LANE, 128 - FF_LANE - FOX_H)))
    gh = p["g_gla_head"].reshape(1, GLA_V)

    pa, pg, ps, u, gathered = _proj(x, p["g_mix"], wa, wg, ws, late_shards)
    p.update({n: _unslab(t, ax) for (n, ax), t in zip(BIG[1:], gathered)})
    o_gla, og, sprev = _gla_fwd(pa, ps, wau, p["b_alpha"], gh)
    qa, ka, qat, kat, vt, fox_stats = _fox_prep(pa, ps, bfg)
    fox_tk = _tile(S, FOX_TK)
    fox_first, fox_narrow_end, fox_wide_end = _fox_live_ranges(fox_stats, _tile(S, FOX_TQ) // fox_tk,
                                                               _tile(S, FOX_BWD_WIDE) // fox_tk)
    o_fox, lse = _fox_fwd(qa, ka, vt, fox_first)
    mn, mkv = _mem_prep(mem, p["g_mem"], p["w_mem_kv"])
    o_mem = _mem_fwd(pa, mkv)
    y3, mg = _merge(og, o_fox, o_mem, p["w_gla_o"], p["w_fox_o"], p["w_mem_o"], pg)
    h1, u2 = _out_proj(mg, p["w_out"], x, p["g_ffn"])
    a, act = _ff1(u2, p["w_ff1"])
    dh2, dh2b, loss8, dg_final = _ff2_loss(act, p["w_ff2"], h1, p["g_final"].reshape(1, D), target)

    d_a = _dact(dh2b, p["w_ff2"], a)
    dw_ff2 = _wgrad(act, dh2b, "wgrad_ff2", 0)
    dh1, dh1b, dg_ffn = _nt_rmsbwd(d_a, p["w_ff1"], h1, p["g_ffn"], dh2, "dffn", True)
    dw_ff1 = _wgrad(u2, d_a, "wgrad_ff1", 1)
    dy_g, dy_f, dy_m, do_g, do_f, do_m, d_proj, arrived_ff = _dmerge(dh1b, p["w_out"], pg, y3, p["w_gla_o"], p["w_fox_o"],
                                                                     p["w_mem_o"], [dw_ff1, dw_ff2])
    dw_out = _wgrad(mg, dh1b, "wgrad_out", 0)
    dw_gla_o = _wgrad(og, dy_g, "wgrad_gla_o", 1)
    dw_fox_o = _wgrad(o_fox, dy_f, "wgrad_fox_o", 1)
    dw_mem_o = _wgrad(o_mem, dy_m, "wgrad_mem_o", 1)
    d_proj, d_mkv = _mem_bwd(pa, mkv, do_m, d_proj)
    dw_mem_kv, dg_mem = _mem_prep_bwd(mem, p["g_mem"], mn, p["w_mem_kv"], d_mkv)
    dob, dobt, delta = _fox_delta(do_f, o_fox)
    dq, dk, dv = _fox_bwd(qa, qat, ka, kat, pa, dob, dobt, lse, delta, fox_narrow_end, fox_wide_end)
    d_proj, dgaff_fox, db_forget = _fox_post(dq, dk, dv, ps, bfg, d_proj)
    d_proj, dw_au, db_alpha, dg_gla, arrived = _gla_bwd(pa, ps, wau, p["b_alpha"], gh, o_gla, do_g, sprev, dgaff_fox, d_proj,
                                                        [dw_mem_kv, dw_gla_o, dw_fox_o, dw_mem_o, dw_out])
    dw_in = _slabs(_restore_w_in_grad(_wgrad(u, d_proj, "wgrad_in")), 1).astype(BF16)
    dx, dg_mix, arrived_in = _nt_rmsbwd(d_proj, wp, x, p["g_mix"], dh1, "dmix", False, [dw_in])

    big = dict(w_in=arrived_in[0], w_ff1=arrived_ff[0], w_ff2=arrived_ff[1],
               **dict(zip(("w_mem_kv", "w_gla_o", "w_fox_o", "w_mem_o", "w_out"), arrived)))
    small = dict(g_mix=dg_mix, g_mem=dg_mem, g_ffn=dg_ffn, g_final=dg_final, b_alpha=db_alpha, g_gla_head=dg_gla,
                 b_forget=db_forget, w_alpha_up=dw_au, loss=loss8)
    return dx, big, small


BIG = (("w_in", 1), ("w_mem_kv", 0), ("w_gla_o", 1), ("w_fox_o", 1), ("w_mem_o", 1), ("w_out", 0), ("w_ff1", 1), ("w_ff2", 0))


def _peer(d):
    me = lax.axis_index("x") * 4 + lax.axis_index("y") * 2 + lax.axis_index("c")
    t = (me + d) % N_DEV
    return (t // 4, (t // 2) % 2, t % 2), me


def _exchange_sems(n):
    return [pltpu.SemaphoreType.DMA((n, N_DEV - 1)), pltpu.SemaphoreType.DMA((n, N_DEV - 1)), pltpu.SemaphoreType.DMA((n,))]


def _exchange_call(body, blocks, out_shape, name):
    n = len(blocks)
    any_spec = pl.BlockSpec(memory_space=pl.ANY)
    return pl.pallas_call(body, name=name, in_specs=[any_spec] * n, out_specs=[any_spec] * n, out_shape=out_shape,
                          scratch_shapes=_exchange_sems(n))(*blocks)


class _AllToAll:
    def __init__(self, ins, outs, sems, gather):
        send, recv, loc = sems
        n = len(ins)
        _, me = _peer(0)
        src = (lambda k, j: ins[k]) if gather else (lambda k, j: ins[k].at[j])
        self.local = [pltpu.make_async_copy(src(k, me), outs[k].at[me], loc.at[k]) for k in range(n)]
        self.remote = []
        for d in range(1, N_DEV):
            to, _ = _peer(d)
            self.remote += [pltpu.make_async_remote_copy(
                src_ref=src(k, (me + d) % N_DEV), dst_ref=outs[k].at[me], send_sem=send.at[k, d - 1],
                recv_sem=recv.at[k, d - 1], device_id=to, device_id_type=MESH) for k in range(n)]

    def start(self):
        for cp in self.local + self.remote:
            cp.start()

    def wait(self):
        for cp in self.remote:
            cp.wait_send()
        for cp in self.remote:
            cp.wait_recv()
        for cp in self.local:
            cp.wait()


def _gathered_shapes(shards):
    return [jax.ShapeDtypeStruct((N_DEV,) + b.shape, b.dtype) for b in shards]


def _gather_weights(shards):
    n = len(shards)

    def body(*refs):
        ins, outs = refs[:n], refs[n:2 * n]
        send, recv, loc = refs[2 * n:]
        x, y, c = lax.axis_index("x"), lax.axis_index("y"), lax.axis_index("c")
        sibling = (x, y, 1 - c)
        chips = [(1 - x, y), (x, 1 - y), (1 - x, 1 - y)]
        slot = lambda px, py, pc: px * 4 + py * 2 + pc

        def copy(k, s, block, to, src=None):
            rows = outs[k].at[slot(*block)]
            return pltpu.make_async_remote_copy(src_ref=rows if src is None else src, dst_ref=rows, send_sem=send.at[k, s],
                                                recv_sem=recv.at[k, s], device_id=to, device_id_type=MESH)

        me = (x, y, c)
        own = [pltpu.make_async_copy(ins[k], outs[k].at[slot(*me)], loc.at[k]) for k in range(n)]
        first = [copy(k, 0, me, sibling, src=ins[k]) for k in range(n)]
        first += [copy(k, 1 + j, me, (*chip, c), src=ins[k]) for j, chip in enumerate(chips) for k in range(n)]
        for cp in own + first:
            cp.start()
        passed = []
        for j, chip in enumerate(chips):
            for k in range(n):
                copy(k, 1 + j, (*chip, c), me).wait_recv()
                fwd = copy(k, 4 + j, (*chip, c), sibling)
                fwd.start()
                passed.append(fwd)
        for k in range(n):
            copy(k, 0, sibling, me).wait_recv()
        for j, chip in enumerate(chips):
            for k in range(n):
                copy(k, 4 + j, (*chip, 1 - c), me).wait_recv()
        for cp in first + passed:
            cp.wait_send()
        for cp in own:
            cp.wait()

    return _exchange_call(body, shards, [jax.ShapeDtypeStruct((N_DEV,) + b.shape, b.dtype) for b in shards], "gather_weights")


def _adamw_math(g, w, m, v):
    m2 = ADAM_B1 * m + (1.0 - ADAM_B1) * g
    v2 = ADAM_B2 * v + (1.0 - ADAM_B2) * jnp.square(g)
    m_hat = m2 / (1.0 - ADAM_B1 ** ADAM_STEP)
    v_hat = v2 / (1.0 - ADAM_B2 ** ADAM_STEP)
    delta = -ADAM_LR * (m_hat / (jnp.sqrt(v_hat) + ADAM_EPS) + ADAM_WD * w)
    return delta, m2, v2


def _adamw_sum(parts, w, m, v, name):
    R, C = w.shape
    tr = _tile(R, 128)

    def body(p_ref, w_ref, m_ref, v_ref, g_ref, d_ref, m2_ref, v2_ref):
        g = p_ref[0].astype(F32)
        for j in range(1, p_ref.shape[0]):
            g = g + p_ref[j].astype(F32)
        g_ref[...] = g
        d_ref[...], m2_ref[...], v2_ref[...] = _adamw_math(g, w_ref[...], m_ref[...], v_ref[...])

    blk = pl.BlockSpec((tr, C), lambda i: (i, 0))
    return pl.pallas_call(
        body, name=name, grid=(R // tr,),
        in_specs=[pl.BlockSpec((parts.shape[0], tr, C), lambda i: (0, i, 0)), blk, blk, blk],
        out_specs=[blk] * 4, out_shape=[jax.ShapeDtypeStruct((R, C), F32)] * 4,
        compiler_params=_params("arbitrary"),
    )(parts, w, m, v)


SMALL_ROWS = 24


def _pack_small(d):
    mixed = jnp.concatenate([d["b_alpha"].reshape(1, GLA_K), d["g_gla_head"].reshape(1, GLA_V),
                             jnp.pad(d["b_forget"].reshape(1, FOX_H), ((0, 0), (FF_LANE, 128 - FF_LANE - FOX_H))),
                             jnp.zeros((1, 128), F32)], axis=1)
    rows = [d["g_mix"].reshape(1, D), d["g_mem"].reshape(1, D), d["g_ffn"].reshape(1, D), d["g_final"].reshape(1, D), mixed,
            jnp.zeros((3, D), F32), jnp.pad(d["w_alpha_up"].reshape(GLA_R, GLA_K), ((0, 0), (0, D - GLA_K)))]
    return jnp.concatenate(rows, axis=0)


def _unpack_small(t):
    return dict(g_mix=t[0:1], g_mem=t[1:2], g_ffn=t[2:3], g_final=t[3], b_alpha=t[4:5, 0:GLA_K],
                g_gla_head=t[4:5, GLA_K:GLA_K + GLA_V].reshape(1, GLA_H, GLA_DV),
                b_forget=t[4:5, 768 + FF_LANE:768 + FF_LANE + FOX_H], w_alpha_up=t[8:24, 0:GLA_K].reshape(1, GLA_R, GLA_K))


def _small_allreduce(small, w, m, v):
    def body(gm, gme, gf, gfi, ba, gg, bf, wau, ls, w_ref, m_ref, v_ref, g_ref, d_ref, m2_ref, v2_ref, l_ref,
             buf, send, recv):
        _, me = _peer(0)
        buf[me] = jnp.zeros((SMALL_ROWS, D), F32)
        for r, ref in enumerate((gm, gme, gf, gfi)):
            buf[me, r:r + 1, :] = jnp.sum(ref[...], axis=0, keepdims=True)
        buf[me, 4:5, 0:GLA_K] = jnp.sum(ba[...], axis=0, keepdims=True)
        buf[me, 4:5, GLA_K:GLA_K + GLA_V] = jnp.sum(gg[...], axis=0, keepdims=True)
        buf[me, 4:5, 768:896] = jnp.sum(bf[...], axis=0, keepdims=True)
        lrow = jnp.sum(ls[...], axis=0, keepdims=True)
        lsum = lrow[:, 0:128]
        for c in range(1, D // 128):
            lsum = lsum + lrow[:, 128 * c:128 * (c + 1)]
        buf[me, 4:5, 896:1024] = lsum
        buf[me, 8:24, 0:GLA_K] = wau[0:GLA_R, :]
        remote = []
        for d in range(1, N_DEV):
            to, me = _peer(d)
            cp = pltpu.make_async_remote_copy(src_ref=buf.at[me], dst_ref=buf.at[me], send_sem=send.at[d - 1],
                                              recv_sem=recv.at[d - 1], device_id=to, device_id_type=MESH)
            cp.start()
            remote.append(cp)
        for cp in remote:
            cp.wait_send()
        for cp in remote:
            cp.wait_recv()
        g = buf[0]
        for j in range(1, N_DEV):
            g = g + buf[j]
        g_ref[...] = g
        d_ref[...], m2_ref[...], v2_ref[...] = _adamw_math(g, w_ref[...], m_ref[...], v_ref[...])
        l_ref[...] = g[4:5, 896:1024]

    packed = jax.ShapeDtypeStruct((SMALL_ROWS, D), F32)
    return pl.pallas_call(
        body, name="small_allreduce",
        out_shape=[packed, packed, packed, packed, jax.ShapeDtypeStruct((1, 128), F32)],
        scratch_shapes=[pltpu.VMEM((N_DEV, SMALL_ROWS, D), F32), pltpu.SemaphoreType.DMA((N_DEV - 1,)),
                        pltpu.SemaphoreType.DMA((N_DEV - 1,))],
    )(small["g_mix"], small["g_mem"], small["g_ffn"], small["g_final"], small["b_alpha"], small["g_gla_head"],
      small["b_forget"], small["w_alpha_up"], small["loss"], w, m, v)


def _slabs(g, axis):
    R, C = g.shape
    if axis == 0:
        return g.reshape(N_DEV, R // N_DEV, C)
    return g.reshape(R, N_DEV, C // N_DEV).transpose(1, 0, 2)


def _unslab(t, axis):
    n, r, c = t.shape
    if axis == 0:
        return t.reshape(n * r, c)
    return t.transpose(1, 0, 2).reshape(r, n * c)


def kernel(x, mem, g_mix, w_in, w_alpha_up, b_alpha, b_forget, g_gla_head, g_mem, w_mem_kv, w_gla_o, w_fox_o, w_mem_o, w_out, g_ffn, w_ff1, w_ff2, g_final, loss_target, m_g_mix, m_w_in, m_w_alpha_up, m_b_alpha, m_b_forget, m_g_gla_head, m_g_mem, m_w_mem_kv, m_w_gla_o, m_w_fox_o, m_w_mem_o, m_w_out, m_g_ffn, m_w_ff1, m_w_ff2, m_g_final, v_g_mix, v_w_in, v_w_alpha_up, v_b_alpha, v_b_forget, v_g_gla_head, v_g_mem, v_w_mem_kv, v_w_gla_o, v_w_fox_o, v_w_mem_o, v_w_out, v_g_ffn, v_w_ff1, v_w_ff2, v_g_final):
    names = ["g_mix", "w_in", "w_alpha_up", "b_alpha", "b_forget", "g_gla_head", "g_mem", "w_mem_kv", "w_gla_o", "w_fox_o",
             "w_mem_o", "w_out", "g_ffn", "w_ff1", "w_ff2", "g_final"]
    w = dict(g_mix=g_mix, w_in=w_in, w_alpha_up=w_alpha_up, b_alpha=b_alpha, b_forget=b_forget, g_gla_head=g_gla_head,
             g_mem=g_mem, w_mem_kv=w_mem_kv, w_gla_o=w_gla_o, w_fox_o=w_fox_o, w_mem_o=w_mem_o, w_out=w_out, g_ffn=g_ffn,
             w_ff1=w_ff1, w_ff2=w_ff2, g_final=g_final)
    m = dict(g_mix=m_g_mix, w_in=m_w_in, w_alpha_up=m_w_alpha_up, b_alpha=m_b_alpha, b_forget=m_b_forget,
             g_gla_head=m_g_gla_head, g_mem=m_g_mem, w_mem_kv=m_w_mem_kv, w_gla_o=m_w_gla_o, w_fox_o=m_w_fox_o,
             w_mem_o=m_w_mem_o, w_out=m_w_out, g_ffn=m_g_ffn, w_ff1=m_w_ff1, w_ff2=m_w_ff2, g_final=m_g_final)
    v = dict(g_mix=v_g_mix, w_in=v_w_in, w_alpha_up=v_w_alpha_up, b_alpha=v_b_alpha, b_forget=v_b_forget,
             g_gla_head=v_g_gla_head, g_mem=v_g_mem, w_mem_kv=v_w_mem_kv, w_gla_o=v_w_gla_o, w_fox_o=v_w_fox_o,
             w_mem_o=v_w_mem_o, w_out=v_w_out, g_ffn=v_g_ffn, w_ff1=v_w_ff1, w_ff2=v_w_ff2, g_final=v_g_final)
    me = lax.axis_index("x") * 4 + lax.axis_index("y") * 2 + lax.axis_index("c")

    shard = lambda n: w[n][0].astype(BF16)
    w_in_all, w_au_all = _gather_weights([shard("w_in"), shard("w_alpha_up")])
    p = dict(w_in=_unslab(w_in_all, 1), w_alpha_up=_unslab(w_au_all, 1), g_mix=g_mix, b_alpha=b_alpha, b_forget=b_forget,
             g_gla_head=g_gla_head, g_mem=g_mem, g_ffn=g_ffn, g_final=g_final)

    dx, big, small = _local_step(x[0], mem[0], loss_target[0], p, [shard(n) for n, _ in BIG[1:]])

    out_g, out_d, out_m, out_v = {}, {}, {}, {}
    for n, _ in BIG:
        g_, d_, m_, v_ = _adamw_sum(big[n], w[n][0], m[n][0], v[n][0], "adamw_" + n)
        out_g[n], out_d[n], out_m[n], out_v[n] = g_[None], d_[None], m_[None], v_[None]

    full = lambda d: dict(d, w_alpha_up=jnp.zeros((1, GLA_R, GLA_K), F32))
    gs, ds, ms, vs, lrow = _small_allreduce(small, _pack_small(full(w)), _pack_small(full(m)), _pack_small(full(v)))
    g_s, d_s, m_s, v_s = _unpack_small(gs), _unpack_small(ds), _unpack_small(ms), _unpack_small(vs)
    for n in names:
        if n not in out_g and n != "w_alpha_up":
            out_g[n], out_d[n], out_m[n], out_v[n] = g_s[n], d_s[n], m_s[n], v_s[n]
    g_au = lax.dynamic_slice_in_dim(g_s["w_alpha_up"][0], me * (GLA_K // N_DEV), GLA_K // N_DEV, axis=1)
    g_, d_, m_, v_ = _adamw_sum(g_au[None], w_alpha_up[0], m_w_alpha_up[0], v_w_alpha_up[0], "adamw_w_alpha_up")
    out_g["w_alpha_up"], out_d["w_alpha_up"], out_m["w_alpha_up"], out_v["w_alpha_up"] = g_[None], d_[None], m_[None], v_[None]

    loss = jnp.sum(lrow) * (0.5 / D)
    return (loss, dx[None], *[out_g[n] for n in names], *[out_d[n] for n in names], *[out_m[n] for n in names],
            *[out_v[n] for n in names])
```

```python
import jax
import jax.numpy as jnp
from jax import lax
from jax.experimental import pallas as pl
from jax.experimental.pallas import tpu as pltpu

F32, BF16 = jnp.float32, jnp.bfloat16
HIGHEST = lax.Precision.HIGHEST
MESH = pl.DeviceIdType.MESH

N_DEV = 8
D = 1024
EPS = 1e-6
CHUNK = 64
N_MEM = 256
GLA_H, GLA_DK, GLA_DV = 4, 64, 128
GLA_K, GLA_V, GLA_R = 256, 512, 16
FOX_H, FOX_DH, FOX_W = 8, 64, 512
MEM_H, MEM_DH, MEM_W = 4, 128, 512
D_FF = 4096
D_IN = 6680
FOX_SCALE = 0.125
GLA_SCALE = 0.125
MEM_SCALE = MEM_DH ** -0.5
GLA_TAU_INV = 1.0 / 16.0
NEG = -1e30

O_GQ, O_GK, O_GV, O_GG, O_GA, O_FQ, O_FK, O_FV, O_FF, O_MQ, O_GT = 0, 256, 512, 1024, 1536, 1552, 2064, 2576, 3088, 3096, 3608
A_FQ, A_FK, A_FV, A_MQ, A_W = 1024, 2048, 3072, 3584, 4096
S_W = 640
G_W = 3072
P_FOX, P_FOX_W, P_MQ, P_GT, P_GLA, P_GLA_W, P_W = 0, 2560, 2560, 3072, 6144, 2048, 8192
FF_LANE = 16
AUG = 64
FOX_LIVE = 80
FOX_VT = 80

ADAM_LR, ADAM_B1, ADAM_B2, ADAM_EPS, ADAM_WD, ADAM_STEP = 0.001, 0.9, 0.999, 1e-08, 0.01, 10
V7X_VMEM_LIMIT = 54 * 1024 * 1024
FOX_TK = 512
FOX_TQ = 2048
FOX_BWD_WIDE = 1024


def _params(*sem):
    return pltpu.CompilerParams(dimension_semantics=sem, vmem_limit_bytes=V7X_VMEM_LIMIT)


def _nt(a, b):
    return lax.dot_general(a, b, (((1,), (1,)), ((), ())), preferred_element_type=F32)


def _tn(a, b):
    return lax.dot_general(a, b, (((0,), (0,)), ((), ())), preferred_element_type=F32)


def _nn(a, b):
    return jnp.dot(a, b, preferred_element_type=F32)


def _log_sigmoid(z):
    return jnp.minimum(z, 0.0) - jnp.log(1.0 + jnp.exp(-jnp.abs(z)))


def _sum01(m01, x):
    x1 = x.astype(BF16)
    x2 = (x - x1.astype(F32)).astype(BF16)
    x3 = (x - x1.astype(F32) - x2.astype(F32)).astype(BF16)
    return _nn(m01, x1) + _nn(m01, x2) + _nn(m01, x3)


def _sum8(x):
    return x.reshape(x.shape[0] // 8, 8, x.shape[1]).sum(axis=0)


def _rms(xv):
    r = lax.rsqrt(jnp.mean(xv * xv, axis=-1, keepdims=True) + EPS)
    return r, xv * r


def _rms_bwd(du, g, r, xh):
    w = du * g
    return r * (w - xh * jnp.mean(w * xh, axis=-1, keepdims=True))


def _row_chunks(n, size=256):
    return [slice(r, r + min(size, n)) for r in range(0, n, min(size, n))]


def _tile(n, pref):
    t = min(n, pref)
    assert n % t == 0, (n, t)
    return t


def _proj(x, g, wa, wg, ws, shards):
    S = x.shape[0]
    tm, tn = _tile(S, 1024), 1024
    n_a, n_g = A_W // tn, G_W // tn
    n_i, n_j = S // tm, n_a + n_g + 1
    n_x = len(shards)

    def body(*refs):
        x_ref, g_ref, wa_ref, wg_ref, ws_ref = refs[:5]
        pa_ref, pg_ref, ps_ref, u_ref = refs[5 + n_x:9 + n_x]
        u_s = refs[9 + 2 * n_x]
        gather = lambda: _AllToAll(refs[5:5 + n_x], refs[9 + n_x:9 + 2 * n_x], refs[10 + 2 * n_x:], True)
        i, j = pl.program_id(0), pl.program_id(1)

        @pl.when((i == 0) & (j == 0))
        def _():
            gather().start()

        @pl.when(j == 0)
        def _():
            r, xh = _rms(x_ref[...])
            u_s[...] = (xh * g_ref[...]).astype(BF16)
            u_ref[...] = u_s[...]

        @pl.when(j < n_a)
        def _():
            pa_ref[...] = _nn(u_s[...], wa_ref[...]).astype(BF16)

        @pl.when((j >= n_a) & (j < n_a + n_g))
        def _():
            pg_ref[...] = _nn(u_s[...], wg_ref[...]).astype(BF16)

        @pl.when(j == n_a + n_g)
        def _():
            ps_ref[...] = _nn(u_s[...], ws_ref[...])

        @pl.when((i == n_i - 1) & (j == n_j - 1))
        def _():
            gather().wait()

    in_a = lambda j: jnp.minimum(j, n_a - 1)
    in_g = lambda j: jnp.clip(j - n_a, 0, n_g - 1)
    row = pl.BlockSpec((tm, D), lambda i, j: (i, 0))
    any_spec = pl.BlockSpec(memory_space=pl.ANY)
    out = pl.pallas_call(
        body, name="proj", grid=(n_i, n_j),
        in_specs=[row, pl.BlockSpec((1, D), lambda i, j: (0, 0)), pl.BlockSpec((D, tn), lambda i, j: (0, in_a(j))),
                  pl.BlockSpec((D, tn), lambda i, j: (0, in_g(j))),
                  pl.BlockSpec((D, S_W), lambda i, j: (0, 0), pipeline_mode=pl.Buffered(1))] + [any_spec] * n_x,
        out_specs=[pl.BlockSpec((tm, tn), lambda i, j: (i, in_a(j))), pl.BlockSpec((tm, tn), lambda i, j: (i, in_g(j))),
                   pl.BlockSpec((tm, S_W), lambda i, j: (i, 0)), row] + [any_spec] * n_x,
        out_shape=[jax.ShapeDtypeStruct((S, A_W), BF16), jax.ShapeDtypeStruct((S, G_W), BF16),
                   jax.ShapeDtypeStruct((S, S_W), F32), jax.ShapeDtypeStruct((S, D), BF16)] + _gathered_shapes(shards),
        scratch_shapes=[pltpu.VMEM((tm, D), BF16)] + _exchange_sems(n_x),
        compiler_params=_params("arbitrary", "arbitrary"),
    )(x, g, wa, wg, ws, *shards)
    return out[0], out[1], out[2], out[3], out[4:]


def _wgrad(a, b, name, slab_axis=None):
    S, Ka = a.shape
    N = b.shape[1]
    tka, tn, ts = _tile(Ka, 1024), _tile(N, 1024), _tile(S, 2048)
    n_s = S // ts
    per = N // N_DEV
    slabs_per_step = tn // per

    def body(a_ref, b_ref, o_ref, acc):
        s = pl.program_id(2)

        @pl.when(s == 0)
        def _():
            acc[...] = jnp.zeros_like(acc)

        acc[...] += _tn(a_ref[...].astype(BF16), b_ref[...].astype(BF16))

        @pl.when(s == n_s - 1)
        def _():
            if slab_axis == 1:
                for q in range(slabs_per_step):
                    o_ref[q] = acc[:, per * q:per * (q + 1)].astype(BF16)
            else:
                o_ref[...] = acc[...].astype(o_ref.dtype)

    if slab_axis == 1:
        out_spec = pl.BlockSpec((slabs_per_step, tka, per), lambda i, j, s: (j, i, 0))
        out_shape = jax.ShapeDtypeStruct((N_DEV, Ka, per), BF16)
    else:
        out_spec = pl.BlockSpec((tka, tn), lambda i, j, s: (i, j))
        out_shape = jax.ShapeDtypeStruct((Ka, N), F32 if slab_axis is None else BF16)
    out = pl.pallas_call(
        body, name=name, grid=(Ka // tka, N // tn, n_s),
        in_specs=[pl.BlockSpec((ts, tka), lambda i, j, s: (s, i)), pl.BlockSpec((ts, tn), lambda i, j, s: (s, j))],
        out_specs=out_spec, out_shape=out_shape,
        scratch_shapes=[pltpu.VMEM((tka, tn), F32)],
        compiler_params=_params("arbitrary", "arbitrary", "arbitrary"),
    )(a, b)
    return out.reshape(N_DEV, Ka // N_DEV, N) if slab_axis == 0 else out


def _nt_rmsbwd(a, w, xin, g, dres, name, emit_bf16, slabs=()):
    S, K = a.shape
    tm, tk = _tile(S, 1024), _tile(K, 1024 if emit_bf16 else 2048)
    n_i, n_k = S // tm, K // tk
    n_x, n_o = len(slabs), 3 if emit_bf16 else 2

    def body(*refs):
        a_ref, w_ref, x_ref, g_ref, r_ref = refs[:5]
        o_ref = refs[5 + n_x]
        rest = refs[6 + n_x:5 + n_x + n_o] + (refs[5 + 2 * n_x + n_o],)
        dg_ref, acc = rest[-2], rest[-1]
        scatter = lambda: _AllToAll(refs[5:5 + n_x], refs[5 + n_x + n_o:5 + 2 * n_x + n_o], refs[6 + 2 * n_x + n_o:], False)
        i, k = pl.program_id(0), pl.program_id(1)

        if n_x:
            @pl.when((i == 0) & (k == 0))
            def _():
                scatter().start()

        @pl.when(k == 0)
        def _():
            acc[...] = jnp.zeros_like(acc)

        acc[...] += _nt(a_ref[...], w_ref[...])

        @pl.when(k == n_k - 1)
        def _():
            @pl.when(i == 0)
            def _():
                dg_ref[...] = jnp.zeros_like(dg_ref)

            for rows in _row_chunks(tm):
                du = acc[rows, :]
                r, xh = _rms(x_ref[rows, :])
                out = r_ref[rows, :] + _rms_bwd(du, g_ref[...], r, xh)
                o_ref[rows, :] = out
                if emit_bf16:
                    rest[0][rows, :] = out.astype(BF16)
                dg_ref[...] += _sum8(du * xh)

        if n_x:
            @pl.when((i == n_i - 1) & (k == n_k - 1))
            def _():
                scatter().wait()

    row = pl.BlockSpec((tm, D), lambda i, k: (i, 0))
    any_spec = pl.BlockSpec(memory_space=pl.ANY)
    out_shape = [jax.ShapeDtypeStruct((S, D), F32)]
    out_specs = [row]
    if emit_bf16:
        out_shape.append(jax.ShapeDtypeStruct((S, D), BF16))
        out_specs.append(row)
    out_shape.append(jax.ShapeDtypeStruct((8, D), F32))
    out_specs.append(pl.BlockSpec((8, D), lambda i, k: (0, 0)))
    out = pl.pallas_call(
        body, name=name, grid=(n_i, n_k),
        in_specs=[pl.BlockSpec((tm, tk), lambda i, k: (i, k)), pl.BlockSpec((D, tk), lambda i, k: (0, k)),
                  row, pl.BlockSpec((1, D), lambda i, k: (0, 0)), row] + [any_spec] * n_x,
        out_specs=out_specs + [any_spec] * n_x,
        out_shape=out_shape + [jax.ShapeDtypeStruct(b.shape, b.dtype) for b in slabs],
        scratch_shapes=[pltpu.VMEM((tm, D), F32)] + (_exchange_sems(n_x) if n_x else []),
        compiler_params=_params("arbitrary", "arbitrary"),
    )(a, w, xin, g, dres, *slabs)
    return (*out[:n_o], out[n_o:]) if n_x else out


def _merge(og, ofox, omem, wg, wf, wm, pg):
    S = og.shape[0]
    tm = _tile(S, 512)

    def body(og_ref, of_ref, om_ref, wg_ref, wf_ref, wm_ref, pg_ref, y_ref, mg_ref):
        tot = None
        for i, (o_ref, w_ref) in enumerate(((og_ref, wg_ref), (of_ref, wf_ref), (om_ref, wm_ref))):
            y = _nn(o_ref[...].astype(BF16), w_ref[...])
            y_ref[i] = y.astype(BF16)
            t = jax.nn.sigmoid(pg_ref[:, D * i:D * (i + 1)].astype(F32)) * y
            tot = t if tot is None else tot + t
        mg_ref[...] = tot.astype(BF16)

    o_spec = pl.BlockSpec((tm, 512), lambda i: (i, 0))
    w_spec = pl.BlockSpec((512, D), lambda i: (0, 0))
    return pl.pallas_call(
        body, name="merge", grid=(S // tm,),
        in_specs=[o_spec, o_spec, o_spec, w_spec, w_spec, w_spec, pl.BlockSpec((tm, G_W), lambda i: (i, 0))],
        out_specs=[pl.BlockSpec((3, tm, D), lambda i: (0, i, 0)), pl.BlockSpec((tm, D), lambda i: (i, 0))],
        out_shape=[jax.ShapeDtypeStruct((3, S, D), BF16), jax.ShapeDtypeStruct((S, D), BF16)],
        compiler_params=_params("arbitrary"),
    )(og, ofox, omem, wg, wf, wm, pg)


def _out_proj(mg, w_out, x, g_ffn):
    S = x.shape[0]
    tm = _tile(S, 512)

    def body(mg_ref, w_ref, x_ref, g_ref, h_ref, u_ref):
        h = x_ref[...] + _nn(mg_ref[...], w_ref[...])
        h_ref[...] = h
        r, xh = _rms(h)
        u_ref[...] = (xh * g_ref[...]).astype(BF16)

    row = pl.BlockSpec((tm, D), lambda i: (i, 0))
    return pl.pallas_call(
        body, name="out_proj", grid=(S // tm,),
        in_specs=[row, pl.BlockSpec((D, D), lambda i: (0, 0)), row, pl.BlockSpec((1, D), lambda i: (0, 0))],
        out_specs=[row, row],
        out_shape=[jax.ShapeDtypeStruct((S, D), F32), jax.ShapeDtypeStruct((S, D), BF16)],
        compiler_params=_params("arbitrary"),
    )(mg, w_out, x, g_ffn)


def _ff1(u2, w1):
    S = u2.shape[0]
    tm, tn = _tile(S, 1024), 1024

    def body(u_ref, w_ref, a_ref, act_ref):
        a = _nn(u_ref[...], w_ref[...])
        a_ref[...] = a.astype(BF16)
        act_ref[...] = jnp.square(jnp.maximum(a, 0.0)).astype(BF16)

    blk = pl.BlockSpec((tm, tn), lambda i, j: (i, j))
    return pl.pallas_call(
        body, name="ff1", grid=(S // tm, D_FF // tn),
        in_specs=[pl.BlockSpec((tm, D), lambda i, j: (i, 0)), pl.BlockSpec((D, tn), lambda i, j: (0, j))],
        out_specs=[blk, blk],
        out_shape=[jax.ShapeDtypeStruct((S, D_FF), BF16), jax.ShapeDtypeStruct((S, D_FF), BF16)],
        compiler_params=_params("arbitrary", "arbitrary"),
    )(u2, w1)


def _ff2_loss(act, w2, h1, g_final, target):
    S = act.shape[0]
    tm, tk = _tile(S, 1024), 1024
    n_k = D_FF // tk

    def body(a_ref, w_ref, h_ref, g_ref, t_ref, d_ref, db_ref, ls_ref, dg_ref, acc):
        i, k = pl.program_id(0), pl.program_id(1)

        @pl.when(k == 0)
        def _():
            acc[...] = jnp.zeros_like(acc)

        acc[...] += _nn(a_ref[...], w_ref[...])

        @pl.when(k == n_k - 1)
        def _():
            @pl.when(i == 0)
            def _():
                ls_ref[...] = jnp.zeros_like(ls_ref)
                dg_ref[...] = jnp.zeros_like(dg_ref)

            gf = g_ref[...]
            for rows in _row_chunks(tm):
                r, xh = _rms(h_ref[rows, :] + acc[rows, :])
                err = xh * gf - t_ref[rows, :]
                dy = err * (1.0 / D)
                dh = _rms_bwd(dy, gf, r, xh)
                d_ref[rows, :] = dh
                db_ref[rows, :] = dh.astype(BF16)
                ls_ref[...] += _sum8(err * err)
                dg_ref[...] += _sum8(dy * xh)

    row = pl.BlockSpec((tm, D), lambda i, k: (i, 0))
    part = pl.BlockSpec((8, D), lambda i, k: (0, 0))
    return pl.pallas_call(
        body, name="ff2_loss", grid=(S // tm, n_k),
        in_specs=[pl.BlockSpec((tm, tk), lambda i, k: (i, k)), pl.BlockSpec((tk, D), lambda i, k: (k, 0)),
                  row, pl.BlockSpec((1, D), lambda i, k: (0, 0)), row],
        out_specs=[row, row, part, part],
        out_shape=[jax.ShapeDtypeStruct((S, D), F32), jax.ShapeDtypeStruct((S, D), BF16),
                   jax.ShapeDtypeStruct((8, D), F32), jax.ShapeDtypeStruct((8, D), F32)],
        scratch_shapes=[pltpu.VMEM((tm, D), F32)],
        compiler_params=_params("arbitrary", "arbitrary"),
    )(act, w2, h1, g_final, target)


def _dact(dh2b, w2, a):
    S = a.shape[0]
    tm, tn = _tile(S, 1024), 1024

    def body(d_ref, w_ref, a_ref, o_ref):
        da = _nt(d_ref[...], w_ref[...])
        o_ref[...] = (da * (2.0 * jnp.maximum(a_ref[...].astype(F32), 0.0))).astype(BF16)

    blk = pl.BlockSpec((tm, tn), lambda i, j: (i, j))
    return pl.pallas_call(
        body, name="dact", grid=(S // tm, D_FF // tn),
        in_specs=[pl.BlockSpec((tm, D), lambda i, j: (i, 0)), pl.BlockSpec((tn, D), lambda i, j: (j, 0)), blk],
        out_specs=blk, out_shape=jax.ShapeDtypeStruct((S, D_FF), BF16),
        compiler_params=_params("arbitrary", "arbitrary"),
    )(dh2b, w2, a)


def _dmerge(dh1b, w_out, pg, y3, wg, wf, wm, slabs):
    S = dh1b.shape[0]
    tm = _tile(S, 512)
    n_i, n_x = S // tm, len(slabs)

    def body(*refs):
        d_ref, w_ref, pg_ref, y_ref, wg_ref, wf_ref, wm_ref = refs[:7]
        outs = refs[7 + n_x:14 + n_x]
        scatter = lambda: _AllToAll(refs[7:7 + n_x], refs[14 + n_x:14 + 2 * n_x], refs[14 + 2 * n_x:], False)
        dy_refs, do_refs, dg_ref = outs[0:3], outs[3:6], outs[6]

        @pl.when(pl.program_id(0) == 0)
        def _():
            scatter().start()

        dm = _nt(d_ref[...], w_ref[...])
        for i, wo_ref in enumerate((wg_ref, wf_ref, wm_ref)):
            gt = jax.nn.sigmoid(pg_ref[:, D * i:D * (i + 1)].astype(F32))
            dy = (dm * gt).astype(BF16)
            dy_refs[i][...] = dy
            do_refs[i][...] = _nt(dy, wo_ref[...])
            dg_ref[:, D * i:D * (i + 1)] = (dm * y_ref[i].astype(F32) * (gt * (1.0 - gt))).astype(BF16)

        @pl.when(pl.program_id(0) == n_i - 1)
        def _():
            scatter().wait()

    row = pl.BlockSpec((tm, D), lambda i: (i, 0))
    half = pl.BlockSpec((tm, 512), lambda i: (i, 0))
    w_spec = pl.BlockSpec((512, D), lambda i: (0, 0))
    any_spec = pl.BlockSpec(memory_space=pl.ANY)
    out = pl.pallas_call(
        body, name="dmerge", grid=(n_i,),
        in_specs=[row, pl.BlockSpec((D, D), lambda i: (0, 0)), pl.BlockSpec((tm, G_W), lambda i: (i, 0)),
                  pl.BlockSpec((3, tm, D), lambda i: (0, i, 0)), w_spec, w_spec, w_spec] + [any_spec] * n_x,
        out_specs=[row, row, row, half, half, half, pl.BlockSpec((tm, G_W), lambda i: (i, P_GT // G_W))] + [any_spec] * n_x,
        out_shape=[jax.ShapeDtypeStruct((S, D), BF16)] * 3 + [jax.ShapeDtypeStruct((S, 512), F32)] * 3
        + [jax.ShapeDtypeStruct((S, P_W), BF16)] + [jax.ShapeDtypeStruct(b.shape, b.dtype) for b in slabs],
        scratch_shapes=_exchange_sems(n_x),
        compiler_params=_params("arbitrary"),
    )(dh1b, w_out, pg, y3, wg, wf, wm, *slabs)
    return (*out[:7], out[7:])


def _gla_block_terms(gq_ref, gk_ref, ps_ref, wau_ref, ba_ref, tb):
    gaff = ps_ref[:, 512:640]
    z = _nn(gaff.astype(BF16), wau_ref[...]) + ba_ref[...]
    la = _log_sigmoid(z) * GLA_TAU_INV
    rr = lax.broadcasted_iota(jnp.int32, (tb, tb), 0)
    cc = lax.broadcasted_iota(jnp.int32, (tb, tb), 1)
    same = jnp.right_shift(rr, 6) == jnp.right_shift(cc, 6)
    tri = jnp.where(same & (cc <= rr), 1.0, 0.0).astype(BF16)
    ones = jnp.where(same, 1.0, 0.0).astype(BF16)
    b = _sum01(tri, la)
    bl = _sum01(ones, la)
    e_pos, e_neg, e_last, dec = jnp.exp(b), jnp.exp(-b), jnp.exp(bl - b), jnp.exp(bl)
    q = gq_ref[...].astype(F32) * GLA_SCALE
    k = gk_ref[...].astype(F32)
    return dict(gaff=gaff, z=z, same=same, rr=rr, cc=cc, ones=ones, e_pos=e_pos, e_neg=e_neg, e_last=e_last, dec=dec,
                qp=q * e_pos, qn=q * e_neg, kn=k * e_neg, kp=k * e_pos, kd=k * e_last)


def _head_masked(x, store):
    lane = lax.broadcasted_iota(jnp.int32, x.shape, 1)
    for h in range(GLA_H):
        store[:, h] = jnp.where(jnp.right_shift(lane, 6) == h, x, 0.0).astype(BF16).reshape(-1, CHUNK, GLA_K)


def _lower4():
    t = jnp.bitwise_and(lax.broadcasted_iota(jnp.int32, (GLA_H * CHUNK, CHUNK), 0), CHUNK - 1)
    return t >= lax.broadcasted_iota(jnp.int32, (GLA_H * CHUNK, CHUNK), 1)


def _stack_heads(ref, rows):
    return jnp.concatenate([ref[rows, GLA_DV * h:GLA_DV * (h + 1)] for h in range(GLA_H)], axis=0)


def _gla_fwd(pa, ps, wau, ba, gh):
    S = pa.shape[0]
    tb = _tile(S, 512)
    n_c = tb // CHUNK
    n_b = S // tb

    def body(gq_ref, gk_ref, gv_ref, ps_ref, wau_ref, ba_ref, gh_ref, o_ref, og_ref, sp_ref,
             qpm, qnm, kdm, kn_s, kp_s, dec_s, state):
        @pl.when(pl.program_id(0) == 0)
        def _():
            state[...] = jnp.zeros_like(state)

        t = _gla_block_terms(gq_ref, gk_ref, ps_ref, wau_ref, ba_ref, tb)
        _head_masked(t["qp"], qpm)
        _head_masked(t["qn"], qnm)
        _head_masked(t["kd"], kdm)
        kn_s[...] = t["kn"].astype(BF16)
        kp_s[...] = t["kp"].astype(BF16)
        dec_s[...] = t["dec"]
        lower = _lower4()

        sp = state[...]
        for c in range(n_c):
            rows = slice(c * CHUNK, (c + 1) * CHUNK)
            sp_ref[c] = sp
            qp, qn, kd = (s[c].reshape(GLA_H * CHUNK, GLA_K) for s in (qpm, qnm, kdm))
            attn = jnp.where(lower, _nt(qp, kn_s[rows, :]), _nt(qn, kp_s[rows, :])).astype(BF16)
            inter = _nt(qp, sp.astype(BF16))
            for h in range(GLA_H):
                mine = slice(CHUNK * h, CHUNK * (h + 1))
                cols = slice(GLA_DV * h, GLA_DV * (h + 1))
                o_ref[rows, cols] = _nn(attn[mine], gv_ref[rows, cols]) + inter[mine]
            sp = sp * dec_s[c * CHUNK:c * CHUNK + 1, :] + _tn(_stack_heads(gv_ref, rows), kd)
        state[...] = sp
        for h in range(GLA_H):
            cols = slice(GLA_DV * h, GLA_DV * (h + 1))
            r, xh = _rms(o_ref[:, cols])
            gg = ps_ref[:, cols]
            og_ref[:, cols] = ((xh * gh_ref[:, cols]) * (gg * jax.nn.sigmoid(gg))).astype(BF16)

    return pl.pallas_call(
        body, name="gla_fwd", grid=(n_b,),
        in_specs=[pl.BlockSpec((tb, GLA_K), lambda i: (i, 0)), pl.BlockSpec((tb, GLA_K), lambda i: (i, 1)),
                  pl.BlockSpec((tb, GLA_V), lambda i: (i, 1)), pl.BlockSpec((tb, S_W), lambda i: (i, 0)),
                  pl.BlockSpec((128, GLA_K), lambda i: (0, 0)), pl.BlockSpec((1, GLA_K), lambda i: (0, 0)),
                  pl.BlockSpec((1, GLA_V), lambda i: (0, 0))],
        out_specs=[pl.BlockSpec((tb, GLA_V), lambda i: (i, 0)), pl.BlockSpec((tb, GLA_V), lambda i: (i, 0)),
                   pl.BlockSpec((n_c, GLA_DV, GLA_K), lambda i: (i, 0, 0))],
        out_shape=[jax.ShapeDtypeStruct((S, GLA_V), F32), jax.ShapeDtypeStruct((S, GLA_V), BF16),
                   jax.ShapeDtypeStruct((S // CHUNK, GLA_DV, GLA_K), F32)],
        scratch_shapes=[pltpu.VMEM((n_c, GLA_H, CHUNK, GLA_K), BF16)] * 3
        + [pltpu.VMEM((tb, GLA_K), BF16), pltpu.VMEM((tb, GLA_K), BF16), pltpu.VMEM((tb, GLA_K), F32),
           pltpu.VMEM((GLA_DV, GLA_K), F32)],
        compiler_params=_params("arbitrary"),
    )(pa, pa, pa, ps, wau, ba, gh)


def _gla_bwd(pa, ps, wau, ba, gh, o_gla, d_og, sprev, dgaff_fox, d_proj, slabs):
    S = pa.shape[0]
    tb = _tile(S, 512)
    n_c = tb // CHUNK
    n_b = S // tb
    n_x = len(slabs)
    c_gk, c_gv, c_gg, c_ga, c_end = GLA_K, 2 * GLA_K, 2 * GLA_K + GLA_V, 2 * GLA_K + 2 * GLA_V, 2 * GLA_K + 2 * GLA_V + 128

    def body(*refs):
        gq_ref, gk_ref, gv_ref, ps_ref, wau_ref, ba_ref, gh_ref, o_ref, dog_ref, sp_ref, dfx_ref = refs[:11]
        dp_ref, dwau_ref, dba_ref, dgh_ref = refs[12 + n_x:16 + n_x]
        (qpm, qnm, kdm, kn_s, kp_s, dec_s, do_s, dqp_s, dqn_s, dkn_s, dkp_s, dkd_s, ddec_s,
         dstate) = refs[16 + 2 * n_x:30 + 2 * n_x]
        scatter = lambda: _AllToAll(refs[12:12 + n_x], refs[16 + n_x:16 + 2 * n_x], refs[30 + 2 * n_x:], False)
        first = pl.program_id(0) == 0
        dp_ref[:, c_end:] = jnp.zeros((tb, P_GLA_W - c_end), BF16)

        @pl.when(first)
        def _():
            dstate[...] = jnp.zeros_like(dstate)
            scatter().start()

        t = _gla_block_terms(gq_ref, gk_ref, ps_ref, wau_ref, ba_ref, tb)
        _head_masked(t["qp"], qpm)
        _head_masked(t["qn"], qnm)
        _head_masked(t["kd"], kdm)
        kn_s[...] = t["kn"].astype(BF16)
        kp_s[...] = t["kp"].astype(BF16)
        dec_s[...] = t["dec"]

        dgh_parts = []
        for h in range(GLA_H):
            cols = slice(GLA_DV * h, GLA_DV * (h + 1))
            r, xh = _rms(o_ref[:, cols])
            g = gh_ref[:, cols]
            gg = ps_ref[:, cols]
            sg = jax.nn.sigmoid(gg)
            d_out = dog_ref[:, cols]
            dp_ref[:, c_gg + GLA_DV * h:c_gg + GLA_DV * (h + 1)] = (d_out * (xh * g) * (sg * (1.0 + gg * (1.0 - sg)))).astype(BF16)
            d_on = d_out * (gg * sg)
            dgh_parts.append(_sum8(d_on * xh))
            do_s[:, cols] = _rms_bwd(d_on, g, r, xh).astype(BF16)
        dgh_part = jnp.concatenate(dgh_parts, axis=1)

        lower = _lower4()
        lane = lax.broadcasted_iota(jnp.int32, (CHUNK, GLA_K), 1)

        def own_columns(stacked):
            return sum(jnp.where(jnp.right_shift(lane, 6) == h, stacked[CHUNK * h:CHUNK * (h + 1)], 0.0) for h in range(GLA_H))

        ds_next = dstate[...]
        for c in reversed(range(n_c)):
            rows = slice(c * CHUNK, (c + 1) * CHUNK)
            dsb = ds_next.astype(BF16)
            sp = sp_ref[c]
            knc, kpc = kn_s[rows, :], kp_s[rows, :]
            qp, qn, kd = (s[c].reshape(GLA_H * CHUNK, GLA_K) for s in (qpm, qnm, kdm))
            v4, do4 = _stack_heads(gv_ref, rows), _stack_heads(do_s, rows)
            ddec_s[rows, :] = jnp.broadcast_to(jnp.sum(ds_next * sp, axis=0, keepdims=True), (CHUNK, GLA_K))
            attn = jnp.where(lower, _nt(qp, knc), _nt(qn, kpc)).astype(BF16)
            da = jnp.concatenate([_nt(do4[CHUNK * h:CHUNK * (h + 1)], v4[CHUNK * h:CHUNK * (h + 1)]) for h in range(GLA_H)],
                                 axis=0)
            dac = jnp.where(lower, da, 0.0).astype(BF16)
            daa = jnp.where(lower, 0.0, da).astype(BF16)
            dqp_s[rows, :] = own_columns(_nn(dac, knc) + _nn(do4, sp.astype(BF16)))
            dqn_s[rows, :] = own_columns(_nn(daa, kpc))
            dkd_s[rows, :] = own_columns(_nn(v4, dsb))
            dkn_s[rows, :] = _tn(dac, qp)
            dkp_s[rows, :] = _tn(daa, qn)
            dv_state = _nt(kd, dsb)
            for h in range(GLA_H):
                mine = slice(CHUNK * h, CHUNK * (h + 1))
                dp_ref[rows, c_gv + GLA_DV * h:c_gv + GLA_DV * (h + 1)] = (_tn(attn[mine], do4[mine]) + dv_state[mine]).astype(BF16)
            ds_next = ds_next * dec_s[c * CHUNK:c * CHUNK + 1, :] + _tn(do4, qp)
        dstate[...] = ds_next

        dqp, dqn, dkn, dkp, dkd = dqp_s[...], dqn_s[...], dkn_s[...], dkp_s[...], dkd_s[...]
        dp_ref[:, 0:c_gk] = ((dqp * t["e_pos"] + dqn * t["e_neg"]) * GLA_SCALE).astype(BF16)
        dp_ref[:, c_gk:c_gv] = (dkn * t["e_neg"] + dkp * t["e_pos"] + dkd * t["e_last"]).astype(BF16)
        kd_term = dkd * t["kd"]
        db = dqp * t["qp"] - dqn * t["qn"] - dkn * t["kn"] + dkp * t["kp"] - kd_term
        upper = jnp.where(t["same"] & (t["cc"] >= t["rr"]), 1.0, 0.0).astype(BF16)
        dla = (_sum01(upper, db) + _sum01(t["ones"], kd_term)
               + ddec_s[...] * t["dec"])
        dz = dla * GLA_TAU_INV * jax.nn.sigmoid(-t["z"])
        dzb = dz.astype(BF16)
        dp_ref[:, c_ga:c_end] = (_nt(dzb, wau_ref[...]) + dfx_ref[...]).astype(BF16)
        dwau_part = _tn(t["gaff"].astype(BF16), dzb)
        dba_part = _sum8(dz)

        @pl.when(first)
        def _():
            dwau_ref[...] = dwau_part
            dba_ref[...] = dba_part
            dgh_ref[...] = dgh_part

        @pl.when(jnp.logical_not(first))
        def _():
            dwau_ref[...] += dwau_part
            dba_ref[...] += dba_part
            dgh_ref[...] += dgh_part

        @pl.when(pl.program_id(0) == n_b - 1)
        def _():
            scatter().wait()

    rev = lambda i: (n_b - 1 - i, 0)
    f32k = pltpu.VMEM((tb, GLA_K), F32)
    bf4 = pltpu.VMEM((n_c, GLA_H, CHUNK, GLA_K), BF16)
    any_spec = pl.BlockSpec(memory_space=pl.ANY)
    out = pl.pallas_call(
        body, name="gla_bwd", grid=(n_b,),
        in_specs=[pl.BlockSpec((tb, GLA_K), rev), pl.BlockSpec((tb, GLA_K), lambda i: (n_b - 1 - i, 1)),
                  pl.BlockSpec((tb, GLA_V), lambda i: (n_b - 1 - i, 1)), pl.BlockSpec((tb, S_W), rev),
                  pl.BlockSpec((128, GLA_K), lambda i: (0, 0)), pl.BlockSpec((1, GLA_K), lambda i: (0, 0)),
                  pl.BlockSpec((1, GLA_V), lambda i: (0, 0)), pl.BlockSpec((tb, GLA_V), rev), pl.BlockSpec((tb, GLA_V), rev),
                  pl.BlockSpec((n_c, GLA_DV, GLA_K), lambda i: (n_b - 1 - i, 0, 0)), pl.BlockSpec((tb, 128), rev),
                  any_spec] + [any_spec] * n_x,
        out_specs=[pl.BlockSpec((tb, P_GLA_W), lambda i: (n_b - 1 - i, P_GLA // P_GLA_W)),
                   pl.BlockSpec((128, GLA_K), lambda i: (0, 0)), pl.BlockSpec((8, GLA_K), lambda i: (0, 0)),
                   pl.BlockSpec((8, GLA_V), lambda i: (0, 0))] + [any_spec] * n_x,
        out_shape=[jax.ShapeDtypeStruct((S, P_W), BF16), jax.ShapeDtypeStruct((128, GLA_K), F32),
                   jax.ShapeDtypeStruct((8, GLA_K), F32), jax.ShapeDtypeStruct((8, GLA_V), F32)]
        + [jax.ShapeDtypeStruct(b.shape, b.dtype) for b in slabs],
        input_output_aliases={11: 0},
        scratch_shapes=[bf4, bf4, bf4, pltpu.VMEM((tb, GLA_K), BF16), pltpu.VMEM((tb, GLA_K), BF16), f32k,
                        pltpu.VMEM((tb, GLA_V), BF16), f32k, f32k, f32k, f32k, f32k, f32k, pltpu.VMEM((GLA_DV, GLA_K), F32)]
        + _exchange_sems(n_x),
        compiler_params=_params("arbitrary"),
    )(pa, pa, pa, ps, wau, ba, gh, o_gla, d_og, sprev, dgaff_fox, d_proj, *slabs)
    return out[0], out[1], out[2], out[3], out[4:]


def _split3(x):
    x1 = x.astype(BF16).astype(F32)
    x2 = (x - x1).astype(BF16).astype(F32)
    x3 = (x - x1 - x2).astype(BF16).astype(F32)
    return x1, x2, x3


def _fox_prep(pa, ps, bfg):
    S = pa.shape[0]
    tm = _tile(S, FOX_TK)

    def body(ps_ref, b_ref, fq_ref, fk_ref, fv_ref, q_ref, k_ref, qt_ref, kt_ref, vt_ref, st_ref, carry):
        @pl.when(pl.program_id(0) == 0)
        def _():
            carry[...] = jnp.zeros_like(carry)

        vt = fv_ref[...].astype(F32).T.astype(BF16)
        ones_row = jnp.where(lax.broadcasted_iota(jnp.int32, (FOX_VT - FOX_DH, tm), 0) == 0, 1.0, 0.0).astype(BF16)
        for h in range(FOX_H):
            vt_ref[FOX_VT * h:FOX_VT * h + FOX_DH, :] = vt[FOX_DH * h:FOX_DH * (h + 1), :]
            vt_ref[FOX_VT * h + FOX_DH:FOX_VT * (h + 1), :] = ones_row
        lf = _log_sigmoid(ps_ref[...] + b_ref[...])
        rr = lax.broadcasted_iota(jnp.int32, (tm, tm), 0)
        cc = lax.broadcasted_iota(jnp.int32, (tm, tm), 1)
        tri = jnp.where(cc <= rr, 1.0, 0.0).astype(F32)
        f = jnp.dot(tri, lf, preferred_element_type=F32, precision=HIGHEST) + carry[0:1, :]
        carry[...] = jnp.broadcast_to(f[tm - 1:tm, :], carry.shape)
        f1, f2, f3 = _split3(f)
        lane = lax.broadcasted_iota(jnp.int32, (tm, 128), 1)
        st_row = lax.broadcasted_iota(jnp.int32, (8, 128), 0)
        st_lane = lax.broadcasted_iota(jnp.int32, (8, 128), 1)
        stats = jnp.zeros((8, 128), F32)
        for h in range(FOX_H):
            cols = slice(128 * h, 128 * (h + 1))
            c = FF_LANE + h
            a1, a2, a3 = f1[:, c:c + 1], f2[:, c:c + 1], f3[:, c:c + 1]
            q = fq_ref[:, cols].astype(F32) * FOX_SCALE
            k = fk_ref[:, cols].astype(F32)
            fh = f[:, c:c + 1]
            vals = (jnp.max(jnp.sum(q * q, axis=-1, keepdims=True)), jnp.max(jnp.sum(k * k, axis=-1, keepdims=True)),
                    jnp.max(fh), jnp.min(fh), jnp.min(jnp.sum(q * k, axis=-1, keepdims=True)))
            for n, val in enumerate(vals):
                stats = jnp.where((st_row == h) & (st_lane == n), val, stats)
            for n, a in enumerate((a1, a2, a3)):
                q = jnp.where(lane == AUG + n, a, q)
                k = jnp.where(lane == AUG + 3 + n, -a, k)
            q = jnp.where((lane >= AUG + 3) & (lane < AUG + 6), 1.0, q)
            k = jnp.where((lane >= AUG) & (lane < AUG + 3), 1.0, k)
            q_ref[:, cols] = q.astype(BF16)
            k_ref[:, cols] = k.astype(BF16)
            qt_ref[cols, :] = q.T.astype(BF16)
            kt_ref[cols, :] = k.T.astype(BF16)
        st_ref[0] = stats

    wide = lambda j: pl.BlockSpec((tm, 1024), lambda i: (i, j))
    tall = lambda n: pl.BlockSpec((n, tm), lambda i: (0, i))
    return pl.pallas_call(
        body, name="fox_prep", grid=(S // tm,),
        in_specs=[pl.BlockSpec((tm, 128), lambda i: (i, 4)), pl.BlockSpec((1, 128), lambda i: (0, 0)), wide(1), wide(2),
                  pl.BlockSpec((tm, FOX_W), lambda i: (i, A_FV // FOX_W))],
        out_specs=[wide(0), wide(0), tall(1024), tall(1024), tall(FOX_H * FOX_VT), pl.BlockSpec((1, 8, 128), lambda i: (i, 0, 0))],
        out_shape=[jax.ShapeDtypeStruct((S, 1024), BF16), jax.ShapeDtypeStruct((S, 1024), BF16),
                   jax.ShapeDtypeStruct((1024, S), BF16), jax.ShapeDtypeStruct((1024, S), BF16),
                   jax.ShapeDtypeStruct((FOX_H * FOX_VT, S), BF16), jax.ShapeDtypeStruct((S // tm, 8, 128), F32)],
        scratch_shapes=[pltpu.VMEM((8, 128), F32)],
        compiler_params=_params("arbitrary"),
    )(ps, bfg, pa, pa, pa)


FOX_PRUNE_AT = -90.0


def _fox_live_ranges(stats, n_sub, ratio):
    n_b = stats.shape[0]
    q2, k2, f_max, f_min, own = (stats[:, :, n].T for n in range(5))
    slack = 0.01 * jnp.sqrt(q2 * k2) + 1e-5 * jnp.abs(f_max) + 1.0
    bound = (1.01 * jnp.sqrt(q2[:, :, None] * k2[:, None, :]) + (f_max + slack - own)[:, :, None]
             - (f_min - 1e-5 * jnp.abs(f_min))[:, None, :])
    blocks = jnp.arange(n_b)
    dead = (bound <= FOX_PRUNE_AT) & (blocks[None, :] < blocks[:, None])[None]
    dead_fwd = dead.reshape(FOX_H, n_b // n_sub, n_sub, n_b).all(axis=2)
    first = jnp.sum(jnp.cumprod(dead_fwd.astype(jnp.int32), axis=2), axis=2)
    last_live = n_b - 1 - jnp.sum(jnp.cumprod(dead[:, ::-1, :].astype(jnp.int32), axis=1), axis=1)
    first_wide = blocks // ratio + 1
    narrow_end = jnp.minimum(jnp.minimum(first_wide * ratio, n_b)[None], last_live + 1)
    wide_end = jnp.where(last_live >= (first_wide * ratio)[None], last_live // ratio + 1, first_wide[None])
    return first.astype(jnp.int32), narrow_end.astype(jnp.int32), wide_end.astype(jnp.int32)


def _fox_fwd(qa, ka, vt, first):
    S = qa.shape[0]
    tq = _tile(S, FOX_TQ)
    tk = _tile(tq, FOX_TK)
    n_sub = tq // tk

    def body(first_ref, q_ref, k_ref, vt_ref, o_ref, lse_ref):
        pair, i = pl.program_id(0), pl.program_id(1)
        both = lambda f: tuple(f(hh) for hh in range(2))

        def blk(j, carry, diag, heads=(0, 1)):
            ks = pl.ds(pl.multiple_of(j * tk, tk), tk)
            q0 = 0 if diag is None else diag * tk

            def head(hh):
                if hh not in heads:
                    return carry[hh]
                m, acc = carry[hh]
                mo, ao = m[:, q0:], acc[:, q0:]
                s = _nt(k_ref[ks, 128 * hh:128 * (hh + 1)], q_ref[q0:, 128 * hh:128 * (hh + 1)])
                if diag is not None:
                    live = lax.broadcasted_iota(jnp.int32, s.shape, 1) >= lax.broadcasted_iota(jnp.int32, s.shape, 0)
                    s = jnp.where(live, s, NEG)
                mn = jnp.maximum(mo, jnp.max(s, axis=0, keepdims=True))
                p = jnp.exp((s - mn).astype(BF16))
                an = jnp.exp(mo - mn) * ao + _nn(vt_ref[FOX_VT * hh:FOX_VT * (hh + 1), ks], p)
                if q0:
                    mn, an = (jnp.concatenate([old[:, :q0], new], axis=1) for old, new in ((m, mn), (acc, an)))
                return mn, an

            return both(head)

        one = (jnp.full((1, tq), NEG, F32), jnp.zeros((FOX_VT, tq), F32))
        past = i * n_sub
        f0, f1 = first_ref[2 * pair, i], first_ref[2 * pair + 1, i]
        join = jnp.maximum(f0, f1)
        solo = lambda hh: lambda c: lax.fori_loop(jnp.minimum(f0, f1), join, lambda j, cc: blk(j, cc, None, (hh,)), c)
        carry = lax.cond(f0 < f1, solo(0), solo(1), (one, one))
        n_both = past - join
        carry = lax.fori_loop(0, n_both // 2, lambda jj, c: blk(join + 2 * jj + 1, blk(join + 2 * jj, c, None), None), carry)
        carry = lax.cond(n_both % 2 == 1, lambda c: blk(past - 1, c, None), lambda c: c, carry)
        for d in range(n_sub):
            carry = blk(past + d, carry, d)
        (m0, a0), (m1, a1) = carry
        l0, l1 = a0[FOX_DH:FOX_DH + 1], a1[FOX_DH:FOX_DH + 1]
        o_ref[...] = jnp.concatenate([a0[:FOX_DH] / l0, a1[:FOX_DH] / l1], axis=0).T
        lse_ref[0, 0:1, :] = m0 + jnp.log(l0)
        lse_ref[0, 1:2, :] = m1 + jnp.log(l1)
        lse_ref[0, 2:8, :] = jnp.zeros((6, tq), F32)

    return pl.pallas_call(
        body, name="fox_fwd", grid=(FOX_H // 2, S // tq),
        in_specs=[pl.BlockSpec(memory_space=pltpu.SMEM), pl.BlockSpec((tq, 256), lambda p, i: (i, p)),
                  pl.BlockSpec((S, 256), lambda p, i: (0, p)), pl.BlockSpec((2 * FOX_VT, S), lambda p, i: (p, 0))],
        out_specs=[pl.BlockSpec((tq, 128), lambda p, i: (i, p)), pl.BlockSpec((1, 8, tq), lambda p, i: (p, 0, i))],
        out_shape=[jax.ShapeDtypeStruct((S, FOX_W), F32), jax.ShapeDtypeStruct((FOX_H // 2, 8, S), F32)],
        compiler_params=_params("arbitrary", "arbitrary"),
    )(first, qa, ka, vt)


def _fox_delta(d_o, o):
    S = o.shape[0]
    tm = _tile(S, 512)

    def body(d_ref, o_ref, db_ref, dbt_ref, dl_ref):
        d = d_ref[...]
        db_ref[...] = d.astype(BF16)
        dbt_ref[...] = d.T.astype(BF16)
        prod = d * o_ref[...]
        rr = lax.broadcasted_iota(jnp.int32, (8, 128), 0)
        cc = lax.broadcasted_iota(jnp.int32, (8, 128), 1)
        ind = jnp.where(jnp.right_shift(cc, 6) == rr, 1.0, 0.0).astype(F32)
        for p in range(FOX_H // 2):
            dl_ref[p] = lax.dot_general(ind, prod[:, 128 * p:128 * (p + 1)], (((1,), (1,)), ((), ())),
                                        preferred_element_type=F32, precision=HIGHEST)

    row = pl.BlockSpec((tm, FOX_W), lambda i: (i, 0))
    return pl.pallas_call(
        body, name="fox_delta", grid=(S // tm,),
        in_specs=[row, row],
        out_specs=[row, pl.BlockSpec((FOX_W, tm), lambda i: (0, i)), pl.BlockSpec((FOX_H // 2, 8, tm), lambda i: (0, 0, i))],
        out_shape=[jax.ShapeDtypeStruct((S, FOX_W), BF16), jax.ShapeDtypeStruct((FOX_W, S), BF16),
                   jax.ShapeDtypeStruct((FOX_H // 2, 8, S), F32)],
        compiler_params=_params("arbitrary"),
    )(d_o, o)


def _fox_bwd(qa, qat, ka, kat, pa, dob, dobt, lse, delta, narrow_end, wide_end):
    S = qa.shape[0]
    tk = _tile(S, FOX_TK)
    wide = _tile(S, FOX_BWD_WIDE)
    ratio = wide // tk
    n_wide = S // wide

    def body(ne_ref, we_ref, q_ref, qt_ref, k_ref, kt_ref, v_ref, do_ref, dot_ref, lse_ref, dl_ref, dq_ref, dk_ref, dv_ref):
        h, jb = pl.program_id(0), pl.program_id(1)
        hh = h % 2

        @pl.when(jb == 0)
        def _():
            dq_ref[...] = jnp.zeros_like(dq_ref)

        lane = lax.broadcasted_iota(jnp.int32, (tk, 128), 1)
        vm = jnp.where(jnp.right_shift(lane, 6) == hh, v_ref[...], jnp.zeros((), BF16))
        kb, ktb = k_ref[...], kt_ref[0:FOX_LIVE, :]
        mine = pl.ds(pl.multiple_of(hh * FOX_DH, FOX_DH), FOX_DH)

        def blk(ib, tq, carry, masked):
            dk, dv = carry
            qs = pl.ds(pl.multiple_of(ib * tq, tq), tq)
            p = jnp.exp(_nt(kb, q_ref[qs, :]) - lse_ref[0, pl.ds(hh, 1), qs])
            if masked:
                live = lax.broadcasted_iota(jnp.int32, p.shape, 1) >= lax.broadcasted_iota(jnp.int32, p.shape, 0)
                p = jnp.where(live, p, 0.0)
            ds = (p * (_nt(vm, do_ref[qs, :]) - dl_ref[0, pl.ds(hh, 1), qs])).astype(BF16)
            dq_ref[0:FOX_LIVE, qs] += _nn(ktb, ds)
            return dk + _nt(qt_ref[0:FOX_LIVE, qs], ds), dv + _nt(dot_ref[mine, qs], p.astype(BF16))

        carry = blk(jb, tk, (jnp.zeros((FOX_LIVE, tk), F32), jnp.zeros((FOX_DH, tk), F32)), True)
        first_wide = jb // ratio + 1
        carry = lax.fori_loop(jb + 1, ne_ref[h, jb], lambda ib, c: blk(ib, tk, c, False), carry)
        last_wide = we_ref[h, jb]
        rest = jnp.maximum(last_wide - first_wide, 0)
        carry = lax.fori_loop(0, rest // 2, lambda t, c: blk(first_wide + 2 * t + 1, wide, blk(first_wide + 2 * t, wide, c, False),
                                                             False), carry)
        dk, dv = lax.cond(rest % 2 == 1, lambda c: blk(last_wide - 1, wide, c, False), lambda c: c, carry)
        dk_ref[0:FOX_LIVE, :] = dk
        dk_ref[FOX_LIVE:, :] = jnp.zeros((128 - FOX_LIVE, tk), F32)
        dv_ref[...] = dv

    once = pl.Buffered(1)
    rows = pl.BlockSpec((1, 8, S), lambda h, j: (h // 2, 0, 0))
    return pl.pallas_call(
        body, name="fox_bwd", grid=(FOX_H, S // tk),
        in_specs=[pl.BlockSpec(memory_space=pltpu.SMEM), pl.BlockSpec(memory_space=pltpu.SMEM),
                  pl.BlockSpec((S, 128), lambda h, j: (0, h)), pl.BlockSpec((128, S), lambda h, j: (h, 0)),
                  pl.BlockSpec((tk, 128), lambda h, j: (j, h)), pl.BlockSpec((128, tk), lambda h, j: (h, j)),
                  pl.BlockSpec((tk, 128), lambda h, j: (j, A_FV // 128 + h // 2)),
                  pl.BlockSpec((S, 128), lambda h, j: (0, h // 2)), pl.BlockSpec((128, S), lambda h, j: (h // 2, 0)),
                  rows, rows],
        out_specs=[pl.BlockSpec((128, S), lambda h, j: (h, 0), pipeline_mode=once),
                   pl.BlockSpec((128, tk), lambda h, j: (h, j)), pl.BlockSpec((FOX_DH, tk), lambda h, j: (h, j))],
        out_shape=[jax.ShapeDtypeStruct((1024, S), F32), jax.ShapeDtypeStruct((1024, S), F32),
                   jax.ShapeDtypeStruct((FOX_W, S), F32)],
        compiler_params=_params("arbitrary", "arbitrary"),
    )(narrow_end, wide_end, qa, qat, ka, kat, pa, dob, dobt, lse, delta)


def _fox_post(dq, dk, dv, ps, bfg, d_proj):
    S = dq.shape[1]
    tm = _tile(S, 512)
    n_b = S // tm

    def body(dq_ref, dk_ref, dv_ref, ps_ref, b_ref, _, dp_ref, dff_ref, dbf_ref, carry):
        first = pl.program_id(0) == 0

        @pl.when(first)
        def _():
            carry[...] = jnp.zeros_like(carry)

        low = lax.broadcasted_iota(jnp.int32, (tm, 128), 1) < FOX_DH
        for h in range(FOX_H):
            blk = slice(128 * h, 128 * (h + 1))
            dp_ref[:, blk] = jnp.where(low, dq_ref[blk, :].T * FOX_SCALE, 0.0).astype(BF16)
            dp_ref[:, 1024 + 128 * h:1024 + 128 * (h + 1)] = jnp.where(low, dk_ref[blk, :].T, 0.0).astype(BF16)
        dp_ref[:, 2048:P_FOX_W] = dv_ref[...].T.astype(BF16)
        rr = lax.broadcasted_iota(jnp.int32, (FOX_H, 1024), 0)
        cc = lax.broadcasted_iota(jnp.int32, (FOX_H, 1024), 1)
        sel_k = jnp.where(cc == 128 * rr + AUG + 3, 1.0, 0.0).astype(F32)
        sel_q = jnp.where(cc == 128 * rr + AUG, 1.0, 0.0).astype(F32)
        g = (jnp.dot(sel_k, dk_ref[...], preferred_element_type=F32, precision=HIGHEST)
             - jnp.dot(sel_q, dq_ref[...], preferred_element_type=F32, precision=HIGHEST))
        t_from = lax.broadcasted_iota(jnp.int32, (tm, tm), 0)
        t_to = lax.broadcasted_iota(jnp.int32, (tm, tm), 1)
        later = jnp.where(t_from >= t_to, 1.0, 0.0).astype(F32)
        dlf = jnp.dot(-g, later, preferred_element_type=F32, precision=HIGHEST) + carry[:, 0:1]
        carry[...] = jnp.broadcast_to(dlf[:, 0:1], carry.shape)
        cols = jnp.concatenate([jnp.zeros((FF_LANE, tm), F32), dlf, jnp.zeros((128 - FF_LANE - FOX_H, tm), F32)], axis=0).T
        dff = cols * jax.nn.sigmoid(-(ps_ref[...] + b_ref[...]))
        dff_ref[...] = dff
        part = _sum8(dff)

        @pl.when(first)
        def _():
            dbf_ref[...] = part

        @pl.when(jnp.logical_not(first))
        def _():
            dbf_ref[...] += part

    rev = lambda i: (n_b - 1 - i, 0)
    tall = lambda n: pl.BlockSpec((n, tm), lambda i: (0, n_b - 1 - i))
    return pl.pallas_call(
        body, name="fox_post", grid=(n_b,),
        in_specs=[tall(1024), tall(1024), tall(FOX_W), pl.BlockSpec((tm, 128), lambda i: (n_b - 1 - i, 4)),
                  pl.BlockSpec((1, 128), lambda i: (0, 0)), pl.BlockSpec(memory_space=pl.ANY)],
        out_specs=[pl.BlockSpec((tm, P_FOX_W), lambda i: (n_b - 1 - i, P_FOX // P_FOX_W)), pl.BlockSpec((tm, 128), rev),
                   pl.BlockSpec((8, 128), lambda i: (0, 0))],
        out_shape=[jax.ShapeDtypeStruct((S, P_W), BF16), jax.ShapeDtypeStruct((S, 128), F32),
                   jax.ShapeDtypeStruct((8, 128), F32)],
        input_output_aliases={5: 0},
        scratch_shapes=[pltpu.VMEM((8, 128), F32)],
        compiler_params=_params("arbitrary"),
    )(dq, dk, dv, ps, bfg, d_proj)


def _mem_prep(mem, g_mem, wkv):
    def body(m_ref, g_ref, w_ref, mn_ref, kv_ref):
        r, xh = _rms(m_ref[...])
        mn = (xh * g_ref[...]).astype(BF16)
        mn_ref[...] = mn
        kv_ref[...] = _nn(mn, w_ref[...]).astype(BF16)

    return pl.pallas_call(
        body, name="mem_prep",
        out_shape=[jax.ShapeDtypeStruct((N_MEM, D), BF16), jax.ShapeDtypeStruct((N_MEM, 2 * MEM_W), BF16)],
        compiler_params=pltpu.CompilerParams(vmem_limit_bytes=V7X_VMEM_LIMIT),
    )(mem, g_mem, wkv)


def _mem_softmax(qh, kh):
    s = _nt(qh, kh) * MEM_SCALE
    e = jnp.exp(s - jnp.max(s, axis=-1, keepdims=True))
    return e / jnp.sum(e, axis=-1, keepdims=True)


def _mem_fwd(pa, mkv):
    S = pa.shape[0]
    tm = _tile(S, 1024)

    def body(q_ref, kv_ref, o_ref):
        for h in range(MEM_H):
            cols = slice(MEM_DH * h, MEM_DH * (h + 1))
            p = _mem_softmax(q_ref[:, cols], kv_ref[:, cols])
            o_ref[:, cols] = _nn(p.astype(BF16), kv_ref[:, MEM_W + MEM_DH * h:MEM_W + MEM_DH * (h + 1)])

    return pl.pallas_call(
        body, name="mem_fwd", grid=(S // tm,),
        in_specs=[pl.BlockSpec((tm, MEM_W), lambda i: (i, A_MQ // MEM_W)), pl.BlockSpec((N_MEM, 2 * MEM_W), lambda i: (0, 0))],
        out_specs=pl.BlockSpec((tm, MEM_W), lambda i: (i, 0)),
        out_shape=jax.ShapeDtypeStruct((S, MEM_W), F32),
        compiler_params=_params("arbitrary"),
    )(pa, mkv)


def _mem_bwd(pa, mkv, d_o, d_proj):
    S = pa.shape[0]
    tm = _tile(S, 1024)

    def body(q_ref, kv_ref, do_ref, _, dq_ref, dkv_ref):
        first = pl.program_id(0) == 0
        parts = []
        for h in range(MEM_H):
            cols = slice(MEM_DH * h, MEM_DH * (h + 1))
            vcols = slice(MEM_W + MEM_DH * h, MEM_W + MEM_DH * (h + 1))
            qh, kh = q_ref[:, cols], kv_ref[:, cols]
            p = _mem_softmax(qh, kh)
            dob = do_ref[:, cols].astype(BF16)
            dp = _nt(dob, kv_ref[:, vcols])
            ds = (p * (dp - jnp.sum(p * dp, axis=-1, keepdims=True)) * MEM_SCALE).astype(BF16)
            dq_ref[:, cols] = _nn(ds, kh).astype(BF16)
            parts.append((cols, _tn(ds, qh)))
            parts.append((vcols, _tn(p.astype(BF16), dob)))

        @pl.when(first)
        def _():
            for sl, v in parts:
                dkv_ref[:, sl] = v

        @pl.when(jnp.logical_not(first))
        def _():
            for sl, v in parts:
                dkv_ref[:, sl] += v

    return pl.pallas_call(
        body, name="mem_bwd", grid=(S // tm,),
        in_specs=[pl.BlockSpec((tm, MEM_W), lambda i: (i, A_MQ // MEM_W)), pl.BlockSpec((N_MEM, 2 * MEM_W), lambda i: (0, 0)),
                  pl.BlockSpec((tm, MEM_W), lambda i: (i, 0)), pl.BlockSpec(memory_space=pl.ANY)],
        out_specs=[pl.BlockSpec((tm, MEM_W), lambda i: (i, P_MQ // MEM_W)), pl.BlockSpec((N_MEM, 2 * MEM_W), lambda i: (0, 0))],
        out_shape=[jax.ShapeDtypeStruct((S, P_W), BF16), jax.ShapeDtypeStruct((N_MEM, 2 * MEM_W), F32)],
        input_output_aliases={3: 0},
        compiler_params=_params("arbitrary"),
    )(pa, mkv, d_o, d_proj)


def _mem_prep_bwd(mem, g_mem, mn, wkv, dkv):
    def body(m_ref, g_ref, mn_ref, w_ref, d_ref, dw_ref, dg_ref):
        db = d_ref[...].astype(BF16)
        dw_ref[...] = _tn(mn_ref[...], db).astype(BF16)
        r, xh = _rms(m_ref[...])
        dg_ref[...] = _sum8(_nt(db, w_ref[...]) * xh)

    dw, dg = pl.pallas_call(
        body, name="mem_prep_bwd",
        out_shape=[jax.ShapeDtypeStruct((D, 2 * MEM_W), BF16), jax.ShapeDtypeStruct((8, D), F32)],
        compiler_params=pltpu.CompilerParams(vmem_limit_bytes=V7X_VMEM_LIMIT),
    )(mem, g_mem, mn, wkv, dkv)
    return dw.reshape(N_DEV, D // N_DEV, 2 * MEM_W), dg


def _rearrange_w_in(w):
    def heads128(cols):
        blk = w[:, cols:cols + FOX_W].reshape(D, FOX_H, FOX_DH)
        return jnp.pad(blk, ((0, 0), (0, 0), (0, 128 - FOX_DH))).reshape(D, FOX_H * 128)

    fq, fk, fv, mq, wg = heads128(O_FQ), heads128(O_FK), w[:, O_FV:O_FF], w[:, O_MQ:O_GT], w[:, O_GT:]
    gaff = jnp.concatenate([w[:, O_GA:O_FQ], w[:, O_FF:O_MQ], jnp.zeros((D, 128 - GLA_R - FOX_H), w.dtype)], axis=1)
    wa = jnp.concatenate([w[:, O_GQ:O_GG], fq, fk, fv, mq], axis=1)
    ws = jnp.concatenate([w[:, O_GG:O_GA], gaff], axis=1)
    wp = jnp.concatenate([fq, fk, fv, mq, wg, w[:, O_GQ:O_GG], ws, jnp.zeros((D, P_W - P_GLA - 1024 - S_W), w.dtype)], axis=1)
    return wa, wg, ws, wp


def _restore_w_in_grad(dwp):
    def unheads(off):
        return dwp[:, off:off + FOX_H * 128].reshape(D, FOX_H, 128)[:, :, :FOX_DH].reshape(D, FOX_W)

    g0 = P_GLA + 1024
    return jnp.concatenate([
        dwp[:, P_GLA:g0], dwp[:, g0:g0 + 512], dwp[:, g0 + 512:g0 + 512 + GLA_R], unheads(P_FOX), unheads(P_FOX + 1024),
        dwp[:, P_FOX + 2048:P_FOX + P_FOX_W], dwp[:, g0 + 512 + GLA_R:g0 + 512 + GLA_R + FOX_H], dwp[:, P_MQ:P_GT],
        dwp[:, P_GT:P_GLA]], axis=1)


def _local_step(x, mem, target, p, late_shards):
    S = x.shape[0]
    p = dict(p)
    wa, wg, ws, wp = _rearrange_w_in(p["w_in"])
    wau = jnp.pad(p["w_alpha_up"], ((0, 128 - GLA_R), (0, 0)))
    bfg = jnp.pad(p["b_forget"], ((0, 0), (FF_LANE, 128 - FF_LANE - FOX_H)))
    gh = p["g_gla_head"].reshape(1, GLA_V)

    pa, pg, ps, u, gathered = _proj(x, p["g_mix"], wa, wg, ws, late_shards)
    p.update({n: _unslab(t, ax) for (n, ax), t in zip(BIG[1:], gathered)})
    o_gla, og, sprev = _gla_fwd(pa, ps, wau, p["b_alpha"], gh)
    qa, ka, qat, kat, vt, fox_stats = _fox_prep(pa, ps, bfg)
    fox_tk = _tile(S, FOX_TK)
    fox_first, fox_narrow_end, fox_wide_end = _fox_live_ranges(fox_stats, _tile(S, FOX_TQ) // fox_tk,
                                                               _tile(S, FOX_BWD_WIDE) // fox_tk)
    o_fox, lse = _fox_fwd(qa, ka, vt, fox_first)
    mn, mkv = _mem_prep(mem, p["g_mem"], p["w_mem_kv"])
    o_mem = _mem_fwd(pa, mkv)
    y3, mg = _merge(og, o_fox, o_mem, p["w_gla_o"], p["w_fox_o"], p["w_mem_o"], pg)
    h1, u2 = _out_proj(mg, p["w_out"], x, p["g_ffn"])
    a, act = _ff1(u2, p["w_ff1"])
    dh2, dh2b, loss8, dg_final = _ff2_loss(act, p["w_ff2"], h1, p["g_final"].reshape(1, D), target)

    d_a = _dact(dh2b, p["w_ff2"], a)
    dw_ff2 = _wgrad(act, dh2b, "wgrad_ff2", 0)
    dh1, dh1b, dg_ffn = _nt_rmsbwd(d_a, p["w_ff1"], h1, p["g_ffn"], dh2, "dffn", True)
    dw_ff1 = _wgrad(u2, d_a, "wgrad_ff1", 1)
    dy_g, dy_f, dy_m, do_g, do_f, do_m, d_proj, arrived_ff = _dmerge(dh1b, p["w_out"], pg, y3, p["w_gla_o"], p["w_fox_o"],
                                                                     p["w_mem_o"], [dw_ff1, dw_ff2])
    dw_out = _wgrad(mg, dh1b, "wgrad_out", 0)
    dw_gla_o = _wgrad(og, dy_g, "wgrad_gla_o", 1)
    dw_fox_o = _wgrad(o_fox, dy_f, "wgrad_fox_o", 1)
    dw_mem_o = _wgrad(o_mem, dy_m, "wgrad_mem_o", 1)
    d_proj, d_mkv = _mem_bwd(pa, mkv, do_m, d_proj)
    dw_mem_kv, dg_mem = _mem_prep_bwd(mem, p["g_mem"], mn, p["w_mem_kv"], d_mkv)
    dob, dobt, delta = _fox_delta(do_f, o_fox)
    dq, dk, dv = _fox_bwd(qa, qat, ka, kat, pa, dob, dobt, lse, delta, fox_narrow_end, fox_wide_end)
    d_proj, dgaff_fox, db_forget = _fox_post(dq, dk, dv, ps, bfg, d_proj)
    d_proj, dw_au, db_alpha, dg_gla, arrived = _gla_bwd(pa, ps, wau, p["b_alpha"], gh, o_gla, do_g, sprev, dgaff_fox, d_proj,
                                                        [dw_mem_kv, dw_gla_o, dw_fox_o, dw_mem_o, dw_out])
    dw_in = _slabs(_restore_w_in_grad(_wgrad(u, d_proj, "wgrad_in")), 1).astype(BF16)
    dx, dg_mix, arrived_in = _nt_rmsbwd(d_proj, wp, x, p["g_mix"], dh1, "dmix", False, [dw_in])

    big = dict(w_in=arrived_in[0], w_ff1=arrived_ff[0], w_ff2=arrived_ff[1],
               **dict(zip(("w_mem_kv", "w_gla_o", "w_fox_o", "w_mem_o", "w_out"), arrived)))
    small = dict(g_mix=dg_mix, g_mem=dg_mem, g_ffn=dg_ffn, g_final=dg_final, b_alpha=db_alpha, g_gla_head=dg_gla,
                 b_forget=db_forget, w_alpha_up=dw_au, loss=loss8)
    return dx, big, small


BIG = (("w_in", 1), ("w_mem_kv", 0), ("w_gla_o", 1), ("w_fox_o", 1), ("w_mem_o", 1), ("w_out", 0), ("w_ff1", 1), ("w_ff2", 0))


def _peer(d):
    me = lax.axis_index("x") * 4 + lax.axis_index("y") * 2 + lax.axis_index("c")
    t = (me + d) % N_DEV
    return (t // 4, (t // 2) % 2, t % 2), me


def _exchange_sems(n):
    return [pltpu.SemaphoreType.DMA((n, N_DEV - 1)), pltpu.SemaphoreType.DMA((n, N_DEV - 1)), pltpu.SemaphoreType.DMA((n,))]


def _exchange_call(body, blocks, out_shape, name):
    n = len(blocks)
    any_spec = pl.BlockSpec(memory_space=pl.ANY)
    return pl.pallas_call(body, name=name, in_specs=[any_spec] * n, out_specs=[any_spec] * n, out_shape=out_shape,
                          scratch_shapes=_exchange_sems(n))(*blocks)


class _AllToAll:
    def __init__(self, ins, outs, sems, gather):
        send, recv, loc = sems
        n = len(ins)
        _, me = _peer(0)
        src = (lambda k, j: ins[k]) if gather else (lambda k, j: ins[k].at[j])
        self.local = [pltpu.make_async_copy(src(k, me), outs[k].at[me], loc.at[k]) for k in range(n)]
        self.remote = []
        for d in range(1, N_DEV):
            to, _ = _peer(d)
            self.remote += [pltpu.make_async_remote_copy(
                src_ref=src(k, (me + d) % N_DEV), dst_ref=outs[k].at[me], send_sem=send.at[k, d - 1],
                recv_sem=recv.at[k, d - 1], device_id=to, device_id_type=MESH) for k in range(n)]

    def start(self):
        for cp in self.local + self.remote:
            cp.start()

    def wait(self):
        for cp in self.remote:
            cp.wait_send()
        for cp in self.remote:
            cp.wait_recv()
        for cp in self.local:
            cp.wait()


def _gathered_shapes(shards):
    return [jax.ShapeDtypeStruct((N_DEV,) + b.shape, b.dtype) for b in shards]


def _gather_weights(shards):
    n = len(shards)

    def body(*refs):
        ins, outs = refs[:n], refs[n:2 * n]
        send, recv, loc = refs[2 * n:]
        x, y, c = lax.axis_index("x"), lax.axis_index("y"), lax.axis_index("c")
        sibling = (x, y, 1 - c)
        chips = [(1 - x, y), (x, 1 - y), (1 - x, 1 - y)]
        slot = lambda px, py, pc: px * 4 + py * 2 + pc

        def copy(k, s, block, to, src=None):
            rows = outs[k].at[slot(*block)]
            return pltpu.make_async_remote_copy(src_ref=rows if src is None else src, dst_ref=rows, send_sem=send.at[k, s],
                                                recv_sem=recv.at[k, s], device_id=to, device_id_type=MESH)

        me = (x, y, c)
        own = [pltpu.make_async_copy(ins[k], outs[k].at[slot(*me)], loc.at[k]) for k in range(n)]
        first = [copy(k, 0, me, sibling, src=ins[k]) for k in range(n)]
        first += [copy(k, 1 + j, me, (*chip, c), src=ins[k]) for j, chip in enumerate(chips) for k in range(n)]
        for cp in own + first:
            cp.start()
        passed = []
        for j, chip in enumerate(chips):
            for k in range(n):
                copy(k, 1 + j, (*chip, c), me).wait_recv()
                fwd = copy(k, 4 + j, (*chip, c), sibling)
                fwd.start()
                passed.append(fwd)
        for k in range(n):
            copy(k, 0, sibling, me).wait_recv()
        for j, chip in enumerate(chips):
            for k in range(n):
                copy(k, 4 + j, (*chip, 1 - c), me).wait_recv()
        for cp in first + passed:
            cp.wait_send()
        for cp in own:
            cp.wait()

    return _exchange_call(body, shards, [jax.ShapeDtypeStruct((N_DEV,) + b.shape, b.dtype) for b in shards], "gather_weights")


def _adamw_math(g, w, m, v):
    m2 = ADAM_B1 * m + (1.0 - ADAM_B1) * g
    v2 = ADAM_B2 * v + (1.0 - ADAM_B2) * jnp.square(g)
    m_hat = m2 / (1.0 - ADAM_B1 ** ADAM_STEP)
    v_hat = v2 / (1.0 - ADAM_B2 ** ADAM_STEP)
    delta = -ADAM_LR * (m_hat / (jnp.sqrt(v_hat) + ADAM_EPS) + ADAM_WD * w)
    return delta, m2, v2


def _adamw_sum(parts, w, m, v, name):
    R, C = w.shape
    tr = _tile(R, 128)

    def body(p_ref, w_ref, m_ref, v_ref, g_ref, d_ref, m2_ref, v2_ref):
        g = p_ref[0].astype(F32)
        for j in range(1, p_ref.shape[0]):
            g = g + p_ref[j].astype(F32)
        g_ref[...] = g
        d_ref[...], m2_ref[...], v2_ref[...] = _adamw_math(g, w_ref[...], m_ref[...], v_ref[...])

    blk = pl.BlockSpec((tr, C), lambda i: (i, 0))
    return pl.pallas_call(
        body, name=name, grid=(R // tr,),
        in_specs=[pl.BlockSpec((parts.shape[0], tr, C), lambda i: (0, i, 0)), blk, blk, blk],
        out_specs=[blk] * 4, out_shape=[jax.ShapeDtypeStruct((R, C), F32)] * 4,
        compiler_params=_params("arbitrary"),
    )(parts, w, m, v)


SMALL_ROWS = 24


def _pack_small(d):
    mixed = jnp.concatenate([d["b_alpha"].reshape(1, GLA_K), d["g_gla_head"].reshape(1, GLA_V),
                             jnp.pad(d["b_forget"].reshape(1, FOX_H), ((0, 0), (FF_LANE, 128 - FF_LANE - FOX_H))),
                             jnp.zeros((1, 128), F32)], axis=1)
    rows = [d["g_mix"].reshape(1, D), d["g_mem"].reshape(1, D), d["g_ffn"].reshape(1, D), d["g_final"].reshape(1, D), mixed,
            jnp.zeros((3, D), F32), jnp.pad(d["w_alpha_up"].reshape(GLA_R, GLA_K), ((0, 0), (0, D - GLA_K)))]
    return jnp.concatenate(rows, axis=0)


def _unpack_small(t):
    return dict(g_mix=t[0:1], g_mem=t[1:2], g_ffn=t[2:3], g_final=t[3], b_alpha=t[4:5, 0:GLA_K],
                g_gla_head=t[4:5, GLA_K:GLA_K + GLA_V].reshape(1, GLA_H, GLA_DV),
                b_forget=t[4:5, 768 + FF_LANE:768 + FF_LANE + FOX_H], w_alpha_up=t[8:24, 0:GLA_K].reshape(1, GLA_R, GLA_K))


def _small_allreduce(small, w, m, v):
    def body(gm, gme, gf, gfi, ba, gg, bf, wau, ls, w_ref, m_ref, v_ref, g_ref, d_ref, m2_ref, v2_ref, l_ref,
             buf, send, recv):
        _, me = _peer(0)
        buf[me] = jnp.zeros((SMALL_ROWS, D), F32)
        for r, ref in enumerate((gm, gme, gf, gfi)):
            buf[me, r:r + 1, :] = jnp.sum(ref[...], axis=0, keepdims=True)
        buf[me, 4:5, 0:GLA_K] = jnp.sum(ba[...], axis=0, keepdims=True)
        buf[me, 4:5, GLA_K:GLA_K + GLA_V] = jnp.sum(gg[...], axis=0, keepdims=True)
        buf[me, 4:5, 768:896] = jnp.sum(bf[...], axis=0, keepdims=True)
        lrow = jnp.sum(ls[...], axis=0, keepdims=True)
        lsum = lrow[:, 0:128]
        for c in range(1, D // 128):
            lsum = lsum + lrow[:, 128 * c:128 * (c + 1)]
        buf[me, 4:5, 896:1024] = lsum
        buf[me, 8:24, 0:GLA_K] = wau[0:GLA_R, :]
        remote = []
        for d in range(1, N_DEV):
            to, me = _peer(d)
            cp = pltpu.make_async_remote_copy(src_ref=buf.at[me], dst_ref=buf.at[me], send_sem=send.at[d - 1],
                                              recv_sem=recv.at[d - 1], device_id=to, device_id_type=MESH)
            cp.start()
            remote.append(cp)
        for cp in remote:
            cp.wait_send()
        for cp in remote:
            cp.wait_recv()
        g = buf[0]
        for j in range(1, N_DEV):
            g = g + buf[j]
        g_ref[...] = g
        d_ref[...], m2_ref[...], v2_ref[...] = _adamw_math(g, w_ref[...], m_ref[...], v_ref[...])
        l_ref[...] = g[4:5, 896:1024]

    packed = jax.ShapeDtypeStruct((SMALL_ROWS, D), F32)
    return pl.pallas_call(
        body, name="small_allreduce",
        out_shape=[packed, packed, packed, packed, jax.ShapeDtypeStruct((1, 128), F32)],
        scratch_shapes=[pltpu.VMEM((N_DEV, SMALL_ROWS, D), F32), pltpu.SemaphoreType.DMA((N_DEV - 1,)),
                        pltpu.SemaphoreType.DMA((N_DEV - 1,))],
    )(small["g_mix"], small["g_mem"], small["g_ffn"], small["g_final"], small["b_alpha"], small["g_gla_head"],
      small["b_forget"], small["w_alpha_up"], small["loss"], w, m, v)


def _slabs(g, axis):
    R, C = g.shape
    if axis == 0:
        return g.reshape(N_DEV, R // N_DEV, C)
    return g.reshape(R, N_DEV, C // N_DEV).transpose(1, 0, 2)


def _unslab(t, axis):
    n, r, c = t.shape
    if axis == 0:
        return t.reshape(n * r, c)
    return t.transpose(1, 0, 2).reshape(r, n * c)


def kernel(x, mem, g_mix, w_in, w_alpha_up, b_alpha, b_forget, g_gla_head, g_mem, w_mem_kv, w_gla_o, w_fox_o, w_mem_o, w_out, g_ffn, w_ff1, w_ff2, g_final, loss_target, m_g_mix, m_w_in, m_w_alpha_up, m_b_alpha, m_b_forget, m_g_gla_head, m_g_mem, m_w_mem_kv, m_w_gla_o, m_w_fox_o, m_w_mem_o, m_w_out, m_g_ffn, m_w_ff1, m_w_ff2, m_g_final, v_g_mix, v_w_in, v_w_alpha_up, v_b_alpha, v_b_forget, v_g_gla_head, v_g_mem, v_w_mem_kv, v_w_gla_o, v_w_fox_o, v_w_mem_o, v_w_out, v_g_ffn, v_w_ff1, v_w_ff2, v_g_final):
    names = ["g_mix", "w_in", "w_alpha_up", "b_alpha", "b_forget", "g_gla_head", "g_mem", "w_mem_kv", "w_gla_o", "w_fox_o",
             "w_mem_o", "w_out", "g_ffn", "w_ff1", "w_ff2", "g_final"]
    w = dict(g_mix=g_mix, w_in=w_in, w_alpha_up=w_alpha_up, b_alpha=b_alpha, b_forget=b_forget, g_gla_head=g_gla_head,
             g_mem=g_mem, w_mem_kv=w_mem_kv, w_gla_o=w_gla_o, w_fox_o=w_fox_o, w_mem_o=w_mem_o, w_out=w_out, g_ffn=g_ffn,
             w_ff1=w_ff1, w_ff2=w_ff2, g_final=g_final)
    m = dict(g_mix=m_g_mix, w_in=m_w_in, w_alpha_up=m_w_alpha_up, b_alpha=m_b_alpha, b_forget=m_b_forget,
             g_gla_head=m_g_gla_head, g_mem=m_g_mem, w_mem_kv=m_w_mem_kv, w_gla_o=m_w_gla_o, w_fox_o=m_w_fox_o,
             w_mem_o=m_w_mem_o, w_out=m_w_out, g_ffn=m_g_ffn, w_ff1=m_w_ff1, w_ff2=m_w_ff2, g_final=m_g_final)
    v = dict(g_mix=v_g_mix, w_in=v_w_in, w_alpha_up=v_w_alpha_up, b_alpha=v_b_alpha, b_forget=v_b_forget,
             g_gla_head=v_g_gla_head, g_mem=v_g_mem, w_mem_kv=v_w_mem_kv, w_gla_o=v_w_gla_o, w_fox_o=v_w_fox_o,
             w_mem_o=v_w_mem_o, w_out=v_w_out, g_ffn=v_g_ffn, w_ff1=v_w_ff1, w_ff2=v_w_ff2, g_final=v_g_final)
    me = lax.axis_index("x") * 4 + lax.axis_index("y") * 2 + lax.axis_index("c")

    shard = lambda n: w[n][0].astype(BF16)
    w_in_all, w_au_all = _gather_weights([shard("w_in"), shard("w_alpha_up")])
    p = dict(w_in=_unslab(w_in_all, 1), w_alpha_up=_unslab(w_au_all, 1), g_mix=g_mix, b_alpha=b_alpha, b_forget=b_forget,
             g_gla_head=g_gla_head, g_mem=g_mem, g_ffn=g_ffn, g_final=g_final)

    dx, big, small = _local_step(x[0], mem[0], loss_target[0], p, [shard(n) for n, _ in BIG[1:]])

    out_g, out_d, out_m, out_v = {}, {}, {}, {}
    for n, _ in BIG:
        g_, d_, m_, v_ = _adamw_sum(big[n], w[n][0], m[n][0], v[n][0], "adamw_" + n)
        out_g[n], out_d[n], out_m[n], out_v[n] = g_[None], d_[None], m_[None], v_[None]

    full = lambda d: dict(d, w_alpha_up=jnp.zeros((1, GLA_R, GLA_K), F32))
    gs, ds, ms, vs, lrow = _small_allreduce(small, _pack_small(full(w)), _pack_small(full(m)), _pack_small(full(v)))
    g_s, d_s, m_s, v_s = _unpack_small(gs), _unpack_small(ds), _unpack_small(ms), _unpack_small(vs)
    for n in names:
        if n not in out_g and n != "w_alpha_up":
            out_g[n], out_d[n], out_m[n], out_v[n] = g_s[n], d_s[n], m_s[n], v_s[n]
    g_au = lax.dynamic_slice_in_dim(g_s["w_alpha_up"][0], me * (GLA_K // N_DEV), GLA_K // N_DEV, axis=1)
    g_, d_, m_, v_ = _adamw_sum(g_au[None], w_alpha_up[0], m_w_alpha_up[0], v_w_alpha_up[0], "adamw_w_alpha_up")
    out_g["w_alpha_up"], out_d["w_alpha_up"], out_m["w_alpha_up"], out_v["w_alpha_up"] = g_[None], d_[None], m_[None], v_[None]

    loss = jnp.sum(lrow) * (0.5 / D)
    return (loss, dx[None], *[out_g[n] for n in names], *[out_d[n] for n in names], *[out_m[n] for n in names],
            *[out_v[n] for n in names])
```

```python
import jax
import jax.numpy as jnp
from jax import lax
from jax.experimental import pallas as pl
from jax.experimental.pallas import tpu as pltpu

F32, BF16 = jnp.float32, jnp.bfloat16
HIGHEST = lax.Precision.HIGHEST
MESH = pl.DeviceIdType.MESH

N_DEV = 8
D = 1024
EPS = 1e-6
CHUNK = 64
N_MEM = 256
GLA_H, GLA_DK, GLA_DV = 4, 64, 128
GLA_K, GLA_V, GLA_R = 256, 512, 16
FOX_H, FOX_DH, FOX_W = 8, 64, 512
MEM_H, MEM_DH, MEM_W = 4, 128, 512
D_FF = 4096
D_IN = 6680
FOX_SCALE = 0.125
GLA_SCALE = 0.125
MEM_SCALE = MEM_DH ** -0.5
GLA_TAU_INV = 1.0 / 16.0
NEG = -1e30

O_GQ, O_GK, O_GV, O_GG, O_GA, O_FQ, O_FK, O_FV, O_FF, O_MQ, O_GT = 0, 256, 512, 1024, 1536, 1552, 2064, 2576, 3088, 3096, 3608
A_FQ, A_FK, A_FV, A_MQ, A_W = 1024, 2048, 3072, 3584, 4096
S_W = 640
G_W = 3072
P_FOX, P_FOX_W, P_MQ, P_GT, P_GLA, P_GLA_W, P_W = 0, 2560, 2560, 3072, 6144, 2048, 8192
FF_LANE = 16
AUG = 64
FOX_LIVE = 80
FOX_VT = 80

ADAM_LR, ADAM_B1, ADAM_B2, ADAM_EPS, ADAM_WD, ADAM_STEP = 0.001, 0.9, 0.999, 1e-08, 0.01, 10
V7X_VMEM_LIMIT = 54 * 1024 * 1024
FOX_TK = 512
FOX_TQ = 2048
FOX_BWD_WIDE = 1024


def _params(*sem):
    return pltpu.CompilerParams(dimension_semantics=sem, vmem_limit_bytes=V7X_VMEM_LIMIT)


def _nt(a, b):
    return lax.dot_general(a, b, (((1,), (1,)), ((), ())), preferred_element_type=F32)


def _tn(a, b):
    return lax.dot_general(a, b, (((0,), (0,)), ((), ())), preferred_element_type=F32)


def _nn(a, b):
    return jnp.dot(a, b, preferred_element_type=F32)


def _log_sigmoid(z):
    return jnp.minimum(z, 0.0) - jnp.log(1.0 + jnp.exp(-jnp.abs(z)))


def _sum01(m01, x):
    x1 = x.astype(BF16)
    x2 = (x - x1.astype(F32)).astype(BF16)
    x3 = (x - x1.astype(F32) - x2.astype(F32)).astype(BF16)
    return _nn(m01, x1) + _nn(m01, x2) + _nn(m01, x3)


def _sum8(x):
    return x.reshape(x.shape[0] // 8, 8, x.shape[1]).sum(axis=0)


def _rms(xv):
    r = lax.rsqrt(jnp.mean(xv * xv, axis=-1, keepdims=True) + EPS)
    return r, xv * r


def _rms_bwd(du, g, r, xh):
    w = du * g
    return r * (w - xh * jnp.mean(w * xh, axis=-1, keepdims=True))


def _row_chunks(n, size=256):
    return [slice(r, r + min(size, n)) for r in range(0, n, min(size, n))]


def _tile(n, pref):
    t = min(n, pref)
    assert n % t == 0, (n, t)
    return t


def _proj(x, g, wa, wg, ws, shards):
    S = x.shape[0]
    tm, tn = _tile(S, 1024), 1024
    n_a, n_g = A_W // tn, G_W // tn
    n_i, n_j = S // tm, n_a + n_g + 1
    n_x = len(shards)

    def body(*refs):
        x_ref, g_ref, wa_ref, wg_ref, ws_ref = refs[:5]
        pa_ref, pg_ref, ps_ref, u_ref = refs[5 + n_x:9 + n_x]
        u_s = refs[9 + 2 * n_x]
        gather = lambda: _AllToAll(refs[5:5 + n_x], refs[9 + n_x:9 + 2 * n_x], refs[10 + 2 * n_x:], True)
        i, j = pl.program_id(0), pl.program_id(1)

        @pl.when((i == 0) & (j == 0))
        def _():
            gather().start()

        @pl.when(j == 0)
        def _():
            r, xh = _rms(x_ref[...])
            u_s[...] = (xh * g_ref[...]).astype(BF16)
            u_ref[...] = u_s[...]

        @pl.when(j < n_a)
        def _():
            pa_ref[...] = _nn(u_s[...], wa_ref[...]).astype(BF16)

        @pl.when((j >= n_a) & (j < n_a + n_g))
        def _():
            pg_ref[...] = _nn(u_s[...], wg_ref[...]).astype(BF16)

        @pl.when(j == n_a + n_g)
        def _():
            ps_ref[...] = _nn(u_s[...], ws_ref[...])

        @pl.when((i == n_i - 1) & (j == n_j - 1))
        def _():
            gather().wait()

    in_a = lambda j: jnp.minimum(j, n_a - 1)
    in_g = lambda j: jnp.clip(j - n_a, 0, n_g - 1)
    row = pl.BlockSpec((tm, D), lambda i, j: (i, 0))
    any_spec = pl.BlockSpec(memory_space=pl.ANY)
    out = pl.pallas_call(
        body, name="proj", grid=(n_i, n_j),
        in_specs=[row, pl.BlockSpec((1, D), lambda i, j: (0, 0)), pl.BlockSpec((D, tn), lambda i, j: (0, in_a(j))),
                  pl.BlockSpec((D, tn), lambda i, j: (0, in_g(j))),
                  pl.BlockSpec((D, S_W), lambda i, j: (0, 0), pipeline_mode=pl.Buffered(1))] + [any_spec] * n_x,
        out_specs=[pl.BlockSpec((tm, tn), lambda i, j: (i, in_a(j))), pl.BlockSpec((tm, tn), lambda i, j: (i, in_g(j))),
                   pl.BlockSpec((tm, S_W), lambda i, j: (i, 0)), row] + [any_spec] * n_x,
        out_shape=[jax.ShapeDtypeStruct((S, A_W), BF16), jax.ShapeDtypeStruct((S, G_W), BF16),
                   jax.ShapeDtypeStruct((S, S_W), F32), jax.ShapeDtypeStruct((S, D), BF16)] + _gathered_shapes(shards),
        scratch_shapes=[pltpu.VMEM((tm, D), BF16)] + _exchange_sems(n_x),
        compiler_params=_params("arbitrary", "arbitrary"),
    )(x, g, wa, wg, ws, *shards)
    return out[0], out[1], out[2], out[3], out[4:]


def _wgrad(a, b, name, slab_axis=None):
    S, Ka = a.shape
    N = b.shape[1]
    tka, tn, ts = _tile(Ka, 1024), _tile(N, 1024), _tile(S, 4096)
    n_s = S // ts
    per = N // N_DEV
    slabs_per_step = tn // per

    def body(a_ref, b_ref, o_ref, acc):
        s = pl.program_id(2)

        @pl.when(s == 0)
        def _():
            acc[...] = jnp.zeros_like(acc)

        acc[...] += _tn(a_ref[...].astype(BF16), b_ref[...].astype(BF16))

        @pl.when(s == n_s - 1)
        def _():
            if slab_axis == 1:
                for q in range(slabs_per_step):
                    o_ref[q] = acc[:, per * q:per * (q + 1)].astype(BF16)
            else:
                o_ref[...] = acc[...].astype(o_ref.dtype)

    if slab_axis == 1:
        out_spec = pl.BlockSpec((slabs_per_step, tka, per), lambda i, j, s: (j, i, 0))
        out_shape = jax.ShapeDtypeStruct((N_DEV, Ka, per), BF16)
    else:
        out_spec = pl.BlockSpec((tka, tn), lambda i, j, s: (i, j))
        out_shape = jax.ShapeDtypeStruct((Ka, N), F32 if slab_axis is None else BF16)
    out = pl.pallas_call(
        body, name=name, grid=(Ka // tka, N // tn, n_s),
        in_specs=[pl.BlockSpec((ts, tka), lambda i, j, s: (s, i)), pl.BlockSpec((ts, tn), lambda i, j, s: (s, j))],
        out_specs=out_spec, out_shape=out_shape,
        scratch_shapes=[pltpu.VMEM((tka, tn), F32)],
        compiler_params=_params("arbitrary", "arbitrary", "arbitrary"),
    )(a, b)
    return out.reshape(N_DEV, Ka // N_DEV, N) if slab_axis == 0 else out


def _nt_rmsbwd(a, w, xin, g, dres, name, emit_bf16, slabs=()):
    S, K = a.shape
    tm, tk = _tile(S, 1024), _tile(K, 1024 if emit_bf16 else 2048)
    n_i, n_k = S // tm, K // tk
    n_x, n_o = len(slabs), 3 if emit_bf16 else 2

    def body(*refs):
        a_ref, w_ref, x_ref, g_ref, r_ref = refs[:5]
        o_ref = refs[5 + n_x]
        rest = refs[6 + n_x:5 + n_x + n_o] + (refs[5 + 2 * n_x + n_o],)
        dg_ref, acc = rest[-2], rest[-1]
        scatter = lambda: _AllToAll(refs[5:5 + n_x], refs[5 + n_x + n_o:5 + 2 * n_x + n_o], refs[6 + 2 * n_x + n_o:], False)
        i, k = pl.program_id(0), pl.program_id(1)

        if n_x:
            @pl.when((i == 0) & (k == 0))
            def _():
                scatter().start()

        @pl.when(k == 0)
        def _():
            acc[...] = jnp.zeros_like(acc)

        acc[...] += _nt(a_ref[...], w_ref[...])

        @pl.when(k == n_k - 1)
        def _():
            @pl.when(i == 0)
            def _():
                dg_ref[...] = jnp.zeros_like(dg_ref)

            for rows in _row_chunks(tm):
                du = acc[rows, :]
                r, xh = _rms(x_ref[rows, :])
                out = r_ref[rows, :] + _rms_bwd(du, g_ref[...], r, xh)
                o_ref[rows, :] = out
                if emit_bf16:
                    rest[0][rows, :] = out.astype(BF16)
                dg_ref[...] += _sum8(du * xh)

        if n_x:
            @pl.when((i == n_i - 1) & (k == n_k - 1))
            def _():
                scatter().wait()

    row = pl.BlockSpec((tm, D), lambda i, k: (i, 0))
    any_spec = pl.BlockSpec(memory_space=pl.ANY)
    out_shape = [jax.ShapeDtypeStruct((S, D), F32)]
    out_specs = [row]
    if emit_bf16:
        out_shape.append(jax.ShapeDtypeStruct((S, D), BF16))
        out_specs.append(row)
    out_shape.append(jax.ShapeDtypeStruct((8, D), F32))
    out_specs.append(pl.BlockSpec((8, D), lambda i, k: (0, 0)))
    out = pl.pallas_call(
        body, name=name, grid=(n_i, n_k),
        in_specs=[pl.BlockSpec((tm, tk), lambda i, k: (i, k)), pl.BlockSpec((D, tk), lambda i, k: (0, k)),
                  row, pl.BlockSpec((1, D), lambda i, k: (0, 0)), row] + [any_spec] * n_x,
        out_specs=out_specs + [any_spec] * n_x,
        out_shape=out_shape + [jax.ShapeDtypeStruct(b.shape, b.dtype) for b in slabs],
        scratch_shapes=[pltpu.VMEM((tm, D), F32)] + (_exchange_sems(n_x) if n_x else []),
        compiler_params=_params("arbitrary", "arbitrary"),
    )(a, w, xin, g, dres, *slabs)
    return (*out[:n_o], out[n_o:]) if n_x else out


def _merge(og, ofox, omem, wg, wf, wm, pg):
    S = og.shape[0]
    tm = _tile(S, 512)

    def body(og_ref, of_ref, om_ref, wg_ref, wf_ref, wm_ref, pg_ref, y_ref, mg_ref):
        tot = None
        for i, (o_ref, w_ref) in enumerate(((og_ref, wg_ref), (of_ref, wf_ref), (om_ref, wm_ref))):
            y = _nn(o_ref[...].astype(BF16), w_ref[...])
            y_ref[i] = y.astype(BF16)
            t = jax.nn.sigmoid(pg_ref[:, D * i:D * (i + 1)].astype(F32)) * y
            tot = t if tot is None else tot + t
        mg_ref[...] = tot.astype(BF16)

    o_spec = pl.BlockSpec((tm, 512), lambda i: (i, 0))
    w_spec = pl.BlockSpec((512, D), lambda i: (0, 0))
    return pl.pallas_call(
        body, name="merge", grid=(S // tm,),
        in_specs=[o_spec, o_spec, o_spec, w_spec, w_spec, w_spec, pl.BlockSpec((tm, G_W), lambda i: (i, 0))],
        out_specs=[pl.BlockSpec((3, tm, D), lambda i: (0, i, 0)), pl.BlockSpec((tm, D), lambda i: (i, 0))],
        out_shape=[jax.ShapeDtypeStruct((3, S, D), BF16), jax.ShapeDtypeStruct((S, D), BF16)],
        compiler_params=_params("arbitrary"),
    )(og, ofox, omem, wg, wf, wm, pg)


def _out_proj(mg, w_out, x, g_ffn):
    S = x.shape[0]
    tm = _tile(S, 1024)

    def body(mg_ref, w_ref, x_ref, g_ref, h_ref, u_ref):
        h = x_ref[...] + _nn(mg_ref[...], w_ref[...])
        h_ref[...] = h
        r, xh = _rms(h)
        u_ref[...] = (xh * g_ref[...]).astype(BF16)

    row = pl.BlockSpec((tm, D), lambda i: (i, 0))
    return pl.pallas_call(
        body, name="out_proj", grid=(S // tm,),
        in_specs=[row, pl.BlockSpec((D, D), lambda i: (0, 0)), row, pl.BlockSpec((1, D), lambda i: (0, 0))],
        out_specs=[row, row],
        out_shape=[jax.ShapeDtypeStruct((S, D), F32), jax.ShapeDtypeStruct((S, D), BF16)],
        compiler_params=_params("arbitrary"),
    )(mg, w_out, x, g_ffn)


def _ff1(u2, w1):
    S = u2.shape[0]
    tm, tn = _tile(S, 2048), 1024

    def body(u_ref, w_ref, a_ref, act_ref):
        a = _nn(u_ref[...], w_ref[...])
        a_ref[...] = a.astype(BF16)
        act_ref[...] = jnp.square(jnp.maximum(a, 0.0)).astype(BF16)

    blk = pl.BlockSpec((tm, tn), lambda i, j: (i, j))
    return pl.pallas_call(
        body, name="ff1", grid=(S // tm, D_FF // tn),
        in_specs=[pl.BlockSpec((tm, D), lambda i, j: (i, 0)), pl.BlockSpec((D, tn), lambda i, j: (0, j))],
        out_specs=[blk, blk],
        out_shape=[jax.ShapeDtypeStruct((S, D_FF), BF16), jax.ShapeDtypeStruct((S, D_FF), BF16)],
        compiler_params=_params("arbitrary", "arbitrary"),
    )(u2, w1)


def _ff2_loss(act, w2, h1, g_final, target):
    S = act.shape[0]
    tm, tk = _tile(S, 1024), 1024
    n_k = D_FF // tk

    def body(a_ref, w_ref, h_ref, g_ref, t_ref, d_ref, db_ref, ls_ref, dg_ref, acc):
        i, k = pl.program_id(0), pl.program_id(1)

        @pl.when(k == 0)
        def _():
            acc[...] = jnp.zeros_like(acc)

        acc[...] += _nn(a_ref[...], w_ref[...])

        @pl.when(k == n_k - 1)
        def _():
            @pl.when(i == 0)
            def _():
                ls_ref[...] = jnp.zeros_like(ls_ref)
                dg_ref[...] = jnp.zeros_like(dg_ref)

            gf = g_ref[...]
            for rows in _row_chunks(tm):
                r, xh = _rms(h_ref[rows, :] + acc[rows, :])
                err = xh * gf - t_ref[rows, :]
                dy = err * (1.0 / D)
                dh = _rms_bwd(dy, gf, r, xh)
                d_ref[rows, :] = dh
                db_ref[rows, :] = dh.astype(BF16)
                ls_ref[...] += _sum8(err * err)
                dg_ref[...] += _sum8(dy * xh)

    row = pl.BlockSpec((tm, D), lambda i, k: (i, 0))
    part = pl.BlockSpec((8, D), lambda i, k: (0, 0))
    return pl.pallas_call(
        body, name="ff2_loss", grid=(S // tm, n_k),
        in_specs=[pl.BlockSpec((tm, tk), lambda i, k: (i, k)), pl.BlockSpec((tk, D), lambda i, k: (k, 0)),
                  row, pl.BlockSpec((1, D), lambda i, k: (0, 0)), row],
        out_specs=[row, row, part, part],
        out_shape=[jax.ShapeDtypeStruct((S, D), F32), jax.ShapeDtypeStruct((S, D), BF16),
                   jax.ShapeDtypeStruct((8, D), F32), jax.ShapeDtypeStruct((8, D), F32)],
        scratch_shapes=[pltpu.VMEM((tm, D), F32)],
        compiler_params=_params("arbitrary", "arbitrary"),
    )(act, w2, h1, g_final, target)


def _dact(dh2b, w2, a):
    S = a.shape[0]
    tm, tn = _tile(S, 2048), 1024

    def body(d_ref, w_ref, a_ref, o_ref):
        da = _nt(d_ref[...], w_ref[...])
        o_ref[...] = (da * (2.0 * jnp.maximum(a_ref[...].astype(F32), 0.0))).astype(BF16)

    blk = pl.BlockSpec((tm, tn), lambda i, j: (i, j))
    return pl.pallas_call(
        body, name="dact", grid=(S // tm, D_FF // tn),
        in_specs=[pl.BlockSpec((tm, D), lambda i, j: (i, 0)), pl.BlockSpec((tn, D), lambda i, j: (j, 0)), blk],
        out_specs=blk, out_shape=jax.ShapeDtypeStruct((S, D_FF), BF16),
        compiler_params=_params("arbitrary", "arbitrary"),
    )(dh2b, w2, a)


def _dmerge(dh1b, w_out, pg, y3, wg, wf, wm, slabs):
    S = dh1b.shape[0]
    tm = _tile(S, 512)
    n_i, n_x = S // tm, len(slabs)

    def body(*refs):
        d_ref, w_ref, pg_ref, y_ref, wg_ref, wf_ref, wm_ref = refs[:7]
        outs = refs[7 + n_x:14 + n_x]
        scatter = lambda: _AllToAll(refs[7:7 + n_x], refs[14 + n_x:14 + 2 * n_x], refs[14 + 2 * n_x:], False)
        dy_refs, do_refs, dg_ref = outs[0:3], outs[3:6], outs[6]

        @pl.when(pl.program_id(0) == 0)
        def _():
            scatter().start()

        dm = _nt(d_ref[...], w_ref[...])
        for i, wo_ref in enumerate((wg_ref, wf_ref, wm_ref)):
            gt = jax.nn.sigmoid(pg_ref[:, D * i:D * (i + 1)].astype(F32))
            dy = (dm * gt).astype(BF16)
            dy_refs[i][...] = dy
            do_refs[i][...] = _nt(dy, wo_ref[...])
            dg_ref[:, D * i:D * (i + 1)] = (dm * y_ref[i].astype(F32) * (gt * (1.0 - gt))).astype(BF16)

        @pl.when(pl.program_id(0) == n_i - 1)
        def _():
            scatter().wait()

    row = pl.BlockSpec((tm, D), lambda i: (i, 0))
    half = pl.BlockSpec((tm, 512), lambda i: (i, 0))
    w_spec = pl.BlockSpec((512, D), lambda i: (0, 0))
    any_spec = pl.BlockSpec(memory_space=pl.ANY)
    out = pl.pallas_call(
        body, name="dmerge", grid=(n_i,),
        in_specs=[row, pl.BlockSpec((D, D), lambda i: (0, 0)), pl.BlockSpec((tm, G_W), lambda i: (i, 0)),
                  pl.BlockSpec((3, tm, D), lambda i: (0, i, 0)), w_spec, w_spec, w_spec] + [any_spec] * n_x,
        out_specs=[row, row, row, half, half, half, pl.BlockSpec((tm, G_W), lambda i: (i, P_GT // G_W))] + [any_spec] * n_x,
        out_shape=[jax.ShapeDtypeStruct((S, D), BF16)] * 3 + [jax.ShapeDtypeStruct((S, 512), F32)] * 3
        + [jax.ShapeDtypeStruct((S, P_W), BF16)] + [jax.ShapeDtypeStruct(b.shape, b.dtype) for b in slabs],
        scratch_shapes=_exchange_sems(n_x),
        compiler_params=_params("arbitrary"),
    )(dh1b, w_out, pg, y3, wg, wf, wm, *slabs)
    return (*out[:7], out[7:])


def _gla_block_terms(gq_ref, gk_ref, ps_ref, wau_ref, ba_ref, tb):
    gaff = ps_ref[:, 512:640]
    z = _nn(gaff.astype(BF16), wau_ref[...]) + ba_ref[...]
    la = _log_sigmoid(z) * GLA_TAU_INV
    rr = lax.broadcasted_iota(jnp.int32, (tb, tb), 0)
    cc = lax.broadcasted_iota(jnp.int32, (tb, tb), 1)
    same = jnp.right_shift(rr, 6) == jnp.right_shift(cc, 6)
    tri = jnp.where(same & (cc <= rr), 1.0, 0.0).astype(BF16)
    ones = jnp.where(same, 1.0, 0.0).astype(BF16)
    b = _sum01(tri, la)
    bl = _sum01(ones, la)
    e_pos, e_neg, e_last, dec = jnp.exp(b), jnp.exp(-b), jnp.exp(bl - b), jnp.exp(bl)
    q = gq_ref[...].astype(F32) * GLA_SCALE
    k = gk_ref[...].astype(F32)
    return dict(gaff=gaff, z=z, same=same, rr=rr, cc=cc, ones=ones, e_pos=e_pos, e_neg=e_neg, e_last=e_last, dec=dec,
                qp=q * e_pos, qn=q * e_neg, kn=k * e_neg, kp=k * e_pos, kd=k * e_last)


def _head_masked(x, store):
    lane = lax.broadcasted_iota(jnp.int32, x.shape, 1)
    for h in range(GLA_H):
        store[:, h] = jnp.where(jnp.right_shift(lane, 6) == h, x, 0.0).astype(BF16).reshape(-1, CHUNK, GLA_K)


def _lower4():
    t = jnp.bitwise_and(lax.broadcasted_iota(jnp.int32, (GLA_H * CHUNK, CHUNK), 0), CHUNK - 1)
    return t >= lax.broadcasted_iota(jnp.int32, (GLA_H * CHUNK, CHUNK), 1)


def _stack_heads(ref, rows):
    return jnp.concatenate([ref[rows, GLA_DV * h:GLA_DV * (h + 1)] for h in range(GLA_H)], axis=0)


def _gla_fwd(pa, ps, wau, ba, gh):
    S = pa.shape[0]
    tb = _tile(S, 512)
    n_c = tb // CHUNK
    n_b = S // tb

    def body(gq_ref, gk_ref, gv_ref, ps_ref, wau_ref, ba_ref, gh_ref, o_ref, og_ref, sp_ref,
             qpm, qnm, kdm, kn_s, kp_s, dec_s, state):
        @pl.when(pl.program_id(0) == 0)
        def _():
            state[...] = jnp.zeros_like(state)

        t = _gla_block_terms(gq_ref, gk_ref, ps_ref, wau_ref, ba_ref, tb)
        _head_masked(t["qp"], qpm)
        _head_masked(t["qn"], qnm)
        _head_masked(t["kd"], kdm)
        kn_s[...] = t["kn"].astype(BF16)
        kp_s[...] = t["kp"].astype(BF16)
        dec_s[...] = t["dec"]
        lower = _lower4()

        sp = state[...]
        for c in range(n_c):
            rows = slice(c * CHUNK, (c + 1) * CHUNK)
            sp_ref[c] = sp
            qp, qn, kd = (s[c].reshape(GLA_H * CHUNK, GLA_K) for s in (qpm, qnm, kdm))
            attn = jnp.where(lower, _nt(qp, kn_s[rows, :]), _nt(qn, kp_s[rows, :])).astype(BF16)
            inter = _nt(qp, sp.astype(BF16))
            for h in range(GLA_H):
                mine = slice(CHUNK * h, CHUNK * (h + 1))
                cols = slice(GLA_DV * h, GLA_DV * (h + 1))
                o_ref[rows, cols] = _nn(attn[mine], gv_ref[rows, cols]) + inter[mine]
            sp = sp * dec_s[c * CHUNK:c * CHUNK + 1, :] + _tn(_stack_heads(gv_ref, rows), kd)
        state[...] = sp
        for h in range(GLA_H):
            cols = slice(GLA_DV * h, GLA_DV * (h + 1))
            r, xh = _rms(o_ref[:, cols])
            gg = ps_ref[:, cols]
            og_ref[:, cols] = ((xh * gh_ref[:, cols]) * (gg * jax.nn.sigmoid(gg))).astype(BF16)

    return pl.pallas_call(
        body, name="gla_fwd", grid=(n_b,),
        in_specs=[pl.BlockSpec((tb, GLA_K), lambda i: (i, 0)), pl.BlockSpec((tb, GLA_K), lambda i: (i, 1)),
                  pl.BlockSpec((tb, GLA_V), lambda i: (i, 1)), pl.BlockSpec((tb, S_W), lambda i: (i, 0)),
                  pl.BlockSpec((128, GLA_K), lambda i: (0, 0)), pl.BlockSpec((1, GLA_K), lambda i: (0, 0)),
                  pl.BlockSpec((1, GLA_V), lambda i: (0, 0))],
        out_specs=[pl.BlockSpec((tb, GLA_V), lambda i: (i, 0)), pl.BlockSpec((tb, GLA_V), lambda i: (i, 0)),
                   pl.BlockSpec((n_c, GLA_DV, GLA_K), lambda i: (i, 0, 0))],
        out_shape=[jax.ShapeDtypeStruct((S, GLA_V), F32), jax.ShapeDtypeStruct((S, GLA_V), BF16),
                   jax.ShapeDtypeStruct((S // CHUNK, GLA_DV, GLA_K), F32)],
        scratch_shapes=[pltpu.VMEM((n_c, GLA_H, CHUNK, GLA_K), BF16)] * 3
        + [pltpu.VMEM((tb, GLA_K), BF16), pltpu.VMEM((tb, GLA_K), BF16), pltpu.VMEM((tb, GLA_K), F32),
           pltpu.VMEM((GLA_DV, GLA_K), F32)],
        compiler_params=_params("arbitrary"),
    )(pa, pa, pa, ps, wau, ba, gh)


def _gla_bwd(pa, ps, wau, ba, gh, o_gla, d_og, sprev, dgaff_fox, d_proj, slabs):
    S = pa.shape[0]
    tb = _tile(S, 512)
    n_c = tb // CHUNK
    n_b = S // tb
    n_x = len(slabs)
    c_gk, c_gv, c_gg, c_ga, c_end = GLA_K, 2 * GLA_K, 2 * GLA_K + GLA_V, 2 * GLA_K + 2 * GLA_V, 2 * GLA_K + 2 * GLA_V + 128

    def body(*refs):
        gq_ref, gk_ref, gv_ref, ps_ref, wau_ref, ba_ref, gh_ref, o_ref, dog_ref, sp_ref, dfx_ref = refs[:11]
        dp_ref, dwau_ref, dba_ref, dgh_ref = refs[12 + n_x:16 + n_x]
        (qpm, qnm, kdm, kn_s, kp_s, dec_s, do_s, dqp_s, dqn_s, dkn_s, dkp_s, dkd_s, ddec_s,
         dstate) = refs[16 + 2 * n_x:30 + 2 * n_x]
        scatter = lambda: _AllToAll(refs[12:12 + n_x], refs[16 + n_x:16 + 2 * n_x], refs[30 + 2 * n_x:], False)
        first = pl.program_id(0) == 0
        dp_ref[:, c_end:] = jnp.zeros((tb, P_GLA_W - c_end), BF16)

        @pl.when(first)
        def _():
            dstate[...] = jnp.zeros_like(dstate)
            scatter().start()

        t = _gla_block_terms(gq_ref, gk_ref, ps_ref, wau_ref, ba_ref, tb)
        _head_masked(t["qp"], qpm)
        _head_masked(t["qn"], qnm)
        _head_masked(t["kd"], kdm)
        kn_s[...] = t["kn"].astype(BF16)
        kp_s[...] = t["kp"].astype(BF16)
        dec_s[...] = t["dec"]

        dgh_parts = []
        for h in range(GLA_H):
            cols = slice(GLA_DV * h, GLA_DV * (h + 1))
            r, xh = _rms(o_ref[:, cols])
            g = gh_ref[:, cols]
            gg = ps_ref[:, cols]
            sg = jax.nn.sigmoid(gg)
            d_out = dog_ref[:, cols]
            dp_ref[:, c_gg + GLA_DV * h:c_gg + GLA_DV * (h + 1)] = (d_out * (xh * g) * (sg * (1.0 + gg * (1.0 - sg)))).astype(BF16)
            d_on = d_out * (gg * sg)
            dgh_parts.append(_sum8(d_on * xh))
            do_s[:, cols] = _rms_bwd(d_on, g, r, xh).astype(BF16)
        dgh_part = jnp.concatenate(dgh_parts, axis=1)

        lower = _lower4()
        lane = lax.broadcasted_iota(jnp.int32, (CHUNK, GLA_K), 1)

        def own_columns(stacked):
            return sum(jnp.where(jnp.right_shift(lane, 6) == h, stacked[CHUNK * h:CHUNK * (h + 1)], 0.0) for h in range(GLA_H))

        ds_next = dstate[...]
        for c in reversed(range(n_c)):
            rows = slice(c * CHUNK, (c + 1) * CHUNK)
            dsb = ds_next.astype(BF16)
            sp = sp_ref[c]
            knc, kpc = kn_s[rows, :], kp_s[rows, :]
            qp, qn, kd = (s[c].reshape(GLA_H * CHUNK, GLA_K) for s in (qpm, qnm, kdm))
            v4, do4 = _stack_heads(gv_ref, rows), _stack_heads(do_s, rows)
            ddec_s[rows, :] = jnp.broadcast_to(jnp.sum(ds_next * sp, axis=0, keepdims=True), (CHUNK, GLA_K))
            attn = jnp.where(lower, _nt(qp, knc), _nt(qn, kpc)).astype(BF16)
            da = jnp.concatenate([_nt(do4[CHUNK * h:CHUNK * (h + 1)], v4[CHUNK * h:CHUNK * (h + 1)]) for h in range(GLA_H)],
                                 axis=0)
            dac = jnp.where(lower, da, 0.0).astype(BF16)
            daa = jnp.where(lower, 0.0, da).astype(BF16)
            dqp_s[rows, :] = own_columns(_nn(dac, knc) + _nn(do4, sp.astype(BF16)))
            dqn_s[rows, :] = own_columns(_nn(daa, kpc))
            dkd_s[rows, :] = own_columns(_nn(v4, dsb))
            dkn_s[rows, :] = _tn(dac, qp)
            dkp_s[rows, :] = _tn(daa, qn)
            dv_state = _nt(kd, dsb)
            for h in range(GLA_H):
                mine = slice(CHUNK * h, CHUNK * (h + 1))
                dp_ref[rows, c_gv + GLA_DV * h:c_gv + GLA_DV * (h + 1)] = (_tn(attn[mine], do4[mine]) + dv_state[mine]).astype(BF16)
            ds_next = ds_next * dec_s[c * CHUNK:c * CHUNK + 1, :] + _tn(do4, qp)
        dstate[...] = ds_next

        dqp, dqn, dkn, dkp, dkd = dqp_s[...], dqn_s[...], dkn_s[...], dkp_s[...], dkd_s[...]
        dp_ref[:, 0:c_gk] = ((dqp * t["e_pos"] + dqn * t["e_neg"]) * GLA_SCALE).astype(BF16)
        dp_ref[:, c_gk:c_gv] = (dkn * t["e_neg"] + dkp * t["e_pos"] + dkd * t["e_last"]).astype(BF16)
        kd_term = dkd * t["kd"]
        db = dqp * t["qp"] - dqn * t["qn"] - dkn * t["kn"] + dkp * t["kp"] - kd_term
        upper = jnp.where(t["same"] & (t["cc"] >= t["rr"]), 1.0, 0.0).astype(BF16)
        dla = (_sum01(upper, db) + _sum01(t["ones"], kd_term)
               + ddec_s[...] * t["dec"])
        dz = dla * GLA_TAU_INV * jax.nn.sigmoid(-t["z"])
        dzb = dz.astype(BF16)
        dp_ref[:, c_ga:c_end] = (_nt(dzb, wau_ref[...]) + dfx_ref[...]).astype(BF16)
        dwau_part = _tn(t["gaff"].astype(BF16), dzb)
        dba_part = _sum8(dz)

        @pl.when(first)
        def _():
            dwau_ref[...] = dwau_part
            dba_ref[...] = dba_part
            dgh_ref[...] = dgh_part

        @pl.when(jnp.logical_not(first))
        def _():
            dwau_ref[...] += dwau_part
            dba_ref[...] += dba_part
            dgh_ref[...] += dgh_part

        @pl.when(pl.program_id(0) == n_b - 1)
        def _():
            scatter().wait()

    rev = lambda i: (n_b - 1 - i, 0)
    f32k = pltpu.VMEM((tb, GLA_K), F32)
    bf4 = pltpu.VMEM((n_c, GLA_H, CHUNK, GLA_K), BF16)
    any_spec = pl.BlockSpec(memory_space=pl.ANY)
    out = pl.pallas_call(
        body, name="gla_bwd", grid=(n_b,),
        in_specs=[pl.BlockSpec((tb, GLA_K), rev), pl.BlockSpec((tb, GLA_K), lambda i: (n_b - 1 - i, 1)),
                  pl.BlockSpec((tb, GLA_V), lambda i: (n_b - 1 - i, 1)), pl.BlockSpec((tb, S_W), rev),
                  pl.BlockSpec((128, GLA_K), lambda i: (0, 0)), pl.BlockSpec((1, GLA_K), lambda i: (0, 0)),
                  pl.BlockSpec((1, GLA_V), lambda i: (0, 0)), pl.BlockSpec((tb, GLA_V), rev), pl.BlockSpec((tb, GLA_V), rev),
                  pl.BlockSpec((n_c, GLA_DV, GLA_K), lambda i: (n_b - 1 - i, 0, 0)), pl.BlockSpec((tb, 128), rev),
                  any_spec] + [any_spec] * n_x,
        out_specs=[pl.BlockSpec((tb, P_GLA_W), lambda i: (n_b - 1 - i, P_GLA // P_GLA_W)),
                   pl.BlockSpec((128, GLA_K), lambda i: (0, 0)), pl.BlockSpec((8, GLA_K), lambda i: (0, 0)),
                   pl.BlockSpec((8, GLA_V), lambda i: (0, 0))] + [any_spec] * n_x,
        out_shape=[jax.ShapeDtypeStruct((S, P_W), BF16), jax.ShapeDtypeStruct((128, GLA_K), F32),
                   jax.ShapeDtypeStruct((8, GLA_K), F32), jax.ShapeDtypeStruct((8, GLA_V), F32)]
        + [jax.ShapeDtypeStruct(b.shape, b.dtype) for b in slabs],
        input_output_aliases={11: 0},
        scratch_shapes=[bf4, bf4, bf4, pltpu.VMEM((tb, GLA_K), BF16), pltpu.VMEM((tb, GLA_K), BF16), f32k,
                        pltpu.VMEM((tb, GLA_V), BF16), f32k, f32k, f32k, f32k, f32k, f32k, pltpu.VMEM((GLA_DV, GLA_K), F32)]
        + _exchange_sems(n_x),
        compiler_params=_params("arbitrary"),
    )(pa, pa, pa, ps, wau, ba, gh, o_gla, d_og, sprev, dgaff_fox, d_proj, *slabs)
    return out[0], out[1], out[2], out[3], out[4:]


def _split3(x):
    x1 = x.astype(BF16).astype(F32)
    x2 = (x - x1).astype(BF16).astype(F32)
    x3 = (x - x1 - x2).astype(BF16).astype(F32)
    return x1, x2, x3


def _fox_prep(pa, ps, bfg):
    S = pa.shape[0]
    tm = _tile(S, FOX_TK)

    def body(ps_ref, b_ref, fq_ref, fk_ref, fv_ref, q_ref, k_ref, qt_ref, kt_ref, vt_ref, st_ref, carry):
        @pl.when(pl.program_id(0) == 0)
        def _():
            carry[...] = jnp.zeros_like(carry)

        vt = fv_ref[...].astype(F32).T.astype(BF16)
        ones_row = jnp.where(lax.broadcasted_iota(jnp.int32, (FOX_VT - FOX_DH, tm), 0) == 0, 1.0, 0.0).astype(BF16)
        for h in range(FOX_H):
            vt_ref[FOX_VT * h:FOX_VT * h + FOX_DH, :] = vt[FOX_DH * h:FOX_DH * (h + 1), :]
            vt_ref[FOX_VT * h + FOX_DH:FOX_VT * (h + 1), :] = ones_row
        lf = _log_sigmoid(ps_ref[...] + b_ref[...])
        rr = lax.broadcasted_iota(jnp.int32, (tm, tm), 0)
        cc = lax.broadcasted_iota(jnp.int32, (tm, tm), 1)
        tri = jnp.where(cc <= rr, 1.0, 0.0).astype(F32)
        f = jnp.dot(tri, lf, preferred_element_type=F32, precision=HIGHEST) + carry[0:1, :]
        carry[...] = jnp.broadcast_to(f[tm - 1:tm, :], carry.shape)
        f1, f2, f3 = _split3(f)
        lane = lax.broadcasted_iota(jnp.int32, (tm, 128), 1)
        st_row = lax.broadcasted_iota(jnp.int32, (8, 128), 0)
        st_lane = lax.broadcasted_iota(jnp.int32, (8, 128), 1)
        stats = jnp.zeros((8, 128), F32)
        for h in range(FOX_H):
            cols = slice(128 * h, 128 * (h + 1))
            c = FF_LANE + h
            a1, a2, a3 = f1[:, c:c + 1], f2[:, c:c + 1], f3[:, c:c + 1]
            q = fq_ref[:, cols].astype(F32) * FOX_SCALE
            k = fk_ref[:, cols].astype(F32)
            fh = f[:, c:c + 1]
            vals = (jnp.max(jnp.sum(q * q, axis=-1, keepdims=True)), jnp.max(jnp.sum(k * k, axis=-1, keepdims=True)),
                    jnp.max(fh), jnp.min(fh), jnp.min(jnp.sum(q * k, axis=-1, keepdims=True)))
            for n, val in enumerate(vals):
                stats = jnp.where((st_row == h) & (st_lane == n), val, stats)
            for n, a in enumerate((a1, a2, a3)):
                q = jnp.where(lane == AUG + n, a, q)
                k = jnp.where(lane == AUG + 3 + n, -a, k)
            q = jnp.where((lane >= AUG + 3) & (lane < AUG + 6), 1.0, q)
            k = jnp.where((lane >= AUG) & (lane < AUG + 3), 1.0, k)
            q_ref[:, cols] = q.astype(BF16)
            k_ref[:, cols] = k.astype(BF16)
            qt_ref[cols, :] = q.T.astype(BF16)
            kt_ref[cols, :] = k.T.astype(BF16)
        st_ref[0] = stats

    wide = lambda j: pl.BlockSpec((tm, 1024), lambda i: (i, j))
    tall = lambda n: pl.BlockSpec((n, tm), lambda i: (0, i))
    return pl.pallas_call(
        body, name="fox_prep", grid=(S // tm,),
        in_specs=[pl.BlockSpec((tm, 128), lambda i: (i, 4)), pl.BlockSpec((1, 128), lambda i: (0, 0)), wide(1), wide(2),
                  pl.BlockSpec((tm, FOX_W), lambda i: (i, A_FV // FOX_W))],
        out_specs=[wide(0), wide(0), tall(1024), tall(1024), tall(FOX_H * FOX_VT), pl.BlockSpec((1, 8, 128), lambda i: (i, 0, 0))],
        out_shape=[jax.ShapeDtypeStruct((S, 1024), BF16), jax.ShapeDtypeStruct((S, 1024), BF16),
                   jax.ShapeDtypeStruct((1024, S), BF16), jax.ShapeDtypeStruct((1024, S), BF16),
                   jax.ShapeDtypeStruct((FOX_H * FOX_VT, S), BF16), jax.ShapeDtypeStruct((S // tm, 8, 128), F32)],
        scratch_shapes=[pltpu.VMEM((8, 128), F32)],
        compiler_params=_params("arbitrary"),
    )(ps, bfg, pa, pa, pa)


FOX_PRUNE_AT = -90.0


def _fox_live_ranges(stats, n_sub, ratio):
    n_b = stats.shape[0]
    q2, k2, f_max, f_min, own = (stats[:, :, n].T for n in range(5))
    slack = 0.01 * jnp.sqrt(q2 * k2) + 1e-5 * jnp.abs(f_max) + 1.0
    bound = (1.01 * jnp.sqrt(q2[:, :, None] * k2[:, None, :]) + (f_max + slack - own)[:, :, None]
             - (f_min - 1e-5 * jnp.abs(f_min))[:, None, :])
    blocks = jnp.arange(n_b)
    dead = (bound <= FOX_PRUNE_AT) & (blocks[None, :] < blocks[:, None])[None]
    dead_fwd = dead.reshape(FOX_H, n_b // n_sub, n_sub, n_b).all(axis=2)
    first = jnp.sum(jnp.cumprod(dead_fwd.astype(jnp.int32), axis=2), axis=2)
    last_live = n_b - 1 - jnp.sum(jnp.cumprod(dead[:, ::-1, :].astype(jnp.int32), axis=1), axis=1)
    first_wide = blocks // ratio + 1
    narrow_end = jnp.minimum(jnp.minimum(first_wide * ratio, n_b)[None], last_live + 1)
    wide_end = jnp.where(last_live >= (first_wide * ratio)[None], last_live // ratio + 1, first_wide[None])
    return first.astype(jnp.int32), narrow_end.astype(jnp.int32), wide_end.astype(jnp.int32)


def _fox_fwd(qa, ka, vt, first):
    S = qa.shape[0]
    tq = _tile(S, FOX_TQ)
    tk = _tile(tq, FOX_TK)
    n_sub = tq // tk

    def body(first_ref, q_ref, k_ref, vt_ref, o_ref, lse_ref):
        pair, i = pl.program_id(0), pl.program_id(1)
        both = lambda f: tuple(f(hh) for hh in range(2))

        def blk(j, carry, diag, heads=(0, 1)):
            ks = pl.ds(pl.multiple_of(j * tk, tk), tk)
            q0 = 0 if diag is None else diag * tk

            def head(hh):
                if hh not in heads:
                    return carry[hh]
                m, acc = carry[hh]
                mo, ao = m[:, q0:], acc[:, q0:]
                s = _nt(k_ref[ks, 128 * hh:128 * (hh + 1)], q_ref[q0:, 128 * hh:128 * (hh + 1)])
                if diag is not None:
                    live = lax.broadcasted_iota(jnp.int32, s.shape, 1) >= lax.broadcasted_iota(jnp.int32, s.shape, 0)
                    s = jnp.where(live, s, NEG)
                mn = jnp.maximum(mo, jnp.max(s, axis=0, keepdims=True))
                p = jnp.exp((s - mn).astype(BF16))
                an = jnp.exp(mo - mn) * ao + _nn(vt_ref[FOX_VT * hh:FOX_VT * (hh + 1), ks], p)
                if q0:
                    mn, an = (jnp.concatenate([old[:, :q0], new], axis=1) for old, new in ((m, mn), (acc, an)))
                return mn, an

            return both(head)

        one = (jnp.full((1, tq), NEG, F32), jnp.zeros((FOX_VT, tq), F32))
        past = i * n_sub
        f0, f1 = first_ref[2 * pair, i], first_ref[2 * pair + 1, i]
        join = jnp.maximum(f0, f1)
        solo = lambda hh: lambda c: lax.fori_loop(jnp.minimum(f0, f1), join, lambda j, cc: blk(j, cc, None, (hh,)), c)
        carry = lax.cond(f0 < f1, solo(0), solo(1), (one, one))
        n_both = past - join
        carry = lax.fori_loop(0, n_both // 2, lambda jj, c: blk(join + 2 * jj + 1, blk(join + 2 * jj, c, None), None), carry)
        carry = lax.cond(n_both % 2 == 1, lambda c: blk(past - 1, c, None), lambda c: c, carry)
        for d in range(n_sub):
            carry = blk(past + d, carry, d)
        (m0, a0), (m1, a1) = carry
        l0, l1 = a0[FOX_DH:FOX_DH + 1], a1[FOX_DH:FOX_DH + 1]
        o_ref[...] = jnp.concatenate([a0[:FOX_DH] / l0, a1[:FOX_DH] / l1], axis=0).T
        lse_ref[0, 0:1, :] = m0 + jnp.log(l0)
        lse_ref[0, 1:2, :] = m1 + jnp.log(l1)
        lse_ref[0, 2:8, :] = jnp.zeros((6, tq), F32)

    return pl.pallas_call(
        body, name="fox_fwd", grid=(FOX_H // 2, S // tq),
        in_specs=[pl.BlockSpec(memory_space=pltpu.SMEM), pl.BlockSpec((tq, 256), lambda p, i: (i, p)),
                  pl.BlockSpec((S, 256), lambda p, i: (0, p)), pl.BlockSpec((2 * FOX_VT, S), lambda p, i: (p, 0))],
        out_specs=[pl.BlockSpec((tq, 128), lambda p, i: (i, p)), pl.BlockSpec((1, 8, tq), lambda p, i: (p, 0, i))],
        out_shape=[jax.ShapeDtypeStruct((S, FOX_W), F32), jax.ShapeDtypeStruct((FOX_H // 2, 8, S), F32)],
        compiler_params=_params("arbitrary", "arbitrary"),
    )(first, qa, ka, vt)


def _fox_delta(d_o, o):
    S = o.shape[0]
    tm = _tile(S, 512)

    def body(d_ref, o_ref, db_ref, dbt_ref, dl_ref):
        d = d_ref[...]
        db_ref[...] = d.astype(BF16)
        dbt_ref[...] = d.T.astype(BF16)
        prod = d * o_ref[...]
        rr = lax.broadcasted_iota(jnp.int32, (8, 128), 0)
        cc = lax.broadcasted_iota(jnp.int32, (8, 128), 1)
        ind = jnp.where(jnp.right_shift(cc, 6) == rr, 1.0, 0.0).astype(F32)
        for p in range(FOX_H // 2):
            dl_ref[p] = lax.dot_general(ind, prod[:, 128 * p:128 * (p + 1)], (((1,), (1,)), ((), ())),
                                        preferred_element_type=F32, precision=HIGHEST)

    row = pl.BlockSpec((tm, FOX_W), lambda i: (i, 0))
    return pl.pallas_call(
        body, name="fox_delta", grid=(S // tm,),
        in_specs=[row, row],
        out_specs=[row, pl.BlockSpec((FOX_W, tm), lambda i: (0, i)), pl.BlockSpec((FOX_H // 2, 8, tm), lambda i: (0, 0, i))],
        out_shape=[jax.ShapeDtypeStruct((S, FOX_W), BF16), jax.ShapeDtypeStruct((FOX_W, S), BF16),
                   jax.ShapeDtypeStruct((FOX_H // 2, 8, S), F32)],
        compiler_params=_params("arbitrary"),
    )(d_o, o)


def _fox_bwd(qa, qat, ka, kat, pa, dob, dobt, lse, delta, narrow_end, wide_end):
    S = qa.shape[0]
    tk = _tile(S, FOX_TK)
    wide = _tile(S, FOX_BWD_WIDE)
    ratio = wide // tk
    n_wide = S // wide

    def body(ne_ref, we_ref, q_ref, qt_ref, k_ref, kt_ref, v_ref, do_ref, dot_ref, lse_ref, dl_ref, dq_ref, dk_ref, dv_ref):
        h, jb = pl.program_id(0), pl.program_id(1)
        hh = h % 2

        @pl.when(jb == 0)
        def _():
            dq_ref[...] = jnp.zeros_like(dq_ref)

        lane = lax.broadcasted_iota(jnp.int32, (tk, 128), 1)
        vm = jnp.where(jnp.right_shift(lane, 6) == hh, v_ref[...], jnp.zeros((), BF16))
        kb, ktb = k_ref[...], kt_ref[0:FOX_LIVE, :]
        mine = pl.ds(pl.multiple_of(hh * FOX_DH, FOX_DH), FOX_DH)

        def blk(ib, tq, carry, masked):
            dk, dv = carry
            qs = pl.ds(pl.multiple_of(ib * tq, tq), tq)
            p = jnp.exp(_nt(kb, q_ref[qs, :]) - lse_ref[0, pl.ds(hh, 1), qs])
            if masked:
                live = lax.broadcasted_iota(jnp.int32, p.shape, 1) >= lax.broadcasted_iota(jnp.int32, p.shape, 0)
                p = jnp.where(live, p, 0.0)
            ds = (p * (_nt(vm, do_ref[qs, :]) - dl_ref[0, pl.ds(hh, 1), qs])).astype(BF16)
            dq_ref[0:FOX_LIVE, qs] += _nn(ktb, ds)
            return dk + _nt(qt_ref[0:FOX_LIVE, qs], ds), dv + _nt(dot_ref[mine, qs], p.astype(BF16))

        carry = blk(jb, tk, (jnp.zeros((FOX_LIVE, tk), F32), jnp.zeros((FOX_DH, tk), F32)), True)
        first_wide = jb // ratio + 1
        carry = lax.fori_loop(jb + 1, ne_ref[h, jb], lambda ib, c: blk(ib, tk, c, False), carry)
        last_wide = we_ref[h, jb]
        rest = jnp.maximum(last_wide - first_wide, 0)
        carry = lax.fori_loop(0, rest // 2, lambda t, c: blk(first_wide + 2 * t + 1, wide, blk(first_wide + 2 * t, wide, c, False),
                                                             False), carry)
        dk, dv = lax.cond(rest % 2 == 1, lambda c: blk(last_wide - 1, wide, c, False), lambda c: c, carry)
        dk_ref[0:FOX_LIVE, :] = dk
        dk_ref[FOX_LIVE:, :] = jnp.zeros((128 - FOX_LIVE, tk), F32)
        dv_ref[...] = dv

    once = pl.Buffered(1)
    rows = pl.BlockSpec((1, 8, S), lambda h, j: (h // 2, 0, 0))
    return pl.pallas_call(
        body, name="fox_bwd", grid=(FOX_H, S // tk),
        in_specs=[pl.BlockSpec(memory_space=pltpu.SMEM), pl.BlockSpec(memory_space=pltpu.SMEM),
                  pl.BlockSpec((S, 128), lambda h, j: (0, h)), pl.BlockSpec((128, S), lambda h, j: (h, 0)),
                  pl.BlockSpec((tk, 128), lambda h, j: (j, h)), pl.BlockSpec((128, tk), lambda h, j: (h, j)),
                  pl.BlockSpec((tk, 128), lambda h, j: (j, A_FV // 128 + h // 2)),
                  pl.BlockSpec((S, 128), lambda h, j: (0, h // 2)), pl.BlockSpec((128, S), lambda h, j: (h // 2, 0)),
                  rows, rows],
        out_specs=[pl.BlockSpec((128, S), lambda h, j: (h, 0), pipeline_mode=once),
                   pl.BlockSpec((128, tk), lambda h, j: (h, j)), pl.BlockSpec((FOX_DH, tk), lambda h, j: (h, j))],
        out_shape=[jax.ShapeDtypeStruct((1024, S), F32), jax.ShapeDtypeStruct((1024, S), F32),
                   jax.ShapeDtypeStruct((FOX_W, S), F32)],
        compiler_params=_params("arbitrary", "arbitrary"),
    )(narrow_end, wide_end, qa, qat, ka, kat, pa, dob, dobt, lse, delta)


def _fox_post(dq, dk, dv, ps, bfg, d_proj):
    S = dq.shape[1]
    tm = _tile(S, 512)
    n_b = S // tm

    def body(dq_ref, dk_ref, dv_ref, ps_ref, b_ref, _, dp_ref, dff_ref, dbf_ref, carry):
        first = pl.program_id(0) == 0

        @pl.when(first)
        def _():
            carry[...] = jnp.zeros_like(carry)

        low = lax.broadcasted_iota(jnp.int32, (tm, 128), 1) < FOX_DH
        for h in range(FOX_H):
            blk = slice(128 * h, 128 * (h + 1))
            dp_ref[:, blk] = jnp.where(low, dq_ref[blk, :].T * FOX_SCALE, 0.0).astype(BF16)
            dp_ref[:, 1024 + 128 * h:1024 + 128 * (h + 1)] = jnp.where(low, dk_ref[blk, :].T, 0.0).astype(BF16)
        dp_ref[:, 2048:P_FOX_W] = dv_ref[...].T.astype(BF16)
        rr = lax.broadcasted_iota(jnp.int32, (FOX_H, 1024), 0)
        cc = lax.broadcasted_iota(jnp.int32, (FOX_H, 1024), 1)
        sel_k = jnp.where(cc == 128 * rr + AUG + 3, 1.0, 0.0).astype(F32)
        sel_q = jnp.where(cc == 128 * rr + AUG, 1.0, 0.0).astype(F32)
        g = (jnp.dot(sel_k, dk_ref[...], preferred_element_type=F32, precision=HIGHEST)
             - jnp.dot(sel_q, dq_ref[...], preferred_element_type=F32, precision=HIGHEST))
        t_from = lax.broadcasted_iota(jnp.int32, (tm, tm), 0)
        t_to = lax.broadcasted_iota(jnp.int32, (tm, tm), 1)
        later = jnp.where(t_from >= t_to, 1.0, 0.0).astype(F32)
        dlf = jnp.dot(-g, later, preferred_element_type=F32, precision=HIGHEST) + carry[:, 0:1]
        carry[...] = jnp.broadcast_to(dlf[:, 0:1], carry.shape)
        cols = jnp.concatenate([jnp.zeros((FF_LANE, tm), F32), dlf, jnp.zeros((128 - FF_LANE - FOX_H, tm), F32)], axis=0).T
        dff = cols * jax.nn.sigmoid(-(ps_ref[...] + b_ref[...]))
        dff_ref[...] = dff
        part = _sum8(dff)

        @pl.when(first)
        def _():
            dbf_ref[...] = part

        @pl.when(jnp.logical_not(first))
        def _():
            dbf_ref[...] += part

    rev = lambda i: (n_b - 1 - i, 0)
    tall = lambda n: pl.BlockSpec((n, tm), lambda i: (0, n_b - 1 - i))
    return pl.pallas_call(
        body, name="fox_post", grid=(n_b,),
        in_specs=[tall(1024), tall(1024), tall(FOX_W), pl.BlockSpec((tm, 128), lambda i: (n_b - 1 - i, 4)),
                  pl.BlockSpec((1, 128), lambda i: (0, 0)), pl.BlockSpec(memory_space=pl.ANY)],
        out_specs=[pl.BlockSpec((tm, P_FOX_W), lambda i: (n_b - 1 - i, P_FOX // P_FOX_W)), pl.BlockSpec((tm, 128), rev),
                   pl.BlockSpec((8, 128), lambda i: (0, 0))],
        out_shape=[jax.ShapeDtypeStruct((S, P_W), BF16), jax.ShapeDtypeStruct((S, 128), F32),
                   jax.ShapeDtypeStruct((8, 128), F32)],
        input_output_aliases={5: 0},
        scratch_shapes=[pltpu.VMEM((8, 128), F32)],
        compiler_params=_params("arbitrary"),
    )(dq, dk, dv, ps, bfg, d_proj)


def _mem_prep(mem, g_mem, wkv):
    def body(m_ref, g_ref, w_ref, mn_ref, kv_ref):
        r, xh = _rms(m_ref[...])
        mn = (xh * g_ref[...]).astype(BF16)
        mn_ref[...] = mn
        kv_ref[...] = _nn(mn, w_ref[...]).astype(BF16)

    return pl.pallas_call(
        body, name="mem_prep",
        out_shape=[jax.ShapeDtypeStruct((N_MEM, D), BF16), jax.ShapeDtypeStruct((N_MEM, 2 * MEM_W), BF16)],
        compiler_params=pltpu.CompilerParams(vmem_limit_bytes=V7X_VMEM_LIMIT),
    )(mem, g_mem, wkv)


def _mem_softmax(qh, kh):
    s = _nt(qh, kh) * MEM_SCALE
    e = jnp.exp(s - jnp.max(s, axis=-1, keepdims=True))
    return e / jnp.sum(e, axis=-1, keepdims=True)


def _mem_fwd(pa, mkv):
    S = pa.shape[0]
    tm = _tile(S, 1024)

    def body(q_ref, kv_ref, o_ref):
        for h in range(MEM_H):
            cols = slice(MEM_DH * h, MEM_DH * (h + 1))
            p = _mem_softmax(q_ref[:, cols], kv_ref[:, cols])
            o_ref[:, cols] = _nn(p.astype(BF16), kv_ref[:, MEM_W + MEM_DH * h:MEM_W + MEM_DH * (h + 1)])

    return pl.pallas_call(
        body, name="mem_fwd", grid=(S // tm,),
        in_specs=[pl.BlockSpec((tm, MEM_W), lambda i: (i, A_MQ // MEM_W)), pl.BlockSpec((N_MEM, 2 * MEM_W), lambda i: (0, 0))],
        out_specs=pl.BlockSpec((tm, MEM_W), lambda i: (i, 0)),
        out_shape=jax.ShapeDtypeStruct((S, MEM_W), F32),
        compiler_params=_params("arbitrary"),
    )(pa, mkv)


def _mem_bwd(pa, mkv, d_o, d_proj):
    S = pa.shape[0]
    tm = _tile(S, 1024)

    def body(q_ref, kv_ref, do_ref, _, dq_ref, dkv_ref):
        first = pl.program_id(0) == 0
        parts = []
        for h in range(MEM_H):
            cols = slice(MEM_DH * h, MEM_DH * (h + 1))
            vcols = slice(MEM_W + MEM_DH * h, MEM_W + MEM_DH * (h + 1))
            qh, kh = q_ref[:, cols], kv_ref[:, cols]
            p = _mem_softmax(qh, kh)
            dob = do_ref[:, cols].astype(BF16)
            dp = _nt(dob, kv_ref[:, vcols])
            ds = (p * (dp - jnp.sum(p * dp, axis=-1, keepdims=True)) * MEM_SCALE).astype(BF16)
            dq_ref[:, cols] = _nn(ds, kh).astype(BF16)
            parts.append((cols, _tn(ds, qh)))
            parts.append((vcols, _tn(p.astype(BF16), dob)))

        @pl.when(first)
        def _():
            for sl, v in parts:
                dkv_ref[:, sl] = v

        @pl.when(jnp.logical_not(first))
        def _():
            for sl, v in parts:
                dkv_ref[:, sl] += v

    return pl.pallas_call(
        body, name="mem_bwd", grid=(S // tm,),
        in_specs=[pl.BlockSpec((tm, MEM_W), lambda i: (i, A_MQ // MEM_W)), pl.BlockSpec((N_MEM, 2 * MEM_W), lambda i: (0, 0)),
                  pl.BlockSpec((tm, MEM_W), lambda i: (i, 0)), pl.BlockSpec(memory_space=pl.ANY)],
        out_specs=[pl.BlockSpec((tm, MEM_W), lambda i: (i, P_MQ // MEM_W)), pl.BlockSpec((N_MEM, 2 * MEM_W), lambda i: (0, 0))],
        out_shape=[jax.ShapeDtypeStruct((S, P_W), BF16), jax.ShapeDtypeStruct((N_MEM, 2 * MEM_W), F32)],
        input_output_aliases={3: 0},
        compiler_params=_params("arbitrary"),
    )(pa, mkv, d_o, d_proj)


def _mem_prep_bwd(mem, g_mem, mn, wkv, dkv):
    def body(m_ref, g_ref, mn_ref, w_ref, d_ref, dw_ref, dg_ref):
        db = d_ref[...].astype(BF16)
        dw_ref[...] = _tn(mn_ref[...], db).astype(BF16)
        r, xh = _rms(m_ref[...])
        dg_ref[...] = _sum8(_nt(db, w_ref[...]) * xh)

    dw, dg = pl.pallas_call(
        body, name="mem_prep_bwd",
        out_shape=[jax.ShapeDtypeStruct((D, 2 * MEM_W), BF16), jax.ShapeDtypeStruct((8, D), F32)],
        compiler_params=pltpu.CompilerParams(vmem_limit_bytes=V7X_VMEM_LIMIT),
    )(mem, g_mem, mn, wkv, dkv)
    return dw.reshape(N_DEV, D // N_DEV, 2 * MEM_W), dg


def _rearrange_w_in(w):
    def heads128(cols):
        blk = w[:, cols:cols + FOX_W].reshape(D, FOX_H, FOX_DH)
        return jnp.pad(blk, ((0, 0), (0, 0), (0, 128 - FOX_DH))).reshape(D, FOX_H * 128)

    fq, fk, fv, mq, wg = heads128(O_FQ), heads128(O_FK), w[:, O_FV:O_FF], w[:, O_MQ:O_GT], w[:, O_GT:]
    gaff = jnp.concatenate([w[:, O_GA:O_FQ], w[:, O_FF:O_MQ], jnp.zeros((D, 128 - GLA_R - FOX_H), w.dtype)], axis=1)
    wa = jnp.concatenate([w[:, O_GQ:O_GG], fq, fk, fv, mq], axis=1)
    ws = jnp.concatenate([w[:, O_GG:O_GA], gaff], axis=1)
    wp = jnp.concatenate([fq, fk, fv, mq, wg, w[:, O_GQ:O_GG], ws, jnp.zeros((D, P_W - P_GLA - 1024 - S_W), w.dtype)], axis=1)
    return wa, wg, ws, wp


def _restore_w_in_grad(dwp):
    def unheads(off):
        return dwp[:, off:off + FOX_H * 128].reshape(D, FOX_H, 128)[:, :, :FOX_DH].reshape(D, FOX_W)

    g0 = P_GLA + 1024
    return jnp.concatenate([
        dwp[:, P_GLA:g0], dwp[:, g0:g0 + 512], dwp[:, g0 + 512:g0 + 512 + GLA_R], unheads(P_FOX), unheads(P_FOX + 1024),
        dwp[:, P_FOX + 2048:P_FOX + P_FOX_W], dwp[:, g0 + 512 + GLA_R:g0 + 512 + GLA_R + FOX_H], dwp[:, P_MQ:P_GT],
        dwp[:, P_GT:P_GLA]], axis=1)


def _local_step(x, mem, target, p, late_shards):
    S = x.shape[0]
    p = dict(p)
    wa, wg, ws, wp = _rearrange_w_in(p["w_in"])
    wau = jnp.pad(p["w_alpha_up"], ((0, 128 - GLA_R), (0, 0)))
    bfg = jnp.pad(p["b_forget"], ((0, 0), (FF_LANE, 128 - FF_LANE - FOX_H)))
    gh = p["g_gla_head"].reshape(1, GLA_V)

    pa, pg, ps, u, gathered = _proj(x, p["g_mix"], wa, wg, ws, late_shards)
    p.update({n: _unslab(t, ax) for (n, ax), t in zip(BIG[1:], gathered)})
    o_gla, og, sprev = _gla_fwd(pa, ps, wau, p["b_alpha"], gh)
    qa, ka, qat, kat, vt, fox_stats = _fox_prep(pa, ps, bfg)
    fox_tk = _tile(S, FOX_TK)
    fox_first, fox_narrow_end, fox_wide_end = _fox_live_ranges(fox_stats, _tile(S, FOX_TQ) // fox_tk,
                                                               _tile(S, FOX_BWD_WIDE) // fox_tk)
    o_fox, lse = _fox_fwd(qa, ka, vt, fox_first)
    mn, mkv = _mem_prep(mem, p["g_mem"], p["w_mem_kv"])
    o_mem = _mem_fwd(pa, mkv)
    y3, mg = _merge(og, o_fox, o_mem, p["w_gla_o"], p["w_fox_o"], p["w_mem_o"], pg)
    h1, u2 = _out_proj(mg, p["w_out"], x, p["g_ffn"])
    a, act = _ff1(u2, p["w_ff1"])
    dh2, dh2b, loss8, dg_final = _ff2_loss(act, p["w_ff2"], h1, p["g_final"].reshape(1, D), target)

    d_a = _dact(dh2b, p["w_ff2"], a)
    dw_ff2 = _wgrad(act, dh2b, "wgrad_ff2", 0)
    dh1, dh1b, dg_ffn = _nt_rmsbwd(d_a, p["w_ff1"], h1, p["g_ffn"], dh2, "dffn", True)
    dw_ff1 = _wgrad(u2, d_a, "wgrad_ff1", 1)
    dy_g, dy_f, dy_m, do_g, do_f, do_m, d_proj, arrived_ff = _dmerge(dh1b, p["w_out"], pg, y3, p["w_gla_o"], p["w_fox_o"],
                                                                     p["w_mem_o"], [dw_ff1, dw_ff2])
    dw_out = _wgrad(mg, dh1b, "wgrad_out", 0)
    dw_gla_o = _wgrad(og, dy_g, "wgrad_gla_o", 1)
    dw_fox_o = _wgrad(o_fox, dy_f, "wgrad_fox_o", 1)
    dw_mem_o = _wgrad(o_mem, dy_m, "wgrad_mem_o", 1)
    d_proj, d_mkv = _mem_bwd(pa, mkv, do_m, d_proj)
    dw_mem_kv, dg_mem = _mem_prep_bwd(mem, p["g_mem"], mn, p["w_mem_kv"], d_mkv)
    dob, dobt, delta = _fox_delta(do_f, o_fox)
    dq, dk, dv = _fox_bwd(qa, qat, ka, kat, pa, dob, dobt, lse, delta, fox_narrow_end, fox_wide_end)
    d_proj, dgaff_fox, db_forget = _fox_post(dq, dk, dv, ps, bfg, d_proj)
    d_proj, dw_au, db_alpha, dg_gla, arrived = _gla_bwd(pa, ps, wau, p["b_alpha"], gh, o_gla, do_g, sprev, dgaff_fox, d_proj,
                                                        [dw_mem_kv, dw_gla_o, dw_fox_o, dw_mem_o, dw_out])
    dw_in = _slabs(_restore_w_in_grad(_wgrad(u, d_proj, "wgrad_in")), 1).astype(BF16)
    dx, dg_mix, arrived_in = _nt_rmsbwd(d_proj, wp, x, p["g_mix"], dh1, "dmix", False, [dw_in])

    big = dict(w_in=arrived_in[0], w_ff1=arrived_ff[0], w_ff2=arrived_ff[1],
               **dict(zip(("w_mem_kv", "w_gla_o", "w_fox_o", "w_mem_o", "w_out"), arrived)))
    small = dict(g_mix=dg_mix, g_mem=dg_mem, g_ffn=dg_ffn, g_final=dg_final, b_alpha=db_alpha, g_gla_head=dg_gla,
                 b_forget=db_forget, w_alpha_up=dw_au, loss=loss8)
    return dx, big, small


BIG = (("w_in", 1), ("w_mem_kv", 0), ("w_gla_o", 1), ("w_fox_o", 1), ("w_mem_o", 1), ("w_out", 0), ("w_ff1", 1), ("w_ff2", 0))


def _peer(d):
    me = lax.axis_index("x") * 4 + lax.axis_index("y") * 2 + lax.axis_index("c")
    t = (me + d) % N_DEV
    return (t // 4, (t // 2) % 2, t % 2), me


def _exchange_sems(n):
    return [pltpu.SemaphoreType.DMA((n, N_DEV - 1)), pltpu.SemaphoreType.DMA((n, N_DEV - 1)), pltpu.SemaphoreType.DMA((n,))]


def _exchange_call(body, blocks, out_shape, name):
    n = len(blocks)
    any_spec = pl.BlockSpec(memory_space=pl.ANY)
    return pl.pallas_call(body, name=name, in_specs=[any_spec] * n, out_specs=[any_spec] * n, out_shape=out_shape,
                          scratch_shapes=_exchange_sems(n))(*blocks)


class _AllToAll:
    def __init__(self, ins, outs, sems, gather):
        send, recv, loc = sems
        n = len(ins)
        _, me = _peer(0)
        src = (lambda k, j: ins[k]) if gather else (lambda k, j: ins[k].at[j])
        self.local = [pltpu.make_async_copy(src(k, me), outs[k].at[me], loc.at[k]) for k in range(n)]
        self.remote = []
        for d in range(1, N_DEV):
            to, _ = _peer(d)
            self.remote += [pltpu.make_async_remote_copy(
                src_ref=src(k, (me + d) % N_DEV), dst_ref=outs[k].at[me], send_sem=send.at[k, d - 1],
                recv_sem=recv.at[k, d - 1], device_id=to, device_id_type=MESH) for k in range(n)]

    def start(self):
        for cp in self.local + self.remote:
            cp.start()

    def wait(self):
        for cp in self.remote:
            cp.wait_send()
        for cp in self.remote:
            cp.wait_recv()
        for cp in self.local:
            cp.wait()


def _gathered_shapes(shards):
    return [jax.ShapeDtypeStruct((N_DEV,) + b.shape, b.dtype) for b in shards]


def _gather_weights(shards):
    n = len(shards)

    def body(*refs):
        ins, outs = refs[:n], refs[n:2 * n]
        send, recv, loc = refs[2 * n:]
        x, y, c = lax.axis_index("x"), lax.axis_index("y"), lax.axis_index("c")
        sibling = (x, y, 1 - c)
        chips = [(1 - x, y), (x, 1 - y), (1 - x, 1 - y)]
        slot = lambda px, py, pc: px * 4 + py * 2 + pc

        def copy(k, s, block, to, src=None):
            rows = outs[k].at[slot(*block)]
            return pltpu.make_async_remote_copy(src_ref=rows if src is None else src, dst_ref=rows, send_sem=send.at[k, s],
                                                recv_sem=recv.at[k, s], device_id=to, device_id_type=MESH)

        me = (x, y, c)
        own = [pltpu.make_async_copy(ins[k], outs[k].at[slot(*me)], loc.at[k]) for k in range(n)]
        first = [copy(k, 0, me, sibling, src=ins[k]) for k in range(n)]
        first += [copy(k, 1 + j, me, (*chip, c), src=ins[k]) for j, chip in enumerate(chips) for k in range(n)]
        for cp in own + first:
            cp.start()
        passed = []
        for j, chip in enumerate(chips):
            for k in range(n):
                copy(k, 1 + j, (*chip, c), me).wait_recv()
                fwd = copy(k, 4 + j, (*chip, c), sibling)
                fwd.start()
                passed.append(fwd)
        for k in range(n):
            copy(k, 0, sibling, me).wait_recv()
        for j, chip in enumerate(chips):
            for k in range(n):
                copy(k, 4 + j, (*chip, 1 - c), me).wait_recv()
        for cp in first + passed:
            cp.wait_send()
        for cp in own:
            cp.wait()

    return _exchange_call(body, shards, [jax.ShapeDtypeStruct((N_DEV,) + b.shape, b.dtype) for b in shards], "gather_weights")


def _adamw_math(g, w, m, v):
    m2 = ADAM_B1 * m + (1.0 - ADAM_B1) * g
    v2 = ADAM_B2 * v + (1.0 - ADAM_B2) * jnp.square(g)
    m_hat = m2 / (1.0 - ADAM_B1 ** ADAM_STEP)
    v_hat = v2 / (1.0 - ADAM_B2 ** ADAM_STEP)
    delta = -ADAM_LR * (m_hat / (jnp.sqrt(v_hat) + ADAM_EPS) + ADAM_WD * w)
    return delta, m2, v2


def _adamw_sum(parts, w, m, v, name):
    R, C = w.shape
    tr = _tile(R, 128)

    def body(p_ref, w_ref, m_ref, v_ref, g_ref, d_ref, m2_ref, v2_ref):
        g = p_ref[0].astype(F32)
        for j in range(1, p_ref.shape[0]):
            g = g + p_ref[j].astype(F32)
        g_ref[...] = g
        d_ref[...], m2_ref[...], v2_ref[...] = _adamw_math(g, w_ref[...], m_ref[...], v_ref[...])

    blk = pl.BlockSpec((tr, C), lambda i: (i, 0))
    return pl.pallas_call(
        body, name=name, grid=(R // tr,),
        in_specs=[pl.BlockSpec((parts.shape[0], tr, C), lambda i: (0, i, 0)), blk, blk, blk],
        out_specs=[blk] * 4, out_shape=[jax.ShapeDtypeStruct((R, C), F32)] * 4,
        compiler_params=_params("arbitrary"),
    )(parts, w, m, v)


SMALL_ROWS = 24


def _pack_small(d):
    mixed = jnp.concatenate([d["b_alpha"].reshape(1, GLA_K), d["g_gla_head"].reshape(1, GLA_V),
                             jnp.pad(d["b_forget"].reshape(1, FOX_H), ((0, 0), (FF_LANE, 128 - FF_LANE - FOX_H))),
                             jnp.zeros((1, 128), F32)], axis=1)
    rows = [d["g_mix"].reshape(1, D), d["g_mem"].reshape(1, D), d["g_ffn"].reshape(1, D), d["g_final"].reshape(1, D), mixed,
            jnp.zeros((3, D), F32), jnp.pad(d["w_alpha_up"].reshape(GLA_R, GLA_K), ((0, 0), (0, D - GLA_K)))]
    return jnp.concatenate(rows, axis=0)


def _unpack_small(t):
    return dict(g_mix=t[0:1], g_mem=t[1:2], g_ffn=t[2:3], g_final=t[3], b_alpha=t[4:5, 0:GLA_K],
                g_gla_head=t[4:5, GLA_K:GLA_K + GLA_V].reshape(1, GLA_H, GLA_DV),
                b_forget=t[4:5, 768 + FF_LANE:768 + FF_LANE + FOX_H], w_alpha_up=t[8:24, 0:GLA_K].reshape(1, GLA_R, GLA_K))


def _small_allreduce(small, w, m, v):
    def body(gm, gme, gf, gfi, ba, gg, bf, wau, ls, w_ref, m_ref, v_ref, g_ref, d_ref, m2_ref, v2_ref, l_ref,
             buf, send, recv):
        _, me = _peer(0)
        buf[me] = jnp.zeros((SMALL_ROWS, D), F32)
        for r, ref in enumerate((gm, gme, gf, gfi)):
            buf[me, r:r + 1, :] = jnp.sum(ref[...], axis=0, keepdims=True)
        buf[me, 4:5, 0:GLA_K] = jnp.sum(ba[...], axis=0, keepdims=True)
        buf[me, 4:5, GLA_K:GLA_K + GLA_V] = jnp.sum(gg[...], axis=0, keepdims=True)
        buf[me, 4:5, 768:896] = jnp.sum(bf[...], axis=0, keepdims=True)
        lrow = jnp.sum(ls[...], axis=0, keepdims=True)
        lsum = lrow[:, 0:128]
        for c in range(1, D // 128):
            lsum = lsum + lrow[:, 128 * c:128 * (c + 1)]
        buf[me, 4:5, 896:1024] = lsum
        buf[me, 8:24, 0:GLA_K] = wau[0:GLA_R, :]
        remote = []
        for d in range(1, N_DEV):
            to, me = _peer(d)
            cp = pltpu.make_async_remote_copy(src_ref=buf.at[me], dst_ref=buf.at[me], send_sem=send.at[d - 1],
                                              recv_sem=recv.at[d - 1], device_id=to, device_id_type=MESH)
            cp.start()
            remote.append(cp)
        for cp in remote:
            cp.wait_send()
        for cp in remote:
            cp.wait_recv()
        g = buf[0]
        for j in range(1, N_DEV):
            g = g + buf[j]
        g_ref[...] = g
        d_ref[...], m2_ref[...], v2_ref[...] = _adamw_math(g, w_ref[...], m_ref[...], v_ref[...])
        l_ref[...] = g[4:5, 896:1024]

    packed = jax.ShapeDtypeStruct((SMALL_ROWS, D), F32)
    return pl.pallas_call(
        body, name="small_allreduce",
        out_shape=[packed, packed, packed, packed, jax.ShapeDtypeStruct((1, 128), F32)],
        scratch_shapes=[pltpu.VMEM((N_DEV, SMALL_ROWS, D), F32), pltpu.SemaphoreType.DMA((N_DEV - 1,)),
                        pltpu.SemaphoreType.DMA((N_DEV - 1,))],
    )(small["g_mix"], small["g_mem"], small["g_ffn"], small["g_final"], small["b_alpha"], small["g_gla_head"],
      small["b_forget"], small["w_alpha_up"], small["loss"], w, m, v)


def _slabs(g, axis):
    R, C = g.shape
    if axis == 0:
        return g.reshape(N_DEV, R // N_DEV, C)
    return g.reshape(R, N_DEV, C // N_DEV).transpose(1, 0, 2)


def _unslab(t, axis):
    n, r, c = t.shape
    if axis == 0:
        return t.reshape(n * r, c)
    return t.transpose(1, 0, 2).reshape(r, n * c)


def kernel(x, mem, g_mix, w_in, w_alpha_up, b_alpha, b_forget, g_gla_head, g_mem, w_mem_kv, w_gla_o, w_fox_o, w_mem_o, w_out, g_ffn, w_ff1, w_ff2, g_final, loss_target, m_g_mix, m_w_in, m_w_alpha_up, m_b_alpha, m_b_forget, m_g_gla_head, m_g_mem, m_w_mem_kv, m_w_gla_o, m_w_fox_o, m_w_mem_o, m_w_out, m_g_ffn, m_w_ff1, m_w_ff2, m_g_final, v_g_mix, v_w_in, v_w_alpha_up, v_b_alpha, v_b_forget, v_g_gla_head, v_g_mem, v_w_mem_kv, v_w_gla_o, v_w_fox_o, v_w_mem_o, v_w_out, v_g_ffn, v_w_ff1, v_w_ff2, v_g_final):
    names = ["g_mix", "w_in", "w_alpha_up", "b_alpha", "b_forget", "g_gla_head", "g_mem", "w_mem_kv", "w_gla_o", "w_fox_o",
             "w_mem_o", "w_out", "g_ffn", "w_ff1", "w_ff2", "g_final"]
    w = dict(g_mix=g_mix, w_in=w_in, w_alpha_up=w_alpha_up, b_alpha=b_alpha, b_forget=b_forget, g_gla_head=g_gla_head,
             g_mem=g_mem, w_mem_kv=w_mem_kv, w_gla_o=w_gla_o, w_fox_o=w_fox_o, w_mem_o=w_mem_o, w_out=w_out, g_ffn=g_ffn,
             w_ff1=w_ff1, w_ff2=w_ff2, g_final=g_final)
    m = dict(g_mix=m_g_mix, w_in=m_w_in, w_alpha_up=m_w_alpha_up, b_alpha=m_b_alpha, b_forget=m_b_forget,
             g_gla_head=m_g_gla_head, g_mem=m_g_mem, w_mem_kv=m_w_mem_kv, w_gla_o=m_w_gla_o, w_fox_o=m_w_fox_o,
             w_mem_o=m_w_mem_o, w_out=m_w_out, g_ffn=m_g_ffn, w_ff1=m_w_ff1, w_ff2=m_w_ff2, g_final=m_g_final)
    v = dict(g_mix=v_g_mix, w_in=v_w_in, w_alpha_up=v_w_alpha_up, b_alpha=v_b_alpha, b_forget=v_b_forget,
             g_gla_head=v_g_gla_head, g_mem=v_g_mem, w_mem_kv=v_w_mem_kv, w_gla_o=v_w_gla_o, w_fox_o=v_w_fox_o,
             w_mem_o=v_w_mem_o, w_out=v_w_out, g_ffn=v_g_ffn, w_ff1=v_w_ff1, w_ff2=v_w_ff2, g_final=v_g_final)
    me = lax.axis_index("x") * 4 + lax.axis_index("y") * 2 + lax.axis_index("c")

    shard = lambda n: w[n][0].astype(BF16)
    w_in_all, w_au_all = _gather_weights([shard("w_in"), shard("w_alpha_up")])
    p = dict(w_in=_unslab(w_in_all, 1), w_alpha_up=_unslab(w_au_all, 1), g_mix=g_mix, b_alpha=b_alpha, b_forget=b_forget,
             g_gla_head=g_gla_head, g_mem=g_mem, g_ffn=g_ffn, g_final=g_final)

    dx, big, small = _local_step(x[0], mem[0], loss_target[0], p, [shard(n) for n, _ in BIG[1:]])

    out_g, out_d, out_m, out_v = {}, {}, {}, {}
    for n, _ in BIG:
        g_, d_, m_, v_ = _adamw_sum(big[n], w[n][0], m[n][0], v[n][0], "adamw_" + n)
        out_g[n], out_d[n], out_m[n], out_v[n] = g_[None], d_[None], m_[None], v_[None]

    full = lambda d: dict(d, w_alpha_up=jnp.zeros((1, GLA_R, GLA_K), F32))
    gs, ds, ms, vs, lrow = _small_allreduce(small, _pack_small(full(w)), _pack_small(full(m)), _pack_small(full(v)))
    g_s, d_s, m_s, v_s = _unpack_small(gs), _unpack_small(ds), _unpack_small(ms), _unpack_small(vs)
    for n in names:
        if n not in out_g and n != "w_alpha_up":
            out_g[n], out_d[n], out_m[n], out_v[n] = g_s[n], d_s[n], m_s[n], v_s[n]
    g_au = lax.dynamic_slice_in_dim(g_s["w_alpha_up"][0], me * (GLA_K // N_DEV), GLA_K // N_DEV, axis=1)
    g_, d_, m_, v_ = _adamw_sum(g_au[None], w_alpha_up[0], m_w_alpha_up[0], v_w_alpha_up[0], "adamw_w_alpha_up")
    out_g["w_alpha_up"], out_d["w_alpha_up"], out_m["w_alpha_up"], out_v["w_alpha_up"] = g_[None], d_[None], m_[None], v_[None]

    loss = jnp.sum(lrow) * (0.5 / D)
    return (loss, dx[None], *[out_g[n] for n in names], *[out_d[n] for n in names], *[out_m[n] for n in names],
            *[out_v[n] for n in names])
```

```python
import jax
import jax.numpy as jnp
from jax import lax
from jax.experimental import pallas as pl
from jax.experimental.pallas import tpu as pltpu

F32, BF16 = jnp.float32, jnp.bfloat16
HIGHEST = lax.Precision.HIGHEST
MESH = pl.DeviceIdType.MESH

N_DEV = 8
D = 1024
EPS = 1e-6
CHUNK = 64
N_MEM = 256
GLA_H, GLA_DK, GLA_DV = 4, 64, 128
GLA_K, GLA_V, GLA_R = 256, 512, 16
FOX_H, FOX_DH, FOX_W = 8, 64, 512
MEM_H, MEM_DH, MEM_W = 4, 128, 512
D_FF = 4096
D_IN = 6680
FOX_SCALE = 0.125
GLA_SCALE = 0.125
MEM_SCALE = MEM_DH ** -0.5
GLA_TAU_INV = 1.0 / 16.0
NEG = -1e30

O_GQ, O_GK, O_GV, O_GG, O_GA, O_FQ, O_FK, O_FV, O_FF, O_MQ, O_GT = 0, 256, 512, 1024, 1536, 1552, 2064, 2576, 3088, 3096, 3608
A_FQ, A_FK, A_FV, A_MQ, A_W = 1024, 2048, 3072, 3584, 4096
S_W = 640
G_W = 3072
P_FOX, P_FOX_W, P_MQ, P_GT, P_GLA, P_GLA_W, P_W = 0, 2560, 2560, 3072, 6144, 2048, 8192
FF_LANE = 16
AUG = 64
FOX_LIVE = 80
FOX_VT = 80

ADAM_LR, ADAM_B1, ADAM_B2, ADAM_EPS, ADAM_WD, ADAM_STEP = 0.001, 0.9, 0.999, 1e-08, 0.01, 10
V7X_VMEM_LIMIT = 54 * 1024 * 1024
FOX_TK = 512
FOX_TQ = 2048
FOX_BWD_WIDE = 1024


def _params(*sem):
    return pltpu.CompilerParams(dimension_semantics=sem, vmem_limit_bytes=V7X_VMEM_LIMIT)


def _nt(a, b):
    return lax.dot_general(a, b, (((1,), (1,)), ((), ())), preferred_element_type=F32)


def _tn(a, b):
    return lax.dot_general(a, b, (((0,), (0,)), ((), ())), preferred_element_type=F32)


def _nn(a, b):
    return jnp.dot(a, b, preferred_element_type=F32)


def _log_sigmoid(z):
    return jnp.minimum(z, 0.0) - jnp.log(1.0 + jnp.exp(-jnp.abs(z)))


def _sum01(m01, x):
    x1 = x.astype(BF16)
    x2 = (x - x1.astype(F32)).astype(BF16)
    x3 = (x - x1.astype(F32) - x2.astype(F32)).astype(BF16)
    return _nn(m01, x1) + _nn(m01, x2) + _nn(m01, x3)


def _sum8(x):
    return x.reshape(x.shape[0] // 8, 8, x.shape[1]).sum(axis=0)


def _rms(xv):
    r = lax.rsqrt(jnp.mean(xv * xv, axis=-1, keepdims=True) + EPS)
    return r, xv * r


def _rms_bwd(du, g, r, xh):
    w = du * g
    return r * (w - xh * jnp.mean(w * xh, axis=-1, keepdims=True))


def _row_chunks(n, size=256):
    return [slice(r, r + min(size, n)) for r in range(0, n, min(size, n))]


def _tile(n, pref):
    t = min(n, pref)
    assert n % t == 0, (n, t)
    return t


def _proj(x, g, wa, wg, ws, shards):
    S = x.shape[0]
    tm, tn = _tile(S, 1024), 1024
    n_a, n_g = A_W // tn, G_W // tn
    n_i, n_j = S // tm, n_a + n_g + 1
    n_x = len(shards)

    def body(*refs):
        x_ref, g_ref, wa_ref, wg_ref, ws_ref = refs[:5]
        pa_ref, pg_ref, ps_ref, u_ref = refs[5 + n_x:9 + n_x]
        u_s = refs[9 + 2 * n_x]
        gather = lambda: _AllToAll(refs[5:5 + n_x], refs[9 + n_x:9 + 2 * n_x], refs[10 + 2 * n_x:], True)
        i, j = pl.program_id(0), pl.program_id(1)

        @pl.when((i == 0) & (j == 0))
        def _():
            gather().start()

        @pl.when(j == 0)
        def _():
            r, xh = _rms(x_ref[...])
            u_s[...] = (xh * g_ref[...]).astype(BF16)
            u_ref[...] = u_s[...]

        @pl.when(j < n_a)
        def _():
            pa_ref[...] = _nn(u_s[...], wa_ref[...]).astype(BF16)

        @pl.when((j >= n_a) & (j < n_a + n_g))
        def _():
            pg_ref[...] = _nn(u_s[...], wg_ref[...]).astype(BF16)

        @pl.when(j == n_a + n_g)
        def _():
            ps_ref[...] = _nn(u_s[...], ws_ref[...])

        @pl.when((i == n_i - 1) & (j == n_j - 1))
        def _():
            gather().wait()

    in_a = lambda j: jnp.minimum(j, n_a - 1)
    in_g = lambda j: jnp.clip(j - n_a, 0, n_g - 1)
    row = pl.BlockSpec((tm, D), lambda i, j: (i, 0))
    any_spec = pl.BlockSpec(memory_space=pl.ANY)
    out = pl.pallas_call(
        body, name="proj", grid=(n_i, n_j),
        in_specs=[row, pl.BlockSpec((1, D), lambda i, j: (0, 0)), pl.BlockSpec((D, tn), lambda i, j: (0, in_a(j))),
                  pl.BlockSpec((D, tn), lambda i, j: (0, in_g(j))),
                  pl.BlockSpec((D, S_W), lambda i, j: (0, 0), pipeline_mode=pl.Buffered(1))] + [any_spec] * n_x,
        out_specs=[pl.BlockSpec((tm, tn), lambda i, j: (i, in_a(j))), pl.BlockSpec((tm, tn), lambda i, j: (i, in_g(j))),
                   pl.BlockSpec((tm, S_W), lambda i, j: (i, 0)), row] + [any_spec] * n_x,
        out_shape=[jax.ShapeDtypeStruct((S, A_W), BF16), jax.ShapeDtypeStruct((S, G_W), BF16),
                   jax.ShapeDtypeStruct((S, S_W), F32), jax.ShapeDtypeStruct((S, D), BF16)] + _gathered_shapes(shards),
        scratch_shapes=[pltpu.VMEM((tm, D), BF16)] + _exchange_sems(n_x),
        compiler_params=_params("arbitrary", "arbitrary"),
    )(x, g, wa, wg, ws, *shards)
    return out[0], out[1], out[2], out[3], out[4:]


def _wgrad(a, b, name, slab_axis=None):
    S, Ka = a.shape
    N = b.shape[1]
    tka, tn, ts = _tile(Ka, 1024), _tile(N, 1024), _tile(S, 4096)
    n_s = S // ts
    per = N // N_DEV
    slabs_per_step = tn // per

    def body(a_ref, b_ref, o_ref, acc):
        s = pl.program_id(2)

        @pl.when(s == 0)
        def _():
            acc[...] = jnp.zeros_like(acc)

        acc[...] += _tn(a_ref[...].astype(BF16), b_ref[...].astype(BF16))

        @pl.when(s == n_s - 1)
        def _():
            if slab_axis == 1:
                for q in range(slabs_per_step):
                    o_ref[q] = acc[:, per * q:per * (q + 1)].astype(BF16)
            else:
                o_ref[...] = acc[...].astype(o_ref.dtype)

    if slab_axis == 1:
        out_spec = pl.BlockSpec((slabs_per_step, tka, per), lambda i, j, s: (j, i, 0))
        out_shape = jax.ShapeDtypeStruct((N_DEV, Ka, per), BF16)
    else:
        out_spec = pl.BlockSpec((tka, tn), lambda i, j, s: (i, j))
        out_shape = jax.ShapeDtypeStruct((Ka, N), BF16)
    out = pl.pallas_call(
        body, name=name, grid=(Ka // tka, N // tn, n_s),
        in_specs=[pl.BlockSpec((ts, tka), lambda i, j, s: (s, i)), pl.BlockSpec((ts, tn), lambda i, j, s: (s, j))],
        out_specs=out_spec, out_shape=out_shape,
        scratch_shapes=[pltpu.VMEM((tka, tn), F32)],
        compiler_params=_params("arbitrary", "arbitrary", "arbitrary"),
    )(a, b)
    return out.reshape(N_DEV, Ka // N_DEV, N) if slab_axis == 0 else out


def _nt_rmsbwd(a, w, xin, g, dres, name, emit_bf16, slabs=()):
    S, K = a.shape
    tm, tk = _tile(S, 1024), _tile(K, 1024 if emit_bf16 else 2048)
    n_i, n_k = S // tm, K // tk
    n_x, n_o = len(slabs), 3 if emit_bf16 else 2

    def body(*refs):
        a_ref, w_ref, x_ref, g_ref, r_ref = refs[:5]
        o_ref = refs[5 + n_x]
        rest = refs[6 + n_x:5 + n_x + n_o] + (refs[5 + 2 * n_x + n_o],)
        dg_ref, acc = rest[-2], rest[-1]
        scatter = lambda: _AllToAll(refs[5:5 + n_x], refs[5 + n_x + n_o:5 + 2 * n_x + n_o], refs[6 + 2 * n_x + n_o:], False)
        i, k = pl.program_id(0), pl.program_id(1)

        if n_x:
            @pl.when((i == 0) & (k == 0))
            def _():
                scatter().start()

        @pl.when(k == 0)
        def _():
            acc[...] = jnp.zeros_like(acc)

        acc[...] += _nt(a_ref[...], w_ref[...])

        @pl.when(k == n_k - 1)
        def _():
            @pl.when(i == 0)
            def _():
                dg_ref[...] = jnp.zeros_like(dg_ref)

            for rows in _row_chunks(tm):
                du = acc[rows, :]
                r, xh = _rms(x_ref[rows, :])
                out = r_ref[rows, :] + _rms_bwd(du, g_ref[...], r, xh)
                o_ref[rows, :] = out
                if emit_bf16:
                    rest[0][rows, :] = out.astype(BF16)
                dg_ref[...] += _sum8(du * xh)

        if n_x:
            @pl.when((i == n_i - 1) & (k == n_k - 1))
            def _():
                scatter().wait()

    row = pl.BlockSpec((tm, D), lambda i, k: (i, 0))
    any_spec = pl.BlockSpec(memory_space=pl.ANY)
    out_shape = [jax.ShapeDtypeStruct((S, D), F32)]
    out_specs = [row]
    if emit_bf16:
        out_shape.append(jax.ShapeDtypeStruct((S, D), BF16))
        out_specs.append(row)
    out_shape.append(jax.ShapeDtypeStruct((8, D), F32))
    out_specs.append(pl.BlockSpec((8, D), lambda i, k: (0, 0)))
    out = pl.pallas_call(
        body, name=name, grid=(n_i, n_k),
        in_specs=[pl.BlockSpec((tm, tk), lambda i, k: (i, k)), pl.BlockSpec((D, tk), lambda i, k: (0, k)),
                  row, pl.BlockSpec((1, D), lambda i, k: (0, 0)), row] + [any_spec] * n_x,
        out_specs=out_specs + [any_spec] * n_x,
        out_shape=out_shape + [jax.ShapeDtypeStruct(b.shape, b.dtype) for b in slabs],
        scratch_shapes=[pltpu.VMEM((tm, D), F32)] + (_exchange_sems(n_x) if n_x else []),
        compiler_params=_params("arbitrary", "arbitrary"),
    )(a, w, xin, g, dres, *slabs)
    return (*out[:n_o], out[n_o:]) if n_x else out


def _merge(og, ofox, omem, wg, wf, wm, pg):
    S = og.shape[0]
    tm = _tile(S, 512)

    def body(og_ref, of_ref, om_ref, wg_ref, wf_ref, wm_ref, pg_ref, y_ref, mg_ref):
        tot = None
        for i, (o_ref, w_ref) in enumerate(((og_ref, wg_ref), (of_ref, wf_ref), (om_ref, wm_ref))):
            y = _nn(o_ref[...].astype(BF16), w_ref[...])
            y_ref[i] = y.astype(BF16)
            t = jax.nn.sigmoid(pg_ref[:, D * i:D * (i + 1)].astype(F32)) * y
            tot = t if tot is None else tot + t
        mg_ref[...] = tot.astype(BF16)

    o_spec = pl.BlockSpec((tm, 512), lambda i: (i, 0))
    w_spec = pl.BlockSpec((512, D), lambda i: (0, 0))
    return pl.pallas_call(
        body, name="merge", grid=(S // tm,),
        in_specs=[o_spec, o_spec, o_spec, w_spec, w_spec, w_spec, pl.BlockSpec((tm, G_W), lambda i: (i, 0))],
        out_specs=[pl.BlockSpec((3, tm, D), lambda i: (0, i, 0)), pl.BlockSpec((tm, D), lambda i: (i, 0))],
        out_shape=[jax.ShapeDtypeStruct((3, S, D), BF16), jax.ShapeDtypeStruct((S, D), BF16)],
        compiler_params=_params("arbitrary"),
    )(og, ofox, omem, wg, wf, wm, pg)


def _out_proj(mg, w_out, x, g_ffn):
    S = x.shape[0]
    tm = _tile(S, 1024)

    def body(mg_ref, w_ref, x_ref, g_ref, h_ref, u_ref):
        h = x_ref[...] + _nn(mg_ref[...], w_ref[...])
        h_ref[...] = h
        r, xh = _rms(h)
        u_ref[...] = (xh * g_ref[...]).astype(BF16)

    row = pl.BlockSpec((tm, D), lambda i: (i, 0))
    return pl.pallas_call(
        body, name="out_proj", grid=(S // tm,),
        in_specs=[row, pl.BlockSpec((D, D), lambda i: (0, 0)), row, pl.BlockSpec((1, D), lambda i: (0, 0))],
        out_specs=[row, row],
        out_shape=[jax.ShapeDtypeStruct((S, D), F32), jax.ShapeDtypeStruct((S, D), BF16)],
        compiler_params=_params("arbitrary"),
    )(mg, w_out, x, g_ffn)


def _ff1(u2, w1):
    S = u2.shape[0]
    tm, tn = _tile(S, 2048), 1024

    def body(u_ref, w_ref, a_ref, act_ref):
        a = _nn(u_ref[...], w_ref[...])
        a_ref[...] = a.astype(BF16)
        act_ref[...] = jnp.square(jnp.maximum(a, 0.0)).astype(BF16)

    blk = pl.BlockSpec((tm, tn), lambda i, j: (i, j))
    return pl.pallas_call(
        body, name="ff1", grid=(S // tm, D_FF // tn),
        in_specs=[pl.BlockSpec((tm, D), lambda i, j: (i, 0)), pl.BlockSpec((D, tn), lambda i, j: (0, j))],
        out_specs=[blk, blk],
        out_shape=[jax.ShapeDtypeStruct((S, D_FF), BF16), jax.ShapeDtypeStruct((S, D_FF), BF16)],
        compiler_params=_params("arbitrary", "arbitrary"),
    )(u2, w1)


def _ff2_loss(act, w2, h1, g_final, target):
    S = act.shape[0]
    tm, tk = _tile(S, 1024), 1024
    n_k = D_FF // tk

    def body(a_ref, w_ref, h_ref, g_ref, t_ref, d_ref, db_ref, ls_ref, dg_ref, acc):
        i, k = pl.program_id(0), pl.program_id(1)

        @pl.when(k == 0)
        def _():
            acc[...] = jnp.zeros_like(acc)

        acc[...] += _nn(a_ref[...], w_ref[...])

        @pl.when(k == n_k - 1)
        def _():
            @pl.when(i == 0)
            def _():
                ls_ref[...] = jnp.zeros_like(ls_ref)
                dg_ref[...] = jnp.zeros_like(dg_ref)

            gf = g_ref[...]
            for rows in _row_chunks(tm):
                r, xh = _rms(h_ref[rows, :] + acc[rows, :])
                err = xh * gf - t_ref[rows, :]
                dy = err * (1.0 / D)
                dh = _rms_bwd(dy, gf, r, xh)
                d_ref[rows, :] = dh
                db_ref[rows, :] = dh.astype(BF16)
                ls_ref[...] += _sum8(err * err)
                dg_ref[...] += _sum8(dy * xh)

    row = pl.BlockSpec((tm, D), lambda i, k: (i, 0))
    part = pl.BlockSpec((8, D), lambda i, k: (0, 0))
    return pl.pallas_call(
        body, name="ff2_loss", grid=(S // tm, n_k),
        in_specs=[pl.BlockSpec((tm, tk), lambda i, k: (i, k)), pl.BlockSpec((tk, D), lambda i, k: (k, 0)),
                  row, pl.BlockSpec((1, D), lambda i, k: (0, 0)), row],
        out_specs=[row, row, part, part],
        out_shape=[jax.ShapeDtypeStruct((S, D), F32), jax.ShapeDtypeStruct((S, D), BF16),
                   jax.ShapeDtypeStruct((8, D), F32), jax.ShapeDtypeStruct((8, D), F32)],
        scratch_shapes=[pltpu.VMEM((tm, D), F32)],
        compiler_params=_params("arbitrary", "arbitrary"),
    )(act, w2, h1, g_final, target)


def _dact(dh2b, w2, a):
    S = a.shape[0]
    tm, tn = _tile(S, 2048), 1024

    def body(d_ref, w_ref, a_ref, o_ref):
        da = _nt(d_ref[...], w_ref[...])
        o_ref[...] = (da * (2.0 * jnp.maximum(a_ref[...].astype(F32), 0.0))).astype(BF16)

    blk = pl.BlockSpec((tm, tn), lambda i, j: (i, j))
    return pl.pallas_call(
        body, name="dact", grid=(S // tm, D_FF // tn),
        in_specs=[pl.BlockSpec((tm, D), lambda i, j: (i, 0)), pl.BlockSpec((tn, D), lambda i, j: (j, 0)), blk],
        out_specs=blk, out_shape=jax.ShapeDtypeStruct((S, D_FF), BF16),
        compiler_params=_params("arbitrary", "arbitrary"),
    )(dh2b, w2, a)


def _dmerge(dh1b, w_out, pg, y3, wg, wf, wm, slabs):
    S = dh1b.shape[0]
    tm = _tile(S, 512)
    n_i, n_x = S // tm, len(slabs)

    def body(*refs):
        d_ref, w_ref, pg_ref, y_ref, wg_ref, wf_ref, wm_ref = refs[:7]
        outs = refs[7 + n_x:14 + n_x]
        scatter = lambda: _AllToAll(refs[7:7 + n_x], refs[14 + n_x:14 + 2 * n_x], refs[14 + 2 * n_x:], False)
        dy_refs, do_refs, dg_ref = outs[0:3], outs[3:6], outs[6]

        @pl.when(pl.program_id(0) == 0)
        def _():
            scatter().start()

        dm = _nt(d_ref[...], w_ref[...])
        for i, wo_ref in enumerate((wg_ref, wf_ref, wm_ref)):
            gt = jax.nn.sigmoid(pg_ref[:, D * i:D * (i + 1)].astype(F32))
            dy = (dm * gt).astype(BF16)
            dy_refs[i][...] = dy
            do_refs[i][...] = _nt(dy, wo_ref[...])
            dg_ref[:, D * i:D * (i + 1)] = (dm * y_ref[i].astype(F32) * (gt * (1.0 - gt))).astype(BF16)

        @pl.when(pl.program_id(0) == n_i - 1)
        def _():
            scatter().wait()

    row = pl.BlockSpec((tm, D), lambda i: (i, 0))
    half = pl.BlockSpec((tm, 512), lambda i: (i, 0))
    w_spec = pl.BlockSpec((512, D), lambda i: (0, 0))
    any_spec = pl.BlockSpec(memory_space=pl.ANY)
    out = pl.pallas_call(
        body, name="dmerge", grid=(n_i,),
        in_specs=[row, pl.BlockSpec((D, D), lambda i: (0, 0)), pl.BlockSpec((tm, G_W), lambda i: (i, 0)),
                  pl.BlockSpec((3, tm, D), lambda i: (0, i, 0)), w_spec, w_spec, w_spec] + [any_spec] * n_x,
        out_specs=[row, row, row, half, half, half, pl.BlockSpec((tm, G_W), lambda i: (i, P_GT // G_W))] + [any_spec] * n_x,
        out_shape=[jax.ShapeDtypeStruct((S, D), BF16)] * 3 + [jax.ShapeDtypeStruct((S, 512), F32)] * 3
        + [jax.ShapeDtypeStruct((S, P_W), BF16)] + [jax.ShapeDtypeStruct(b.shape, b.dtype) for b in slabs],
        scratch_shapes=_exchange_sems(n_x),
        compiler_params=_params("arbitrary"),
    )(dh1b, w_out, pg, y3, wg, wf, wm, *slabs)
    return (*out[:7], out[7:])


def _gla_block_terms(gq_ref, gk_ref, ps_ref, wau_ref, ba_ref, tb):
    gaff = ps_ref[:, 512:640]
    z = _nn(gaff.astype(BF16), wau_ref[...]) + ba_ref[...]
    la = _log_sigmoid(z) * GLA_TAU_INV
    rr = lax.broadcasted_iota(jnp.int32, (tb, tb), 0)
    cc = lax.broadcasted_iota(jnp.int32, (tb, tb), 1)
    same = jnp.right_shift(rr, 6) == jnp.right_shift(cc, 6)
    tri = jnp.where(same & (cc <= rr), 1.0, 0.0).astype(BF16)
    ones = jnp.where(same, 1.0, 0.0).astype(BF16)
    b = _sum01(tri, la)
    bl = _sum01(ones, la)
    e_pos, e_neg, e_last, dec = jnp.exp(b), jnp.exp(-b), jnp.exp(bl - b), jnp.exp(bl)
    q = gq_ref[...].astype(F32) * GLA_SCALE
    k = gk_ref[...].astype(F32)
    return dict(gaff=gaff, z=z, same=same, rr=rr, cc=cc, ones=ones, e_pos=e_pos, e_neg=e_neg, e_last=e_last, dec=dec,
                qp=q * e_pos, qn=q * e_neg, kn=k * e_neg, kp=k * e_pos, kd=k * e_last)


def _head_masked(x, store):
    lane = lax.broadcasted_iota(jnp.int32, x.shape, 1)
    for h in range(GLA_H):
        store[:, h] = jnp.where(jnp.right_shift(lane, 6) == h, x, 0.0).astype(BF16).reshape(-1, CHUNK, GLA_K)


def _lower4():
    t = jnp.bitwise_and(lax.broadcasted_iota(jnp.int32, (GLA_H * CHUNK, CHUNK), 0), CHUNK - 1)
    return t >= lax.broadcasted_iota(jnp.int32, (GLA_H * CHUNK, CHUNK), 1)


def _stack_heads(ref, rows):
    return jnp.concatenate([ref[rows, GLA_DV * h:GLA_DV * (h + 1)] for h in range(GLA_H)], axis=0)


def _gla_fwd(pa, ps, wau, ba, gh):
    S = pa.shape[0]
    tb = _tile(S, 512)
    n_c = tb // CHUNK
    n_b = S // tb

    def body(gq_ref, gk_ref, gv_ref, ps_ref, wau_ref, ba_ref, gh_ref, o_ref, og_ref, sp_ref,
             qpm, qnm, kdm, kn_s, kp_s, dec_s, state):
        @pl.when(pl.program_id(0) == 0)
        def _():
            state[...] = jnp.zeros_like(state)

        t = _gla_block_terms(gq_ref, gk_ref, ps_ref, wau_ref, ba_ref, tb)
        _head_masked(t["qp"], qpm)
        _head_masked(t["qn"], qnm)
        _head_masked(t["kd"], kdm)
        kn_s[...] = t["kn"].astype(BF16)
        kp_s[...] = t["kp"].astype(BF16)
        dec_s[...] = t["dec"]
        lower = _lower4()

        sp = state[...]
        for c in range(n_c):
            rows = slice(c * CHUNK, (c + 1) * CHUNK)
            sp_ref[c] = sp
            qp, qn, kd = (s[c].reshape(GLA_H * CHUNK, GLA_K) for s in (qpm, qnm, kdm))
            attn = jnp.where(lower, _nt(qp, kn_s[rows, :]), _nt(qn, kp_s[rows, :])).astype(BF16)
            inter = _nt(qp, sp.astype(BF16))
            for h in range(GLA_H):
                mine = slice(CHUNK * h, CHUNK * (h + 1))
                cols = slice(GLA_DV * h, GLA_DV * (h + 1))
                o_ref[rows, cols] = _nn(attn[mine], gv_ref[rows, cols]) + inter[mine]
            sp = sp * dec_s[c * CHUNK:c * CHUNK + 1, :] + _tn(_stack_heads(gv_ref, rows), kd)
        state[...] = sp
        for h in range(GLA_H):
            cols = slice(GLA_DV * h, GLA_DV * (h + 1))
            r, xh = _rms(o_ref[:, cols])
            gg = ps_ref[:, cols]
            og_ref[:, cols] = ((xh * gh_ref[:, cols]) * (gg * jax.nn.sigmoid(gg))).astype(BF16)

    return pl.pallas_call(
        body, name="gla_fwd", grid=(n_b,),
        in_specs=[pl.BlockSpec((tb, GLA_K), lambda i: (i, 0)), pl.BlockSpec((tb, GLA_K), lambda i: (i, 1)),
                  pl.BlockSpec((tb, GLA_V), lambda i: (i, 1)), pl.BlockSpec((tb, S_W), lambda i: (i, 0)),
                  pl.BlockSpec((128, GLA_K), lambda i: (0, 0)), pl.BlockSpec((1, GLA_K), lambda i: (0, 0)),
                  pl.BlockSpec((1, GLA_V), lambda i: (0, 0))],
        out_specs=[pl.BlockSpec((tb, GLA_V), lambda i: (i, 0)), pl.BlockSpec((tb, GLA_V), lambda i: (i, 0)),
                   pl.BlockSpec((n_c, GLA_DV, GLA_K), lambda i: (i, 0, 0))],
        out_shape=[jax.ShapeDtypeStruct((S, GLA_V), F32), jax.ShapeDtypeStruct((S, GLA_V), BF16),
                   jax.ShapeDtypeStruct((S // CHUNK, GLA_DV, GLA_K), F32)],
        scratch_shapes=[pltpu.VMEM((n_c, GLA_H, CHUNK, GLA_K), BF16)] * 3
        + [pltpu.VMEM((tb, GLA_K), BF16), pltpu.VMEM((tb, GLA_K), BF16), pltpu.VMEM((tb, GLA_K), F32),
           pltpu.VMEM((GLA_DV, GLA_K), F32)],
        compiler_params=_params("arbitrary"),
    )(pa, pa, pa, ps, wau, ba, gh)


def _gla_bwd(pa, ps, wau, ba, gh, o_gla, d_og, sprev, dgaff_fox, d_proj, slabs):
    S = pa.shape[0]
    tb = _tile(S, 512)
    n_c = tb // CHUNK
    n_b = S // tb
    n_x = len(slabs)
    c_gk, c_gv, c_gg, c_ga, c_end = GLA_K, 2 * GLA_K, 2 * GLA_K + GLA_V, 2 * GLA_K + 2 * GLA_V, 2 * GLA_K + 2 * GLA_V + 128

    def body(*refs):
        gq_ref, gk_ref, gv_ref, ps_ref, wau_ref, ba_ref, gh_ref, o_ref, dog_ref, sp_ref, dfx_ref = refs[:11]
        dp_ref, dwau_ref, dba_ref, dgh_ref = refs[12 + n_x:16 + n_x]
        (qpm, qnm, kdm, kn_s, kp_s, dec_s, do_s, dqp_s, dqn_s, dkn_s, dkp_s, dkd_s, ddec_s,
         dstate) = refs[16 + 2 * n_x:30 + 2 * n_x]
        scatter = lambda: _AllToAll(refs[12:12 + n_x], refs[16 + n_x:16 + 2 * n_x], refs[30 + 2 * n_x:], False)
        first = pl.program_id(0) == 0
        dp_ref[:, c_end:] = jnp.zeros((tb, P_GLA_W - c_end), BF16)

        @pl.when(first)
        def _():
            dstate[...] = jnp.zeros_like(dstate)
            scatter().start()

        t = _gla_block_terms(gq_ref, gk_ref, ps_ref, wau_ref, ba_ref, tb)
        _head_masked(t["qp"], qpm)
        _head_masked(t["qn"], qnm)
        _head_masked(t["kd"], kdm)
        kn_s[...] = t["kn"].astype(BF16)
        kp_s[...] = t["kp"].astype(BF16)
        dec_s[...] = t["dec"]

        dgh_parts = []
        for h in range(GLA_H):
            cols = slice(GLA_DV * h, GLA_DV * (h + 1))
            r, xh = _rms(o_ref[:, cols])
            g = gh_ref[:, cols]
            gg = ps_ref[:, cols]
            sg = jax.nn.sigmoid(gg)
            d_out = dog_ref[:, cols]
            dp_ref[:, c_gg + GLA_DV * h:c_gg + GLA_DV * (h + 1)] = (d_out * (xh * g) * (sg * (1.0 + gg * (1.0 - sg)))).astype(BF16)
            d_on = d_out * (gg * sg)
            dgh_parts.append(_sum8(d_on * xh))
            do_s[:, cols] = _rms_bwd(d_on, g, r, xh).astype(BF16)
        dgh_part = jnp.concatenate(dgh_parts, axis=1)

        lower = _lower4()
        lane = lax.broadcasted_iota(jnp.int32, (CHUNK, GLA_K), 1)

        def own_columns(stacked):
            return sum(jnp.where(jnp.right_shift(lane, 6) == h, stacked[CHUNK * h:CHUNK * (h + 1)], 0.0) for h in range(GLA_H))

        ds_next = dstate[...]
        for c in reversed(range(n_c)):
            rows = slice(c * CHUNK, (c + 1) * CHUNK)
            dsb = ds_next.astype(BF16)
            sp = sp_ref[c]
            knc, kpc = kn_s[rows, :], kp_s[rows, :]
            qp, qn, kd = (s[c].reshape(GLA_H * CHUNK, GLA_K) for s in (qpm, qnm, kdm))
            v4, do4 = _stack_heads(gv_ref, rows), _stack_heads(do_s, rows)
            ddec_s[rows, :] = jnp.broadcast_to(jnp.sum(ds_next * sp, axis=0, keepdims=True), (CHUNK, GLA_K))
            attn = jnp.where(lower, _nt(qp, knc), _nt(qn, kpc)).astype(BF16)
            da = jnp.concatenate([_nt(do4[CHUNK * h:CHUNK * (h + 1)], v4[CHUNK * h:CHUNK * (h + 1)]) for h in range(GLA_H)],
                                 axis=0)
            dac = jnp.where(lower, da, 0.0).astype(BF16)
            daa = jnp.where(lower, 0.0, da).astype(BF16)
            dqp_s[rows, :] = own_columns(_nn(dac, knc) + _nn(do4, sp.astype(BF16)))
            dqn_s[rows, :] = own_columns(_nn(daa, kpc))
            dkd_s[rows, :] = own_columns(_nn(v4, dsb))
            dkn_s[rows, :] = _tn(dac, qp)
            dkp_s[rows, :] = _tn(daa, qn)
            dv_state = _nt(kd, dsb)
            for h in range(GLA_H):
                mine = slice(CHUNK * h, CHUNK * (h + 1))
                dp_ref[rows, c_gv + GLA_DV * h:c_gv + GLA_DV * (h + 1)] = (_tn(attn[mine], do4[mine]) + dv_state[mine]).astype(BF16)
            ds_next = ds_next * dec_s[c * CHUNK:c * CHUNK + 1, :] + _tn(do4, qp)
        dstate[...] = ds_next

        dqp, dqn, dkn, dkp, dkd = dqp_s[...], dqn_s[...], dkn_s[...], dkp_s[...], dkd_s[...]
        dp_ref[:, 0:c_gk] = ((dqp * t["e_pos"] + dqn * t["e_neg"]) * GLA_SCALE).astype(BF16)
        dp_ref[:, c_gk:c_gv] = (dkn * t["e_neg"] + dkp * t["e_pos"] + dkd * t["e_last"]).astype(BF16)
        kd_term = dkd * t["kd"]
        db = dqp * t["qp"] - dqn * t["qn"] - dkn * t["kn"] + dkp * t["kp"] - kd_term
        upper = jnp.where(t["same"] & (t["cc"] >= t["rr"]), 1.0, 0.0).astype(BF16)
        dla = (_sum01(upper, db) + _sum01(t["ones"], kd_term)
               + ddec_s[...] * t["dec"])
        dz = dla * GLA_TAU_INV * jax.nn.sigmoid(-t["z"])
        dzb = dz.astype(BF16)
        dp_ref[:, c_ga:c_end] = (_nt(dzb, wau_ref[...]) + dfx_ref[...]).astype(BF16)
        dwau_part = _tn(t["gaff"].astype(BF16), dzb)
        dba_part = _sum8(dz)

        @pl.when(first)
        def _():
            dwau_ref[...] = dwau_part
            dba_ref[...] = dba_part
            dgh_ref[...] = dgh_part

        @pl.when(jnp.logical_not(first))
        def _():
            dwau_ref[...] += dwau_part
            dba_ref[...] += dba_part
            dgh_ref[...] += dgh_part

        @pl.when(pl.program_id(0) == n_b - 1)
        def _():
            scatter().wait()

    rev = lambda i: (n_b - 1 - i, 0)
    f32k = pltpu.VMEM((tb, GLA_K), F32)
    bf4 = pltpu.VMEM((n_c, GLA_H, CHUNK, GLA_K), BF16)
    any_spec = pl.BlockSpec(memory_space=pl.ANY)
    out = pl.pallas_call(
        body, name="gla_bwd", grid=(n_b,),
        in_specs=[pl.BlockSpec((tb, GLA_K), rev), pl.BlockSpec((tb, GLA_K), lambda i: (n_b - 1 - i, 1)),
                  pl.BlockSpec((tb, GLA_V), lambda i: (n_b - 1 - i, 1)), pl.BlockSpec((tb, S_W), rev),
                  pl.BlockSpec((128, GLA_K), lambda i: (0, 0)), pl.BlockSpec((1, GLA_K), lambda i: (0, 0)),
                  pl.BlockSpec((1, GLA_V), lambda i: (0, 0)), pl.BlockSpec((tb, GLA_V), rev), pl.BlockSpec((tb, GLA_V), rev),
                  pl.BlockSpec((n_c, GLA_DV, GLA_K), lambda i: (n_b - 1 - i, 0, 0)), pl.BlockSpec((tb, 128), rev),
                  any_spec] + [any_spec] * n_x,
        out_specs=[pl.BlockSpec((tb, P_GLA_W), lambda i: (n_b - 1 - i, P_GLA // P_GLA_W)),
                   pl.BlockSpec((128, GLA_K), lambda i: (0, 0)), pl.BlockSpec((8, GLA_K), lambda i: (0, 0)),
                   pl.BlockSpec((8, GLA_V), lambda i: (0, 0))] + [any_spec] * n_x,
        out_shape=[jax.ShapeDtypeStruct((S, P_W), BF16), jax.ShapeDtypeStruct((128, GLA_K), F32),
                   jax.ShapeDtypeStruct((8, GLA_K), F32), jax.ShapeDtypeStruct((8, GLA_V), F32)]
        + [jax.ShapeDtypeStruct(b.shape, b.dtype) for b in slabs],
        input_output_aliases={11: 0},
        scratch_shapes=[bf4, bf4, bf4, pltpu.VMEM((tb, GLA_K), BF16), pltpu.VMEM((tb, GLA_K), BF16), f32k,
                        pltpu.VMEM((tb, GLA_V), BF16), f32k, f32k, f32k, f32k, f32k, f32k, pltpu.VMEM((GLA_DV, GLA_K), F32)]
        + _exchange_sems(n_x),
        compiler_params=_params("arbitrary"),
    )(pa, pa, pa, ps, wau, ba, gh, o_gla, d_og, sprev, dgaff_fox, d_proj, *slabs)
    return out[0], out[1], out[2], out[3], out[4:]


def _split3(x):
    x1 = x.astype(BF16).astype(F32)
    x2 = (x - x1).astype(BF16).astype(F32)
    x3 = (x - x1 - x2).astype(BF16).astype(F32)
    return x1, x2, x3


def _fox_prep(pa, ps, bfg):
    S = pa.shape[0]
    tm = _tile(S, FOX_TK)

    def body(ps_ref, b_ref, fq_ref, fk_ref, fv_ref, q_ref, k_ref, qt_ref, kt_ref, vt_ref, st_ref, carry):
        @pl.when(pl.program_id(0) == 0)
        def _():
            carry[...] = jnp.zeros_like(carry)

        vt = fv_ref[...].astype(F32).T.astype(BF16)
        ones_row = jnp.where(lax.broadcasted_iota(jnp.int32, (FOX_VT - FOX_DH, tm), 0) == 0, 1.0, 0.0).astype(BF16)
        for h in range(FOX_H):
            vt_ref[FOX_VT * h:FOX_VT * h + FOX_DH, :] = vt[FOX_DH * h:FOX_DH * (h + 1), :]
            vt_ref[FOX_VT * h + FOX_DH:FOX_VT * (h + 1), :] = ones_row
        lf = _log_sigmoid(ps_ref[...] + b_ref[...])
        rr = lax.broadcasted_iota(jnp.int32, (tm, tm), 0)
        cc = lax.broadcasted_iota(jnp.int32, (tm, tm), 1)
        tri = jnp.where(cc <= rr, 1.0, 0.0).astype(F32)
        f = jnp.dot(tri, lf, preferred_element_type=F32, precision=HIGHEST) + carry[0:1, :]
        carry[...] = jnp.broadcast_to(f[tm - 1:tm, :], carry.shape)
        f1, f2, f3 = _split3(f)
        lane = lax.broadcasted_iota(jnp.int32, (tm, 128), 1)
        st_row = lax.broadcasted_iota(jnp.int32, (8, 128), 0)
        st_lane = lax.broadcasted_iota(jnp.int32, (8, 128), 1)
        stats = jnp.zeros((8, 128), F32)
        for h in range(FOX_H):
            cols = slice(128 * h, 128 * (h + 1))
            c = FF_LANE + h
            a1, a2, a3 = f1[:, c:c + 1], f2[:, c:c + 1], f3[:, c:c + 1]
            q = fq_ref[:, cols].astype(F32) * FOX_SCALE
            k = fk_ref[:, cols].astype(F32)
            fh = f[:, c:c + 1]
            vals = (jnp.max(jnp.sum(q * q, axis=-1, keepdims=True)), jnp.max(jnp.sum(k * k, axis=-1, keepdims=True)),
                    jnp.max(fh), jnp.min(fh), jnp.min(jnp.sum(q * k, axis=-1, keepdims=True)))
            for n, val in enumerate(vals):
                stats = jnp.where((st_row == h) & (st_lane == n), val, stats)
            for n, a in enumerate((a1, a2, a3)):
                q = jnp.where(lane == AUG + n, a, q)
                k = jnp.where(lane == AUG + 3 + n, -a, k)
            q = jnp.where((lane >= AUG + 3) & (lane < AUG + 6), 1.0, q)
            k = jnp.where((lane >= AUG) & (lane < AUG + 3), 1.0, k)
            q_ref[:, cols] = q.astype(BF16)
            k_ref[:, cols] = k.astype(BF16)
            qt_ref[cols, :] = q.T.astype(BF16)
            kt_ref[cols, :] = k.T.astype(BF16)
        st_ref[0] = stats

    wide = lambda j: pl.BlockSpec((tm, 1024), lambda i: (i, j))
    tall = lambda n: pl.BlockSpec((n, tm), lambda i: (0, i))
    return pl.pallas_call(
        body, name="fox_prep", grid=(S // tm,),
        in_specs=[pl.BlockSpec((tm, 128), lambda i: (i, 4)), pl.BlockSpec((1, 128), lambda i: (0, 0)), wide(1), wide(2),
                  pl.BlockSpec((tm, FOX_W), lambda i: (i, A_FV // FOX_W))],
        out_specs=[wide(0), wide(0), tall(1024), tall(1024), tall(FOX_H * FOX_VT), pl.BlockSpec((1, 8, 128), lambda i: (i, 0, 0))],
        out_shape=[jax.ShapeDtypeStruct((S, 1024), BF16), jax.ShapeDtypeStruct((S, 1024), BF16),
                   jax.ShapeDtypeStruct((1024, S), BF16), jax.ShapeDtypeStruct((1024, S), BF16),
                   jax.ShapeDtypeStruct((FOX_H * FOX_VT, S), BF16), jax.ShapeDtypeStruct((S // tm, 8, 128), F32)],
        scratch_shapes=[pltpu.VMEM((8, 128), F32)],
        compiler_params=_params("arbitrary"),
    )(ps, bfg, pa, pa, pa)


FOX_PRUNE_AT = -90.0


def _fox_live_ranges(stats, n_sub, ratio):
    n_b = stats.shape[0]
    q2, k2, f_max, f_min, own = (stats[:, :, n].T for n in range(5))
    slack = 0.01 * jnp.sqrt(q2 * k2) + 1e-5 * jnp.abs(f_max) + 1.0
    bound = (1.01 * jnp.sqrt(q2[:, :, None] * k2[:, None, :]) + (f_max + slack - own)[:, :, None]
             - (f_min - 1e-5 * jnp.abs(f_min))[:, None, :])
    blocks = jnp.arange(n_b)
    dead = (bound <= FOX_PRUNE_AT) & (blocks[None, :] < blocks[:, None])[None]
    dead_fwd = dead.reshape(FOX_H, n_b // n_sub, n_sub, n_b).all(axis=2)
    first = jnp.sum(jnp.cumprod(dead_fwd.astype(jnp.int32), axis=2), axis=2)
    last_live = n_b - 1 - jnp.sum(jnp.cumprod(dead[:, ::-1, :].astype(jnp.int32), axis=1), axis=1)
    first_wide = blocks // ratio + 1
    narrow_end = jnp.minimum(jnp.minimum(first_wide * ratio, n_b)[None], last_live + 1)
    wide_end = jnp.where(last_live >= (first_wide * ratio)[None], last_live // ratio + 1, first_wide[None])
    return first.astype(jnp.int32), narrow_end.astype(jnp.int32), wide_end.astype(jnp.int32)


def _fox_fwd(qa, ka, vt, first):
    S = qa.shape[0]
    tq = _tile(S, FOX_TQ)
    tk = _tile(tq, FOX_TK)
    n_sub = tq // tk

    def body(first_ref, q_ref, k_ref, vt_ref, o_ref, lse_ref):
        pair, i = pl.program_id(0), pl.program_id(1)
        both = lambda f: tuple(f(hh) for hh in range(2))

        def blk(j, carry, diag, heads=(0, 1)):
            ks = pl.ds(pl.multiple_of(j * tk, tk), tk)
            q0 = 0 if diag is None else diag * tk

            def head(hh):
                if hh not in heads:
                    return carry[hh]
                m, acc = carry[hh]
                mo, ao = m[:, q0:], acc[:, q0:]
                s = _nt(k_ref[ks, 128 * hh:128 * (hh + 1)], q_ref[q0:, 128 * hh:128 * (hh + 1)])
                if diag is not None:
                    live = lax.broadcasted_iota(jnp.int32, s.shape, 1) >= lax.broadcasted_iota(jnp.int32, s.shape, 0)
                    s = jnp.where(live, s, NEG)
                mn = jnp.maximum(mo, jnp.max(s, axis=0, keepdims=True))
                p = jnp.exp((s - mn).astype(BF16))
                an = jnp.exp(mo - mn) * ao + _nn(vt_ref[FOX_VT * hh:FOX_VT * (hh + 1), ks], p)
                if q0:
                    mn, an = (jnp.concatenate([old[:, :q0], new], axis=1) for old, new in ((m, mn), (acc, an)))
                return mn, an

            return both(head)

        one = (jnp.full((1, tq), NEG, F32), jnp.zeros((FOX_VT, tq), F32))
        past = i * n_sub
        f0, f1 = first_ref[2 * pair, i], first_ref[2 * pair + 1, i]
        join = jnp.maximum(f0, f1)
        solo = lambda hh: lambda c: lax.fori_loop(jnp.minimum(f0, f1), join, lambda j, cc: blk(j, cc, None, (hh,)), c)
        carry = lax.cond(f0 < f1, solo(0), solo(1), (one, one))
        n_both = past - join
        carry = lax.fori_loop(0, n_both // 2, lambda jj, c: blk(join + 2 * jj + 1, blk(join + 2 * jj, c, None), None), carry)
        carry = lax.cond(n_both % 2 == 1, lambda c: blk(past - 1, c, None), lambda c: c, carry)
        for d in range(n_sub):
            carry = blk(past + d, carry, d)
        (m0, a0), (m1, a1) = carry
        l0, l1 = a0[FOX_DH:FOX_DH + 1], a1[FOX_DH:FOX_DH + 1]
        o_ref[...] = jnp.concatenate([a0[:FOX_DH] / l0, a1[:FOX_DH] / l1], axis=0).T
        lse_ref[0, 0:1, :] = m0 + jnp.log(l0)
        lse_ref[0, 1:2, :] = m1 + jnp.log(l1)
        lse_ref[0, 2:8, :] = jnp.zeros((6, tq), F32)

    return pl.pallas_call(
        body, name="fox_fwd", grid=(FOX_H // 2, S // tq),
        in_specs=[pl.BlockSpec(memory_space=pltpu.SMEM), pl.BlockSpec((tq, 256), lambda p, i: (i, p)),
                  pl.BlockSpec((S, 256), lambda p, i: (0, p)), pl.BlockSpec((2 * FOX_VT, S), lambda p, i: (p, 0))],
        out_specs=[pl.BlockSpec((tq, 128), lambda p, i: (i, p)), pl.BlockSpec((1, 8, tq), lambda p, i: (p, 0, i))],
        out_shape=[jax.ShapeDtypeStruct((S, FOX_W), F32), jax.ShapeDtypeStruct((FOX_H // 2, 8, S), F32)],
        compiler_params=_params("arbitrary", "arbitrary"),
    )(first, qa, ka, vt)


def _fox_delta(d_o, o):
    S = o.shape[0]
    tm = _tile(S, 512)

    def body(d_ref, o_ref, db_ref, dbt_ref, dl_ref):
        d = d_ref[...]
        db_ref[...] = d.astype(BF16)
        dbt_ref[...] = d.T.astype(BF16)
        prod = d * o_ref[...]
        rr = lax.broadcasted_iota(jnp.int32, (8, 128), 0)
        cc = lax.broadcasted_iota(jnp.int32, (8, 128), 1)
        ind = jnp.where(jnp.right_shift(cc, 6) == rr, 1.0, 0.0).astype(F32)
        for p in range(FOX_H // 2):
            dl_ref[p] = lax.dot_general(ind, prod[:, 128 * p:128 * (p + 1)], (((1,), (1,)), ((), ())),
                                        preferred_element_type=F32, precision=HIGHEST)

    row = pl.BlockSpec((tm, FOX_W), lambda i: (i, 0))
    return pl.pallas_call(
        body, name="fox_delta", grid=(S // tm,),
        in_specs=[row, row],
        out_specs=[row, pl.BlockSpec((FOX_W, tm), lambda i: (0, i)), pl.BlockSpec((FOX_H // 2, 8, tm), lambda i: (0, 0, i))],
        out_shape=[jax.ShapeDtypeStruct((S, FOX_W), BF16), jax.ShapeDtypeStruct((FOX_W, S), BF16),
                   jax.ShapeDtypeStruct((FOX_H // 2, 8, S), F32)],
        compiler_params=_params("arbitrary"),
    )(d_o, o)


def _fox_bwd(qa, qat, ka, kat, pa, dob, dobt, lse, delta, narrow_end, wide_end):
    S = qa.shape[0]
    tk = _tile(S, FOX_TK)
    wide = _tile(S, FOX_BWD_WIDE)
    ratio = wide // tk
    n_wide = S // wide

    def body(ne_ref, we_ref, q_ref, qt_ref, k_ref, kt_ref, v_ref, do_ref, dot_ref, lse_ref, dl_ref, dq_ref, dk_ref, dv_ref):
        h, jb = pl.program_id(0), pl.program_id(1)
        hh = h % 2

        @pl.when(jb == 0)
        def _():
            dq_ref[...] = jnp.zeros_like(dq_ref)

        lane = lax.broadcasted_iota(jnp.int32, (tk, 128), 1)
        vm = jnp.where(jnp.right_shift(lane, 6) == hh, v_ref[...], jnp.zeros((), BF16))
        kb, ktb = k_ref[...], kt_ref[0:FOX_LIVE, :]
        mine = pl.ds(pl.multiple_of(hh * FOX_DH, FOX_DH), FOX_DH)

        def blk(ib, tq, carry, masked):
            dk, dv = carry
            qs = pl.ds(pl.multiple_of(ib * tq, tq), tq)
            p = jnp.exp(_nt(kb, q_ref[qs, :]) - lse_ref[0, pl.ds(hh, 1), qs])
            if masked:
                live = lax.broadcasted_iota(jnp.int32, p.shape, 1) >= lax.broadcasted_iota(jnp.int32, p.shape, 0)
                p = jnp.where(live, p, 0.0)
            ds = (p * (_nt(vm, do_ref[qs, :]) - dl_ref[0, pl.ds(hh, 1), qs])).astype(BF16)
            dq_ref[0:FOX_LIVE, qs] += _nn(ktb, ds)
            return dk + _nt(qt_ref[0:FOX_LIVE, qs], ds), dv + _nt(dot_ref[mine, qs], p.astype(BF16))

        carry = blk(jb, tk, (jnp.zeros((FOX_LIVE, tk), F32), jnp.zeros((FOX_DH, tk), F32)), True)
        first_wide = jb // ratio + 1
        carry = lax.fori_loop(jb + 1, ne_ref[h, jb], lambda ib, c: blk(ib, tk, c, False), carry)
        last_wide = we_ref[h, jb]
        rest = jnp.maximum(last_wide - first_wide, 0)
        carry = lax.fori_loop(0, rest // 2, lambda t, c: blk(first_wide + 2 * t + 1, wide, blk(first_wide + 2 * t, wide, c, False),
                                                             False), carry)
        dk, dv = lax.cond(rest % 2 == 1, lambda c: blk(last_wide - 1, wide, c, False), lambda c: c, carry)
        dk_ref[0:FOX_LIVE, :] = dk
        dk_ref[FOX_LIVE:, :] = jnp.zeros((128 - FOX_LIVE, tk), F32)
        dv_ref[...] = dv

    once = pl.Buffered(1)
    rows = pl.BlockSpec((1, 8, S), lambda h, j: (h // 2, 0, 0))
    return pl.pallas_call(
        body, name="fox_bwd", grid=(FOX_H, S // tk),
        in_specs=[pl.BlockSpec(memory_space=pltpu.SMEM), pl.BlockSpec(memory_space=pltpu.SMEM),
                  pl.BlockSpec((S, 128), lambda h, j: (0, h)), pl.BlockSpec((128, S), lambda h, j: (h, 0)),
                  pl.BlockSpec((tk, 128), lambda h, j: (j, h)), pl.BlockSpec((128, tk), lambda h, j: (h, j)),
                  pl.BlockSpec((tk, 128), lambda h, j: (j, A_FV // 128 + h // 2)),
                  pl.BlockSpec((S, 128), lambda h, j: (0, h // 2)), pl.BlockSpec((128, S), lambda h, j: (h // 2, 0)),
                  rows, rows],
        out_specs=[pl.BlockSpec((128, S), lambda h, j: (h, 0), pipeline_mode=once),
                   pl.BlockSpec((128, tk), lambda h, j: (h, j)), pl.BlockSpec((FOX_DH, tk), lambda h, j: (h, j))],
        out_shape=[jax.ShapeDtypeStruct((1024, S), F32), jax.ShapeDtypeStruct((1024, S), F32),
                   jax.ShapeDtypeStruct((FOX_W, S), F32)],
        compiler_params=_params("arbitrary", "arbitrary"),
    )(narrow_end, wide_end, qa, qat, ka, kat, pa, dob, dobt, lse, delta)


def _fox_post(dq, dk, dv, ps, bfg, d_proj):
    S = dq.shape[1]
    tm = _tile(S, 512)
    n_b = S // tm

    def body(dq_ref, dk_ref, dv_ref, ps_ref, b_ref, _, dp_ref, dff_ref, dbf_ref, carry):
        first = pl.program_id(0) == 0

        @pl.when(first)
        def _():
            carry[...] = jnp.zeros_like(carry)

        low = lax.broadcasted_iota(jnp.int32, (tm, 128), 1) < FOX_DH
        for h in range(FOX_H):
            blk = slice(128 * h, 128 * (h + 1))
            dp_ref[:, blk] = jnp.where(low, dq_ref[blk, :].T * FOX_SCALE, 0.0).astype(BF16)
            dp_ref[:, 1024 + 128 * h:1024 + 128 * (h + 1)] = jnp.where(low, dk_ref[blk, :].T, 0.0).astype(BF16)
        dp_ref[:, 2048:P_FOX_W] = dv_ref[...].T.astype(BF16)
        rr = lax.broadcasted_iota(jnp.int32, (FOX_H, 1024), 0)
        cc = lax.broadcasted_iota(jnp.int32, (FOX_H, 1024), 1)
        sel_k = jnp.where(cc == 128 * rr + AUG + 3, 1.0, 0.0).astype(F32)
        sel_q = jnp.where(cc == 128 * rr + AUG, 1.0, 0.0).astype(F32)
        g = (jnp.dot(sel_k, dk_ref[...], preferred_element_type=F32, precision=HIGHEST)
             - jnp.dot(sel_q, dq_ref[...], preferred_element_type=F32, precision=HIGHEST))
        t_from = lax.broadcasted_iota(jnp.int32, (tm, tm), 0)
        t_to = lax.broadcasted_iota(jnp.int32, (tm, tm), 1)
        later = jnp.where(t_from >= t_to, 1.0, 0.0).astype(F32)
        dlf = jnp.dot(-g, later, preferred_element_type=F32, precision=HIGHEST) + carry[:, 0:1]
        carry[...] = jnp.broadcast_to(dlf[:, 0:1], carry.shape)
        cols = jnp.concatenate([jnp.zeros((FF_LANE, tm), F32), dlf, jnp.zeros((128 - FF_LANE - FOX_H, tm), F32)], axis=0).T
        dff = cols * jax.nn.sigmoid(-(ps_ref[...] + b_ref[...]))
        dff_ref[...] = dff
        part = _sum8(dff)

        @pl.when(first)
        def _():
            dbf_ref[...] = part

        @pl.when(jnp.logical_not(first))
        def _():
            dbf_ref[...] += part

    rev = lambda i: (n_b - 1 - i, 0)
    tall = lambda n: pl.BlockSpec((n, tm), lambda i: (0, n_b - 1 - i))
    return pl.pallas_call(
        body, name="fox_post", grid=(n_b,),
        in_specs=[tall(1024), tall(1024), tall(FOX_W), pl.BlockSpec((tm, 128), lambda i: (n_b - 1 - i, 4)),
                  pl.BlockSpec((1, 128), lambda i: (0, 0)), pl.BlockSpec(memory_space=pl.ANY)],
        out_specs=[pl.BlockSpec((tm, P_FOX_W), lambda i: (n_b - 1 - i, P_FOX // P_FOX_W)), pl.BlockSpec((tm, 128), rev),
                   pl.BlockSpec((8, 128), lambda i: (0, 0))],
        out_shape=[jax.ShapeDtypeStruct((S, P_W), BF16), jax.ShapeDtypeStruct((S, 128), F32),
                   jax.ShapeDtypeStruct((8, 128), F32)],
        input_output_aliases={5: 0},
        scratch_shapes=[pltpu.VMEM((8, 128), F32)],
        compiler_params=_params("arbitrary"),
    )(dq, dk, dv, ps, bfg, d_proj)


def _mem_prep(mem, g_mem, wkv):
    def body(m_ref, g_ref, w_ref, mn_ref, kv_ref):
        r, xh = _rms(m_ref[...])
        mn = (xh * g_ref[...]).astype(BF16)
        mn_ref[...] = mn
        kv_ref[...] = _nn(mn, w_ref[...]).astype(BF16)

    return pl.pallas_call(
        body, name="mem_prep",
        out_shape=[jax.ShapeDtypeStruct((N_MEM, D), BF16), jax.ShapeDtypeStruct((N_MEM, 2 * MEM_W), BF16)],
        compiler_params=pltpu.CompilerParams(vmem_limit_bytes=V7X_VMEM_LIMIT),
    )(mem, g_mem, wkv)


def _mem_softmax(qh, kh):
    s = _nt(qh, kh) * MEM_SCALE
    e = jnp.exp(s - jnp.max(s, axis=-1, keepdims=True))
    return e / jnp.sum(e, axis=-1, keepdims=True)


def _mem_fwd(pa, mkv):
    S = pa.shape[0]
    tm = _tile(S, 1024)

    def body(q_ref, kv_ref, o_ref):
        for h in range(MEM_H):
            cols = slice(MEM_DH * h, MEM_DH * (h + 1))
            p = _mem_softmax(q_ref[:, cols], kv_ref[:, cols])
            o_ref[:, cols] = _nn(p.astype(BF16), kv_ref[:, MEM_W + MEM_DH * h:MEM_W + MEM_DH * (h + 1)])

    return pl.pallas_call(
        body, name="mem_fwd", grid=(S // tm,),
        in_specs=[pl.BlockSpec((tm, MEM_W), lambda i: (i, A_MQ // MEM_W)), pl.BlockSpec((N_MEM, 2 * MEM_W), lambda i: (0, 0))],
        out_specs=pl.BlockSpec((tm, MEM_W), lambda i: (i, 0)),
        out_shape=jax.ShapeDtypeStruct((S, MEM_W), F32),
        compiler_params=_params("arbitrary"),
    )(pa, mkv)


def _mem_bwd(pa, mkv, d_o, d_proj):
    S = pa.shape[0]
    tm = _tile(S, 1024)

    def body(q_ref, kv_ref, do_ref, _, dq_ref, dkv_ref):
        first = pl.program_id(0) == 0
        parts = []
        for h in range(MEM_H):
            cols = slice(MEM_DH * h, MEM_DH * (h + 1))
            vcols = slice(MEM_W + MEM_DH * h, MEM_W + MEM_DH * (h + 1))
            qh, kh = q_ref[:, cols], kv_ref[:, cols]
            p = _mem_softmax(qh, kh)
            dob = do_ref[:, cols].astype(BF16)
            dp = _nt(dob, kv_ref[:, vcols])
            ds = (p * (dp - jnp.sum(p * dp, axis=-1, keepdims=True)) * MEM_SCALE).astype(BF16)
            dq_ref[:, cols] = _nn(ds, kh).astype(BF16)
            parts.append((cols, _tn(ds, qh)))
            parts.append((vcols, _tn(p.astype(BF16), dob)))

        @pl.when(first)
        def _():
            for sl, v in parts:
                dkv_ref[:, sl] = v

        @pl.when(jnp.logical_not(first))
        def _():
            for sl, v in parts:
                dkv_ref[:, sl] += v

    return pl.pallas_call(
        body, name="mem_bwd", grid=(S // tm,),
        in_specs=[pl.BlockSpec((tm, MEM_W), lambda i: (i, A_MQ // MEM_W)), pl.BlockSpec((N_MEM, 2 * MEM_W), lambda i: (0, 0)),
                  pl.BlockSpec((tm, MEM_W), lambda i: (i, 0)), pl.BlockSpec(memory_space=pl.ANY)],
        out_specs=[pl.BlockSpec((tm, MEM_W), lambda i: (i, P_MQ // MEM_W)), pl.BlockSpec((N_MEM, 2 * MEM_W), lambda i: (0, 0))],
        out_shape=[jax.ShapeDtypeStruct((S, P_W), BF16), jax.ShapeDtypeStruct((N_MEM, 2 * MEM_W), F32)],
        input_output_aliases={3: 0},
        compiler_params=_params("arbitrary"),
    )(pa, mkv, d_o, d_proj)


def _mem_prep_bwd(mem, g_mem, mn, wkv, dkv):
    def body(m_ref, g_ref, mn_ref, w_ref, d_ref, dw_ref, dg_ref):
        db = d_ref[...].astype(BF16)
        dw_ref[...] = _tn(mn_ref[...], db).astype(BF16)
        r, xh = _rms(m_ref[...])
        dg_ref[...] = _sum8(_nt(db, w_ref[...]) * xh)

    dw, dg = pl.pallas_call(
        body, name="mem_prep_bwd",
        out_shape=[jax.ShapeDtypeStruct((D, 2 * MEM_W), BF16), jax.ShapeDtypeStruct((8, D), F32)],
        compiler_params=pltpu.CompilerParams(vmem_limit_bytes=V7X_VMEM_LIMIT),
    )(mem, g_mem, mn, wkv, dkv)
    return dw.reshape(N_DEV, D // N_DEV, 2 * MEM_W), dg


def _rearrange_w_in(w):
    def heads128(cols):
        blk = w[:, cols:cols + FOX_W].reshape(D, FOX_H, FOX_DH)
        return jnp.pad(blk, ((0, 0), (0, 0), (0, 128 - FOX_DH))).reshape(D, FOX_H * 128)

    fq, fk, fv, mq, wg = heads128(O_FQ), heads128(O_FK), w[:, O_FV:O_FF], w[:, O_MQ:O_GT], w[:, O_GT:]
    gaff = jnp.concatenate([w[:, O_GA:O_FQ], w[:, O_FF:O_MQ], jnp.zeros((D, 128 - GLA_R - FOX_H), w.dtype)], axis=1)
    wa = jnp.concatenate([w[:, O_GQ:O_GG], fq, fk, fv, mq], axis=1)
    ws = jnp.concatenate([w[:, O_GG:O_GA], gaff], axis=1)
    wp = jnp.concatenate([fq, fk, fv, mq, wg, w[:, O_GQ:O_GG], ws, jnp.zeros((D, P_W - P_GLA - 1024 - S_W), w.dtype)], axis=1)
    return wa, wg, ws, wp


def _restore_w_in_grad(dwp):
    def unheads(off):
        return dwp[:, off:off + FOX_H * 128].reshape(D, FOX_H, 128)[:, :, :FOX_DH].reshape(D, FOX_W)

    g0 = P_GLA + 1024
    return jnp.concatenate([
        dwp[:, P_GLA:g0], dwp[:, g0:g0 + 512], dwp[:, g0 + 512:g0 + 512 + GLA_R], unheads(P_FOX), unheads(P_FOX + 1024),
        dwp[:, P_FOX + 2048:P_FOX + P_FOX_W], dwp[:, g0 + 512 + GLA_R:g0 + 512 + GLA_R + FOX_H], dwp[:, P_MQ:P_GT],
        dwp[:, P_GT:P_GLA]], axis=1)


def _local_step(x, mem, target, p, late_shards):
    S = x.shape[0]
    p = dict(p)
    wa, wg, ws, wp = _rearrange_w_in(p["w_in"])
    wau = jnp.pad(p["w_alpha_up"], ((0, 128 - GLA_R), (0, 0)))
    bfg = jnp.pad(p["b_forget"], ((0, 0), (FF_LANE, 128 - FF_LANE - FOX_H)))
    gh = p["g_gla_head"].reshape(1, GLA_V)

    pa, pg, ps, u, gathered = _proj(x, p["g_mix"], wa, wg, ws, late_shards)
    p.update({n: _unslab(t, ax) for (n, ax), t in zip(BIG[1:], gathered)})
    o_gla, og, sprev = _gla_fwd(pa, ps, wau, p["b_alpha"], gh)
    qa, ka, qat, kat, vt, fox_stats = _fox_prep(pa, ps, bfg)
    fox_tk = _tile(S, FOX_TK)
    fox_first, fox_narrow_end, fox_wide_end = _fox_live_ranges(fox_stats, _tile(S, FOX_TQ) // fox_tk,
                                                               _tile(S, FOX_BWD_WIDE) // fox_tk)
    o_fox, lse = _fox_fwd(qa, ka, vt, fox_first)
    mn, mkv = _mem_prep(mem, p["g_mem"], p["w_mem_kv"])
    o_mem = _mem_fwd(pa, mkv)
    y3, mg = _merge(og, o_fox, o_mem, p["w_gla_o"], p["w_fox_o"], p["w_mem_o"], pg)
    h1, u2 = _out_proj(mg, p["w_out"], x, p["g_ffn"])
    a, act = _ff1(u2, p["w_ff1"])
    dh2, dh2b, loss8, dg_final = _ff2_loss(act, p["w_ff2"], h1, p["g_final"].reshape(1, D), target)

    d_a = _dact(dh2b, p["w_ff2"], a)
    dw_ff2 = _wgrad(act, dh2b, "wgrad_ff2", 0)
    dh1, dh1b, dg_ffn = _nt_rmsbwd(d_a, p["w_ff1"], h1, p["g_ffn"], dh2, "dffn", True)
    dw_ff1 = _wgrad(u2, d_a, "wgrad_ff1", 1)
    dy_g, dy_f, dy_m, do_g, do_f, do_m, d_proj, arrived_ff = _dmerge(dh1b, p["w_out"], pg, y3, p["w_gla_o"], p["w_fox_o"],
                                                                     p["w_mem_o"], [dw_ff1, dw_ff2])
    dw_out = _wgrad(mg, dh1b, "wgrad_out", 0)
    dw_gla_o = _wgrad(og, dy_g, "wgrad_gla_o", 1)
    dw_fox_o = _wgrad(o_fox, dy_f, "wgrad_fox_o", 1)
    dw_mem_o = _wgrad(o_mem, dy_m, "wgrad_mem_o", 1)
    d_proj, d_mkv = _mem_bwd(pa, mkv, do_m, d_proj)
    dw_mem_kv, dg_mem = _mem_prep_bwd(mem, p["g_mem"], mn, p["w_mem_kv"], d_mkv)
    dob, dobt, delta = _fox_delta(do_f, o_fox)
    dq, dk, dv = _fox_bwd(qa, qat, ka, kat, pa, dob, dobt, lse, delta, fox_narrow_end, fox_wide_end)
    d_proj, dgaff_fox, db_forget = _fox_post(dq, dk, dv, ps, bfg, d_proj)
    d_proj, dw_au, db_alpha, dg_gla, arrived = _gla_bwd(pa, ps, wau, p["b_alpha"], gh, o_gla, do_g, sprev, dgaff_fox, d_proj,
                                                        [dw_mem_kv, dw_gla_o, dw_fox_o, dw_mem_o, dw_out])
    dw_in = _slabs(_restore_w_in_grad(_wgrad(u, d_proj, "wgrad_in")), 1)
    dx, dg_mix, arrived_in = _nt_rmsbwd(d_proj, wp, x, p["g_mix"], dh1, "dmix", False, [dw_in])

    big = dict(w_in=arrived_in[0], w_ff1=arrived_ff[0], w_ff2=arrived_ff[1],
               **dict(zip(("w_mem_kv", "w_gla_o", "w_fox_o", "w_mem_o", "w_out"), arrived)))
    small = dict(g_mix=dg_mix, g_mem=dg_mem, g_ffn=dg_ffn, g_final=dg_final, b_alpha=db_alpha, g_gla_head=dg_gla,
                 b_forget=db_forget, w_alpha_up=dw_au, loss=loss8)
    return dx, big, small


BIG = (("w_in", 1), ("w_mem_kv", 0), ("w_gla_o", 1), ("w_fox_o", 1), ("w_mem_o", 1), ("w_out", 0), ("w_ff1", 1), ("w_ff2", 0))


def _peer(d):
    me = lax.axis_index("x") * 4 + lax.axis_index("y") * 2 + lax.axis_index("c")
    t = (me + d) % N_DEV
    return (t // 4, (t // 2) % 2, t % 2), me


def _exchange_sems(n):
    return [pltpu.SemaphoreType.DMA((n, N_DEV - 1)), pltpu.SemaphoreType.DMA((n, N_DEV - 1)), pltpu.SemaphoreType.DMA((n,))]


def _exchange_call(body, blocks, out_shape, name):
    n = len(blocks)
    any_spec = pl.BlockSpec(memory_space=pl.ANY)
    return pl.pallas_call(body, name=name, in_specs=[any_spec] * n, out_specs=[any_spec] * n, out_shape=out_shape,
                          scratch_shapes=_exchange_sems(n))(*blocks)


class _AllToAll:
    def __init__(self, ins, outs, sems, gather):
        send, recv, loc = sems
        n = len(ins)
        _, me = _peer(0)
        src = (lambda k, j: ins[k]) if gather else (lambda k, j: ins[k].at[j])
        self.local = [pltpu.make_async_copy(src(k, me), outs[k].at[me], loc.at[k]) for k in range(n)]
        self.remote = []
        for d in range(1, N_DEV):
            to, _ = _peer(d)
            self.remote += [pltpu.make_async_remote_copy(
                src_ref=src(k, (me + d) % N_DEV), dst_ref=outs[k].at[me], send_sem=send.at[k, d - 1],
                recv_sem=recv.at[k, d - 1], device_id=to, device_id_type=MESH) for k in range(n)]

    def start(self):
        for cp in self.local + self.remote:
            cp.start()

    def wait(self):
        for cp in self.remote:
            cp.wait_send()
        for cp in self.remote:
            cp.wait_recv()
        for cp in self.local:
            cp.wait()


def _gathered_shapes(shards):
    return [jax.ShapeDtypeStruct((N_DEV,) + b.shape, b.dtype) for b in shards]


def _gather_weights(shards):
    n = len(shards)

    def body(*refs):
        ins, outs = refs[:n], refs[n:2 * n]
        send, recv, loc = refs[2 * n:]
        x, y, c = lax.axis_index("x"), lax.axis_index("y"), lax.axis_index("c")
        sibling = (x, y, 1 - c)
        chips = [(1 - x, y), (x, 1 - y), (1 - x, 1 - y)]
        slot = lambda px, py, pc: px * 4 + py * 2 + pc

        def copy(k, s, block, to, src=None):
            rows = outs[k].at[slot(*block)]
            return pltpu.make_async_remote_copy(src_ref=rows if src is None else src, dst_ref=rows, send_sem=send.at[k, s],
                                                recv_sem=recv.at[k, s], device_id=to, device_id_type=MESH)

        me = (x, y, c)
        own = [pltpu.make_async_copy(ins[k], outs[k].at[slot(*me)], loc.at[k]) for k in range(n)]
        first = [copy(k, 0, me, sibling, src=ins[k]) for k in range(n)]
        first += [copy(k, 1 + j, me, (*chip, c), src=ins[k]) for j, chip in enumerate(chips) for k in range(n)]
        for cp in own + first:
            cp.start()
        passed = []
        for j, chip in enumerate(chips):
            for k in range(n):
                copy(k, 1 + j, (*chip, c), me).wait_recv()
                fwd = copy(k, 4 + j, (*chip, c), sibling)
                fwd.start()
                passed.append(fwd)
        for k in range(n):
            copy(k, 0, sibling, me).wait_recv()
        for j, chip in enumerate(chips):
            for k in range(n):
                copy(k, 4 + j, (*chip, 1 - c), me).wait_recv()
        for cp in first + passed:
            cp.wait_send()
        for cp in own:
            cp.wait()

    return _exchange_call(body, shards, [jax.ShapeDtypeStruct((N_DEV,) + b.shape, b.dtype) for b in shards], "gather_weights")


def _adamw_math(g, w, m, v):
    m2 = ADAM_B1 * m + (1.0 - ADAM_B1) * g
    v2 = ADAM_B2 * v + (1.0 - ADAM_B2) * jnp.square(g)
    m_hat = m2 / (1.0 - ADAM_B1 ** ADAM_STEP)
    v_hat = v2 / (1.0 - ADAM_B2 ** ADAM_STEP)
    delta = -ADAM_LR * (m_hat / (jnp.sqrt(v_hat) + ADAM_EPS) + ADAM_WD * w)
    return delta, m2, v2


def _adamw_sum(parts, w, m, v, name):
    R, C = w.shape
    tr = _tile(R, 128)

    def body(p_ref, w_ref, m_ref, v_ref, g_ref, d_ref, m2_ref, v2_ref):
        g = p_ref[0].astype(F32)
        for j in range(1, p_ref.shape[0]):
            g = g + p_ref[j].astype(F32)
        g_ref[...] = g
        d_ref[...], m2_ref[...], v2_ref[...] = _adamw_math(g, w_ref[...], m_ref[...], v_ref[...])

    blk = pl.BlockSpec((tr, C), lambda i: (i, 0))
    return pl.pallas_call(
        body, name=name, grid=(R // tr,),
        in_specs=[pl.BlockSpec((parts.shape[0], tr, C), lambda i: (0, i, 0)), blk, blk, blk],
        out_specs=[blk] * 4, out_shape=[jax.ShapeDtypeStruct((R, C), F32)] * 4,
        compiler_params=_params("arbitrary"),
    )(parts, w, m, v)


SMALL_ROWS = 24


def _pack_small(d):
    mixed = jnp.concatenate([d["b_alpha"].reshape(1, GLA_K), d["g_gla_head"].reshape(1, GLA_V),
                             jnp.pad(d["b_forget"].reshape(1, FOX_H), ((0, 0), (FF_LANE, 128 - FF_LANE - FOX_H))),
                             jnp.zeros((1, 128), F32)], axis=1)
    rows = [d["g_mix"].reshape(1, D), d["g_mem"].reshape(1, D), d["g_ffn"].reshape(1, D), d["g_final"].reshape(1, D), mixed,
            jnp.zeros((3, D), F32), jnp.pad(d["w_alpha_up"].reshape(GLA_R, GLA_K), ((0, 0), (0, D - GLA_K)))]
    return jnp.concatenate(rows, axis=0)


def _unpack_small(t):
    return dict(g_mix=t[0:1], g_mem=t[1:2], g_ffn=t[2:3], g_final=t[3], b_alpha=t[4:5, 0:GLA_K],
                g_gla_head=t[4:5, GLA_K:GLA_K + GLA_V].reshape(1, GLA_H, GLA_DV),
                b_forget=t[4:5, 768 + FF_LANE:768 + FF_LANE + FOX_H], w_alpha_up=t[8:24, 0:GLA_K].reshape(1, GLA_R, GLA_K))


def _small_allreduce(small, w, m, v):
    def body(gm, gme, gf, gfi, ba, gg, bf, wau, ls, w_ref, m_ref, v_ref, g_ref, d_ref, m2_ref, v2_ref, l_ref,
             buf, send, recv):
        _, me = _peer(0)
        buf[me] = jnp.zeros((SMALL_ROWS, D), F32)
        for r, ref in enumerate((gm, gme, gf, gfi)):
            buf[me, r:r + 1, :] = jnp.sum(ref[...], axis=0, keepdims=True)
        buf[me, 4:5, 0:GLA_K] = jnp.sum(ba[...], axis=0, keepdims=True)
        buf[me, 4:5, GLA_K:GLA_K + GLA_V] = jnp.sum(gg[...], axis=0, keepdims=True)
        buf[me, 4:5, 768:896] = jnp.sum(bf[...], axis=0, keepdims=True)
        lrow = jnp.sum(ls[...], axis=0, keepdims=True)
        lsum = lrow[:, 0:128]
        for c in range(1, D // 128):
            lsum = lsum + lrow[:, 128 * c:128 * (c + 1)]
        buf[me, 4:5, 896:1024] = lsum
        buf[me, 8:24, 0:GLA_K] = wau[0:GLA_R, :]
        remote = []
        for d in range(1, N_DEV):
            to, me = _peer(d)
            cp = pltpu.make_async_remote_copy(src_ref=buf.at[me], dst_ref=buf.at[me], send_sem=send.at[d - 1],
                                              recv_sem=recv.at[d - 1], device_id=to, device_id_type=MESH)
            cp.start()
            remote.append(cp)
        for cp in remote:
            cp.wait_send()
        for cp in remote:
            cp.wait_recv()
        g = buf[0]
        for j in range(1, N_DEV):
            g = g + buf[j]
        g_ref[...] = g
        d_ref[...], m2_ref[...], v2_ref[...] = _adamw_math(g, w_ref[...], m_ref[...], v_ref[...])
        l_ref[...] = g[4:5, 896:1024]

    packed = jax.ShapeDtypeStruct((SMALL_ROWS, D), F32)
    return pl.pallas_call(
        body, name="small_allreduce",
        out_shape=[packed, packed, packed, packed, jax.ShapeDtypeStruct((1, 128), F32)],
        scratch_shapes=[pltpu.VMEM((N_DEV, SMALL_ROWS, D), F32), pltpu.SemaphoreType.DMA((N_DEV - 1,)),
                        pltpu.SemaphoreType.DMA((N_DEV - 1,))],
    )(small["g_mix"], small["g_mem"], small["g_ffn"], small["g_final"], small["b_alpha"], small["g_gla_head"],
      small["b_forget"], small["w_alpha_up"], small["loss"], w, m, v)


def _slabs(g, axis):
    R, C = g.shape
    if axis == 0:
        return g.reshape(N_DEV, R // N_DEV, C)
    return g.reshape(R, N_DEV, C // N_DEV).transpose(1, 0, 2)


def _unslab(t, axis):
    n, r, c = t.shape
    if axis == 0:
        return t.reshape(n * r, c)
    return t.transpose(1, 0, 2).reshape(r, n * c)


def kernel(x, mem, g_mix, w_in, w_alpha_up, b_alpha, b_forget, g_gla_head, g_mem, w_mem_kv, w_gla_o, w_fox_o, w_mem_o, w_out, g_ffn, w_ff1, w_ff2, g_final, loss_target, m_g_mix, m_w_in, m_w_alpha_up, m_b_alpha, m_b_forget, m_g_gla_head, m_g_mem, m_w_mem_kv, m_w_gla_o, m_w_fox_o, m_w_mem_o, m_w_out, m_g_ffn, m_w_ff1, m_w_ff2, m_g_final, v_g_mix, v_w_in, v_w_alpha_up, v_b_alpha, v_b_forget, v_g_gla_head, v_g_mem, v_w_mem_kv, v_w_gla_o, v_w_fox_o, v_w_mem_o, v_w_out, v_g_ffn, v_w_ff1, v_w_ff2, v_g_final):
    names = ["g_mix", "w_in", "w_alpha_up", "b_alpha", "b_forget", "g_gla_head", "g_mem", "w_mem_kv", "w_gla_o", "w_fox_o",
             "w_mem_o", "w_out", "g_ffn", "w_ff1", "w_ff2", "g_final"]
    w = dict(g_mix=g_mix, w_in=w_in, w_alpha_up=w_alpha_up, b_alpha=b_alpha, b_forget=b_forget, g_gla_head=g_gla_head,
             g_mem=g_mem, w_mem_kv=w_mem_kv, w_gla_o=w_gla_o, w_fox_o=w_fox_o, w_mem_o=w_mem_o, w_out=w_out, g_ffn=g_ffn,
             w_ff1=w_ff1, w_ff2=w_ff2, g_final=g_final)
    m = dict(g_mix=m_g_mix, w_in=m_w_in, w_alpha_up=m_w_alpha_up, b_alpha=m_b_alpha, b_forget=m_b_forget,
             g_gla_head=m_g_gla_head, g_mem=m_g_mem, w_mem_kv=m_w_mem_kv, w_gla_o=m_w_gla_o, w_fox_o=m_w_fox_o,
             w_mem_o=m_w_mem_o, w_out=m_w_out, g_ffn=m_g_ffn, w_ff1=m_w_ff1, w_ff2=m_w_ff2, g_final=m_g_final)
    v = dict(g_mix=v_g_mix, w_in=v_w_in, w_alpha_up=v_w_alpha_up, b_alpha=v_b_alpha, b_forget=v_b_forget,
             g_gla_head=v_g_gla_head, g_mem=v_g_mem, w_mem_kv=v_w_mem_kv, w_gla_o=v_w_gla_o, w_fox_o=v_w_fox_o,
             w_mem_o=v_w_mem_o, w_out=v_w_out, g_ffn=v_g_ffn, w_ff1=v_w_ff1, w_ff2=v_w_ff2, g_final=v_g_final)
    me = lax.axis_index("x") * 4 + lax.axis_index("y") * 2 + lax.axis_index("c")

    shard = lambda n: w[n][0].astype(BF16)
    w_in_all, w_au_all = _gather_weights([shard("w_in"), shard("w_alpha_up")])
    p = dict(w_in=_unslab(w_in_all, 1), w_alpha_up=_unslab(w_au_all, 1), g_mix=g_mix, b_alpha=b_alpha, b_forget=b_forget,
             g_gla_head=g_gla_head, g_mem=g_mem, g_ffn=g_ffn, g_final=g_final)

    dx, big, small = _local_step(x[0], mem[0], loss_target[0], p, [shard(n) for n, _ in BIG[1:]])

    out_g, out_d, out_m, out_v = {}, {}, {}, {}
    for n, _ in BIG:
        g_, d_, m_, v_ = _adamw_sum(big[n], w[n][0], m[n][0], v[n][0], "adamw_" + n)
        out_g[n], out_d[n], out_m[n], out_v[n] = g_[None], d_[None], m_[None], v_[None]

    full = lambda d: dict(d, w_alpha_up=jnp.zeros((1, GLA_R, GLA_K), F32))
    gs, ds, ms, vs, lrow = _small_allreduce(small, _pack_small(full(w)), _pack_small(full(m)), _pack_small(full(v)))
    g_s, d_s, m_s, v_s = _unpack_small(gs), _unpack_small(ds), _unpack_small(ms), _unpack_small(vs)
    for n in names:
        if n not in out_g and n != "w_alpha_up":
            out_g[n], out_d[n], out_m[n], out_v[n] = g_s[n], d_s[n], m_s[n], v_s[n]
    g_au = lax.dynamic_slice_in_dim(g_s["w_alpha_up"][0], me * (GLA_K // N_DEV), GLA_K // N_DEV, axis=1)
    g_, d_, m_, v_ = _adamw_sum(g_au[None], w_alpha_up[0], m_w_alpha_up[0], v_w_alpha_up[0], "adamw_w_alpha_up")
    out_g["w_alpha_up"], out_d["w_alpha_up"], out_m["w_alpha_up"], out_v["w_alpha_up"] = g_[None], d_[None], m_[None], v_[None]

    loss = jnp.sum(lrow) * (0.5 / D)
    return (loss, dx[None], *[out_g[n] for n in names], *[out_d[n] for n in names], *[out_m[n] for n in names],
            *[out_v[n] for n in names])
```

```python
import jax
import jax.numpy as jnp
from jax import lax
from jax.experimental import pallas as pl
from jax.experimental.pallas import tpu as pltpu

F32, BF16 = jnp.float32, jnp.bfloat16
HIGHEST = lax.Precision.HIGHEST
MESH = pl.DeviceIdType.MESH

N_DEV = 8
D = 1024
EPS = 1e-6
CHUNK = 64
N_MEM = 256
GLA_H, GLA_DK, GLA_DV = 4, 64, 128
GLA_K, GLA_V, GLA_R = 256, 512, 16
FOX_H, FOX_DH, FOX_W = 8, 64, 512
MEM_H, MEM_DH, MEM_W = 4, 128, 512
D_FF = 4096
D_IN = 6680
FOX_SCALE = 0.125
GLA_SCALE = 0.125
MEM_SCALE = MEM_DH ** -0.5
GLA_TAU_INV = 1.0 / 16.0
NEG = -1e30

O_GQ, O_GK, O_GV, O_GG, O_GA, O_FQ, O_FK, O_FV, O_FF, O_MQ, O_GT = 0, 256, 512, 1024, 1536, 1552, 2064, 2576, 3088, 3096, 3608
A_FQ, A_FK, A_FV, A_MQ, A_W = 1024, 2048, 3072, 3584, 4096
S_W = 640
G_W = 3072
P_FOX, P_FOX_W, P_MQ, P_GT, P_GLA, P_GLA_W, P_W = 0, 2560, 2560, 3072, 6144, 2048, 8192
FF_LANE = 16
AUG = 64
FOX_LIVE = 80
FOX_VT = 80

ADAM_LR, ADAM_B1, ADAM_B2, ADAM_EPS, ADAM_WD, ADAM_STEP = 0.001, 0.9, 0.999, 1e-08, 0.01, 10
V7X_VMEM_LIMIT = 54 * 1024 * 1024
FOX_TK = 512
FOX_TQ = 2048
FOX_BWD_WIDE = 1024


def _params(*sem):
    return pltpu.CompilerParams(dimension_semantics=sem, vmem_limit_bytes=V7X_VMEM_LIMIT)


def _nt(a, b):
    return lax.dot_general(a, b, (((1,), (1,)), ((), ())), preferred_element_type=F32)


def _tn(a, b):
    return lax.dot_general(a, b, (((0,), (0,)), ((), ())), preferred_element_type=F32)


def _nn(a, b):
    return jnp.dot(a, b, preferred_element_type=F32)


def _log_sigmoid(z):
    return jnp.minimum(z, 0.0) - jnp.log(1.0 + jnp.exp(-jnp.abs(z)))


def _sum01(m01, x):
    x1 = x.astype(BF16)
    x2 = (x - x1.astype(F32)).astype(BF16)
    x3 = (x - x1.astype(F32) - x2.astype(F32)).astype(BF16)
    return _nn(m01, x1) + _nn(m01, x2) + _nn(m01, x3)


def _sum8(x):
    return x.reshape(x.shape[0] // 8, 8, x.shape[1]).sum(axis=0)


def _rms(xv):
    r = lax.rsqrt(jnp.mean(xv * xv, axis=-1, keepdims=True) + EPS)
    return r, xv * r


def _rms_bwd(du, g, r, xh):
    w = du * g
    return r * (w - xh * jnp.mean(w * xh, axis=-1, keepdims=True))


def _row_chunks(n, size=256):
    return [slice(r, r + min(size, n)) for r in range(0, n, min(size, n))]


def _tile(n, pref):
    t = min(n, pref)
    assert n % t == 0, (n, t)
    return t


def _proj(x, g, wa, wg, ws, shards):
    S = x.shape[0]
    tm, tn = _tile(S, 1024), 1024
    n_a, n_g = A_W // tn, G_W // tn
    n_i, n_j = S // tm, n_a + n_g + 1
    n_x = len(shards)

    def body(*refs):
        x_ref, g_ref, wa_ref, wg_ref, ws_ref = refs[:5]
        pa_ref, pg_ref, ps_ref, u_ref = refs[5 + n_x:9 + n_x]
        u_s = refs[9 + 2 * n_x]
        gather = lambda: _AllToAll(refs[5:5 + n_x], refs[9 + n_x:9 + 2 * n_x], refs[10 + 2 * n_x:], True)
        i, j = pl.program_id(0), pl.program_id(1)

        @pl.when((i == 0) & (j == 0))
        def _():
            gather().start()

        @pl.when(j == 0)
        def _():
            r, xh = _rms(x_ref[...])
            u_s[...] = (xh * g_ref[...]).astype(BF16)
            u_ref[...] = u_s[...]

        @pl.when(j < n_a)
        def _():
            pa_ref[...] = _nn(u_s[...], wa_ref[...]).astype(BF16)

        @pl.when((j >= n_a) & (j < n_a + n_g))
        def _():
            pg_ref[...] = _nn(u_s[...], wg_ref[...]).astype(BF16)

        @pl.when(j == n_a + n_g)
        def _():
            ps_ref[...] = _nn(u_s[...], ws_ref[...])

        @pl.when((i == n_i - 1) & (j == n_j - 1))
        def _():
            gather().wait()

    in_a = lambda j: jnp.minimum(j, n_a - 1)
    in_g = lambda j: jnp.clip(j - n_a, 0, n_g - 1)
    row = pl.BlockSpec((tm, D), lambda i, j: (i, 0))
    any_spec = pl.BlockSpec(memory_space=pl.ANY)
    out = pl.pallas_call(
        body, name="proj", grid=(n_i, n_j),
        in_specs=[row, pl.BlockSpec((1, D), lambda i, j: (0, 0)), pl.BlockSpec((D, tn), lambda i, j: (0, in_a(j))),
                  pl.BlockSpec((D, tn), lambda i, j: (0, in_g(j))),
                  pl.BlockSpec((D, S_W), lambda i, j: (0, 0), pipeline_mode=pl.Buffered(1))] + [any_spec] * n_x,
        out_specs=[pl.BlockSpec((tm, tn), lambda i, j: (i, in_a(j))), pl.BlockSpec((tm, tn), lambda i, j: (i, in_g(j))),
                   pl.BlockSpec((tm, S_W), lambda i, j: (i, 0)), row] + [any_spec] * n_x,
        out_shape=[jax.ShapeDtypeStruct((S, A_W), BF16), jax.ShapeDtypeStruct((S, G_W), BF16),
                   jax.ShapeDtypeStruct((S, S_W), F32), jax.ShapeDtypeStruct((S, D), BF16)] + _gathered_shapes(shards),
        scratch_shapes=[pltpu.VMEM((tm, D), BF16)] + _exchange_sems(n_x),
        compiler_params=_params("arbitrary", "arbitrary"),
    )(x, g, wa, wg, ws, *shards)
    return out[0], out[1], out[2], out[3], out[4:]


def _wgrad(a, b, name, slab_axis=None):
    S, Ka = a.shape
    N = b.shape[1]
    tka, tn, ts = _tile(Ka, 1024), _tile(N, 1024), _tile(S, 4096)
    n_s = S // ts
    per = N // N_DEV
    slabs_per_step = tn // per

    def body(a_ref, b_ref, o_ref, acc):
        s = pl.program_id(2)

        @pl.when(s == 0)
        def _():
            acc[...] = jnp.zeros_like(acc)

        acc[...] += _tn(a_ref[...].astype(BF16), b_ref[...].astype(BF16))

        @pl.when(s == n_s - 1)
        def _():
            if slab_axis == 1:
                for q in range(slabs_per_step):
                    o_ref[q] = acc[:, per * q:per * (q + 1)].astype(BF16)
            else:
                o_ref[...] = acc[...].astype(o_ref.dtype)

    if slab_axis == 1:
        out_spec = pl.BlockSpec((slabs_per_step, tka, per), lambda i, j, s: (j, i, 0))
        out_shape = jax.ShapeDtypeStruct((N_DEV, Ka, per), BF16)
    else:
        out_spec = pl.BlockSpec((tka, tn), lambda i, j, s: (i, j))
        out_shape = jax.ShapeDtypeStruct((Ka, N), BF16)
    out = pl.pallas_call(
        body, name=name, grid=(Ka // tka, N // tn, n_s),
        in_specs=[pl.BlockSpec((ts, tka), lambda i, j, s: (s, i)), pl.BlockSpec((ts, tn), lambda i, j, s: (s, j))],
        out_specs=out_spec, out_shape=out_shape,
        scratch_shapes=[pltpu.VMEM((tka, tn), F32)],
        compiler_params=_params("arbitrary", "arbitrary", "arbitrary"),
    )(a, b)
    return out.reshape(N_DEV, Ka // N_DEV, N) if slab_axis == 0 else out


def _nt_rmsbwd(a, w, xin, g, dres, name, emit_bf16, slabs=()):
    S, K = a.shape
    tm, tk = _tile(S, 1024), _tile(K, 1024 if emit_bf16 else 2048)
    n_i, n_k = S // tm, K // tk
    n_x, n_o = len(slabs), 3 if emit_bf16 else 2

    def body(*refs):
        a_ref, w_ref, x_ref, g_ref, r_ref = refs[:5]
        o_ref = refs[5 + n_x]
        rest = refs[6 + n_x:5 + n_x + n_o] + (refs[5 + 2 * n_x + n_o],)
        dg_ref, acc = rest[-2], rest[-1]
        scatter = lambda: _AllToAll(refs[5:5 + n_x], refs[5 + n_x + n_o:5 + 2 * n_x + n_o], refs[6 + 2 * n_x + n_o:], False)
        i, k = pl.program_id(0), pl.program_id(1)

        if n_x:
            @pl.when((i == 0) & (k == 0))
            def _():
                scatter().start()

        @pl.when(k == 0)
        def _():
            acc[...] = jnp.zeros_like(acc)

        acc[...] += _nt(a_ref[...], w_ref[...])

        @pl.when(k == n_k - 1)
        def _():
            @pl.when(i == 0)
            def _():
                dg_ref[...] = jnp.zeros_like(dg_ref)

            for rows in _row_chunks(tm):
                du = acc[rows, :]
                r, xh = _rms(x_ref[rows, :])
                out = r_ref[rows, :] + _rms_bwd(du, g_ref[...], r, xh)
                o_ref[rows, :] = out
                if emit_bf16:
                    rest[0][rows, :] = out.astype(BF16)
                dg_ref[...] += _sum8(du * xh)

        if n_x:
            @pl.when((i == n_i - 1) & (k == n_k - 1))
            def _():
                scatter().wait()

    row = pl.BlockSpec((tm, D), lambda i, k: (i, 0))
    any_spec = pl.BlockSpec(memory_space=pl.ANY)
    out_shape = [jax.ShapeDtypeStruct((S, D), F32)]
    out_specs = [row]
    if emit_bf16:
        out_shape.append(jax.ShapeDtypeStruct((S, D), BF16))
        out_specs.append(row)
    out_shape.append(jax.ShapeDtypeStruct((8, D), F32))
    out_specs.append(pl.BlockSpec((8, D), lambda i, k: (0, 0)))
    out = pl.pallas_call(
        body, name=name, grid=(n_i, n_k),
        in_specs=[pl.BlockSpec((tm, tk), lambda i, k: (i, k)), pl.BlockSpec((D, tk), lambda i, k: (0, k)),
                  row, pl.BlockSpec((1, D), lambda i, k: (0, 0)), row] + [any_spec] * n_x,
        out_specs=out_specs + [any_spec] * n_x,
        out_shape=out_shape + [jax.ShapeDtypeStruct(b.shape, b.dtype) for b in slabs],
        scratch_shapes=[pltpu.VMEM((tm, D), F32)] + (_exchange_sems(n_x) if n_x else []),
        compiler_params=_params("arbitrary", "arbitrary"),
    )(a, w, xin, g, dres, *slabs)
    return (*out[:n_o], out[n_o:]) if n_x else out


def _merge(og, ofox, omem, wg, wf, wm, pg):
    S = og.shape[0]
    tm = _tile(S, 512)

    def body(og_ref, of_ref, om_ref, wg_ref, wf_ref, wm_ref, pg_ref, y_ref, mg_ref):
        tot = None
        for i, (o_ref, w_ref) in enumerate(((og_ref, wg_ref), (of_ref, wf_ref), (om_ref, wm_ref))):
            y = _nn(o_ref[...].astype(BF16), w_ref[...])
            y_ref[i] = y.astype(BF16)
            t = jax.nn.sigmoid(pg_ref[:, D * i:D * (i + 1)].astype(F32)) * y
            tot = t if tot is None else tot + t
        mg_ref[...] = tot.astype(BF16)

    o_spec = pl.BlockSpec((tm, 512), lambda i: (i, 0))
    w_spec = pl.BlockSpec((512, D), lambda i: (0, 0))
    return pl.pallas_call(
        body, name="merge", grid=(S // tm,),
        in_specs=[o_spec, o_spec, o_spec, w_spec, w_spec, w_spec, pl.BlockSpec((tm, G_W), lambda i: (i, 0))],
        out_specs=[pl.BlockSpec((3, tm, D), lambda i: (0, i, 0)), pl.BlockSpec((tm, D), lambda i: (i, 0))],
        out_shape=[jax.ShapeDtypeStruct((3, S, D), BF16), jax.ShapeDtypeStruct((S, D), BF16)],
        compiler_params=_params("arbitrary"),
    )(og, ofox, omem, wg, wf, wm, pg)


def _out_proj(mg, w_out, x, g_ffn):
    S = x.shape[0]
    tm = _tile(S, 1024)

    def body(mg_ref, w_ref, x_ref, g_ref, h_ref, u_ref):
        h = x_ref[...] + _nn(mg_ref[...], w_ref[...])
        h_ref[...] = h
        r, xh = _rms(h)
        u_ref[...] = (xh * g_ref[...]).astype(BF16)

    row = pl.BlockSpec((tm, D), lambda i: (i, 0))
    return pl.pallas_call(
        body, name="out_proj", grid=(S // tm,),
        in_specs=[row, pl.BlockSpec((D, D), lambda i: (0, 0)), row, pl.BlockSpec((1, D), lambda i: (0, 0))],
        out_specs=[row, row],
        out_shape=[jax.ShapeDtypeStruct((S, D), F32), jax.ShapeDtypeStruct((S, D), BF16)],
        compiler_params=_params("arbitrary"),
    )(mg, w_out, x, g_ffn)


def _ff1(u2, w1):
    S = u2.shape[0]
    tm, tn = _tile(S, 2048), 1024

    def body(u_ref, w_ref, a_ref, act_ref):
        a = _nn(u_ref[...], w_ref[...])
        a_ref[...] = a.astype(BF16)
        act_ref[...] = jnp.square(jnp.maximum(a, 0.0)).astype(BF16)

    blk = pl.BlockSpec((tm, tn), lambda i, j: (i, j))
    return pl.pallas_call(
        body, name="ff1", grid=(S // tm, D_FF // tn),
        in_specs=[pl.BlockSpec((tm, D), lambda i, j: (i, 0)), pl.BlockSpec((D, tn), lambda i, j: (0, j))],
        out_specs=[blk, blk],
        out_shape=[jax.ShapeDtypeStruct((S, D_FF), BF16), jax.ShapeDtypeStruct((S, D_FF), BF16)],
        compiler_params=_params("arbitrary", "arbitrary"),
    )(u2, w1)


def _ff2_loss(act, w2, h1, g_final, target):
    S = act.shape[0]
    tm, tk = _tile(S, 1024), 1024
    n_k = D_FF // tk

    def body(a_ref, w_ref, h_ref, g_ref, t_ref, d_ref, db_ref, ls_ref, dg_ref, acc):
        i, k = pl.program_id(0), pl.program_id(1)

        @pl.when(k == 0)
        def _():
            acc[...] = jnp.zeros_like(acc)

        acc[...] += _nn(a_ref[...], w_ref[...])

        @pl.when(k == n_k - 1)
        def _():
            @pl.when(i == 0)
            def _():
                ls_ref[...] = jnp.zeros_like(ls_ref)
                dg_ref[...] = jnp.zeros_like(dg_ref)

            gf = g_ref[...]
            for rows in _row_chunks(tm):
                r, xh = _rms(h_ref[rows, :] + acc[rows, :])
                err = xh * gf - t_ref[rows, :]
                dy = err * (1.0 / D)
                dh = _rms_bwd(dy, gf, r, xh)
                d_ref[rows, :] = dh
                db_ref[rows, :] = dh.astype(BF16)
                ls_ref[...] += _sum8(err * err)
                dg_ref[...] += _sum8(dy * xh)

    row = pl.BlockSpec((tm, D), lambda i, k: (i, 0))
    part = pl.BlockSpec((8, D), lambda i, k: (0, 0))
    return pl.pallas_call(
        body, name="ff2_loss", grid=(S // tm, n_k),
        in_specs=[pl.BlockSpec((tm, tk), lambda i, k: (i, k)), pl.BlockSpec((tk, D), lambda i, k: (k, 0)),
                  row, pl.BlockSpec((1, D), lambda i, k: (0, 0)), row],
        out_specs=[row, row, part, part],
        out_shape=[jax.ShapeDtypeStruct((S, D), F32), jax.ShapeDtypeStruct((S, D), BF16),
                   jax.ShapeDtypeStruct((8, D), F32), jax.ShapeDtypeStruct((8, D), F32)],
        scratch_shapes=[pltpu.VMEM((tm, D), F32)],
        compiler_params=_params("arbitrary", "arbitrary"),
    )(act, w2, h1, g_final, target)


def _dact(dh2b, w2, a):
    S = a.shape[0]
    tm, tn = _tile(S, 2048), 1024

    def body(d_ref, w_ref, a_ref, o_ref):
        da = _nt(d_ref[...], w_ref[...])
        o_ref[...] = (da * (2.0 * jnp.maximum(a_ref[...].astype(F32), 0.0))).astype(BF16)

    blk = pl.BlockSpec((tm, tn), lambda i, j: (i, j))
    return pl.pallas_call(
        body, name="dact", grid=(S // tm, D_FF // tn),
        in_specs=[pl.BlockSpec((tm, D), lambda i, j: (i, 0)), pl.BlockSpec((tn, D), lambda i, j: (j, 0)), blk],
        out_specs=blk, out_shape=jax.ShapeDtypeStruct((S, D_FF), BF16),
        compiler_params=_params("arbitrary", "arbitrary"),
    )(dh2b, w2, a)


def _dmerge(dh1b, w_out, pg, y3, wg, wf, wm, slabs):
    S = dh1b.shape[0]
    tm = _tile(S, 512)
    n_i, n_x = S // tm, len(slabs)

    def body(*refs):
        d_ref, w_ref, pg_ref, y_ref, wg_ref, wf_ref, wm_ref = refs[:7]
        outs = refs[7 + n_x:14 + n_x]
        scatter = lambda: _AllToAll(refs[7:7 + n_x], refs[14 + n_x:14 + 2 * n_x], refs[14 + 2 * n_x:], False)
        dy_refs, do_refs, dg_ref = outs[0:3], outs[3:6], outs[6]

        @pl.when(pl.program_id(0) == 0)
        def _():
            scatter().start()

        dm = _nt(d_ref[...], w_ref[...])
        for i, wo_ref in enumerate((wg_ref, wf_ref, wm_ref)):
            gt = jax.nn.sigmoid(pg_ref[:, D * i:D * (i + 1)].astype(F32))
            dy = (dm * gt).astype(BF16)
            dy_refs[i][...] = dy
            do_refs[i][...] = _nt(dy, wo_ref[...])
            dg_ref[:, D * i:D * (i + 1)] = (dm * y_ref[i].astype(F32) * (gt * (1.0 - gt))).astype(BF16)

        @pl.when(pl.program_id(0) == n_i - 1)
        def _():
            scatter().wait()

    row = pl.BlockSpec((tm, D), lambda i: (i, 0))
    half = pl.BlockSpec((tm, 512), lambda i: (i, 0))
    w_spec = pl.BlockSpec((512, D), lambda i: (0, 0))
    any_spec = pl.BlockSpec(memory_space=pl.ANY)
    out = pl.pallas_call(
        body, name="dmerge", grid=(n_i,),
        in_specs=[row, pl.BlockSpec((D, D), lambda i: (0, 0)), pl.BlockSpec((tm, G_W), lambda i: (i, 0)),
                  pl.BlockSpec((3, tm, D), lambda i: (0, i, 0)), w_spec, w_spec, w_spec] + [any_spec] * n_x,
        out_specs=[row, row, row, half, half, half, pl.BlockSpec((tm, G_W), lambda i: (i, P_GT // G_W))] + [any_spec] * n_x,
        out_shape=[jax.ShapeDtypeStruct((S, D), BF16)] * 3 + [jax.ShapeDtypeStruct((S, 512), F32)] * 3
        + [jax.ShapeDtypeStruct((S, P_W), BF16)] + [jax.ShapeDtypeStruct(b.shape, b.dtype) for b in slabs],
        scratch_shapes=_exchange_sems(n_x),
        compiler_params=_params("arbitrary"),
    )(dh1b, w_out, pg, y3, wg, wf, wm, *slabs)
    return (*out[:7], out[7:])


def _gla_block_terms(gq_ref, gk_ref, ps_ref, wau_ref, ba_ref, tb):
    gaff = ps_ref[:, 512:640]
    z = _nn(gaff.astype(BF16), wau_ref[...]) + ba_ref[...]
    la = _log_sigmoid(z) * GLA_TAU_INV
    rr = lax.broadcasted_iota(jnp.int32, (tb, tb), 0)
    cc = lax.broadcasted_iota(jnp.int32, (tb, tb), 1)
    same = jnp.right_shift(rr, 6) == jnp.right_shift(cc, 6)
    tri = jnp.where(same & (cc <= rr), 1.0, 0.0).astype(BF16)
    ones = jnp.where(same, 1.0, 0.0).astype(BF16)
    b = _sum01(tri, la)
    bl = _sum01(ones, la)
    e_pos, e_neg, e_last, dec = jnp.exp(b), jnp.exp(-b), jnp.exp(bl - b), jnp.exp(bl)
    q = gq_ref[...].astype(F32) * GLA_SCALE
    k = gk_ref[...].astype(F32)
    return dict(gaff=gaff, z=z, same=same, rr=rr, cc=cc, ones=ones, e_pos=e_pos, e_neg=e_neg, e_last=e_last, dec=dec,
                qp=q * e_pos, qn=q * e_neg, kn=k * e_neg, kp=k * e_pos, kd=k * e_last)


def _head_masked(x, store):
    lane = lax.broadcasted_iota(jnp.int32, x.shape, 1)
    for h in range(GLA_H):
        store[:, h] = jnp.where(jnp.right_shift(lane, 6) == h, x, 0.0).astype(BF16).reshape(-1, CHUNK, GLA_K)


def _lower4():
    t = jnp.bitwise_and(lax.broadcasted_iota(jnp.int32, (GLA_H * CHUNK, CHUNK), 0), CHUNK - 1)
    return t >= lax.broadcasted_iota(jnp.int32, (GLA_H * CHUNK, CHUNK), 1)


def _stack_heads(ref, rows):
    return jnp.concatenate([ref[rows, GLA_DV * h:GLA_DV * (h + 1)] for h in range(GLA_H)], axis=0)


def _gla_fwd(pa, ps, wau, ba, gh):
    S = pa.shape[0]
    tb = _tile(S, 512)
    n_c = tb // CHUNK
    n_b = S // tb

    def body(gq_ref, gk_ref, gv_ref, ps_ref, wau_ref, ba_ref, gh_ref, o_ref, og_ref, sp_ref,
             qpm, qnm, kdm, kn_s, kp_s, dec_s, state):
        @pl.when(pl.program_id(0) == 0)
        def _():
            state[...] = jnp.zeros_like(state)

        t = _gla_block_terms(gq_ref, gk_ref, ps_ref, wau_ref, ba_ref, tb)
        _head_masked(t["qp"], qpm)
        _head_masked(t["qn"], qnm)
        _head_masked(t["kd"], kdm)
        kn_s[...] = t["kn"].astype(BF16)
        kp_s[...] = t["kp"].astype(BF16)
        dec_s[...] = t["dec"]
        lower = _lower4()

        sp = state[...]
        for c in range(n_c):
            rows = slice(c * CHUNK, (c + 1) * CHUNK)
            sp_ref[c] = sp
            qp, qn, kd = (s[c].reshape(GLA_H * CHUNK, GLA_K) for s in (qpm, qnm, kdm))
            attn = jnp.where(lower, _nt(qp, kn_s[rows, :]), _nt(qn, kp_s[rows, :])).astype(BF16)
            inter = _nt(qp, sp.astype(BF16))
            for h in range(GLA_H):
                mine = slice(CHUNK * h, CHUNK * (h + 1))
                cols = slice(GLA_DV * h, GLA_DV * (h + 1))
                o_ref[rows, cols] = _nn(attn[mine], gv_ref[rows, cols]) + inter[mine]
            sp = sp * dec_s[c * CHUNK:c * CHUNK + 1, :] + _tn(_stack_heads(gv_ref, rows), kd)
        state[...] = sp
        for h in range(GLA_H):
            cols = slice(GLA_DV * h, GLA_DV * (h + 1))
            r, xh = _rms(o_ref[:, cols])
            gg = ps_ref[:, cols]
            og_ref[:, cols] = ((xh * gh_ref[:, cols]) * (gg * jax.nn.sigmoid(gg))).astype(BF16)

    return pl.pallas_call(
        body, name="gla_fwd", grid=(n_b,),
        in_specs=[pl.BlockSpec((tb, GLA_K), lambda i: (i, 0)), pl.BlockSpec((tb, GLA_K), lambda i: (i, 1)),
                  pl.BlockSpec((tb, GLA_V), lambda i: (i, 1)), pl.BlockSpec((tb, S_W), lambda i: (i, 0)),
                  pl.BlockSpec((128, GLA_K), lambda i: (0, 0)), pl.BlockSpec((1, GLA_K), lambda i: (0, 0)),
                  pl.BlockSpec((1, GLA_V), lambda i: (0, 0))],
        out_specs=[pl.BlockSpec((tb, GLA_V), lambda i: (i, 0)), pl.BlockSpec((tb, GLA_V), lambda i: (i, 0)),
                   pl.BlockSpec((n_c, GLA_DV, GLA_K), lambda i: (i, 0, 0))],
        out_shape=[jax.ShapeDtypeStruct((S, GLA_V), F32), jax.ShapeDtypeStruct((S, GLA_V), BF16),
                   jax.ShapeDtypeStruct((S // CHUNK, GLA_DV, GLA_K), F32)],
        scratch_shapes=[pltpu.VMEM((n_c, GLA_H, CHUNK, GLA_K), BF16)] * 3
        + [pltpu.VMEM((tb, GLA_K), BF16), pltpu.VMEM((tb, GLA_K), BF16), pltpu.VMEM((tb, GLA_K), F32),
           pltpu.VMEM((GLA_DV, GLA_K), F32)],
        compiler_params=_params("arbitrary"),
    )(pa, pa, pa, ps, wau, ba, gh)


def _gla_bwd(pa, ps, wau, ba, gh, o_gla, d_og, sprev, dgaff_fox, d_proj, slabs):
    S = pa.shape[0]
    tb = _tile(S, 512)
    n_c = tb // CHUNK
    n_b = S // tb
    n_x = len(slabs)
    c_gk, c_gv, c_gg, c_ga, c_end = GLA_K, 2 * GLA_K, 2 * GLA_K + GLA_V, 2 * GLA_K + 2 * GLA_V, 2 * GLA_K + 2 * GLA_V + 128

    def body(*refs):
        gq_ref, gk_ref, gv_ref, ps_ref, wau_ref, ba_ref, gh_ref, o_ref, dog_ref, sp_ref, dfx_ref = refs[:11]
        dp_ref, dwau_ref, dba_ref, dgh_ref = refs[12 + n_x:16 + n_x]
        (qpm, qnm, kdm, kn_s, kp_s, dec_s, do_s, dqp_s, dqn_s, dkn_s, dkp_s, dkd_s, ddec_s,
         dstate) = refs[16 + 2 * n_x:30 + 2 * n_x]
        scatter = lambda: _AllToAll(refs[12:12 + n_x], refs[16 + n_x:16 + 2 * n_x], refs[30 + 2 * n_x:], False)
        first = pl.program_id(0) == 0
        dp_ref[:, c_end:] = jnp.zeros((tb, P_GLA_W - c_end), BF16)

        @pl.when(first)
        def _():
            dstate[...] = jnp.zeros_like(dstate)
            scatter().start()

        t = _gla_block_terms(gq_ref, gk_ref, ps_ref, wau_ref, ba_ref, tb)
        _head_masked(t["qp"], qpm)
        _head_masked(t["qn"], qnm)
        _head_masked(t["kd"], kdm)
        kn_s[...] = t["kn"].astype(BF16)
        kp_s[...] = t["kp"].astype(BF16)
        dec_s[...] = t["dec"]

        dgh_parts = []
        for h in range(GLA_H):
            cols = slice(GLA_DV * h, GLA_DV * (h + 1))
            r, xh = _rms(o_ref[:, cols])
            g = gh_ref[:, cols]
            gg = ps_ref[:, cols]
            sg = jax.nn.sigmoid(gg)
            d_out = dog_ref[:, cols]
            dp_ref[:, c_gg + GLA_DV * h:c_gg + GLA_DV * (h + 1)] = (d_out * (xh * g) * (sg * (1.0 + gg * (1.0 - sg)))).astype(BF16)
            d_on = d_out * (gg * sg)
            dgh_parts.append(_sum8(d_on * xh))
            do_s[:, cols] = _rms_bwd(d_on, g, r, xh).astype(BF16)
        dgh_part = jnp.concatenate(dgh_parts, axis=1)

        lower = _lower4()
        lane = lax.broadcasted_iota(jnp.int32, (CHUNK, GLA_K), 1)

        def own_columns(stacked):
            return sum(jnp.where(jnp.right_shift(lane, 6) == h, stacked[CHUNK * h:CHUNK * (h + 1)], 0.0) for h in range(GLA_H))

        ds_next = dstate[...]
        for c in reversed(range(n_c)):
            rows = slice(c * CHUNK, (c + 1) * CHUNK)
            dsb = ds_next.astype(BF16)
            sp = sp_ref[c]
            knc, kpc = kn_s[rows, :], kp_s[rows, :]
            qp, qn, kd = (s[c].reshape(GLA_H * CHUNK, GLA_K) for s in (qpm, qnm, kdm))
            v4, do4 = _stack_heads(gv_ref, rows), _stack_heads(do_s, rows)
            ddec_s[rows, :] = jnp.broadcast_to(jnp.sum(ds_next * sp, axis=0, keepdims=True), (CHUNK, GLA_K))
            attn = jnp.where(lower, _nt(qp, knc), _nt(qn, kpc)).astype(BF16)
            da = jnp.concatenate([_nt(do4[CHUNK * h:CHUNK * (h + 1)], v4[CHUNK * h:CHUNK * (h + 1)]) for h in range(GLA_H)],
                                 axis=0)
            dac = jnp.where(lower, da, 0.0).astype(BF16)
            daa = jnp.where(lower, 0.0, da).astype(BF16)
            dqp_s[rows, :] = own_columns(_nn(dac, knc) + _nn(do4, sp.astype(BF16)))
            dqn_s[rows, :] = own_columns(_nn(daa, kpc))
            dkd_s[rows, :] = own_columns(_nn(v4, dsb))
            dkn_s[rows, :] = _tn(dac, qp)
            dkp_s[rows, :] = _tn(daa, qn)
            dv_state = _nt(kd, dsb)
            for h in range(GLA_H):
                mine = slice(CHUNK * h, CHUNK * (h + 1))
                dp_ref[rows, c_gv + GLA_DV * h:c_gv + GLA_DV * (h + 1)] = (_tn(attn[mine], do4[mine]) + dv_state[mine]).astype(BF16)
            ds_next = ds_next * dec_s[c * CHUNK:c * CHUNK + 1, :] + _tn(do4, qp)
        dstate[...] = ds_next

        dqp, dqn, dkn, dkp, dkd = dqp_s[...], dqn_s[...], dkn_s[...], dkp_s[...], dkd_s[...]
        dp_ref[:, 0:c_gk] = ((dqp * t["e_pos"] + dqn * t["e_neg"]) * GLA_SCALE).astype(BF16)
        dp_ref[:, c_gk:c_gv] = (dkn * t["e_neg"] + dkp * t["e_pos"] + dkd * t["e_last"]).astype(BF16)
        kd_term = dkd * t["kd"]
        db = dqp * t["qp"] - dqn * t["qn"] - dkn * t["kn"] + dkp * t["kp"] - kd_term
        upper = jnp.where(t["same"] & (t["cc"] >= t["rr"]), 1.0, 0.0).astype(BF16)
        dla = (_sum01(upper, db) + _sum01(t["ones"], kd_term)
               + ddec_s[...] * t["dec"])
        dz = dla * GLA_TAU_INV * jax.nn.sigmoid(-t["z"])
        dzb = dz.astype(BF16)
        dp_ref[:, c_ga:c_end] = (_nt(dzb, wau_ref[...]) + dfx_ref[...]).astype(BF16)
        dwau_part = _tn(t["gaff"].astype(BF16), dzb)
        dba_part = _sum8(dz)

        @pl.when(first)
        def _():
            dwau_ref[...] = dwau_part
            dba_ref[...] = dba_part
            dgh_ref[...] = dgh_part

        @pl.when(jnp.logical_not(first))
        def _():
            dwau_ref[...] += dwau_part
            dba_ref[...] += dba_part
            dgh_ref[...] += dgh_part

        @pl.when(pl.program_id(0) == n_b - 1)
        def _():
            scatter().wait()

    rev = lambda i: (n_b - 1 - i, 0)
    f32k = pltpu.VMEM((tb, GLA_K), F32)
    bf4 = pltpu.VMEM((n_c, GLA_H, CHUNK, GLA_K), BF16)
    any_spec = pl.BlockSpec(memory_space=pl.ANY)
    out = pl.pallas_call(
        body, name="gla_bwd", grid=(n_b,),
        in_specs=[pl.BlockSpec((tb, GLA_K), rev), pl.BlockSpec((tb, GLA_K), lambda i: (n_b - 1 - i, 1)),
                  pl.BlockSpec((tb, GLA_V), lambda i: (n_b - 1 - i, 1)), pl.BlockSpec((tb, S_W), rev),
                  pl.BlockSpec((128, GLA_K), lambda i: (0, 0)), pl.BlockSpec((1, GLA_K), lambda i: (0, 0)),
                  pl.BlockSpec((1, GLA_V), lambda i: (0, 0)), pl.BlockSpec((tb, GLA_V), rev), pl.BlockSpec((tb, GLA_V), rev),
                  pl.BlockSpec((n_c, GLA_DV, GLA_K), lambda i: (n_b - 1 - i, 0, 0)), pl.BlockSpec((tb, 128), rev),
                  any_spec] + [any_spec] * n_x,
        out_specs=[pl.BlockSpec((tb, P_GLA_W), lambda i: (n_b - 1 - i, P_GLA // P_GLA_W)),
                   pl.BlockSpec((128, GLA_K), lambda i: (0, 0)), pl.BlockSpec((8, GLA_K), lambda i: (0, 0)),
                   pl.BlockSpec((8, GLA_V), lambda i: (0, 0))] + [any_spec] * n_x,
        out_shape=[jax.ShapeDtypeStruct((S, P_W), BF16), jax.ShapeDtypeStruct((128, GLA_K), F32),
                   jax.ShapeDtypeStruct((8, GLA_K), F32), jax.ShapeDtypeStruct((8, GLA_V), F32)]
        + [jax.ShapeDtypeStruct(b.shape, b.dtype) for b in slabs],
        input_output_aliases={11: 0},
        scratch_shapes=[bf4, bf4, bf4, pltpu.VMEM((tb, GLA_K), BF16), pltpu.VMEM((tb, GLA_K), BF16), f32k,
                        pltpu.VMEM((tb, GLA_V), BF16), f32k, f32k, f32k, f32k, f32k, f32k, pltpu.VMEM((GLA_DV, GLA_K), F32)]
        + _exchange_sems(n_x),
        compiler_params=_params("arbitrary"),
    )(pa, pa, pa, ps, wau, ba, gh, o_gla, d_og, sprev, dgaff_fox, d_proj, *slabs)
    return out[0], out[1], out[2], out[3], out[4:]


def _split3(x):
    x1 = x.astype(BF16).astype(F32)
    x2 = (x - x1).astype(BF16).astype(F32)
    x3 = (x - x1 - x2).astype(BF16).astype(F32)
    return x1, x2, x3


def _fox_prep(pa, ps, bfg):
    S = pa.shape[0]
    tm = _tile(S, FOX_TK)

    def body(ps_ref, b_ref, fq_ref, fk_ref, fv_ref, q_ref, k_ref, qt_ref, kt_ref, vt_ref, st_ref, carry):
        @pl.when(pl.program_id(0) == 0)
        def _():
            carry[...] = jnp.zeros_like(carry)

        vt = fv_ref[...].astype(F32).T.astype(BF16)
        ones_row = jnp.where(lax.broadcasted_iota(jnp.int32, (FOX_VT - FOX_DH, tm), 0) == 0, 1.0, 0.0).astype(BF16)
        for h in range(FOX_H):
            vt_ref[FOX_VT * h:FOX_VT * h + FOX_DH, :] = vt[FOX_DH * h:FOX_DH * (h + 1), :]
            vt_ref[FOX_VT * h + FOX_DH:FOX_VT * (h + 1), :] = ones_row
        lf = _log_sigmoid(ps_ref[...] + b_ref[...])
        rr = lax.broadcasted_iota(jnp.int32, (tm, tm), 0)
        cc = lax.broadcasted_iota(jnp.int32, (tm, tm), 1)
        tri = jnp.where(cc <= rr, 1.0, 0.0).astype(F32)
        f = jnp.dot(tri, lf, preferred_element_type=F32, precision=HIGHEST) + carry[0:1, :]
        carry[...] = jnp.broadcast_to(f[tm - 1:tm, :], carry.shape)
        f1, f2, f3 = _split3(f)
        lane = lax.broadcasted_iota(jnp.int32, (tm, 128), 1)
        st_row = lax.broadcasted_iota(jnp.int32, (8, 128), 0)
        st_lane = lax.broadcasted_iota(jnp.int32, (8, 128), 1)
        stats = jnp.zeros((8, 128), F32)
        for h in range(FOX_H):
            cols = slice(128 * h, 128 * (h + 1))
            c = FF_LANE + h
            a1, a2, a3 = f1[:, c:c + 1], f2[:, c:c + 1], f3[:, c:c + 1]
            q = fq_ref[:, cols].astype(F32) * FOX_SCALE
            k = fk_ref[:, cols].astype(F32)
            fh = f[:, c:c + 1]
            vals = (jnp.max(jnp.sum(q * q, axis=-1, keepdims=True)), jnp.max(jnp.sum(k * k, axis=-1, keepdims=True)),
                    jnp.max(fh), jnp.min(fh), jnp.min(jnp.sum(q * k, axis=-1, keepdims=True)))
            for n, val in enumerate(vals):
                stats = jnp.where((st_row == h) & (st_lane == n), val, stats)
            for n, a in enumerate((a1, a2, a3)):
                q = jnp.where(lane == AUG + n, a, q)
                k = jnp.where(lane == AUG + 3 + n, -a, k)
            q = jnp.where((lane >= AUG + 3) & (lane < AUG + 6), 1.0, q)
            k = jnp.where((lane >= AUG) & (lane < AUG + 3), 1.0, k)
            q_ref[:, cols] = q.astype(BF16)
            k_ref[:, cols] = k.astype(BF16)
            qt_ref[cols, :] = q.T.astype(BF16)
            kt_ref[cols, :] = k.T.astype(BF16)
        st_ref[0] = stats

    wide = lambda j: pl.BlockSpec((tm, 1024), lambda i: (i, j))
    tall = lambda n: pl.BlockSpec((n, tm), lambda i: (0, i))
    return pl.pallas_call(
        body, name="fox_prep", grid=(S // tm,),
        in_specs=[pl.BlockSpec((tm, 128), lambda i: (i, 4)), pl.BlockSpec((1, 128), lambda i: (0, 0)), wide(1), wide(2),
                  pl.BlockSpec((tm, FOX_W), lambda i: (i, A_FV // FOX_W))],
        out_specs=[wide(0), wide(0), tall(1024), tall(1024), tall(FOX_H * FOX_VT), pl.BlockSpec((1, 8, 128), lambda i: (i, 0, 0))],
        out_shape=[jax.ShapeDtypeStruct((S, 1024), BF16), jax.ShapeDtypeStruct((S, 1024), BF16),
                   jax.ShapeDtypeStruct((1024, S), BF16), jax.ShapeDtypeStruct((1024, S), BF16),
                   jax.ShapeDtypeStruct((FOX_H * FOX_VT, S), BF16), jax.ShapeDtypeStruct((S // tm, 8, 128), F32)],
        scratch_shapes=[pltpu.VMEM((8, 128), F32)],
        compiler_params=_params("arbitrary"),
    )(ps, bfg, pa, pa, pa)


FOX_PRUNE_AT = -90.0


def _fox_live_ranges(stats, n_sub, ratio):
    n_b = stats.shape[0]
    q2, k2, f_max, f_min, own = (stats[:, :, n].T for n in range(5))
    slack = 0.01 * jnp.sqrt(q2 * k2) + 1e-5 * jnp.abs(f_max) + 1.0
    bound = (1.01 * jnp.sqrt(q2[:, :, None] * k2[:, None, :]) + (f_max + slack - own)[:, :, None]
             - (f_min - 1e-5 * jnp.abs(f_min))[:, None, :])
    blocks = jnp.arange(n_b)
    dead = (bound <= FOX_PRUNE_AT) & (blocks[None, :] < blocks[:, None])[None]
    dead_fwd = dead.reshape(FOX_H, n_b // n_sub, n_sub, n_b).all(axis=2)
    first = jnp.sum(jnp.cumprod(dead_fwd.astype(jnp.int32), axis=2), axis=2)
    last_live = n_b - 1 - jnp.sum(jnp.cumprod(dead[:, ::-1, :].astype(jnp.int32), axis=1), axis=1)
    first_wide = blocks // ratio + 1
    narrow_end = jnp.minimum(jnp.minimum(first_wide * ratio, n_b)[None], last_live + 1)
    wide_end = jnp.where(last_live >= (first_wide * ratio)[None], last_live // ratio + 1, first_wide[None])
    return first.astype(jnp.int32), narrow_end.astype(jnp.int32), wide_end.astype(jnp.int32)


def _fox_fwd(qa, ka, vt, first):
    S = qa.shape[0]
    tq = _tile(S, FOX_TQ)
    tk = _tile(tq, FOX_TK)
    n_sub = tq // tk

    def body(first_ref, q_ref, k_ref, vt_ref, o_ref, lse_ref):
        pair, i = pl.program_id(0), pl.program_id(1)
        both = lambda f: tuple(f(hh) for hh in range(2))

        def blk(j, carry, diag, heads=(0, 1)):
            ks = pl.ds(pl.multiple_of(j * tk, tk), tk)
            q0 = 0 if diag is None else diag * tk

            def head(hh):
                if hh not in heads:
                    return carry[hh]
                m, acc = carry[hh]
                mo, ao = m[:, q0:], acc[:, q0:]
                s = _nt(k_ref[ks, 128 * hh:128 * (hh + 1)], q_ref[q0:, 128 * hh:128 * (hh + 1)])
                if diag is not None:
                    live = lax.broadcasted_iota(jnp.int32, s.shape, 1) >= lax.broadcasted_iota(jnp.int32, s.shape, 0)
                    s = jnp.where(live, s, NEG)
                mn = jnp.maximum(mo, jnp.max(s, axis=0, keepdims=True))
                p = jnp.exp((s - mn).astype(BF16))
                an = jnp.exp(mo - mn) * ao + _nn(vt_ref[FOX_VT * hh:FOX_VT * (hh + 1), ks], p)
                if q0:
                    mn, an = (jnp.concatenate([old[:, :q0], new], axis=1) for old, new in ((m, mn), (acc, an)))
                return mn, an

            return both(head)

        one = (jnp.full((1, tq), NEG, F32), jnp.zeros((FOX_VT, tq), F32))
        past = i * n_sub
        f0, f1 = first_ref[2 * pair, i], first_ref[2 * pair + 1, i]
        join = jnp.maximum(f0, f1)
        solo = lambda hh: lambda c: lax.fori_loop(jnp.minimum(f0, f1), join, lambda j, cc: blk(j, cc, None, (hh,)), c)
        carry = lax.cond(f0 < f1, solo(0), solo(1), (one, one))
        n_both = past - join
        carry = lax.fori_loop(0, n_both // 2, lambda jj, c: blk(join + 2 * jj + 1, blk(join + 2 * jj, c, None), None), carry)
        carry = lax.cond(n_both % 2 == 1, lambda c: blk(past - 1, c, None), lambda c: c, carry)
        for d in range(n_sub):
            carry = blk(past + d, carry, d)
        (m0, a0), (m1, a1) = carry
        l0, l1 = a0[FOX_DH:FOX_DH + 1], a1[FOX_DH:FOX_DH + 1]
        o_ref[...] = jnp.concatenate([a0[:FOX_DH] / l0, a1[:FOX_DH] / l1], axis=0).T
        lse_ref[0, 0:1, :] = m0 + jnp.log(l0)
        lse_ref[0, 1:2, :] = m1 + jnp.log(l1)
        lse_ref[0, 2:8, :] = jnp.zeros((6, tq), F32)

    return pl.pallas_call(
        body, name="fox_fwd", grid=(FOX_H // 2, S // tq),
        in_specs=[pl.BlockSpec(memory_space=pltpu.SMEM), pl.BlockSpec((tq, 256), lambda p, i: (i, p)),
                  pl.BlockSpec((S, 256), lambda p, i: (0, p)), pl.BlockSpec((2 * FOX_VT, S), lambda p, i: (p, 0))],
        out_specs=[pl.BlockSpec((tq, 128), lambda p, i: (i, p)), pl.BlockSpec((1, 8, tq), lambda p, i: (p, 0, i))],
        out_shape=[jax.ShapeDtypeStruct((S, FOX_W), F32), jax.ShapeDtypeStruct((FOX_H // 2, 8, S), F32)],
        compiler_params=_params("arbitrary", "arbitrary"),
    )(first, qa, ka, vt)


def _fox_delta(d_o, o):
    S = o.shape[0]
    tm = _tile(S, 512)

    def body(d_ref, o_ref, db_ref, dbt_ref, dl_ref):
        d = d_ref[...]
        db_ref[...] = d.astype(BF16)
        dbt_ref[...] = d.T.astype(BF16)
        prod = d * o_ref[...]
        rr = lax.broadcasted_iota(jnp.int32, (8, 128), 0)
        cc = lax.broadcasted_iota(jnp.int32, (8, 128), 1)
        ind = jnp.where(jnp.right_shift(cc, 6) == rr, 1.0, 0.0).astype(F32)
        for p in range(FOX_H // 2):
            dl_ref[p] = lax.dot_general(ind, prod[:, 128 * p:128 * (p + 1)], (((1,), (1,)), ((), ())),
                                        preferred_element_type=F32, precision=HIGHEST)

    row = pl.BlockSpec((tm, FOX_W), lambda i: (i, 0))
    return pl.pallas_call(
        body, name="fox_delta", grid=(S // tm,),
        in_specs=[row, row],
        out_specs=[row, pl.BlockSpec((FOX_W, tm), lambda i: (0, i)), pl.BlockSpec((FOX_H // 2, 8, tm), lambda i: (0, 0, i))],
        out_shape=[jax.ShapeDtypeStruct((S, FOX_W), BF16), jax.ShapeDtypeStruct((FOX_W, S), BF16),
                   jax.ShapeDtypeStruct((FOX_H // 2, 8, S), F32)],
        compiler_params=_params("arbitrary"),
    )(d_o, o)


def _fox_bwd(qa, qat, ka, kat, pa, dob, dobt, lse, delta, narrow_end, wide_end):
    S = qa.shape[0]
    tk = _tile(S, FOX_TK)
    wide = _tile(S, FOX_BWD_WIDE)
    ratio = wide // tk
    n_wide = S // wide

    def body(ne_ref, we_ref, q_ref, qt_ref, k_ref, kt_ref, v_ref, do_ref, dot_ref, lse_ref, dl_ref, dq_ref, dk_ref, dv_ref):
        h, jb = pl.program_id(0), pl.program_id(1)
        hh = h % 2

        @pl.when(jb == 0)
        def _():
            dq_ref[...] = jnp.zeros_like(dq_ref)

        lane = lax.broadcasted_iota(jnp.int32, (tk, 128), 1)
        vm = jnp.where(jnp.right_shift(lane, 6) == hh, v_ref[...], jnp.zeros((), BF16))
        kb, ktb = k_ref[...], kt_ref[0:FOX_LIVE, :]
        mine = pl.ds(pl.multiple_of(hh * FOX_DH, FOX_DH), FOX_DH)

        def blk(ib, tq, carry, masked):
            dk, dv = carry
            qs = pl.ds(pl.multiple_of(ib * tq, tq), tq)
            p = jnp.exp(_nt(kb, q_ref[qs, :]) - lse_ref[0, pl.ds(hh, 1), qs])
            if masked:
                live = lax.broadcasted_iota(jnp.int32, p.shape, 1) >= lax.broadcasted_iota(jnp.int32, p.shape, 0)
                p = jnp.where(live, p, 0.0)
            ds = (p * (_nt(vm, do_ref[qs, :]) - dl_ref[0, pl.ds(hh, 1), qs])).astype(BF16)
            dq_ref[0:FOX_LIVE, qs] += _nn(ktb, ds)
            return dk + _nt(qt_ref[0:FOX_LIVE, qs], ds), dv + _nt(dot_ref[mine, qs], p.astype(BF16))

        carry = blk(jb, tk, (jnp.zeros((FOX_LIVE, tk), F32), jnp.zeros((FOX_DH, tk), F32)), True)
        first_wide = jb // ratio + 1
        carry = lax.fori_loop(jb + 1, ne_ref[h, jb], lambda ib, c: blk(ib, tk, c, False), carry)
        last_wide = we_ref[h, jb]
        rest = jnp.maximum(last_wide - first_wide, 0)
        carry = lax.fori_loop(0, rest // 2, lambda t, c: blk(first_wide + 2 * t + 1, wide, blk(first_wide + 2 * t, wide, c, False),
                                                             False), carry)
        dk, dv = lax.cond(rest % 2 == 1, lambda c: blk(last_wide - 1, wide, c, False), lambda c: c, carry)
        dk_ref[0:FOX_LIVE, :] = dk
        dk_ref[FOX_LIVE:, :] = jnp.zeros((128 - FOX_LIVE, tk), F32)
        dv_ref[...] = dv

    once = pl.Buffered(1)
    rows = pl.BlockSpec((1, 8, S), lambda h, j: (h // 2, 0, 0))
    return pl.pallas_call(
        body, name="fox_bwd", grid=(FOX_H, S // tk),
        in_specs=[pl.BlockSpec(memory_space=pltpu.SMEM), pl.BlockSpec(memory_space=pltpu.SMEM),
                  pl.BlockSpec((S, 128), lambda h, j: (0, h)), pl.BlockSpec((128, S), lambda h, j: (h, 0)),
                  pl.BlockSpec((tk, 128), lambda h, j: (j, h)), pl.BlockSpec((128, tk), lambda h, j: (h, j)),
                  pl.BlockSpec((tk, 128), lambda h, j: (j, A_FV // 128 + h // 2)),
                  pl.BlockSpec((S, 128), lambda h, j: (0, h // 2)), pl.BlockSpec((128, S), lambda h, j: (h // 2, 0)),
                  rows, rows],
        out_specs=[pl.BlockSpec((128, S), lambda h, j: (h, 0), pipeline_mode=once),
                   pl.BlockSpec((128, tk), lambda h, j: (h, j)), pl.BlockSpec((FOX_DH, tk), lambda h, j: (h, j))],
        out_shape=[jax.ShapeDtypeStruct((1024, S), F32), jax.ShapeDtypeStruct((1024, S), F32),
                   jax.ShapeDtypeStruct((FOX_W, S), F32)],
        compiler_params=_params("arbitrary", "arbitrary"),
    )(narrow_end, wide_end, qa, qat, ka, kat, pa, dob, dobt, lse, delta)


def _fox_post(dq, dk, dv, ps, bfg, d_proj):
    S = dq.shape[1]
    tm = _tile(S, 512)
    n_b = S // tm

    def body(dq_ref, dk_ref, dv_ref, ps_ref, b_ref, _, dp_ref, dff_ref, dbf_ref, carry):
        first = pl.program_id(0) == 0

        @pl.when(first)
        def _():
            carry[...] = jnp.zeros_like(carry)

        low = lax.broadcasted_iota(jnp.int32, (tm, 128), 1) < FOX_DH
        for h in range(FOX_H):
            blk = slice(128 * h, 128 * (h + 1))
            dp_ref[:, blk] = jnp.where(low, dq_ref[blk, :].T * FOX_SCALE, 0.0).astype(BF16)
            dp_ref[:, 1024 + 128 * h:1024 + 128 * (h + 1)] = jnp.where(low, dk_ref[blk, :].T, 0.0).astype(BF16)
        dp_ref[:, 2048:P_FOX_W] = dv_ref[...].T.astype(BF16)
        rr = lax.broadcasted_iota(jnp.int32, (FOX_H, 1024), 0)
        cc = lax.broadcasted_iota(jnp.int32, (FOX_H, 1024), 1)
        sel_k = jnp.where(cc == 128 * rr + AUG + 3, 1.0, 0.0).astype(F32)
        sel_q = jnp.where(cc == 128 * rr + AUG, 1.0, 0.0).astype(F32)
        g = (jnp.dot(sel_k, dk_ref[...], preferred_element_type=F32, precision=HIGHEST)
             - jnp.dot(sel_q, dq_ref[...], preferred_element_type=F32, precision=HIGHEST))
        t_from = lax.broadcasted_iota(jnp.int32, (tm, tm), 0)
        t_to = lax.broadcasted_iota(jnp.int32, (tm, tm), 1)
        later = jnp.where(t_from >= t_to, 1.0, 0.0).astype(F32)
        dlf = jnp.dot(-g, later, preferred_element_type=F32, precision=HIGHEST) + carry[:, 0:1]
        carry[...] = jnp.broadcast_to(dlf[:, 0:1], carry.shape)
        cols = jnp.concatenate([jnp.zeros((FF_LANE, tm), F32), dlf, jnp.zeros((128 - FF_LANE - FOX_H, tm), F32)], axis=0).T
        dff = cols * jax.nn.sigmoid(-(ps_ref[...] + b_ref[...]))
        dff_ref[...] = dff
        part = _sum8(dff)

        @pl.when(first)
        def _():
            dbf_ref[...] = part

        @pl.when(jnp.logical_not(first))
        def _():
            dbf_ref[...] += part

    rev = lambda i: (n_b - 1 - i, 0)
    tall = lambda n: pl.BlockSpec((n, tm), lambda i: (0, n_b - 1 - i))
    return pl.pallas_call(
        body, name="fox_post", grid=(n_b,),
        in_specs=[tall(1024), tall(1024), tall(FOX_W), pl.BlockSpec((tm, 128), lambda i: (n_b - 1 - i, 4)),
                  pl.BlockSpec((1, 128), lambda i: (0, 0)), pl.BlockSpec(memory_space=pl.ANY)],
        out_specs=[pl.BlockSpec((tm, P_FOX_W), lambda i: (n_b - 1 - i, P_FOX // P_FOX_W)), pl.BlockSpec((tm, 128), rev),
                   pl.BlockSpec((8, 128), lambda i: (0, 0))],
        out_shape=[jax.ShapeDtypeStruct((S, P_W), BF16), jax.ShapeDtypeStruct((S, 128), F32),
                   jax.ShapeDtypeStruct((8, 128), F32)],
        input_output_aliases={5: 0},
        scratch_shapes=[pltpu.VMEM((8, 128), F32)],
        compiler_params=_params("arbitrary"),
    )(dq, dk, dv, ps, bfg, d_proj)


def _mem_prep(mem, g_mem, wkv):
    def body(m_ref, g_ref, w_ref, mn_ref, kv_ref):
        r, xh = _rms(m_ref[...])
        mn = (xh * g_ref[...]).astype(BF16)
        mn_ref[...] = mn
        kv_ref[...] = _nn(mn, w_ref[...]).astype(BF16)

    return pl.pallas_call(
        body, name="mem_prep",
        out_shape=[jax.ShapeDtypeStruct((N_MEM, D), BF16), jax.ShapeDtypeStruct((N_MEM, 2 * MEM_W), BF16)],
        compiler_params=pltpu.CompilerParams(vmem_limit_bytes=V7X_VMEM_LIMIT),
    )(mem, g_mem, wkv)


def _mem_softmax(qh, kh):
    s = _nt(qh, kh) * MEM_SCALE
    e = jnp.exp(s - jnp.max(s, axis=-1, keepdims=True))
    return e / jnp.sum(e, axis=-1, keepdims=True)


def _mem_fwd(pa, mkv):
    S = pa.shape[0]
    tm = _tile(S, 1024)

    def body(q_ref, kv_ref, o_ref):
        for h in range(MEM_H):
            cols = slice(MEM_DH * h, MEM_DH * (h + 1))
            p = _mem_softmax(q_ref[:, cols], kv_ref[:, cols])
            o_ref[:, cols] = _nn(p.astype(BF16), kv_ref[:, MEM_W + MEM_DH * h:MEM_W + MEM_DH * (h + 1)])

    return pl.pallas_call(
        body, name="mem_fwd", grid=(S // tm,),
        in_specs=[pl.BlockSpec((tm, MEM_W), lambda i: (i, A_MQ // MEM_W)), pl.BlockSpec((N_MEM, 2 * MEM_W), lambda i: (0, 0))],
        out_specs=pl.BlockSpec((tm, MEM_W), lambda i: (i, 0)),
        out_shape=jax.ShapeDtypeStruct((S, MEM_W), F32),
        compiler_params=_params("arbitrary"),
    )(pa, mkv)


def _mem_bwd(pa, mkv, d_o, d_proj):
    S = pa.shape[0]
    tm = _tile(S, 1024)

    def body(q_ref, kv_ref, do_ref, _, dq_ref, dkv_ref):
        first = pl.program_id(0) == 0
        parts = []
        for h in range(MEM_H):
            cols = slice(MEM_DH * h, MEM_DH * (h + 1))
            vcols = slice(MEM_W + MEM_DH * h, MEM_W + MEM_DH * (h + 1))
            qh, kh = q_ref[:, cols], kv_ref[:, cols]
            p = _mem_softmax(qh, kh)
            dob = do_ref[:, cols].astype(BF16)
            dp = _nt(dob, kv_ref[:, vcols])
            ds = (p * (dp - jnp.sum(p * dp, axis=-1, keepdims=True)) * MEM_SCALE).astype(BF16)
            dq_ref[:, cols] = _nn(ds, kh).astype(BF16)
            parts.append((cols, _tn(ds, qh)))
            parts.append((vcols, _tn(p.astype(BF16), dob)))

        @pl.when(first)
        def _():
            for sl, v in parts:
                dkv_ref[:, sl] = v

        @pl.when(jnp.logical_not(first))
        def _():
            for sl, v in parts:
                dkv_ref[:, sl] += v

    return pl.pallas_call(
        body, name="mem_bwd", grid=(S // tm,),
        in_specs=[pl.BlockSpec((tm, MEM_W), lambda i: (i, A_MQ // MEM_W)), pl.BlockSpec((N_MEM, 2 * MEM_W), lambda i: (0, 0)),
                  pl.BlockSpec((tm, MEM_W), lambda i: (i, 0)), pl.BlockSpec(memory_space=pl.ANY)],
        out_specs=[pl.BlockSpec((tm, MEM_W), lambda i: (i, P_MQ // MEM_W)), pl.BlockSpec((N_MEM, 2 * MEM_W), lambda i: (0, 0))],
        out_shape=[jax.ShapeDtypeStruct((S, P_W), BF16), jax.ShapeDtypeStruct((N_MEM, 2 * MEM_W), F32)],
        input_output_aliases={3: 0},
        compiler_params=_params("arbitrary"),
    )(pa, mkv, d_o, d_proj)


def _mem_prep_bwd(mem, g_mem, mn, wkv, dkv):
    def body(m_ref, g_ref, mn_ref, w_ref, d_ref, dw_ref, dg_ref):
        db = d_ref[...].astype(BF16)
        dw_ref[...] = _tn(mn_ref[...], db).astype(BF16)
        r, xh = _rms(m_ref[...])
        dg_ref[...] = _sum8(_nt(db, w_ref[...]) * xh)

    dw, dg = pl.pallas_call(
        body, name="mem_prep_bwd",
        out_shape=[jax.ShapeDtypeStruct((D, 2 * MEM_W), BF16), jax.ShapeDtypeStruct((8, D), F32)],
        compiler_params=pltpu.CompilerParams(vmem_limit_bytes=V7X_VMEM_LIMIT),
    )(mem, g_mem, mn, wkv, dkv)
    return dw.reshape(N_DEV, D // N_DEV, 2 * MEM_W), dg


def _rearrange_w_in(w):
    def heads128(cols):
        blk = w[:, cols:cols + FOX_W].reshape(D, FOX_H, FOX_DH)
        return jnp.pad(blk, ((0, 0), (0, 0), (0, 128 - FOX_DH))).reshape(D, FOX_H * 128)

    fq, fk, fv, mq, wg = heads128(O_FQ), heads128(O_FK), w[:, O_FV:O_FF], w[:, O_MQ:O_GT], w[:, O_GT:]
    gaff = jnp.concatenate([w[:, O_GA:O_FQ], w[:, O_FF:O_MQ], jnp.zeros((D, 128 - GLA_R - FOX_H), w.dtype)], axis=1)
    wa = jnp.concatenate([w[:, O_GQ:O_GG], fq, fk, fv, mq], axis=1)
    ws = jnp.concatenate([w[:, O_GG:O_GA], gaff], axis=1)
    wp = jnp.concatenate([fq, fk, fv, mq, wg, w[:, O_GQ:O_GG], ws, jnp.zeros((D, P_W - P_GLA - 1024 - S_W), w.dtype)], axis=1)
    return wa, wg, ws, wp


def _restore_w_in_grad(dwp):
    def unheads(off):
        return dwp[:, off:off + FOX_H * 128].reshape(D, FOX_H, 128)[:, :, :FOX_DH].reshape(D, FOX_W)

    g0 = P_GLA + 1024
    return jnp.concatenate([
        dwp[:, P_GLA:g0], dwp[:, g0:g0 + 512], dwp[:, g0 + 512:g0 + 512 + GLA_R], unheads(P_FOX), unheads(P_FOX + 1024),
        dwp[:, P_FOX + 2048:P_FOX + P_FOX_W], dwp[:, g0 + 512 + GLA_R:g0 + 512 + GLA_R + FOX_H], dwp[:, P_MQ:P_GT],
        dwp[:, P_GT:P_GLA]], axis=1)


def _local_step(x, mem, target, p, late_shards):
    S = x.shape[0]
    p = dict(p)
    wa, wg, ws, wp = _rearrange_w_in(p["w_in"])
    wau = jnp.pad(p["w_alpha_up"], ((0, 128 - GLA_R), (0, 0)))
    bfg = jnp.pad(p["b_forget"], ((0, 0), (FF_LANE, 128 - FF_LANE - FOX_H)))
    gh = p["g_gla_head"].reshape(1, GLA_V)

    pa, pg, ps, u, gathered = _proj(x, p["g_mix"], wa, wg, ws, late_shards)
    p.update({n: _unslab(t, ax) for (n, ax), t in zip(BIG[1:], gathered)})
    o_gla, og, sprev = _gla_fwd(pa, ps, wau, p["b_alpha"], gh)
    qa, ka, qat, kat, vt, fox_stats = _fox_prep(pa, ps, bfg)
    fox_tk = _tile(S, FOX_TK)
    fox_first, fox_narrow_end, fox_wide_end = _fox_live_ranges(fox_stats, _tile(S, FOX_TQ) // fox_tk,
                                                               _tile(S, FOX_BWD_WIDE) // fox_tk)
    o_fox, lse = _fox_fwd(qa, ka, vt, fox_first)
    mn, mkv = _mem_prep(mem, p["g_mem"], p["w_mem_kv"])
    o_mem = _mem_fwd(pa, mkv)
    y3, mg = _merge(og, o_fox, o_mem, p["w_gla_o"], p["w_fox_o"], p["w_mem_o"], pg)
    h1, u2 = _out_proj(mg, p["w_out"], x, p["g_ffn"])
    a, act = _ff1(u2, p["w_ff1"])
    dh2, dh2b, loss8, dg_final = _ff2_loss(act, p["w_ff2"], h1, p["g_final"].reshape(1, D), target)

    d_a = _dact(dh2b, p["w_ff2"], a)
    dw_ff2 = _wgrad(act, dh2b, "wgrad_ff2", 0)
    dh1, dh1b, dg_ffn, arrived_ff2 = _nt_rmsbwd(d_a, p["w_ff1"], h1, p["g_ffn"], dh2, "dffn", True, [dw_ff2])
    dw_ff1 = _wgrad(u2, d_a, "wgrad_ff1", 1)
    dy_g, dy_f, dy_m, do_g, do_f, do_m, d_proj, arrived_ff1 = _dmerge(dh1b, p["w_out"], pg, y3, p["w_gla_o"], p["w_fox_o"],
                                                                      p["w_mem_o"], [dw_ff1])
    dw_out = _wgrad(mg, dh1b, "wgrad_out", 0)
    dw_gla_o = _wgrad(og, dy_g, "wgrad_gla_o", 1)
    dw_fox_o = _wgrad(o_fox, dy_f, "wgrad_fox_o", 1)
    dw_mem_o = _wgrad(o_mem, dy_m, "wgrad_mem_o", 1)
    d_proj, d_mkv = _mem_bwd(pa, mkv, do_m, d_proj)
    dw_mem_kv, dg_mem = _mem_prep_bwd(mem, p["g_mem"], mn, p["w_mem_kv"], d_mkv)
    dob, dobt, delta = _fox_delta(do_f, o_fox)
    dq, dk, dv = _fox_bwd(qa, qat, ka, kat, pa, dob, dobt, lse, delta, fox_narrow_end, fox_wide_end)
    d_proj, dgaff_fox, db_forget = _fox_post(dq, dk, dv, ps, bfg, d_proj)
    d_proj, dw_au, db_alpha, dg_gla, arrived = _gla_bwd(pa, ps, wau, p["b_alpha"], gh, o_gla, do_g, sprev, dgaff_fox, d_proj,
                                                        [dw_mem_kv, dw_gla_o, dw_fox_o, dw_mem_o, dw_out])
    dw_in = _slabs(_restore_w_in_grad(_wgrad(u, d_proj, "wgrad_in")), 1)
    dx, dg_mix, arrived_in = _nt_rmsbwd(d_proj, wp, x, p["g_mix"], dh1, "dmix", False, [dw_in])

    big = dict(w_in=arrived_in[0], w_ff1=arrived_ff1[0], w_ff2=arrived_ff2[0],
               **dict(zip(("w_mem_kv", "w_gla_o", "w_fox_o", "w_mem_o", "w_out"), arrived)))
    small = dict(g_mix=dg_mix, g_mem=dg_mem, g_ffn=dg_ffn, g_final=dg_final, b_alpha=db_alpha, g_gla_head=dg_gla,
                 b_forget=db_forget, w_alpha_up=dw_au, loss=loss8)
    return dx, big, small


BIG = (("w_in", 1), ("w_mem_kv", 0), ("w_gla_o", 1), ("w_fox_o", 1), ("w_mem_o", 1), ("w_out", 0), ("w_ff1", 1), ("w_ff2", 0))


def _peer(d):
    me = lax.axis_index("x") * 4 + lax.axis_index("y") * 2 + lax.axis_index("c")
    t = (me + d) % N_DEV
    return (t // 4, (t // 2) % 2, t % 2), me


def _exchange_sems(n):
    return [pltpu.SemaphoreType.DMA((n, N_DEV - 1)), pltpu.SemaphoreType.DMA((n, N_DEV - 1)), pltpu.SemaphoreType.DMA((n,))]


def _exchange_call(body, blocks, out_shape, name):
    n = len(blocks)
    any_spec = pl.BlockSpec(memory_space=pl.ANY)
    return pl.pallas_call(body, name=name, in_specs=[any_spec] * n, out_specs=[any_spec] * n, out_shape=out_shape,
                          scratch_shapes=_exchange_sems(n))(*blocks)


class _AllToAll:
    def __init__(self, ins, outs, sems, gather):
        send, recv, loc = sems
        n = len(ins)
        _, me = _peer(0)
        src = (lambda k, j: ins[k]) if gather else (lambda k, j: ins[k].at[j])
        self.local = [pltpu.make_async_copy(src(k, me), outs[k].at[me], loc.at[k]) for k in range(n)]
        self.remote = []
        for d in range(1, N_DEV):
            to, _ = _peer(d)
            self.remote += [pltpu.make_async_remote_copy(
                src_ref=src(k, (me + d) % N_DEV), dst_ref=outs[k].at[me], send_sem=send.at[k, d - 1],
                recv_sem=recv.at[k, d - 1], device_id=to, device_id_type=MESH) for k in range(n)]

    def start(self):
        for cp in self.local + self.remote:
            cp.start()

    def wait(self):
        for cp in self.remote:
            cp.wait_send()
        for cp in self.remote:
            cp.wait_recv()
        for cp in self.local:
            cp.wait()


def _gathered_shapes(shards):
    return [jax.ShapeDtypeStruct((N_DEV,) + b.shape, b.dtype) for b in shards]


def _gather_weights(shards):
    n = len(shards)

    def body(*refs):
        ins, outs = refs[:n], refs[n:2 * n]
        send, recv, loc = refs[2 * n:]
        x, y, c = lax.axis_index("x"), lax.axis_index("y"), lax.axis_index("c")
        sibling = (x, y, 1 - c)
        chips = [(1 - x, y), (x, 1 - y), (1 - x, 1 - y)]
        slot = lambda px, py, pc: px * 4 + py * 2 + pc

        def copy(k, s, block, to, src=None):
            rows = outs[k].at[slot(*block)]
            return pltpu.make_async_remote_copy(src_ref=rows if src is None else src, dst_ref=rows, send_sem=send.at[k, s],
                                                recv_sem=recv.at[k, s], device_id=to, device_id_type=MESH)

        me = (x, y, c)
        own = [pltpu.make_async_copy(ins[k], outs[k].at[slot(*me)], loc.at[k]) for k in range(n)]
        first = [copy(k, 0, me, sibling, src=ins[k]) for k in range(n)]
        first += [copy(k, 1 + j, me, (*chip, c), src=ins[k]) for j, chip in enumerate(chips) for k in range(n)]
        for cp in own + first:
            cp.start()
        passed = []
        for j, chip in enumerate(chips):
            for k in range(n):
                copy(k, 1 + j, (*chip, c), me).wait_recv()
                fwd = copy(k, 4 + j, (*chip, c), sibling)
                fwd.start()
                passed.append(fwd)
        for k in range(n):
            copy(k, 0, sibling, me).wait_recv()
        for j, chip in enumerate(chips):
            for k in range(n):
                copy(k, 4 + j, (*chip, 1 - c), me).wait_recv()
        for cp in first + passed:
            cp.wait_send()
        for cp in own:
            cp.wait()

    return _exchange_call(body, shards, [jax.ShapeDtypeStruct((N_DEV,) + b.shape, b.dtype) for b in shards], "gather_weights")


def _adamw_math(g, w, m, v):
    m2 = ADAM_B1 * m + (1.0 - ADAM_B1) * g
    v2 = ADAM_B2 * v + (1.0 - ADAM_B2) * jnp.square(g)
    m_hat = m2 / (1.0 - ADAM_B1 ** ADAM_STEP)
    v_hat = v2 / (1.0 - ADAM_B2 ** ADAM_STEP)
    delta = -ADAM_LR * (m_hat / (jnp.sqrt(v_hat) + ADAM_EPS) + ADAM_WD * w)
    return delta, m2, v2


def _adamw_sum(parts, w, m, v, name):
    R, C = w.shape
    tr = _tile(R, 128)

    def body(p_ref, w_ref, m_ref, v_ref, g_ref, d_ref, m2_ref, v2_ref):
        g = p_ref[0].astype(F32)
        for j in range(1, p_ref.shape[0]):
            g = g + p_ref[j].astype(F32)
        g_ref[...] = g
        d_ref[...], m2_ref[...], v2_ref[...] = _adamw_math(g, w_ref[...], m_ref[...], v_ref[...])

    blk = pl.BlockSpec((tr, C), lambda i: (i, 0))
    return pl.pallas_call(
        body, name=name, grid=(R // tr,),
        in_specs=[pl.BlockSpec((parts.shape[0], tr, C), lambda i: (0, i, 0)), blk, blk, blk],
        out_specs=[blk] * 4, out_shape=[jax.ShapeDtypeStruct((R, C), F32)] * 4,
        compiler_params=_params("arbitrary"),
    )(parts, w, m, v)


SMALL_ROWS = 24


def _pack_small(d):
    mixed = jnp.concatenate([d["b_alpha"].reshape(1, GLA_K), d["g_gla_head"].reshape(1, GLA_V),
                             jnp.pad(d["b_forget"].reshape(1, FOX_H), ((0, 0), (FF_LANE, 128 - FF_LANE - FOX_H))),
                             jnp.zeros((1, 128), F32)], axis=1)
    rows = [d["g_mix"].reshape(1, D), d["g_mem"].reshape(1, D), d["g_ffn"].reshape(1, D), d["g_final"].reshape(1, D), mixed,
            jnp.zeros((3, D), F32), jnp.pad(d["w_alpha_up"].reshape(GLA_R, GLA_K), ((0, 0), (0, D - GLA_K)))]
    return jnp.concatenate(rows, axis=0)


def _unpack_small(t):
    return dict(g_mix=t[0:1], g_mem=t[1:2], g_ffn=t[2:3], g_final=t[3], b_alpha=t[4:5, 0:GLA_K],
                g_gla_head=t[4:5, GLA_K:GLA_K + GLA_V].reshape(1, GLA_H, GLA_DV),
                b_forget=t[4:5, 768 + FF_LANE:768 + FF_LANE + FOX_H], w_alpha_up=t[8:24, 0:GLA_K].reshape(1, GLA_R, GLA_K))


def _small_allreduce(small, w, m, v):
    def body(gm, gme, gf, gfi, ba, gg, bf, wau, ls, w_ref, m_ref, v_ref, g_ref, d_ref, m2_ref, v2_ref, l_ref,
             buf, send, recv):
        _, me = _peer(0)
        buf[me] = jnp.zeros((SMALL_ROWS, D), F32)
        for r, ref in enumerate((gm, gme, gf, gfi)):
            buf[me, r:r + 1, :] = jnp.sum(ref[...], axis=0, keepdims=True)
        buf[me, 4:5, 0:GLA_K] = jnp.sum(ba[...], axis=0, keepdims=True)
        buf[me, 4:5, GLA_K:GLA_K + GLA_V] = jnp.sum(gg[...], axis=0, keepdims=True)
        buf[me, 4:5, 768:896] = jnp.sum(bf[...], axis=0, keepdims=True)
        lrow = jnp.sum(ls[...], axis=0, keepdims=True)
        lsum = lrow[:, 0:128]
        for c in range(1, D // 128):
            lsum = lsum + lrow[:, 128 * c:128 * (c + 1)]
        buf[me, 4:5, 896:1024] = lsum
        buf[me, 8:24, 0:GLA_K] = wau[0:GLA_R, :]
        remote = []
        for d in range(1, N_DEV):
            to, me = _peer(d)
            cp = pltpu.make_async_remote_copy(src_ref=buf.at[me], dst_ref=buf.at[me], send_sem=send.at[d - 1],
                                              recv_sem=recv.at[d - 1], device_id=to, device_id_type=MESH)
            cp.start()
            remote.append(cp)
        for cp in remote:
            cp.wait_send()
        for cp in remote:
            cp.wait_recv()
        g = buf[0]
        for j in range(1, N_DEV):
            g = g + buf[j]
        g_ref[...] = g
        d_ref[...], m2_ref[...], v2_ref[...] = _adamw_math(g, w_ref[...], m_ref[...], v_ref[...])
        l_ref[...] = g[4:5, 896:1024]

    packed = jax.ShapeDtypeStruct((SMALL_ROWS, D), F32)
    return pl.pallas_call(
        body, name="small_allreduce",
        out_shape=[packed, packed, packed, packed, jax.ShapeDtypeStruct((1, 128), F32)],
        scratch_shapes=[pltpu.VMEM((N_DEV, SMALL_ROWS, D), F32), pltpu.SemaphoreType.DMA((N_DEV - 1,)),
                        pltpu.SemaphoreType.DMA((N_DEV - 1,))],
    )(small["g_mix"], small["g_mem"], small["g_ffn"], small["g_final"], small["b_alpha"], small["g_gla_head"],
      small["b_forget"], small["w_alpha_up"], small["loss"], w, m, v)


def _slabs(g, axis):
    R, C = g.shape
    if axis == 0:
        return g.reshape(N_DEV, R // N_DEV, C)
    return g.reshape(R, N_DEV, C // N_DEV).transpose(1, 0, 2)


def _unslab(t, axis):
    n, r, c = t.shape
    if axis == 0:
        return t.reshape(n * r, c)
    return t.transpose(1, 0, 2).reshape(r, n * c)


def kernel(x, mem, g_mix, w_in, w_alpha_up, b_alpha, b_forget, g_gla_head, g_mem, w_mem_kv, w_gla_o, w_fox_o, w_mem_o, w_out, g_ffn, w_ff1, w_ff2, g_final, loss_target, m_g_mix, m_w_in, m_w_alpha_up, m_b_alpha, m_b_forget, m_g_gla_head, m_g_mem, m_w_mem_kv, m_w_gla_o, m_w_fox_o, m_w_mem_o, m_w_out, m_g_ffn, m_w_ff1, m_w_ff2, m_g_final, v_g_mix, v_w_in, v_w_alpha_up, v_b_alpha, v_b_forget, v_g_gla_head, v_g_mem, v_w_mem_kv, v_w_gla_o, v_w_fox_o, v_w_mem_o, v_w_out, v_g_ffn, v_w_ff1, v_w_ff2, v_g_final):
    names = ["g_mix", "w_in", "w_alpha_up", "b_alpha", "b_forget", "g_gla_head", "g_mem", "w_mem_kv", "w_gla_o", "w_fox_o",
             "w_mem_o", "w_out", "g_ffn", "w_ff1", "w_ff2", "g_final"]
    w = dict(g_mix=g_mix, w_in=w_in, w_alpha_up=w_alpha_up, b_alpha=b_alpha, b_forget=b_forget, g_gla_head=g_gla_head,
             g_mem=g_mem, w_mem_kv=w_mem_kv, w_gla_o=w_gla_o, w_fox_o=w_fox_o, w_mem_o=w_mem_o, w_out=w_out, g_ffn=g_ffn,
             w_ff1=w_ff1, w_ff2=w_ff2, g_final=g_final)
    m = dict(g_mix=m_g_mix, w_in=m_w_in, w_alpha_up=m_w_alpha_up, b_alpha=m_b_alpha, b_forget=m_b_forget,
             g_gla_head=m_g_gla_head, g_mem=m_g_mem, w_mem_kv=m_w_mem_kv, w_gla_o=m_w_gla_o, w_fox_o=m_w_fox_o,
             w_mem_o=m_w_mem_o, w_out=m_w_out, g_ffn=m_g_ffn, w_ff1=m_w_ff1, w_ff2=m_w_ff2, g_final=m_g_final)
    v = dict(g_mix=v_g_mix, w_in=v_w_in, w_alpha_up=v_w_alpha_up, b_alpha=v_b_alpha, b_forget=v_b_forget,
             g_gla_head=v_g_gla_head, g_mem=v_g_mem, w_mem_kv=v_w_mem_kv, w_gla_o=v_w_gla_o, w_fox_o=v_w_fox_o,
             w_mem_o=v_w_mem_o, w_out=v_w_out, g_ffn=v_g_ffn, w_ff1=v_w_ff1, w_ff2=v_w_ff2, g_final=v_g_final)
    me = lax.axis_index("x") * 4 + lax.axis_index("y") * 2 + lax.axis_index("c")

    shard = lambda n: w[n][0].astype(BF16)
    w_in_all, w_au_all = _gather_weights([shard("w_in"), shard("w_alpha_up")])
    p = dict(w_in=_unslab(w_in_all, 1), w_alpha_up=_unslab(w_au_all, 1), g_mix=g_mix, b_alpha=b_alpha, b_forget=b_forget,
             g_gla_head=g_gla_head, g_mem=g_mem, g_ffn=g_ffn, g_final=g_final)

    dx, big, small = _local_step(x[0], mem[0], loss_target[0], p, [shard(n) for n, _ in BIG[1:]])

    out_g, out_d, out_m, out_v = {}, {}, {}, {}
    for n, _ in BIG:
        g_, d_, m_, v_ = _adamw_sum(big[n], w[n][0], m[n][0], v[n][0], "adamw_" + n)
        out_g[n], out_d[n], out_m[n], out_v[n] = g_[None], d_[None], m_[None], v_[None]

    full = lambda d: dict(d, w_alpha_up=jnp.zeros((1, GLA_R, GLA_K), F32))
    gs, ds, ms, vs, lrow = _small_allreduce(small, _pack_small(full(w)), _pack_small(full(m)), _pack_small(full(v)))
    g_s, d_s, m_s, v_s = _unpack_small(gs), _unpack_small(ds), _unpack_small(ms), _unpack_small(vs)
    for n in names:
        if n not in out_g and n != "w_alpha_up":
            out_g[n], out_d[n], out_m[n], out_v[n] = g_s[n], d_s[n], m_s[n], v_s[n]
    g_au = lax.dynamic_slice_in_dim(g_s["w_alpha_up"][0], me * (GLA_K // N_DEV), GLA_K // N_DEV, axis=1)
    g_, d_, m_, v_ = _adamw_sum(g_au[None], w_alpha_up[0], m_w_alpha_up[0], v_w_alpha_up[0], "adamw_w_alpha_up")
    out_g["w_alpha_up"], out_d["w_alpha_up"], out_m["w_alpha_up"], out_v["w_alpha_up"] = g_[None], d_[None], m_[None], v_[None]

    loss = jnp.sum(lrow) * (0.5 / D)
    return (loss, dx[None], *[out_g[n] for n in names], *[out_d[n] for n in names], *[out_m[n] for n in names],
            *[out_v[n] for n in names])
```

```python
import jax
import jax.numpy as jnp
from jax import lax
from jax.experimental import pallas as pl
from jax.experimental.pallas import tpu as pltpu

F32, BF16 = jnp.float32, jnp.bfloat16
HIGHEST = lax.Precision.HIGHEST
MESH = pl.DeviceIdType.MESH

N_DEV = 8
D = 1024
EPS = 1e-6
CHUNK = 64
N_MEM = 256
GLA_H, GLA_DK, GLA_DV = 4, 64, 128
GLA_K, GLA_V, GLA_R = 256, 512, 16
FOX_H, FOX_DH, FOX_W = 8, 64, 512
MEM_H, MEM_DH, MEM_W = 4, 128, 512
D_FF = 4096
D_IN = 6680
FOX_SCALE = 0.125
GLA_SCALE = 0.125
MEM_SCALE = MEM_DH ** -0.5
GLA_TAU_INV = 1.0 / 16.0
NEG = -1e30

O_GQ, O_GK, O_GV, O_GG, O_GA, O_FQ, O_FK, O_FV, O_FF, O_MQ, O_GT = 0, 256, 512, 1024, 1536, 1552, 2064, 2576, 3088, 3096, 3608
A_FQ, A_FK, A_FV, A_MQ, A_W = 1024, 2048, 3072, 3584, 4096
S_W = 640
G_W = 3072
P_FOX, P_FOX_W, P_MQ, P_GT, P_GLA, P_GLA_W, P_W = 0, 2560, 2560, 3072, 6144, 2048, 8192
FF_LANE = 16
AUG = 64
FOX_LIVE = 80
FOX_VT = 80

ADAM_LR, ADAM_B1, ADAM_B2, ADAM_EPS, ADAM_WD, ADAM_STEP = 0.001, 0.9, 0.999, 1e-08, 0.01, 10
V7X_VMEM_LIMIT = 54 * 1024 * 1024
FOX_TK = 512
FOX_TQ = 2048
FOX_BWD_WIDE = 512


def _params(*sem):
    return pltpu.CompilerParams(dimension_semantics=sem, vmem_limit_bytes=V7X_VMEM_LIMIT)


def _nt(a, b):
    return lax.dot_general(a, b, (((1,), (1,)), ((), ())), preferred_element_type=F32)


def _tn(a, b):
    return lax.dot_general(a, b, (((0,), (0,)), ((), ())), preferred_element_type=F32)


def _nn(a, b):
    return jnp.dot(a, b, preferred_element_type=F32)


def _log_sigmoid(z):
    return jnp.minimum(z, 0.0) - jnp.log(1.0 + jnp.exp(-jnp.abs(z)))


def _sum01(m01, x):
    x1 = x.astype(BF16)
    x2 = (x - x1.astype(F32)).astype(BF16)
    x3 = (x - x1.astype(F32) - x2.astype(F32)).astype(BF16)
    return _nn(m01, x1) + _nn(m01, x2) + _nn(m01, x3)


def _sum8(x):
    return x.reshape(x.shape[0] // 8, 8, x.shape[1]).sum(axis=0)


def _rms(xv):
    r = lax.rsqrt(jnp.mean(xv * xv, axis=-1, keepdims=True) + EPS)
    return r, xv * r


def _rms_bwd(du, g, r, xh):
    w = du * g
    return r * (w - xh * jnp.mean(w * xh, axis=-1, keepdims=True))


def _row_chunks(n, size=256):
    return [slice(r, r + min(size, n)) for r in range(0, n, min(size, n))]


def _tile(n, pref):
    t = min(n, pref)
    assert n % t == 0, (n, t)
    return t


def _proj(x, g, wa, wg, ws, shards):
    S = x.shape[0]
    tm, tn = _tile(S, 1024), 1024
    n_a, n_g = A_W // tn, G_W // tn
    n_i, n_j = S // tm, n_a + n_g + 1
    n_x = len(shards)

    def body(*refs):
        x_ref, g_ref, wa_ref, wg_ref, ws_ref = refs[:5]
        pa_ref, pg_ref, ps_ref, u_ref = refs[5 + n_x:9 + n_x]
        u_s = refs[9 + 2 * n_x]
        gather = lambda: _AllToAll(refs[5:5 + n_x], refs[9 + n_x:9 + 2 * n_x], refs[10 + 2 * n_x:], True)
        i, j = pl.program_id(0), pl.program_id(1)

        @pl.when((i == 0) & (j == 0))
        def _():
            gather().start()

        @pl.when(j == 0)
        def _():
            r, xh = _rms(x_ref[...])
            u_s[...] = (xh * g_ref[...]).astype(BF16)
            u_ref[...] = u_s[...]

        @pl.when(j < n_a)
        def _():
            pa_ref[...] = _nn(u_s[...], wa_ref[...]).astype(BF16)

        @pl.when((j >= n_a) & (j < n_a + n_g))
        def _():
            pg_ref[...] = _nn(u_s[...], wg_ref[...]).astype(BF16)

        @pl.when(j == n_a + n_g)
        def _():
            ps_ref[...] = _nn(u_s[...], ws_ref[...])

        @pl.when((i == n_i - 1) & (j == n_j - 1))
        def _():
            gather().wait()

    in_a = lambda j: jnp.minimum(j, n_a - 1)
    in_g = lambda j: jnp.clip(j - n_a, 0, n_g - 1)
    row = pl.BlockSpec((tm, D), lambda i, j: (i, 0))
    any_spec = pl.BlockSpec(memory_space=pl.ANY)
    out = pl.pallas_call(
        body, name="proj", grid=(n_i, n_j),
        in_specs=[row, pl.BlockSpec((1, D), lambda i, j: (0, 0)), pl.BlockSpec((D, tn), lambda i, j: (0, in_a(j))),
                  pl.BlockSpec((D, tn), lambda i, j: (0, in_g(j))),
                  pl.BlockSpec((D, S_W), lambda i, j: (0, 0), pipeline_mode=pl.Buffered(1))] + [any_spec] * n_x,
        out_specs=[pl.BlockSpec((tm, tn), lambda i, j: (i, in_a(j))), pl.BlockSpec((tm, tn), lambda i, j: (i, in_g(j))),
                   pl.BlockSpec((tm, S_W), lambda i, j: (i, 0)), row] + [any_spec] * n_x,
        out_shape=[jax.ShapeDtypeStruct((S, A_W), BF16), jax.ShapeDtypeStruct((S, G_W), BF16),
                   jax.ShapeDtypeStruct((S, S_W), F32), jax.ShapeDtypeStruct((S, D), BF16)] + _gathered_shapes(shards),
        scratch_shapes=[pltpu.VMEM((tm, D), BF16)] + _exchange_sems(n_x),
        compiler_params=_params("arbitrary", "arbitrary"),
    )(x, g, wa, wg, ws, *shards)
    return out[0], out[1], out[2], out[3], out[4:]


def _wgrad(a, b, name, slab_axis=None):
    S, Ka = a.shape
    N = b.shape[1]
    tka, tn, ts = _tile(Ka, 1024), _tile(N, 1024), _tile(S, 4096)
    n_s = S // ts
    per = N // N_DEV
    slabs_per_step = tn // per

    def body(a_ref, b_ref, o_ref, acc):
        s = pl.program_id(2)

        @pl.when(s == 0)
        def _():
            acc[...] = jnp.zeros_like(acc)

        acc[...] += _tn(a_ref[...].astype(BF16), b_ref[...].astype(BF16))

        @pl.when(s == n_s - 1)
        def _():
            if slab_axis == 1:
                for q in range(slabs_per_step):
                    o_ref[q] = acc[:, per * q:per * (q + 1)].astype(BF16)
            else:
                o_ref[...] = acc[...].astype(o_ref.dtype)

    if slab_axis == 1:
        out_spec = pl.BlockSpec((slabs_per_step, tka, per), lambda i, j, s: (j, i, 0))
        out_shape = jax.ShapeDtypeStruct((N_DEV, Ka, per), BF16)
    else:
        out_spec = pl.BlockSpec((tka, tn), lambda i, j, s: (i, j))
        out_shape = jax.ShapeDtypeStruct((Ka, N), BF16)
    out = pl.pallas_call(
        body, name=name, grid=(Ka // tka, N // tn, n_s),
        in_specs=[pl.BlockSpec((ts, tka), lambda i, j, s: (s, i)), pl.BlockSpec((ts, tn), lambda i, j, s: (s, j))],
        out_specs=out_spec, out_shape=out_shape,
        scratch_shapes=[pltpu.VMEM((tka, tn), F32)],
        compiler_params=_params("arbitrary", "arbitrary", "arbitrary"),
    )(a, b)
    return out.reshape(N_DEV, Ka // N_DEV, N) if slab_axis == 0 else out


def _nt_rmsbwd(a, w, xin, g, dres, name, emit_bf16, slabs=()):
    S, K = a.shape
    tm, tk = _tile(S, 1024), _tile(K, 1024 if emit_bf16 else 2048)
    n_i, n_k = S // tm, K // tk
    n_x, n_o = len(slabs), 3 if emit_bf16 else 2

    def body(*refs):
        a_ref, w_ref, x_ref, g_ref, r_ref = refs[:5]
        o_ref = refs[5 + n_x]
        rest = refs[6 + n_x:5 + n_x + n_o] + (refs[5 + 2 * n_x + n_o],)
        dg_ref, acc = rest[-2], rest[-1]
        scatter = lambda: _AllToAll(refs[5:5 + n_x], refs[5 + n_x + n_o:5 + 2 * n_x + n_o], refs[6 + 2 * n_x + n_o:], False)
        i, k = pl.program_id(0), pl.program_id(1)

        if n_x:
            @pl.when((i == 0) & (k == 0))
            def _():
                scatter().start()

        @pl.when(k == 0)
        def _():
            acc[...] = jnp.zeros_like(acc)

        acc[...] += _nt(a_ref[...], w_ref[...])

        @pl.when(k == n_k - 1)
        def _():
            @pl.when(i == 0)
            def _():
                dg_ref[...] = jnp.zeros_like(dg_ref)

            for rows in _row_chunks(tm):
                du = acc[rows, :]
                r, xh = _rms(x_ref[rows, :])
                out = r_ref[rows, :] + _rms_bwd(du, g_ref[...], r, xh)
                o_ref[rows, :] = out
                if emit_bf16:
                    rest[0][rows, :] = out.astype(BF16)
                dg_ref[...] += _sum8(du * xh)

        if n_x:
            @pl.when((i == n_i - 1) & (k == n_k - 1))
            def _():
                scatter().wait()

    row = pl.BlockSpec((tm, D), lambda i, k: (i, 0))
    any_spec = pl.BlockSpec(memory_space=pl.ANY)
    out_shape = [jax.ShapeDtypeStruct((S, D), F32)]
    out_specs = [row]
    if emit_bf16:
        out_shape.append(jax.ShapeDtypeStruct((S, D), BF16))
        out_specs.append(row)
    out_shape.append(jax.ShapeDtypeStruct((8, D), F32))
    out_specs.append(pl.BlockSpec((8, D), lambda i, k: (0, 0)))
    out = pl.pallas_call(
        body, name=name, grid=(n_i, n_k),
        in_specs=[pl.BlockSpec((tm, tk), lambda i, k: (i, k)), pl.BlockSpec((D, tk), lambda i, k: (0, k)),
                  row, pl.BlockSpec((1, D), lambda i, k: (0, 0)), row] + [any_spec] * n_x,
        out_specs=out_specs + [any_spec] * n_x,
        out_shape=out_shape + [jax.ShapeDtypeStruct(b.shape, b.dtype) for b in slabs],
        scratch_shapes=[pltpu.VMEM((tm, D), F32)] + (_exchange_sems(n_x) if n_x else []),
        compiler_params=_params("arbitrary", "arbitrary"),
    )(a, w, xin, g, dres, *slabs)
    return (*out[:n_o], out[n_o:]) if n_x else out


def _merge(og, ofox, omem, wg, wf, wm, pg):
    S = og.shape[0]
    tm = _tile(S, 512)

    def body(og_ref, of_ref, om_ref, wg_ref, wf_ref, wm_ref, pg_ref, y_ref, mg_ref):
        tot = None
        for i, (o_ref, w_ref) in enumerate(((og_ref, wg_ref), (of_ref, wf_ref), (om_ref, wm_ref))):
            y = _nn(o_ref[...].astype(BF16), w_ref[...])
            y_ref[i] = y.astype(BF16)
            t = jax.nn.sigmoid(pg_ref[:, D * i:D * (i + 1)].astype(F32)) * y
            tot = t if tot is None else tot + t
        mg_ref[...] = tot.astype(BF16)

    o_spec = pl.BlockSpec((tm, 512), lambda i: (i, 0))
    w_spec = pl.BlockSpec((512, D), lambda i: (0, 0))
    return pl.pallas_call(
        body, name="merge", grid=(S // tm,),
        in_specs=[o_spec, o_spec, o_spec, w_spec, w_spec, w_spec, pl.BlockSpec((tm, G_W), lambda i: (i, 0))],
        out_specs=[pl.BlockSpec((3, tm, D), lambda i: (0, i, 0)), pl.BlockSpec((tm, D), lambda i: (i, 0))],
        out_shape=[jax.ShapeDtypeStruct((3, S, D), BF16), jax.ShapeDtypeStruct((S, D), BF16)],
        compiler_params=_params("arbitrary"),
    )(og, ofox, omem, wg, wf, wm, pg)


def _out_proj(mg, w_out, x, g_ffn):
    S = x.shape[0]
    tm = _tile(S, 1024)

    def body(mg_ref, w_ref, x_ref, g_ref, h_ref, u_ref):
        h = x_ref[...] + _nn(mg_ref[...], w_ref[...])
        h_ref[...] = h
        r, xh = _rms(h)
        u_ref[...] = (xh * g_ref[...]).astype(BF16)

    row = pl.BlockSpec((tm, D), lambda i: (i, 0))
    return pl.pallas_call(
        body, name="out_proj", grid=(S // tm,),
        in_specs=[row, pl.BlockSpec((D, D), lambda i: (0, 0)), row, pl.BlockSpec((1, D), lambda i: (0, 0))],
        out_specs=[row, row],
        out_shape=[jax.ShapeDtypeStruct((S, D), F32), jax.ShapeDtypeStruct((S, D), BF16)],
        compiler_params=_params("arbitrary"),
    )(mg, w_out, x, g_ffn)


def _ff1(u2, w1):
    S = u2.shape[0]
    tm, tn = _tile(S, 2048), 1024

    def body(u_ref, w_ref, a_ref, act_ref):
        a = _nn(u_ref[...], w_ref[...])
        a_ref[...] = a.astype(BF16)
        act_ref[...] = jnp.square(jnp.maximum(a, 0.0)).astype(BF16)

    blk = pl.BlockSpec((tm, tn), lambda i, j: (i, j))
    return pl.pallas_call(
        body, name="ff1", grid=(S // tm, D_FF // tn),
        in_specs=[pl.BlockSpec((tm, D), lambda i, j: (i, 0)), pl.BlockSpec((D, tn), lambda i, j: (0, j))],
        out_specs=[blk, blk],
        out_shape=[jax.ShapeDtypeStruct((S, D_FF), BF16), jax.ShapeDtypeStruct((S, D_FF), BF16)],
        compiler_params=_params("arbitrary", "arbitrary"),
    )(u2, w1)


def _ff2_loss(act, w2, h1, g_final, target):
    S = act.shape[0]
    tm, tk = _tile(S, 1024), 1024
    n_k = D_FF // tk

    def body(a_ref, w_ref, h_ref, g_ref, t_ref, d_ref, db_ref, ls_ref, dg_ref, acc):
        i, k = pl.program_id(0), pl.program_id(1)

        @pl.when(k == 0)
        def _():
            acc[...] = jnp.zeros_like(acc)

        acc[...] += _nn(a_ref[...], w_ref[...])

        @pl.when(k == n_k - 1)
        def _():
            @pl.when(i == 0)
            def _():
                ls_ref[...] = jnp.zeros_like(ls_ref)
                dg_ref[...] = jnp.zeros_like(dg_ref)

            gf = g_ref[...]
            for rows in _row_chunks(tm):
                r, xh = _rms(h_ref[rows, :] + acc[rows, :])
                err = xh * gf - t_ref[rows, :]
                dy = err * (1.0 / D)
                dh = _rms_bwd(dy, gf, r, xh)
                d_ref[rows, :] = dh
                db_ref[rows, :] = dh.astype(BF16)
                ls_ref[...] += _sum8(err * err)
                dg_ref[...] += _sum8(dy * xh)

    row = pl.BlockSpec((tm, D), lambda i, k: (i, 0))
    part = pl.BlockSpec((8, D), lambda i, k: (0, 0))
    return pl.pallas_call(
        body, name="ff2_loss", grid=(S // tm, n_k),
        in_specs=[pl.BlockSpec((tm, tk), lambda i, k: (i, k)), pl.BlockSpec((tk, D), lambda i, k: (k, 0)),
                  row, pl.BlockSpec((1, D), lambda i, k: (0, 0)), row],
        out_specs=[row, row, part, part],
        out_shape=[jax.ShapeDtypeStruct((S, D), F32), jax.ShapeDtypeStruct((S, D), BF16),
                   jax.ShapeDtypeStruct((8, D), F32), jax.ShapeDtypeStruct((8, D), F32)],
        scratch_shapes=[pltpu.VMEM((tm, D), F32)],
        compiler_params=_params("arbitrary", "arbitrary"),
    )(act, w2, h1, g_final, target)


def _dact(dh2b, w2, a):
    S = a.shape[0]
    tm, tn = _tile(S, 2048), 1024

    def body(d_ref, w_ref, a_ref, o_ref):
        da = _nt(d_ref[...], w_ref[...])
        o_ref[...] = (da * (2.0 * jnp.maximum(a_ref[...].astype(F32), 0.0))).astype(BF16)

    blk = pl.BlockSpec((tm, tn), lambda i, j: (i, j))
    return pl.pallas_call(
        body, name="dact", grid=(S // tm, D_FF // tn),
        in_specs=[pl.BlockSpec((tm, D), lambda i, j: (i, 0)), pl.BlockSpec((tn, D), lambda i, j: (j, 0)), blk],
        out_specs=blk, out_shape=jax.ShapeDtypeStruct((S, D_FF), BF16),
        compiler_params=_params("arbitrary", "arbitrary"),
    )(dh2b, w2, a)


def _dmerge(dh1b, w_out, pg, y3, wg, wf, wm, slabs):
    S = dh1b.shape[0]
    tm = _tile(S, 512)
    n_i, n_x = S // tm, len(slabs)

    def body(*refs):
        d_ref, w_ref, pg_ref, y_ref, wg_ref, wf_ref, wm_ref = refs[:7]
        outs = refs[7 + n_x:14 + n_x]
        scatter = lambda: _AllToAll(refs[7:7 + n_x], refs[14 + n_x:14 + 2 * n_x], refs[14 + 2 * n_x:], False)
        dy_refs, do_refs, dg_ref = outs[0:3], outs[3:6], outs[6]

        @pl.when(pl.program_id(0) == 0)
        def _():
            scatter().start()

        dm = _nt(d_ref[...], w_ref[...])
        for i, wo_ref in enumerate((wg_ref, wf_ref, wm_ref)):
            gt = jax.nn.sigmoid(pg_ref[:, D * i:D * (i + 1)].astype(F32))
            dy = (dm * gt).astype(BF16)
            dy_refs[i][...] = dy
            do_refs[i][...] = _nt(dy, wo_ref[...])
            dg_ref[:, D * i:D * (i + 1)] = (dm * y_ref[i].astype(F32) * (gt * (1.0 - gt))).astype(BF16)

        @pl.when(pl.program_id(0) == n_i - 1)
        def _():
            scatter().wait()

    row = pl.BlockSpec((tm, D), lambda i: (i, 0))
    half = pl.BlockSpec((tm, 512), lambda i: (i, 0))
    w_spec = pl.BlockSpec((512, D), lambda i: (0, 0))
    any_spec = pl.BlockSpec(memory_space=pl.ANY)
    out = pl.pallas_call(
        body, name="dmerge", grid=(n_i,),
        in_specs=[row, pl.BlockSpec((D, D), lambda i: (0, 0)), pl.BlockSpec((tm, G_W), lambda i: (i, 0)),
                  pl.BlockSpec((3, tm, D), lambda i: (0, i, 0)), w_spec, w_spec, w_spec] + [any_spec] * n_x,
        out_specs=[row, row, row, half, half, half, pl.BlockSpec((tm, G_W), lambda i: (i, P_GT // G_W))] + [any_spec] * n_x,
        out_shape=[jax.ShapeDtypeStruct((S, D), BF16)] * 3 + [jax.ShapeDtypeStruct((S, 512), F32)] * 3
        + [jax.ShapeDtypeStruct((S, P_W), BF16)] + [jax.ShapeDtypeStruct(b.shape, b.dtype) for b in slabs],
        scratch_shapes=_exchange_sems(n_x),
        compiler_params=_params("arbitrary"),
    )(dh1b, w_out, pg, y3, wg, wf, wm, *slabs)
    return (*out[:7], out[7:])


def _gla_block_terms(gq_ref, gk_ref, ps_ref, wau_ref, ba_ref, tb):
    gaff = ps_ref[:, 512:640]
    z = _nn(gaff.astype(BF16), wau_ref[...]) + ba_ref[...]
    la = _log_sigmoid(z) * GLA_TAU_INV
    rr = lax.broadcasted_iota(jnp.int32, (tb, tb), 0)
    cc = lax.broadcasted_iota(jnp.int32, (tb, tb), 1)
    same = jnp.right_shift(rr, 6) == jnp.right_shift(cc, 6)
    tri = jnp.where(same & (cc <= rr), 1.0, 0.0).astype(BF16)
    ones = jnp.where(same, 1.0, 0.0).astype(BF16)
    b = _sum01(tri, la)
    bl = _sum01(ones, la)
    e_pos, e_neg, e_last, dec = jnp.exp(b), jnp.exp(-b), jnp.exp(bl - b), jnp.exp(bl)
    q = gq_ref[...].astype(F32) * GLA_SCALE
    k = gk_ref[...].astype(F32)
    return dict(gaff=gaff, z=z, same=same, rr=rr, cc=cc, ones=ones, e_pos=e_pos, e_neg=e_neg, e_last=e_last, dec=dec,
                qp=q * e_pos, qn=q * e_neg, kn=k * e_neg, kp=k * e_pos, kd=k * e_last)


def _head_masked(x, store):
    lane = lax.broadcasted_iota(jnp.int32, x.shape, 1)
    for h in range(GLA_H):
        store[:, h] = jnp.where(jnp.right_shift(lane, 6) == h, x, 0.0).astype(BF16).reshape(-1, CHUNK, GLA_K)


def _lower4():
    t = jnp.bitwise_and(lax.broadcasted_iota(jnp.int32, (GLA_H * CHUNK, CHUNK), 0), CHUNK - 1)
    return t >= lax.broadcasted_iota(jnp.int32, (GLA_H * CHUNK, CHUNK), 1)


def _stack_heads(ref, rows):
    return jnp.concatenate([ref[rows, GLA_DV * h:GLA_DV * (h + 1)] for h in range(GLA_H)], axis=0)


def _gla_fwd(pa, ps, wau, ba, gh):
    S = pa.shape[0]
    tb = _tile(S, 512)
    n_c = tb // CHUNK
    n_b = S // tb

    def body(gq_ref, gk_ref, gv_ref, ps_ref, wau_ref, ba_ref, gh_ref, o_ref, og_ref, sp_ref,
             qpm, qnm, kdm, kn_s, kp_s, dec_s, state):
        @pl.when(pl.program_id(0) == 0)
        def _():
            state[...] = jnp.zeros_like(state)

        t = _gla_block_terms(gq_ref, gk_ref, ps_ref, wau_ref, ba_ref, tb)
        _head_masked(t["qp"], qpm)
        _head_masked(t["qn"], qnm)
        _head_masked(t["kd"], kdm)
        kn_s[...] = t["kn"].astype(BF16)
        kp_s[...] = t["kp"].astype(BF16)
        dec_s[...] = t["dec"]
        lower = _lower4()

        sp = state[...]
        for c in range(n_c):
            rows = slice(c * CHUNK, (c + 1) * CHUNK)
            sp_ref[c] = sp
            qp, qn, kd = (s[c].reshape(GLA_H * CHUNK, GLA_K) for s in (qpm, qnm, kdm))
            attn = jnp.where(lower, _nt(qp, kn_s[rows, :]), _nt(qn, kp_s[rows, :])).astype(BF16)
            inter = _nt(qp, sp.astype(BF16))
            for h in range(GLA_H):
                mine = slice(CHUNK * h, CHUNK * (h + 1))
                cols = slice(GLA_DV * h, GLA_DV * (h + 1))
                o_ref[rows, cols] = _nn(attn[mine], gv_ref[rows, cols]) + inter[mine]
            sp = sp * dec_s[c * CHUNK:c * CHUNK + 1, :] + _tn(_stack_heads(gv_ref, rows), kd)
        state[...] = sp
        for h in range(GLA_H):
            cols = slice(GLA_DV * h, GLA_DV * (h + 1))
            r, xh = _rms(o_ref[:, cols])
            gg = ps_ref[:, cols]
            og_ref[:, cols] = ((xh * gh_ref[:, cols]) * (gg * jax.nn.sigmoid(gg))).astype(BF16)

    return pl.pallas_call(
        body, name="gla_fwd", grid=(n_b,),
        in_specs=[pl.BlockSpec((tb, GLA_K), lambda i: (i, 0)), pl.BlockSpec((tb, GLA_K), lambda i: (i, 1)),
                  pl.BlockSpec((tb, GLA_V), lambda i: (i, 1)), pl.BlockSpec((tb, S_W), lambda i: (i, 0)),
                  pl.BlockSpec((128, GLA_K), lambda i: (0, 0)), pl.BlockSpec((1, GLA_K), lambda i: (0, 0)),
                  pl.BlockSpec((1, GLA_V), lambda i: (0, 0))],
        out_specs=[pl.BlockSpec((tb, GLA_V), lambda i: (i, 0)), pl.BlockSpec((tb, GLA_V), lambda i: (i, 0)),
                   pl.BlockSpec((n_c, GLA_DV, GLA_K), lambda i: (i, 0, 0))],
        out_shape=[jax.ShapeDtypeStruct((S, GLA_V), F32), jax.ShapeDtypeStruct((S, GLA_V), BF16),
                   jax.ShapeDtypeStruct((S // CHUNK, GLA_DV, GLA_K), F32)],
        scratch_shapes=[pltpu.VMEM((n_c, GLA_H, CHUNK, GLA_K), BF16)] * 3
        + [pltpu.VMEM((tb, GLA_K), BF16), pltpu.VMEM((tb, GLA_K), BF16), pltpu.VMEM((tb, GLA_K), F32),
           pltpu.VMEM((GLA_DV, GLA_K), F32)],
        compiler_params=_params("arbitrary"),
    )(pa, pa, pa, ps, wau, ba, gh)


def _gla_bwd(pa, ps, wau, ba, gh, o_gla, d_og, sprev, dgaff_fox, d_proj, slabs):
    S = pa.shape[0]
    tb = _tile(S, 512)
    n_c = tb // CHUNK
    n_b = S // tb
    n_x = len(slabs)
    c_gk, c_gv, c_gg, c_ga, c_end = GLA_K, 2 * GLA_K, 2 * GLA_K + GLA_V, 2 * GLA_K + 2 * GLA_V, 2 * GLA_K + 2 * GLA_V + 128

    def body(*refs):
        gq_ref, gk_ref, gv_ref, ps_ref, wau_ref, ba_ref, gh_ref, o_ref, dog_ref, sp_ref, dfx_ref = refs[:11]
        dp_ref, dwau_ref, dba_ref, dgh_ref = refs[12 + n_x:16 + n_x]
        (qpm, qnm, kdm, kn_s, kp_s, dec_s, do_s, dqp_s, dqn_s, dkn_s, dkp_s, dkd_s, ddec_s,
         dstate) = refs[16 + 2 * n_x:30 + 2 * n_x]
        scatter = lambda: _AllToAll(refs[12:12 + n_x], refs[16 + n_x:16 + 2 * n_x], refs[30 + 2 * n_x:], False)
        first = pl.program_id(0) == 0
        dp_ref[:, c_end:] = jnp.zeros((tb, P_GLA_W - c_end), BF16)

        @pl.when(first)
        def _():
            dstate[...] = jnp.zeros_like(dstate)
            scatter().start()

        t = _gla_block_terms(gq_ref, gk_ref, ps_ref, wau_ref, ba_ref, tb)
        _head_masked(t["qp"], qpm)
        _head_masked(t["qn"], qnm)
        _head_masked(t["kd"], kdm)
        kn_s[...] = t["kn"].astype(BF16)
        kp_s[...] = t["kp"].astype(BF16)
        dec_s[...] = t["dec"]

        dgh_parts = []
        for h in range(GLA_H):
            cols = slice(GLA_DV * h, GLA_DV * (h + 1))
            r, xh = _rms(o_ref[:, cols])
            g = gh_ref[:, cols]
            gg = ps_ref[:, cols]
            sg = jax.nn.sigmoid(gg)
            d_out = dog_ref[:, cols]
            dp_ref[:, c_gg + GLA_DV * h:c_gg + GLA_DV * (h + 1)] = (d_out * (xh * g) * (sg * (1.0 + gg * (1.0 - sg)))).astype(BF16)
            d_on = d_out * (gg * sg)
            dgh_parts.append(_sum8(d_on * xh))
            do_s[:, cols] = _rms_bwd(d_on, g, r, xh).astype(BF16)
        dgh_part = jnp.concatenate(dgh_parts, axis=1)

        lower = _lower4()
        lane = lax.broadcasted_iota(jnp.int32, (CHUNK, GLA_K), 1)

        def own_columns(stacked):
            return sum(jnp.where(jnp.right_shift(lane, 6) == h, stacked[CHUNK * h:CHUNK * (h + 1)], 0.0) for h in range(GLA_H))

        ds_next = dstate[...]
        for c in reversed(range(n_c)):
            rows = slice(c * CHUNK, (c + 1) * CHUNK)
            dsb = ds_next.astype(BF16)
            sp = sp_ref[c]
            knc, kpc = kn_s[rows, :], kp_s[rows, :]
            qp, qn, kd = (s[c].reshape(GLA_H * CHUNK, GLA_K) for s in (qpm, qnm, kdm))
            v4, do4 = _stack_heads(gv_ref, rows), _stack_heads(do_s, rows)
            ddec_s[rows, :] = jnp.broadcast_to(jnp.sum(ds_next * sp, axis=0, keepdims=True), (CHUNK, GLA_K))
            attn = jnp.where(lower, _nt(qp, knc), _nt(qn, kpc)).astype(BF16)
            da = jnp.concatenate([_nt(do4[CHUNK * h:CHUNK * (h + 1)], v4[CHUNK * h:CHUNK * (h + 1)]) for h in range(GLA_H)],
                                 axis=0)
            dac = jnp.where(lower, da, 0.0).astype(BF16)
            daa = jnp.where(lower, 0.0, da).astype(BF16)
            dqp_s[rows, :] = own_columns(_nn(dac, knc) + _nn(do4, sp.astype(BF16)))
            dqn_s[rows, :] = own_columns(_nn(daa, kpc))
            dkd_s[rows, :] = own_columns(_nn(v4, dsb))
            dkn_s[rows, :] = _tn(dac, qp)
            dkp_s[rows, :] = _tn(daa, qn)
            dv_state = _nt(kd, dsb)
            for h in range(GLA_H):
                mine = slice(CHUNK * h, CHUNK * (h + 1))
                dp_ref[rows, c_gv + GLA_DV * h:c_gv + GLA_DV * (h + 1)] = (_tn(attn[mine], do4[mine]) + dv_state[mine]).astype(BF16)
            ds_next = ds_next * dec_s[c * CHUNK:c * CHUNK + 1, :] + _tn(do4, qp)
        dstate[...] = ds_next

        dqp, dqn, dkn, dkp, dkd = dqp_s[...], dqn_s[...], dkn_s[...], dkp_s[...], dkd_s[...]
        dp_ref[:, 0:c_gk] = ((dqp * t["e_pos"] + dqn * t["e_neg"]) * GLA_SCALE).astype(BF16)
        dp_ref[:, c_gk:c_gv] = (dkn * t["e_neg"] + dkp * t["e_pos"] + dkd * t["e_last"]).astype(BF16)
        kd_term = dkd * t["kd"]
        db = dqp * t["qp"] - dqn * t["qn"] - dkn * t["kn"] + dkp * t["kp"] - kd_term
        upper = jnp.where(t["same"] & (t["cc"] >= t["rr"]), 1.0, 0.0).astype(BF16)
        dla = (_sum01(upper, db) + _sum01(t["ones"], kd_term)
               + ddec_s[...] * t["dec"])
        dz = dla * GLA_TAU_INV * jax.nn.sigmoid(-t["z"])
        dzb = dz.astype(BF16)
        dp_ref[:, c_ga:c_end] = (_nt(dzb, wau_ref[...]) + dfx_ref[...]).astype(BF16)
        dwau_part = _tn(t["gaff"].astype(BF16), dzb)
        dba_part = _sum8(dz)

        @pl.when(first)
        def _():
            dwau_ref[...] = dwau_part
            dba_ref[...] = dba_part
            dgh_ref[...] = dgh_part

        @pl.when(jnp.logical_not(first))
        def _():
            dwau_ref[...] += dwau_part
            dba_ref[...] += dba_part
            dgh_ref[...] += dgh_part

        @pl.when(pl.program_id(0) == n_b - 1)
        def _():
            scatter().wait()

    rev = lambda i: (n_b - 1 - i, 0)
    f32k = pltpu.VMEM((tb, GLA_K), F32)
    bf4 = pltpu.VMEM((n_c, GLA_H, CHUNK, GLA_K), BF16)
    any_spec = pl.BlockSpec(memory_space=pl.ANY)
    out = pl.pallas_call(
        body, name="gla_bwd", grid=(n_b,),
        in_specs=[pl.BlockSpec((tb, GLA_K), rev), pl.BlockSpec((tb, GLA_K), lambda i: (n_b - 1 - i, 1)),
                  pl.BlockSpec((tb, GLA_V), lambda i: (n_b - 1 - i, 1)), pl.BlockSpec((tb, S_W), rev),
                  pl.BlockSpec((128, GLA_K), lambda i: (0, 0)), pl.BlockSpec((1, GLA_K), lambda i: (0, 0)),
                  pl.BlockSpec((1, GLA_V), lambda i: (0, 0)), pl.BlockSpec((tb, GLA_V), rev), pl.BlockSpec((tb, GLA_V), rev),
                  pl.BlockSpec((n_c, GLA_DV, GLA_K), lambda i: (n_b - 1 - i, 0, 0)), pl.BlockSpec((tb, 128), rev),
                  any_spec] + [any_spec] * n_x,
        out_specs=[pl.BlockSpec((tb, P_GLA_W), lambda i: (n_b - 1 - i, P_GLA // P_GLA_W)),
                   pl.BlockSpec((128, GLA_K), lambda i: (0, 0)), pl.BlockSpec((8, GLA_K), lambda i: (0, 0)),
                   pl.BlockSpec((8, GLA_V), lambda i: (0, 0))] + [any_spec] * n_x,
        out_shape=[jax.ShapeDtypeStruct((S, P_W), BF16), jax.ShapeDtypeStruct((128, GLA_K), F32),
                   jax.ShapeDtypeStruct((8, GLA_K), F32), jax.ShapeDtypeStruct((8, GLA_V), F32)]
        + [jax.ShapeDtypeStruct(b.shape, b.dtype) for b in slabs],
        input_output_aliases={11: 0},
        scratch_shapes=[bf4, bf4, bf4, pltpu.VMEM((tb, GLA_K), BF16), pltpu.VMEM((tb, GLA_K), BF16), f32k,
                        pltpu.VMEM((tb, GLA_V), BF16), f32k, f32k, f32k, f32k, f32k, f32k, pltpu.VMEM((GLA_DV, GLA_K), F32)]
        + _exchange_sems(n_x),
        compiler_params=_params("arbitrary"),
    )(pa, pa, pa, ps, wau, ba, gh, o_gla, d_og, sprev, dgaff_fox, d_proj, *slabs)
    return out[0], out[1], out[2], out[3], out[4:]


def _split3(x):
    x1 = x.astype(BF16).astype(F32)
    x2 = (x - x1).astype(BF16).astype(F32)
    x3 = (x - x1 - x2).astype(BF16).astype(F32)
    return x1, x2, x3


def _fox_prep(pa, ps, bfg):
    S = pa.shape[0]
    tm = _tile(S, FOX_TK)

    def body(ps_ref, b_ref, fq_ref, fk_ref, fv_ref, q_ref, k_ref, qt_ref, kt_ref, vt_ref, st_ref, carry):
        @pl.when(pl.program_id(0) == 0)
        def _():
            carry[...] = jnp.zeros_like(carry)

        vt = fv_ref[...].astype(F32).T.astype(BF16)
        ones_row = jnp.where(lax.broadcasted_iota(jnp.int32, (FOX_VT - FOX_DH, tm), 0) == 0, 1.0, 0.0).astype(BF16)
        for h in range(FOX_H):
            vt_ref[FOX_VT * h:FOX_VT * h + FOX_DH, :] = vt[FOX_DH * h:FOX_DH * (h + 1), :]
            vt_ref[FOX_VT * h + FOX_DH:FOX_VT * (h + 1), :] = ones_row
        lf = _log_sigmoid(ps_ref[...] + b_ref[...])
        rr = lax.broadcasted_iota(jnp.int32, (tm, tm), 0)
        cc = lax.broadcasted_iota(jnp.int32, (tm, tm), 1)
        tri = jnp.where(cc <= rr, 1.0, 0.0).astype(F32)
        f = jnp.dot(tri, lf, preferred_element_type=F32, precision=HIGHEST) + carry[0:1, :]
        carry[...] = jnp.broadcast_to(f[tm - 1:tm, :], carry.shape)
        f1, f2, f3 = _split3(f)
        lane = lax.broadcasted_iota(jnp.int32, (tm, 128), 1)
        st_row = lax.broadcasted_iota(jnp.int32, (8, 128), 0)
        st_lane = lax.broadcasted_iota(jnp.int32, (8, 128), 1)
        stats = jnp.zeros((8, 128), F32)
        for h in range(FOX_H):
            cols = slice(128 * h, 128 * (h + 1))
            c = FF_LANE + h
            a1, a2, a3 = f1[:, c:c + 1], f2[:, c:c + 1], f3[:, c:c + 1]
            q = fq_ref[:, cols].astype(F32) * FOX_SCALE
            k = fk_ref[:, cols].astype(F32)
            fh = f[:, c:c + 1]
            vals = (jnp.max(jnp.sum(q * q, axis=-1, keepdims=True)), jnp.max(jnp.sum(k * k, axis=-1, keepdims=True)),
                    jnp.max(fh), jnp.min(fh), jnp.min(jnp.sum(q * k, axis=-1, keepdims=True)))
            for n, val in enumerate(vals):
                stats = jnp.where((st_row == h) & (st_lane == n), val, stats)
            for n, a in enumerate((a1, a2, a3)):
                q = jnp.where(lane == AUG + n, a, q)
                k = jnp.where(lane == AUG + 3 + n, -a, k)
            q = jnp.where((lane >= AUG + 3) & (lane < AUG + 6), 1.0, q)
            k = jnp.where((lane >= AUG) & (lane < AUG + 3), 1.0, k)
            q_ref[:, cols] = q.astype(BF16)
            k_ref[:, cols] = k.astype(BF16)
            qt_ref[cols, :] = q.T.astype(BF16)
            kt_ref[cols, :] = k.T.astype(BF16)
        st_ref[0] = stats

    wide = lambda j: pl.BlockSpec((tm, 1024), lambda i: (i, j))
    tall = lambda n: pl.BlockSpec((n, tm), lambda i: (0, i))
    return pl.pallas_call(
        body, name="fox_prep", grid=(S // tm,),
        in_specs=[pl.BlockSpec((tm, 128), lambda i: (i, 4)), pl.BlockSpec((1, 128), lambda i: (0, 0)), wide(1), wide(2),
                  pl.BlockSpec((tm, FOX_W), lambda i: (i, A_FV // FOX_W))],
        out_specs=[wide(0), wide(0), tall(1024), tall(1024), tall(FOX_H * FOX_VT), pl.BlockSpec((1, 8, 128), lambda i: (i, 0, 0))],
        out_shape=[jax.ShapeDtypeStruct((S, 1024), BF16), jax.ShapeDtypeStruct((S, 1024), BF16),
                   jax.ShapeDtypeStruct((1024, S), BF16), jax.ShapeDtypeStruct((1024, S), BF16),
                   jax.ShapeDtypeStruct((FOX_H * FOX_VT, S), BF16), jax.ShapeDtypeStruct((S // tm, 8, 128), F32)],
        scratch_shapes=[pltpu.VMEM((8, 128), F32)],
        compiler_params=_params("arbitrary"),
    )(ps, bfg, pa, pa, pa)


FOX_PRUNE_AT = -90.0


def _fox_live_ranges(stats, n_sub, ratio):
    n_b = stats.shape[0]
    q2, k2, f_max, f_min, own = (stats[:, :, n].T for n in range(5))
    slack = 0.01 * jnp.sqrt(q2 * k2) + 1e-5 * jnp.abs(f_max) + 1.0
    bound = (1.01 * jnp.sqrt(q2[:, :, None] * k2[:, None, :]) + (f_max + slack - own)[:, :, None]
             - (f_min - 1e-5 * jnp.abs(f_min))[:, None, :])
    blocks = jnp.arange(n_b)
    dead = (bound <= FOX_PRUNE_AT) & (blocks[None, :] < blocks[:, None])[None]
    dead_fwd = dead.reshape(FOX_H, n_b // n_sub, n_sub, n_b).all(axis=2)
    first = jnp.sum(jnp.cumprod(dead_fwd.astype(jnp.int32), axis=2), axis=2)
    last_live = n_b - 1 - jnp.sum(jnp.cumprod(dead[:, ::-1, :].astype(jnp.int32), axis=1), axis=1)
    first_wide = blocks // ratio + 1
    narrow_end = jnp.minimum(jnp.minimum(first_wide * ratio, n_b)[None], last_live + 1)
    wide_end = jnp.where(last_live >= (first_wide * ratio)[None], last_live // ratio + 1, first_wide[None])
    return first.astype(jnp.int32), narrow_end.astype(jnp.int32), wide_end.astype(jnp.int32)


def _fox_fwd(qa, ka, vt, first):
    S = qa.shape[0]
    tq = _tile(S, FOX_TQ)
    tk = _tile(tq, FOX_TK)
    n_sub = tq // tk

    def body(first_ref, q_ref, k_ref, vt_ref, o_ref, lse_ref):
        pair, i = pl.program_id(0), pl.program_id(1)
        both = lambda f: tuple(f(hh) for hh in range(2))

        def blk(j, carry, diag, heads=(0, 1)):
            ks = pl.ds(pl.multiple_of(j * tk, tk), tk)
            q0 = 0 if diag is None else diag * tk

            def head(hh):
                if hh not in heads:
                    return carry[hh]
                m, acc = carry[hh]
                mo, ao = m[:, q0:], acc[:, q0:]
                s = _nt(k_ref[ks, 128 * hh:128 * (hh + 1)], q_ref[q0:, 128 * hh:128 * (hh + 1)])
                if diag is not None:
                    live = lax.broadcasted_iota(jnp.int32, s.shape, 1) >= lax.broadcasted_iota(jnp.int32, s.shape, 0)
                    s = jnp.where(live, s, NEG)
                mn = jnp.maximum(mo, jnp.max(s, axis=0, keepdims=True))
                p = jnp.exp((s - mn).astype(BF16))
                an = jnp.exp(mo - mn) * ao + _nn(vt_ref[FOX_VT * hh:FOX_VT * (hh + 1), ks], p)
                if q0:
                    mn, an = (jnp.concatenate([old[:, :q0], new], axis=1) for old, new in ((m, mn), (acc, an)))
                return mn, an

            return both(head)

        one = (jnp.full((1, tq), NEG, F32), jnp.zeros((FOX_VT, tq), F32))
        past = i * n_sub
        f0, f1 = first_ref[2 * pair, i], first_ref[2 * pair + 1, i]
        join = jnp.maximum(f0, f1)
        solo = lambda hh: lambda c: lax.fori_loop(jnp.minimum(f0, f1), join, lambda j, cc: blk(j, cc, None, (hh,)), c)
        carry = lax.cond(f0 < f1, solo(0), solo(1), (one, one))
        n_both = past - join
        carry = lax.fori_loop(0, n_both // 2, lambda jj, c: blk(join + 2 * jj + 1, blk(join + 2 * jj, c, None), None), carry)
        carry = lax.cond(n_both % 2 == 1, lambda c: blk(past - 1, c, None), lambda c: c, carry)
        for d in range(n_sub):
            carry = blk(past + d, carry, d)
        (m0, a0), (m1, a1) = carry
        l0, l1 = a0[FOX_DH:FOX_DH + 1], a1[FOX_DH:FOX_DH + 1]
        o_ref[...] = jnp.concatenate([a0[:FOX_DH] / l0, a1[:FOX_DH] / l1], axis=0).T
        lse_ref[0, 0:1, :] = m0 + jnp.log(l0)
        lse_ref[0, 1:2, :] = m1 + jnp.log(l1)
        lse_ref[0, 2:8, :] = jnp.zeros((6, tq), F32)

    return pl.pallas_call(
        body, name="fox_fwd", grid=(FOX_H // 2, S // tq),
        in_specs=[pl.BlockSpec(memory_space=pltpu.SMEM), pl.BlockSpec((tq, 256), lambda p, i: (i, p)),
                  pl.BlockSpec((S, 256), lambda p, i: (0, p)), pl.BlockSpec((2 * FOX_VT, S), lambda p, i: (p, 0))],
        out_specs=[pl.BlockSpec((tq, 128), lambda p, i: (i, p)), pl.BlockSpec((1, 8, tq), lambda p, i: (p, 0, i))],
        out_shape=[jax.ShapeDtypeStruct((S, FOX_W), F32), jax.ShapeDtypeStruct((FOX_H // 2, 8, S), F32)],
        compiler_params=_params("arbitrary", "arbitrary"),
    )(first, qa, ka, vt)


def _fox_delta(d_o, o):
    S = o.shape[0]
    tm = _tile(S, 512)

    def body(d_ref, o_ref, db_ref, dbt_ref, dl_ref):
        d = d_ref[...]
        db_ref[...] = d.astype(BF16)
        dbt_ref[...] = d.T.astype(BF16)
        prod = d * o_ref[...]
        rr = lax.broadcasted_iota(jnp.int32, (8, 128), 0)
        cc = lax.broadcasted_iota(jnp.int32, (8, 128), 1)
        ind = jnp.where(jnp.right_shift(cc, 6) == rr, 1.0, 0.0).astype(F32)
        for p in range(FOX_H // 2):
            dl_ref[p] = lax.dot_general(ind, prod[:, 128 * p:128 * (p + 1)], (((1,), (1,)), ((), ())),
                                        preferred_element_type=F32, precision=HIGHEST)

    row = pl.BlockSpec((tm, FOX_W), lambda i: (i, 0))
    return pl.pallas_call(
        body, name="fox_delta", grid=(S // tm,),
        in_specs=[row, row],
        out_specs=[row, pl.BlockSpec((FOX_W, tm), lambda i: (0, i)), pl.BlockSpec((FOX_H // 2, 8, tm), lambda i: (0, 0, i))],
        out_shape=[jax.ShapeDtypeStruct((S, FOX_W), BF16), jax.ShapeDtypeStruct((FOX_W, S), BF16),
                   jax.ShapeDtypeStruct((FOX_H // 2, 8, S), F32)],
        compiler_params=_params("arbitrary"),
    )(d_o, o)


def _fox_bwd(qa, qat, ka, kat, pa, dob, dobt, lse, delta, narrow_end, wide_end):
    S = qa.shape[0]
    tk = _tile(S, FOX_TK)
    wide = _tile(S, FOX_BWD_WIDE)
    ratio = wide // tk
    n_wide = S // wide

    def body(ne_ref, we_ref, q_ref, qt_ref, k_ref, kt_ref, v_ref, do_ref, dot_ref, lse_ref, dl_ref, dq_ref, dk_ref, dv_ref):
        h, jb = pl.program_id(0), pl.program_id(1)
        hh = h % 2

        @pl.when(jb == 0)
        def _():
            dq_ref[...] = jnp.zeros_like(dq_ref)

        lane = lax.broadcasted_iota(jnp.int32, (tk, 128), 1)
        vm = jnp.where(jnp.right_shift(lane, 6) == hh, v_ref[...], jnp.zeros((), BF16))
        kb, ktb = k_ref[...], kt_ref[0:FOX_LIVE, :]
        mine = pl.ds(pl.multiple_of(hh * FOX_DH, FOX_DH), FOX_DH)

        def blk(ib, tq, carry, masked):
            dk, dv = carry
            qs = pl.ds(pl.multiple_of(ib * tq, tq), tq)
            p = jnp.exp(_nt(kb, q_ref[qs, :]) - lse_ref[0, pl.ds(hh, 1), qs])
            if masked:
                live = lax.broadcasted_iota(jnp.int32, p.shape, 1) >= lax.broadcasted_iota(jnp.int32, p.shape, 0)
                p = jnp.where(live, p, 0.0)
            ds = (p * (_nt(vm, do_ref[qs, :]) - dl_ref[0, pl.ds(hh, 1), qs])).astype(BF16)
            dq_ref[0:FOX_LIVE, qs] += _nn(ktb, ds)
            return dk + _nt(qt_ref[0:FOX_LIVE, qs], ds), dv + _nt(dot_ref[mine, qs], p.astype(BF16))

        carry = blk(jb, tk, (jnp.zeros((FOX_LIVE, tk), F32), jnp.zeros((FOX_DH, tk), F32)), True)
        first_wide = jb // ratio + 1
        carry = lax.fori_loop(jb + 1, ne_ref[h, jb], lambda ib, c: blk(ib, tk, c, False), carry)
        last_wide = we_ref[h, jb]
        rest = jnp.maximum(last_wide - first_wide, 0)
        carry = lax.fori_loop(0, rest // 2, lambda t, c: blk(first_wide + 2 * t + 1, wide, blk(first_wide + 2 * t, wide, c, False),
                                                             False), carry)
        dk, dv = lax.cond(rest % 2 == 1, lambda c: blk(last_wide - 1, wide, c, False), lambda c: c, carry)
        dk_ref[0:FOX_LIVE, :] = dk
        dk_ref[FOX_LIVE:, :] = jnp.zeros((128 - FOX_LIVE, tk), F32)
        dv_ref[...] = dv

    once = pl.Buffered(1)
    rows = pl.BlockSpec((1, 8, S), lambda h, j: (h // 2, 0, 0))
    return pl.pallas_call(
        body, name="fox_bwd", grid=(FOX_H, S // tk),
        in_specs=[pl.BlockSpec(memory_space=pltpu.SMEM), pl.BlockSpec(memory_space=pltpu.SMEM),
                  pl.BlockSpec((S, 128), lambda h, j: (0, h)), pl.BlockSpec((128, S), lambda h, j: (h, 0)),
                  pl.BlockSpec((tk, 128), lambda h, j: (j, h)), pl.BlockSpec((128, tk), lambda h, j: (h, j)),
                  pl.BlockSpec((tk, 128), lambda h, j: (j, A_FV // 128 + h // 2)),
                  pl.BlockSpec((S, 128), lambda h, j: (0, h // 2)), pl.BlockSpec((128, S), lambda h, j: (h // 2, 0)),
                  rows, rows],
        out_specs=[pl.BlockSpec((128, S), lambda h, j: (h, 0), pipeline_mode=once),
                   pl.BlockSpec((128, tk), lambda h, j: (h, j)), pl.BlockSpec((FOX_DH, tk), lambda h, j: (h, j))],
        out_shape=[jax.ShapeDtypeStruct((1024, S), F32), jax.ShapeDtypeStruct((1024, S), F32),
                   jax.ShapeDtypeStruct((FOX_W, S), F32)],
        compiler_params=_params("arbitrary", "arbitrary"),
    )(narrow_end, wide_end, qa, qat, ka, kat, pa, dob, dobt, lse, delta)


def _fox_post(dq, dk, dv, ps, bfg, d_proj):
    S = dq.shape[1]
    tm = _tile(S, 512)
    n_b = S // tm

    def body(dq_ref, dk_ref, dv_ref, ps_ref, b_ref, _, dp_ref, dff_ref, dbf_ref, carry):
        first = pl.program_id(0) == 0

        @pl.when(first)
        def _():
            carry[...] = jnp.zeros_like(carry)

        low = lax.broadcasted_iota(jnp.int32, (tm, 128), 1) < FOX_DH
        for h in range(FOX_H):
            blk = slice(128 * h, 128 * (h + 1))
            dp_ref[:, blk] = jnp.where(low, dq_ref[blk, :].T * FOX_SCALE, 0.0).astype(BF16)
            dp_ref[:, 1024 + 128 * h:1024 + 128 * (h + 1)] = jnp.where(low, dk_ref[blk, :].T, 0.0).astype(BF16)
        dp_ref[:, 2048:P_FOX_W] = dv_ref[...].T.astype(BF16)
        rr = lax.broadcasted_iota(jnp.int32, (FOX_H, 1024), 0)
        cc = lax.broadcasted_iota(jnp.int32, (FOX_H, 1024), 1)
        sel_k = jnp.where(cc == 128 * rr + AUG + 3, 1.0, 0.0).astype(F32)
        sel_q = jnp.where(cc == 128 * rr + AUG, 1.0, 0.0).astype(F32)
        g = (jnp.dot(sel_k, dk_ref[...], preferred_element_type=F32, precision=HIGHEST)
             - jnp.dot(sel_q, dq_ref[...], preferred_element_type=F32, precision=HIGHEST))
        t_from = lax.broadcasted_iota(jnp.int32, (tm, tm), 0)
        t_to = lax.broadcasted_iota(jnp.int32, (tm, tm), 1)
        later = jnp.where(t_from >= t_to, 1.0, 0.0).astype(F32)
        dlf = jnp.dot(-g, later, preferred_element_type=F32, precision=HIGHEST) + carry[:, 0:1]
        carry[...] = jnp.broadcast_to(dlf[:, 0:1], carry.shape)
        cols = jnp.concatenate([jnp.zeros((FF_LANE, tm), F32), dlf, jnp.zeros((128 - FF_LANE - FOX_H, tm), F32)], axis=0).T
        dff = cols * jax.nn.sigmoid(-(ps_ref[...] + b_ref[...]))
        dff_ref[...] = dff
        part = _sum8(dff)

        @pl.when(first)
        def _():
            dbf_ref[...] = part

        @pl.when(jnp.logical_not(first))
        def _():
            dbf_ref[...] += part

    rev = lambda i: (n_b - 1 - i, 0)
    tall = lambda n: pl.BlockSpec((n, tm), lambda i: (0, n_b - 1 - i))
    return pl.pallas_call(
        body, name="fox_post", grid=(n_b,),
        in_specs=[tall(1024), tall(1024), tall(FOX_W), pl.BlockSpec((tm, 128), lambda i: (n_b - 1 - i, 4)),
                  pl.BlockSpec((1, 128), lambda i: (0, 0)), pl.BlockSpec(memory_space=pl.ANY)],
        out_specs=[pl.BlockSpec((tm, P_FOX_W), lambda i: (n_b - 1 - i, P_FOX // P_FOX_W)), pl.BlockSpec((tm, 128), rev),
                   pl.BlockSpec((8, 128), lambda i: (0, 0))],
        out_shape=[jax.ShapeDtypeStruct((S, P_W), BF16), jax.ShapeDtypeStruct((S, 128), F32),
                   jax.ShapeDtypeStruct((8, 128), F32)],
        input_output_aliases={5: 0},
        scratch_shapes=[pltpu.VMEM((8, 128), F32)],
        compiler_params=_params("arbitrary"),
    )(dq, dk, dv, ps, bfg, d_proj)


def _mem_prep(mem, g_mem, wkv):
    def body(m_ref, g_ref, w_ref, mn_ref, kv_ref):
        r, xh = _rms(m_ref[...])
        mn = (xh * g_ref[...]).astype(BF16)
        mn_ref[...] = mn
        kv_ref[...] = _nn(mn, w_ref[...]).astype(BF16)

    return pl.pallas_call(
        body, name="mem_prep",
        out_shape=[jax.ShapeDtypeStruct((N_MEM, D), BF16), jax.ShapeDtypeStruct((N_MEM, 2 * MEM_W), BF16)],
        compiler_params=pltpu.CompilerParams(vmem_limit_bytes=V7X_VMEM_LIMIT),
    )(mem, g_mem, wkv)


def _mem_softmax(qh, kh):
    s = _nt(qh, kh) * MEM_SCALE
    e = jnp.exp(s - jnp.max(s, axis=-1, keepdims=True))
    return e / jnp.sum(e, axis=-1, keepdims=True)


def _mem_fwd(pa, mkv):
    S = pa.shape[0]
    tm = _tile(S, 1024)

    def body(q_ref, kv_ref, o_ref):
        for h in range(MEM_H):
            cols = slice(MEM_DH * h, MEM_DH * (h + 1))
            p = _mem_softmax(q_ref[:, cols], kv_ref[:, cols])
            o_ref[:, cols] = _nn(p.astype(BF16), kv_ref[:, MEM_W + MEM_DH * h:MEM_W + MEM_DH * (h + 1)])

    return pl.pallas_call(
        body, name="mem_fwd", grid=(S // tm,),
        in_specs=[pl.BlockSpec((tm, MEM_W), lambda i: (i, A_MQ // MEM_W)), pl.BlockSpec((N_MEM, 2 * MEM_W), lambda i: (0, 0))],
        out_specs=pl.BlockSpec((tm, MEM_W), lambda i: (i, 0)),
        out_shape=jax.ShapeDtypeStruct((S, MEM_W), F32),
        compiler_params=_params("arbitrary"),
    )(pa, mkv)


def _mem_bwd(pa, mkv, d_o, d_proj):
    S = pa.shape[0]
    tm = _tile(S, 1024)

    def body(q_ref, kv_ref, do_ref, _, dq_ref, dkv_ref):
        first = pl.program_id(0) == 0
        parts = []
        for h in range(MEM_H):
            cols = slice(MEM_DH * h, MEM_DH * (h + 1))
            vcols = slice(MEM_W + MEM_DH * h, MEM_W + MEM_DH * (h + 1))
            qh, kh = q_ref[:, cols], kv_ref[:, cols]
            p = _mem_softmax(qh, kh)
            dob = do_ref[:, cols].astype(BF16)
            dp = _nt(dob, kv_ref[:, vcols])
            ds = (p * (dp - jnp.sum(p * dp, axis=-1, keepdims=True)) * MEM_SCALE).astype(BF16)
            dq_ref[:, cols] = _nn(ds, kh).astype(BF16)
            parts.append((cols, _tn(ds, qh)))
            parts.append((vcols, _tn(p.astype(BF16), dob)))

        @pl.when(first)
        def _():
            for sl, v in parts:
                dkv_ref[:, sl] = v

        @pl.when(jnp.logical_not(first))
        def _():
            for sl, v in parts:
                dkv_ref[:, sl] += v

    return pl.pallas_call(
        body, name="mem_bwd", grid=(S // tm,),
        in_specs=[pl.BlockSpec((tm, MEM_W), lambda i: (i, A_MQ // MEM_W)), pl.BlockSpec((N_MEM, 2 * MEM_W), lambda i: (0, 0)),
                  pl.BlockSpec((tm, MEM_W), lambda i: (i, 0)), pl.BlockSpec(memory_space=pl.ANY)],
        out_specs=[pl.BlockSpec((tm, MEM_W), lambda i: (i, P_MQ // MEM_W)), pl.BlockSpec((N_MEM, 2 * MEM_W), lambda i: (0, 0))],
        out_shape=[jax.ShapeDtypeStruct((S, P_W), BF16), jax.ShapeDtypeStruct((N_MEM, 2 * MEM_W), F32)],
        input_output_aliases={3: 0},
        compiler_params=_params("arbitrary"),
    )(pa, mkv, d_o, d_proj)


def _mem_prep_bwd(mem, g_mem, mn, wkv, dkv):
    def body(m_ref, g_ref, mn_ref, w_ref, d_ref, dw_ref, dg_ref):
        db = d_ref[...].astype(BF16)
        dw_ref[...] = _tn(mn_ref[...], db).astype(BF16)
        r, xh = _rms(m_ref[...])
        dg_ref[...] = _sum8(_nt(db, w_ref[...]) * xh)

    dw, dg = pl.pallas_call(
        body, name="mem_prep_bwd",
        out_shape=[jax.ShapeDtypeStruct((D, 2 * MEM_W), BF16), jax.ShapeDtypeStruct((8, D), F32)],
        compiler_params=pltpu.CompilerParams(vmem_limit_bytes=V7X_VMEM_LIMIT),
    )(mem, g_mem, mn, wkv, dkv)
    return dw.reshape(N_DEV, D // N_DEV, 2 * MEM_W), dg


def _rearrange_w_in(w):
    def heads128(cols):
        blk = w[:, cols:cols + FOX_W].reshape(D, FOX_H, FOX_DH)
        return jnp.pad(blk, ((0, 0), (0, 0), (0, 128 - FOX_DH))).reshape(D, FOX_H * 128)

    fq, fk, fv, mq, wg = heads128(O_FQ), heads128(O_FK), w[:, O_FV:O_FF], w[:, O_MQ:O_GT], w[:, O_GT:]
    gaff = jnp.concatenate([w[:, O_GA:O_FQ], w[:, O_FF:O_MQ], jnp.zeros((D, 128 - GLA_R - FOX_H), w.dtype)], axis=1)
    wa = jnp.concatenate([w[:, O_GQ:O_GG], fq, fk, fv, mq], axis=1)
    ws = jnp.concatenate([w[:, O_GG:O_GA], gaff], axis=1)
    wp = jnp.concatenate([fq, fk, fv, mq, wg, w[:, O_GQ:O_GG], ws, jnp.zeros((D, P_W - P_GLA - 1024 - S_W), w.dtype)], axis=1)
    return wa, wg, ws, wp


def _restore_w_in_grad(dwp):
    def unheads(off):
        return dwp[:, off:off + FOX_H * 128].reshape(D, FOX_H, 128)[:, :, :FOX_DH].reshape(D, FOX_W)

    g0 = P_GLA + 1024
    return jnp.concatenate([
        dwp[:, P_GLA:g0], dwp[:, g0:g0 + 512], dwp[:, g0 + 512:g0 + 512 + GLA_R], unheads(P_FOX), unheads(P_FOX + 1024),
        dwp[:, P_FOX + 2048:P_FOX + P_FOX_W], dwp[:, g0 + 512 + GLA_R:g0 + 512 + GLA_R + FOX_H], dwp[:, P_MQ:P_GT],
        dwp[:, P_GT:P_GLA]], axis=1)


def _local_step(x, mem, target, p, late_shards):
    S = x.shape[0]
    p = dict(p)
    wa, wg, ws, wp = _rearrange_w_in(p["w_in"])
    wau = jnp.pad(p["w_alpha_up"], ((0, 128 - GLA_R), (0, 0)))
    bfg = jnp.pad(p["b_forget"], ((0, 0), (FF_LANE, 128 - FF_LANE - FOX_H)))
    gh = p["g_gla_head"].reshape(1, GLA_V)

    pa, pg, ps, u, gathered = _proj(x, p["g_mix"], wa, wg, ws, late_shards)
    p.update({n: _unslab(t, ax) for (n, ax), t in zip(BIG[1:], gathered)})
    o_gla, og, sprev = _gla_fwd(pa, ps, wau, p["b_alpha"], gh)
    qa, ka, qat, kat, vt, fox_stats = _fox_prep(pa, ps, bfg)
    fox_tk = _tile(S, FOX_TK)
    fox_first, fox_narrow_end, fox_wide_end = _fox_live_ranges(fox_stats, _tile(S, FOX_TQ) // fox_tk,
                                                               _tile(S, FOX_BWD_WIDE) // fox_tk)
    o_fox, lse = _fox_fwd(qa, ka, vt, fox_first)
    mn, mkv = _mem_prep(mem, p["g_mem"], p["w_mem_kv"])
    o_mem = _mem_fwd(pa, mkv)
    y3, mg = _merge(og, o_fox, o_mem, p["w_gla_o"], p["w_fox_o"], p["w_mem_o"], pg)
    h1, u2 = _out_proj(mg, p["w_out"], x, p["g_ffn"])
    a, act = _ff1(u2, p["w_ff1"])
    dh2, dh2b, loss8, dg_final = _ff2_loss(act, p["w_ff2"], h1, p["g_final"].reshape(1, D), target)

    d_a = _dact(dh2b, p["w_ff2"], a)
    dw_ff2 = _wgrad(act, dh2b, "wgrad_ff2", 0)
    dh1, dh1b, dg_ffn, arrived_ff2 = _nt_rmsbwd(d_a, p["w_ff1"], h1, p["g_ffn"], dh2, "dffn", True, [dw_ff2])
    dw_ff1 = _wgrad(u2, d_a, "wgrad_ff1", 1)
    dy_g, dy_f, dy_m, do_g, do_f, do_m, d_proj, arrived_ff1 = _dmerge(dh1b, p["w_out"], pg, y3, p["w_gla_o"], p["w_fox_o"],
                                                                      p["w_mem_o"], [dw_ff1])
    dw_out = _wgrad(mg, dh1b, "wgrad_out", 0)
    dw_gla_o = _wgrad(og, dy_g, "wgrad_gla_o", 1)
    dw_fox_o = _wgrad(o_fox, dy_f, "wgrad_fox_o", 1)
    dw_mem_o = _wgrad(o_mem, dy_m, "wgrad_mem_o", 1)
    d_proj, d_mkv = _mem_bwd(pa, mkv, do_m, d_proj)
    dw_mem_kv, dg_mem = _mem_prep_bwd(mem, p["g_mem"], mn, p["w_mem_kv"], d_mkv)
    dob, dobt, delta = _fox_delta(do_f, o_fox)
    dq, dk, dv = _fox_bwd(qa, qat, ka, kat, pa, dob, dobt, lse, delta, fox_narrow_end, fox_wide_end)
    d_proj, dgaff_fox, db_forget = _fox_post(dq, dk, dv, ps, bfg, d_proj)
    d_proj, dw_au, db_alpha, dg_gla, arrived = _gla_bwd(pa, ps, wau, p["b_alpha"], gh, o_gla, do_g, sprev, dgaff_fox, d_proj,
                                                        [dw_mem_kv, dw_gla_o, dw_fox_o, dw_mem_o, dw_out])
    dw_in = _slabs(_restore_w_in_grad(_wgrad(u, d_proj, "wgrad_in")), 1)
    dx, dg_mix, arrived_in = _nt_rmsbwd(d_proj, wp, x, p["g_mix"], dh1, "dmix", False, [dw_in])

    big = dict(w_in=arrived_in[0], w_ff1=arrived_ff1[0], w_ff2=arrived_ff2[0],
               **dict(zip(("w_mem_kv", "w_gla_o", "w_fox_o", "w_mem_o", "w_out"), arrived)))
    small = dict(g_mix=dg_mix, g_mem=dg_mem, g_ffn=dg_ffn, g_final=dg_final, b_alpha=db_alpha, g_gla_head=dg_gla,
                 b_forget=db_forget, w_alpha_up=dw_au, loss=loss8)
    return dx, big, small


BIG = (("w_in", 1), ("w_mem_kv", 0), ("w_gla_o", 1), ("w_fox_o", 1), ("w_mem_o", 1), ("w_out", 0), ("w_ff1", 1), ("w_ff2", 0))


def _peer(d):
    me = lax.axis_index("x") * 4 + lax.axis_index("y") * 2 + lax.axis_index("c")
    t = (me + d) % N_DEV
    return (t // 4, (t // 2) % 2, t % 2), me


def _exchange_sems(n):
    return [pltpu.SemaphoreType.DMA((n, N_DEV - 1)), pltpu.SemaphoreType.DMA((n, N_DEV - 1)), pltpu.SemaphoreType.DMA((n,))]


def _exchange_call(body, blocks, out_shape, name):
    n = len(blocks)
    any_spec = pl.BlockSpec(memory_space=pl.ANY)
    return pl.pallas_call(body, name=name, in_specs=[any_spec] * n, out_specs=[any_spec] * n, out_shape=out_shape,
                          scratch_shapes=_exchange_sems(n))(*blocks)


class _AllToAll:
    def __init__(self, ins, outs, sems, gather):
        send, recv, loc = sems
        n = len(ins)
        _, me = _peer(0)
        src = (lambda k, j: ins[k]) if gather else (lambda k, j: ins[k].at[j])
        self.local = [pltpu.make_async_copy(src(k, me), outs[k].at[me], loc.at[k]) for k in range(n)]
        self.remote = []
        for d in range(1, N_DEV):
            to, _ = _peer(d)
            self.remote += [pltpu.make_async_remote_copy(
                src_ref=src(k, (me + d) % N_DEV), dst_ref=outs[k].at[me], send_sem=send.at[k, d - 1],
                recv_sem=recv.at[k, d - 1], device_id=to, device_id_type=MESH) for k in range(n)]

    def start(self):
        for cp in self.local + self.remote:
            cp.start()

    def wait(self):
        for cp in self.remote:
            cp.wait_send()
        for cp in self.remote:
            cp.wait_recv()
        for cp in self.local:
            cp.wait()


def _gathered_shapes(shards):
    return [jax.ShapeDtypeStruct((N_DEV,) + b.shape, b.dtype) for b in shards]


def _gather_weights(shards):
    n = len(shards)

    def body(*refs):
        ins, outs = refs[:n], refs[n:2 * n]
        send, recv, loc = refs[2 * n:]
        x, y, c = lax.axis_index("x"), lax.axis_index("y"), lax.axis_index("c")
        sibling = (x, y, 1 - c)
        chips = [(1 - x, y), (x, 1 - y), (1 - x, 1 - y)]
        slot = lambda px, py, pc: px * 4 + py * 2 + pc

        def copy(k, s, block, to, src=None):
            rows = outs[k].at[slot(*block)]
            return pltpu.make_async_remote_copy(src_ref=rows if src is None else src, dst_ref=rows, send_sem=send.at[k, s],
                                                recv_sem=recv.at[k, s], device_id=to, device_id_type=MESH)

        me = (x, y, c)
        own = [pltpu.make_async_copy(ins[k], outs[k].at[slot(*me)], loc.at[k]) for k in range(n)]
        first = [copy(k, 0, me, sibling, src=ins[k]) for k in range(n)]
        first += [copy(k, 1 + j, me, (*chip, c), src=ins[k]) for j, chip in enumerate(chips) for k in range(n)]
        for cp in own + first:
            cp.start()
        passed = []
        for j, chip in enumerate(chips):
            for k in range(n):
                copy(k, 1 + j, (*chip, c), me).wait_recv()
                fwd = copy(k, 4 + j, (*chip, c), sibling)
                fwd.start()
                passed.append(fwd)
        for k in range(n):
            copy(k, 0, sibling, me).wait_recv()
        for j, chip in enumerate(chips):
            for k in range(n):
                copy(k, 4 + j, (*chip, 1 - c), me).wait_recv()
        for cp in first + passed:
            cp.wait_send()
        for cp in own:
            cp.wait()

    return _exchange_call(body, shards, [jax.ShapeDtypeStruct((N_DEV,) + b.shape, b.dtype) for b in shards], "gather_weights")


def _adamw_math(g, w, m, v):
    m2 = ADAM_B1 * m + (1.0 - ADAM_B1) * g
    v2 = ADAM_B2 * v + (1.0 - ADAM_B2) * jnp.square(g)
    m_hat = m2 / (1.0 - ADAM_B1 ** ADAM_STEP)
    v_hat = v2 / (1.0 - ADAM_B2 ** ADAM_STEP)
    delta = -ADAM_LR * (m_hat / (jnp.sqrt(v_hat) + ADAM_EPS) + ADAM_WD * w)
    return delta, m2, v2


def _adamw_sum(parts, w, m, v, name):
    R, C = w.shape
    tr = _tile(R, 128)

    def body(p_ref, w_ref, m_ref, v_ref, g_ref, d_ref, m2_ref, v2_ref):
        g = p_ref[0].astype(F32)
        for j in range(1, p_ref.shape[0]):
            g = g + p_ref[j].astype(F32)
        g_ref[...] = g
        d_ref[...], m2_ref[...], v2_ref[...] = _adamw_math(g, w_ref[...], m_ref[...], v_ref[...])

    blk = pl.BlockSpec((tr, C), lambda i: (i, 0))
    return pl.pallas_call(
        body, name=name, grid=(R // tr,),
        in_specs=[pl.BlockSpec((parts.shape[0], tr, C), lambda i: (0, i, 0)), blk, blk, blk],
        out_specs=[blk] * 4, out_shape=[jax.ShapeDtypeStruct((R, C), F32)] * 4,
        compiler_params=_params("arbitrary"),
    )(parts, w, m, v)


SMALL_ROWS = 24


def _pack_small(d):
    mixed = jnp.concatenate([d["b_alpha"].reshape(1, GLA_K), d["g_gla_head"].reshape(1, GLA_V),
                             jnp.pad(d["b_forget"].reshape(1, FOX_H), ((0, 0), (FF_LANE, 128 - FF_LANE - FOX_H))),
                             jnp.zeros((1, 128), F32)], axis=1)
    rows = [d["g_mix"].reshape(1, D), d["g_mem"].reshape(1, D), d["g_ffn"].reshape(1, D), d["g_final"].reshape(1, D), mixed,
            jnp.zeros((3, D), F32), jnp.pad(d["w_alpha_up"].reshape(GLA_R, GLA_K), ((0, 0), (0, D - GLA_K)))]
    return jnp.concatenate(rows, axis=0)


def _unpack_small(t):
    return dict(g_mix=t[0:1], g_mem=t[1:2], g_ffn=t[2:3], g_final=t[3], b_alpha=t[4:5, 0:GLA_K],
                g_gla_head=t[4:5, GLA_K:GLA_K + GLA_V].reshape(1, GLA_H, GLA_DV),
                b_forget=t[4:5, 768 + FF_LANE:768 + FF_LANE + FOX_H], w_alpha_up=t[8:24, 0:GLA_K].reshape(1, GLA_R, GLA_K))


def _small_allreduce(small, w, m, v):
    def body(gm, gme, gf, gfi, ba, gg, bf, wau, ls, w_ref, m_ref, v_ref, g_ref, d_ref, m2_ref, v2_ref, l_ref,
             buf, send, recv):
        _, me = _peer(0)
        buf[me] = jnp.zeros((SMALL_ROWS, D), F32)
        for r, ref in enumerate((gm, gme, gf, gfi)):
            buf[me, r:r + 1, :] = jnp.sum(ref[...], axis=0, keepdims=True)
        buf[me, 4:5, 0:GLA_K] = jnp.sum(ba[...], axis=0, keepdims=True)
        buf[me, 4:5, GLA_K:GLA_K + GLA_V] = jnp.sum(gg[...], axis=0, keepdims=True)
        buf[me, 4:5, 768:896] = jnp.sum(bf[...], axis=0, keepdims=True)
        lrow = jnp.sum(ls[...], axis=0, keepdims=True)
        lsum = lrow[:, 0:128]
        for c in range(1, D // 128):
            lsum = lsum + lrow[:, 128 * c:128 * (c + 1)]
        buf[me, 4:5, 896:1024] = lsum
        buf[me, 8:24, 0:GLA_K] = wau[0:GLA_R, :]
        remote = []
        for d in range(1, N_DEV):
            to, me = _peer(d)
            cp = pltpu.make_async_remote_copy(src_ref=buf.at[me], dst_ref=buf.at[me], send_sem=send.at[d - 1],
                                              recv_sem=recv.at[d - 1], device_id=to, device_id_type=MESH)
            cp.start()
            remote.append(cp)
        for cp in remote:
            cp.wait_send()
        for cp in remote:
            cp.wait_recv()
        g = buf[0]
        for j in range(1, N_DEV):
            g = g + buf[j]
        g_ref[...] = g
        d_ref[...], m2_ref[...], v2_ref[...] = _adamw_math(g, w_ref[...], m_ref[...], v_ref[...])
        l_ref[...] = g[4:5, 896:1024]

    packed = jax.ShapeDtypeStruct((SMALL_ROWS, D), F32)
    return pl.pallas_call(
        body, name="small_allreduce",
        out_shape=[packed, packed, packed, packed, jax.ShapeDtypeStruct((1, 128), F32)],
        scratch_shapes=[pltpu.VMEM((N_DEV, SMALL_ROWS, D), F32), pltpu.SemaphoreType.DMA((N_DEV - 1,)),
                        pltpu.SemaphoreType.DMA((N_DEV - 1,))],
    )(small["g_mix"], small["g_mem"], small["g_ffn"], small["g_final"], small["b_alpha"], small["g_gla_head"],
      small["b_forget"], small["w_alpha_up"], small["loss"], w, m, v)


def _slabs(g, axis):
    R, C = g.shape
    if axis == 0:
        return g.reshape(N_DEV, R // N_DEV, C)
    return g.reshape(R, N_DEV, C // N_DEV).transpose(1, 0, 2)


def _unslab(t, axis):
    n, r, c = t.shape
    if axis == 0:
        return t.reshape(n * r, c)
    return t.transpose(1, 0, 2).reshape(r, n * c)


def kernel(x, mem, g_mix, w_in, w_alpha_up, b_alpha, b_forget, g_gla_head, g_mem, w_mem_kv, w_gla_o, w_fox_o, w_mem_o, w_out, g_ffn, w_ff1, w_ff2, g_final, loss_target, m_g_mix, m_w_in, m_w_alpha_up, m_b_alpha, m_b_forget, m_g_gla_head, m_g_mem, m_w_mem_kv, m_w_gla_o, m_w_fox_o, m_w_mem_o, m_w_out, m_g_ffn, m_w_ff1, m_w_ff2, m_g_final, v_g_mix, v_w_in, v_w_alpha_up, v_b_alpha, v_b_forget, v_g_gla_head, v_g_mem, v_w_mem_kv, v_w_gla_o, v_w_fox_o, v_w_mem_o, v_w_out, v_g_ffn, v_w_ff1, v_w_ff2, v_g_final):
    names = ["g_mix", "w_in", "w_alpha_up", "b_alpha", "b_forget", "g_gla_head", "g_mem", "w_mem_kv", "w_gla_o", "w_fox_o",
             "w_mem_o", "w_out", "g_ffn", "w_ff1", "w_ff2", "g_final"]
    w = dict(g_mix=g_mix, w_in=w_in, w_alpha_up=w_alpha_up, b_alpha=b_alpha, b_forget=b_forget, g_gla_head=g_gla_head,
             g_mem=g_mem, w_mem_kv=w_mem_kv, w_gla_o=w_gla_o, w_fox_o=w_fox_o, w_mem_o=w_mem_o, w_out=w_out, g_ffn=g_ffn,
             w_ff1=w_ff1, w_ff2=w_ff2, g_final=g_final)
    m = dict(g_mix=m_g_mix, w_in=m_w_in, w_alpha_up=m_w_alpha_up, b_alpha=m_b_alpha, b_forget=m_b_forget,
             g_gla_head=m_g_gla_head, g_mem=m_g_mem, w_mem_kv=m_w_mem_kv, w_gla_o=m_w_gla_o, w_fox_o=m_w_fox_o,
             w_mem_o=m_w_mem_o, w_out=m_w_out, g_ffn=m_g_ffn, w_ff1=m_w_ff1, w_ff2=m_w_ff2, g_final=m_g_final)
    v = dict(g_mix=v_g_mix, w_in=v_w_in, w_alpha_up=v_w_alpha_up, b_alpha=v_b_alpha, b_forget=v_b_forget,
             g_gla_head=v_g_gla_head, g_mem=v_g_mem, w_mem_kv=v_w_mem_kv, w_gla_o=v_w_gla_o, w_fox_o=v_w_fox_o,
             w_mem_o=v_w_mem_o, w_out=v_w_out, g_ffn=v_g_ffn, w_ff1=v_w_ff1, w_ff2=v_w_ff2, g_final=v_g_final)
    me = lax.axis_index("x") * 4 + lax.axis_index("y") * 2 + lax.axis_index("c")

    shard = lambda n: w[n][0].astype(BF16)
    w_in_all, w_au_all = _gather_weights([shard("w_in"), shard("w_alpha_up")])
    p = dict(w_in=_unslab(w_in_all, 1), w_alpha_up=_unslab(w_au_all, 1), g_mix=g_mix, b_alpha=b_alpha, b_forget=b_forget,
             g_gla_head=g_gla_head, g_mem=g_mem, g_ffn=g_ffn, g_final=g_final)

    dx, big, small = _local_step(x[0], mem[0], loss_target[0], p, [shard(n) for n, _ in BIG[1:]])

    out_g, out_d, out_m, out_v = {}, {}, {}, {}
    for n, _ in BIG:
        g_, d_, m_, v_ = _adamw_sum(big[n], w[n][0], m[n][0], v[n][0], "adamw_" + n)
        out_g[n], out_d[n], out_m[n], out_v[n] = g_[None], d_[None], m_[None], v_[None]

    full = lambda d: dict(d, w_alpha_up=jnp.zeros((1, GLA_R, GLA_K), F32))
    gs, ds, ms, vs, lrow = _small_allreduce(small, _pack_small(full(w)), _pack_small(full(m)), _pack_small(full(v)))
    g_s, d_s, m_s, v_s = _unpack_small(gs), _unpack_small(ds), _unpack_small(ms), _unpack_small(vs)
    for n in names:
        if n not in out_g and n != "w_alpha_up":
            out_g[n], out_d[n], out_m[n], out_v[n] = g_s[n], d_s[n], m_s[n], v_s[n]
    g_au = lax.dynamic_slice_in_dim(g_s["w_alpha_up"][0], me * (GLA_K // N_DEV), GLA_K // N_DEV, axis=1)
    g_, d_, m_, v_ = _adamw_sum(g_au[None], w_alpha_up[0], m_w_alpha_up[0], v_w_alpha_up[0], "adamw_w_alpha_up")
    out_g["w_alpha_up"], out_d["w_alpha_up"], out_m["w_alpha_up"], out_v["w_alpha_up"] = g_[None], d_[None], m_[None], v_[None]

    loss = jnp.sum(lrow) * (0.5 / D)
    return (loss, dx[None], *[out_g[n] for n in names], *[out_d[n] for n in names], *[out_m[n] for n in names],
            *[out_v[n] for n in names])
```
